```python
import jax, jax.numpy as jnp
from jax import lax
import numpy as np

D_MODEL = 1024
BATCH = 8
SEQ = 4096
DEPTH = 4

CHUNK = 64
N_MEM = 256
N_BRANCH = 4
GROUPS = 4
GROUP_W = D_MODEL // 8
MIX_W = GROUPS * GROUP_W
SHORT_K = 3
POOL_WINDOWS = (2, 4, 8, 16)
SGU_BLOCK = 128
CONF_K = 31
D_FF = 2816
XA_HEADS = 4
XA_HEAD_DIM = D_MODEL // XA_HEADS
EPS = 1e-6
IN_COLS = 3 * MIX_W + MIX_W + 2 * MIX_W + 2 * MIX_W
IN_SPLITS = (MIX_W, 2 * MIX_W, 3 * MIX_W, 4 * MIX_W, 6 * MIX_W)

kernel_name = "hybrid_gated_conv_pool_sgu_conformer_block"


def rmsnorm(x, g):
    xf = x.astype(jnp.float32)
    y = xf * lax.rsqrt(jnp.mean(xf * xf, axis=-1, keepdims=True) + EPS)
    return (y * g.astype(jnp.float32)).astype(x.dtype)


def layernorm(x, g, b):
    xf = x.astype(jnp.float32)
    mu = jnp.mean(xf, axis=-1, keepdims=True)
    var = jnp.mean(jnp.square(xf - mu), axis=-1, keepdims=True)
    y = (xf - mu) * lax.rsqrt(var + EPS)
    return (y * g.astype(jnp.float32) + b.astype(jnp.float32)).astype(x.dtype)


def causal_depthwise_conv(x, w, b):
    k, c = w.shape
    y = lax.conv_general_dilated(x, w[:, None, :].astype(x.dtype), window_strides=(1,),
                                 padding=[(k - 1, 0)],
                                 dimension_numbers=('NWC', 'WIO', 'NWC'),
                                 feature_group_count=c)
    return y + b


def swiglu(h, w1, w3, w2):
    return (jax.nn.silu(h @ w1) * (h @ w3)) @ w2


def short_conv_mixer(xa, ba, ca, w, b):
    return ba * causal_depthwise_conv(ca * xa, w, b)


def pool_mixer(p, w_grp, scale):
    bsz, s, _ = p.shape
    pg = p.reshape(bsz, s, GROUPS, GROUP_W)
    pos = jnp.arange(1, s + 1, dtype=jnp.float32)
    outs = []
    for gi, win in enumerate(POOL_WINDOWS):
        xg = pg[:, :, gi].astype(jnp.float32)
        cs = jnp.cumsum(xg, axis=1)
        lag = jnp.pad(cs, ((0, 0), (win, 0), (0, 0)))[:, :s]
        mean = (cs - lag) / jnp.minimum(pos, float(win))[None, :, None]
        outs.append(mean - xg)
    pooled = jnp.stack(outs, axis=2).astype(p.dtype)
    mixed = jnp.einsum('bsgc,gcd->bsgd', pooled, w_grp)
    return mixed.reshape(bsz, s, MIX_W) * scale


def sgu_mixer(gc, ln_g, ln_b, ws, bs):
    bsz, s, _ = gc.shape
    gc = jax.nn.gelu(gc)
    u, v = jnp.split(gc, 2, axis=-1)
    v = layernorm(v, ln_g, ln_b)
    cidx = jnp.arange(SGU_BLOCK) // CHUNK
    mask = cidx[None, :] <= cidx[:, None]
    wm = jnp.where(mask[None], ws, jnp.zeros_like(ws))
    vb = v.reshape(bsz, s // SGU_BLOCK, SGU_BLOCK, GROUPS, GROUP_W)
    mixed = jnp.einsum('gij,bnjgc->bnigc', wm, vb) + bs.T[None, None, :, :, None]
    return u * mixed.reshape(bsz, s, MIX_W)


def conformer_conv_mixer(d, w, b, ln_g, ln_b):
    a, gate = jnp.split(d, 2, axis=-1)
    y = causal_depthwise_conv(a * jax.nn.sigmoid(gate), w, b)
    return jax.nn.silu(layernorm(y, ln_g, ln_b))


def memory_cross_attention(h, mem_n, wq, wk, wv, wo):
    bsz, s, _ = h.shape
    q = (h @ wq).reshape(bsz, s, XA_HEADS, XA_HEAD_DIM)
    k = (mem_n @ wk).reshape(bsz, -1, XA_HEADS, XA_HEAD_DIM)
    v = (mem_n @ wv).reshape(bsz, -1, XA_HEADS, XA_HEAD_DIM)
    scores = jnp.einsum('bshd,bmhd->bhsm', q.astype(jnp.float32), k.astype(jnp.float32))
    probs = jax.nn.softmax(scores * (XA_HEAD_DIM ** -0.5), axis=-1).astype(v.dtype)
    o = jnp.einsum('bhsm,bmhd->bshd', probs, v).reshape(bsz, s, D_MODEL)
    return o @ wo


def _fwd_setup_inputs(seed: int = 0) -> dict:
    key = jax.random.key(seed)
    ks = iter(jax.random.split(key, 64))
    L, D = DEPTH, D_MODEL

    def w(shape, fan_in):
        return jax.random.normal(next(ks), shape, jnp.float32) * (fan_in ** -0.5)

    def gain(shape):
        return 1.0 + 0.05 * jax.random.normal(next(ks), shape, jnp.float32)

    def bias(shape, s=0.02):
        return s * jax.random.normal(next(ks), shape, jnp.float32)

    return {
        'x': jax.random.normal(next(ks), (BATCH, SEQ, D), jnp.float32),
        'mem': jax.random.normal(next(ks), (BATCH, N_MEM, D), jnp.float32),
        'ffn1_pre_g': gain((L, D)),
        'ffn1_post_g': gain((L, D)),
        'ffn1_w1': w((L, D, D_FF), D),
        'ffn1_w3': w((L, D, D_FF), D),
        'ffn1_w2': w((L, D_FF, D), D_FF),
        'mix_pre_g': gain((L, D)),
        'mix_post_g': gain((L, D)),
        'w_in': w((L, D, IN_COLS), D),
        'conv_a_w': w((L, SHORT_K, MIX_W), SHORT_K),
        'conv_a_b': bias((L, MIX_W)),
        'pool_w': w((L, GROUPS, GROUP_W, GROUP_W), GROUP_W),
        'pool_scale': gain((L, MIX_W)),
        'sgu_ln_g': gain((L, MIX_W)),
        'sgu_ln_b': bias((L, MIX_W)),
        'sgu_ws': w((L, GROUPS, SGU_BLOCK, SGU_BLOCK), SGU_BLOCK),
        'sgu_b': 1.0 + bias((L, GROUPS, SGU_BLOCK), 0.05),
        'conv_d_w': w((L, CONF_K, MIX_W), CONF_K),
        'conv_d_b': bias((L, MIX_W)),
        'conv_d_ln_g': gain((L, MIX_W)),
        'conv_d_ln_b': bias((L, MIX_W)),
        'w_branch': w((L, N_BRANCH, MIX_W, D), MIX_W),
        'w_gate': w((L, D, N_BRANCH * D), D),
        'b_gate': bias((L, N_BRANCH * D)),
        'w_o': w((L, D, D), D),
        'xa_pre_g': gain((L, D)),
        'xa_post_g': gain((L, D)),
        'mem_g': gain((L, D)),
        'xa_wq': w((L, D, D), D),
        'xa_wk': w((L, D, D), D),
        'xa_wv': w((L, D, D), D),
        'xa_wo': w((L, D, D), D),
        'ffn2_pre_g': gain((L, D)),
        'ffn2_post_g': gain((L, D)),
        'ffn2_w1': w((L, D, D_FF), D),
        'ffn2_w3': w((L, D, D_FF), D),
        'ffn2_w2': w((L, D_FF, D), D_FF),
    }


def _fwd_reference(x, mem, ffn1_pre_g, ffn1_post_g, ffn1_w1, ffn1_w3, ffn1_w2,
              mix_pre_g, mix_post_g, w_in, conv_a_w, conv_a_b, pool_w, pool_scale,
              sgu_ln_g, sgu_ln_b, sgu_ws, sgu_b, conv_d_w, conv_d_b, conv_d_ln_g, conv_d_ln_b,
              w_branch, w_gate, b_gate, w_o,
              xa_pre_g, xa_post_g, mem_g, xa_wq, xa_wk, xa_wv, xa_wo,
              ffn2_pre_g, ffn2_post_g, ffn2_w1, ffn2_w3, ffn2_w2):
    bsz, s, _ = x.shape
    for l in range(DEPTH):
        h = rmsnorm(x, ffn1_pre_g[l])
        x = x + 0.5 * rmsnorm(swiglu(h, ffn1_w1[l], ffn1_w3[l], ffn1_w2[l]), ffn1_post_g[l])

        h = rmsnorm(x, mix_pre_g[l])
        z = h @ w_in[l]
        xa, ba, ca, p, gc, d = jnp.split(z, IN_SPLITS, axis=-1)
        y_a = short_conv_mixer(xa, ba, ca, conv_a_w[l], conv_a_b[l]) @ w_branch[l, 0]
        y_b = pool_mixer(p, pool_w[l], pool_scale[l]) @ w_branch[l, 1]
        y_c = sgu_mixer(gc, sgu_ln_g[l], sgu_ln_b[l], sgu_ws[l], sgu_b[l]) @ w_branch[l, 2]
        y_d = conformer_conv_mixer(d, conv_d_w[l], conv_d_b[l], conv_d_ln_g[l],
                                   conv_d_ln_b[l]) @ w_branch[l, 3]
        gates = jax.nn.sigmoid(h @ w_gate[l] + b_gate[l]).reshape(bsz, s, N_BRANCH, D_MODEL)
        merged = (gates[:, :, 0] * y_a + gates[:, :, 1] * y_b
                  + gates[:, :, 2] * y_c + gates[:, :, 3] * y_d)
        x = x + rmsnorm(merged @ w_o[l], mix_post_g[l])

        h = rmsnorm(x, xa_pre_g[l])
        mem_n = rmsnorm(mem, mem_g[l])
        xa_out = memory_cross_attention(h, mem_n, xa_wq[l], xa_wk[l], xa_wv[l], xa_wo[l])
        x = x + rmsnorm(xa_out, xa_post_g[l])

        h = rmsnorm(x, ffn2_pre_g[l])
        x = x + 0.5 * rmsnorm(swiglu(h, ffn2_w1[l], ffn2_w3[l], ffn2_w2[l]), ffn2_post_g[l])
    return x


import jax as _jax
import jax.numpy as _jnp

TWIN_FORMAT = 'train_step'
FWD_PARAMS = ['x', 'mem', 'ffn1_pre_g', 'ffn1_post_g', 'ffn1_w1', 'ffn1_w3', 'ffn1_w2', 'mix_pre_g', 'mix_post_g', 'w_in', 'conv_a_w', 'conv_a_b', 'pool_w', 'pool_scale', 'sgu_ln_g', 'sgu_ln_b', 'sgu_ws', 'sgu_b', 'conv_d_w', 'conv_d_b', 'conv_d_ln_g', 'conv_d_ln_b', 'w_branch', 'w_gate', 'b_gate', 'w_o', 'xa_pre_g', 'xa_post_g', 'mem_g', 'xa_wq', 'xa_wk', 'xa_wv', 'xa_wo', 'ffn2_pre_g', 'ffn2_post_g', 'ffn2_w1', 'ffn2_w3', 'ffn2_w2']
TWIN_WEIGHTS = ['ffn1_pre_g', 'ffn1_post_g', 'ffn1_w1', 'ffn1_w3', 'ffn1_w2', 'mix_pre_g', 'mix_post_g', 'w_in', 'conv_a_w', 'conv_a_b', 'pool_w', 'pool_scale', 'sgu_ln_g', 'sgu_ln_b', 'sgu_ws', 'sgu_b', 'conv_d_w', 'conv_d_b', 'conv_d_ln_g', 'conv_d_ln_b', 'w_branch', 'w_gate', 'b_gate', 'w_o', 'xa_pre_g', 'xa_post_g', 'mem_g', 'xa_wq', 'xa_wk', 'xa_wv', 'xa_wo', 'ffn2_pre_g', 'ffn2_post_g', 'ffn2_w1', 'ffn2_w3', 'ffn2_w2']
TWIN_DIFF_INPUT = 'x'
TWIN_INPUTS = ['x', 'mem', 'ffn1_pre_g', 'ffn1_post_g', 'ffn1_w1', 'ffn1_w3', 'ffn1_w2', 'mix_pre_g', 'mix_post_g', 'w_in', 'conv_a_w', 'conv_a_b', 'pool_w', 'pool_scale', 'sgu_ln_g', 'sgu_ln_b', 'sgu_ws', 'sgu_b', 'conv_d_w', 'conv_d_b', 'conv_d_ln_g', 'conv_d_ln_b', 'w_branch', 'w_gate', 'b_gate', 'w_o', 'xa_pre_g', 'xa_post_g', 'mem_g', 'xa_wq', 'xa_wk', 'xa_wv', 'xa_wo', 'ffn2_pre_g', 'ffn2_post_g', 'ffn2_w1', 'ffn2_w3', 'ffn2_w2', 'loss_target', 'm_ffn1_pre_g', 'm_ffn1_post_g', 'm_ffn1_w1', 'm_ffn1_w3', 'm_ffn1_w2', 'm_mix_pre_g', 'm_mix_post_g', 'm_w_in', 'm_conv_a_w', 'm_conv_a_b', 'm_pool_w', 'm_pool_scale', 'm_sgu_ln_g', 'm_sgu_ln_b', 'm_sgu_ws', 'm_sgu_b', 'm_conv_d_w', 'm_conv_d_b', 'm_conv_d_ln_g', 'm_conv_d_ln_b', 'm_w_branch', 'm_w_gate', 'm_b_gate', 'm_w_o', 'm_xa_pre_g', 'm_xa_post_g', 'm_mem_g', 'm_xa_wq', 'm_xa_wk', 'm_xa_wv', 'm_xa_wo', 'm_ffn2_pre_g', 'm_ffn2_post_g', 'm_ffn2_w1', 'm_ffn2_w3', 'm_ffn2_w2', 'v_ffn1_pre_g', 'v_ffn1_post_g', 'v_ffn1_w1', 'v_ffn1_w3', 'v_ffn1_w2', 'v_mix_pre_g', 'v_mix_post_g', 'v_w_in', 'v_conv_a_w', 'v_conv_a_b', 'v_pool_w', 'v_pool_scale', 'v_sgu_ln_g', 'v_sgu_ln_b', 'v_sgu_ws', 'v_sgu_b', 'v_conv_d_w', 'v_conv_d_b', 'v_conv_d_ln_g', 'v_conv_d_ln_b', 'v_w_branch', 'v_w_gate', 'v_b_gate', 'v_w_o', 'v_xa_pre_g', 'v_xa_post_g', 'v_mem_g', 'v_xa_wq', 'v_xa_wk', 'v_xa_wv', 'v_xa_wo', 'v_ffn2_pre_g', 'v_ffn2_post_g', 'v_ffn2_w1', 'v_ffn2_w3', 'v_ffn2_w2']
TWIN_OUTPUTS = ['loss', 'grad_x', 'grad_ffn1_pre_g', 'grad_ffn1_post_g', 'grad_ffn1_w1', 'grad_ffn1_w3', 'grad_ffn1_w2', 'grad_mix_pre_g', 'grad_mix_post_g', 'grad_w_in', 'grad_conv_a_w', 'grad_conv_a_b', 'grad_pool_w', 'grad_pool_scale', 'grad_sgu_ln_g', 'grad_sgu_ln_b', 'grad_sgu_ws', 'grad_sgu_b', 'grad_conv_d_w', 'grad_conv_d_b', 'grad_conv_d_ln_g', 'grad_conv_d_ln_b', 'grad_w_branch', 'grad_w_gate', 'grad_b_gate', 'grad_w_o', 'grad_xa_pre_g', 'grad_xa_post_g', 'grad_mem_g', 'grad_xa_wq', 'grad_xa_wk', 'grad_xa_wv', 'grad_xa_wo', 'grad_ffn2_pre_g', 'grad_ffn2_post_g', 'grad_ffn2_w1', 'grad_ffn2_w3', 'grad_ffn2_w2', 'delta_ffn1_pre_g', 'delta_ffn1_post_g', 'delta_ffn1_w1', 'delta_ffn1_w3', 'delta_ffn1_w2', 'delta_mix_pre_g', 'delta_mix_post_g', 'delta_w_in', 'delta_conv_a_w', 'delta_conv_a_b', 'delta_pool_w', 'delta_pool_scale', 'delta_sgu_ln_g', 'delta_sgu_ln_b', 'delta_sgu_ws', 'delta_sgu_b', 'delta_conv_d_w', 'delta_conv_d_b', 'delta_conv_d_ln_g', 'delta_conv_d_ln_b', 'delta_w_branch', 'delta_w_gate', 'delta_b_gate', 'delta_w_o', 'delta_xa_pre_g', 'delta_xa_post_g', 'delta_mem_g', 'delta_xa_wq', 'delta_xa_wk', 'delta_xa_wv', 'delta_xa_wo', 'delta_ffn2_pre_g', 'delta_ffn2_post_g', 'delta_ffn2_w1', 'delta_ffn2_w3', 'delta_ffn2_w2', 'new_m_ffn1_pre_g', 'new_m_ffn1_post_g', 'new_m_ffn1_w1', 'new_m_ffn1_w3', 'new_m_ffn1_w2', 'new_m_mix_pre_g', 'new_m_mix_post_g', 'new_m_w_in', 'new_m_conv_a_w', 'new_m_conv_a_b', 'new_m_pool_w', 'new_m_pool_scale', 'new_m_sgu_ln_g', 'new_m_sgu_ln_b', 'new_m_sgu_ws', 'new_m_sgu_b', 'new_m_conv_d_w', 'new_m_conv_d_b', 'new_m_conv_d_ln_g', 'new_m_conv_d_ln_b', 'new_m_w_branch', 'new_m_w_gate', 'new_m_b_gate', 'new_m_w_o', 'new_m_xa_pre_g', 'new_m_xa_post_g', 'new_m_mem_g', 'new_m_xa_wq', 'new_m_xa_wk', 'new_m_xa_wv', 'new_m_xa_wo', 'new_m_ffn2_pre_g', 'new_m_ffn2_post_g', 'new_m_ffn2_w1', 'new_m_ffn2_w3', 'new_m_ffn2_w2', 'new_v_ffn1_pre_g', 'new_v_ffn1_post_g', 'new_v_ffn1_w1', 'new_v_ffn1_w3', 'new_v_ffn1_w2', 'new_v_mix_pre_g', 'new_v_mix_post_g', 'new_v_w_in', 'new_v_conv_a_w', 'new_v_conv_a_b', 'new_v_pool_w', 'new_v_pool_scale', 'new_v_sgu_ln_g', 'new_v_sgu_ln_b', 'new_v_sgu_ws', 'new_v_sgu_b', 'new_v_conv_d_w', 'new_v_conv_d_b', 'new_v_conv_d_ln_g', 'new_v_conv_d_ln_b', 'new_v_w_branch', 'new_v_w_gate', 'new_v_b_gate', 'new_v_w_o', 'new_v_xa_pre_g', 'new_v_xa_post_g', 'new_v_mem_g', 'new_v_xa_wq', 'new_v_xa_wk', 'new_v_xa_wv', 'new_v_xa_wo', 'new_v_ffn2_pre_g', 'new_v_ffn2_post_g', 'new_v_ffn2_w1', 'new_v_ffn2_w3', 'new_v_ffn2_w2']
TWIN_LEAF_KINDS = {'loss': 'loss', 'grad_x': 'grad_x', 'grad_ffn1_pre_g': 'grad_w', 'grad_ffn1_post_g': 'grad_w', 'grad_ffn1_w1': 'grad_w', 'grad_ffn1_w3': 'grad_w', 'grad_ffn1_w2': 'grad_w', 'grad_mix_pre_g': 'grad_w', 'grad_mix_post_g': 'grad_w', 'grad_w_in': 'grad_w', 'grad_conv_a_w': 'grad_w', 'grad_conv_a_b': 'grad_w', 'grad_pool_w': 'grad_w', 'grad_pool_scale': 'grad_w', 'grad_sgu_ln_g': 'grad_w', 'grad_sgu_ln_b': 'grad_w', 'grad_sgu_ws': 'grad_w', 'grad_sgu_b': 'grad_w', 'grad_conv_d_w': 'grad_w', 'grad_conv_d_b': 'grad_w', 'grad_conv_d_ln_g': 'grad_w', 'grad_conv_d_ln_b': 'grad_w', 'grad_w_branch': 'grad_w', 'grad_w_gate': 'grad_w', 'grad_b_gate': 'grad_w', 'grad_w_o': 'grad_w', 'grad_xa_pre_g': 'grad_w', 'grad_xa_post_g': 'grad_w', 'grad_mem_g': 'grad_w', 'grad_xa_wq': 'grad_w', 'grad_xa_wk': 'grad_w', 'grad_xa_wv': 'grad_w', 'grad_xa_wo': 'grad_w', 'grad_ffn2_pre_g': 'grad_w', 'grad_ffn2_post_g': 'grad_w', 'grad_ffn2_w1': 'grad_w', 'grad_ffn2_w3': 'grad_w', 'grad_ffn2_w2': 'grad_w', 'delta_ffn1_pre_g': 'delta_w', 'delta_ffn1_post_g': 'delta_w', 'delta_ffn1_w1': 'delta_w', 'delta_ffn1_w3': 'delta_w', 'delta_ffn1_w2': 'delta_w', 'delta_mix_pre_g': 'delta_w', 'delta_mix_post_g': 'delta_w', 'delta_w_in': 'delta_w', 'delta_conv_a_w': 'delta_w', 'delta_conv_a_b': 'delta_w', 'delta_pool_w': 'delta_w', 'delta_pool_scale': 'delta_w', 'delta_sgu_ln_g': 'delta_w', 'delta_sgu_ln_b': 'delta_w', 'delta_sgu_ws': 'delta_w', 'delta_sgu_b': 'delta_w', 'delta_conv_d_w': 'delta_w', 'delta_conv_d_b': 'delta_w', 'delta_conv_d_ln_g': 'delta_w', 'delta_conv_d_ln_b': 'delta_w', 'delta_w_branch': 'delta_w', 'delta_w_gate': 'delta_w', 'delta_b_gate': 'delta_w', 'delta_w_o': 'delta_w', 'delta_xa_pre_g': 'delta_w', 'delta_xa_post_g': 'delta_w', 'delta_mem_g': 'delta_w', 'delta_xa_wq': 'delta_w', 'delta_xa_wk': 'delta_w', 'delta_xa_wv': 'delta_w', 'delta_xa_wo': 'delta_w', 'delta_ffn2_pre_g': 'delta_w', 'delta_ffn2_post_g': 'delta_w', 'delta_ffn2_w1': 'delta_w', 'delta_ffn2_w3': 'delta_w', 'delta_ffn2_w2': 'delta_w', 'new_m_ffn1_pre_g': 'new_m', 'new_m_ffn1_post_g': 'new_m', 'new_m_ffn1_w1': 'new_m', 'new_m_ffn1_w3': 'new_m', 'new_m_ffn1_w2': 'new_m', 'new_m_mix_pre_g': 'new_m', 'new_m_mix_post_g': 'new_m', 'new_m_w_in': 'new_m', 'new_m_conv_a_w': 'new_m', 'new_m_conv_a_b': 'new_m', 'new_m_pool_w': 'new_m', 'new_m_pool_scale': 'new_m', 'new_m_sgu_ln_g': 'new_m', 'new_m_sgu_ln_b': 'new_m', 'new_m_sgu_ws': 'new_m', 'new_m_sgu_b': 'new_m', 'new_m_conv_d_w': 'new_m', 'new_m_conv_d_b': 'new_m', 'new_m_conv_d_ln_g': 'new_m', 'new_m_conv_d_ln_b': 'new_m', 'new_m_w_branch': 'new_m', 'new_m_w_gate': 'new_m', 'new_m_b_gate': 'new_m', 'new_m_w_o': 'new_m', 'new_m_xa_pre_g': 'new_m', 'new_m_xa_post_g': 'new_m', 'new_m_mem_g': 'new_m', 'new_m_xa_wq': 'new_m', 'new_m_xa_wk': 'new_m', 'new_m_xa_wv': 'new_m', 'new_m_xa_wo': 'new_m', 'new_m_ffn2_pre_g': 'new_m', 'new_m_ffn2_post_g': 'new_m', 'new_m_ffn2_w1': 'new_m', 'new_m_ffn2_w3': 'new_m', 'new_m_ffn2_w2': 'new_m', 'new_v_ffn1_pre_g': 'new_v', 'new_v_ffn1_post_g': 'new_v', 'new_v_ffn1_w1': 'new_v', 'new_v_ffn1_w3': 'new_v', 'new_v_ffn1_w2': 'new_v', 'new_v_mix_pre_g': 'new_v', 'new_v_mix_post_g': 'new_v', 'new_v_w_in': 'new_v', 'new_v_conv_a_w': 'new_v', 'new_v_conv_a_b': 'new_v', 'new_v_pool_w': 'new_v', 'new_v_pool_scale': 'new_v', 'new_v_sgu_ln_g': 'new_v', 'new_v_sgu_ln_b': 'new_v', 'new_v_sgu_ws': 'new_v', 'new_v_sgu_b': 'new_v', 'new_v_conv_d_w': 'new_v', 'new_v_conv_d_b': 'new_v', 'new_v_conv_d_ln_g': 'new_v', 'new_v_conv_d_ln_b': 'new_v', 'new_v_w_branch': 'new_v', 'new_v_w_gate': 'new_v', 'new_v_b_gate': 'new_v', 'new_v_w_o': 'new_v', 'new_v_xa_pre_g': 'new_v', 'new_v_xa_post_g': 'new_v', 'new_v_mem_g': 'new_v', 'new_v_xa_wq': 'new_v', 'new_v_xa_wk': 'new_v', 'new_v_xa_wv': 'new_v', 'new_v_xa_wo': 'new_v', 'new_v_ffn2_pre_g': 'new_v', 'new_v_ffn2_post_g': 'new_v', 'new_v_ffn2_w1': 'new_v', 'new_v_ffn2_w3': 'new_v', 'new_v_ffn2_w2': 'new_v'}


def _forward(args):
    return _fwd_reference(*[args[k] for k in FWD_PARAMS])


def _output_shape():
    def fwd():
        inp = _fwd_setup_inputs(0)
        return _fwd_reference(*[inp[k] for k in FWD_PARAMS])
    out = _jax.eval_shape(fwd)
    return out.shape, out.dtype

N_MICROBATCH = 1
ADAM_LR = 0.001
ADAM_B1 = 0.9
ADAM_B2 = 0.999
ADAM_EPS = 1e-08
ADAM_WD = 0.01
ADAM_STEP = 10
PER_EXAMPLE_BATCH_AXIS = {'x': 0, 'mem': 0, 'loss_target': 0}
SHARED_INPUTS = []
_WEIGHT_DTYPES = {'ffn1_pre_g': _jnp.float32, 'ffn1_post_g': _jnp.float32, 'ffn1_w1': _jnp.float32, 'ffn1_w3': _jnp.float32, 'ffn1_w2': _jnp.float32, 'mix_pre_g': _jnp.float32, 'mix_post_g': _jnp.float32, 'w_in': _jnp.float32, 'conv_a_w': _jnp.float32, 'conv_a_b': _jnp.float32, 'pool_w': _jnp.float32, 'pool_scale': _jnp.float32, 'sgu_ln_g': _jnp.float32, 'sgu_ln_b': _jnp.float32, 'sgu_ws': _jnp.float32, 'sgu_b': _jnp.float32, 'conv_d_w': _jnp.float32, 'conv_d_b': _jnp.float32, 'conv_d_ln_g': _jnp.float32, 'conv_d_ln_b': _jnp.float32, 'w_branch': _jnp.float32, 'w_gate': _jnp.float32, 'b_gate': _jnp.float32, 'w_o': _jnp.float32, 'xa_pre_g': _jnp.float32, 'xa_post_g': _jnp.float32, 'mem_g': _jnp.float32, 'xa_wq': _jnp.float32, 'xa_wk': _jnp.float32, 'xa_wv': _jnp.float32, 'xa_wo': _jnp.float32, 'ffn2_pre_g': _jnp.float32, 'ffn2_post_g': _jnp.float32, 'ffn2_w1': _jnp.float32, 'ffn2_w3': _jnp.float32, 'ffn2_w2': _jnp.float32}
MOMENT_SCALE = {'ffn1_pre_g': 1.760178e+00, 'ffn1_post_g': 7.777321e+00, 'ffn1_w1': 7.010013e-01, 'ffn1_w3': 8.415268e-01, 'ffn1_w2': 1.399476e+00, 'mix_pre_g': 2.807978e+00, 'mix_post_g': 3.226003e+01, 'w_in': 1.295167e+00, 'conv_a_w': 1.140817e+00, 'conv_a_b': 4.115293e+00, 'pool_w': 1.487422e+00, 'pool_scale': 1.551321e+00, 'sgu_ln_g': 5.707995e-01, 'sgu_ln_b': 6.520641e-01, 'sgu_ws': 5.771949e-01, 'sgu_b': 7.043243e-01, 'conv_d_w': 2.022496e+00, 'conv_d_b': 1.409636e+01, 'conv_d_ln_g': 5.805024e+00, 'conv_d_ln_b': 8.082290e+00, 'w_branch': 2.140585e+00, 'w_gate': 3.222226e-01, 'b_gate': 8.831968e-01, 'w_o': 4.378349e+00, 'xa_pre_g': 2.910791e+00, 'xa_post_g': 3.394686e+01, 'mem_g': 9.546745e+00, 'xa_wq': 2.954900e+00, 'xa_wk': 2.978046e+00, 'xa_wv': 9.084470e+00, 'xa_wo': 9.217795e+00, 'ffn2_pre_g': 1.889296e+00, 'ffn2_post_g': 8.044831e+00, 'ffn2_w1': 6.550318e-01, 'ffn2_w3': 9.329747e-01, 'ffn2_w2': 1.538108e+00}


def _to_microbatches(a, axis):
    t = _jnp.moveaxis(a, axis, 0)
    t = t.reshape((N_MICROBATCH, t.shape[0] // N_MICROBATCH) + t.shape[1:])
    return _jnp.moveaxis(t, 1, axis + 1)


def setup_inputs(seed: int = 0) -> dict:
    inp = _fwd_setup_inputs(seed)
    key = _jax.random.fold_in(_jax.random.key(seed), 7919)
    shape, _ = _output_shape()
    out = dict(inp)
    out["loss_target"] = _jax.random.normal(_jax.random.fold_in(key, 0), shape, _jnp.float32)
    for i, name in enumerate(TWIN_WEIGHTS):
        w = inp[name].astype(_jnp.float32)
        if MOMENT_SCALE is None:
            s = _jnp.sqrt(_jnp.mean(_jnp.square(w)) + 1e-30)
        else:
            s = MOMENT_SCALE[name]
        km, kv = _jax.random.split(_jax.random.fold_in(key, i + 1))
        out[name] = w
        out["m_" + name] = s * _jax.random.normal(km, w.shape, _jnp.float32)
        out["v_" + name] = (s * s) * _jax.random.uniform(kv, w.shape, _jnp.float32, 0.5, 1.5)
    if N_MICROBATCH > 1:
        for name, axis in PER_EXAMPLE_BATCH_AXIS.items():
            out[name] = _to_microbatches(out[name], axis)
    return {'x': out['x'], 'mem': out['mem'], 'ffn1_pre_g': out['ffn1_pre_g'], 'ffn1_post_g': out['ffn1_post_g'], 'ffn1_w1': out['ffn1_w1'], 'ffn1_w3': out['ffn1_w3'], 'ffn1_w2': out['ffn1_w2'], 'mix_pre_g': out['mix_pre_g'], 'mix_post_g': out['mix_post_g'], 'w_in': out['w_in'], 'conv_a_w': out['conv_a_w'], 'conv_a_b': out['conv_a_b'], 'pool_w': out['pool_w'], 'pool_scale': out['pool_scale'], 'sgu_ln_g': out['sgu_ln_g'], 'sgu_ln_b': out['sgu_ln_b'], 'sgu_ws': out['sgu_ws'], 'sgu_b': out['sgu_b'], 'conv_d_w': out['conv_d_w'], 'conv_d_b': out['conv_d_b'], 'conv_d_ln_g': out['conv_d_ln_g'], 'conv_d_ln_b': out['conv_d_ln_b'], 'w_branch': out['w_branch'], 'w_gate': out['w_gate'], 'b_gate': out['b_gate'], 'w_o': out['w_o'], 'xa_pre_g': out['xa_pre_g'], 'xa_post_g': out['xa_post_g'], 'mem_g': out['mem_g'], 'xa_wq': out['xa_wq'], 'xa_wk': out['xa_wk'], 'xa_wv': out['xa_wv'], 'xa_wo': out['xa_wo'], 'ffn2_pre_g': out['ffn2_pre_g'], 'ffn2_post_g': out['ffn2_post_g'], 'ffn2_w1': out['ffn2_w1'], 'ffn2_w3': out['ffn2_w3'], 'ffn2_w2': out['ffn2_w2'], 'loss_target': out['loss_target'], 'm_ffn1_pre_g': out['m_ffn1_pre_g'], 'm_ffn1_post_g': out['m_ffn1_post_g'], 'm_ffn1_w1': out['m_ffn1_w1'], 'm_ffn1_w3': out['m_ffn1_w3'], 'm_ffn1_w2': out['m_ffn1_w2'], 'm_mix_pre_g': out['m_mix_pre_g'], 'm_mix_post_g': out['m_mix_post_g'], 'm_w_in': out['m_w_in'], 'm_conv_a_w': out['m_conv_a_w'], 'm_conv_a_b': out['m_conv_a_b'], 'm_pool_w': out['m_pool_w'], 'm_pool_scale': out['m_pool_scale'], 'm_sgu_ln_g': out['m_sgu_ln_g'], 'm_sgu_ln_b': out['m_sgu_ln_b'], 'm_sgu_ws': out['m_sgu_ws'], 'm_sgu_b': out['m_sgu_b'], 'm_conv_d_w': out['m_conv_d_w'], 'm_conv_d_b': out['m_conv_d_b'], 'm_conv_d_ln_g': out['m_conv_d_ln_g'], 'm_conv_d_ln_b': out['m_conv_d_ln_b'], 'm_w_branch': out['m_w_branch'], 'm_w_gate': out['m_w_gate'], 'm_b_gate': out['m_b_gate'], 'm_w_o': out['m_w_o'], 'm_xa_pre_g': out['m_xa_pre_g'], 'm_xa_post_g': out['m_xa_post_g'], 'm_mem_g': out['m_mem_g'], 'm_xa_wq': out['m_xa_wq'], 'm_xa_wk': out['m_xa_wk'], 'm_xa_wv': out['m_xa_wv'], 'm_xa_wo': out['m_xa_wo'], 'm_ffn2_pre_g': out['m_ffn2_pre_g'], 'm_ffn2_post_g': out['m_ffn2_post_g'], 'm_ffn2_w1': out['m_ffn2_w1'], 'm_ffn2_w3': out['m_ffn2_w3'], 'm_ffn2_w2': out['m_ffn2_w2'], 'v_ffn1_pre_g': out['v_ffn1_pre_g'], 'v_ffn1_post_g': out['v_ffn1_post_g'], 'v_ffn1_w1': out['v_ffn1_w1'], 'v_ffn1_w3': out['v_ffn1_w3'], 'v_ffn1_w2': out['v_ffn1_w2'], 'v_mix_pre_g': out['v_mix_pre_g'], 'v_mix_post_g': out['v_mix_post_g'], 'v_w_in': out['v_w_in'], 'v_conv_a_w': out['v_conv_a_w'], 'v_conv_a_b': out['v_conv_a_b'], 'v_pool_w': out['v_pool_w'], 'v_pool_scale': out['v_pool_scale'], 'v_sgu_ln_g': out['v_sgu_ln_g'], 'v_sgu_ln_b': out['v_sgu_ln_b'], 'v_sgu_ws': out['v_sgu_ws'], 'v_sgu_b': out['v_sgu_b'], 'v_conv_d_w': out['v_conv_d_w'], 'v_conv_d_b': out['v_conv_d_b'], 'v_conv_d_ln_g': out['v_conv_d_ln_g'], 'v_conv_d_ln_b': out['v_conv_d_ln_b'], 'v_w_branch': out['v_w_branch'], 'v_w_gate': out['v_w_gate'], 'v_b_gate': out['v_b_gate'], 'v_w_o': out['v_w_o'], 'v_xa_pre_g': out['v_xa_pre_g'], 'v_xa_post_g': out['v_xa_post_g'], 'v_mem_g': out['v_mem_g'], 'v_xa_wq': out['v_xa_wq'], 'v_xa_wk': out['v_xa_wk'], 'v_xa_wv': out['v_xa_wv'], 'v_xa_wo': out['v_xa_wo'], 'v_ffn2_pre_g': out['v_ffn2_pre_g'], 'v_ffn2_post_g': out['v_ffn2_post_g'], 'v_ffn2_w1': out['v_ffn2_w1'], 'v_ffn2_w3': out['v_ffn2_w3'], 'v_ffn2_w2': out['v_ffn2_w2']}


def _loss(weights, diff, rest, loss_target):
    with _jax.named_scope("forward"):
        args = {**rest, TWIN_DIFF_INPUT: diff, **{k: w.astype(_WEIGHT_DTYPES[k]) for k, w in weights.items()}}
        y = _forward(args)
    with _jax.named_scope("loss_head"):
        err = _jnp.square(y.astype(_jnp.float32) - loss_target)
        return 0.5 * _jnp.sum(_jnp.mean(err, axis=-1)) if err.ndim else 0.5 * err


def _adamw(w, g, m, v):
    m = ADAM_B1 * m + (1.0 - ADAM_B1) * g
    v = ADAM_B2 * v + (1.0 - ADAM_B2) * _jnp.square(g)
    m_hat = m / (1.0 - ADAM_B1 ** ADAM_STEP)
    v_hat = v / (1.0 - ADAM_B2 ** ADAM_STEP)
    delta = -ADAM_LR * (m_hat / (_jnp.sqrt(v_hat) + ADAM_EPS) + ADAM_WD * w)
    return delta, m, v


def reference(x, mem, ffn1_pre_g, ffn1_post_g, ffn1_w1, ffn1_w3, ffn1_w2, mix_pre_g, mix_post_g, w_in, conv_a_w, conv_a_b, pool_w, pool_scale, sgu_ln_g, sgu_ln_b, sgu_ws, sgu_b, conv_d_w, conv_d_b, conv_d_ln_g, conv_d_ln_b, w_branch, w_gate, b_gate, w_o, xa_pre_g, xa_post_g, mem_g, xa_wq, xa_wk, xa_wv, xa_wo, ffn2_pre_g, ffn2_post_g, ffn2_w1, ffn2_w3, ffn2_w2, loss_target, m_ffn1_pre_g, m_ffn1_post_g, m_ffn1_w1, m_ffn1_w3, m_ffn1_w2, m_mix_pre_g, m_mix_post_g, m_w_in, m_conv_a_w, m_conv_a_b, m_pool_w, m_pool_scale, m_sgu_ln_g, m_sgu_ln_b, m_sgu_ws, m_sgu_b, m_conv_d_w, m_conv_d_b, m_conv_d_ln_g, m_conv_d_ln_b, m_w_branch, m_w_gate, m_b_gate, m_w_o, m_xa_pre_g, m_xa_post_g, m_mem_g, m_xa_wq, m_xa_wk, m_xa_wv, m_xa_wo, m_ffn2_pre_g, m_ffn2_post_g, m_ffn2_w1, m_ffn2_w3, m_ffn2_w2, v_ffn1_pre_g, v_ffn1_post_g, v_ffn1_w1, v_ffn1_w3, v_ffn1_w2, v_mix_pre_g, v_mix_post_g, v_w_in, v_conv_a_w, v_conv_a_b, v_pool_w, v_pool_scale, v_sgu_ln_g, v_sgu_ln_b, v_sgu_ws, v_sgu_b, v_conv_d_w, v_conv_d_b, v_conv_d_ln_g, v_conv_d_ln_b, v_w_branch, v_w_gate, v_b_gate, v_w_o, v_xa_pre_g, v_xa_post_g, v_mem_g, v_xa_wq, v_xa_wk, v_xa_wv, v_xa_wo, v_ffn2_pre_g, v_ffn2_post_g, v_ffn2_w1, v_ffn2_w3, v_ffn2_w2):
    given = dict(x=x, mem=mem, ffn1_pre_g=ffn1_pre_g, ffn1_post_g=ffn1_post_g, ffn1_w1=ffn1_w1, ffn1_w3=ffn1_w3, ffn1_w2=ffn1_w2, mix_pre_g=mix_pre_g, mix_post_g=mix_post_g, w_in=w_in, conv_a_w=conv_a_w, conv_a_b=conv_a_b, pool_w=pool_w, pool_scale=pool_scale, sgu_ln_g=sgu_ln_g, sgu_ln_b=sgu_ln_b, sgu_ws=sgu_ws, sgu_b=sgu_b, conv_d_w=conv_d_w, conv_d_b=conv_d_b, conv_d_ln_g=conv_d_ln_g, conv_d_ln_b=conv_d_ln_b, w_branch=w_branch, w_gate=w_gate, b_gate=b_gate, w_o=w_o, xa_pre_g=xa_pre_g, xa_post_g=xa_post_g, mem_g=mem_g, xa_wq=xa_wq, xa_wk=xa_wk, xa_wv=xa_wv, xa_wo=xa_wo, ffn2_pre_g=ffn2_pre_g, ffn2_post_g=ffn2_post_g, ffn2_w1=ffn2_w1, ffn2_w3=ffn2_w3, ffn2_w2=ffn2_w2, loss_target=loss_target, m_ffn1_pre_g=m_ffn1_pre_g, m_ffn1_post_g=m_ffn1_post_g, m_ffn1_w1=m_ffn1_w1, m_ffn1_w3=m_ffn1_w3, m_ffn1_w2=m_ffn1_w2, m_mix_pre_g=m_mix_pre_g, m_mix_post_g=m_mix_post_g, m_w_in=m_w_in, m_conv_a_w=m_conv_a_w, m_conv_a_b=m_conv_a_b, m_pool_w=m_pool_w, m_pool_scale=m_pool_scale, m_sgu_ln_g=m_sgu_ln_g, m_sgu_ln_b=m_sgu_ln_b, m_sgu_ws=m_sgu_ws, m_sgu_b=m_sgu_b, m_conv_d_w=m_conv_d_w, m_conv_d_b=m_conv_d_b, m_conv_d_ln_g=m_conv_d_ln_g, m_conv_d_ln_b=m_conv_d_ln_b, m_w_branch=m_w_branch, m_w_gate=m_w_gate, m_b_gate=m_b_gate, m_w_o=m_w_o, m_xa_pre_g=m_xa_pre_g, m_xa_post_g=m_xa_post_g, m_mem_g=m_mem_g, m_xa_wq=m_xa_wq, m_xa_wk=m_xa_wk, m_xa_wv=m_xa_wv, m_xa_wo=m_xa_wo, m_ffn2_pre_g=m_ffn2_pre_g, m_ffn2_post_g=m_ffn2_post_g, m_ffn2_w1=m_ffn2_w1, m_ffn2_w3=m_ffn2_w3, m_ffn2_w2=m_ffn2_w2, v_ffn1_pre_g=v_ffn1_pre_g, v_ffn1_post_g=v_ffn1_post_g, v_ffn1_w1=v_ffn1_w1, v_ffn1_w3=v_ffn1_w3, v_ffn1_w2=v_ffn1_w2, v_mix_pre_g=v_mix_pre_g, v_mix_post_g=v_mix_post_g, v_w_in=v_w_in, v_conv_a_w=v_conv_a_w, v_conv_a_b=v_conv_a_b, v_pool_w=v_pool_w, v_pool_scale=v_pool_scale, v_sgu_ln_g=v_sgu_ln_g, v_sgu_ln_b=v_sgu_ln_b, v_sgu_ws=v_sgu_ws, v_sgu_b=v_sgu_b, v_conv_d_w=v_conv_d_w, v_conv_d_b=v_conv_d_b, v_conv_d_ln_g=v_conv_d_ln_g, v_conv_d_ln_b=v_conv_d_ln_b, v_w_branch=v_w_branch, v_w_gate=v_w_gate, v_b_gate=v_b_gate, v_w_o=v_w_o, v_xa_pre_g=v_xa_pre_g, v_xa_post_g=v_xa_post_g, v_mem_g=v_mem_g, v_xa_wq=v_xa_wq, v_xa_wk=v_xa_wk, v_xa_wv=v_xa_wv, v_xa_wo=v_xa_wo, v_ffn2_pre_g=v_ffn2_pre_g, v_ffn2_post_g=v_ffn2_post_g, v_ffn2_w1=v_ffn2_w1, v_ffn2_w3=v_ffn2_w3, v_ffn2_w2=v_ffn2_w2)
    weights = {n: given[n] for n in TWIN_WEIGHTS}
    shared = {n: given[n] for n in SHARED_INPUTS}
    per_example = {n: given[n] for n in ['x', 'mem']}
    grad_fn = _jax.value_and_grad(_loss, argnums=(0, 1))

    def one_microbatch(ex, loss_target):
        ex = dict(ex)
        diff = ex.pop(TWIN_DIFF_INPUT)
        return grad_fn(weights, diff, {**shared, **ex}, loss_target)

    if N_MICROBATCH == 1:
        loss, (grad_w, grad_x) = one_microbatch(per_example, given["loss_target"])
    else:
        def body(carry, xs):
            loss_sum, grad_sum = carry
            l_k, (gw_k, gx_k) = one_microbatch(xs[0], xs[1])
            with _jax.named_scope("update"):
                return (loss_sum + l_k, _jax.tree.map(_jnp.add, grad_sum, gw_k)), gx_k

        init = (_jnp.zeros((), _jnp.float32), _jax.tree.map(_jnp.zeros_like, weights))
        (loss, grad_w), grad_x = _jax.lax.scan(body, init, (per_example, given["loss_target"]))
    with _jax.named_scope("update"):
        delta_w, new_m, new_v = {}, {}, {}
        for n in TWIN_WEIGHTS:
            delta_w[n], new_m[n], new_v[n] = _adamw(weights[n], grad_w[n], given["m_" + n], given["v_" + n])
    return (loss, grad_x, *[grad_w[n] for n in TWIN_WEIGHTS], *[delta_w[n] for n in TWIN_WEIGHTS],
            *[new_m[n] for n in TWIN_WEIGHTS], *[new_v[n] for n in TWIN_WEIGHTS])
```

```python
import jax
import jax.numpy as jnp
from jax import lax
from jax.experimental import pallas as pl
from jax.experimental.pallas import tpu as pltpu

F32 = jnp.float32
CDT = jnp.bfloat16
EPS = 1e-6
NS = 8
GW = 128
MW = 512
CHUNK = 64
XA_HEADS = 4
POOL_WINDOWS = (2, 4, 8, 16)
VMEM_LIMIT = 56 * 1024 * 1024
ADAM_LR, ADAM_B1, ADAM_B2, ADAM_EPS, ADAM_WD, ADAM_STEP = 0.001, 0.9, 0.999, 1e-08, 0.01, 10

SDS = jax.ShapeDtypeStruct

_SP_NAMES = (("ffn1_pre_g", 1024), ("ffn1_post_g", 1024), ("mix_pre_g", 1024), ("mix_post_g", 1024),
             ("xa_pre_g", 1024), ("xa_post_g", 1024), ("mem_g", 1024), ("ffn2_pre_g", 1024), ("ffn2_post_g", 1024),
             ("conv_a_b", 512), ("pool_scale", 512), ("sgu_ln_g", 512), ("sgu_ln_b", 512), ("conv_d_b", 512),
             ("conv_d_ln_g", 512), ("conv_d_ln_b", 512), ("b_gate", 4096))
_SP = {}
_off = 0
for _n, _w in _SP_NAMES:
    _SP[_n] = (_off, _w)
    _off += _w
_SP_TOTAL = _off


def _call(body, name, grid, in_specs, out_specs, out_shape, scratch=()):
    return pl.pallas_call(
        body, name=name, grid=grid, in_specs=in_specs, out_specs=out_specs, out_shape=out_shape,
        scratch_shapes=list(scratch),
        compiler_params=pltpu.CompilerParams(dimension_semantics=("arbitrary",) * len(grid),
                                             vmem_limit_bytes=VMEM_LIMIT))


def _nn(a, b):
    return lax.dot_general(a, b, (((1,), (0,)), ((), ())), preferred_element_type=F32)


def _nt(a, b):
    return lax.dot_general(a, b, (((1,), (1,)), ((), ())), preferred_element_type=F32)


def _tn(a, b):
    return lax.dot_general(a, b, (((0,), (0,)), ((), ())), preferred_element_type=F32)


def _rms(x):
    r = lax.rsqrt(jnp.mean(x * x, axis=-1, keepdims=True) + EPS)
    return x * r, r


def _rms_bwd(n, r, g, dout):
    dn = dout * g
    dx = r * (dn - n * jnp.mean(dn * n, axis=-1, keepdims=True))
    return dx, jnp.sum(dout * n, axis=0, keepdims=True)


def _ln(y):
    mu = jnp.mean(y, axis=-1, keepdims=True)
    yc = y - mu
    rs = lax.rsqrt(jnp.mean(yc * yc, axis=-1, keepdims=True) + EPS)
    return yc * rs, rs


def _ln_bwd(xh, rs, dxh):
    return rs * (dxh - jnp.mean(dxh, axis=-1, keepdims=True) - xh * jnp.mean(dxh * xh, axis=-1, keepdims=True))


def _silu_parts(a):
    s = jax.nn.sigmoid(a)
    sl = a * s
    return sl, s + sl * (1.0 - s)


_GELU_C = 0.7978845608028654
_GELU_A = 0.044715


def _gelu(x):
    return 0.5 * x * (1.0 + jnp.tanh(_GELU_C * (x + _GELU_A * x * x * x)))


def _gelu_parts(x):
    t = jnp.tanh(_GELU_C * (x + _GELU_A * x * x * x))
    g = 0.5 * x * (1.0 + t)
    dg = 0.5 * (1.0 + t) + 0.5 * x * (1.0 - t * t) * _GELU_C * (1.0 + 3.0 * _GELU_A * x * x)
    return g, dg


def _spspec(name, width, imap):
    off = _SP[name][0]
    assert off % width == 0
    return pl.BlockSpec((1, width), lambda *a: (0, off // width + imap(*a)))


def _zero(*a):
    return 0


def _ffn_fwd(x, sp, pre, post, pf):
    S, D = x.shape
    FS = pf.shape[1] // 3
    TM = min(512, S)

    def body(x_ref, pg_ref, qg_ref, w1_ref, w3_ref, w2_ref, xo_ref, hb_ref, a_ref, b_ref, y_ref, hb_s, acc):
        j = pl.program_id(1)

        @pl.when(j == 0)
        def _():
            n, _ = _rms(x_ref[...])
            hb = (n * pg_ref[...]).astype(CDT)
            hb_s[...] = hb
            hb_ref[...] = hb
            acc[...] = jnp.zeros_like(acc)

        hb = hb_s[...]
        a = _nt(hb, w1_ref[...])
        b = _nt(hb, w3_ref[...])
        a_ref[...] = a.astype(CDT)
        b_ref[...] = b.astype(CDT)
        u = (a * jax.nn.sigmoid(a) * b).astype(CDT)
        acc[...] += _nn(u, w2_ref[...])

        @pl.when(j == NS - 1)
        def _():
            y = acc[...]
            y_ref[...] = y.astype(CDT)
            n, _ = _rms(y)
            xo_ref[...] = x_ref[...] + 0.5 * (n * qg_ref[...])

    row = lambda i, j: (i, 0)
    slot = lambda i, j: (j, i, 0)
    return _call(
        body, "ffn_fwd", (S // TM, NS),
        [pl.BlockSpec((TM, D), row), _spspec(pre, D, _zero), _spspec(post, D, _zero),
         pl.BlockSpec((None, FS, D), lambda i, j: (j, 0, 0)), pl.BlockSpec((None, FS, D), lambda i, j: (j, 1, 0)),
         pl.BlockSpec((None, FS, D), lambda i, j: (j, 2, 0))],
        [pl.BlockSpec((TM, D), row), pl.BlockSpec((TM, D), row), pl.BlockSpec((None, TM, FS), slot),
         pl.BlockSpec((None, TM, FS), slot), pl.BlockSpec((TM, D), row)],
        [SDS((S, D), F32), SDS((S, D), CDT), SDS((NS, S, FS), CDT), SDS((NS, S, FS), CDT), SDS((S, D), CDT)],
        [pltpu.VMEM((TM, D), CDT), pltpu.VMEM((TM, D), F32)])(x, sp, sp, pf, pf, pf)


def _ffn_bwd_act(dxo, x, y, a, b, sp, pre, post, pf):
    S, D = x.shape
    FS = pf.shape[1] // 3
    TM = min(512, S)

    def body(dxo_ref, x_ref, y_ref, a_ref, b_ref, pg_ref, qg_ref, w1_ref, w3_ref, w2_ref,
             dx_ref, dyb_ref, da_ref, db_ref, gp_ref, dyb_s, acc):
        i = pl.program_id(0)
        j = pl.program_id(1)

        @pl.when((i == 0) & (j == 0))
        def _():
            gp_ref[...] = jnp.zeros_like(gp_ref)

        @pl.when(j == 0)
        def _():
            n, r = _rms(y_ref[...].astype(F32))
            dy, dg = _rms_bwd(n, r, qg_ref[...], 0.5 * dxo_ref[...])
            dyb = dy.astype(CDT)
            dyb_s[...] = dyb
            dyb_ref[...] = dyb
            gp_ref[1:2, :] += dg
            acc[...] = jnp.zeros_like(acc)

        sl, dsl = _silu_parts(a_ref[...].astype(F32))
        du = _nt(dyb_s[...], w2_ref[...])
        db = (du * sl).astype(CDT)
        da = (du * b_ref[...].astype(F32) * dsl).astype(CDT)
        da_ref[...] = da
        db_ref[...] = db
        acc[...] += _nn(da, w1_ref[...]) + _nn(db, w3_ref[...])

        @pl.when(j == NS - 1)
        def _():
            n, r = _rms(x_ref[...])
            dx, dg = _rms_bwd(n, r, pg_ref[...], acc[...])
            dx_ref[...] = dxo_ref[...] + dx
            gp_ref[0:1, :] += dg

    row = lambda i, j: (i, 0)
    slot = lambda i, j: (j, i, 0)
    return _call(
        body, "ffn_bwd_act", (S // TM, NS),
        [pl.BlockSpec((TM, D), row), pl.BlockSpec((TM, D), row), pl.BlockSpec((TM, D), row),
         pl.BlockSpec((None, TM, FS), slot), pl.BlockSpec((None, TM, FS), slot),
         _spspec(pre, D, _zero), _spspec(post, D, _zero),
         pl.BlockSpec((None, FS, D), lambda i, j: (j, 0, 0)), pl.BlockSpec((None, FS, D), lambda i, j: (j, 1, 0)),
         pl.BlockSpec((None, FS, D), lambda i, j: (j, 2, 0))],
        [pl.BlockSpec((TM, D), row), pl.BlockSpec((TM, D), row), pl.BlockSpec((None, TM, FS), slot),
         pl.BlockSpec((None, TM, FS), slot), pl.BlockSpec((8, D), lambda i, j: (0, 0))],
        [SDS((S, D), F32), SDS((S, D), CDT), SDS((NS, S, FS), CDT), SDS((NS, S, FS), CDT), SDS((8, D), F32)],
        [pltpu.VMEM((TM, D), CDT), pltpu.VMEM((TM, D), F32)])(dxo, x, y, a, b, sp, sp, pf, pf, pf)


def _ffn_bwd_w(hb, dyb, a, b, da, db):
    S, D = hb.shape
    FS = a.shape[2]
    TK = min(512, S)
    NK = S // TK

    def body(hb_ref, dyb_ref, a_ref, b_ref, da_ref, db_ref, g_ref, acc):
        k = pl.program_id(1)

        @pl.when(k == 0)
        def _():
            acc[...] = jnp.zeros_like(acc)

        af = a_ref[...].astype(F32)
        u = (af * jax.nn.sigmoid(af) * b_ref[...].astype(F32)).astype(CDT)
        hb = hb_ref[...]
        acc[0:FS, :] += _tn(da_ref[...], hb)
        acc[FS:2 * FS, :] += _tn(db_ref[...], hb)
        acc[2 * FS:3 * FS, :] += _tn(u, dyb_ref[...])

        @pl.when(k == NK - 1)
        def _():
            g_ref[...] = acc[...].astype(CDT)

    row = lambda j, k: (k, 0)
    slot = lambda j, k: (j, k, 0)
    return _call(
        body, "ffn_bwd_w", (NS, NK),
        [pl.BlockSpec((TK, D), row), pl.BlockSpec((TK, D), row)] + [pl.BlockSpec((None, TK, FS), slot)] * 4,
        pl.BlockSpec((None, 3 * FS, D), lambda j, k: (j, 0, 0)),
        SDS((NS, 3 * FS, D), CDT),
        [pltpu.VMEM((3 * FS, D), F32)])(hb, dyb, a, b, da, db)


def _mix_in(x, sp, pma):
    S, D = x.shape
    TM = min(512, S)

    def body(x_ref, pg_ref, bg_ref, wi_ref, wg_ref, hb_ref, z_ref, g_ref, hb_s):
        @pl.when(pl.program_id(1) == 0)
        def _():
            n, _ = _rms(x_ref[...])
            hb = (n * pg_ref[...]).astype(CDT)
            hb_s[...] = hb
            hb_ref[...] = hb

        hb = hb_s[...]
        z_ref[...] = _nt(hb, wi_ref[...]).astype(CDT)
        g_ref[...] = jax.nn.sigmoid(_nt(hb, wg_ref[...]) + bg_ref[...]).astype(CDT)

    row = lambda i, j: (i, 0)
    return _call(
        body, "mix_in", (S // TM, NS),
        [pl.BlockSpec((TM, D), row), _spspec("mix_pre_g", D, _zero), _spspec("b_gate", MW, lambda i, j: j),
         pl.BlockSpec((None, MW, D), lambda i, j: (j, 0, 0)), pl.BlockSpec((None, MW, D), lambda i, j: (j, 1, 0))],
        [pl.BlockSpec((TM, D), row), pl.BlockSpec((None, TM, MW), lambda i, j: (j, i, 0)),
         pl.BlockSpec((None, TM, MW), lambda i, j: (j // 2, i, j % 2))],
        [SDS((S, D), CDT), SDS((NS, S, MW), CDT), SDS((4, S, D), CDT)],
        [pltpu.VMEM((TM, D), CDT)])(x, sp, sp, pma, pma)


def _mix_in_bwd_act(dz, dgp, dxr, x, sp, pma):
    S, D = x.shape
    TM = min(512, S)

    def body(dz_ref, dg_ref, dxr_ref, x_ref, pg_ref, wi_ref, wg_ref, dx_ref, gp_ref, acc):
        i = pl.program_id(0)
        j = pl.program_id(1)

        @pl.when((i == 0) & (j == 0))
        def _():
            gp_ref[...] = jnp.zeros_like(gp_ref)

        @pl.when(j == 0)
        def _():
            acc[...] = jnp.zeros_like(acc)

        acc[...] += _nn(dz_ref[...], wi_ref[...]) + _nn(dg_ref[...], wg_ref[...])

        @pl.when(j == NS - 1)
        def _():
            n, r = _rms(x_ref[...])
            dx, dg = _rms_bwd(n, r, pg_ref[...], acc[...])
            dx_ref[...] = dxr_ref[...] + dx
            gp_ref[0:1, :] += dg

    row = lambda i, j: (i, 0)
    return _call(
        body, "mix_in_bwd_act", (S // TM, NS),
        [pl.BlockSpec((None, TM, MW), lambda i, j: (j, i, 0)), pl.BlockSpec((None, TM, MW), lambda i, j: (j // 2, i, j % 2)),
         pl.BlockSpec((TM, D), row), pl.BlockSpec((TM, D), row), _spspec("mix_pre_g", D, _zero),
         pl.BlockSpec((None, MW, D), lambda i, j: (j, 0, 0)), pl.BlockSpec((None, MW, D), lambda i, j: (j, 1, 0))],
        [pl.BlockSpec((TM, D), row), pl.BlockSpec((8, D), lambda i, j: (0, 0))],
        [SDS((S, D), F32), SDS((8, D), F32)],
        [pltpu.VMEM((TM, D), F32)])(dz, dgp, dxr, x, sp, pma, pma)


def _mix_in_bwd_w(dz, dgp, hb):
    S, D = hb.shape
    TK = min(512, S)
    NK = S // TK

    def body(dz_ref, dg_ref, hb_ref, g_ref, bg_ref, acc):
        k = pl.program_id(1)

        @pl.when(k == 0)
        def _():
            acc[...] = jnp.zeros_like(acc)
            bg_ref[...] = jnp.zeros_like(bg_ref)

        hb = hb_ref[...]
        dg = dg_ref[...]
        acc[0:MW, :] += _tn(dz_ref[...], hb)
        acc[MW:2 * MW, :] += _tn(dg, hb)
        bg_ref[0:1, :] += jnp.sum(dg.astype(F32), axis=0, keepdims=True)

        @pl.when(k == NK - 1)
        def _():
            g_ref[...] = acc[...].astype(CDT)

    return _call(
        body, "mix_in_bwd_w", (NS, NK),
        [pl.BlockSpec((None, TK, MW), lambda j, k: (j, k, 0)), pl.BlockSpec((None, TK, MW), lambda j, k: (j // 2, k, j % 2)),
         pl.BlockSpec((TK, D), lambda j, k: (k, 0))],
        [pl.BlockSpec((None, 2 * MW, D), lambda j, k: (j, 0, 0)), pl.BlockSpec((None, 8, MW), lambda j, k: (j, 0, 0))],
        [SDS((NS, 2 * MW, D), CDT), SDS((NS, 8, MW), F32)],
        [pltpu.VMEM((2 * MW, D), F32)])(dz, dgp, hb)


def _causal_taps(pad_ref, i, ch, halo, k_taps, lanes=slice(None)):
    val = pad_ref[pl.ds(pl.multiple_of(i * ch, 8), ch + halo), lanes]
    out = []
    for k in range(k_taps):
        s = k_taps - 1 - k
        out.append((k, (pltpu.roll(val, s, 0) if s else val)[halo:, :]))
    return out


def _anti_taps(pad_ref, i, ch, halo, k_taps, lanes=slice(None)):
    val = pad_ref[pl.ds(pl.multiple_of(i * ch, 8), ch + halo), lanes]
    n = ch + halo
    out = []
    for k in range(k_taps):
        s = k_taps - 1 - k
        out.append((k, (pltpu.roll(val, n - s, 0) if s else val)[:ch, :]))
    return out


def _conv_geometry(S, k_taps):
    halo = 8 * ((k_taps - 1 + 7) // 8)
    ch = min(256, S)
    return halo, ch, S // ch


def _rows(i, ch):
    return pl.ds(pl.multiple_of(i * ch, ch), ch)


def _mixA_fwd(z, cw, sp):
    S = z.shape[1]
    K = cw.shape[0]
    H, CH, NCH = _conv_geometry(S, K)

    def body(z_ref, w_ref, b_ref, o_ref, pad):
        pad[0:H, :] = jnp.zeros((H, GW), F32)

        def fill(i, c):
            r = _rows(i, CH)
            pad[pl.ds(pl.multiple_of(i * CH + H, 8), CH), :] = z_ref[2, r, :].astype(F32) * z_ref[0, r, :].astype(F32)
            return c

        lax.fori_loop(0, NCH, fill, 0)

        def conv(i, c):
            r = _rows(i, CH)
            acc = jnp.zeros((CH, GW), F32)
            for k, sh in _causal_taps(pad, i, CH, H, K):
                acc = acc + w_ref[k:k + 1, :] * sh
            o_ref[r, :] = (z_ref[1, r, :].astype(F32) * (acc + b_ref[...])).astype(CDT)
            return c

        lax.fori_loop(0, NCH, conv, 0)

    return _call(
        body, "mixA_fwd", (MW // GW,),
        [pl.BlockSpec((3, S, GW), lambda c: (0, 0, c)), pl.BlockSpec((K, GW), lambda c: (0, c)),
         _spspec("conv_a_b", GW, lambda c: c)],
        pl.BlockSpec((S, GW), lambda c: (0, c)), SDS((S, MW), CDT),
        [pltpu.VMEM((H + S, GW), F32)])(z, cw, sp)


def _mixA_bwd(z, dm, cw, sp):
    S = z.shape[1]
    K = cw.shape[0]
    H, CH, NCH = _conv_geometry(S, K)

    def body(z_ref, dm_ref, w_ref, b_ref, dz_ref, dw_ref, db_ref, pad, dpad, dw_s):
        pad[0:H, :] = jnp.zeros((H, GW), F32)
        dpad[pl.ds(S, H), :] = jnp.zeros((H, GW), F32)
        dw_s[...] = jnp.zeros_like(dw_s)
        db_ref[...] = jnp.zeros_like(db_ref)

        def fill(i, c):
            r = _rows(i, CH)
            pad[pl.ds(pl.multiple_of(i * CH + H, 8), CH), :] = z_ref[2, r, :].astype(F32) * z_ref[0, r, :].astype(F32)
            return c

        lax.fori_loop(0, NCH, fill, 0)

        def p1(i, c):
            r = _rows(i, CH)
            taps = _causal_taps(pad, i, CH, H, K)
            acc = jnp.zeros((CH, GW), F32)
            for k, sh in taps:
                acc = acc + w_ref[k:k + 1, :] * sh
            dmf = dm_ref[r, :].astype(F32)
            dz_ref[1, r, :] = (dmf * (acc + b_ref[...])).astype(CDT)
            dc = dmf * z_ref[1, r, :].astype(F32)
            dpad[r, :] = dc
            for k, sh in taps:
                dw_s[k:k + 1, :] += jnp.sum(dc * sh, axis=0, keepdims=True)
            db_ref[0:1, :] += jnp.sum(dc, axis=0, keepdims=True)
            return c

        lax.fori_loop(0, NCH, p1, 0)

        def p2(i, c):
            r = _rows(i, CH)
            dq = jnp.zeros((CH, GW), F32)
            for k, sh in _anti_taps(dpad, i, CH, H, K):
                dq = dq + w_ref[k:k + 1, :] * sh
            dz_ref[0, r, :] = (dq * z_ref[2, r, :].astype(F32)).astype(CDT)
            dz_ref[2, r, :] = (dq * z_ref[0, r, :].astype(F32)).astype(CDT)
            return c

        lax.fori_loop(0, NCH, p2, 0)
        dw_ref[...] = dw_s[0:K, :]

    return _call(
        body, "mixA_bwd", (MW // GW,),
        [pl.BlockSpec((3, S, GW), lambda c: (0, 0, c)), pl.BlockSpec((S, GW), lambda c: (0, c)),
         pl.BlockSpec((K, GW), lambda c: (0, c)), _spspec("conv_a_b", GW, lambda c: c)],
        [pl.BlockSpec((3, S, GW), lambda c: (0, 0, c)), pl.BlockSpec((K, GW), lambda c: (0, c)),
         pl.BlockSpec((8, GW), lambda c: (0, c))],
        [SDS((3, S, MW), CDT), SDS((K, MW), F32), SDS((8, MW), F32)],
        [pltpu.VMEM((H + S, GW), F32), pltpu.VMEM((S + H, GW), F32), pltpu.VMEM((8 * ((K + 7) // 8), GW), F32)])(z, dm, cw, sp)


def _mixD_conv_fwd(z, cw, sp):
    S = z.shape[1]
    K = cw.shape[0]
    H, CH, NCH = _conv_geometry(S, K)

    def body(z_ref, w_ref, b_ref, o_ref, pad):
        pad[0:H, :] = jnp.zeros((H, GW), F32)

        def fill(i, c):
            r = _rows(i, CH)
            pad[pl.ds(pl.multiple_of(i * CH + H, 8), CH), :] = (
                z_ref[0, r, :].astype(F32) * jax.nn.sigmoid(z_ref[1, r, :].astype(F32)))
            return c

        lax.fori_loop(0, NCH, fill, 0)

        def conv(i, c):
            acc = jnp.zeros((CH, GW), F32)
            for k, sh in _causal_taps(pad, i, CH, H, K):
                acc = acc + w_ref[k:k + 1, :] * sh
            o_ref[_rows(i, CH), :] = (acc + b_ref[...]).astype(CDT)
            return c

        lax.fori_loop(0, NCH, conv, 0)

    return _call(
        body, "mixD_conv_fwd", (MW // GW,),
        [pl.BlockSpec((2, S, GW), lambda c: (3, 0, c)), pl.BlockSpec((K, GW), lambda c: (0, c)),
         _spspec("conv_d_b", GW, lambda c: c)],
        pl.BlockSpec((S, GW), lambda c: (0, c)), SDS((S, MW), CDT),
        [pltpu.VMEM((H + S, GW), F32)])(z, cw, sp)


def _mixD_conv_bwd(z, dy, cw):
    S = z.shape[1]
    K = cw.shape[0]
    H, CH, NCH = _conv_geometry(S, K)

    def body(z_ref, dy_ref, w_ref, dz_ref, dw_ref, db_ref, pad, dpad, dw_s):
        pad[0:H, :] = jnp.zeros((H, GW), F32)
        dpad[pl.ds(S, H), :] = jnp.zeros((H, GW), F32)
        dw_s[...] = jnp.zeros_like(dw_s)
        db_ref[...] = jnp.zeros_like(db_ref)

        def fill(i, c):
            r = _rows(i, CH)
            pad[pl.ds(pl.multiple_of(i * CH + H, 8), CH), :] = (
                z_ref[0, r, :].astype(F32) * jax.nn.sigmoid(z_ref[1, r, :].astype(F32)))
            dpad[r, :] = dy_ref[r, :].astype(F32)
            return c

        lax.fori_loop(0, NCH, fill, 0)

        def p1(i, c):
            dyf = dy_ref[_rows(i, CH), :].astype(F32)
            for k, sh in _causal_taps(pad, i, CH, H, K):
                dw_s[k:k + 1, :] += jnp.sum(dyf * sh, axis=0, keepdims=True)
            db_ref[0:1, :] += jnp.sum(dyf, axis=0, keepdims=True)
            return c

        lax.fori_loop(0, NCH, p1, 0)

        def p2(i, c):
            r = _rows(i, CH)
            dq = jnp.zeros((CH, GW), F32)
            for k, sh in _anti_taps(dpad, i, CH, H, K):
                dq = dq + w_ref[k:k + 1, :] * sh
            a = z_ref[0, r, :].astype(F32)
            sg = jax.nn.sigmoid(z_ref[1, r, :].astype(F32))
            dz_ref[0, r, :] = (dq * sg).astype(CDT)
            dz_ref[1, r, :] = (dq * a * sg * (1.0 - sg)).astype(CDT)
            return c

        lax.fori_loop(0, NCH, p2, 0)
        dw_ref[...] = dw_s[0:K, :]

    return _call(
        body, "mixD_conv_bwd", (MW // GW,),
        [pl.BlockSpec((2, S, GW), lambda c: (3, 0, c)), pl.BlockSpec((S, GW), lambda c: (0, c)),
         pl.BlockSpec((K, GW), lambda c: (0, c))],
        [pl.BlockSpec((2, S, GW), lambda c: (0, 0, c)), pl.BlockSpec((K, GW), lambda c: (0, c)),
         pl.BlockSpec((8, GW), lambda c: (0, c))],
        [SDS((2, S, MW), CDT), SDS((K, MW), F32), SDS((8, MW), F32)],
        [pltpu.VMEM((H + S, GW), F32), pltpu.VMEM((S + H, GW), F32), pltpu.VMEM((8 * ((K + 7) // 8), GW), F32)])(z, dy, cw)


def _mixD_ln_bwd(dm, yd, sp):
    S = yd.shape[0]
    TM = min(512, S)

    def body(dm_ref, y_ref, lg_ref, lb_ref, dy_ref, gp_ref):
        @pl.when(pl.program_id(0) == 0)
        def _():
            gp_ref[...] = jnp.zeros_like(gp_ref)

        xh, rs = _ln(y_ref[...].astype(F32))
        _, dsl = _silu_parts(xh * lg_ref[...] + lb_ref[...])
        dl = dm_ref[...].astype(F32) * dsl
        gp_ref[0:1, :] += jnp.sum(dl * xh, axis=0, keepdims=True)
        gp_ref[1:2, :] += jnp.sum(dl, axis=0, keepdims=True)
        dy_ref[...] = _ln_bwd(xh, rs, dl * lg_ref[...]).astype(CDT)

    row = lambda i: (i, 0)
    return _call(
        body, "mixD_ln_bwd", (S // TM,),
        [pl.BlockSpec((TM, MW), row), pl.BlockSpec((TM, MW), row), _spspec("conv_d_ln_g", MW, _zero),
         _spspec("conv_d_ln_b", MW, _zero)],
        [pl.BlockSpec((TM, MW), row), pl.BlockSpec((8, MW), lambda i: (0, 0))],
        [SDS((S, MW), CDT), SDS((8, MW), F32)])(dm, yd, sp, sp)


def _box_causal(val, g):
    s = val
    for d in range(g + 1):
        s = s + pltpu.roll(s, 1 << d, 0)
    return s


def _box_anti(val, g):
    n = val.shape[0]
    s = val
    for d in range(g + 1):
        s = s + pltpu.roll(s, n - (1 << d), 0)
    return s


def _pool_count(i, ch, win):
    t = lax.broadcasted_iota(jnp.int32, (ch, GW), 0) + (i * ch + 1)
    return jnp.minimum(t, win).astype(F32)


def _mixB_fwd(z, wp, sp):
    S = z.shape[1]
    H, CH = 16, min(256, S)
    NCH = S // CH
    assert POOL_WINDOWS == tuple(2 << g for g in range(4))

    def body(p_ref, wp_ref, sc_ref, o_ref, pad):
        pad[0:H, :] = jnp.zeros((H, MW), F32)

        def fill(i, c):
            pad[pl.ds(pl.multiple_of(i * CH + H, 8), CH), :] = p_ref[_rows(i, CH), :].astype(F32)
            return c

        lax.fori_loop(0, NCH, fill, 0)

        def step(i, c):
            r = _rows(i, CH)
            for g in range(4):
                gs = slice(g * GW, (g + 1) * GW)
                val = pad[pl.ds(pl.multiple_of(i * CH, 8), CH + H), gs]
                pooled = _box_causal(val, g)[H:, :] / _pool_count(i, CH, POOL_WINDOWS[g]) - val[H:, :]
                mixed = _nn(pooled.astype(CDT), wp_ref[g].astype(CDT))
                o_ref[r, gs] = (mixed * sc_ref[:, gs]).astype(CDT)
            return c

        lax.fori_loop(0, NCH, step, 0)

    return _call(
        body, "mixB_fwd", (1,),
        [pl.BlockSpec((None, S, MW), lambda i: (3, 0, 0)), pl.BlockSpec((4, GW, GW), lambda i: (0, 0, 0)),
         _spspec("pool_scale", MW, _zero)],
        pl.BlockSpec((S, MW), lambda i: (0, 0)), SDS((S, MW), CDT),
        [pltpu.VMEM((H + S, MW), F32)])(z, wp, sp)


def _mixB_bwd(z, dm, wp, sp):
    S = z.shape[1]
    H, CH = 16, min(256, S)
    NCH = S // CH

    def body(p_ref, dm_ref, wp_ref, sc_ref, dz_ref, dwp_ref, dsc_ref, pad, rpad):
        pad[0:H, :] = jnp.zeros((H, MW), F32)
        rpad[pl.ds(S, H), :] = jnp.zeros((H, MW), F32)
        dwp_ref[...] = jnp.zeros_like(dwp_ref)
        dsc_ref[...] = jnp.zeros_like(dsc_ref)

        def fill(i, c):
            pad[pl.ds(pl.multiple_of(i * CH + H, 8), CH), :] = p_ref[_rows(i, CH), :].astype(F32)
            return c

        lax.fori_loop(0, NCH, fill, 0)

        def p1(i, c):
            r = _rows(i, CH)
            for g in range(4):
                gs = slice(g * GW, (g + 1) * GW)
                cnt = _pool_count(i, CH, POOL_WINDOWS[g])
                val = pad[pl.ds(pl.multiple_of(i * CH, 8), CH + H), gs]
                pooled = (_box_causal(val, g)[H:, :] / cnt - val[H:, :]).astype(CDT)
                w = wp_ref[g].astype(CDT)
                mixed = _nn(pooled, w)
                dmf = dm_ref[r, gs].astype(F32)
                dsc_ref[0:1, gs] += jnp.sum(dmf * mixed, axis=0, keepdims=True)
                dmx = (dmf * sc_ref[:, gs]).astype(CDT)
                dwp_ref[g] += _tn(pooled, dmx)
                rpad[r, gs] = _nt(dmx, w) / cnt
            return c

        lax.fori_loop(0, NCH, p1, 0)

        def p2(i, c):
            r = _rows(i, CH)
            for g in range(4):
                gs = slice(g * GW, (g + 1) * GW)
                val = rpad[pl.ds(pl.multiple_of(i * CH, 8), CH + H), gs]
                dp = _box_anti(val, g)[:CH, :] - val[:CH, :] * _pool_count(i, CH, POOL_WINDOWS[g])
                dz_ref[r, gs] = dp.astype(CDT)
            return c

        lax.fori_loop(0, NCH, p2, 0)

    return _call(
        body, "mixB_bwd", (1,),
        [pl.BlockSpec((None, S, MW), lambda i: (3, 0, 0)), pl.BlockSpec((S, MW), lambda i: (0, 0)),
         pl.BlockSpec((4, GW, GW), lambda i: (0, 0, 0)), _spspec("pool_scale", MW, _zero)],
        [pl.BlockSpec((None, S, MW), lambda i: (0, 0, 0)), pl.BlockSpec((4, GW, GW), lambda i: (0, 0, 0)),
         pl.BlockSpec((8, MW), lambda i: (0, 0))],
        [SDS((1, S, MW), CDT), SDS((4, GW, GW), F32), SDS((8, MW), F32)],
        [pltpu.VMEM((H + S, MW), F32), pltpu.VMEM((S + H, MW), F32)])(z, dm, wp, sp)


def _sgu_mask():
    ci = lax.broadcasted_iota(jnp.int32, (GW, GW), 0) // CHUNK
    cj = lax.broadcasted_iota(jnp.int32, (GW, GW), 1) // CHUNK
    return cj <= ci


def _mixC_fwd(z, ws, bsc, sp):
    S = z.shape[1]
    RB = min(512, S)

    def body(z_ref, lg_ref, lb_ref, ws_ref, bs_ref, o_ref):
        mask = _sgu_mask()
        gu = _gelu(z_ref[0].astype(F32))
        xh, _ = _ln(_gelu(z_ref[1].astype(F32)))
        vn = (xh * lg_ref[...] + lb_ref[...]).astype(CDT)
        for g in range(4):
            gs = slice(g * GW, (g + 1) * GW)
            wm = jnp.where(mask, ws_ref[g], 0.0).astype(CDT)
            for nb in range(RB // GW):
                rs = slice(nb * GW, (nb + 1) * GW)
                mixed = _nn(wm, vn[rs, gs]) + bs_ref[g]
                o_ref[rs, gs] = (gu[rs, gs] * mixed).astype(CDT)

    return _call(
        body, "mixC_fwd", (S // RB,),
        [pl.BlockSpec((2, RB, MW), lambda i: (2, i, 0)), _spspec("sgu_ln_g", MW, _zero), _spspec("sgu_ln_b", MW, _zero),
         pl.BlockSpec((4, GW, GW), lambda i: (0, 0, 0)), pl.BlockSpec((4, GW, 1), lambda i: (0, 0, 0))],
        pl.BlockSpec((RB, MW), lambda i: (i, 0)), SDS((S, MW), CDT))(z, sp, sp, ws, bsc)


def _mixC_bwd(z, dm, ws, bsc, sp):
    S = z.shape[1]
    RB = min(512, S)
    NR = S // RB

    def body(z_ref, dm_ref, lg_ref, lb_ref, ws_ref, bs_ref, dz_ref, dws_ref, dbs_ref, gp_ref, dvn_s):
        i = pl.program_id(0)

        @pl.when(i == 0)
        def _():
            dws_ref[...] = jnp.zeros_like(dws_ref)
            dbs_ref[...] = jnp.zeros_like(dbs_ref)
            gp_ref[...] = jnp.zeros_like(gp_ref)

        mask = _sgu_mask()
        gu, dgu = _gelu_parts(z_ref[0].astype(F32))
        gv, dgv = _gelu_parts(z_ref[1].astype(F32))
        xh, rs_ = _ln(gv)
        vn = (xh * lg_ref[...] + lb_ref[...]).astype(CDT)
        dmf = dm_ref[...].astype(F32)
        for g in range(4):
            gs = slice(g * GW, (g + 1) * GW)
            wm = jnp.where(mask, ws_ref[g], 0.0).astype(CDT)
            for nb in range(RB // GW):
                rs = slice(nb * GW, (nb + 1) * GW)
                vb = vn[rs, gs]
                mixed = _nn(wm, vb) + bs_ref[g]
                dz_ref[0, rs, gs] = (dmf[rs, gs] * mixed * dgu[rs, gs]).astype(CDT)
                dmx = dmf[rs, gs] * gu[rs, gs]
                dbs_ref[g] += dmx
                dmxc = dmx.astype(CDT)
                dws_ref[g] += _nt(dmxc, vb)
                dvn_s[rs, gs] = _tn(wm, dmxc)
        dvn = dvn_s[...]
        gp_ref[0:1, :] += jnp.sum(dvn * xh, axis=0, keepdims=True)
        gp_ref[1:2, :] += jnp.sum(dvn, axis=0, keepdims=True)
        dz_ref[1] = (_ln_bwd(xh, rs_, dvn * lg_ref[...]) * dgv).astype(CDT)

        @pl.when(i == NR - 1)
        def _():
            for g in range(4):
                dws_ref[g] = jnp.where(mask, dws_ref[g], 0.0)
                dbs_ref[g] = jnp.broadcast_to(jnp.sum(dbs_ref[g], axis=1, keepdims=True), (GW, GW))

    full3 = lambda i: (0, 0, 0)
    return _call(
        body, "mixC_bwd", (NR,),
        [pl.BlockSpec((2, RB, MW), lambda i: (2, i, 0)), pl.BlockSpec((RB, MW), lambda i: (i, 0)),
         _spspec("sgu_ln_g", MW, _zero), _spspec("sgu_ln_b", MW, _zero),
         pl.BlockSpec((4, GW, GW), full3), pl.BlockSpec((4, GW, 1), full3)],
        [pl.BlockSpec((2, RB, MW), lambda i: (0, i, 0)), pl.BlockSpec((4, GW, GW), full3), pl.BlockSpec((4, GW, GW), full3),
         pl.BlockSpec((8, MW), lambda i: (0, 0))],
        [SDS((2, S, MW), CDT), SDS((4, GW, GW), F32), SDS((4, GW, GW), F32), SDS((8, MW), F32)],
        [pltpu.VMEM((RB, MW), F32)])(z, dm, sp, sp, ws, bsc)


def _merge_fwd(ma, mb, mc, yd, g, wb, pwo, x, sp):
    S, D = x.shape
    TM = min(512, S)

    def body(ma_ref, mb_ref, mc_ref, yd_ref, g_ref, wb_ref, wo_ref, x_ref, lg_ref, lb_ref, qg_ref,
             xo_ref, md_ref, yk_ref, mg_ref, mo_ref, md_s, acc):
        j = pl.program_id(1)

        @pl.when(j == 0)
        def _():
            xh, _ = _ln(yd_ref[...].astype(F32))
            sl, _ = _silu_parts(xh * lg_ref[...] + lb_ref[...])
            md = sl.astype(CDT)
            md_s[...] = md
            md_ref[...] = md
            acc[...] = jnp.zeros_like(acc)

        merged = jnp.zeros((TM, GW), F32)
        for k, m in enumerate((ma_ref[...], mb_ref[...], mc_ref[...], md_s[...])):
            yk = _nn(m, wb_ref[k])
            yk_ref[k] = yk.astype(CDT)
            merged = merged + g_ref[k].astype(F32) * yk
        mgc = merged.astype(CDT)
        mg_ref[...] = mgc
        acc[...] += _nn(mgc, wo_ref[...])

        @pl.when(j == NS - 1)
        def _():
            mo = acc[...]
            mo_ref[...] = mo.astype(CDT)
            n, _ = _rms(mo)
            xo_ref[...] = x_ref[...] + n * qg_ref[...]

    row = lambda i, j: (i, 0)
    rowm = pl.BlockSpec((TM, MW), row)
    colb = pl.BlockSpec((4, TM, GW), lambda i, j: (0, i, j))
    return _call(
        body, "merge_fwd", (S // TM, NS),
        [rowm, rowm, rowm, rowm, colb, pl.BlockSpec((None, 4, MW, GW), lambda i, j: (j, 0, 0, 0)),
         pl.BlockSpec((None, GW, D), lambda i, j: (j, 0, 0)), pl.BlockSpec((TM, D), row),
         _spspec("conv_d_ln_g", MW, _zero), _spspec("conv_d_ln_b", MW, _zero), _spspec("mix_post_g", D, _zero)],
        [pl.BlockSpec((TM, D), row), rowm, colb, pl.BlockSpec((TM, GW), lambda i, j: (i, j)), pl.BlockSpec((TM, D), row)],
        [SDS((S, D), F32), SDS((S, MW), CDT), SDS((4, S, D), CDT), SDS((S, D), CDT), SDS((S, D), CDT)],
        [pltpu.VMEM((TM, MW), CDT), pltpu.VMEM((TM, D), F32)])(ma, mb, mc, yd, g, wb, pwo, x, sp, sp, sp)


def _merge_bwd_act(dxo, mo, g, yk, wb, pwo, sp):
    S, D = dxo.shape
    TM = min(512, S)

    def body(dxo_ref, mo_ref, g_ref, yk_ref, wb_ref, wo_ref, qg_ref, dmo_ref, dm_ref, dgp_ref, dyk_ref, gp_ref, dmo_s, acc):
        i = pl.program_id(0)
        j = pl.program_id(1)

        @pl.when((i == 0) & (j == 0))
        def _():
            gp_ref[...] = jnp.zeros_like(gp_ref)

        @pl.when(j == 0)
        def _():
            n, r = _rms(mo_ref[...].astype(F32))
            dmo, dg = _rms_bwd(n, r, qg_ref[...], dxo_ref[...])
            gp_ref[0:1, :] += dg
            dmoc = dmo.astype(CDT)
            dmo_s[...] = dmoc
            dmo_ref[...] = dmoc
            acc[...] = jnp.zeros_like(acc)

        dmg = _nt(dmo_s[...], wo_ref[...])
        for k in range(4):
            gk = g_ref[k].astype(F32)
            dyk = (dmg * gk).astype(CDT)
            dyk_ref[k] = dyk
            dgp_ref[k] = (dmg * yk_ref[k].astype(F32) * gk * (1.0 - gk)).astype(CDT)
            acc[k] += _nt(dyk, wb_ref[k])

        @pl.when(j == NS - 1)
        def _():
            dm_ref[...] = acc[...].astype(CDT)

    row = lambda i, j: (i, 0)
    colb = pl.BlockSpec((4, TM, GW), lambda i, j: (0, i, j))
    return _call(
        body, "merge_bwd_act", (S // TM, NS),
        [pl.BlockSpec((TM, D), row), pl.BlockSpec((TM, D), row), colb, colb,
         pl.BlockSpec((None, 4, MW, GW), lambda i, j: (j, 0, 0, 0)), pl.BlockSpec((None, GW, D), lambda i, j: (j, 0, 0)),
         _spspec("mix_post_g", D, _zero)],
        [pl.BlockSpec((TM, D), row), pl.BlockSpec((4, TM, MW), lambda i, j: (0, i, 0)), colb, colb,
         pl.BlockSpec((8, D), lambda i, j: (0, 0))],
        [SDS((S, D), CDT), SDS((4, S, MW), CDT), SDS((4, S, D), CDT), SDS((4, S, D), CDT), SDS((8, D), F32)],
        [pltpu.VMEM((TM, D), CDT), pltpu.VMEM((4, TM, MW), F32)])(dxo, mo, g, yk, wb, pwo, sp)


def _merge_bwd_w(ma, mb, mc, md, dyk, mg, dmo):
    S, D = dmo.shape
    TK = min(512, S)
    NK = S // TK

    def body(ma_ref, mb_ref, mc_ref, md_ref, dyk_ref, mg_ref, dmo_ref, gwb_ref, gwo_ref, accb, acco):
        k = pl.program_id(1)

        @pl.when(k == 0)
        def _():
            accb[...] = jnp.zeros_like(accb)
            acco[...] = jnp.zeros_like(acco)

        for b, m in enumerate((ma_ref, mb_ref, mc_ref, md_ref)):
            accb[b] += _tn(m[...], dyk_ref[b])
        acco[...] += _tn(mg_ref[...], dmo_ref[...])

        @pl.when(k == NK - 1)
        def _():
            gwb_ref[...] = accb[...].astype(CDT)
            gwo_ref[...] = acco[...].astype(CDT)

    rowm = pl.BlockSpec((TK, MW), lambda j, k: (k, 0))
    return _call(
        body, "merge_bwd_w", (NS, NK),
        [rowm, rowm, rowm, rowm, pl.BlockSpec((4, TK, GW), lambda j, k: (0, k, j)),
         pl.BlockSpec((TK, GW), lambda j, k: (k, j)), pl.BlockSpec((TK, D), lambda j, k: (k, 0))],
        [pl.BlockSpec((None, 4, MW, GW), lambda j, k: (j, 0, 0, 0)), pl.BlockSpec((None, GW, D), lambda j, k: (j, 0, 0))],
        [SDS((NS, 4, MW, GW), CDT), SDS((NS, GW, D), CDT)],
        [pltpu.VMEM((4, MW, GW), F32), pltpu.VMEM((GW, D), F32)])(ma, mb, mc, md, dyk, mg, dmo)


def _xa_kv(mem, sp, pxa):
    M, D = mem.shape

    def body(m_ref, g_ref, wk_ref, wv_ref, mn_ref, k_ref, v_ref):
        n, _ = _rms(m_ref[...])
        mn = (n * g_ref[...]).astype(CDT)
        mn_ref[...] = mn
        k_ref[...] = _nn(mn, wk_ref[...].reshape(D, D)).astype(CDT)
        v_ref[...] = _nn(mn, wv_ref[...].reshape(D, D)).astype(CDT)

    full = pl.BlockSpec((M, D), lambda i: (0, 0))
    return _call(
        body, "xa_kv", (1,),
        [full, _spspec("mem_g", D, _zero), pl.BlockSpec((NS, GW, D), lambda i: (0, 1, 0)),
         pl.BlockSpec((NS, GW, D), lambda i: (0, 2, 0))],
        [full, full, full], [SDS((M, D), CDT)] * 3)(mem, sp, pxa, pxa)


def _softmax(s):
    e = jnp.exp(s - jnp.max(s, axis=-1, keepdims=True))
    return e / jnp.sum(e, axis=-1, keepdims=True)


def _xa_fwd(x, kk, vv, sp, pxa):
    S, D = x.shape
    M = kk.shape[0]
    TM = min(512, S)
    HD = D // XA_HEADS
    scale = HD ** -0.5

    def body(x_ref, k_ref, v_ref, pg_ref, qg_ref, wq_ref, wo_ref, xo_ref, hb_ref, q_ref, o_ref, po_ref):
        n, _ = _rms(x_ref[...])
        hb = (n * pg_ref[...]).astype(CDT)
        hb_ref[...] = hb
        q = _nn(hb, wq_ref[...].reshape(D, D)).astype(CDT)
        q_ref[...] = q
        for h in range(XA_HEADS):
            hs = slice(h * HD, (h + 1) * HD)
            p = _softmax(_nt(q[:, hs], k_ref[:, hs]) * scale)
            o_ref[:, hs] = _nn(p.astype(CDT), v_ref[:, hs]).astype(CDT)
        po = _nn(o_ref[...], wo_ref[...].reshape(D, D))
        po_ref[...] = po.astype(CDT)
        n, _ = _rms(po)
        xo_ref[...] = x_ref[...] + n * qg_ref[...]

    row = pl.BlockSpec((TM, D), lambda i: (i, 0))
    full = pl.BlockSpec((M, D), lambda i: (0, 0))
    return _call(
        body, "xa_fwd", (S // TM,),
        [row, full, full, _spspec("xa_pre_g", D, _zero), _spspec("xa_post_g", D, _zero),
         pl.BlockSpec((NS, GW, D), lambda i: (0, 0, 0)), pl.BlockSpec((NS, GW, D), lambda i: (0, 3, 0))],
        [row] * 5, [SDS((S, D), F32)] + [SDS((S, D), CDT)] * 4)(x, kk, vv, sp, sp, pxa, pxa)


def _xa_bwd_act(dxo, x, po, q, kk, vv, sp, pxa):
    S, D = x.shape
    M = kk.shape[0]
    TM = min(512, S)
    HD = D // XA_HEADS
    scale = HD ** -0.5

    def body(dxo_ref, x_ref, po_ref, q_ref, k_ref, v_ref, pg_ref, qg_ref, wq_ref, wo_ref,
             dx_ref, dpo_ref, dq_ref, dk_ref, dv_ref, gp_ref):
        @pl.when(pl.program_id(0) == 0)
        def _():
            gp_ref[...] = jnp.zeros_like(gp_ref)
            dk_ref[...] = jnp.zeros_like(dk_ref)
            dv_ref[...] = jnp.zeros_like(dv_ref)

        n, r = _rms(po_ref[...].astype(F32))
        dpo, dg = _rms_bwd(n, r, qg_ref[...], dxo_ref[...])
        gp_ref[1:2, :] += dg
        dpoc = dpo.astype(CDT)
        dpo_ref[...] = dpoc
        do = _nt(dpoc, wo_ref[...].reshape(D, D)).astype(CDT)
        for h in range(XA_HEADS):
            hs = slice(h * HD, (h + 1) * HD)
            qh = q_ref[:, hs]
            p = _softmax(_nt(qh, k_ref[:, hs]) * scale)
            pc = p.astype(CDT)
            dv_ref[:, hs] += _tn(pc, do[:, hs])
            dp = _nt(do[:, hs], v_ref[:, hs])
            ds = (p * (dp - jnp.sum(p * dp, axis=-1, keepdims=True)) * scale).astype(CDT)
            dq_ref[:, hs] = _nn(ds, k_ref[:, hs]).astype(CDT)
            dk_ref[:, hs] += _tn(ds, qh)
        dhb = _nt(dq_ref[...], wq_ref[...].reshape(D, D))
        n, r = _rms(x_ref[...])
        dx, dg = _rms_bwd(n, r, pg_ref[...], dhb)
        dx_ref[...] = dxo_ref[...] + dx
        gp_ref[0:1, :] += dg

    row = pl.BlockSpec((TM, D), lambda i: (i, 0))
    full = pl.BlockSpec((M, D), lambda i: (0, 0))
    return _call(
        body, "xa_bwd_act", (S // TM,),
        [row, row, row, row, full, full, _spspec("xa_pre_g", D, _zero), _spspec("xa_post_g", D, _zero),
         pl.BlockSpec((NS, GW, D), lambda i: (0, 0, 0)), pl.BlockSpec((NS, GW, D), lambda i: (0, 3, 0))],
        [row, row, row, full, full, pl.BlockSpec((8, D), lambda i: (0, 0))],
        [SDS((S, D), F32), SDS((S, D), CDT), SDS((S, D), CDT), SDS((M, D), F32), SDS((M, D), F32), SDS((8, D), F32)],
    )(dxo, x, po, q, kk, vv, sp, sp, pxa, pxa)


def _xa_bwd_w(hb, dq, o, dpo, mn, dk, dv):
    S, D = hb.shape
    M = mn.shape[0]
    TK = min(512, S)
    NK = S // TK

    def body(hb_ref, dq_ref, o_ref, dpo_ref, mn_ref, dk_ref, dv_ref, g_ref, acc):
        k = pl.program_id(1)

        @pl.when(k == 0)
        def _():
            acc[...] = jnp.zeros_like(acc)

        acc[0:GW, :] += _tn(hb_ref[...], dq_ref[...])
        acc[3 * GW:4 * GW, :] += _tn(o_ref[...], dpo_ref[...])

        @pl.when(k == NK - 1)
        def _():
            acc[GW:2 * GW, :] = _tn(mn_ref[...], dk_ref[...].astype(CDT))
            acc[2 * GW:3 * GW, :] = _tn(mn_ref[...], dv_ref[...].astype(CDT))
            g_ref[...] = acc[...].astype(CDT)

    colb = pl.BlockSpec((TK, GW), lambda j, k: (k, j))
    rowb = pl.BlockSpec((TK, D), lambda j, k: (k, 0))
    full = pl.BlockSpec((M, D), lambda j, k: (0, 0))
    return _call(
        body, "xa_bwd_w", (NS, NK),
        [colb, rowb, colb, rowb, pl.BlockSpec((M, GW), lambda j, k: (0, j)), full, full],
        pl.BlockSpec((None, 4 * GW, D), lambda j, k: (j, 0, 0)), SDS((NS, 4 * GW, D), CDT),
        [pltpu.VMEM((4 * GW, D), F32)])(hb, dq, o, dpo, mn, dk, dv)


def _xa_kv_bwd(mem, dk, dv, sp, pxa):
    M, D = mem.shape

    def body(m_ref, dk_ref, dv_ref, wk_ref, wv_ref, gp_ref):
        dmn = _nt(dk_ref[...].astype(CDT), wk_ref[...].reshape(D, D)) + _nt(dv_ref[...].astype(CDT), wv_ref[...].reshape(D, D))
        n, _ = _rms(m_ref[...])
        gp_ref[...] = jnp.zeros_like(gp_ref)
        gp_ref[0:1, :] = jnp.sum(dmn * n, axis=0, keepdims=True)

    full = pl.BlockSpec((M, D), lambda i: (0, 0))
    return _call(
        body, "xa_kv_bwd", (1,),
        [full, full, full, pl.BlockSpec((NS, GW, D), lambda i: (0, 1, 0)), pl.BlockSpec((NS, GW, D), lambda i: (0, 2, 0))],
        pl.BlockSpec((8, D), lambda i: (0, 0)), SDS((8, D), F32))(mem, dk, dv, pxa, pxa)


def _loss_head(y, t):
    S, D = y.shape
    TM = min(512, S)

    def body(y_ref, t_ref, dy_ref, l_ref):
        @pl.when(pl.program_id(0) == 0)
        def _():
            l_ref[...] = jnp.zeros_like(l_ref)

        e = y_ref[...] - t_ref[...]
        dy_ref[...] = e * (1.0 / D)
        l_ref[...] += 0.5 * jnp.sum(jnp.mean(e * e, axis=-1, keepdims=True), axis=0, keepdims=True)

    row = pl.BlockSpec((TM, D), lambda i: (i, 0))
    return _call(body, "loss_head", (S // TM,), [row, row], [row, pl.BlockSpec((8, 128), lambda i: (0, 0))],
                 [SDS((S, D), F32), SDS((8, 128), F32)])(y, t)


def _row_tile(rows, cols, limit=1 << 18):
    if rows * cols <= limit or rows % 8:
        return rows
    best = 8
    for t in range(8, rows + 1, 8):
        if rows % t == 0 and t * cols <= limit:
            best = t
    return best


def _slot_sum(r):
    _, R, C = r.shape
    TR = _row_tile(R, C * NS)

    def body(r_ref, o_ref):
        acc = r_ref[0].astype(F32)
        for j in range(1, NS):
            acc = acc + r_ref[j].astype(F32)
        o_ref[...] = acc

    return _call(body, "slot_sum", (R // TR,), [pl.BlockSpec((NS, TR, C), lambda i: (0, i, 0))],
                 pl.BlockSpec((TR, C), lambda i: (i, 0)), SDS((R, C), F32))(r)


def _adamw(w, g, m, v):
    shape = w.shape
    C = shape[-1]
    R = w.size // C
    TR = _row_tile(R, C)
    c1 = 1.0 - ADAM_B1 ** ADAM_STEP
    c2 = 1.0 - ADAM_B2 ** ADAM_STEP

    def body(w_ref, g_ref, m_ref, v_ref, d_ref, nm_ref, nv_ref):
        gg = g_ref[...]
        nm = ADAM_B1 * m_ref[...] + (1.0 - ADAM_B1) * gg
        nv = ADAM_B2 * v_ref[...] + (1.0 - ADAM_B2) * (gg * gg)
        nm_ref[...] = nm
        nv_ref[...] = nv
        d_ref[...] = -ADAM_LR * ((nm / c1) / (jnp.sqrt(nv / c2) + ADAM_EPS) + ADAM_WD * w_ref[...])

    blk = pl.BlockSpec((TR, C), lambda i: (i, 0))
    outs = _call(body, "adamw", (R // TR,), [blk] * 4, [blk] * 3, [SDS((R, C), F32)] * 3)(
        w.reshape(R, C), g.reshape(R, C), m.reshape(R, C), v.reshape(R, C))
    return tuple(o.reshape(shape) for o in outs)


def _exchange(arrs, scatter, name):
    n = len(arrs)
    np_ = NS - 1

    def body(*refs):
        ins, outs = refs[:n], refs[n:2 * n]
        send_sems, recv_sems, loc_sems = refs[2 * n:]
        x, y, c = lax.axis_index("x"), lax.axis_index("y"), lax.axis_index("c")
        me = 4 * x + 2 * y + c
        peers = []
        for f in range(1, NS):
            px = 1 - x if f & 4 else x
            py = 1 - y if f & 2 else y
            pc = 1 - c if f & 1 else c
            peers.append(((px, py, pc), 4 * px + 2 * py + pc))

        def src(a, pid):
            return ins[a].at[pid] if scatter else ins[a]

        local = [pltpu.make_async_copy(src(a, me), outs[a].at[me], loc_sems.at[a]) for a in range(n)]
        for cp in local:
            cp.start()
        sends = []
        for a in range(n):
            for f, (dev, pid) in enumerate(peers):
                sends.append(pltpu.make_async_remote_copy(
                    src_ref=src(a, pid), dst_ref=outs[a].at[me], send_sem=send_sems.at[a * np_ + f],
                    recv_sem=recv_sems.at[a * np_ + f], device_id=dev, device_id_type=pl.DeviceIdType.MESH))
        for cp in sends:
            cp.start()
        for a in range(n):
            for f, (dev, pid) in enumerate(peers):
                pltpu.make_async_remote_copy(
                    src_ref=src(a, pid), dst_ref=outs[a].at[pid], send_sem=send_sems.at[a * np_ + f],
                    recv_sem=recv_sems.at[a * np_ + f], device_id=dev, device_id_type=pl.DeviceIdType.MESH).wait_recv()
        for cp in sends:
            cp.wait_send()
        for cp in local:
            cp.wait()

    out_shape = [SDS(a.shape if scatter else (NS,) + a.shape, a.dtype) for a in arrs]
    anyspec = pl.BlockSpec(memory_space=pl.ANY)
    outs = pl.pallas_call(
        body, name=name, in_specs=[anyspec] * n, out_specs=[anyspec] * n, out_shape=out_shape,
        scratch_shapes=[pltpu.SemaphoreType.DMA((n * np_,)), pltpu.SemaphoreType.DMA((n * np_,)),
                        pltpu.SemaphoreType.DMA((n,))],
        compiler_params=pltpu.CompilerParams(has_side_effects=True))(*arrs)
    return list(outs)


_W_NAMES = ("ffn1_pre_g", "ffn1_post_g", "ffn1_w1", "ffn1_w3", "ffn1_w2", "mix_pre_g", "mix_post_g", "w_in", "conv_a_w",
            "conv_a_b", "pool_w", "pool_scale", "sgu_ln_g", "sgu_ln_b", "sgu_ws", "sgu_b", "conv_d_w", "conv_d_b",
            "conv_d_ln_g", "conv_d_ln_b", "w_branch", "w_gate", "b_gate", "w_o", "xa_pre_g", "xa_post_g", "mem_g",
            "xa_wq", "xa_wk", "xa_wv", "xa_wo", "ffn2_pre_g", "ffn2_post_g", "ffn2_w1", "ffn2_w3", "ffn2_w2")
_REP_NAMES = tuple(n for n, _ in _SP_NAMES) + ("pool_w", "sgu_ws", "sgu_b", "conv_a_w", "conv_d_w")


def _t(w):
    return jnp.swapaxes(w, -1, -2)


def _step(x, mem, loss_target, W, M, V):
    L = W["w_in"].shape[0]
    S, D = x.shape[1], x.shape[2]
    x0 = x.reshape(S, D)
    memf = mem.reshape(mem.shape[1], D)
    me = 4 * lax.axis_index("x") + 2 * lax.axis_index("y") + lax.axis_index("c")
    FS = W["ffn1_w2"].shape[1]
    KA, KD = W["conv_a_w"].shape[1], W["conv_d_w"].shape[1]
    CS = W["conv_a_w"].shape[2]

    pf1 = jnp.concatenate([_t(W["ffn1_w1"]), _t(W["ffn1_w3"]), W["ffn1_w2"]], axis=1).astype(CDT)
    pf2 = jnp.concatenate([_t(W["ffn2_w1"]), _t(W["ffn2_w3"]), W["ffn2_w2"]], axis=1).astype(CDT)
    pma = jnp.concatenate([_t(W["w_in"]), _t(W["w_gate"])], axis=1).astype(CDT)
    pwo = W["w_o"].astype(CDT)
    pxa = jnp.concatenate([W["xa_wq"], W["xa_wk"], W["xa_wv"], W["xa_wo"]], axis=1).astype(CDT)
    wbs = W["w_branch"].astype(CDT)
    cws = jnp.concatenate([W["conv_a_w"], W["conv_d_w"]], axis=1).reshape(-1, 128)
    sp_all = jnp.concatenate([W[n] for n, _ in _SP_NAMES], axis=1)
    bsc_all = W["sgu_b"][..., None]

    (cwg,) = _exchange([cws], False, "gather_conv_w")
    cwf = cwg.reshape(NS, L, KA + KD, CS).transpose(1, 2, 0, 3).reshape(L, KA + KD, NS * CS)

    packs = []
    for l in range(L):
        packs.append(_exchange([pf1[l], pf2[l], pma[l], pwo[l], pxa[l], wbs[l]], False, "gather_weights"))

    saved = []
    xc = x0
    for l in range(L):
        gf1, gf2, gma, gwo, gxa, gwb = packs[l]
        sp = sp_all[l:l + 1]
        cwa, cwd = cwf[l, :KA], cwf[l, KA:]
        wp, ws, bsc = W["pool_w"][l], W["sgu_ws"][l], bsc_all[l]
        s = {"x0": xc}
        xc, s["hb1"], s["a1"], s["b1"], s["y1"] = _ffn_fwd(xc, sp, "ffn1_pre_g", "ffn1_post_g", gf1)
        s["x1"] = xc
        s["hbm"], s["z"], s["g"] = _mix_in(xc, sp, gma)
        s["ma"] = _mixA_fwd(s["z"], cwa, sp)
        s["mb"] = _mixB_fwd(s["z"], wp, sp)
        s["mc"] = _mixC_fwd(s["z"], ws, bsc, sp)
        s["yd"] = _mixD_conv_fwd(s["z"], cwd, sp)
        xc, s["md"], s["yk"], s["mg"], s["mo"] = _merge_fwd(s["ma"], s["mb"], s["mc"], s["yd"], s["g"], gwb, gwo, xc, sp)
        s["x2"] = xc
        s["mn"], s["k"], s["v"] = _xa_kv(memf, sp, gxa)
        xc, s["hbx"], s["q"], s["o"], s["po"] = _xa_fwd(xc, s["k"], s["v"], sp, gxa)
        s["x3"] = xc
        xc, s["hb2"], s["a2"], s["b2"], s["y2"] = _ffn_fwd(xc, sp, "ffn2_pre_g", "ffn2_post_g", gf2)
        saved.append(s)

    dx, lpart = _loss_head(xc, loss_target.reshape(S, D))
    loss = lax.psum(lpart[0, 0], ("x", "y", "c"))

    rep = {n: [None] * L for n in _REP_NAMES}
    big = []
    for l in reversed(range(L)):
        gf1, gf2, gma, gwo, gxa, gwb = packs[l]
        sp = sp_all[l:l + 1]
        cwa, cwd = cwf[l, :KA], cwf[l, KA:]
        wp, ws, bsc = W["pool_w"][l], W["sgu_ws"][l], bsc_all[l]
        s = saved[l]

        dx, dyb, da, db, gp = _ffn_bwd_act(dx, s["x3"], s["y2"], s["a2"], s["b2"], sp, "ffn2_pre_g", "ffn2_post_g", gf2)
        rep["ffn2_pre_g"][l], rep["ffn2_post_g"][l] = gp[0], gp[1]
        d_f2 = _ffn_bwd_w(s["hb2"], dyb, s["a2"], s["b2"], da, db)

        dx, dpo, dq, dk, dv, gp = _xa_bwd_act(dx, s["x2"], s["po"], s["q"], s["k"], s["v"], sp, gxa)
        rep["xa_pre_g"][l], rep["xa_post_g"][l] = gp[0], gp[1]
        d_xa = _xa_bwd_w(s["hbx"], dq, s["o"], dpo, s["mn"], dk, dv)
        rep["mem_g"][l] = _xa_kv_bwd(memf, dk, dv, sp, gxa)[0]

        dmo, dm, dgp, dyk, gp = _merge_bwd_act(dx, s["mo"], s["g"], s["yk"], gwb, gwo, sp)
        rep["mix_post_g"][l] = gp[0]
        d_wb, d_wo = _merge_bwd_w(s["ma"], s["mb"], s["mc"], s["md"], dyk, s["mg"], dmo)
        dza, dcw, gp = _mixA_bwd(s["z"], dm[0], cwa, sp)
        rep["conv_a_w"][l], rep["conv_a_b"][l] = dcw, gp[0]
        dzb, dwp, gp = _mixB_bwd(s["z"], dm[1], wp, sp)
        rep["pool_w"][l], rep["pool_scale"][l] = dwp, gp[0]
        dzc, dws, dbs, gp = _mixC_bwd(s["z"], dm[2], ws, bsc, sp)
        rep["sgu_ws"][l], rep["sgu_b"][l], rep["sgu_ln_g"][l], rep["sgu_ln_b"][l] = dws, dbs[:, :, 0], gp[0], gp[1]
        dyd, gp = _mixD_ln_bwd(dm[3], s["yd"], sp)
        rep["conv_d_ln_g"][l], rep["conv_d_ln_b"][l] = gp[0], gp[1]
        dzd, dcw, gp = _mixD_conv_bwd(s["z"], dyd, cwd)
        rep["conv_d_w"][l], rep["conv_d_b"][l] = dcw, gp[0]
        dz = jnp.concatenate([dza, dzb, dzc, dzd], axis=0)
        dx, gp = _mix_in_bwd_act(dz, dgp, dx, s["x1"], sp, gma)
        rep["mix_pre_g"][l] = gp[0]
        d_ma, dbg = _mix_in_bwd_w(dz, dgp, s["hbm"])
        rep["b_gate"][l] = dbg[:, 0, :].reshape(-1)

        dx, dyb, da, db, gp = _ffn_bwd_act(dx, s["x0"], s["y1"], s["a1"], s["b1"], sp, "ffn1_pre_g", "ffn1_post_g", gf1)
        rep["ffn1_pre_g"][l], rep["ffn1_post_g"][l] = gp[0], gp[1]
        d_f1 = _ffn_bwd_w(s["hb1"], dyb, s["a1"], s["b1"], da, db)

        recv = _exchange([d_f1, d_f2, d_ma, d_wo, d_xa, d_wb], True, "scatter_grads")
        big.append([_slot_sum(r.reshape(NS, -1, r.shape[-1])) for r in recv])
    big = big[::-1]

    G = {}
    stk = lambda i: jnp.stack([big[l][i] for l in range(L)])
    f1, f2, ma_, wo_, xa_, wb_ = (stk(i) for i in range(6))
    for nm, f in (("ffn1", f1), ("ffn2", f2)):
        G[nm + "_w1"], G[nm + "_w3"], G[nm + "_w2"] = _t(f[:, :FS]), _t(f[:, FS:2 * FS]), f[:, 2 * FS:]
    G["w_in"], G["w_gate"] = _t(ma_[:, :MW]), _t(ma_[:, MW:])
    G["w_o"] = wo_
    G["xa_wq"], G["xa_wk"], G["xa_wv"], G["xa_wo"] = (xa_[:, i * GW:(i + 1) * GW] for i in range(4))
    G["w_branch"] = wb_.reshape(W["w_branch"].shape)

    flat = jnp.concatenate([jnp.stack(rep[n]).reshape(-1) for n in _REP_NAMES])
    npad = -flat.size % 1024
    flat = jnp.pad(flat, (0, npad)).reshape(-1, 128)
    (allp,) = _exchange([flat], False, "gather_small_grads")
    tot = _slot_sum(allp).reshape(-1)
    off = 0
    for n in _REP_NAMES:
        shape = (L, KA, NS * CS) if n == "conv_a_w" else (L, KD, NS * CS) if n == "conv_d_w" else W[n].shape
        size = 1
        for d in shape:
            size *= d
        G[n] = tot[off:off + size].reshape(shape)
        off += size
    for n in ("conv_a_w", "conv_d_w"):
        G[n] = lax.dynamic_slice_in_dim(G[n], me * CS, CS, axis=2)

    deltas, new_m, new_v = {}, {}, {}
    for n in _W_NAMES:
        deltas[n], new_m[n], new_v[n] = _adamw(W[n], G[n], M[n], V[n])
    grad_x = dx.reshape(x.shape)
    return (loss, grad_x, *[G[n] for n in _W_NAMES], *[deltas[n] for n in _W_NAMES],
            *[new_m[n] for n in _W_NAMES], *[new_v[n] for n in _W_NAMES])


def kernel(x, mem, ffn1_pre_g, ffn1_post_g, ffn1_w1, ffn1_w3, ffn1_w2, mix_pre_g, mix_post_g, w_in, conv_a_w, conv_a_b, pool_w, pool_scale, sgu_ln_g, sgu_ln_b, sgu_ws, sgu_b, conv_d_w, conv_d_b, conv_d_ln_g, conv_d_ln_b, w_branch, w_gate, b_gate, w_o, xa_pre_g, xa_post_g, mem_g, xa_wq, xa_wk, xa_wv, xa_wo, ffn2_pre_g, ffn2_post_g, ffn2_w1, ffn2_w3, ffn2_w2, loss_target, m_ffn1_pre_g, m_ffn1_post_g, m_ffn1_w1, m_ffn1_w3, m_ffn1_w2, m_mix_pre_g, m_mix_post_g, m_w_in, m_conv_a_w, m_conv_a_b, m_pool_w, m_pool_scale, m_sgu_ln_g, m_sgu_ln_b, m_sgu_ws, m_sgu_b, m_conv_d_w, m_conv_d_b, m_conv_d_ln_g, m_conv_d_ln_b, m_w_branch, m_w_gate, m_b_gate, m_w_o, m_xa_pre_g, m_xa_post_g, m_mem_g, m_xa_wq, m_xa_wk, m_xa_wv, m_xa_wo, m_ffn2_pre_g, m_ffn2_post_g, m_ffn2_w1, m_ffn2_w3, m_ffn2_w2, v_ffn1_pre_g, v_ffn1_post_g, v_ffn1_w1, v_ffn1_w3, v_ffn1_w2, v_mix_pre_g, v_mix_post_g, v_w_in, v_conv_a_w, v_conv_a_b, v_pool_w, v_pool_scale, v_sgu_ln_g, v_sgu_ln_b, v_sgu_ws, v_sgu_b, v_conv_d_w, v_conv_d_b, v_conv_d_ln_g, v_conv_d_ln_b, v_w_branch, v_w_gate, v_b_gate, v_w_o, v_xa_pre_g, v_xa_post_g, v_mem_g, v_xa_wq, v_xa_wk, v_xa_wv, v_xa_wo, v_ffn2_pre_g, v_ffn2_post_g, v_ffn2_w1, v_ffn2_w3, v_ffn2_w2):
    args = dict(locals())
    W = {n: args[n] for n in _W_NAMES}
    M = {n: args["m_" + n] for n in _W_NAMES}
    V = {n: args["v_" + n] for n in _W_NAMES}
    return _step(x, mem, loss_target, W, M, V)
```

```python
import jax
import jax.numpy as jnp
from jax import lax
from jax.experimental import pallas as pl
from jax.experimental.pallas import tpu as pltpu

F32 = jnp.float32
CDT = jnp.bfloat16
EPS = 1e-6
NS = 8
GW = 128
MW = 512
CHUNK = 64
XA_HEADS = 4
POOL_WINDOWS = (2, 4, 8, 16)
VMEM_LIMIT = 56 * 1024 * 1024
ADAM_LR, ADAM_B1, ADAM_B2, ADAM_EPS, ADAM_WD, ADAM_STEP = 0.001, 0.9, 0.999, 1e-08, 0.01, 10

SDS = jax.ShapeDtypeStruct

_SP_NAMES = (("ffn1_pre_g", 1024), ("ffn1_post_g", 1024), ("mix_pre_g", 1024), ("mix_post_g", 1024),
             ("xa_pre_g", 1024), ("xa_post_g", 1024), ("mem_g", 1024), ("ffn2_pre_g", 1024), ("ffn2_post_g", 1024),
             ("conv_a_b", 512), ("pool_scale", 512), ("sgu_ln_g", 512), ("sgu_ln_b", 512), ("conv_d_b", 512),
             ("conv_d_ln_g", 512), ("conv_d_ln_b", 512), ("b_gate", 4096))
_SP = {}
_off = 0
for _n, _w in _SP_NAMES:
    _SP[_n] = (_off, _w)
    _off += _w
_SP_TOTAL = _off


def _call(body, name, grid, in_specs, out_specs, out_shape, scratch=()):
    return pl.pallas_call(
        body, name=name, grid=grid, in_specs=in_specs, out_specs=out_specs, out_shape=out_shape,
        scratch_shapes=list(scratch),
        compiler_params=pltpu.CompilerParams(dimension_semantics=("arbitrary",) * len(grid),
                                             vmem_limit_bytes=VMEM_LIMIT))


def _nn(a, b):
    return lax.dot_general(a, b, (((1,), (0,)), ((), ())), preferred_element_type=F32)


def _nt(a, b):
    return lax.dot_general(a, b, (((1,), (1,)), ((), ())), preferred_element_type=F32)


def _tn(a, b):
    return lax.dot_general(a, b, (((0,), (0,)), ((), ())), preferred_element_type=F32)


def _rms(x):
    r = lax.rsqrt(jnp.mean(x * x, axis=-1, keepdims=True) + EPS)
    return x * r, r


def _rms_bwd(n, r, g, dout):
    dn = dout * g
    dx = r * (dn - n * jnp.mean(dn * n, axis=-1, keepdims=True))
    return dx, jnp.sum(dout * n, axis=0, keepdims=True)


def _ln(y):
    mu = jnp.mean(y, axis=-1, keepdims=True)
    yc = y - mu
    rs = lax.rsqrt(jnp.mean(yc * yc, axis=-1, keepdims=True) + EPS)
    return yc * rs, rs


def _ln_bwd(xh, rs, dxh):
    return rs * (dxh - jnp.mean(dxh, axis=-1, keepdims=True) - xh * jnp.mean(dxh * xh, axis=-1, keepdims=True))


def _silu_parts(a):
    s = jax.nn.sigmoid(a)
    sl = a * s
    return sl, s + sl * (1.0 - s)


_GELU_C = 0.7978845608028654
_GELU_A = 0.044715


def _gelu(x):
    return 0.5 * x * (1.0 + jnp.tanh(_GELU_C * (x + _GELU_A * x * x * x)))


def _gelu_parts(x):
    t = jnp.tanh(_GELU_C * (x + _GELU_A * x * x * x))
    g = 0.5 * x * (1.0 + t)
    dg = 0.5 * (1.0 + t) + 0.5 * x * (1.0 - t * t) * _GELU_C * (1.0 + 3.0 * _GELU_A * x * x)
    return g, dg


def _spspec(name, width, imap):
    off = _SP[name][0]
    assert off % width == 0
    return pl.BlockSpec((1, width), lambda *a: (0, off // width + imap(*a)))


def _zero(*a):
    return 0


def _ffn_fwd(x, sp, pre, post, pf, dep):
    S, D = x.shape
    FS = pf.shape[1] // 3
    TM = min(512, S)

    def body(x_ref, pg_ref, qg_ref, w1_ref, w3_ref, w2_ref, dep_ref, xo_ref, hb_ref, a_ref, b_ref, y_ref, hb_s, acc):
        j = pl.program_id(1)

        @pl.when(j == 0)
        def _():
            n, _ = _rms(x_ref[...])
            hb = (n * pg_ref[...]).astype(CDT)
            hb_s[...] = hb
            hb_ref[...] = hb
            acc[...] = jnp.zeros_like(acc)

        hb = hb_s[...]
        a = _nt(hb, w1_ref[...])
        b = _nt(hb, w3_ref[...])
        a_ref[...] = a.astype(CDT)
        b_ref[...] = b.astype(CDT)
        u = (a * jax.nn.sigmoid(a) * b).astype(CDT)
        acc[...] += _nn(u, w2_ref[...])

        @pl.when(j == NS - 1)
        def _():
            y = acc[...]
            y_ref[...] = y.astype(CDT)
            n, _ = _rms(y)
            xo_ref[...] = x_ref[...] + 0.5 * (n * qg_ref[...])

    row = lambda i, j: (i, 0)
    slot = lambda i, j: (j, i, 0)
    return _call(
        body, "ffn_fwd", (S // TM, NS),
        [pl.BlockSpec((TM, D), row), _spspec(pre, D, _zero), _spspec(post, D, _zero),
         pl.BlockSpec((None, FS, D), lambda i, j: (j, 0, 0)), pl.BlockSpec((None, FS, D), lambda i, j: (j, 1, 0)),
         pl.BlockSpec((None, FS, D), lambda i, j: (j, 2, 0)), pl.BlockSpec(memory_space=pl.ANY)],
        [pl.BlockSpec((TM, D), row), pl.BlockSpec((TM, D), row), pl.BlockSpec((None, TM, FS), slot),
         pl.BlockSpec((None, TM, FS), slot), pl.BlockSpec((TM, D), row)],
        [SDS((S, D), F32), SDS((S, D), CDT), SDS((NS, S, FS), CDT), SDS((NS, S, FS), CDT), SDS((S, D), CDT)],
        [pltpu.VMEM((TM, D), CDT), pltpu.VMEM((TM, D), F32)])(x, sp, sp, pf, pf, pf, dep)


def _ffn_bwd_act(dxo, x, y, a, b, sp, pre, post, pf, dep):
    S, D = x.shape
    FS = pf.shape[1] // 3
    TM = min(512, S)

    def body(dxo_ref, x_ref, y_ref, a_ref, b_ref, pg_ref, qg_ref, w1_ref, w3_ref, w2_ref, dep_ref,
             dx_ref, dyb_ref, da_ref, db_ref, gp_ref, dyb_s, acc):
        i = pl.program_id(0)
        j = pl.program_id(1)

        @pl.when((i == 0) & (j == 0))
        def _():
            gp_ref[...] = jnp.zeros_like(gp_ref)

        @pl.when(j == 0)
        def _():
            n, r = _rms(y_ref[...].astype(F32))
            dy, dg = _rms_bwd(n, r, qg_ref[...], 0.5 * dxo_ref[...])
            dyb = dy.astype(CDT)
            dyb_s[...] = dyb
            dyb_ref[...] = dyb
            gp_ref[1:2, :] += dg
            acc[...] = jnp.zeros_like(acc)

        sl, dsl = _silu_parts(a_ref[...].astype(F32))
        du = _nt(dyb_s[...], w2_ref[...])
        db = (du * sl).astype(CDT)
        da = (du * b_ref[...].astype(F32) * dsl).astype(CDT)
        da_ref[...] = da
        db_ref[...] = db
        acc[...] += _nn(da, w1_ref[...]) + _nn(db, w3_ref[...])

        @pl.when(j == NS - 1)
        def _():
            n, r = _rms(x_ref[...])
            dx, dg = _rms_bwd(n, r, pg_ref[...], acc[...])
            dx_ref[...] = dxo_ref[...] + dx
            gp_ref[0:1, :] += dg

    row = lambda i, j: (i, 0)
    slot = lambda i, j: (j, i, 0)
    return _call(
        body, "ffn_bwd_act", (S // TM, NS),
        [pl.BlockSpec((TM, D), row), pl.BlockSpec((TM, D), row), pl.BlockSpec((TM, D), row),
         pl.BlockSpec((None, TM, FS), slot), pl.BlockSpec((None, TM, FS), slot),
         _spspec(pre, D, _zero), _spspec(post, D, _zero),
         pl.BlockSpec((None, FS, D), lambda i, j: (j, 0, 0)), pl.BlockSpec((None, FS, D), lambda i, j: (j, 1, 0)),
         pl.BlockSpec((None, FS, D), lambda i, j: (j, 2, 0)), pl.BlockSpec(memory_space=pl.ANY)],
        [pl.BlockSpec((TM, D), row), pl.BlockSpec((TM, D), row), pl.BlockSpec((None, TM, FS), slot),
         pl.BlockSpec((None, TM, FS), slot), pl.BlockSpec((8, D), lambda i, j: (0, 0))],
        [SDS((S, D), F32), SDS((S, D), CDT), SDS((NS, S, FS), CDT), SDS((NS, S, FS), CDT), SDS((8, D), F32)],
        [pltpu.VMEM((TM, D), CDT), pltpu.VMEM((TM, D), F32)])(dxo, x, y, a, b, sp, sp, pf, pf, pf, dep)


def _ffn_bwd_w(hb, dyb, a, b, da, db):
    S, D = hb.shape
    FS = a.shape[2]
    TK = min(512, S)
    NK = S // TK

    def body(hb_ref, dyb_ref, a_ref, b_ref, da_ref, db_ref, g_ref, acc):
        k = pl.program_id(1)

        @pl.when(k == 0)
        def _():
            acc[...] = jnp.zeros_like(acc)

        af = a_ref[...].astype(F32)
        u = (af * jax.nn.sigmoid(af) * b_ref[...].astype(F32)).astype(CDT)
        hb = hb_ref[...]
        acc[0:FS, :] += _tn(da_ref[...], hb)
        acc[FS:2 * FS, :] += _tn(db_ref[...], hb)
        acc[2 * FS:3 * FS, :] += _tn(u, dyb_ref[...])

        @pl.when(k == NK - 1)
        def _():
            g_ref[...] = acc[...].astype(CDT)

    row = lambda j, k: (k, 0)
    slot = lambda j, k: (j, k, 0)
    return _call(
        body, "ffn_bwd_w", (NS, NK),
        [pl.BlockSpec((TK, D), row), pl.BlockSpec((TK, D), row)] + [pl.BlockSpec((None, TK, FS), slot)] * 4,
        pl.BlockSpec((None, 3 * FS, D), lambda j, k: (j, 0, 0)),
        SDS((NS, 3 * FS, D), CDT),
        [pltpu.VMEM((3 * FS, D), F32)])(hb, dyb, a, b, da, db)


def _mix_in(x, sp, pma):
    S, D = x.shape
    TM = min(512, S)

    def body(x_ref, pg_ref, bg_ref, wi_ref, wg_ref, hb_ref, z_ref, g_ref, hb_s):
        @pl.when(pl.program_id(1) == 0)
        def _():
            n, _ = _rms(x_ref[...])
            hb = (n * pg_ref[...]).astype(CDT)
            hb_s[...] = hb
            hb_ref[...] = hb

        hb = hb_s[...]
        z_ref[...] = _nt(hb, wi_ref[...]).astype(CDT)
        g_ref[...] = jax.nn.sigmoid(_nt(hb, wg_ref[...]) + bg_ref[...]).astype(CDT)

    row = lambda i, j: (i, 0)
    return _call(
        body, "mix_in", (S // TM, NS),
        [pl.BlockSpec((TM, D), row), _spspec("mix_pre_g", D, _zero), _spspec("b_gate", MW, lambda i, j: j),
         pl.BlockSpec((None, MW, D), lambda i, j: (j, 0, 0)), pl.BlockSpec((None, MW, D), lambda i, j: (j, 1, 0))],
        [pl.BlockSpec((TM, D), row), pl.BlockSpec((None, TM, MW), lambda i, j: (j, i, 0)),
         pl.BlockSpec((None, TM, MW), lambda i, j: (j // 2, i, j % 2))],
        [SDS((S, D), CDT), SDS((NS, S, MW), CDT), SDS((4, S, D), CDT)],
        [pltpu.VMEM((TM, D), CDT)])(x, sp, sp, pma, pma)


def _mix_in_bwd_act(dz, dgp, dxr, x, sp, pma):
    S, D = x.shape
    TM = min(512, S)

    def body(dz_ref, dg_ref, dxr_ref, x_ref, pg_ref, wi_ref, wg_ref, dx_ref, gp_ref, acc):
        i = pl.program_id(0)
        j = pl.program_id(1)

        @pl.when((i == 0) & (j == 0))
        def _():
            gp_ref[...] = jnp.zeros_like(gp_ref)

        @pl.when(j == 0)
        def _():
            acc[...] = jnp.zeros_like(acc)

        acc[...] += _nn(dz_ref[...], wi_ref[...]) + _nn(dg_ref[...], wg_ref[...])

        @pl.when(j == NS - 1)
        def _():
            n, r = _rms(x_ref[...])
            dx, dg = _rms_bwd(n, r, pg_ref[...], acc[...])
            dx_ref[...] = dxr_ref[...] + dx
            gp_ref[0:1, :] += dg

    row = lambda i, j: (i, 0)
    return _call(
        body, "mix_in_bwd_act", (S // TM, NS),
        [pl.BlockSpec((None, TM, MW), lambda i, j: (j, i, 0)), pl.BlockSpec((None, TM, MW), lambda i, j: (j // 2, i, j % 2)),
         pl.BlockSpec((TM, D), row), pl.BlockSpec((TM, D), row), _spspec("mix_pre_g", D, _zero),
         pl.BlockSpec((None, MW, D), lambda i, j: (j, 0, 0)), pl.BlockSpec((None, MW, D), lambda i, j: (j, 1, 0))],
        [pl.BlockSpec((TM, D), row), pl.BlockSpec((8, D), lambda i, j: (0, 0))],
        [SDS((S, D), F32), SDS((8, D), F32)],
        [pltpu.VMEM((TM, D), F32)])(dz, dgp, dxr, x, sp, pma, pma)


def _mix_in_bwd_w(dz, dgp, hb):
    S, D = hb.shape
    TK = min(512, S)
    NK = S // TK

    def body(dz_ref, dg_ref, hb_ref, g_ref, bg_ref, acc):
        k = pl.program_id(1)

        @pl.when(k == 0)
        def _():
            acc[...] = jnp.zeros_like(acc)
            bg_ref[...] = jnp.zeros_like(bg_ref)

        hb = hb_ref[...]
        dg = dg_ref[...]
        acc[0:MW, :] += _tn(dz_ref[...], hb)
        acc[MW:2 * MW, :] += _tn(dg, hb)
        bg_ref[0:1, :] += jnp.sum(dg.astype(F32), axis=0, keepdims=True)

        @pl.when(k == NK - 1)
        def _():
            g_ref[...] = acc[...].astype(CDT)

    return _call(
        body, "mix_in_bwd_w", (NS, NK),
        [pl.BlockSpec((None, TK, MW), lambda j, k: (j, k, 0)), pl.BlockSpec((None, TK, MW), lambda j, k: (j // 2, k, j % 2)),
         pl.BlockSpec((TK, D), lambda j, k: (k, 0))],
        [pl.BlockSpec((None, 2 * MW, D), lambda j, k: (j, 0, 0)), pl.BlockSpec((None, 8, MW), lambda j, k: (j, 0, 0))],
        [SDS((NS, 2 * MW, D), CDT), SDS((NS, 8, MW), F32)],
        [pltpu.VMEM((2 * MW, D), F32)])(dz, dgp, hb)


def _causal_taps(pad_ref, i, ch, halo, k_taps, lanes=slice(None)):
    val = pad_ref[pl.ds(pl.multiple_of(i * ch, 8), ch + halo), lanes]
    out = []
    for k in range(k_taps):
        s = k_taps - 1 - k
        out.append((k, (pltpu.roll(val, s, 0) if s else val)[halo:, :]))
    return out


def _anti_taps(pad_ref, i, ch, halo, k_taps, lanes=slice(None)):
    val = pad_ref[pl.ds(pl.multiple_of(i * ch, 8), ch + halo), lanes]
    n = ch + halo
    out = []
    for k in range(k_taps):
        s = k_taps - 1 - k
        out.append((k, (pltpu.roll(val, n - s, 0) if s else val)[:ch, :]))
    return out


def _conv_geometry(S, k_taps):
    halo = 8 * ((k_taps - 1 + 7) // 8)
    ch = min(256, S)
    return halo, ch, S // ch


def _rows(i, ch):
    return pl.ds(pl.multiple_of(i * ch, ch), ch)


def _mixA_fwd(z, cw, sp):
    S = z.shape[1]
    K = cw.shape[0]
    H, CH, NCH = _conv_geometry(S, K)

    def body(z_ref, w_ref, b_ref, o_ref, pad):
        pad[0:H, :] = jnp.zeros((H, GW), F32)

        def fill(i, c):
            r = _rows(i, CH)
            pad[pl.ds(pl.multiple_of(i * CH + H, 8), CH), :] = z_ref[2, r, :].astype(F32) * z_ref[0, r, :].astype(F32)
            return c

        lax.fori_loop(0, NCH, fill, 0)

        def conv(i, c):
            r = _rows(i, CH)
            acc = jnp.zeros((CH, GW), F32)
            for k, sh in _causal_taps(pad, i, CH, H, K):
                acc = acc + w_ref[k:k + 1, :] * sh
            o_ref[r, :] = (z_ref[1, r, :].astype(F32) * (acc + b_ref[...])).astype(CDT)
            return c

        lax.fori_loop(0, NCH, conv, 0)

    return _call(
        body, "mixA_fwd", (MW // GW,),
        [pl.BlockSpec((3, S, GW), lambda c: (0, 0, c)), pl.BlockSpec((K, GW), lambda c: (0, c)),
         _spspec("conv_a_b", GW, lambda c: c)],
        pl.BlockSpec((S, GW), lambda c: (0, c)), SDS((S, MW), CDT),
        [pltpu.VMEM((H + S, GW), F32)])(z, cw, sp)


def _mixA_bwd(z, dm, cw, sp):
    S = z.shape[1]
    K = cw.shape[0]
    H, CH, NCH = _conv_geometry(S, K)

    def body(z_ref, dm_ref, w_ref, b_ref, dz_ref, dw_ref, db_ref, pad, dpad, dw_s):
        pad[0:H, :] = jnp.zeros((H, GW), F32)
        dpad[pl.ds(S, H), :] = jnp.zeros((H, GW), F32)
        dw_s[...] = jnp.zeros_like(dw_s)
        db_ref[...] = jnp.zeros_like(db_ref)

        def fill(i, c):
            r = _rows(i, CH)
            pad[pl.ds(pl.multiple_of(i * CH + H, 8), CH), :] = z_ref[2, r, :].astype(F32) * z_ref[0, r, :].astype(F32)
            return c

        lax.fori_loop(0, NCH, fill, 0)

        def p1(i, c):
            r = _rows(i, CH)
            taps = _causal_taps(pad, i, CH, H, K)
            acc = jnp.zeros((CH, GW), F32)
            for k, sh in taps:
                acc = acc + w_ref[k:k + 1, :] * sh
            dmf = dm_ref[r, :].astype(F32)
            dz_ref[1, r, :] = (dmf * (acc + b_ref[...])).astype(CDT)
            dc = dmf * z_ref[1, r, :].astype(F32)
            dpad[r, :] = dc
            for k, sh in taps:
                dw_s[k:k + 1, :] += jnp.sum(dc * sh, axis=0, keepdims=True)
            db_ref[0:1, :] += jnp.sum(dc, axis=0, keepdims=True)
            return c

        lax.fori_loop(0, NCH, p1, 0)

        def p2(i, c):
            r = _rows(i, CH)
            dq = jnp.zeros((CH, GW), F32)
            for k, sh in _anti_taps(dpad, i, CH, H, K):
                dq = dq + w_ref[k:k + 1, :] * sh
            dz_ref[0, r, :] = (dq * z_ref[2, r, :].astype(F32)).astype(CDT)
            dz_ref[2, r, :] = (dq * z_ref[0, r, :].astype(F32)).astype(CDT)
            return c

        lax.fori_loop(0, NCH, p2, 0)
        dw_ref[...] = dw_s[0:K, :]

    return _call(
        body, "mixA_bwd", (MW // GW,),
        [pl.BlockSpec((3, S, GW), lambda c: (0, 0, c)), pl.BlockSpec((S, GW), lambda c: (0, c)),
         pl.BlockSpec((K, GW), lambda c: (0, c)), _spspec("conv_a_b", GW, lambda c: c)],
        [pl.BlockSpec((3, S, GW), lambda c: (0, 0, c)), pl.BlockSpec((K, GW), lambda c: (0, c)),
         pl.BlockSpec((8, GW), lambda c: (0, c))],
        [SDS((3, S, MW), CDT), SDS((K, MW), F32), SDS((8, MW), F32)],
        [pltpu.VMEM((H + S, GW), F32), pltpu.VMEM((S + H, GW), F32), pltpu.VMEM((8 * ((K + 7) // 8), GW), F32)])(z, dm, cw, sp)


def _mixD_conv_fwd(z, cw, sp):
    S = z.shape[1]
    K = cw.shape[0]
    H, CH, NCH = _conv_geometry(S, K)

    def body(z_ref, w_ref, b_ref, o_ref, pad):
        pad[0:H, :] = jnp.zeros((H, GW), F32)

        def fill(i, c):
            r = _rows(i, CH)
            pad[pl.ds(pl.multiple_of(i * CH + H, 8), CH), :] = (
                z_ref[0, r, :].astype(F32) * jax.nn.sigmoid(z_ref[1, r, :].astype(F32)))
            return c

        lax.fori_loop(0, NCH, fill, 0)

        def conv(i, c):
            acc = jnp.zeros((CH, GW), F32)
            for k, sh in _causal_taps(pad, i, CH, H, K):
                acc = acc + w_ref[k:k + 1, :] * sh
            o_ref[_rows(i, CH), :] = (acc + b_ref[...]).astype(CDT)
            return c

        lax.fori_loop(0, NCH, conv, 0)

    return _call(
        body, "mixD_conv_fwd", (MW // GW,),
        [pl.BlockSpec((2, S, GW), lambda c: (3, 0, c)), pl.BlockSpec((K, GW), lambda c: (0, c)),
         _spspec("conv_d_b", GW, lambda c: c)],
        pl.BlockSpec((S, GW), lambda c: (0, c)), SDS((S, MW), CDT),
        [pltpu.VMEM((H + S, GW), F32)])(z, cw, sp)


def _mixD_conv_bwd(z, dy, cw):
    S = z.shape[1]
    K = cw.shape[0]
    H, CH, NCH = _conv_geometry(S, K)

    def body(z_ref, dy_ref, w_ref, dz_ref, dw_ref, db_ref, pad, dpad, dw_s):
        pad[0:H, :] = jnp.zeros((H, GW), F32)
        dpad[pl.ds(S, H), :] = jnp.zeros((H, GW), F32)
        dw_s[...] = jnp.zeros_like(dw_s)
        db_ref[...] = jnp.zeros_like(db_ref)

        def fill(i, c):
            r = _rows(i, CH)
            pad[pl.ds(pl.multiple_of(i * CH + H, 8), CH), :] = (
                z_ref[0, r, :].astype(F32) * jax.nn.sigmoid(z_ref[1, r, :].astype(F32)))
            dpad[r, :] = dy_ref[r, :].astype(F32)
            return c

        lax.fori_loop(0, NCH, fill, 0)

        def p1(i, c):
            dyf = dy_ref[_rows(i, CH), :].astype(F32)
            for k, sh in _causal_taps(pad, i, CH, H, K):
                dw_s[k:k + 1, :] += jnp.sum(dyf * sh, axis=0, keepdims=True)
            db_ref[0:1, :] += jnp.sum(dyf, axis=0, keepdims=True)
            return c

        lax.fori_loop(0, NCH, p1, 0)

        def p2(i, c):
            r = _rows(i, CH)
            dq = jnp.zeros((CH, GW), F32)
            for k, sh in _anti_taps(dpad, i, CH, H, K):
                dq = dq + w_ref[k:k + 1, :] * sh
            a = z_ref[0, r, :].astype(F32)
            sg = jax.nn.sigmoid(z_ref[1, r, :].astype(F32))
            dz_ref[0, r, :] = (dq * sg).astype(CDT)
            dz_ref[1, r, :] = (dq * a * sg * (1.0 - sg)).astype(CDT)
            return c

        lax.fori_loop(0, NCH, p2, 0)
        dw_ref[...] = dw_s[0:K, :]

    return _call(
        body, "mixD_conv_bwd", (MW // GW,),
        [pl.BlockSpec((2, S, GW), lambda c: (3, 0, c)), pl.BlockSpec((S, GW), lambda c: (0, c)),
         pl.BlockSpec((K, GW), lambda c: (0, c))],
        [pl.BlockSpec((2, S, GW), lambda c: (0, 0, c)), pl.BlockSpec((K, GW), lambda c: (0, c)),
         pl.BlockSpec((8, GW), lambda c: (0, c))],
        [SDS((2, S, MW), CDT), SDS((K, MW), F32), SDS((8, MW), F32)],
        [pltpu.VMEM((H + S, GW), F32), pltpu.VMEM((S + H, GW), F32), pltpu.VMEM((8 * ((K + 7) // 8), GW), F32)])(z, dy, cw)


def _mixD_ln_bwd(dm, yd, sp):
    S = yd.shape[0]
    TM = min(512, S)

    def body(dm_ref, y_ref, lg_ref, lb_ref, dy_ref, gp_ref):
        @pl.when(pl.program_id(0) == 0)
        def _():
            gp_ref[...] = jnp.zeros_like(gp_ref)

        xh, rs = _ln(y_ref[...].astype(F32))
        _, dsl = _silu_parts(xh * lg_ref[...] + lb_ref[...])
        dl = dm_ref[...].astype(F32) * dsl
        gp_ref[0:1, :] += jnp.sum(dl * xh, axis=0, keepdims=True)
        gp_ref[1:2, :] += jnp.sum(dl, axis=0, keepdims=True)
        dy_ref[...] = _ln_bwd(xh, rs, dl * lg_ref[...]).astype(CDT)

    row = lambda i: (i, 0)
    return _call(
        body, "mixD_ln_bwd", (S // TM,),
        [pl.BlockSpec((TM, MW), row), pl.BlockSpec((TM, MW), row), _spspec("conv_d_ln_g", MW, _zero),
         _spspec("conv_d_ln_b", MW, _zero)],
        [pl.BlockSpec((TM, MW), row), pl.BlockSpec((8, MW), lambda i: (0, 0))],
        [SDS((S, MW), CDT), SDS((8, MW), F32)])(dm, yd, sp, sp)


def _box_causal(val, g):
    s = val
    for d in range(g + 1):
        s = s + pltpu.roll(s, 1 << d, 0)
    return s


def _box_anti(val, g):
    n = val.shape[0]
    s = val
    for d in range(g + 1):
        s = s + pltpu.roll(s, n - (1 << d), 0)
    return s


def _pool_count(i, ch, win):
    t = lax.broadcasted_iota(jnp.int32, (ch, GW), 0) + (i * ch + 1)
    return jnp.minimum(t, win).astype(F32)


def _mixB_fwd(z, wp, sp):
    S = z.shape[1]
    H, CH = 16, min(256, S)
    NCH = S // CH
    assert POOL_WINDOWS == tuple(2 << g for g in range(4))

    def body(p_ref, wp_ref, sc_ref, o_ref, pad):
        pad[0:H, :] = jnp.zeros((H, MW), F32)

        def fill(i, c):
            pad[pl.ds(pl.multiple_of(i * CH + H, 8), CH), :] = p_ref[_rows(i, CH), :].astype(F32)
            return c

        lax.fori_loop(0, NCH, fill, 0)

        def step(i, c):
            r = _rows(i, CH)
            for g in range(4):
                gs = slice(g * GW, (g + 1) * GW)
                val = pad[pl.ds(pl.multiple_of(i * CH, 8), CH + H), gs]
                pooled = _box_causal(val, g)[H:, :] / _pool_count(i, CH, POOL_WINDOWS[g]) - val[H:, :]
                mixed = _nn(pooled.astype(CDT), wp_ref[g].astype(CDT))
                o_ref[r, gs] = (mixed * sc_ref[:, gs]).astype(CDT)
            return c

        lax.fori_loop(0, NCH, step, 0)

    return _call(
        body, "mixB_fwd", (1,),
        [pl.BlockSpec((None, S, MW), lambda i: (3, 0, 0)), pl.BlockSpec((4, GW, GW), lambda i: (0, 0, 0)),
         _spspec("pool_scale", MW, _zero)],
        pl.BlockSpec((S, MW), lambda i: (0, 0)), SDS((S, MW), CDT),
        [pltpu.VMEM((H + S, MW), F32)])(z, wp, sp)


def _mixB_bwd(z, dm, wp, sp):
    S = z.shape[1]
    H, CH = 16, min(256, S)
    NCH = S // CH

    def body(p_ref, dm_ref, wp_ref, sc_ref, dz_ref, dwp_ref, dsc_ref, pad, rpad):
        pad[0:H, :] = jnp.zeros((H, MW), F32)
        rpad[pl.ds(S, H), :] = jnp.zeros((H, MW), F32)
        dwp_ref[...] = jnp.zeros_like(dwp_ref)
        dsc_ref[...] = jnp.zeros_like(dsc_ref)

        def fill(i, c):
            pad[pl.ds(pl.multiple_of(i * CH + H, 8), CH), :] = p_ref[_rows(i, CH), :].astype(F32)
            return c

        lax.fori_loop(0, NCH, fill, 0)

        def p1(i, c):
            r = _rows(i, CH)
            for g in range(4):
                gs = slice(g * GW, (g + 1) * GW)
                cnt = _pool_count(i, CH, POOL_WINDOWS[g])
                val = pad[pl.ds(pl.multiple_of(i * CH, 8), CH + H), gs]
                pooled = (_box_causal(val, g)[H:, :] / cnt - val[H:, :]).astype(CDT)
                w = wp_ref[g].astype(CDT)
                mixed = _nn(pooled, w)
                dmf = dm_ref[r, gs].astype(F32)
                dsc_ref[0:1, gs] += jnp.sum(dmf * mixed, axis=0, keepdims=True)
                dmx = (dmf * sc_ref[:, gs]).astype(CDT)
                dwp_ref[g] += _tn(pooled, dmx)
                rpad[r, gs] = _nt(dmx, w) / cnt
            return c

        lax.fori_loop(0, NCH, p1, 0)

        def p2(i, c):
            r = _rows(i, CH)
            for g in range(4):
                gs = slice(g * GW, (g + 1) * GW)
                val = rpad[pl.ds(pl.multiple_of(i * CH, 8), CH + H), gs]
                dp = _box_anti(val, g)[:CH, :] - val[:CH, :] * _pool_count(i, CH, POOL_WINDOWS[g])
                dz_ref[r, gs] = dp.astype(CDT)
            return c

        lax.fori_loop(0, NCH, p2, 0)

    return _call(
        body, "mixB_bwd", (1,),
        [pl.BlockSpec((None, S, MW), lambda i: (3, 0, 0)), pl.BlockSpec((S, MW), lambda i: (0, 0)),
         pl.BlockSpec((4, GW, GW), lambda i: (0, 0, 0)), _spspec("pool_scale", MW, _zero)],
        [pl.BlockSpec((None, S, MW), lambda i: (0, 0, 0)), pl.BlockSpec((4, GW, GW), lambda i: (0, 0, 0)),
         pl.BlockSpec((8, MW), lambda i: (0, 0))],
        [SDS((1, S, MW), CDT), SDS((4, GW, GW), F32), SDS((8, MW), F32)],
        [pltpu.VMEM((H + S, MW), F32), pltpu.VMEM((S + H, MW), F32)])(z, dm, wp, sp)


def _sgu_mask():
    ci = lax.broadcasted_iota(jnp.int32, (GW, GW), 0) // CHUNK
    cj = lax.broadcasted_iota(jnp.int32, (GW, GW), 1) // CHUNK
    return cj <= ci


def _mixC_fwd(z, ws, bsc, sp):
    S = z.shape[1]
    RB = min(512, S)

    def body(z_ref, lg_ref, lb_ref, ws_ref, bs_ref, o_ref):
        mask = _sgu_mask()
        gu = _gelu(z_ref[0].astype(F32))
        xh, _ = _ln(_gelu(z_ref[1].astype(F32)))
        vn = (xh * lg_ref[...] + lb_ref[...]).astype(CDT)
        for g in range(4):
            gs = slice(g * GW, (g + 1) * GW)
            wm = jnp.where(mask, ws_ref[g], 0.0).astype(CDT)
            for nb in range(RB // GW):
                rs = slice(nb * GW, (nb + 1) * GW)
                mixed = _nn(wm, vn[rs, gs]) + bs_ref[g]
                o_ref[rs, gs] = (gu[rs, gs] * mixed).astype(CDT)

    return _call(
        body, "mixC_fwd", (S // RB,),
        [pl.BlockSpec((2, RB, MW), lambda i: (2, i, 0)), _spspec("sgu_ln_g", MW, _zero), _spspec("sgu_ln_b", MW, _zero),
         pl.BlockSpec((4, GW, GW), lambda i: (0, 0, 0)), pl.BlockSpec((4, GW, 1), lambda i: (0, 0, 0))],
        pl.BlockSpec((RB, MW), lambda i: (i, 0)), SDS((S, MW), CDT))(z, sp, sp, ws, bsc)


def _mixC_bwd(z, dm, ws, bsc, sp):
    S = z.shape[1]
    RB = min(512, S)
    NR = S // RB

    def body(z_ref, dm_ref, lg_ref, lb_ref, ws_ref, bs_ref, dz_ref, dws_ref, dbs_ref, gp_ref, dvn_s):
        i = pl.program_id(0)

        @pl.when(i == 0)
        def _():
            dws_ref[...] = jnp.zeros_like(dws_ref)
            dbs_ref[...] = jnp.zeros_like(dbs_ref)
            gp_ref[...] = jnp.zeros_like(gp_ref)

        mask = _sgu_mask()
        gu, dgu = _gelu_parts(z_ref[0].astype(F32))
        gv, dgv = _gelu_parts(z_ref[1].astype(F32))
        xh, rs_ = _ln(gv)
        vn = (xh * lg_ref[...] + lb_ref[...]).astype(CDT)
        dmf = dm_ref[...].astype(F32)
        for g in range(4):
            gs = slice(g * GW, (g + 1) * GW)
            wm = jnp.where(mask, ws_ref[g], 0.0).astype(CDT)
            for nb in range(RB // GW):
                rs = slice(nb * GW, (nb + 1) * GW)
                vb = vn[rs, gs]
                mixed = _nn(wm, vb) + bs_ref[g]
                dz_ref[0, rs, gs] = (dmf[rs, gs] * mixed * dgu[rs, gs]).astype(CDT)
                dmx = dmf[rs, gs] * gu[rs, gs]
                dbs_ref[g] += dmx
                dmxc = dmx.astype(CDT)
                dws_ref[g] += _nt(dmxc, vb)
                dvn_s[rs, gs] = _tn(wm, dmxc)
        dvn = dvn_s[...]
        gp_ref[0:1, :] += jnp.sum(dvn * xh, axis=0, keepdims=True)
        gp_ref[1:2, :] += jnp.sum(dvn, axis=0, keepdims=True)
        dz_ref[1] = (_ln_bwd(xh, rs_, dvn * lg_ref[...]) * dgv).astype(CDT)

        @pl.when(i == NR - 1)
        def _():
            for g in range(4):
                dws_ref[g] = jnp.where(mask, dws_ref[g], 0.0)
                dbs_ref[g] = jnp.broadcast_to(jnp.sum(dbs_ref[g], axis=1, keepdims=True), (GW, GW))

    full3 = lambda i: (0, 0, 0)
    return _call(
        body, "mixC_bwd", (NR,),
        [pl.BlockSpec((2, RB, MW), lambda i: (2, i, 0)), pl.BlockSpec((RB, MW), lambda i: (i, 0)),
         _spspec("sgu_ln_g", MW, _zero), _spspec("sgu_ln_b", MW, _zero),
         pl.BlockSpec((4, GW, GW), full3), pl.BlockSpec((4, GW, 1), full3)],
        [pl.BlockSpec((2, RB, MW), lambda i: (0, i, 0)), pl.BlockSpec((4, GW, GW), full3), pl.BlockSpec((4, GW, GW), full3),
         pl.BlockSpec((8, MW), lambda i: (0, 0))],
        [SDS((2, S, MW), CDT), SDS((4, GW, GW), F32), SDS((4, GW, GW), F32), SDS((8, MW), F32)],
        [pltpu.VMEM((RB, MW), F32)])(z, dm, sp, sp, ws, bsc)


def _merge_fwd(ma, mb, mc, yd, g, wb, pwo, x, sp):
    S, D = x.shape
    TM = min(512, S)

    def body(ma_ref, mb_ref, mc_ref, yd_ref, g_ref, wb_ref, wo_ref, x_ref, lg_ref, lb_ref, qg_ref,
             xo_ref, md_ref, yk_ref, mg_ref, mo_ref, md_s, acc):
        j = pl.program_id(1)

        @pl.when(j == 0)
        def _():
            xh, _ = _ln(yd_ref[...].astype(F32))
            sl, _ = _silu_parts(xh * lg_ref[...] + lb_ref[...])
            md = sl.astype(CDT)
            md_s[...] = md
            md_ref[...] = md
            acc[...] = jnp.zeros_like(acc)

        merged = jnp.zeros((TM, GW), F32)
        for k, m in enumerate((ma_ref[...], mb_ref[...], mc_ref[...], md_s[...])):
            yk = _nn(m, wb_ref[k])
            yk_ref[k] = yk.astype(CDT)
            merged = merged + g_ref[k].astype(F32) * yk
        mgc = merged.astype(CDT)
        mg_ref[...] = mgc
        acc[...] += _nn(mgc, wo_ref[...])

        @pl.when(j == NS - 1)
        def _():
            mo = acc[...]
            mo_ref[...] = mo.astype(CDT)
            n, _ = _rms(mo)
            xo_ref[...] = x_ref[...] + n * qg_ref[...]

    row = lambda i, j: (i, 0)
    rowm = pl.BlockSpec((TM, MW), row)
    colb = pl.BlockSpec((4, TM, GW), lambda i, j: (0, i, j))
    return _call(
        body, "merge_fwd", (S // TM, NS),
        [rowm, rowm, rowm, rowm, colb, pl.BlockSpec((None, 4, MW, GW), lambda i, j: (j, 0, 0, 0)),
         pl.BlockSpec((None, GW, D), lambda i, j: (j, 0, 0)), pl.BlockSpec((TM, D), row),
         _spspec("conv_d_ln_g", MW, _zero), _spspec("conv_d_ln_b", MW, _zero), _spspec("mix_post_g", D, _zero)],
        [pl.BlockSpec((TM, D), row), rowm, colb, pl.BlockSpec((TM, GW), lambda i, j: (i, j)), pl.BlockSpec((TM, D), row)],
        [SDS((S, D), F32), SDS((S, MW), CDT), SDS((4, S, D), CDT), SDS((S, D), CDT), SDS((S, D), CDT)],
        [pltpu.VMEM((TM, MW), CDT), pltpu.VMEM((TM, D), F32)])(ma, mb, mc, yd, g, wb, pwo, x, sp, sp, sp)


def _merge_bwd_act(dxo, mo, g, yk, wb, pwo, sp):
    S, D = dxo.shape
    TM = min(512, S)

    def body(dxo_ref, mo_ref, g_ref, yk_ref, wb_ref, wo_ref, qg_ref, dmo_ref, dm_ref, dgp_ref, dyk_ref, gp_ref, dmo_s, acc):
        i = pl.program_id(0)
        j = pl.program_id(1)

        @pl.when((i == 0) & (j == 0))
        def _():
            gp_ref[...] = jnp.zeros_like(gp_ref)

        @pl.when(j == 0)
        def _():
            n, r = _rms(mo_ref[...].astype(F32))
            dmo, dg = _rms_bwd(n, r, qg_ref[...], dxo_ref[...])
            gp_ref[0:1, :] += dg
            dmoc = dmo.astype(CDT)
            dmo_s[...] = dmoc
            dmo_ref[...] = dmoc
            acc[...] = jnp.zeros_like(acc)

        dmg = _nt(dmo_s[...], wo_ref[...])
        for k in range(4):
            gk = g_ref[k].astype(F32)
            dyk = (dmg * gk).astype(CDT)
            dyk_ref[k] = dyk
            dgp_ref[k] = (dmg * yk_ref[k].astype(F32) * gk * (1.0 - gk)).astype(CDT)
            acc[k] += _nt(dyk, wb_ref[k])

        @pl.when(j == NS - 1)
        def _():
            dm_ref[...] = acc[...].astype(CDT)

    row = lambda i, j: (i, 0)
    colb = pl.BlockSpec((4, TM, GW), lambda i, j: (0, i, j))
    return _call(
        body, "merge_bwd_act", (S // TM, NS),
        [pl.BlockSpec((TM, D), row), pl.BlockSpec((TM, D), row), colb, colb,
         pl.BlockSpec((None, 4, MW, GW), lambda i, j: (j, 0, 0, 0)), pl.BlockSpec((None, GW, D), lambda i, j: (j, 0, 0)),
         _spspec("mix_post_g", D, _zero)],
        [pl.BlockSpec((TM, D), row), pl.BlockSpec((4, TM, MW), lambda i, j: (0, i, 0)), colb, colb,
         pl.BlockSpec((8, D), lambda i, j: (0, 0))],
        [SDS((S, D), CDT), SDS((4, S, MW), CDT), SDS((4, S, D), CDT), SDS((4, S, D), CDT), SDS((8, D), F32)],
        [pltpu.VMEM((TM, D), CDT), pltpu.VMEM((4, TM, MW), F32)])(dxo, mo, g, yk, wb, pwo, sp)


def _merge_bwd_w(ma, mb, mc, md, dyk, mg, dmo):
    S, D = dmo.shape
    TK = min(512, S)
    NK = S // TK

    def body(ma_ref, mb_ref, mc_ref, md_ref, dyk_ref, mg_ref, dmo_ref, gwb_ref, gwo_ref, accb, acco):
        k = pl.program_id(1)

        @pl.when(k == 0)
        def _():
            accb[...] = jnp.zeros_like(accb)
            acco[...] = jnp.zeros_like(acco)

        for b, m in enumerate((ma_ref, mb_ref, mc_ref, md_ref)):
            accb[b] += _tn(m[...], dyk_ref[b])
        acco[...] += _tn(mg_ref[...], dmo_ref[...])

        @pl.when(k == NK - 1)
        def _():
            gwb_ref[...] = accb[...].astype(CDT)
            gwo_ref[...] = acco[...].astype(CDT)

    rowm = pl.BlockSpec((TK, MW), lambda j, k: (k, 0))
    return _call(
        body, "merge_bwd_w", (NS, NK),
        [rowm, rowm, rowm, rowm, pl.BlockSpec((4, TK, GW), lambda j, k: (0, k, j)),
         pl.BlockSpec((TK, GW), lambda j, k: (k, j)), pl.BlockSpec((TK, D), lambda j, k: (k, 0))],
        [pl.BlockSpec((None, 4, MW, GW), lambda j, k: (j, 0, 0, 0)), pl.BlockSpec((None, GW, D), lambda j, k: (j, 0, 0))],
        [SDS((NS, 4, MW, GW), CDT), SDS((NS, GW, D), CDT)],
        [pltpu.VMEM((4, MW, GW), F32), pltpu.VMEM((GW, D), F32)])(ma, mb, mc, md, dyk, mg, dmo)


def _xa_kv(mem, sp, pxa):
    M, D = mem.shape

    def body(m_ref, g_ref, wk_ref, wv_ref, mn_ref, k_ref, v_ref):
        n, _ = _rms(m_ref[...])
        mn = (n * g_ref[...]).astype(CDT)
        mn_ref[...] = mn
        k_ref[...] = _nn(mn, wk_ref[...].reshape(D, D)).astype(CDT)
        v_ref[...] = _nn(mn, wv_ref[...].reshape(D, D)).astype(CDT)

    full = pl.BlockSpec((M, D), lambda i: (0, 0))
    return _call(
        body, "xa_kv", (1,),
        [full, _spspec("mem_g", D, _zero), pl.BlockSpec((NS, GW, D), lambda i: (0, 1, 0)),
         pl.BlockSpec((NS, GW, D), lambda i: (0, 2, 0))],
        [full, full, full], [SDS((M, D), CDT)] * 3)(mem, sp, pxa, pxa)


def _softmax(s):
    e = jnp.exp(s - jnp.max(s, axis=-1, keepdims=True))
    return e / jnp.sum(e, axis=-1, keepdims=True)


def _xa_fwd(x, kk, vv, sp, pxa):
    S, D = x.shape
    M = kk.shape[0]
    TM = min(512, S)
    HD = D // XA_HEADS
    scale = HD ** -0.5

    def body(x_ref, k_ref, v_ref, pg_ref, qg_ref, wq_ref, wo_ref, xo_ref, hb_ref, q_ref, o_ref, po_ref):
        n, _ = _rms(x_ref[...])
        hb = (n * pg_ref[...]).astype(CDT)
        hb_ref[...] = hb
        q = _nn(hb, wq_ref[...].reshape(D, D)).astype(CDT)
        q_ref[...] = q
        for h in range(XA_HEADS):
            hs = slice(h * HD, (h + 1) * HD)
            p = _softmax(_nt(q[:, hs], k_ref[:, hs]) * scale)
            o_ref[:, hs] = _nn(p.astype(CDT), v_ref[:, hs]).astype(CDT)
        po = _nn(o_ref[...], wo_ref[...].reshape(D, D))
        po_ref[...] = po.astype(CDT)
        n, _ = _rms(po)
        xo_ref[...] = x_ref[...] + n * qg_ref[...]

    row = pl.BlockSpec((TM, D), lambda i: (i, 0))
    full = pl.BlockSpec((M, D), lambda i: (0, 0))
    return _call(
        body, "xa_fwd", (S // TM,),
        [row, full, full, _spspec("xa_pre_g", D, _zero), _spspec("xa_post_g", D, _zero),
         pl.BlockSpec((NS, GW, D), lambda i: (0, 0, 0)), pl.BlockSpec((NS, GW, D), lambda i: (0, 3, 0))],
        [row] * 5, [SDS((S, D), F32)] + [SDS((S, D), CDT)] * 4)(x, kk, vv, sp, sp, pxa, pxa)


def _xa_bwd_act(dxo, x, po, q, kk, vv, sp, pxa):
    S, D = x.shape
    M = kk.shape[0]
    TM = min(512, S)
    HD = D // XA_HEADS
    scale = HD ** -0.5

    def body(dxo_ref, x_ref, po_ref, q_ref, k_ref, v_ref, pg_ref, qg_ref, wq_ref, wo_ref,
             dx_ref, dpo_ref, dq_ref, dk_ref, dv_ref, gp_ref):
        @pl.when(pl.program_id(0) == 0)
        def _():
            gp_ref[...] = jnp.zeros_like(gp_ref)
            dk_ref[...] = jnp.zeros_like(dk_ref)
            dv_ref[...] = jnp.zeros_like(dv_ref)

        n, r = _rms(po_ref[...].astype(F32))
        dpo, dg = _rms_bwd(n, r, qg_ref[...], dxo_ref[...])
        gp_ref[1:2, :] += dg
        dpoc = dpo.astype(CDT)
        dpo_ref[...] = dpoc
        do = _nt(dpoc, wo_ref[...].reshape(D, D)).astype(CDT)
        for h in range(XA_HEADS):
            hs = slice(h * HD, (h + 1) * HD)
            qh = q_ref[:, hs]
            p = _softmax(_nt(qh, k_ref[:, hs]) * scale)
            pc = p.astype(CDT)
            dv_ref[:, hs] += _tn(pc, do[:, hs])
            dp = _nt(do[:, hs], v_ref[:, hs])
            ds = (p * (dp - jnp.sum(p * dp, axis=-1, keepdims=True)) * scale).astype(CDT)
            dq_ref[:, hs] = _nn(ds, k_ref[:, hs]).astype(CDT)
            dk_ref[:, hs] += _tn(ds, qh)
        dhb = _nt(dq_ref[...], wq_ref[...].reshape(D, D))
        n, r = _rms(x_ref[...])
        dx, dg = _rms_bwd(n, r, pg_ref[...], dhb)
        dx_ref[...] = dxo_ref[...] + dx
        gp_ref[0:1, :] += dg

    row = pl.BlockSpec((TM, D), lambda i: (i, 0))
    full = pl.BlockSpec((M, D), lambda i: (0, 0))
    return _call(
        body, "xa_bwd_act", (S // TM,),
        [row, row, row, row, full, full, _spspec("xa_pre_g", D, _zero), _spspec("xa_post_g", D, _zero),
         pl.BlockSpec((NS, GW, D), lambda i: (0, 0, 0)), pl.BlockSpec((NS, GW, D), lambda i: (0, 3, 0))],
        [row, row, row, full, full, pl.BlockSpec((8, D), lambda i: (0, 0))],
        [SDS((S, D), F32), SDS((S, D), CDT), SDS((S, D), CDT), SDS((M, D), F32), SDS((M, D), F32), SDS((8, D), F32)],
    )(dxo, x, po, q, kk, vv, sp, sp, pxa, pxa)


def _xa_bwd_w(hb, dq, o, dpo, mn, dk, dv):
    S, D = hb.shape
    M = mn.shape[0]
    TK = min(512, S)
    NK = S // TK

    def body(hb_ref, dq_ref, o_ref, dpo_ref, mn_ref, dk_ref, dv_ref, g_ref, acc):
        k = pl.program_id(1)

        @pl.when(k == 0)
        def _():
            acc[...] = jnp.zeros_like(acc)

        acc[0:GW, :] += _tn(hb_ref[...], dq_ref[...])
        acc[3 * GW:4 * GW, :] += _tn(o_ref[...], dpo_ref[...])

        @pl.when(k == NK - 1)
        def _():
            acc[GW:2 * GW, :] = _tn(mn_ref[...], dk_ref[...].astype(CDT))
            acc[2 * GW:3 * GW, :] = _tn(mn_ref[...], dv_ref[...].astype(CDT))
            g_ref[...] = acc[...].astype(CDT)

    colb = pl.BlockSpec((TK, GW), lambda j, k: (k, j))
    rowb = pl.BlockSpec((TK, D), lambda j, k: (k, 0))
    full = pl.BlockSpec((M, D), lambda j, k: (0, 0))
    return _call(
        body, "xa_bwd_w", (NS, NK),
        [colb, rowb, colb, rowb, pl.BlockSpec((M, GW), lambda j, k: (0, j)), full, full],
        pl.BlockSpec((None, 4 * GW, D), lambda j, k: (j, 0, 0)), SDS((NS, 4 * GW, D), CDT),
        [pltpu.VMEM((4 * GW, D), F32)])(hb, dq, o, dpo, mn, dk, dv)


def _xa_kv_bwd(mem, dk, dv, sp, pxa):
    M, D = mem.shape

    def body(m_ref, dk_ref, dv_ref, wk_ref, wv_ref, gp_ref):
        dmn = _nt(dk_ref[...].astype(CDT), wk_ref[...].reshape(D, D)) + _nt(dv_ref[...].astype(CDT), wv_ref[...].reshape(D, D))
        n, _ = _rms(m_ref[...])
        gp_ref[...] = jnp.zeros_like(gp_ref)
        gp_ref[0:1, :] = jnp.sum(dmn * n, axis=0, keepdims=True)

    full = pl.BlockSpec((M, D), lambda i: (0, 0))
    return _call(
        body, "xa_kv_bwd", (1,),
        [full, full, full, pl.BlockSpec((NS, GW, D), lambda i: (0, 1, 0)), pl.BlockSpec((NS, GW, D), lambda i: (0, 2, 0))],
        pl.BlockSpec((8, D), lambda i: (0, 0)), SDS((8, D), F32))(mem, dk, dv, pxa, pxa)


def _loss_head(y, t):
    S, D = y.shape
    TM = min(512, S)

    def body(y_ref, t_ref, dy_ref, l_ref):
        @pl.when(pl.program_id(0) == 0)
        def _():
            l_ref[...] = jnp.zeros_like(l_ref)

        e = y_ref[...] - t_ref[...]
        dy_ref[...] = e * (1.0 / D)
        l_ref[...] += 0.5 * jnp.sum(jnp.mean(e * e, axis=-1, keepdims=True), axis=0, keepdims=True)

    row = pl.BlockSpec((TM, D), lambda i: (i, 0))
    return _call(body, "loss_head", (S // TM,), [row, row], [row, pl.BlockSpec((8, 128), lambda i: (0, 0))],
                 [SDS((S, D), F32), SDS((8, 128), F32)])(y, t)


def _row_tile(rows, cols, limit=1 << 18, step=8):
    if rows * cols <= limit or rows % step:
        return rows
    best = step
    for t in range(step, rows + 1, step):
        if rows % t == 0 and t * cols <= limit:
            best = t
    return best


def _slot_sum(r):
    _, R, C = r.shape
    TR = _row_tile(R, C * NS, limit=1 << 21, step=16)

    def body(r_ref, o_ref):
        acc = r_ref[0].astype(F32)
        for j in range(1, NS):
            acc = acc + r_ref[j].astype(F32)
        o_ref[...] = acc

    return _call(body, "slot_sum", (R // TR,), [pl.BlockSpec((NS, TR, C), lambda i: (0, i, 0))],
                 pl.BlockSpec((TR, C), lambda i: (i, 0)), SDS((R, C), F32))(r)


def _adamw(w, g, m, v):
    shape = w.shape
    C = shape[-1]
    R = w.size // C
    TR = _row_tile(R, C)
    c1 = 1.0 - ADAM_B1 ** ADAM_STEP
    c2 = 1.0 - ADAM_B2 ** ADAM_STEP

    def body(w_ref, g_ref, m_ref, v_ref, d_ref, nm_ref, nv_ref):
        gg = g_ref[...]
        nm = ADAM_B1 * m_ref[...] + (1.0 - ADAM_B1) * gg
        nv = ADAM_B2 * v_ref[...] + (1.0 - ADAM_B2) * (gg * gg)
        nm_ref[...] = nm
        nv_ref[...] = nv
        d_ref[...] = -ADAM_LR * ((nm / c1) / (jnp.sqrt(nv / c2) + ADAM_EPS) + ADAM_WD * w_ref[...])

    blk = pl.BlockSpec((TR, C), lambda i: (i, 0))
    outs = _call(body, "adamw", (R // TR,), [blk] * 4, [blk] * 3, [SDS((R, C), F32)] * 3)(
        w.reshape(R, C), g.reshape(R, C), m.reshape(R, C), v.reshape(R, C))
    return tuple(o.reshape(shape) for o in outs)


def _exchange(arrs, scatter, name):
    n = len(arrs)
    np_ = NS - 1

    def body(*refs):
        ins, outs = refs[:n], refs[n:2 * n]
        send_sems, recv_sems, loc_sems = refs[2 * n:]
        x, y, c = lax.axis_index("x"), lax.axis_index("y"), lax.axis_index("c")
        me = 4 * x + 2 * y + c
        peers = []
        for f in range(1, NS):
            px = 1 - x if f & 4 else x
            py = 1 - y if f & 2 else y
            pc = 1 - c if f & 1 else c
            peers.append(((px, py, pc), 4 * px + 2 * py + pc))

        def src(a, pid):
            return ins[a].at[pid] if scatter else ins[a]

        local = [pltpu.make_async_copy(src(a, me), outs[a].at[me], loc_sems.at[a]) for a in range(n)]
        for cp in local:
            cp.start()
        sends = []
        for a in range(n):
            for f, (dev, pid) in enumerate(peers):
                sends.append(pltpu.make_async_remote_copy(
                    src_ref=src(a, pid), dst_ref=outs[a].at[me], send_sem=send_sems.at[a * np_ + f],
                    recv_sem=recv_sems.at[a * np_ + f], device_id=dev, device_id_type=pl.DeviceIdType.MESH))
        for cp in sends:
            cp.start()
        for a in range(n):
            for f, (dev, pid) in enumerate(peers):
                pltpu.make_async_remote_copy(
                    src_ref=src(a, pid), dst_ref=outs[a].at[pid], send_sem=send_sems.at[a * np_ + f],
                    recv_sem=recv_sems.at[a * np_ + f], device_id=dev, device_id_type=pl.DeviceIdType.MESH).wait_recv()
        for cp in sends:
            cp.wait_send()
        for cp in local:
            cp.wait()

    out_shape = [SDS(a.shape if scatter else (NS,) + a.shape, a.dtype) for a in arrs]
    anyspec = pl.BlockSpec(memory_space=pl.ANY)
    outs = pl.pallas_call(
        body, name=name, in_specs=[anyspec] * n, out_specs=[anyspec] * n, out_shape=out_shape,
        scratch_shapes=[pltpu.SemaphoreType.DMA((n * np_,)), pltpu.SemaphoreType.DMA((n * np_,)),
                        pltpu.SemaphoreType.DMA((n,))],
        compiler_params=pltpu.CompilerParams(has_side_effects=True))(*arrs)
    return list(outs)


def _peers():
    x, y, c = lax.axis_index("x"), lax.axis_index("y"), lax.axis_index("c")
    out = []
    for f in range(1, NS):
        px = 1 - x if f & 4 else x
        py = 1 - y if f & 2 else y
        pc = 1 - c if f & 1 else c
        out.append(((px, py, pc), 4 * px + 2 * py + pc))
    return 4 * x + 2 * y + c, out


def _exchange_copies(ins, lands, scatter, send_sems, recv_sems, loc_sems):
    me, peers = _peers()
    np_ = NS - 1

    def src(a, pid):
        return ins[a].at[pid] if scatter[a] else ins[a]

    def rcopy(a, f, dev, land_slot):
        return pltpu.make_async_remote_copy(
            src_ref=src(a, peers[f][1]), dst_ref=lands[a].at[land_slot], send_sem=send_sems.at[a * np_ + f],
            recv_sem=recv_sems.at[a * np_ + f], device_id=dev, device_id_type=pl.DeviceIdType.MESH)

    local = [pltpu.make_async_copy(src(a, me), lands[a].at[me], loc_sems.at[a]) for a in range(len(ins))]
    sends = [rcopy(a, f, dev, me) for a in range(len(ins)) for f, (dev, _) in enumerate(peers)]
    arrivals = [rcopy(a, f, dev, pid) for a in range(len(ins)) for f, (dev, pid) in enumerate(peers)]
    return local, sends, arrivals


_HBM = pl.BlockSpec(memory_space=pltpu.HBM)
_SEM = pl.BlockSpec(memory_space=pltpu.SEMAPHORE)
_ANY = pl.BlockSpec(memory_space=pl.ANY)


def _exchange_start(arrs, scatter, after, name):
    n = len(arrs)
    np_ = NS - 1
    lands = [lax.empty(a.shape if sc else (NS,) + a.shape, a.dtype) for a, sc in zip(arrs, scatter)]

    def body(*refs):
        ins, lnd = refs[:n], refs[n:2 * n]
        send_sems, recv_sems, loc_sems = refs[2 * n + 1:2 * n + 4]
        token = refs[-1]
        local, sends, _ = _exchange_copies(ins, lnd, scatter, send_sems, recv_sems, loc_sems)
        for cp in local + sends:
            cp.start()
        token[...] = jnp.zeros_like(token)

    hbm = lambda a: pltpu.HBM(a.shape, a.dtype)
    outs = pl.pallas_call(
        body, name=name,
        out_shape=(pltpu.SemaphoreType.DMA((n * np_,)), pltpu.SemaphoreType.DMA((n * np_,)), pltpu.SemaphoreType.DMA((n,)),
                   *[hbm(a) for a in arrs], *[hbm(a) for a in lands], SDS((8, 128), F32)),
        in_specs=[_HBM] * (2 * n) + [_ANY],
        out_specs=(_SEM, _SEM, _SEM, *([_HBM] * (2 * n)), pl.BlockSpec(memory_space=pltpu.VMEM)),
        input_output_aliases={i: 3 + i for i in range(2 * n)},
        compiler_params=pltpu.CompilerParams(has_side_effects=pltpu.SideEffectType.DATAFLOW_SIDE_EFFECTING),
    )(*[pltpu.with_memory_space_constraint(a, pltpu.HBM) for a in list(arrs) + lands], after)
    return {"sems": outs[:3], "ins": outs[3:3 + n], "lands": outs[3 + n:3 + 2 * n], "token": outs[-1], "scatter": scatter}


def _exchange_wait(h, after, name):
    n = len(h["ins"])
    scatter = h["scatter"]

    def body(*refs):
        ins, lnd = refs[:n], refs[n:2 * n]
        send_sems, recv_sems, loc_sems = refs[2 * n:2 * n + 3]
        local, sends, arrivals = _exchange_copies(ins, lnd, scatter, send_sems, recv_sems, loc_sems)
        for cp in sends:
            cp.wait_send()
        for cp in arrivals:
            cp.wait_recv()
        for cp in local:
            cp.wait()

    hbm = lambda a: pltpu.HBM(a.shape, a.dtype)
    outs = pl.pallas_call(
        body, name=name,
        out_shape=tuple(hbm(a) for a in list(h["ins"]) + list(h["lands"])),
        in_specs=[_HBM] * (2 * n) + [_SEM] * 3 + [_ANY],
        out_specs=tuple([_HBM] * (2 * n)),
        input_output_aliases={i: i for i in range(2 * n)},
        compiler_params=pltpu.CompilerParams(has_side_effects=pltpu.SideEffectType.DATAFLOW_SIDE_EFFECTING),
    )(*h["ins"], *h["lands"], *h["sems"], after)
    return list(outs[n:])


_W_NAMES = ("ffn1_pre_g", "ffn1_post_g", "ffn1_w1", "ffn1_w3", "ffn1_w2", "mix_pre_g", "mix_post_g", "w_in", "conv_a_w",
            "conv_a_b", "pool_w", "pool_scale", "sgu_ln_g", "sgu_ln_b", "sgu_ws", "sgu_b", "conv_d_w", "conv_d_b",
            "conv_d_ln_g", "conv_d_ln_b", "w_branch", "w_gate", "b_gate", "w_o", "xa_pre_g", "xa_post_g", "mem_g",
            "xa_wq", "xa_wk", "xa_wv", "xa_wo", "ffn2_pre_g", "ffn2_post_g", "ffn2_w1", "ffn2_w3", "ffn2_w2")
_REP_NAMES = tuple(n for n, _ in _SP_NAMES) + ("pool_w", "sgu_ws", "sgu_b", "conv_a_w", "conv_d_w")


def _t(w):
    return jnp.swapaxes(w, -1, -2)


def _step(x, mem, loss_target, W, M, V):
    L = W["w_in"].shape[0]
    S, D = x.shape[1], x.shape[2]
    x0 = x.reshape(S, D)
    memf = mem.reshape(mem.shape[1], D)
    me = 4 * lax.axis_index("x") + 2 * lax.axis_index("y") + lax.axis_index("c")
    FS = W["ffn1_w2"].shape[1]
    KA, KD = W["conv_a_w"].shape[1], W["conv_d_w"].shape[1]
    CS = W["conv_a_w"].shape[2]

    pf1 = jnp.concatenate([_t(W["ffn1_w1"]), _t(W["ffn1_w3"]), W["ffn1_w2"]], axis=1).astype(CDT)
    pf2 = jnp.concatenate([_t(W["ffn2_w1"]), _t(W["ffn2_w3"]), W["ffn2_w2"]], axis=1).astype(CDT)
    pma = jnp.concatenate([_t(W["w_in"]), _t(W["w_gate"])], axis=1).astype(CDT)
    pwo = W["w_o"].astype(CDT)
    pxa = jnp.concatenate([W["xa_wq"], W["xa_wk"], W["xa_wv"], W["xa_wo"]], axis=1).astype(CDT)
    wbs = W["w_branch"].astype(CDT)
    cws = jnp.concatenate([W["conv_a_w"], W["conv_d_w"]], axis=1).reshape(-1, 128)
    sp_all = jnp.concatenate([W[n] for n, _ in _SP_NAMES], axis=1)
    bsc_all = W["sgu_b"][..., None]

    (cwg,) = _exchange([cws], False, "gather_conv_w")
    cwf = cwg.reshape(NS, L, KA + KD, CS).transpose(1, 2, 0, 3).reshape(L, KA + KD, NS * CS)

    def gather_start(l, after):
        return _exchange_start([pf1[l], pf2[l], pma[l], pwo[l], pxa[l], wbs[l]], (False,) * 6, after, f"gather_start_{l}")

    packs = [None] * L
    packs[0] = _exchange_wait(gather_start(0, sp_all), sp_all, "gather_wait_0")
    saved = []
    xc = x0
    for l in range(L):
        gf1, gf2, gma, gwo, gxa, gwb = packs[l]
        sp = sp_all[l:l + 1]
        cwa, cwd = cwf[l, :KA], cwf[l, KA:]
        wp, ws, bsc = W["pool_w"][l], W["sgu_ws"][l], bsc_all[l]
        s = {"x0": xc}
        nxt = gather_start(l + 1, gf1) if l + 1 < L else None
        xc, s["hb1"], s["a1"], s["b1"], s["y1"] = _ffn_fwd(xc, sp, "ffn1_pre_g", "ffn1_post_g", gf1,
                                                            nxt["token"] if nxt else sp)
        s["x1"] = xc
        s["hbm"], s["z"], s["g"] = _mix_in(xc, sp, gma)
        s["ma"] = _mixA_fwd(s["z"], cwa, sp)
        s["mb"] = _mixB_fwd(s["z"], wp, sp)
        s["mc"] = _mixC_fwd(s["z"], ws, bsc, sp)
        s["yd"] = _mixD_conv_fwd(s["z"], cwd, sp)
        xc, s["md"], s["yk"], s["mg"], s["mo"] = _merge_fwd(s["ma"], s["mb"], s["mc"], s["yd"], s["g"], gwb, gwo, xc, sp)
        s["x2"] = xc
        s["mn"], s["k"], s["v"] = _xa_kv(memf, sp, gxa)
        xc, s["hbx"], s["q"], s["o"], s["po"] = _xa_fwd(xc, s["k"], s["v"], sp, gxa)
        s["x3"] = xc
        xc, s["hb2"], s["a2"], s["b2"], s["y2"] = _ffn_fwd(xc, sp, "ffn2_pre_g", "ffn2_post_g", gf2, sp)
        saved.append(s)
        if nxt:
            packs[l + 1] = _exchange_wait(nxt, xc, f"gather_wait_{l + 1}")

    dx, lpart = _loss_head(xc, loss_target.reshape(S, D))
    loss = lax.psum(lpart[0, 0], ("x", "y", "c"))

    rep = {n: [None] * L for n in _REP_NAMES}
    recv = [None] * L
    pending = None
    for l in reversed(range(L)):
        gf1, gf2, gma, gwo, gxa, gwb = packs[l]
        sp = sp_all[l:l + 1]
        cwa, cwd = cwf[l, :KA], cwf[l, KA:]
        wp, ws, bsc = W["pool_w"][l], W["sgu_ws"][l], bsc_all[l]
        s = saved[l]

        dx, dyb, da, db, gp = _ffn_bwd_act(dx, s["x3"], s["y2"], s["a2"], s["b2"], sp, "ffn2_pre_g", "ffn2_post_g", gf2,
                                           pending[1]["token"] if pending else sp)
        rep["ffn2_pre_g"][l], rep["ffn2_post_g"][l] = gp[0], gp[1]
        d_f2 = _ffn_bwd_w(s["hb2"], dyb, s["a2"], s["b2"], da, db)

        dx, dpo, dq, dk, dv, gp = _xa_bwd_act(dx, s["x2"], s["po"], s["q"], s["k"], s["v"], sp, gxa)
        rep["xa_pre_g"][l], rep["xa_post_g"][l] = gp[0], gp[1]
        d_xa = _xa_bwd_w(s["hbx"], dq, s["o"], dpo, s["mn"], dk, dv)
        rep["mem_g"][l] = _xa_kv_bwd(memf, dk, dv, sp, gxa)[0]

        dmo, dm, dgp, dyk, gp = _merge_bwd_act(dx, s["mo"], s["g"], s["yk"], gwb, gwo, sp)
        rep["mix_post_g"][l] = gp[0]
        d_wb, d_wo = _merge_bwd_w(s["ma"], s["mb"], s["mc"], s["md"], dyk, s["mg"], dmo)
        dza, dcw, gp = _mixA_bwd(s["z"], dm[0], cwa, sp)
        rep["conv_a_w"][l], rep["conv_a_b"][l] = dcw, gp[0]
        dzb, dwp, gp = _mixB_bwd(s["z"], dm[1], wp, sp)
        rep["pool_w"][l], rep["pool_scale"][l] = dwp, gp[0]
        dzc, dws, dbs, gp = _mixC_bwd(s["z"], dm[2], ws, bsc, sp)
        rep["sgu_ws"][l], rep["sgu_b"][l], rep["sgu_ln_g"][l], rep["sgu_ln_b"][l] = dws, dbs[:, :, 0], gp[0], gp[1]
        dyd, gp = _mixD_ln_bwd(dm[3], s["yd"], sp)
        rep["conv_d_ln_g"][l], rep["conv_d_ln_b"][l] = gp[0], gp[1]
        dzd, dcw, gp = _mixD_conv_bwd(s["z"], dyd, cwd)
        rep["conv_d_w"][l], rep["conv_d_b"][l] = dcw, gp[0]
        dz = jnp.concatenate([dza, dzb, dzc, dzd], axis=0)
        dx, gp = _mix_in_bwd_act(dz, dgp, dx, s["x1"], sp, gma)
        rep["mix_pre_g"][l] = gp[0]
        d_ma, dbg = _mix_in_bwd_w(dz, dgp, s["hbm"])
        rep["b_gate"][l] = dbg[:, 0, :].reshape(-1)

        dx, dyb, da, db, gp = _ffn_bwd_act(dx, s["x0"], s["y1"], s["a1"], s["b1"], sp, "ffn1_pre_g", "ffn1_post_g", gf1, sp)
        rep["ffn1_pre_g"][l], rep["ffn1_post_g"][l] = gp[0], gp[1]
        d_f1 = _ffn_bwd_w(s["hb1"], dyb, s["a1"], s["b1"], da, db)

        if pending:
            recv[pending[0]] = _exchange_wait(pending[1], dx, f"scatter_wait_{pending[0]}")
        flat = jnp.concatenate([rep[n][l].reshape(-1) for n in _REP_NAMES])
        flat = jnp.pad(flat, (0, -flat.size % 1024)).reshape(-1, 128)
        pending = (l, _exchange_start([d_f1, d_f2, d_ma, d_wo, d_xa, d_wb, flat], (True,) * 6 + (False,), dx,
                                      f"scatter_start_{l}"))
    recv[pending[0]] = _exchange_wait(pending[1], dx, f"scatter_wait_{pending[0]}")

    G = {}
    stk = lambda i: jnp.stack([_slot_sum(recv[l][i].reshape(NS, -1, recv[l][i].shape[-1])) for l in range(L)])
    f1, f2, ma_, wo_, xa_, wb_ = (stk(i) for i in range(6))
    for nm, f in (("ffn1", f1), ("ffn2", f2)):
        G[nm + "_w1"], G[nm + "_w3"], G[nm + "_w2"] = _t(f[:, :FS]), _t(f[:, FS:2 * FS]), f[:, 2 * FS:]
    G["w_in"], G["w_gate"] = _t(ma_[:, :MW]), _t(ma_[:, MW:])
    G["w_o"] = wo_
    G["xa_wq"], G["xa_wk"], G["xa_wv"], G["xa_wo"] = (xa_[:, i * GW:(i + 1) * GW] for i in range(4))
    G["w_branch"] = wb_.reshape(W["w_branch"].shape)

    tot = [_slot_sum(recv[l][6]).reshape(-1) for l in range(L)]
    off = 0
    for n in _REP_NAMES:
        shape = (KA, NS * CS) if n == "conv_a_w" else (KD, NS * CS) if n == "conv_d_w" else W[n].shape[1:]
        size = 1
        for d in shape:
            size *= d
        G[n] = jnp.stack([tot[l][off:off + size].reshape(shape) for l in range(L)])
        off += size
    for n in ("conv_a_w", "conv_d_w"):
        G[n] = lax.dynamic_slice_in_dim(G[n], me * CS, CS, axis=2)

    deltas, new_m, new_v = {}, {}, {}
    for n in _W_NAMES:
        deltas[n], new_m[n], new_v[n] = _adamw(W[n], G[n], M[n], V[n])
    grad_x = dx.reshape(x.shape)
    return (loss, grad_x, *[G[n] for n in _W_NAMES], *[deltas[n] for n in _W_NAMES],
            *[new_m[n] for n in _W_NAMES], *[new_v[n] for n in _W_NAMES])


def kernel(x, mem, ffn1_pre_g, ffn1_post_g, ffn1_w1, ffn1_w3, ffn1_w2, mix_pre_g, mix_post_g, w_in, conv_a_w, conv_a_b, pool_w, pool_scale, sgu_ln_g, sgu_ln_b, sgu_ws, sgu_b, conv_d_w, conv_d_b, conv_d_ln_g, conv_d_ln_b, w_branch, w_gate, b_gate, w_o, xa_pre_g, xa_post_g, mem_g, xa_wq, xa_wk, xa_wv, xa_wo, ffn2_pre_g, ffn2_post_g, ffn2_w1, ffn2_w3, ffn2_w2, loss_target, m_ffn1_pre_g, m_ffn1_post_g, m_ffn1_w1, m_ffn1_w3, m_ffn1_w2, m_mix_pre_g, m_mix_post_g, m_w_in, m_conv_a_w, m_conv_a_b, m_pool_w, m_pool_scale, m_sgu_ln_g, m_sgu_ln_b, m_sgu_ws, m_sgu_b, m_conv_d_w, m_conv_d_b, m_conv_d_ln_g, m_conv_d_ln_b, m_w_branch, m_w_gate, m_b_gate, m_w_o, m_xa_pre_g, m_xa_post_g, m_mem_g, m_xa_wq, m_xa_wk, m_xa_wv, m_xa_wo, m_ffn2_pre_g, m_ffn2_post_g, m_ffn2_w1, m_ffn2_w3, m_ffn2_w2, v_ffn1_pre_g, v_ffn1_post_g, v_ffn1_w1, v_ffn1_w3, v_ffn1_w2, v_mix_pre_g, v_mix_post_g, v_w_in, v_conv_a_w, v_conv_a_b, v_pool_w, v_pool_scale, v_sgu_ln_g, v_sgu_ln_b, v_sgu_ws, v_sgu_b, v_conv_d_w, v_conv_d_b, v_conv_d_ln_g, v_conv_d_ln_b, v_w_branch, v_w_gate, v_b_gate, v_w_o, v_xa_pre_g, v_xa_post_g, v_mem_g, v_xa_wq, v_xa_wk, v_xa_wv, v_xa_wo, v_ffn2_pre_g, v_ffn2_post_g, v_ffn2_w1, v_ffn2_w3, v_ffn2_w2):
    args = dict(locals())
    W = {n: args[n] for n in _W_NAMES}
    M = {n: args["m_" + n] for n in _W_NAMES}
    V = {n: args["v_" + n] for n in _W_NAMES}
    return _step(x, mem, loss_target, W, M, V)
```

```python
import jax
import jax.numpy as jnp
from jax import lax
from jax.experimental import pallas as pl
from jax.experimental.pallas import tpu as pltpu

F32 = jnp.float32
CDT = jnp.bfloat16
EPS = 1e-6
NS = 8
GW = 128
MW = 512
CHUNK = 64
XA_HEADS = 4
POOL_WINDOWS = (2, 4, 8, 16)
VMEM_LIMIT = 56 * 1024 * 1024
ADAM_LR, ADAM_B1, ADAM_B2, ADAM_EPS, ADAM_WD, ADAM_STEP = 0.001, 0.9, 0.999, 1e-08, 0.01, 10

SDS = jax.ShapeDtypeStruct

_SP_NAMES = (("ffn1_pre_g", 1024), ("ffn1_post_g", 1024), ("mix_pre_g", 1024), ("mix_post_g", 1024),
             ("xa_pre_g", 1024), ("xa_post_g", 1024), ("mem_g", 1024), ("ffn2_pre_g", 1024), ("ffn2_post_g", 1024),
             ("conv_a_b", 512), ("pool_scale", 512), ("sgu_ln_g", 512), ("sgu_ln_b", 512), ("conv_d_b", 512),
             ("conv_d_ln_g", 512), ("conv_d_ln_b", 512), ("b_gate", 4096))
_SP = {}
_off = 0
for _n, _w in _SP_NAMES:
    _SP[_n] = (_off, _w)
    _off += _w
_SP_TOTAL = _off


def _call(body, name, grid, in_specs, out_specs, out_shape, scratch=()):
    return pl.pallas_call(
        body, name=name, grid=grid, in_specs=in_specs, out_specs=out_specs, out_shape=out_shape,
        scratch_shapes=list(scratch),
        compiler_params=pltpu.CompilerParams(dimension_semantics=("arbitrary",) * len(grid),
                                             vmem_limit_bytes=VMEM_LIMIT))


def _nn(a, b):
    return lax.dot_general(a, b, (((1,), (0,)), ((), ())), preferred_element_type=F32)


def _nt(a, b):
    return lax.dot_general(a, b, (((1,), (1,)), ((), ())), preferred_element_type=F32)


def _tn(a, b):
    return lax.dot_general(a, b, (((0,), (0,)), ((), ())), preferred_element_type=F32)


def _rms(x):
    r = lax.rsqrt(jnp.mean(x * x, axis=-1, keepdims=True) + EPS)
    return x * r, r


def _rms_bwd(n, r, g, dout):
    dn = dout * g
    dx = r * (dn - n * jnp.mean(dn * n, axis=-1, keepdims=True))
    return dx, jnp.sum(dout * n, axis=0, keepdims=True)


def _ln(y):
    mu = jnp.mean(y, axis=-1, keepdims=True)
    yc = y - mu
    rs = lax.rsqrt(jnp.mean(yc * yc, axis=-1, keepdims=True) + EPS)
    return yc * rs, rs


def _ln_bwd(xh, rs, dxh):
    return rs * (dxh - jnp.mean(dxh, axis=-1, keepdims=True) - xh * jnp.mean(dxh * xh, axis=-1, keepdims=True))


def _silu_parts(a):
    s = jax.nn.sigmoid(a)
    sl = a * s
    return sl, s + sl * (1.0 - s)


_GELU_C = 0.7978845608028654
_GELU_A = 0.044715


def _gelu(x):
    return 0.5 * x * (1.0 + jnp.tanh(_GELU_C * (x + _GELU_A * x * x * x)))


def _gelu_parts(x):
    t = jnp.tanh(_GELU_C * (x + _GELU_A * x * x * x))
    g = 0.5 * x * (1.0 + t)
    dg = 0.5 * (1.0 + t) + 0.5 * x * (1.0 - t * t) * _GELU_C * (1.0 + 3.0 * _GELU_A * x * x)
    return g, dg


def _spspec(name, width, imap):
    off = _SP[name][0]
    assert off % width == 0
    return pl.BlockSpec((1, width), lambda *a: (0, off // width + imap(*a)))


def _zero(*a):
    return 0


FFN_SG = 2


def _ffn_fwd(x, sp, pre, post, pf, dep):
    S, D = x.shape
    FS = pf.shape[1] // 3
    TM = min(512, S)
    SG, NG, W = FFN_SG, NS // FFN_SG, FFN_SG * FS

    def body(x_ref, pg_ref, qg_ref, w1_ref, w3_ref, w2_ref, dep_ref, xo_ref, hb_ref, a_ref, b_ref, y_ref, hb_s, acc):
        j = pl.program_id(1)

        @pl.when(j == 0)
        def _():
            n, _ = _rms(x_ref[...])
            hb = (n * pg_ref[...]).astype(CDT)
            hb_s[...] = hb
            hb_ref[...] = hb
            acc[...] = jnp.zeros_like(acc)

        hb = hb_s[...]
        a = _nt(hb, w1_ref[...].reshape(W, D))
        b = _nt(hb, w3_ref[...].reshape(W, D))
        a_ref[...] = a.astype(CDT)
        b_ref[...] = b.astype(CDT)
        u = (a * jax.nn.sigmoid(a) * b).astype(CDT)
        acc[...] += _nn(u, w2_ref[...].reshape(W, D))

        @pl.when(j == NG - 1)
        def _():
            y = acc[...]
            y_ref[...] = y.astype(CDT)
            n, _ = _rms(y)
            xo_ref[...] = x_ref[...] + 0.5 * (n * qg_ref[...])

    row = lambda i, j: (i, 0)
    grp = lambda i, j: (j, i, 0)
    return _call(
        body, "ffn_fwd", (S // TM, NG),
        [pl.BlockSpec((TM, D), row), _spspec(pre, D, _zero), _spspec(post, D, _zero),
         pl.BlockSpec((SG, FS, D), lambda i, j: (j, 0, 0)), pl.BlockSpec((SG, FS, D), lambda i, j: (j, 1, 0)),
         pl.BlockSpec((SG, FS, D), lambda i, j: (j, 2, 0)), pl.BlockSpec(memory_space=pl.ANY)],
        [pl.BlockSpec((TM, D), row), pl.BlockSpec((TM, D), row), pl.BlockSpec((None, TM, W), grp),
         pl.BlockSpec((None, TM, W), grp), pl.BlockSpec((TM, D), row)],
        [SDS((S, D), F32), SDS((S, D), CDT), SDS((NG, S, W), CDT), SDS((NG, S, W), CDT), SDS((S, D), CDT)],
        [pltpu.VMEM((TM, D), CDT), pltpu.VMEM((TM, D), F32)])(x, sp, sp, pf, pf, pf, dep)


def _ffn_bwd_act(dxo, x, y, a, b, sp, pre, post, pf, dep):
    S, D = x.shape
    FS = pf.shape[1] // 3
    TM = min(512, S)
    SG, NG, W = FFN_SG, NS // FFN_SG, FFN_SG * FS

    def body(dxo_ref, x_ref, y_ref, a_ref, b_ref, pg_ref, qg_ref, w1_ref, w3_ref, w2_ref, dep_ref,
             dx_ref, dyb_ref, da_ref, db_ref, gp_ref, dyb_s, acc):
        i = pl.program_id(0)
        j = pl.program_id(1)

        @pl.when((i == 0) & (j == 0))
        def _():
            gp_ref[...] = jnp.zeros_like(gp_ref)

        @pl.when(j == 0)
        def _():
            n, r = _rms(y_ref[...].astype(F32))
            dy, dg = _rms_bwd(n, r, qg_ref[...], 0.5 * dxo_ref[...])
            dyb = dy.astype(CDT)
            dyb_s[...] = dyb
            dyb_ref[...] = dyb
            gp_ref[1:2, :] += dg
            acc[...] = jnp.zeros_like(acc)

        sl, dsl = _silu_parts(a_ref[...].astype(F32))
        du = _nt(dyb_s[...], w2_ref[...].reshape(W, D))
        db = (du * sl).astype(CDT)
        da = (du * b_ref[...].astype(F32) * dsl).astype(CDT)
        da_ref[...] = da
        db_ref[...] = db
        acc[...] += _nn(da, w1_ref[...].reshape(W, D)) + _nn(db, w3_ref[...].reshape(W, D))

        @pl.when(j == NG - 1)
        def _():
            n, r = _rms(x_ref[...])
            dx, dg = _rms_bwd(n, r, pg_ref[...], acc[...])
            dx_ref[...] = dxo_ref[...] + dx
            gp_ref[0:1, :] += dg

    row = lambda i, j: (i, 0)
    grp = lambda i, j: (j, i, 0)
    return _call(
        body, "ffn_bwd_act", (S // TM, NG),
        [pl.BlockSpec((TM, D), row), pl.BlockSpec((TM, D), row), pl.BlockSpec((TM, D), row),
         pl.BlockSpec((None, TM, W), grp), pl.BlockSpec((None, TM, W), grp),
         _spspec(pre, D, _zero), _spspec(post, D, _zero),
         pl.BlockSpec((SG, FS, D), lambda i, j: (j, 0, 0)), pl.BlockSpec((SG, FS, D), lambda i, j: (j, 1, 0)),
         pl.BlockSpec((SG, FS, D), lambda i, j: (j, 2, 0)), pl.BlockSpec(memory_space=pl.ANY)],
        [pl.BlockSpec((TM, D), row), pl.BlockSpec((TM, D), row), pl.BlockSpec((None, TM, W), grp),
         pl.BlockSpec((None, TM, W), grp), pl.BlockSpec((8, D), lambda i, j: (0, 0))],
        [SDS((S, D), F32), SDS((S, D), CDT), SDS((NG, S, W), CDT), SDS((NG, S, W), CDT), SDS((8, D), F32)],
        [pltpu.VMEM((TM, D), CDT), pltpu.VMEM((TM, D), F32)])(dxo, x, y, a, b, sp, sp, pf, pf, pf, dep)


def _ffn_bwd_w(hb, dyb, a, b, da, db):
    S, D = hb.shape
    SG, NG = FFN_SG, NS // FFN_SG
    W = a.shape[2]
    FS = W // SG
    TK = min(512, S)
    NK = S // TK

    def body(hb_ref, dyb_ref, a_ref, b_ref, da_ref, db_ref, g_ref, acc):
        k = pl.program_id(1)

        @pl.when(k == 0)
        def _():
            acc[...] = jnp.zeros_like(acc)

        af = a_ref[...].astype(F32)
        u = (af * jax.nn.sigmoid(af) * b_ref[...].astype(F32)).astype(CDT)
        hb = hb_ref[...]
        acc[0:W, :] += _tn(da_ref[...], hb)
        acc[W:2 * W, :] += _tn(db_ref[...], hb)
        acc[2 * W:3 * W, :] += _tn(u, dyb_ref[...])

        @pl.when(k == NK - 1)
        def _():
            for s in range(SG):
                for r in range(3):
                    g_ref[s, r * FS:(r + 1) * FS, :] = acc[r * W + s * FS:r * W + (s + 1) * FS, :].astype(CDT)

    row = lambda j, k: (k, 0)
    grp = lambda j, k: (j, k, 0)
    return _call(
        body, "ffn_bwd_w", (NG, NK),
        [pl.BlockSpec((TK, D), row), pl.BlockSpec((TK, D), row)] + [pl.BlockSpec((None, TK, W), grp)] * 4,
        pl.BlockSpec((SG, 3 * FS, D), lambda j, k: (j, 0, 0)),
        SDS((NS, 3 * FS, D), CDT),
        [pltpu.VMEM((3 * W, D), F32)])(hb, dyb, a, b, da, db)


def _mix_in(x, sp, pma):
    S, D = x.shape
    TM = min(512, S)

    def body(x_ref, pg_ref, bg_ref, wi_ref, wg_ref, hb_ref, z_ref, g_ref, hb_s):
        @pl.when(pl.program_id(1) == 0)
        def _():
            n, _ = _rms(x_ref[...])
            hb = (n * pg_ref[...]).astype(CDT)
            hb_s[...] = hb
            hb_ref[...] = hb

        hb = hb_s[...]
        z_ref[...] = _nt(hb, wi_ref[...]).astype(CDT)
        g_ref[...] = jax.nn.sigmoid(_nt(hb, wg_ref[...]) + bg_ref[...]).astype(CDT)

    row = lambda i, j: (i, 0)
    return _call(
        body, "mix_in", (S // TM, NS),
        [pl.BlockSpec((TM, D), row), _spspec("mix_pre_g", D, _zero), _spspec("b_gate", MW, lambda i, j: j),
         pl.BlockSpec((None, MW, D), lambda i, j: (j, 0, 0)), pl.BlockSpec((None, MW, D), lambda i, j: (j, 1, 0))],
        [pl.BlockSpec((TM, D), row), pl.BlockSpec((None, TM, MW), lambda i, j: (j, i, 0)),
         pl.BlockSpec((None, TM, MW), lambda i, j: (j // 2, i, j % 2))],
        [SDS((S, D), CDT), SDS((NS, S, MW), CDT), SDS((4, S, D), CDT)],
        [pltpu.VMEM((TM, D), CDT)])(x, sp, sp, pma, pma)


def _mix_in_bwd_act(dz, dgp, dxr, x, sp, pma):
    S, D = x.shape
    TM = min(512, S)

    def body(dz_ref, dg_ref, dxr_ref, x_ref, pg_ref, wi_ref, wg_ref, dx_ref, gp_ref, acc):
        i = pl.program_id(0)
        j = pl.program_id(1)

        @pl.when((i == 0) & (j == 0))
        def _():
            gp_ref[...] = jnp.zeros_like(gp_ref)

        @pl.when(j == 0)
        def _():
            acc[...] = jnp.zeros_like(acc)

        acc[...] += _nn(dz_ref[...], wi_ref[...]) + _nn(dg_ref[...], wg_ref[...])

        @pl.when(j == NS - 1)
        def _():
            n, r = _rms(x_ref[...])
            dx, dg = _rms_bwd(n, r, pg_ref[...], acc[...])
            dx_ref[...] = dxr_ref[...] + dx
            gp_ref[0:1, :] += dg

    row = lambda i, j: (i, 0)
    return _call(
        body, "mix_in_bwd_act", (S // TM, NS),
        [pl.BlockSpec((None, TM, MW), lambda i, j: (j, i, 0)), pl.BlockSpec((None, TM, MW), lambda i, j: (j // 2, i, j % 2)),
         pl.BlockSpec((TM, D), row), pl.BlockSpec((TM, D), row), _spspec("mix_pre_g", D, _zero),
         pl.BlockSpec((None, MW, D), lambda i, j: (j, 0, 0)), pl.BlockSpec((None, MW, D), lambda i, j: (j, 1, 0))],
        [pl.BlockSpec((TM, D), row), pl.BlockSpec((8, D), lambda i, j: (0, 0))],
        [SDS((S, D), F32), SDS((8, D), F32)],
        [pltpu.VMEM((TM, D), F32)])(dz, dgp, dxr, x, sp, pma, pma)


def _mix_in_bwd_w(dz, dgp, hb):
    S, D = hb.shape
    TK = min(512, S)
    NK = S // TK

    def body(dz_ref, dg_ref, hb_ref, g_ref, bg_ref, acc):
        k = pl.program_id(1)

        @pl.when(k == 0)
        def _():
            acc[...] = jnp.zeros_like(acc)
            bg_ref[...] = jnp.zeros_like(bg_ref)

        hb = hb_ref[...]
        dg = dg_ref[...]
        acc[0:MW, :] += _tn(dz_ref[...], hb)
        acc[MW:2 * MW, :] += _tn(dg, hb)
        bg_ref[0:1, :] += jnp.sum(dg.astype(F32), axis=0, keepdims=True)

        @pl.when(k == NK - 1)
        def _():
            g_ref[...] = acc[...].astype(CDT)

    return _call(
        body, "mix_in_bwd_w", (NS, NK),
        [pl.BlockSpec((None, TK, MW), lambda j, k: (j, k, 0)), pl.BlockSpec((None, TK, MW), lambda j, k: (j // 2, k, j % 2)),
         pl.BlockSpec((TK, D), lambda j, k: (k, 0))],
        [pl.BlockSpec((None, 2 * MW, D), lambda j, k: (j, 0, 0)), pl.BlockSpec((None, 8, MW), lambda j, k: (j, 0, 0))],
        [SDS((NS, 2 * MW, D), CDT), SDS((NS, 8, MW), F32)],
        [pltpu.VMEM((2 * MW, D), F32)])(dz, dgp, hb)


def _causal_taps(pad_ref, i, ch, halo, k_taps, lanes=slice(None)):
    val = pad_ref[pl.ds(pl.multiple_of(i * ch, 8), ch + halo), lanes]
    out = []
    for k in range(k_taps):
        s = k_taps - 1 - k
        out.append((k, (pltpu.roll(val, s, 0) if s else val)[halo:, :]))
    return out


def _anti_taps(pad_ref, i, ch, halo, k_taps, lanes=slice(None)):
    val = pad_ref[pl.ds(pl.multiple_of(i * ch, 8), ch + halo), lanes]
    n = ch + halo
    out = []
    for k in range(k_taps):
        s = k_taps - 1 - k
        out.append((k, (pltpu.roll(val, n - s, 0) if s else val)[:ch, :]))
    return out


def _conv_geometry(S, k_taps):
    halo = 8 * ((k_taps - 1 + 7) // 8)
    ch = min(256, S)
    return halo, ch, S // ch


def _rows(i, ch):
    return pl.ds(pl.multiple_of(i * ch, ch), ch)


def _mixA_fwd(z, cw, sp):
    S = z.shape[1]
    K = cw.shape[0]
    H, CH, NCH = _conv_geometry(S, K)

    def body(z_ref, w_ref, b_ref, o_ref, pad):
        pad[0:H, :] = jnp.zeros((H, GW), F32)

        def fill(i, c):
            r = _rows(i, CH)
            pad[pl.ds(pl.multiple_of(i * CH + H, 8), CH), :] = z_ref[2, r, :].astype(F32) * z_ref[0, r, :].astype(F32)
            return c

        lax.fori_loop(0, NCH, fill, 0)

        def conv(i, c):
            r = _rows(i, CH)
            acc = jnp.zeros((CH, GW), F32)
            for k, sh in _causal_taps(pad, i, CH, H, K):
                acc = acc + w_ref[k:k + 1, :] * sh
            o_ref[r, :] = (z_ref[1, r, :].astype(F32) * (acc + b_ref[...])).astype(CDT)
            return c

        lax.fori_loop(0, NCH, conv, 0)

    return _call(
        body, "mixA_fwd", (MW // GW,),
        [pl.BlockSpec((3, S, GW), lambda c: (0, 0, c)), pl.BlockSpec((K, GW), lambda c: (0, c)),
         _spspec("conv_a_b", GW, lambda c: c)],
        pl.BlockSpec((S, GW), lambda c: (0, c)), SDS((S, MW), CDT),
        [pltpu.VMEM((H + S, GW), F32)])(z, cw, sp)


def _mixA_bwd(z, dm, cw, sp):
    S = z.shape[1]
    K = cw.shape[0]
    H, CH, NCH = _conv_geometry(S, K)

    def body(z_ref, dm_ref, w_ref, b_ref, dz_ref, dw_ref, db_ref, pad, dpad, dw_s):
        pad[0:H, :] = jnp.zeros((H, GW), F32)
        dpad[pl.ds(S, H), :] = jnp.zeros((H, GW), F32)
        dw_s[...] = jnp.zeros_like(dw_s)
        db_ref[...] = jnp.zeros_like(db_ref)

        def fill(i, c):
            r = _rows(i, CH)
            pad[pl.ds(pl.multiple_of(i * CH + H, 8), CH), :] = z_ref[2, r, :].astype(F32) * z_ref[0, r, :].astype(F32)
            return c

        lax.fori_loop(0, NCH, fill, 0)

        def p1(i, c):
            r = _rows(i, CH)
            taps = _causal_taps(pad, i, CH, H, K)
            acc = jnp.zeros((CH, GW), F32)
            for k, sh in taps:
                acc = acc + w_ref[k:k + 1, :] * sh
            dmf = dm_ref[r, :].astype(F32)
            dz_ref[1, r, :] = (dmf * (acc + b_ref[...])).astype(CDT)
            dc = dmf * z_ref[1, r, :].astype(F32)
            dpad[r, :] = dc
            for k, sh in taps:
                dw_s[k:k + 1, :] += jnp.sum(dc * sh, axis=0, keepdims=True)
            db_ref[0:1, :] += jnp.sum(dc, axis=0, keepdims=True)
            return c

        lax.fori_loop(0, NCH, p1, 0)

        def p2(i, c):
            r = _rows(i, CH)
            dq = jnp.zeros((CH, GW), F32)
            for k, sh in _anti_taps(dpad, i, CH, H, K):
                dq = dq + w_ref[k:k + 1, :] * sh
            dz_ref[0, r, :] = (dq * z_ref[2, r, :].astype(F32)).astype(CDT)
            dz_ref[2, r, :] = (dq * z_ref[0, r, :].astype(F32)).astype(CDT)
            return c

        lax.fori_loop(0, NCH, p2, 0)
        dw_ref[...] = dw_s[0:K, :]

    return _call(
        body, "mixA_bwd", (MW // GW,),
        [pl.BlockSpec((3, S, GW), lambda c: (0, 0, c)), pl.BlockSpec((S, GW), lambda c: (0, c)),
         pl.BlockSpec((K, GW), lambda c: (0, c)), _spspec("conv_a_b", GW, lambda c: c)],
        [pl.BlockSpec((3, S, GW), lambda c: (0, 0, c)), pl.BlockSpec((K, GW), lambda c: (0, c)),
         pl.BlockSpec((8, GW), lambda c: (0, c))],
        [SDS((3, S, MW), CDT), SDS((K, MW), F32), SDS((8, MW), F32)],
        [pltpu.VMEM((H + S, GW), F32), pltpu.VMEM((S + H, GW), F32), pltpu.VMEM((8 * ((K + 7) // 8), GW), F32)])(z, dm, cw, sp)


def _mixD_conv_fwd(z, cw, sp):
    S = z.shape[1]
    K = cw.shape[0]
    H, CH, NCH = _conv_geometry(S, K)

    def body(z_ref, w_ref, b_ref, o_ref, pad):
        pad[0:H, :] = jnp.zeros((H, GW), F32)

        def fill(i, c):
            r = _rows(i, CH)
            pad[pl.ds(pl.multiple_of(i * CH + H, 8), CH), :] = (
                z_ref[0, r, :].astype(F32) * jax.nn.sigmoid(z_ref[1, r, :].astype(F32)))
            return c

        lax.fori_loop(0, NCH, fill, 0)

        def conv(i, c):
            acc = jnp.zeros((CH, GW), F32)
            for k, sh in _causal_taps(pad, i, CH, H, K):
                acc = acc + w_ref[k:k + 1, :] * sh
            o_ref[_rows(i, CH), :] = (acc + b_ref[...]).astype(CDT)
            return c

        lax.fori_loop(0, NCH, conv, 0)

    return _call(
        body, "mixD_conv_fwd", (MW // GW,),
        [pl.BlockSpec((2, S, GW), lambda c: (3, 0, c)), pl.BlockSpec((K, GW), lambda c: (0, c)),
         _spspec("conv_d_b", GW, lambda c: c)],
        pl.BlockSpec((S, GW), lambda c: (0, c)), SDS((S, MW), CDT),
        [pltpu.VMEM((H + S, GW), F32)])(z, cw, sp)


def _mixD_conv_bwd(z, dy, cw):
    S = z.shape[1]
    K = cw.shape[0]
    H, CH, NCH = _conv_geometry(S, K)

    def body(z_ref, dy_ref, w_ref, dz_ref, dw_ref, db_ref, pad, dpad, dw_s):
        pad[0:H, :] = jnp.zeros((H, GW), F32)
        dpad[pl.ds(S, H), :] = jnp.zeros((H, GW), F32)
        dw_s[...] = jnp.zeros_like(dw_s)
        db_ref[...] = jnp.zeros_like(db_ref)

        def fill(i, c):
            r = _rows(i, CH)
            pad[pl.ds(pl.multiple_of(i * CH + H, 8), CH), :] = (
                z_ref[0, r, :].astype(F32) * jax.nn.sigmoid(z_ref[1, r, :].astype(F32)))
            dpad[r, :] = dy_ref[r, :].astype(F32)
            return c

        lax.fori_loop(0, NCH, fill, 0)

        def p1(i, c):
            dyf = dy_ref[_rows(i, CH), :].astype(F32)
            for k, sh in _causal_taps(pad, i, CH, H, K):
                dw_s[k:k + 1, :] += jnp.sum(dyf * sh, axis=0, keepdims=True)
            db_ref[0:1, :] += jnp.sum(dyf, axis=0, keepdims=True)
            return c

        lax.fori_loop(0, NCH, p1, 0)

        def p2(i, c):
            r = _rows(i, CH)
            dq = jnp.zeros((CH, GW), F32)
            for k, sh in _anti_taps(dpad, i, CH, H, K):
                dq = dq + w_ref[k:k + 1, :] * sh
            a = z_ref[0, r, :].astype(F32)
            sg = jax.nn.sigmoid(z_ref[1, r, :].astype(F32))
            dz_ref[0, r, :] = (dq * sg).astype(CDT)
            dz_ref[1, r, :] = (dq * a * sg * (1.0 - sg)).astype(CDT)
            return c

        lax.fori_loop(0, NCH, p2, 0)
        dw_ref[...] = dw_s[0:K, :]

    return _call(
        body, "mixD_conv_bwd", (MW // GW,),
        [pl.BlockSpec((2, S, GW), lambda c: (3, 0, c)), pl.BlockSpec((S, GW), lambda c: (0, c)),
         pl.BlockSpec((K, GW), lambda c: (0, c))],
        [pl.BlockSpec((2, S, GW), lambda c: (0, 0, c)), pl.BlockSpec((K, GW), lambda c: (0, c)),
         pl.BlockSpec((8, GW), lambda c: (0, c))],
        [SDS((2, S, MW), CDT), SDS((K, MW), F32), SDS((8, MW), F32)],
        [pltpu.VMEM((H + S, GW), F32), pltpu.VMEM((S + H, GW), F32), pltpu.VMEM((8 * ((K + 7) // 8), GW), F32)])(z, dy, cw)


def _mixD_ln_bwd(dm, yd, sp):
    S = yd.shape[0]
    TM = min(512, S)

    def body(dm_ref, y_ref, lg_ref, lb_ref, dy_ref, gp_ref):
        @pl.when(pl.program_id(0) == 0)
        def _():
            gp_ref[...] = jnp.zeros_like(gp_ref)

        xh, rs = _ln(y_ref[...].astype(F32))
        _, dsl = _silu_parts(xh * lg_ref[...] + lb_ref[...])
        dl = dm_ref[...].astype(F32) * dsl
        gp_ref[0:1, :] += jnp.sum(dl * xh, axis=0, keepdims=True)
        gp_ref[1:2, :] += jnp.sum(dl, axis=0, keepdims=True)
        dy_ref[...] = _ln_bwd(xh, rs, dl * lg_ref[...]).astype(CDT)

    row = lambda i: (i, 0)
    return _call(
        body, "mixD_ln_bwd", (S // TM,),
        [pl.BlockSpec((TM, MW), row), pl.BlockSpec((TM, MW), row), _spspec("conv_d_ln_g", MW, _zero),
         _spspec("conv_d_ln_b", MW, _zero)],
        [pl.BlockSpec((TM, MW), row), pl.BlockSpec((8, MW), lambda i: (0, 0))],
        [SDS((S, MW), CDT), SDS((8, MW), F32)])(dm, yd, sp, sp)


def _box_causal(val, g):
    s = val
    for d in range(g + 1):
        s = s + pltpu.roll(s, 1 << d, 0)
    return s


def _box_anti(val, g):
    n = val.shape[0]
    s = val
    for d in range(g + 1):
        s = s + pltpu.roll(s, n - (1 << d), 0)
    return s


def _pool_count(i, ch, win):
    t = lax.broadcasted_iota(jnp.int32, (ch, GW), 0) + (i * ch + 1)
    return jnp.minimum(t, win).astype(F32)


def _mixB_fwd(z, wp, sp):
    S = z.shape[1]
    H, CH = 16, min(256, S)
    NCH = S // CH
    assert POOL_WINDOWS == tuple(2 << g for g in range(4))

    def body(p_ref, wp_ref, sc_ref, o_ref, pad):
        pad[0:H, :] = jnp.zeros((H, MW), F32)

        def fill(i, c):
            pad[pl.ds(pl.multiple_of(i * CH + H, 8), CH), :] = p_ref[_rows(i, CH), :].astype(F32)
            return c

        lax.fori_loop(0, NCH, fill, 0)

        def step(i, c):
            r = _rows(i, CH)
            for g in range(4):
                gs = slice(g * GW, (g + 1) * GW)
                val = pad[pl.ds(pl.multiple_of(i * CH, 8), CH + H), gs]
                pooled = _box_causal(val, g)[H:, :] / _pool_count(i, CH, POOL_WINDOWS[g]) - val[H:, :]
                mixed = _nn(pooled.astype(CDT), wp_ref[g].astype(CDT))
                o_ref[r, gs] = (mixed * sc_ref[:, gs]).astype(CDT)
            return c

        lax.fori_loop(0, NCH, step, 0)

    return _call(
        body, "mixB_fwd", (1,),
        [pl.BlockSpec((None, S, MW), lambda i: (3, 0, 0)), pl.BlockSpec((4, GW, GW), lambda i: (0, 0, 0)),
         _spspec("pool_scale", MW, _zero)],
        pl.BlockSpec((S, MW), lambda i: (0, 0)), SDS((S, MW), CDT),
        [pltpu.VMEM((H + S, MW), F32)])(z, wp, sp)


def _mixB_bwd(z, dm, wp, sp):
    S = z.shape[1]
    H, CH = 16, min(256, S)
    NCH = S // CH

    def body(p_ref, dm_ref, wp_ref, sc_ref, dz_ref, dwp_ref, dsc_ref, pad, rpad):
        pad[0:H, :] = jnp.zeros((H, MW), F32)
        rpad[pl.ds(S, H), :] = jnp.zeros((H, MW), F32)
        dwp_ref[...] = jnp.zeros_like(dwp_ref)
        dsc_ref[...] = jnp.zeros_like(dsc_ref)

        def fill(i, c):
            pad[pl.ds(pl.multiple_of(i * CH + H, 8), CH), :] = p_ref[_rows(i, CH), :].astype(F32)
            return c

        lax.fori_loop(0, NCH, fill, 0)

        def p1(i, c):
            r = _rows(i, CH)
            for g in range(4):
                gs = slice(g * GW, (g + 1) * GW)
                cnt = _pool_count(i, CH, POOL_WINDOWS[g])
                val = pad[pl.ds(pl.multiple_of(i * CH, 8), CH + H), gs]
                pooled = (_box_causal(val, g)[H:, :] / cnt - val[H:, :]).astype(CDT)
                w = wp_ref[g].astype(CDT)
                mixed = _nn(pooled, w)
                dmf = dm_ref[r, gs].astype(F32)
                dsc_ref[0:1, gs] += jnp.sum(dmf * mixed, axis=0, keepdims=True)
                dmx = (dmf * sc_ref[:, gs]).astype(CDT)
                dwp_ref[g] += _tn(pooled, dmx)
                rpad[r, gs] = _nt(dmx, w) / cnt
            return c

        lax.fori_loop(0, NCH, p1, 0)

        def p2(i, c):
            r = _rows(i, CH)
            for g in range(4):
                gs = slice(g * GW, (g + 1) * GW)
                val = rpad[pl.ds(pl.multiple_of(i * CH, 8), CH + H), gs]
                dp = _box_anti(val, g)[:CH, :] - val[:CH, :] * _pool_count(i, CH, POOL_WINDOWS[g])
                dz_ref[r, gs] = dp.astype(CDT)
            return c

        lax.fori_loop(0, NCH, p2, 0)

    return _call(
        body, "mixB_bwd", (1,),
        [pl.BlockSpec((None, S, MW), lambda i: (3, 0, 0)), pl.BlockSpec((S, MW), lambda i: (0, 0)),
         pl.BlockSpec((4, GW, GW), lambda i: (0, 0, 0)), _spspec("pool_scale", MW, _zero)],
        [pl.BlockSpec((None, S, MW), lambda i: (0, 0, 0)), pl.BlockSpec((4, GW, GW), lambda i: (0, 0, 0)),
         pl.BlockSpec((8, MW), lambda i: (0, 0))],
        [SDS((1, S, MW), CDT), SDS((4, GW, GW), F32), SDS((8, MW), F32)],
        [pltpu.VMEM((H + S, MW), F32), pltpu.VMEM((S + H, MW), F32)])(z, dm, wp, sp)


def _sgu_mask():
    ci = lax.broadcasted_iota(jnp.int32, (GW, GW), 0) // CHUNK
    cj = lax.broadcasted_iota(jnp.int32, (GW, GW), 1) // CHUNK
    return cj <= ci


def _mixC_fwd(z, ws, bsc, sp):
    S = z.shape[1]
    RB = min(512, S)

    def body(z_ref, lg_ref, lb_ref, ws_ref, bs_ref, o_ref):
        mask = _sgu_mask()
        gu = _gelu(z_ref[0].astype(F32))
        xh, _ = _ln(_gelu(z_ref[1].astype(F32)))
        vn = (xh * lg_ref[...] + lb_ref[...]).astype(CDT)
        for g in range(4):
            gs = slice(g * GW, (g + 1) * GW)
            wm = jnp.where(mask, ws_ref[g], 0.0).astype(CDT)
            for nb in range(RB // GW):
                rs = slice(nb * GW, (nb + 1) * GW)
                mixed = _nn(wm, vn[rs, gs]) + bs_ref[g]
                o_ref[rs, gs] = (gu[rs, gs] * mixed).astype(CDT)

    return _call(
        body, "mixC_fwd", (S // RB,),
        [pl.BlockSpec((2, RB, MW), lambda i: (2, i, 0)), _spspec("sgu_ln_g", MW, _zero), _spspec("sgu_ln_b", MW, _zero),
         pl.BlockSpec((4, GW, GW), lambda i: (0, 0, 0)), pl.BlockSpec((4, GW, 1), lambda i: (0, 0, 0))],
        pl.BlockSpec((RB, MW), lambda i: (i, 0)), SDS((S, MW), CDT))(z, sp, sp, ws, bsc)


def _mixC_bwd(z, dm, ws, bsc, sp):
    S = z.shape[1]
    RB = min(512, S)
    NR = S // RB

    def body(z_ref, dm_ref, lg_ref, lb_ref, ws_ref, bs_ref, dz_ref, dws_ref, dbs_ref, gp_ref, dvn_s):
        i = pl.program_id(0)

        @pl.when(i == 0)
        def _():
            dws_ref[...] = jnp.zeros_like(dws_ref)
            dbs_ref[...] = jnp.zeros_like(dbs_ref)
            gp_ref[...] = jnp.zeros_like(gp_ref)

        mask = _sgu_mask()
        gu, dgu = _gelu_parts(z_ref[0].astype(F32))
        gv, dgv = _gelu_parts(z_ref[1].astype(F32))
        xh, rs_ = _ln(gv)
        vn = (xh * lg_ref[...] + lb_ref[...]).astype(CDT)
        dmf = dm_ref[...].astype(F32)
        for g in range(4):
            gs = slice(g * GW, (g + 1) * GW)
            wm = jnp.where(mask, ws_ref[g], 0.0).astype(CDT)
            for nb in range(RB // GW):
                rs = slice(nb * GW, (nb + 1) * GW)
                vb = vn[rs, gs]
                mixed = _nn(wm, vb) + bs_ref[g]
                dz_ref[0, rs, gs] = (dmf[rs, gs] * mixed * dgu[rs, gs]).astype(CDT)
                dmx = dmf[rs, gs] * gu[rs, gs]
                dbs_ref[g] += dmx
                dmxc = dmx.astype(CDT)
                dws_ref[g] += _nt(dmxc, vb)
                dvn_s[rs, gs] = _tn(wm, dmxc)
        dvn = dvn_s[...]
        gp_ref[0:1, :] += jnp.sum(dvn * xh, axis=0, keepdims=True)
        gp_ref[1:2, :] += jnp.sum(dvn, axis=0, keepdims=True)
        dz_ref[1] = (_ln_bwd(xh, rs_, dvn * lg_ref[...]) * dgv).astype(CDT)

        @pl.when(i == NR - 1)
        def _():
            for g in range(4):
                dws_ref[g] = jnp.where(mask, dws_ref[g], 0.0)
                dbs_ref[g] = jnp.broadcast_to(jnp.sum(dbs_ref[g], axis=1, keepdims=True), (GW, GW))

    full3 = lambda i: (0, 0, 0)
    return _call(
        body, "mixC_bwd", (NR,),
        [pl.BlockSpec((2, RB, MW), lambda i: (2, i, 0)), pl.BlockSpec((RB, MW), lambda i: (i, 0)),
         _spspec("sgu_ln_g", MW, _zero), _spspec("sgu_ln_b", MW, _zero),
         pl.BlockSpec((4, GW, GW), full3), pl.BlockSpec((4, GW, 1), full3)],
        [pl.BlockSpec((2, RB, MW), lambda i: (0, i, 0)), pl.BlockSpec((4, GW, GW), full3), pl.BlockSpec((4, GW, GW), full3),
         pl.BlockSpec((8, MW), lambda i: (0, 0))],
        [SDS((2, S, MW), CDT), SDS((4, GW, GW), F32), SDS((4, GW, GW), F32), SDS((8, MW), F32)],
        [pltpu.VMEM((RB, MW), F32)])(z, dm, sp, sp, ws, bsc)


def _unpack_wb(wb_ref, wbf):
    for j in range(NS):
        for k in range(4):
            wbf[k, :, j * GW:(j + 1) * GW] = wb_ref[j, k]


def _merge_fwd(ma, mb, mc, yd, g, wb, pwo, x, sp):
    S, D = x.shape
    TM = min(256, S)

    def body(ma_ref, mb_ref, mc_ref, yd_ref, g_ref, wb_ref, wo_ref, x_ref, lg_ref, lb_ref, qg_ref,
             xo_ref, md_ref, yk_ref, mg_ref, mo_ref, wbf):
        @pl.when(pl.program_id(0) == 0)
        def _():
            _unpack_wb(wb_ref, wbf)

        xh, _ = _ln(yd_ref[...].astype(F32))
        sl, _ = _silu_parts(xh * lg_ref[...] + lb_ref[...])
        md = sl.astype(CDT)
        md_ref[...] = md
        merged = jnp.zeros((TM, D), F32)
        for k, m in enumerate((ma_ref[...], mb_ref[...], mc_ref[...], md)):
            yk = _nn(m, wbf[k])
            yk_ref[k] = yk.astype(CDT)
            merged = merged + g_ref[k].astype(F32) * yk
        mgc = merged.astype(CDT)
        mg_ref[...] = mgc
        mo = _nn(mgc, wo_ref[...].reshape(D, D))
        mo_ref[...] = mo.astype(CDT)
        n, _ = _rms(mo)
        xo_ref[...] = x_ref[...] + n * qg_ref[...]

    row = lambda i: (i, 0)
    rowm = pl.BlockSpec((TM, MW), row)
    rowd = pl.BlockSpec((TM, D), row)
    row4 = pl.BlockSpec((4, TM, D), lambda i: (0, i, 0))
    return _call(
        body, "merge_fwd", (S // TM,),
        [rowm, rowm, rowm, rowm, row4, pl.BlockSpec((NS, 4, MW, GW), lambda i: (0, 0, 0, 0)),
         pl.BlockSpec((NS, GW, D), lambda i: (0, 0, 0)), rowd,
         _spspec("conv_d_ln_g", MW, _zero), _spspec("conv_d_ln_b", MW, _zero), _spspec("mix_post_g", D, _zero)],
        [rowd, rowm, row4, rowd, rowd],
        [SDS((S, D), F32), SDS((S, MW), CDT), SDS((4, S, D), CDT), SDS((S, D), CDT), SDS((S, D), CDT)],
        [pltpu.VMEM((4, MW, D), CDT)])(ma, mb, mc, yd, g, wb, pwo, x, sp, sp, sp)


def _merge_bwd_act(dxo, mo, g, yk, wb, pwo, sp, dep):
    S, D = dxo.shape
    TM = min(256, S)

    def body(dxo_ref, mo_ref, g_ref, yk_ref, wb_ref, wo_ref, qg_ref, dep_ref, dmo_ref, dm_ref, dgp_ref, dyk_ref, gp_ref, wbf):
        @pl.when(pl.program_id(0) == 0)
        def _():
            gp_ref[...] = jnp.zeros_like(gp_ref)
            _unpack_wb(wb_ref, wbf)

        n, r = _rms(mo_ref[...].astype(F32))
        dmo, dg = _rms_bwd(n, r, qg_ref[...], dxo_ref[...])
        gp_ref[0:1, :] += dg
        dmoc = dmo.astype(CDT)
        dmo_ref[...] = dmoc
        dmg = _nt(dmoc, wo_ref[...].reshape(D, D))
        for k in range(4):
            gk = g_ref[k].astype(F32)
            dyk = (dmg * gk).astype(CDT)
            dyk_ref[k] = dyk
            dgp_ref[k] = (dmg * yk_ref[k].astype(F32) * gk * (1.0 - gk)).astype(CDT)
            dm_ref[k] = _nt(dyk, wbf[k]).astype(CDT)

    rowd = pl.BlockSpec((TM, D), lambda i: (i, 0))
    row4 = pl.BlockSpec((4, TM, D), lambda i: (0, i, 0))
    return _call(
        body, "merge_bwd_act", (S // TM,),
        [rowd, rowd, row4, row4, pl.BlockSpec((NS, 4, MW, GW), lambda i: (0, 0, 0, 0)),
         pl.BlockSpec((NS, GW, D), lambda i: (0, 0, 0)), _spspec("mix_post_g", D, _zero), _ANY],
        [rowd, pl.BlockSpec((4, TM, MW), lambda i: (0, i, 0)), row4, row4, pl.BlockSpec((8, D), lambda i: (0, 0))],
        [SDS((S, D), CDT), SDS((4, S, MW), CDT), SDS((4, S, D), CDT), SDS((4, S, D), CDT), SDS((8, D), F32)],
        [pltpu.VMEM((4, MW, D), CDT)])(dxo, mo, g, yk, wb, pwo, sp, dep)


def _merge_bwd_w(ma, mb, mc, md, dyk, mg, dmo):
    S, D = dmo.shape
    TK = min(512, S)
    NK = S // TK

    def body(ma_ref, mb_ref, mc_ref, md_ref, dyk_ref, mg_ref, dmo_ref, gwb_ref, gwo_ref, accb, acco):
        k = pl.program_id(0)

        @pl.when(k == 0)
        def _():
            accb[...] = jnp.zeros_like(accb)
            acco[...] = jnp.zeros_like(acco)

        for b, m in enumerate((ma_ref, mb_ref, mc_ref, md_ref)):
            accb[b] += _tn(m[...], dyk_ref[b])
        acco[...] += _tn(mg_ref[...], dmo_ref[...])

        @pl.when(k == NK - 1)
        def _():
            for j in range(NS):
                for b in range(4):
                    gwb_ref[j, b] = accb[b, :, j * GW:(j + 1) * GW].astype(CDT)
                gwo_ref[j] = acco[j * GW:(j + 1) * GW, :].astype(CDT)

    rowm = pl.BlockSpec((TK, MW), lambda k: (k, 0))
    rowd = pl.BlockSpec((TK, D), lambda k: (k, 0))
    return _call(
        body, "merge_bwd_w", (NK,),
        [rowm, rowm, rowm, rowm, pl.BlockSpec((4, TK, D), lambda k: (0, k, 0)), rowd, rowd],
        [pl.BlockSpec((NS, 4, MW, GW), lambda k: (0, 0, 0, 0)), pl.BlockSpec((NS, GW, D), lambda k: (0, 0, 0))],
        [SDS((NS, 4, MW, GW), CDT), SDS((NS, GW, D), CDT)],
        [pltpu.VMEM((4, MW, D), F32), pltpu.VMEM((D, D), F32)])(ma, mb, mc, md, dyk, mg, dmo)


def _xa_kv(mem, sp, pxa):
    M, D = mem.shape

    def body(m_ref, g_ref, wk_ref, wv_ref, mn_ref, k_ref, v_ref):
        n, _ = _rms(m_ref[...])
        mn = (n * g_ref[...]).astype(CDT)
        mn_ref[...] = mn
        k_ref[...] = _nn(mn, wk_ref[...].reshape(D, D)).astype(CDT)
        v_ref[...] = _nn(mn, wv_ref[...].reshape(D, D)).astype(CDT)

    full = pl.BlockSpec((M, D), lambda i: (0, 0))
    return _call(
        body, "xa_kv", (1,),
        [full, _spspec("mem_g", D, _zero), pl.BlockSpec((NS, GW, D), lambda i: (0, 1, 0)),
         pl.BlockSpec((NS, GW, D), lambda i: (0, 2, 0))],
        [full, full, full], [SDS((M, D), CDT)] * 3)(mem, sp, pxa, pxa)


def _softmax(s):
    e = jnp.exp(s - jnp.max(s, axis=-1, keepdims=True))
    return e / jnp.sum(e, axis=-1, keepdims=True)


def _xa_fwd(x, kk, vv, sp, pxa):
    S, D = x.shape
    M = kk.shape[0]
    TM = min(512, S)
    HD = D // XA_HEADS
    scale = HD ** -0.5

    def body(x_ref, k_ref, v_ref, pg_ref, qg_ref, wq_ref, wo_ref, xo_ref, hb_ref, q_ref, o_ref, po_ref):
        n, _ = _rms(x_ref[...])
        hb = (n * pg_ref[...]).astype(CDT)
        hb_ref[...] = hb
        q = _nn(hb, wq_ref[...].reshape(D, D)).astype(CDT)
        q_ref[...] = q
        for h in range(XA_HEADS):
            hs = slice(h * HD, (h + 1) * HD)
            p = _softmax(_nt(q[:, hs], k_ref[:, hs]) * scale)
            o_ref[:, hs] = _nn(p.astype(CDT), v_ref[:, hs]).astype(CDT)
        po = _nn(o_ref[...], wo_ref[...].reshape(D, D))
        po_ref[...] = po.astype(CDT)
        n, _ = _rms(po)
        xo_ref[...] = x_ref[...] + n * qg_ref[...]

    row = pl.BlockSpec((TM, D), lambda i: (i, 0))
    full = pl.BlockSpec((M, D), lambda i: (0, 0))
    return _call(
        body, "xa_fwd", (S // TM,),
        [row, full, full, _spspec("xa_pre_g", D, _zero), _spspec("xa_post_g", D, _zero),
         pl.BlockSpec((NS, GW, D), lambda i: (0, 0, 0)), pl.BlockSpec((NS, GW, D), lambda i: (0, 3, 0))],
        [row] * 5, [SDS((S, D), F32)] + [SDS((S, D), CDT)] * 4)(x, kk, vv, sp, sp, pxa, pxa)


def _xa_bwd_act(dxo, x, po, q, kk, vv, sp, pxa, dep):
    S, D = x.shape
    M = kk.shape[0]
    TM = min(512, S)
    HD = D // XA_HEADS
    scale = HD ** -0.5

    def body(dxo_ref, x_ref, po_ref, q_ref, k_ref, v_ref, pg_ref, qg_ref, wq_ref, wo_ref, dep_ref,
             dx_ref, dpo_ref, dq_ref, dk_ref, dv_ref, gp_ref):
        @pl.when(pl.program_id(0) == 0)
        def _():
            gp_ref[...] = jnp.zeros_like(gp_ref)
            dk_ref[...] = jnp.zeros_like(dk_ref)
            dv_ref[...] = jnp.zeros_like(dv_ref)

        n, r = _rms(po_ref[...].astype(F32))
        dpo, dg = _rms_bwd(n, r, qg_ref[...], dxo_ref[...])
        gp_ref[1:2, :] += dg
        dpoc = dpo.astype(CDT)
        dpo_ref[...] = dpoc
        do = _nt(dpoc, wo_ref[...].reshape(D, D)).astype(CDT)
        for h in range(XA_HEADS):
            hs = slice(h * HD, (h + 1) * HD)
            qh = q_ref[:, hs]
            p = _softmax(_nt(qh, k_ref[:, hs]) * scale)
            pc = p.astype(CDT)
            dv_ref[:, hs] += _tn(pc, do[:, hs])
            dp = _nt(do[:, hs], v_ref[:, hs])
            ds = (p * (dp - jnp.sum(p * dp, axis=-1, keepdims=True)) * scale).astype(CDT)
            dq_ref[:, hs] = _nn(ds, k_ref[:, hs]).astype(CDT)
            dk_ref[:, hs] += _tn(ds, qh)
        dhb = _nt(dq_ref[...], wq_ref[...].reshape(D, D))
        n, r = _rms(x_ref[...])
        dx, dg = _rms_bwd(n, r, pg_ref[...], dhb)
        dx_ref[...] = dxo_ref[...] + dx
        gp_ref[0:1, :] += dg

    row = pl.BlockSpec((TM, D), lambda i: (i, 0))
    full = pl.BlockSpec((M, D), lambda i: (0, 0))
    return _call(
        body, "xa_bwd_act", (S // TM,),
        [row, row, row, row, full, full, _spspec("xa_pre_g", D, _zero), _spspec("xa_post_g", D, _zero),
         pl.BlockSpec((NS, GW, D), lambda i: (0, 0, 0)), pl.BlockSpec((NS, GW, D), lambda i: (0, 3, 0)), _ANY],
        [row, row, row, full, full, pl.BlockSpec((8, D), lambda i: (0, 0))],
        [SDS((S, D), F32), SDS((S, D), CDT), SDS((S, D), CDT), SDS((M, D), F32), SDS((M, D), F32), SDS((8, D), F32)],
    )(dxo, x, po, q, kk, vv, sp, sp, pxa, pxa, dep)


def _xa_bwd_w(hb, dq, o, dpo, mn, dk, dv):
    S, D = hb.shape
    M = mn.shape[0]
    TK = min(512, S)
    NK = S // TK

    def body(hb_ref, dq_ref, o_ref, dpo_ref, mn_ref, dk_ref, dv_ref, g_ref, accq, acco):
        k = pl.program_id(0)

        @pl.when(k == 0)
        def _():
            accq[...] = jnp.zeros_like(accq)
            acco[...] = jnp.zeros_like(acco)

        accq[...] += _tn(hb_ref[...], dq_ref[...])
        acco[...] += _tn(o_ref[...], dpo_ref[...])

        @pl.when(k == NK - 1)
        def _():
            gk = _tn(mn_ref[...], dk_ref[...].astype(CDT))
            gv = _tn(mn_ref[...], dv_ref[...].astype(CDT))
            for j in range(NS):
                rs = slice(j * GW, (j + 1) * GW)
                g_ref[j, 0:GW, :] = accq[rs, :].astype(CDT)
                g_ref[j, GW:2 * GW, :] = gk[rs, :].astype(CDT)
                g_ref[j, 2 * GW:3 * GW, :] = gv[rs, :].astype(CDT)
                g_ref[j, 3 * GW:4 * GW, :] = acco[rs, :].astype(CDT)

    rowb = pl.BlockSpec((TK, D), lambda k: (k, 0))
    full = pl.BlockSpec((M, D), lambda k: (0, 0))
    return _call(
        body, "xa_bwd_w", (NK,),
        [rowb, rowb, rowb, rowb, full, full, full],
        pl.BlockSpec((NS, 4 * GW, D), lambda k: (0, 0, 0)), SDS((NS, 4 * GW, D), CDT),
        [pltpu.VMEM((D, D), F32), pltpu.VMEM((D, D), F32)])(hb, dq, o, dpo, mn, dk, dv)


def _xa_kv_bwd(mem, dk, dv, sp, pxa):
    M, D = mem.shape

    def body(m_ref, dk_ref, dv_ref, wk_ref, wv_ref, gp_ref):
        dmn = _nt(dk_ref[...].astype(CDT), wk_ref[...].reshape(D, D)) + _nt(dv_ref[...].astype(CDT), wv_ref[...].reshape(D, D))
        n, _ = _rms(m_ref[...])
        gp_ref[...] = jnp.zeros_like(gp_ref)
        gp_ref[0:1, :] = jnp.sum(dmn * n, axis=0, keepdims=True)

    full = pl.BlockSpec((M, D), lambda i: (0, 0))
    return _call(
        body, "xa_kv_bwd", (1,),
        [full, full, full, pl.BlockSpec((NS, GW, D), lambda i: (0, 1, 0)), pl.BlockSpec((NS, GW, D), lambda i: (0, 2, 0))],
        pl.BlockSpec((8, D), lambda i: (0, 0)), SDS((8, D), F32))(mem, dk, dv, pxa, pxa)


def _loss_head(y, t):
    S, D = y.shape
    TM = min(512, S)

    def body(y_ref, t_ref, dy_ref, l_ref):
        @pl.when(pl.program_id(0) == 0)
        def _():
            l_ref[...] = jnp.zeros_like(l_ref)

        e = y_ref[...] - t_ref[...]
        dy_ref[...] = e * (1.0 / D)
        l_ref[...] += 0.5 * jnp.sum(jnp.mean(e * e, axis=-1, keepdims=True), axis=0, keepdims=True)

    row = pl.BlockSpec((TM, D), lambda i: (i, 0))
    return _call(body, "loss_head", (S // TM,), [row, row], [row, pl.BlockSpec((8, 128), lambda i: (0, 0))],
                 [SDS((S, D), F32), SDS((8, 128), F32)])(y, t)


def _row_tile(rows, cols, limit=1 << 18, step=8):
    if rows * cols <= limit or rows % step:
        return rows
    best = step
    for t in range(step, rows + 1, step):
        if rows % t == 0 and t * cols <= limit:
            best = t
    return best


def _slot_sum(r):
    _, R, C = r.shape
    TR = _row_tile(R, C * NS, limit=1 << 21, step=16)

    def body(r_ref, o_ref):
        acc = r_ref[0].astype(F32)
        for j in range(1, NS):
            acc = acc + r_ref[j].astype(F32)
        o_ref[...] = acc

    return _call(body, "slot_sum", (R // TR,), [pl.BlockSpec((NS, TR, C), lambda i: (0, i, 0))],
                 pl.BlockSpec((TR, C), lambda i: (i, 0)), SDS((R, C), F32))(r)


def _adamw(w, g, m, v):
    shape = w.shape
    C = shape[-1]
    R = w.size // C
    TR = _row_tile(R, C)
    c1 = 1.0 - ADAM_B1 ** ADAM_STEP
    c2 = 1.0 - ADAM_B2 ** ADAM_STEP

    def body(w_ref, g_ref, m_ref, v_ref, d_ref, nm_ref, nv_ref):
        gg = g_ref[...]
        nm = ADAM_B1 * m_ref[...] + (1.0 - ADAM_B1) * gg
        nv = ADAM_B2 * v_ref[...] + (1.0 - ADAM_B2) * (gg * gg)
        nm_ref[...] = nm
        nv_ref[...] = nv
        d_ref[...] = -ADAM_LR * ((nm / c1) / (jnp.sqrt(nv / c2) + ADAM_EPS) + ADAM_WD * w_ref[...])

    blk = pl.BlockSpec((TR, C), lambda i: (i, 0))
    outs = _call(body, "adamw", (R // TR,), [blk] * 4, [blk] * 3, [SDS((R, C), F32)] * 3)(
        w.reshape(R, C), g.reshape(R, C), m.reshape(R, C), v.reshape(R, C))
    return tuple(o.reshape(shape) for o in outs)


def _exchange(arrs, scatter, name):
    n = len(arrs)
    np_ = NS - 1

    def body(*refs):
        ins, outs = refs[:n], refs[n:2 * n]
        send_sems, recv_sems, loc_sems = refs[2 * n:]
        x, y, c = lax.axis_index("x"), lax.axis_index("y"), lax.axis_index("c")
        me = 4 * x + 2 * y + c
        peers = []
        for f in range(1, NS):
            px = 1 - x if f & 4 else x
            py = 1 - y if f & 2 else y
            pc = 1 - c if f & 1 else c
            peers.append(((px, py, pc), 4 * px + 2 * py + pc))

        def src(a, pid):
            return ins[a].at[pid] if scatter else ins[a]

        local = [pltpu.make_async_copy(src(a, me), outs[a].at[me], loc_sems.at[a]) for a in range(n)]
        for cp in local:
            cp.start()
        sends = []
        for a in range(n):
            for f, (dev, pid) in enumerate(peers):
                sends.append(pltpu.make_async_remote_copy(
                    src_ref=src(a, pid), dst_ref=outs[a].at[me], send_sem=send_sems.at[a * np_ + f],
                    recv_sem=recv_sems.at[a * np_ + f], device_id=dev, device_id_type=pl.DeviceIdType.MESH))
        for cp in sends:
            cp.start()
        for a in range(n):
            for f, (dev, pid) in enumerate(peers):
                pltpu.make_async_remote_copy(
                    src_ref=src(a, pid), dst_ref=outs[a].at[pid], send_sem=send_sems.at[a * np_ + f],
                    recv_sem=recv_sems.at[a * np_ + f], device_id=dev, device_id_type=pl.DeviceIdType.MESH).wait_recv()
        for cp in sends:
            cp.wait_send()
        for cp in local:
            cp.wait()

    out_shape = [SDS(a.shape if scatter else (NS,) + a.shape, a.dtype) for a in arrs]
    anyspec = pl.BlockSpec(memory_space=pl.ANY)
    outs = pl.pallas_call(
        body, name=name, in_specs=[anyspec] * n, out_specs=[anyspec] * n, out_shape=out_shape,
        scratch_shapes=[pltpu.SemaphoreType.DMA((n * np_,)), pltpu.SemaphoreType.DMA((n * np_,)),
                        pltpu.SemaphoreType.DMA((n,))],
        compiler_params=pltpu.CompilerParams(has_side_effects=True))(*arrs)
    return list(outs)


def _peers():
    x, y, c = lax.axis_index("x"), lax.axis_index("y"), lax.axis_index("c")
    out = []
    for f in range(1, NS):
        px = 1 - x if f & 4 else x
        py = 1 - y if f & 2 else y
        pc = 1 - c if f & 1 else c
        out.append(((px, py, pc), 4 * px + 2 * py + pc))
    return 4 * x + 2 * y + c, out


def _exchange_copies(ins, lands, scatter, send_sems, recv_sems, loc_sems):
    me, peers = _peers()
    np_ = NS - 1

    def src(a, pid):
        return ins[a].at[pid] if scatter[a] else ins[a]

    def rcopy(a, f, dev, land_slot):
        return pltpu.make_async_remote_copy(
            src_ref=src(a, peers[f][1]), dst_ref=lands[a].at[land_slot], send_sem=send_sems.at[a * np_ + f],
            recv_sem=recv_sems.at[a * np_ + f], device_id=dev, device_id_type=pl.DeviceIdType.MESH)

    local = [pltpu.make_async_copy(src(a, me), lands[a].at[me], loc_sems.at[a]) for a in range(len(ins))]
    sends = [rcopy(a, f, dev, me) for a in range(len(ins)) for f, (dev, _) in enumerate(peers)]
    arrivals = [rcopy(a, f, dev, pid) for a in range(len(ins)) for f, (dev, pid) in enumerate(peers)]
    return local, sends, arrivals


_HBM = pl.BlockSpec(memory_space=pltpu.HBM)
_SEM = pl.BlockSpec(memory_space=pltpu.SEMAPHORE)
_ANY = pl.BlockSpec(memory_space=pl.ANY)


def _exchange_start(arrs, scatter, after, name):
    n = len(arrs)
    np_ = NS - 1
    lands = [lax.empty(a.shape if sc else (NS,) + a.shape, a.dtype) for a, sc in zip(arrs, scatter)]

    def body(*refs):
        ins, lnd = refs[:n], refs[n:2 * n]
        send_sems, recv_sems, loc_sems = refs[2 * n + 1:2 * n + 4]
        token = refs[-1]
        local, sends, _ = _exchange_copies(ins, lnd, scatter, send_sems, recv_sems, loc_sems)
        for cp in local + sends:
            cp.start()
        token[...] = jnp.zeros_like(token)

    hbm = lambda a: pltpu.HBM(a.shape, a.dtype)
    outs = pl.pallas_call(
        body, name=name,
        out_shape=(pltpu.SemaphoreType.DMA((n * np_,)), pltpu.SemaphoreType.DMA((n * np_,)), pltpu.SemaphoreType.DMA((n,)),
                   *[hbm(a) for a in arrs], *[hbm(a) for a in lands], SDS((8, 128), F32)),
        in_specs=[_HBM] * (2 * n) + [_ANY],
        out_specs=(_SEM, _SEM, _SEM, *([_HBM] * (2 * n)), pl.BlockSpec(memory_space=pltpu.VMEM)),
        input_output_aliases={i: 3 + i for i in range(2 * n)},
        compiler_params=pltpu.CompilerParams(has_side_effects=pltpu.SideEffectType.DATAFLOW_SIDE_EFFECTING),
    )(*[pltpu.with_memory_space_constraint(a, pltpu.HBM) for a in list(arrs) + lands], after)
    return {"sems": outs[:3], "ins": outs[3:3 + n], "lands": outs[3 + n:3 + 2 * n], "token": outs[-1], "scatter": scatter}


def _exchange_wait(h, after, name):
    n = len(h["ins"])
    scatter = h["scatter"]
    after = list(after) if isinstance(after, (list, tuple)) else [after]

    def body(*refs):
        ins, lnd = refs[:n], refs[n:2 * n]
        send_sems, recv_sems, loc_sems = refs[2 * n:2 * n + 3]
        local, sends, arrivals = _exchange_copies(ins, lnd, scatter, send_sems, recv_sems, loc_sems)
        for cp in sends:
            cp.wait_send()
        for cp in arrivals:
            cp.wait_recv()
        for cp in local:
            cp.wait()

    hbm = lambda a: pltpu.HBM(a.shape, a.dtype)
    outs = pl.pallas_call(
        body, name=name,
        out_shape=tuple(hbm(a) for a in list(h["ins"]) + list(h["lands"])),
        in_specs=[_HBM] * (2 * n) + [_SEM] * 3 + [_ANY] * len(after),
        out_specs=tuple([_HBM] * (2 * n)),
        input_output_aliases={i: i for i in range(2 * n)},
        compiler_params=pltpu.CompilerParams(has_side_effects=pltpu.SideEffectType.DATAFLOW_SIDE_EFFECTING),
    )(*h["ins"], *h["lands"], *h["sems"], *after)
    return list(outs[n:])


_W_NAMES = ("ffn1_pre_g", "ffn1_post_g", "ffn1_w1", "ffn1_w3", "ffn1_w2", "mix_pre_g", "mix_post_g", "w_in", "conv_a_w",
            "conv_a_b", "pool_w", "pool_scale", "sgu_ln_g", "sgu_ln_b", "sgu_ws", "sgu_b", "conv_d_w", "conv_d_b",
            "conv_d_ln_g", "conv_d_ln_b", "w_branch", "w_gate", "b_gate", "w_o", "xa_pre_g", "xa_post_g", "mem_g",
            "xa_wq", "xa_wk", "xa_wv", "xa_wo", "ffn2_pre_g", "ffn2_post_g", "ffn2_w1", "ffn2_w3", "ffn2_w2")
_REP_NAMES = tuple(n for n, _ in _SP_NAMES) + ("pool_w", "sgu_ws", "sgu_b", "conv_a_w", "conv_d_w")


def _t(w):
    return jnp.swapaxes(w, -1, -2)


def _step(x, mem, loss_target, W, M, V):
    L = W["w_in"].shape[0]
    S, D = x.shape[1], x.shape[2]
    x0 = x.reshape(S, D)
    memf = mem.reshape(mem.shape[1], D)
    me = 4 * lax.axis_index("x") + 2 * lax.axis_index("y") + lax.axis_index("c")
    FS = W["ffn1_w2"].shape[1]
    KA, KD = W["conv_a_w"].shape[1], W["conv_d_w"].shape[1]
    CS = W["conv_a_w"].shape[2]

    pf1 = jnp.concatenate([_t(W["ffn1_w1"]), _t(W["ffn1_w3"]), W["ffn1_w2"]], axis=1).astype(CDT)
    pf2 = jnp.concatenate([_t(W["ffn2_w1"]), _t(W["ffn2_w3"]), W["ffn2_w2"]], axis=1).astype(CDT)
    pma = jnp.concatenate([_t(W["w_in"]), _t(W["w_gate"])], axis=1).astype(CDT)
    pwo = W["w_o"].astype(CDT)
    pxa = jnp.concatenate([W["xa_wq"], W["xa_wk"], W["xa_wv"], W["xa_wo"]], axis=1).astype(CDT)
    wbs = W["w_branch"].astype(CDT)
    cws = jnp.concatenate([W["conv_a_w"], W["conv_d_w"]], axis=1).reshape(-1, 128)
    sp_all = jnp.concatenate([W[n] for n, _ in _SP_NAMES], axis=1)
    bsc_all = W["sgu_b"][..., None]

    (cwg,) = _exchange([cws], False, "gather_conv_w")
    cwf = cwg.reshape(NS, L, KA + KD, CS).transpose(1, 2, 0, 3).reshape(L, KA + KD, NS * CS)

    def gather_start(l, after):
        return _exchange_start([pf1[l], pf2[l], pma[l], pwo[l], pxa[l], wbs[l]], (False,) * 6, after, f"gather_start_{l}")

    packs = [None] * L
    first = []
    for k, grp in enumerate(([pf1[0]], [pma[0], pwo[0], wbs[0]], [pxa[0]], [pf2[0]])):
        first.append(_exchange_start(grp, (False,) * len(grp), first[-1]["token"] if first else sp_all, f"gather_start_0{'abcd'[k]}"))
    saved = []
    xc = x0
    for l in range(L):
        if l == 0:
            (gf1,) = _exchange_wait(first[0], sp_all, "gather_wait_0a")
        else:
            gf1, gf2, gma, gwo, gxa, gwb = packs[l]
        sp = sp_all[l:l + 1]
        cwa, cwd = cwf[l, :KA], cwf[l, KA:]
        wp, ws, bsc = W["pool_w"][l], W["sgu_ws"][l], bsc_all[l]
        s = {"x0": xc}
        nxt = gather_start(l + 1, gf1) if l + 1 < L else None
        xc, s["hb1"], s["a1"], s["b1"], s["y1"] = _ffn_fwd(xc, sp, "ffn1_pre_g", "ffn1_post_g", gf1,
                                                            nxt["token"] if nxt else sp)
        s["x1"] = xc
        if l == 0:
            gma, gwo, gwb = _exchange_wait(first[1], xc, "gather_wait_0b")
        s["hbm"], s["z"], s["g"] = _mix_in(xc, sp, gma)
        s["ma"] = _mixA_fwd(s["z"], cwa, sp)
        s["mb"] = _mixB_fwd(s["z"], wp, sp)
        s["mc"] = _mixC_fwd(s["z"], ws, bsc, sp)
        s["yd"] = _mixD_conv_fwd(s["z"], cwd, sp)
        xc, s["md"], s["yk"], s["mg"], s["mo"] = _merge_fwd(s["ma"], s["mb"], s["mc"], s["yd"], s["g"], gwb, gwo, xc, sp)
        s["x2"] = xc
        if l == 0:
            (gxa,) = _exchange_wait(first[2], xc, "gather_wait_0c")
        s["mn"], s["k"], s["v"] = _xa_kv(memf, sp, gxa)
        xc, s["hbx"], s["q"], s["o"], s["po"] = _xa_fwd(xc, s["k"], s["v"], sp, gxa)
        s["x3"] = xc
        if l == 0:
            (gf2,) = _exchange_wait(first[3], xc, "gather_wait_0d")
            packs[0] = (gf1, gf2, gma, gwo, gxa, gwb)
        xc, s["hb2"], s["a2"], s["b2"], s["y2"] = _ffn_fwd(xc, sp, "ffn2_pre_g", "ffn2_post_g", gf2, sp)
        saved.append(s)
        if nxt:
            packs[l + 1] = _exchange_wait(nxt, xc, f"gather_wait_{l + 1}")

    dx, lpart = _loss_head(xc, loss_target.reshape(S, D))
    loss = lax.psum(lpart[0, 0], ("x", "y", "c"))

    rep = {n: [None] * L for n in _REP_NAMES}
    summed = [dict() for _ in range(L)]
    pending = None
    last = []
    for l in reversed(range(L)):
        gf1, gf2, gma, gwo, gxa, gwb = packs[l]
        sp = sp_all[l:l + 1]
        cwa, cwd = cwf[l, :KA], cwf[l, KA:]
        wp, ws, bsc = W["pool_w"][l], W["sgu_ws"][l], bsc_all[l]
        s = saved[l]

        dx, dyb, da, db, gp = _ffn_bwd_act(dx, s["x3"], s["y2"], s["a2"], s["b2"], sp, "ffn2_pre_g", "ffn2_post_g", gf2,
                                           pending[1]["token"] if pending else sp)
        rep["ffn2_pre_g"][l], rep["ffn2_post_g"][l] = gp[0], gp[1]
        d_f2 = _ffn_bwd_w(s["hb2"], dyb, s["a2"], s["b2"], da, db)
        if l == 0:
            last.append(_exchange_start([d_f2], (True,), dx, "scatter_start_0a"))

        dx, dpo, dq, dk, dv, gp = _xa_bwd_act(dx, s["x2"], s["po"], s["q"], s["k"], s["v"], sp, gxa,
                                              last[-1]["token"] if last else sp)
        rep["xa_pre_g"][l], rep["xa_post_g"][l] = gp[0], gp[1]
        d_xa = _xa_bwd_w(s["hbx"], dq, s["o"], dpo, s["mn"], dk, dv)
        rep["mem_g"][l] = _xa_kv_bwd(memf, dk, dv, sp, gxa)[0]
        if l == 0:
            last.append(_exchange_start([d_xa], (True,), dx, "scatter_start_0b"))

        dmo, dm, dgp, dyk, gp = _merge_bwd_act(dx, s["mo"], s["g"], s["yk"], gwb, gwo, sp, last[-1]["token"] if last else sp)
        rep["mix_post_g"][l] = gp[0]
        d_wb, d_wo = _merge_bwd_w(s["ma"], s["mb"], s["mc"], s["md"], dyk, s["mg"], dmo)
        dza, dcw, gp = _mixA_bwd(s["z"], dm[0], cwa, sp)
        rep["conv_a_w"][l], rep["conv_a_b"][l] = dcw, gp[0]
        dzb, dwp, gp = _mixB_bwd(s["z"], dm[1], wp, sp)
        rep["pool_w"][l], rep["pool_scale"][l] = dwp, gp[0]
        dzc, dws, dbs, gp = _mixC_bwd(s["z"], dm[2], ws, bsc, sp)
        rep["sgu_ws"][l], rep["sgu_b"][l], rep["sgu_ln_g"][l], rep["sgu_ln_b"][l] = dws, dbs[:, :, 0], gp[0], gp[1]
        dyd, gp = _mixD_ln_bwd(dm[3], s["yd"], sp)
        rep["conv_d_ln_g"][l], rep["conv_d_ln_b"][l] = gp[0], gp[1]
        dzd, dcw, gp = _mixD_conv_bwd(s["z"], dyd, cwd)
        rep["conv_d_w"][l], rep["conv_d_b"][l] = dcw, gp[0]
        dz = jnp.concatenate([dza, dzb, dzc, dzd], axis=0)
        dx, gp = _mix_in_bwd_act(dz, dgp, dx, s["x1"], sp, gma)
        rep["mix_pre_g"][l] = gp[0]
        d_ma, dbg = _mix_in_bwd_w(dz, dgp, s["hbm"])
        rep["b_gate"][l] = dbg[:, 0, :].reshape(-1)
        if l == 0:
            last.append(_exchange_start([d_ma, d_wo, d_wb], (True,) * 3, dx, "scatter_start_0c"))

        dx, dyb, da, db, gp = _ffn_bwd_act(dx, s["x0"], s["y1"], s["a1"], s["b1"], sp, "ffn1_pre_g", "ffn1_post_g", gf1,
                                           last[-1]["token"] if last else sp)
        rep["ffn1_pre_g"][l], rep["ffn1_post_g"][l] = gp[0], gp[1]
        d_f1 = _ffn_bwd_w(s["hb1"], dyb, s["a1"], s["b1"], da, db)

        if pending:
            r = _exchange_wait(pending[1], dx, f"scatter_wait_{pending[0]}")
            summed[pending[0]] = dict(zip(("f1", "f2", "ma", "wo", "xa", "wb", "flat"), r))
        flat = jnp.concatenate([rep[n][l].reshape(-1) for n in _REP_NAMES])
        flat = jnp.pad(flat, (0, -flat.size % 1024)).reshape(-1, 128)
        if l == 0:
            last.append(_exchange_start([d_f1, flat], (True, False), dx, "scatter_start_0d"))
        else:
            pending = (l, _exchange_start([d_f1, d_f2, d_ma, d_wo, d_xa, d_wb, flat], (True,) * 6 + (False,), dx,
                                          f"scatter_start_{l}"))

    sums = lambda r: _slot_sum(r.reshape(NS, -1, r.shape[-1]))
    for l in range(1, L):
        summed[l] = {k: sums(v) for k, v in summed[l].items()}
    (r,) = _exchange_wait(last[0], dx, "scatter_wait_0a")
    summed[0]["f2"] = sums(r)
    (r,) = _exchange_wait(last[1], dx, "scatter_wait_0b")
    summed[0]["xa"] = sums(r)
    r = _exchange_wait(last[2], dx, "scatter_wait_0c")
    summed[0]["ma"], summed[0]["wo"], summed[0]["wb"] = (sums(v) for v in r)
    r = _exchange_wait(last[3], [v for d in summed for v in d.values()], "scatter_wait_0d")
    summed[0]["f1"], summed[0]["flat"] = (sums(v) for v in r)

    G = {}
    stk = lambda k: jnp.stack([summed[l][k] for l in range(L)])
    f1, f2, ma_, wo_, xa_, wb_ = (stk(k) for k in ("f1", "f2", "ma", "wo", "xa", "wb"))
    for nm, f in (("ffn1", f1), ("ffn2", f2)):
        G[nm + "_w1"], G[nm + "_w3"], G[nm + "_w2"] = _t(f[:, :FS]), _t(f[:, FS:2 * FS]), f[:, 2 * FS:]
    G["w_in"], G["w_gate"] = _t(ma_[:, :MW]), _t(ma_[:, MW:])
    G["w_o"] = wo_
    G["xa_wq"], G["xa_wk"], G["xa_wv"], G["xa_wo"] = (xa_[:, i * GW:(i + 1) * GW] for i in range(4))
    G["w_branch"] = wb_.reshape(W["w_branch"].shape)

    tot = [summed[l]["flat"].reshape(-1) for l in range(L)]
    off = 0
    for n in _REP_NAMES:
        shape = (KA, NS * CS) if n == "conv_a_w" else (KD, NS * CS) if n == "conv_d_w" else W[n].shape[1:]
        size = 1
        for d in shape:
            size *= d
        G[n] = jnp.stack([tot[l][off:off + size].reshape(shape) for l in range(L)])
        off += size
    for n in ("conv_a_w", "conv_d_w"):
        G[n] = lax.dynamic_slice_in_dim(G[n], me * CS, CS, axis=2)

    deltas, new_m, new_v = {}, {}, {}
    for n in _W_NAMES:
        deltas[n], new_m[n], new_v[n] = _adamw(W[n], G[n], M[n], V[n])
    grad_x = dx.reshape(x.shape)
    return (loss, grad_x, *[G[n] for n in _W_NAMES], *[deltas[n] for n in _W_NAMES],
            *[new_m[n] for n in _W_NAMES], *[new_v[n] for n in _W_NAMES])


def kernel(x, mem, ffn1_pre_g, ffn1_post_g, ffn1_w1, ffn1_w3, ffn1_w2, mix_pre_g, mix_post_g, w_in, conv_a_w, conv_a_b, pool_w, pool_scale, sgu_ln_g, sgu_ln_b, sgu_ws, sgu_b, conv_d_w, conv_d_b, conv_d_ln_g, conv_d_ln_b, w_branch, w_gate, b_gate, w_o, xa_pre_g, xa_post_g, mem_g, xa_wq, xa_wk, xa_wv, xa_wo, ffn2_pre_g, ffn2_post_g, ffn2_w1, ffn2_w3, ffn2_w2, loss_target, m_ffn1_pre_g, m_ffn1_post_g, m_ffn1_w1, m_ffn1_w3, m_ffn1_w2, m_mix_pre_g, m_mix_post_g, m_w_in, m_conv_a_w, m_conv_a_b, m_pool_w, m_pool_scale, m_sgu_ln_g, m_sgu_ln_b, m_sgu_ws, m_sgu_b, m_conv_d_w, m_conv_d_b, m_conv_d_ln_g, m_conv_d_ln_b, m_w_branch, m_w_gate, m_b_gate, m_w_o, m_xa_pre_g, m_xa_post_g, m_mem_g, m_xa_wq, m_xa_wk, m_xa_wv, m_xa_wo, m_ffn2_pre_g, m_ffn2_post_g, m_ffn2_w1, m_ffn2_w3, m_ffn2_w2, v_ffn1_pre_g, v_ffn1_post_g, v_ffn1_w1, v_ffn1_w3, v_ffn1_w2, v_mix_pre_g, v_mix_post_g, v_w_in, v_conv_a_w, v_conv_a_b, v_pool_w, v_pool_scale, v_sgu_ln_g, v_sgu_ln_b, v_sgu_ws, v_sgu_b, v_conv_d_w, v_conv_d_b, v_conv_d_ln_g, v_conv_d_ln_b, v_w_branch, v_w_gate, v_b_gate, v_w_o, v_xa_pre_g, v_xa_post_g, v_mem_g, v_xa_wq, v_xa_wk, v_xa_wv, v_xa_wo, v_ffn2_pre_g, v_ffn2_post_g, v_ffn2_w1, v_ffn2_w3, v_ffn2_w2):
    args = dict(locals())
    W = {n: args[n] for n in _W_NAMES}
    M = {n: args["m_" + n] for n in _W_NAMES}
    V = {n: args["v_" + n] for n in _W_NAMES}
    return _step(x, mem, loss_target, W, M, V)
```

```python
import jax
import jax.numpy as jnp
from jax import lax
from jax.experimental import pallas as pl
from jax.experimental.pallas import tpu as pltpu

F32 = jnp.float32
CDT = jnp.bfloat16
EPS = 1e-6
NS = 8
GW = 128
MW = 512
CHUNK = 64
XA_HEADS = 4
POOL_WINDOWS = (2, 4, 8, 16)
VMEM_LIMIT = 56 * 1024 * 1024
ADAM_LR, ADAM_B1, ADAM_B2, ADAM_EPS, ADAM_WD, ADAM_STEP = 0.001, 0.9, 0.999, 1e-08, 0.01, 10

SDS = jax.ShapeDtypeStruct

_SP_NAMES = (("ffn1_pre_g", 1024), ("ffn1_post_g", 1024), ("mix_pre_g", 1024), ("mix_post_g", 1024),
             ("xa_pre_g", 1024), ("xa_post_g", 1024), ("mem_g", 1024), ("ffn2_pre_g", 1024), ("ffn2_post_g", 1024),
             ("conv_a_b", 512), ("pool_scale", 512), ("sgu_ln_g", 512), ("sgu_ln_b", 512), ("conv_d_b", 512),
             ("conv_d_ln_g", 512), ("conv_d_ln_b", 512), ("b_gate", 4096))
_SP = {}
_off = 0
for _n, _w in _SP_NAMES:
    _SP[_n] = (_off, _w)
    _off += _w
_SP_TOTAL = _off


def _call(body, name, grid, in_specs, out_specs, out_shape, scratch=()):
    return pl.pallas_call(
        body, name=name, grid=grid, in_specs=in_specs, out_specs=out_specs, out_shape=out_shape,
        scratch_shapes=list(scratch),
        compiler_params=pltpu.CompilerParams(dimension_semantics=("arbitrary",) * len(grid),
                                             vmem_limit_bytes=VMEM_LIMIT))


def _nn(a, b):
    return lax.dot_general(a, b, (((1,), (0,)), ((), ())), preferred_element_type=F32)


def _nt(a, b):
    return lax.dot_general(a, b, (((1,), (1,)), ((), ())), preferred_element_type=F32)


def _tn(a, b):
    return lax.dot_general(a, b, (((0,), (0,)), ((), ())), preferred_element_type=F32)


def _rms(x):
    r = lax.rsqrt(jnp.mean(x * x, axis=-1, keepdims=True) + EPS)
    return x * r, r


def _rms_bwd(n, r, g, dout):
    dn = dout * g
    dx = r * (dn - n * jnp.mean(dn * n, axis=-1, keepdims=True))
    return dx, jnp.sum(dout * n, axis=0, keepdims=True)


def _ln(y):
    mu = jnp.mean(y, axis=-1, keepdims=True)
    yc = y - mu
    rs = lax.rsqrt(jnp.mean(yc * yc, axis=-1, keepdims=True) + EPS)
    return yc * rs, rs


def _ln_bwd(xh, rs, dxh):
    return rs * (dxh - jnp.mean(dxh, axis=-1, keepdims=True) - xh * jnp.mean(dxh * xh, axis=-1, keepdims=True))


def _silu_parts(a):
    s = jax.nn.sigmoid(a)
    sl = a * s
    return sl, s + sl * (1.0 - s)


_GELU_C = 0.7978845608028654
_GELU_A = 0.044715


def _gelu(x):
    return 0.5 * x * (1.0 + jnp.tanh(_GELU_C * (x + _GELU_A * x * x * x)))


def _gelu_parts(x):
    t = jnp.tanh(_GELU_C * (x + _GELU_A * x * x * x))
    g = 0.5 * x * (1.0 + t)
    dg = 0.5 * (1.0 + t) + 0.5 * x * (1.0 - t * t) * _GELU_C * (1.0 + 3.0 * _GELU_A * x * x)
    return g, dg


def _spspec(name, width, imap):
    off = _SP[name][0]
    assert off % width == 0
    return pl.BlockSpec((1, width), lambda *a: (0, off // width + imap(*a)))


def _zero(*a):
    return 0


FFN_SG = 2


def _ffn_fwd(x, sp, pre, post, pf, dep):
    S, D = x.shape
    FS = pf.shape[1] // 3
    TM = min(512, S)
    SG, NG, W = FFN_SG, NS // FFN_SG, FFN_SG * FS

    def body(x_ref, pg_ref, qg_ref, w1_ref, w3_ref, w2_ref, dep_ref, xo_ref, hb_ref, a_ref, b_ref, y_ref, hb_s, acc):
        j = pl.program_id(1)

        @pl.when(j == 0)
        def _():
            n, _ = _rms(x_ref[...])
            hb = (n * pg_ref[...]).astype(CDT)
            hb_s[...] = hb
            hb_ref[...] = hb
            acc[...] = jnp.zeros_like(acc)

        hb = hb_s[...]
        a = _nt(hb, w1_ref[...].reshape(W, D))
        b = _nt(hb, w3_ref[...].reshape(W, D))
        a_ref[...] = a.astype(CDT)
        b_ref[...] = b.astype(CDT)
        u = (a * jax.nn.sigmoid(a) * b).astype(CDT)
        acc[...] += _nn(u, w2_ref[...].reshape(W, D))

        @pl.when(j == NG - 1)
        def _():
            y = acc[...]
            y_ref[...] = y.astype(CDT)
            n, _ = _rms(y)
            xo_ref[...] = x_ref[...] + 0.5 * (n * qg_ref[...])

    row = lambda i, j: (i, 0)
    grp = lambda i, j: (j, i, 0)
    return _call(
        body, "ffn_fwd", (S // TM, NG),
        [pl.BlockSpec((TM, D), row), _spspec(pre, D, _zero), _spspec(post, D, _zero),
         pl.BlockSpec((SG, FS, D), lambda i, j: (j, 0, 0)), pl.BlockSpec((SG, FS, D), lambda i, j: (j, 1, 0)),
         pl.BlockSpec((SG, FS, D), lambda i, j: (j, 2, 0)), pl.BlockSpec(memory_space=pl.ANY)],
        [pl.BlockSpec((TM, D), row), pl.BlockSpec((TM, D), row), pl.BlockSpec((None, TM, W), grp),
         pl.BlockSpec((None, TM, W), grp), pl.BlockSpec((TM, D), row)],
        [SDS((S, D), F32), SDS((S, D), CDT), SDS((NG, S, W), CDT), SDS((NG, S, W), CDT), SDS((S, D), CDT)],
        [pltpu.VMEM((TM, D), CDT), pltpu.VMEM((TM, D), F32)])(x, sp, sp, pf, pf, pf, dep)


def _ffn_bwd_act(dxo, x, y, a, b, sp, pre, post, pf, dep):
    S, D = x.shape
    FS = pf.shape[1] // 3
    TM = min(512, S)
    SG, NG, W = FFN_SG, NS // FFN_SG, FFN_SG * FS

    def body(dxo_ref, x_ref, y_ref, a_ref, b_ref, pg_ref, qg_ref, w1_ref, w3_ref, w2_ref, dep_ref,
             dx_ref, dyb_ref, da_ref, db_ref, gp_ref, dyb_s, acc):
        i = pl.program_id(0)
        j = pl.program_id(1)

        @pl.when((i == 0) & (j == 0))
        def _():
            gp_ref[...] = jnp.zeros_like(gp_ref)

        @pl.when(j == 0)
        def _():
            n, r = _rms(y_ref[...].astype(F32))
            dy, dg = _rms_bwd(n, r, qg_ref[...], 0.5 * dxo_ref[...])
            dyb = dy.astype(CDT)
            dyb_s[...] = dyb
            dyb_ref[...] = dyb
            gp_ref[1:2, :] += dg
            acc[...] = jnp.zeros_like(acc)

        sl, dsl = _silu_parts(a_ref[...].astype(F32))
        du = _nt(dyb_s[...], w2_ref[...].reshape(W, D))
        db = (du * sl).astype(CDT)
        da = (du * b_ref[...].astype(F32) * dsl).astype(CDT)
        da_ref[...] = da
        db_ref[...] = db
        acc[...] += _nn(da, w1_ref[...].reshape(W, D)) + _nn(db, w3_ref[...].reshape(W, D))

        @pl.when(j == NG - 1)
        def _():
            n, r = _rms(x_ref[...])
            dx, dg = _rms_bwd(n, r, pg_ref[...], acc[...])
            dx_ref[...] = dxo_ref[...] + dx
            gp_ref[0:1, :] += dg

    row = lambda i, j: (i, 0)
    grp = lambda i, j: (j, i, 0)
    return _call(
        body, "ffn_bwd_act", (S // TM, NG),
        [pl.BlockSpec((TM, D), row), pl.BlockSpec((TM, D), row), pl.BlockSpec((TM, D), row),
         pl.BlockSpec((None, TM, W), grp), pl.BlockSpec((None, TM, W), grp),
         _spspec(pre, D, _zero), _spspec(post, D, _zero),
         pl.BlockSpec((SG, FS, D), lambda i, j: (j, 0, 0)), pl.BlockSpec((SG, FS, D), lambda i, j: (j, 1, 0)),
         pl.BlockSpec((SG, FS, D), lambda i, j: (j, 2, 0)), pl.BlockSpec(memory_space=pl.ANY)],
        [pl.BlockSpec((TM, D), row), pl.BlockSpec((TM, D), row), pl.BlockSpec((None, TM, W), grp),
         pl.BlockSpec((None, TM, W), grp), pl.BlockSpec((8, D), lambda i, j: (0, 0))],
        [SDS((S, D), F32), SDS((S, D), CDT), SDS((NG, S, W), CDT), SDS((NG, S, W), CDT), SDS((8, D), F32)],
        [pltpu.VMEM((TM, D), CDT), pltpu.VMEM((TM, D), F32)])(dxo, x, y, a, b, sp, sp, pf, pf, pf, dep)


def _ffn_bwd_w(hb, dyb, a, b, da, db):
    S, D = hb.shape
    SG, NG = FFN_SG, NS // FFN_SG
    W = a.shape[2]
    FS = W // SG
    TK = min(512, S)
    NK = S // TK

    def body(hb_ref, dyb_ref, a_ref, b_ref, da_ref, db_ref, g_ref, acc):
        k = pl.program_id(1)

        @pl.when(k == 0)
        def _():
            acc[...] = jnp.zeros_like(acc)

        af = a_ref[...].astype(F32)
        u = (af * jax.nn.sigmoid(af) * b_ref[...].astype(F32)).astype(CDT)
        hb = hb_ref[...]
        acc[0:W, :] += _tn(da_ref[...], hb)
        acc[W:2 * W, :] += _tn(db_ref[...], hb)
        acc[2 * W:3 * W, :] += _tn(u, dyb_ref[...])

        @pl.when(k == NK - 1)
        def _():
            for s in range(SG):
                for r in range(3):
                    g_ref[s, r * FS:(r + 1) * FS, :] = acc[r * W + s * FS:r * W + (s + 1) * FS, :].astype(CDT)

    row = lambda j, k: (k, 0)
    grp = lambda j, k: (j, k, 0)
    return _call(
        body, "ffn_bwd_w", (NG, NK),
        [pl.BlockSpec((TK, D), row), pl.BlockSpec((TK, D), row)] + [pl.BlockSpec((None, TK, W), grp)] * 4,
        pl.BlockSpec((SG, 3 * FS, D), lambda j, k: (j, 0, 0)),
        SDS((NS, 3 * FS, D), CDT),
        [pltpu.VMEM((3 * W, D), F32)])(hb, dyb, a, b, da, db)


def _mix_in(x, sp, pma, dep):
    S, D = x.shape
    TM = min(512, S)

    def body(x_ref, pg_ref, bg_ref, wi_ref, wg_ref, dep_ref, hb_ref, z_ref, g_ref, hb_s):
        @pl.when(pl.program_id(1) == 0)
        def _():
            n, _ = _rms(x_ref[...])
            hb = (n * pg_ref[...]).astype(CDT)
            hb_s[...] = hb
            hb_ref[...] = hb

        hb = hb_s[...]
        z_ref[...] = _nt(hb, wi_ref[...]).astype(CDT)
        g_ref[...] = jax.nn.sigmoid(_nt(hb, wg_ref[...]) + bg_ref[...]).astype(CDT)

    row = lambda i, j: (i, 0)
    return _call(
        body, "mix_in", (S // TM, NS),
        [pl.BlockSpec((TM, D), row), _spspec("mix_pre_g", D, _zero), _spspec("b_gate", MW, lambda i, j: j),
         pl.BlockSpec((None, MW, D), lambda i, j: (j, 0, 0)), pl.BlockSpec((None, MW, D), lambda i, j: (j, 1, 0)), _ANY],
        [pl.BlockSpec((TM, D), row), pl.BlockSpec((None, TM, MW), lambda i, j: (j, i, 0)),
         pl.BlockSpec((None, TM, MW), lambda i, j: (j // 2, i, j % 2))],
        [SDS((S, D), CDT), SDS((NS, S, MW), CDT), SDS((4, S, D), CDT)],
        [pltpu.VMEM((TM, D), CDT)])(x, sp, sp, pma, pma, dep)


def _mix_in_bwd_act(dz, dgp, dxr, x, sp, pma):
    S, D = x.shape
    TM = min(512, S)

    def body(dz_ref, dg_ref, dxr_ref, x_ref, pg_ref, wi_ref, wg_ref, dx_ref, gp_ref, acc):
        i = pl.program_id(0)
        j = pl.program_id(1)

        @pl.when((i == 0) & (j == 0))
        def _():
            gp_ref[...] = jnp.zeros_like(gp_ref)

        @pl.when(j == 0)
        def _():
            acc[...] = jnp.zeros_like(acc)

        acc[...] += _nn(dz_ref[...], wi_ref[...]) + _nn(dg_ref[...], wg_ref[...])

        @pl.when(j == NS - 1)
        def _():
            n, r = _rms(x_ref[...])
            dx, dg = _rms_bwd(n, r, pg_ref[...], acc[...])
            dx_ref[...] = dxr_ref[...] + dx
            gp_ref[0:1, :] += dg

    row = lambda i, j: (i, 0)
    return _call(
        body, "mix_in_bwd_act", (S // TM, NS),
        [pl.BlockSpec((None, TM, MW), lambda i, j: (j, i, 0)), pl.BlockSpec((None, TM, MW), lambda i, j: (j // 2, i, j % 2)),
         pl.BlockSpec((TM, D), row), pl.BlockSpec((TM, D), row), _spspec("mix_pre_g", D, _zero),
         pl.BlockSpec((None, MW, D), lambda i, j: (j, 0, 0)), pl.BlockSpec((None, MW, D), lambda i, j: (j, 1, 0))],
        [pl.BlockSpec((TM, D), row), pl.BlockSpec((8, D), lambda i, j: (0, 0))],
        [SDS((S, D), F32), SDS((8, D), F32)],
        [pltpu.VMEM((TM, D), F32)])(dz, dgp, dxr, x, sp, pma, pma)


def _mix_in_bwd_w(dz, dgp, hb):
    S, D = hb.shape
    TK = min(512, S)
    NK = S // TK

    def body(dz_ref, dg_ref, hb_ref, g_ref, bg_ref, acc):
        k = pl.program_id(1)

        @pl.when(k == 0)
        def _():
            acc[...] = jnp.zeros_like(acc)
            bg_ref[...] = jnp.zeros_like(bg_ref)

        hb = hb_ref[...]
        dg = dg_ref[...]
        acc[0:MW, :] += _tn(dz_ref[...], hb)
        acc[MW:2 * MW, :] += _tn(dg, hb)
        bg_ref[0:1, :] += jnp.sum(dg.astype(F32), axis=0, keepdims=True)

        @pl.when(k == NK - 1)
        def _():
            g_ref[...] = acc[...].astype(CDT)

    return _call(
        body, "mix_in_bwd_w", (NS, NK),
        [pl.BlockSpec((None, TK, MW), lambda j, k: (j, k, 0)), pl.BlockSpec((None, TK, MW), lambda j, k: (j // 2, k, j % 2)),
         pl.BlockSpec((TK, D), lambda j, k: (k, 0))],
        [pl.BlockSpec((None, 2 * MW, D), lambda j, k: (j, 0, 0)), pl.BlockSpec((None, 8, MW), lambda j, k: (j, 0, 0))],
        [SDS((NS, 2 * MW, D), CDT), SDS((NS, 8, MW), F32)],
        [pltpu.VMEM((2 * MW, D), F32)])(dz, dgp, hb)


def _causal_taps(pad_ref, i, ch, halo, k_taps, lanes=slice(None)):
    val = pad_ref[pl.ds(pl.multiple_of(i * ch, 8), ch + halo), lanes]
    out = []
    for k in range(k_taps):
        s = k_taps - 1 - k
        out.append((k, (pltpu.roll(val, s, 0) if s else val)[halo:, :]))
    return out


def _anti_taps(pad_ref, i, ch, halo, k_taps, lanes=slice(None)):
    val = pad_ref[pl.ds(pl.multiple_of(i * ch, 8), ch + halo), lanes]
    n = ch + halo
    out = []
    for k in range(k_taps):
        s = k_taps - 1 - k
        out.append((k, (pltpu.roll(val, n - s, 0) if s else val)[:ch, :]))
    return out


def _conv_geometry(S, k_taps):
    halo = 8 * ((k_taps - 1 + 7) // 8)
    ch = min(256, S)
    return halo, ch, S // ch


def _rows(i, ch):
    return pl.ds(pl.multiple_of(i * ch, ch), ch)


def _mixA_fwd(z, cw, sp):
    S = z.shape[1]
    K = cw.shape[0]
    H, CH, NCH = _conv_geometry(S, K)

    def body(z_ref, w_ref, b_ref, o_ref, pad):
        pad[0:H, :] = jnp.zeros((H, GW), F32)

        def fill(i, c):
            r = _rows(i, CH)
            pad[pl.ds(pl.multiple_of(i * CH + H, 8), CH), :] = z_ref[2, r, :].astype(F32) * z_ref[0, r, :].astype(F32)
            return c

        lax.fori_loop(0, NCH, fill, 0)

        def conv(i, c):
            r = _rows(i, CH)
            acc = jnp.zeros((CH, GW), F32)
            for k, sh in _causal_taps(pad, i, CH, H, K):
                acc = acc + w_ref[k:k + 1, :] * sh
            o_ref[r, :] = (z_ref[1, r, :].astype(F32) * (acc + b_ref[...])).astype(CDT)
            return c

        lax.fori_loop(0, NCH, conv, 0)

    return _call(
        body, "mixA_fwd", (MW // GW,),
        [pl.BlockSpec((3, S, GW), lambda c: (0, 0, c)), pl.BlockSpec((K, GW), lambda c: (0, c)),
         _spspec("conv_a_b", GW, lambda c: c)],
        pl.BlockSpec((S, GW), lambda c: (0, c)), SDS((S, MW), CDT),
        [pltpu.VMEM((H + S, GW), F32)])(z, cw, sp)


def _mixA_bwd(z, dm, cw, sp):
    S = z.shape[1]
    K = cw.shape[0]
    H, CH, NCH = _conv_geometry(S, K)

    def body(z_ref, dm_ref, w_ref, b_ref, dz_ref, dw_ref, db_ref, pad, dpad, dw_s):
        pad[0:H, :] = jnp.zeros((H, GW), F32)
        dpad[pl.ds(S, H), :] = jnp.zeros((H, GW), F32)
        dw_s[...] = jnp.zeros_like(dw_s)
        db_ref[...] = jnp.zeros_like(db_ref)

        def fill(i, c):
            r = _rows(i, CH)
            pad[pl.ds(pl.multiple_of(i * CH + H, 8), CH), :] = z_ref[2, r, :].astype(F32) * z_ref[0, r, :].astype(F32)
            return c

        lax.fori_loop(0, NCH, fill, 0)

        def p1(i, c):
            r = _rows(i, CH)
            taps = _causal_taps(pad, i, CH, H, K)
            acc = jnp.zeros((CH, GW), F32)
            for k, sh in taps:
                acc = acc + w_ref[k:k + 1, :] * sh
            dmf = dm_ref[r, :].astype(F32)
            dz_ref[1, r, :] = (dmf * (acc + b_ref[...])).astype(CDT)
            dc = dmf * z_ref[1, r, :].astype(F32)
            dpad[r, :] = dc
            for k, sh in taps:
                dw_s[k:k + 1, :] += jnp.sum(dc * sh, axis=0, keepdims=True)
            db_ref[0:1, :] += jnp.sum(dc, axis=0, keepdims=True)
            return c

        lax.fori_loop(0, NCH, p1, 0)

        def p2(i, c):
            r = _rows(i, CH)
            dq = jnp.zeros((CH, GW), F32)
            for k, sh in _anti_taps(dpad, i, CH, H, K):
                dq = dq + w_ref[k:k + 1, :] * sh
            dz_ref[0, r, :] = (dq * z_ref[2, r, :].astype(F32)).astype(CDT)
            dz_ref[2, r, :] = (dq * z_ref[0, r, :].astype(F32)).astype(CDT)
            return c

        lax.fori_loop(0, NCH, p2, 0)
        dw_ref[...] = dw_s[0:K, :]

    return _call(
        body, "mixA_bwd", (MW // GW,),
        [pl.BlockSpec((3, S, GW), lambda c: (0, 0, c)), pl.BlockSpec((S, GW), lambda c: (0, c)),
         pl.BlockSpec((K, GW), lambda c: (0, c)), _spspec("conv_a_b", GW, lambda c: c)],
        [pl.BlockSpec((3, S, GW), lambda c: (0, 0, c)), pl.BlockSpec((K, GW), lambda c: (0, c)),
         pl.BlockSpec((8, GW), lambda c: (0, c))],
        [SDS((3, S, MW), CDT), SDS((K, MW), F32), SDS((8, MW), F32)],
        [pltpu.VMEM((H + S, GW), F32), pltpu.VMEM((S + H, GW), F32), pltpu.VMEM((8 * ((K + 7) // 8), GW), F32)])(z, dm, cw, sp)


def _mixD_conv_fwd(z, cw, sp):
    S = z.shape[1]
    K = cw.shape[0]
    H, CH, NCH = _conv_geometry(S, K)

    def body(z_ref, w_ref, b_ref, o_ref, pad):
        pad[0:H, :] = jnp.zeros((H, GW), F32)

        def fill(i, c):
            r = _rows(i, CH)
            pad[pl.ds(pl.multiple_of(i * CH + H, 8), CH), :] = (
                z_ref[0, r, :].astype(F32) * jax.nn.sigmoid(z_ref[1, r, :].astype(F32)))
            return c

        lax.fori_loop(0, NCH, fill, 0)

        def conv(i, c):
            acc = jnp.zeros((CH, GW), F32)
            for k, sh in _causal_taps(pad, i, CH, H, K):
                acc = acc + w_ref[k:k + 1, :] * sh
            o_ref[_rows(i, CH), :] = (acc + b_ref[...]).astype(CDT)
            return c

        lax.fori_loop(0, NCH, conv, 0)

    return _call(
        body, "mixD_conv_fwd", (MW // GW,),
        [pl.BlockSpec((2, S, GW), lambda c: (3, 0, c)), pl.BlockSpec((K, GW), lambda c: (0, c)),
         _spspec("conv_d_b", GW, lambda c: c)],
        pl.BlockSpec((S, GW), lambda c: (0, c)), SDS((S, MW), CDT),
        [pltpu.VMEM((H + S, GW), F32)])(z, cw, sp)


def _mixD_conv_bwd(z, dy, cw):
    S = z.shape[1]
    K = cw.shape[0]
    H, CH, NCH = _conv_geometry(S, K)

    def body(z_ref, dy_ref, w_ref, dz_ref, dw_ref, db_ref, pad, dpad, dw_s):
        pad[0:H, :] = jnp.zeros((H, GW), F32)
        dpad[pl.ds(S, H), :] = jnp.zeros((H, GW), F32)
        dw_s[...] = jnp.zeros_like(dw_s)
        db_ref[...] = jnp.zeros_like(db_ref)

        def fill(i, c):
            r = _rows(i, CH)
            pad[pl.ds(pl.multiple_of(i * CH + H, 8), CH), :] = (
                z_ref[0, r, :].astype(F32) * jax.nn.sigmoid(z_ref[1, r, :].astype(F32)))
            dpad[r, :] = dy_ref[r, :].astype(F32)
            return c

        lax.fori_loop(0, NCH, fill, 0)

        def p1(i, c):
            dyf = dy_ref[_rows(i, CH), :].astype(F32)
            for k, sh in _causal_taps(pad, i, CH, H, K):
                dw_s[k:k + 1, :] += jnp.sum(dyf * sh, axis=0, keepdims=True)
            db_ref[0:1, :] += jnp.sum(dyf, axis=0, keepdims=True)
            return c

        lax.fori_loop(0, NCH, p1, 0)

        def p2(i, c):
            r = _rows(i, CH)
            dq = jnp.zeros((CH, GW), F32)
            for k, sh in _anti_taps(dpad, i, CH, H, K):
                dq = dq + w_ref[k:k + 1, :] * sh
            a = z_ref[0, r, :].astype(F32)
            sg = jax.nn.sigmoid(z_ref[1, r, :].astype(F32))
            dz_ref[0, r, :] = (dq * sg).astype(CDT)
            dz_ref[1, r, :] = (dq * a * sg * (1.0 - sg)).astype(CDT)
            return c

        lax.fori_loop(0, NCH, p2, 0)
        dw_ref[...] = dw_s[0:K, :]

    return _call(
        body, "mixD_conv_bwd", (MW // GW,),
        [pl.BlockSpec((2, S, GW), lambda c: (3, 0, c)), pl.BlockSpec((S, GW), lambda c: (0, c)),
         pl.BlockSpec((K, GW), lambda c: (0, c))],
        [pl.BlockSpec((2, S, GW), lambda c: (0, 0, c)), pl.BlockSpec((K, GW), lambda c: (0, c)),
         pl.BlockSpec((8, GW), lambda c: (0, c))],
        [SDS((2, S, MW), CDT), SDS((K, MW), F32), SDS((8, MW), F32)],
        [pltpu.VMEM((H + S, GW), F32), pltpu.VMEM((S + H, GW), F32), pltpu.VMEM((8 * ((K + 7) // 8), GW), F32)])(z, dy, cw)


def _mixD_ln_bwd(dm, yd, sp):
    S = yd.shape[0]
    TM = min(512, S)

    def body(dm_ref, y_ref, lg_ref, lb_ref, dy_ref, gp_ref):
        @pl.when(pl.program_id(0) == 0)
        def _():
            gp_ref[...] = jnp.zeros_like(gp_ref)

        xh, rs = _ln(y_ref[...].astype(F32))
        _, dsl = _silu_parts(xh * lg_ref[...] + lb_ref[...])
        dl = dm_ref[...].astype(F32) * dsl
        gp_ref[0:1, :] += jnp.sum(dl * xh, axis=0, keepdims=True)
        gp_ref[1:2, :] += jnp.sum(dl, axis=0, keepdims=True)
        dy_ref[...] = _ln_bwd(xh, rs, dl * lg_ref[...]).astype(CDT)

    row = lambda i: (i, 0)
    return _call(
        body, "mixD_ln_bwd", (S // TM,),
        [pl.BlockSpec((TM, MW), row), pl.BlockSpec((TM, MW), row), _spspec("conv_d_ln_g", MW, _zero),
         _spspec("conv_d_ln_b", MW, _zero)],
        [pl.BlockSpec((TM, MW), row), pl.BlockSpec((8, MW), lambda i: (0, 0))],
        [SDS((S, MW), CDT), SDS((8, MW), F32)])(dm, yd, sp, sp)


def _box_causal(val, g):
    s = val
    for d in range(g + 1):
        s = s + pltpu.roll(s, 1 << d, 0)
    return s


def _box_anti(val, g):
    n = val.shape[0]
    s = val
    for d in range(g + 1):
        s = s + pltpu.roll(s, n - (1 << d), 0)
    return s


def _pool_count(i, ch, win):
    t = lax.broadcasted_iota(jnp.int32, (ch, GW), 0) + (i * ch + 1)
    return jnp.minimum(t, win).astype(F32)


def _mixB_fwd(z, wp, sp):
    S = z.shape[1]
    H, CH = 16, min(256, S)
    NCH = S // CH
    assert POOL_WINDOWS == tuple(2 << g for g in range(4))

    def body(p_ref, wp_ref, sc_ref, o_ref, pad):
        pad[0:H, :] = jnp.zeros((H, MW), F32)

        def fill(i, c):
            pad[pl.ds(pl.multiple_of(i * CH + H, 8), CH), :] = p_ref[_rows(i, CH), :].astype(F32)
            return c

        lax.fori_loop(0, NCH, fill, 0)

        def step(i, c):
            r = _rows(i, CH)
            for g in range(4):
                gs = slice(g * GW, (g + 1) * GW)
                val = pad[pl.ds(pl.multiple_of(i * CH, 8), CH + H), gs]
                pooled = _box_causal(val, g)[H:, :] / _pool_count(i, CH, POOL_WINDOWS[g]) - val[H:, :]
                mixed = _nn(pooled.astype(CDT), wp_ref[g].astype(CDT))
                o_ref[r, gs] = (mixed * sc_ref[:, gs]).astype(CDT)
            return c

        lax.fori_loop(0, NCH, step, 0)

    return _call(
        body, "mixB_fwd", (1,),
        [pl.BlockSpec((None, S, MW), lambda i: (3, 0, 0)), pl.BlockSpec((4, GW, GW), lambda i: (0, 0, 0)),
         _spspec("pool_scale", MW, _zero)],
        pl.BlockSpec((S, MW), lambda i: (0, 0)), SDS((S, MW), CDT),
        [pltpu.VMEM((H + S, MW), F32)])(z, wp, sp)


def _mixB_bwd(z, dm, wp, sp):
    S = z.shape[1]
    H, CH = 16, min(256, S)
    NCH = S // CH

    def body(p_ref, dm_ref, wp_ref, sc_ref, dz_ref, dwp_ref, dsc_ref, pad, rpad):
        pad[0:H, :] = jnp.zeros((H, MW), F32)
        rpad[pl.ds(S, H), :] = jnp.zeros((H, MW), F32)
        dwp_ref[...] = jnp.zeros_like(dwp_ref)
        dsc_ref[...] = jnp.zeros_like(dsc_ref)

        def fill(i, c):
            pad[pl.ds(pl.multiple_of(i * CH + H, 8), CH), :] = p_ref[_rows(i, CH), :].astype(F32)
            return c

        lax.fori_loop(0, NCH, fill, 0)

        def p1(i, c):
            r = _rows(i, CH)
            for g in range(4):
                gs = slice(g * GW, (g + 1) * GW)
                cnt = _pool_count(i, CH, POOL_WINDOWS[g])
                val = pad[pl.ds(pl.multiple_of(i * CH, 8), CH + H), gs]
                pooled = (_box_causal(val, g)[H:, :] / cnt - val[H:, :]).astype(CDT)
                w = wp_ref[g].astype(CDT)
                mixed = _nn(pooled, w)
                dmf = dm_ref[r, gs].astype(F32)
                dsc_ref[0:1, gs] += jnp.sum(dmf * mixed, axis=0, keepdims=True)
                dmx = (dmf * sc_ref[:, gs]).astype(CDT)
                dwp_ref[g] += _tn(pooled, dmx)
                rpad[r, gs] = _nt(dmx, w) / cnt
            return c

        lax.fori_loop(0, NCH, p1, 0)

        def p2(i, c):
            r = _rows(i, CH)
            for g in range(4):
                gs = slice(g * GW, (g + 1) * GW)
                val = rpad[pl.ds(pl.multiple_of(i * CH, 8), CH + H), gs]
                dp = _box_anti(val, g)[:CH, :] - val[:CH, :] * _pool_count(i, CH, POOL_WINDOWS[g])
                dz_ref[r, gs] = dp.astype(CDT)
            return c

        lax.fori_loop(0, NCH, p2, 0)

    return _call(
        body, "mixB_bwd", (1,),
        [pl.BlockSpec((None, S, MW), lambda i: (3, 0, 0)), pl.BlockSpec((S, MW), lambda i: (0, 0)),
         pl.BlockSpec((4, GW, GW), lambda i: (0, 0, 0)), _spspec("pool_scale", MW, _zero)],
        [pl.BlockSpec((None, S, MW), lambda i: (0, 0, 0)), pl.BlockSpec((4, GW, GW), lambda i: (0, 0, 0)),
         pl.BlockSpec((8, MW), lambda i: (0, 0))],
        [SDS((1, S, MW), CDT), SDS((4, GW, GW), F32), SDS((8, MW), F32)],
        [pltpu.VMEM((H + S, MW), F32), pltpu.VMEM((S + H, MW), F32)])(z, dm, wp, sp)


def _sgu_mask():
    ci = lax.broadcasted_iota(jnp.int32, (GW, GW), 0) // CHUNK
    cj = lax.broadcasted_iota(jnp.int32, (GW, GW), 1) // CHUNK
    return cj <= ci


def _mixC_fwd(z, ws, bsc, sp):
    S = z.shape[1]
    RB = min(512, S)

    def body(z_ref, lg_ref, lb_ref, ws_ref, bs_ref, o_ref):
        mask = _sgu_mask()
        gu = _gelu(z_ref[0].astype(F32))
        xh, _ = _ln(_gelu(z_ref[1].astype(F32)))
        vn = (xh * lg_ref[...] + lb_ref[...]).astype(CDT)
        for g in range(4):
            gs = slice(g * GW, (g + 1) * GW)
            wm = jnp.where(mask, ws_ref[g], 0.0).astype(CDT)
            for nb in range(RB // GW):
                rs = slice(nb * GW, (nb + 1) * GW)
                mixed = _nn(wm, vn[rs, gs]) + bs_ref[g]
                o_ref[rs, gs] = (gu[rs, gs] * mixed).astype(CDT)

    return _call(
        body, "mixC_fwd", (S // RB,),
        [pl.BlockSpec((2, RB, MW), lambda i: (2, i, 0)), _spspec("sgu_ln_g", MW, _zero), _spspec("sgu_ln_b", MW, _zero),
         pl.BlockSpec((4, GW, GW), lambda i: (0, 0, 0)), pl.BlockSpec((4, GW, 1), lambda i: (0, 0, 0))],
        pl.BlockSpec((RB, MW), lambda i: (i, 0)), SDS((S, MW), CDT))(z, sp, sp, ws, bsc)


def _mixC_bwd(z, dm, ws, bsc, sp):
    S = z.shape[1]
    RB = min(512, S)
    NR = S // RB

    def body(z_ref, dm_ref, lg_ref, lb_ref, ws_ref, bs_ref, dz_ref, dws_ref, dbs_ref, gp_ref, dvn_s):
        i = pl.program_id(0)

        @pl.when(i == 0)
        def _():
            dws_ref[...] = jnp.zeros_like(dws_ref)
            dbs_ref[...] = jnp.zeros_like(dbs_ref)
            gp_ref[...] = jnp.zeros_like(gp_ref)

        mask = _sgu_mask()
        gu, dgu = _gelu_parts(z_ref[0].astype(F32))
        gv, dgv = _gelu_parts(z_ref[1].astype(F32))
        xh, rs_ = _ln(gv)
        vn = (xh * lg_ref[...] + lb_ref[...]).astype(CDT)
        dmf = dm_ref[...].astype(F32)
        for g in range(4):
            gs = slice(g * GW, (g + 1) * GW)
            wm = jnp.where(mask, ws_ref[g], 0.0).astype(CDT)
            for nb in range(RB // GW):
                rs = slice(nb * GW, (nb + 1) * GW)
                vb = vn[rs, gs]
                mixed = _nn(wm, vb) + bs_ref[g]
                dz_ref[0, rs, gs] = (dmf[rs, gs] * mixed * dgu[rs, gs]).astype(CDT)
                dmx = dmf[rs, gs] * gu[rs, gs]
                dbs_ref[g] += dmx
                dmxc = dmx.astype(CDT)
                dws_ref[g] += _nt(dmxc, vb)
                dvn_s[rs, gs] = _tn(wm, dmxc)
        dvn = dvn_s[...]
        gp_ref[0:1, :] += jnp.sum(dvn * xh, axis=0, keepdims=True)
        gp_ref[1:2, :] += jnp.sum(dvn, axis=0, keepdims=True)
        dz_ref[1] = (_ln_bwd(xh, rs_, dvn * lg_ref[...]) * dgv).astype(CDT)

        @pl.when(i == NR - 1)
        def _():
            for g in range(4):
                dws_ref[g] = jnp.where(mask, dws_ref[g], 0.0)
                dbs_ref[g] = jnp.broadcast_to(jnp.sum(dbs_ref[g], axis=1, keepdims=True), (GW, GW))

    full3 = lambda i: (0, 0, 0)
    return _call(
        body, "mixC_bwd", (NR,),
        [pl.BlockSpec((2, RB, MW), lambda i: (2, i, 0)), pl.BlockSpec((RB, MW), lambda i: (i, 0)),
         _spspec("sgu_ln_g", MW, _zero), _spspec("sgu_ln_b", MW, _zero),
         pl.BlockSpec((4, GW, GW), full3), pl.BlockSpec((4, GW, 1), full3)],
        [pl.BlockSpec((2, RB, MW), lambda i: (0, i, 0)), pl.BlockSpec((4, GW, GW), full3), pl.BlockSpec((4, GW, GW), full3),
         pl.BlockSpec((8, MW), lambda i: (0, 0))],
        [SDS((2, S, MW), CDT), SDS((4, GW, GW), F32), SDS((4, GW, GW), F32), SDS((8, MW), F32)],
        [pltpu.VMEM((RB, MW), F32)])(z, dm, sp, sp, ws, bsc)


def _unpack_wb(wb_ref, wbf):
    for j in range(NS):
        for k in range(4):
            wbf[k, :, j * GW:(j + 1) * GW] = wb_ref[j, k]


def _merge_fwd(ma, mb, mc, yd, g, wb, pwo, x, sp):
    S, D = x.shape
    TM = min(256, S)

    def body(ma_ref, mb_ref, mc_ref, yd_ref, g_ref, wb_ref, wo_ref, x_ref, lg_ref, lb_ref, qg_ref,
             xo_ref, md_ref, yk_ref, mg_ref, mo_ref, wbf):
        @pl.when(pl.program_id(0) == 0)
        def _():
            _unpack_wb(wb_ref, wbf)

        xh, _ = _ln(yd_ref[...].astype(F32))
        sl, _ = _silu_parts(xh * lg_ref[...] + lb_ref[...])
        md = sl.astype(CDT)
        md_ref[...] = md
        merged = jnp.zeros((TM, D), F32)
        for k, m in enumerate((ma_ref[...], mb_ref[...], mc_ref[...], md)):
            yk = _nn(m, wbf[k])
            yk_ref[k] = yk.astype(CDT)
            merged = merged + g_ref[k].astype(F32) * yk
        mgc = merged.astype(CDT)
        mg_ref[...] = mgc
        mo = _nn(mgc, wo_ref[...].reshape(D, D))
        mo_ref[...] = mo.astype(CDT)
        n, _ = _rms(mo)
        xo_ref[...] = x_ref[...] + n * qg_ref[...]

    row = lambda i: (i, 0)
    rowm = pl.BlockSpec((TM, MW), row)
    rowd = pl.BlockSpec((TM, D), row)
    row4 = pl.BlockSpec((4, TM, D), lambda i: (0, i, 0))
    return _call(
        body, "merge_fwd", (S // TM,),
        [rowm, rowm, rowm, rowm, row4, pl.BlockSpec((NS, 4, MW, GW), lambda i: (0, 0, 0, 0)),
         pl.BlockSpec((NS, GW, D), lambda i: (0, 0, 0)), rowd,
         _spspec("conv_d_ln_g", MW, _zero), _spspec("conv_d_ln_b", MW, _zero), _spspec("mix_post_g", D, _zero)],
        [rowd, rowm, row4, rowd, rowd],
        [SDS((S, D), F32), SDS((S, MW), CDT), SDS((4, S, D), CDT), SDS((S, D), CDT), SDS((S, D), CDT)],
        [pltpu.VMEM((4, MW, D), CDT)])(ma, mb, mc, yd, g, wb, pwo, x, sp, sp, sp)


def _merge_bwd_act(dxo, mo, g, yk, wb, pwo, sp, dep):
    S, D = dxo.shape
    TM = min(256, S)

    def body(dxo_ref, mo_ref, g_ref, yk_ref, wb_ref, wo_ref, qg_ref, dep_ref, dmo_ref, dm_ref, dgp_ref, dyk_ref, gp_ref, wbf):
        @pl.when(pl.program_id(0) == 0)
        def _():
            gp_ref[...] = jnp.zeros_like(gp_ref)
            _unpack_wb(wb_ref, wbf)

        n, r = _rms(mo_ref[...].astype(F32))
        dmo, dg = _rms_bwd(n, r, qg_ref[...], dxo_ref[...])
        gp_ref[0:1, :] += dg
        dmoc = dmo.astype(CDT)
        dmo_ref[...] = dmoc
        dmg = _nt(dmoc, wo_ref[...].reshape(D, D))
        for k in range(4):
            gk = g_ref[k].astype(F32)
            dyk = (dmg * gk).astype(CDT)
            dyk_ref[k] = dyk
            dgp_ref[k] = (dmg * yk_ref[k].astype(F32) * gk * (1.0 - gk)).astype(CDT)
            dm_ref[k] = _nt(dyk, wbf[k]).astype(CDT)

    rowd = pl.BlockSpec((TM, D), lambda i: (i, 0))
    row4 = pl.BlockSpec((4, TM, D), lambda i: (0, i, 0))
    return _call(
        body, "merge_bwd_act", (S // TM,),
        [rowd, rowd, row4, row4, pl.BlockSpec((NS, 4, MW, GW), lambda i: (0, 0, 0, 0)),
         pl.BlockSpec((NS, GW, D), lambda i: (0, 0, 0)), _spspec("mix_post_g", D, _zero), _ANY],
        [rowd, pl.BlockSpec((4, TM, MW), lambda i: (0, i, 0)), row4, row4, pl.BlockSpec((8, D), lambda i: (0, 0))],
        [SDS((S, D), CDT), SDS((4, S, MW), CDT), SDS((4, S, D), CDT), SDS((4, S, D), CDT), SDS((8, D), F32)],
        [pltpu.VMEM((4, MW, D), CDT)])(dxo, mo, g, yk, wb, pwo, sp, dep)


def _merge_bwd_w(ma, mb, mc, md, dyk, mg, dmo):
    S, D = dmo.shape
    TK = min(512, S)
    NK = S // TK

    def body(ma_ref, mb_ref, mc_ref, md_ref, dyk_ref, mg_ref, dmo_ref, gwb_ref, gwo_ref, accb, acco):
        k = pl.program_id(0)

        @pl.when(k == 0)
        def _():
            accb[...] = jnp.zeros_like(accb)
            acco[...] = jnp.zeros_like(acco)

        for b, m in enumerate((ma_ref, mb_ref, mc_ref, md_ref)):
            accb[b] += _tn(m[...], dyk_ref[b])
        acco[...] += _tn(mg_ref[...], dmo_ref[...])

        @pl.when(k == NK - 1)
        def _():
            for j in range(NS):
                for b in range(4):
                    gwb_ref[j, b] = accb[b, :, j * GW:(j + 1) * GW].astype(CDT)
                gwo_ref[j] = acco[j * GW:(j + 1) * GW, :].astype(CDT)

    rowm = pl.BlockSpec((TK, MW), lambda k: (k, 0))
    rowd = pl.BlockSpec((TK, D), lambda k: (k, 0))
    return _call(
        body, "merge_bwd_w", (NK,),
        [rowm, rowm, rowm, rowm, pl.BlockSpec((4, TK, D), lambda k: (0, k, 0)), rowd, rowd],
        [pl.BlockSpec((NS, 4, MW, GW), lambda k: (0, 0, 0, 0)), pl.BlockSpec((NS, GW, D), lambda k: (0, 0, 0))],
        [SDS((NS, 4, MW, GW), CDT), SDS((NS, GW, D), CDT)],
        [pltpu.VMEM((4, MW, D), F32), pltpu.VMEM((D, D), F32)])(ma, mb, mc, md, dyk, mg, dmo)


def _xa_kv(mem, sp, pxa):
    M, D = mem.shape

    def body(m_ref, g_ref, wk_ref, wv_ref, mn_ref, k_ref, v_ref):
        n, _ = _rms(m_ref[...])
        mn = (n * g_ref[...]).astype(CDT)
        mn_ref[...] = mn
        k_ref[...] = _nn(mn, wk_ref[...].reshape(D, D)).astype(CDT)
        v_ref[...] = _nn(mn, wv_ref[...].reshape(D, D)).astype(CDT)

    full = pl.BlockSpec((M, D), lambda i: (0, 0))
    return _call(
        body, "xa_kv", (1,),
        [full, _spspec("mem_g", D, _zero), pl.BlockSpec((NS, GW, D), lambda i: (0, 1, 0)),
         pl.BlockSpec((NS, GW, D), lambda i: (0, 2, 0))],
        [full, full, full], [SDS((M, D), CDT)] * 3)(mem, sp, pxa, pxa)


def _softmax(s):
    e = jnp.exp(s - jnp.max(s, axis=-1, keepdims=True))
    return e / jnp.sum(e, axis=-1, keepdims=True)


def _xa_fwd(x, kk, vv, sp, pxa):
    S, D = x.shape
    M = kk.shape[0]
    TM = min(512, S)
    HD = D // XA_HEADS
    scale = HD ** -0.5

    def body(x_ref, k_ref, v_ref, pg_ref, qg_ref, wq_ref, wo_ref, xo_ref, hb_ref, q_ref, o_ref, po_ref):
        n, _ = _rms(x_ref[...])
        hb = (n * pg_ref[...]).astype(CDT)
        hb_ref[...] = hb
        q = _nn(hb, wq_ref[...].reshape(D, D)).astype(CDT)
        q_ref[...] = q
        for h in range(XA_HEADS):
            hs = slice(h * HD, (h + 1) * HD)
            p = _softmax(_nt(q[:, hs], k_ref[:, hs]) * scale)
            o_ref[:, hs] = _nn(p.astype(CDT), v_ref[:, hs]).astype(CDT)
        po = _nn(o_ref[...], wo_ref[...].reshape(D, D))
        po_ref[...] = po.astype(CDT)
        n, _ = _rms(po)
        xo_ref[...] = x_ref[...] + n * qg_ref[...]

    row = pl.BlockSpec((TM, D), lambda i: (i, 0))
    full = pl.BlockSpec((M, D), lambda i: (0, 0))
    return _call(
        body, "xa_fwd", (S // TM,),
        [row, full, full, _spspec("xa_pre_g", D, _zero), _spspec("xa_post_g", D, _zero),
         pl.BlockSpec((NS, GW, D), lambda i: (0, 0, 0)), pl.BlockSpec((NS, GW, D), lambda i: (0, 3, 0))],
        [row] * 5, [SDS((S, D), F32)] + [SDS((S, D), CDT)] * 4)(x, kk, vv, sp, sp, pxa, pxa)


def _xa_bwd_act(dxo, x, po, q, kk, vv, sp, pxa, dep):
    S, D = x.shape
    M = kk.shape[0]
    TM = min(512, S)
    HD = D // XA_HEADS
    scale = HD ** -0.5

    def body(dxo_ref, x_ref, po_ref, q_ref, k_ref, v_ref, pg_ref, qg_ref, wq_ref, wo_ref, dep_ref,
             dx_ref, dpo_ref, dq_ref, dk_ref, dv_ref, gp_ref):
        @pl.when(pl.program_id(0) == 0)
        def _():
            gp_ref[...] = jnp.zeros_like(gp_ref)
            dk_ref[...] = jnp.zeros_like(dk_ref)
            dv_ref[...] = jnp.zeros_like(dv_ref)

        n, r = _rms(po_ref[...].astype(F32))
        dpo, dg = _rms_bwd(n, r, qg_ref[...], dxo_ref[...])
        gp_ref[1:2, :] += dg
        dpoc = dpo.astype(CDT)
        dpo_ref[...] = dpoc
        do = _nt(dpoc, wo_ref[...].reshape(D, D)).astype(CDT)
        for h in range(XA_HEADS):
            hs = slice(h * HD, (h + 1) * HD)
            qh = q_ref[:, hs]
            p = _softmax(_nt(qh, k_ref[:, hs]) * scale)
            pc = p.astype(CDT)
            dv_ref[:, hs] += _tn(pc, do[:, hs])
            dp = _nt(do[:, hs], v_ref[:, hs])
            ds = (p * (dp - jnp.sum(p * dp, axis=-1, keepdims=True)) * scale).astype(CDT)
            dq_ref[:, hs] = _nn(ds, k_ref[:, hs]).astype(CDT)
            dk_ref[:, hs] += _tn(ds, qh)
        dhb = _nt(dq_ref[...], wq_ref[...].reshape(D, D))
        n, r = _rms(x_ref[...])
        dx, dg = _rms_bwd(n, r, pg_ref[...], dhb)
        dx_ref[...] = dxo_ref[...] + dx
        gp_ref[0:1, :] += dg

    row = pl.BlockSpec((TM, D), lambda i: (i, 0))
    full = pl.BlockSpec((M, D), lambda i: (0, 0))
    return _call(
        body, "xa_bwd_act", (S // TM,),
        [row, row, row, row, full, full, _spspec("xa_pre_g", D, _zero), _spspec("xa_post_g", D, _zero),
         pl.BlockSpec((NS, GW, D), lambda i: (0, 0, 0)), pl.BlockSpec((NS, GW, D), lambda i: (0, 3, 0)), _ANY],
        [row, row, row, full, full, pl.BlockSpec((8, D), lambda i: (0, 0))],
        [SDS((S, D), F32), SDS((S, D), CDT), SDS((S, D), CDT), SDS((M, D), F32), SDS((M, D), F32), SDS((8, D), F32)],
    )(dxo, x, po, q, kk, vv, sp, sp, pxa, pxa, dep)


def _xa_bwd_w(hb, dq, o, dpo, mn, dk, dv):
    S, D = hb.shape
    M = mn.shape[0]
    TK = min(512, S)
    NK = S // TK

    def body(hb_ref, dq_ref, o_ref, dpo_ref, mn_ref, dk_ref, dv_ref, g_ref, accq, acco):
        k = pl.program_id(0)

        @pl.when(k == 0)
        def _():
            accq[...] = jnp.zeros_like(accq)
            acco[...] = jnp.zeros_like(acco)

        accq[...] += _tn(hb_ref[...], dq_ref[...])
        acco[...] += _tn(o_ref[...], dpo_ref[...])

        @pl.when(k == NK - 1)
        def _():
            gk = _tn(mn_ref[...], dk_ref[...].astype(CDT))
            gv = _tn(mn_ref[...], dv_ref[...].astype(CDT))
            for j in range(NS):
                rs = slice(j * GW, (j + 1) * GW)
                g_ref[j, 0:GW, :] = accq[rs, :].astype(CDT)
                g_ref[j, GW:2 * GW, :] = gk[rs, :].astype(CDT)
                g_ref[j, 2 * GW:3 * GW, :] = gv[rs, :].astype(CDT)
                g_ref[j, 3 * GW:4 * GW, :] = acco[rs, :].astype(CDT)

    rowb = pl.BlockSpec((TK, D), lambda k: (k, 0))
    full = pl.BlockSpec((M, D), lambda k: (0, 0))
    return _call(
        body, "xa_bwd_w", (NK,),
        [rowb, rowb, rowb, rowb, full, full, full],
        pl.BlockSpec((NS, 4 * GW, D), lambda k: (0, 0, 0)), SDS((NS, 4 * GW, D), CDT),
        [pltpu.VMEM((D, D), F32), pltpu.VMEM((D, D), F32)])(hb, dq, o, dpo, mn, dk, dv)


def _xa_kv_bwd(mem, dk, dv, sp, pxa):
    M, D = mem.shape

    def body(m_ref, dk_ref, dv_ref, wk_ref, wv_ref, gp_ref):
        dmn = _nt(dk_ref[...].astype(CDT), wk_ref[...].reshape(D, D)) + _nt(dv_ref[...].astype(CDT), wv_ref[...].reshape(D, D))
        n, _ = _rms(m_ref[...])
        gp_ref[...] = jnp.zeros_like(gp_ref)
        gp_ref[0:1, :] = jnp.sum(dmn * n, axis=0, keepdims=True)

    full = pl.BlockSpec((M, D), lambda i: (0, 0))
    return _call(
        body, "xa_kv_bwd", (1,),
        [full, full, full, pl.BlockSpec((NS, GW, D), lambda i: (0, 1, 0)), pl.BlockSpec((NS, GW, D), lambda i: (0, 2, 0))],
        pl.BlockSpec((8, D), lambda i: (0, 0)), SDS((8, D), F32))(mem, dk, dv, pxa, pxa)


def _loss_head(y, t):
    S, D = y.shape
    TM = min(512, S)

    def body(y_ref, t_ref, dy_ref, l_ref):
        @pl.when(pl.program_id(0) == 0)
        def _():
            l_ref[...] = jnp.zeros_like(l_ref)

        e = y_ref[...] - t_ref[...]
        dy_ref[...] = e * (1.0 / D)
        l_ref[...] += 0.5 * jnp.sum(jnp.mean(e * e, axis=-1, keepdims=True), axis=0, keepdims=True)

    row = pl.BlockSpec((TM, D), lambda i: (i, 0))
    return _call(body, "loss_head", (S // TM,), [row, row], [row, pl.BlockSpec((8, 128), lambda i: (0, 0))],
                 [SDS((S, D), F32), SDS((8, 128), F32)])(y, t)


def _row_tile(rows, cols, limit=1 << 18, step=8):
    if rows * cols <= limit or rows % step:
        return rows
    best = step
    for t in range(step, rows + 1, step):
        if rows % t == 0 and t * cols <= limit:
            best = t
    return best


def _slot_sum(r):
    _, R, C = r.shape
    TR = _row_tile(R, C * NS, limit=1 << 21, step=16)

    def body(r_ref, o_ref):
        acc = r_ref[0].astype(F32)
        for j in range(1, NS):
            acc = acc + r_ref[j].astype(F32)
        o_ref[...] = acc

    return _call(body, "slot_sum", (R // TR,), [pl.BlockSpec((NS, TR, C), lambda i: (0, i, 0))],
                 pl.BlockSpec((TR, C), lambda i: (i, 0)), SDS((R, C), F32))(r)


def _adamw(w, g, m, v):
    shape = w.shape
    C = shape[-1]
    R = w.size // C
    TR = _row_tile(R, C)
    c1 = 1.0 - ADAM_B1 ** ADAM_STEP
    c2 = 1.0 - ADAM_B2 ** ADAM_STEP

    def body(w_ref, g_ref, m_ref, v_ref, d_ref, nm_ref, nv_ref):
        gg = g_ref[...]
        nm = ADAM_B1 * m_ref[...] + (1.0 - ADAM_B1) * gg
        nv = ADAM_B2 * v_ref[...] + (1.0 - ADAM_B2) * (gg * gg)
        nm_ref[...] = nm
        nv_ref[...] = nv
        d_ref[...] = -ADAM_LR * ((nm / c1) / (jnp.sqrt(nv / c2) + ADAM_EPS) + ADAM_WD * w_ref[...])

    blk = pl.BlockSpec((TR, C), lambda i: (i, 0))
    outs = _call(body, "adamw", (R // TR,), [blk] * 4, [blk] * 3, [SDS((R, C), F32)] * 3)(
        w.reshape(R, C), g.reshape(R, C), m.reshape(R, C), v.reshape(R, C))
    return tuple(o.reshape(shape) for o in outs)


def _exchange(arrs, scatter, name):
    n = len(arrs)
    np_ = NS - 1

    def body(*refs):
        ins, outs = refs[:n], refs[n:2 * n]
        send_sems, recv_sems, loc_sems = refs[2 * n:]
        x, y, c = lax.axis_index("x"), lax.axis_index("y"), lax.axis_index("c")
        me = 4 * x + 2 * y + c
        peers = []
        for f in range(1, NS):
            px = 1 - x if f & 4 else x
            py = 1 - y if f & 2 else y
            pc = 1 - c if f & 1 else c
            peers.append(((px, py, pc), 4 * px + 2 * py + pc))

        def src(a, pid):
            return ins[a].at[pid] if scatter else ins[a]

        local = [pltpu.make_async_copy(src(a, me), outs[a].at[me], loc_sems.at[a]) for a in range(n)]
        for cp in local:
            cp.start()
        sends = []
        for a in range(n):
            for f, (dev, pid) in enumerate(peers):
                sends.append(pltpu.make_async_remote_copy(
                    src_ref=src(a, pid), dst_ref=outs[a].at[me], send_sem=send_sems.at[a * np_ + f],
                    recv_sem=recv_sems.at[a * np_ + f], device_id=dev, device_id_type=pl.DeviceIdType.MESH))
        for cp in sends:
            cp.start()
        for a in range(n):
            for f, (dev, pid) in enumerate(peers):
                pltpu.make_async_remote_copy(
                    src_ref=src(a, pid), dst_ref=outs[a].at[pid], send_sem=send_sems.at[a * np_ + f],
                    recv_sem=recv_sems.at[a * np_ + f], device_id=dev, device_id_type=pl.DeviceIdType.MESH).wait_recv()
        for cp in sends:
            cp.wait_send()
        for cp in local:
            cp.wait()

    out_shape = [SDS(a.shape if scatter else (NS,) + a.shape, a.dtype) for a in arrs]
    anyspec = pl.BlockSpec(memory_space=pl.ANY)
    outs = pl.pallas_call(
        body, name=name, in_specs=[anyspec] * n, out_specs=[anyspec] * n, out_shape=out_shape,
        scratch_shapes=[pltpu.SemaphoreType.DMA((n * np_,)), pltpu.SemaphoreType.DMA((n * np_,)),
                        pltpu.SemaphoreType.DMA((n,))],
        compiler_params=pltpu.CompilerParams(has_side_effects=True))(*arrs)
    return list(outs)


def _peers():
    x, y, c = lax.axis_index("x"), lax.axis_index("y"), lax.axis_index("c")
    out = []
    for f in range(1, NS):
        px = 1 - x if f & 4 else x
        py = 1 - y if f & 2 else y
        pc = 1 - c if f & 1 else c
        out.append(((px, py, pc), 4 * px + 2 * py + pc))
    return 4 * x + 2 * y + c, out


def _exchange_copies(ins, lands, scatter, send_sems, recv_sems, loc_sems):
    me, peers = _peers()
    np_ = NS - 1

    def src(a, pid):
        return ins[a].at[pid] if scatter[a] else ins[a]

    def rcopy(a, f, dev, land_slot):
        return pltpu.make_async_remote_copy(
            src_ref=src(a, peers[f][1]), dst_ref=lands[a].at[land_slot], send_sem=send_sems.at[a * np_ + f],
            recv_sem=recv_sems.at[a * np_ + f], device_id=dev, device_id_type=pl.DeviceIdType.MESH)

    local = [pltpu.make_async_copy(src(a, me), lands[a].at[me], loc_sems.at[a]) for a in range(len(ins))]
    sends = [rcopy(a, f, dev, me) for a in range(len(ins)) for f, (dev, _) in enumerate(peers)]
    arrivals = [rcopy(a, f, dev, pid) for a in range(len(ins)) for f, (dev, pid) in enumerate(peers)]
    return local, sends, arrivals


_HBM = pl.BlockSpec(memory_space=pltpu.HBM)
_SEM = pl.BlockSpec(memory_space=pltpu.SEMAPHORE)
_ANY = pl.BlockSpec(memory_space=pl.ANY)


def _exchange_start(arrs, scatter, after, name):
    n = len(arrs)
    np_ = NS - 1
    lands = [lax.empty(a.shape if sc else (NS,) + a.shape, a.dtype) for a, sc in zip(arrs, scatter)]

    def body(*refs):
        ins, lnd = refs[:n], refs[n:2 * n]
        send_sems, recv_sems, loc_sems = refs[2 * n + 1:2 * n + 4]
        token = refs[-1]
        local, sends, _ = _exchange_copies(ins, lnd, scatter, send_sems, recv_sems, loc_sems)
        for cp in local + sends:
            cp.start()
        token[...] = jnp.zeros_like(token)

    hbm = lambda a: pltpu.HBM(a.shape, a.dtype)
    outs = pl.pallas_call(
        body, name=name,
        out_shape=(pltpu.SemaphoreType.DMA((n * np_,)), pltpu.SemaphoreType.DMA((n * np_,)), pltpu.SemaphoreType.DMA((n,)),
                   *[hbm(a) for a in arrs], *[hbm(a) for a in lands], SDS((8, 128), F32)),
        in_specs=[_HBM] * (2 * n) + [_ANY],
        out_specs=(_SEM, _SEM, _SEM, *([_HBM] * (2 * n)), pl.BlockSpec(memory_space=pltpu.VMEM)),
        input_output_aliases={i: 3 + i for i in range(2 * n)},
        compiler_params=pltpu.CompilerParams(has_side_effects=pltpu.SideEffectType.DATAFLOW_SIDE_EFFECTING),
    )(*[pltpu.with_memory_space_constraint(a, pltpu.HBM) for a in list(arrs) + lands], after)
    return {"sems": outs[:3], "ins": outs[3:3 + n], "lands": outs[3 + n:3 + 2 * n], "token": outs[-1], "scatter": scatter}


def _exchange_wait(h, after, name):
    n = len(h["ins"])
    scatter = h["scatter"]
    after = list(after) if isinstance(after, (list, tuple)) else [after]

    def body(*refs):
        ins, lnd = refs[:n], refs[n:2 * n]
        send_sems, recv_sems, loc_sems = refs[2 * n:2 * n + 3]
        local, sends, arrivals = _exchange_copies(ins, lnd, scatter, send_sems, recv_sems, loc_sems)
        for cp in sends:
            cp.wait_send()
        for cp in arrivals:
            cp.wait_recv()
        for cp in local:
            cp.wait()

    hbm = lambda a: pltpu.HBM(a.shape, a.dtype)
    outs = pl.pallas_call(
        body, name=name,
        out_shape=tuple(hbm(a) for a in list(h["ins"]) + list(h["lands"])),
        in_specs=[_HBM] * (2 * n) + [_SEM] * 3 + [_ANY] * len(after),
        out_specs=tuple([_HBM] * (2 * n)),
        input_output_aliases={i: i for i in range(2 * n)},
        compiler_params=pltpu.CompilerParams(has_side_effects=pltpu.SideEffectType.DATAFLOW_SIDE_EFFECTING),
    )(*h["ins"], *h["lands"], *h["sems"], *after)
    return list(outs[n:])


def _hgather_copies(ins, lands, send_a, recv_a, send_b, recv_b, loc_sems):
    x, y, c = lax.axis_index("x"), lax.axis_index("y"), lax.axis_index("c")
    me = 4 * x + 2 * y + c
    sib = (x, y, 1 - c)
    chips = [(1 - x, y), (x, 1 - y), (1 - x, 1 - y)]
    slot = lambda px, py, pc: 4 * px + 2 * py + pc

    def rcopy(src, dst, ssem, rsem, dev):
        return pltpu.make_async_remote_copy(src_ref=src, dst_ref=dst, send_sem=ssem, recv_sem=rsem, device_id=dev,
                                            device_id_type=pl.DeviceIdType.MESH)

    local, s1, a1, s2, a2 = [], [], [], [], []
    for a in range(len(ins)):
        first = [(sib, slot(x, y, 1 - c))] + [((px, py, c), slot(px, py, c)) for px, py in chips]
        for k, (dev, origin) in enumerate(first if send_a is not None else ()):
            s1.append(rcopy(ins[a], lands[a].at[me], send_a.at[4 * a + k], recv_a.at[4 * a + k], dev))
            a1.append(rcopy(ins[a], lands[a].at[origin], send_a.at[4 * a + k], recv_a.at[4 * a + k], dev))
        if send_a is not None:
            local.append(pltpu.make_async_copy(ins[a], lands[a].at[me], loc_sems.at[a]))
        for k, (px, py) in enumerate(chips if send_b is not None else ()):
            mine, theirs = lands[a].at[slot(px, py, c)], lands[a].at[slot(px, py, 1 - c)]
            s2.append(rcopy(mine, mine, send_b.at[3 * a + k], recv_b.at[3 * a + k], sib))
            a2.append(rcopy(mine, theirs, send_b.at[3 * a + k], recv_b.at[3 * a + k], sib))
    return local, s1, a1, s2, a2


def _hgather_start(arrs, after, name):
    n = len(arrs)
    lands = [lax.empty((NS,) + a.shape, a.dtype) for a in arrs]

    def body(*refs):
        ins, lnd = refs[:n], refs[n:2 * n]
        send_a, recv_a, loc_sems = refs[2 * n + 1:2 * n + 4]
        token = refs[-1]
        local, s1, _, _, _ = _hgather_copies(ins, lnd, send_a, recv_a, None, None, loc_sems)
        for cp in local + s1:
            cp.start()
        token[...] = jnp.zeros_like(token)

    hbm = lambda a: pltpu.HBM(a.shape, a.dtype)
    outs = pl.pallas_call(
        body, name=name,
        out_shape=(pltpu.SemaphoreType.DMA((4 * n,)), pltpu.SemaphoreType.DMA((4 * n,)), pltpu.SemaphoreType.DMA((n,)),
                   *[hbm(a) for a in arrs], *[hbm(a) for a in lands], SDS((8, 128), F32)),
        in_specs=[_HBM] * (2 * n) + [_ANY],
        out_specs=(_SEM, _SEM, _SEM, *([_HBM] * (2 * n)), pl.BlockSpec(memory_space=pltpu.VMEM)),
        input_output_aliases={i: 3 + i for i in range(2 * n)},
        compiler_params=pltpu.CompilerParams(has_side_effects=pltpu.SideEffectType.DATAFLOW_SIDE_EFFECTING),
    )(*[pltpu.with_memory_space_constraint(a, pltpu.HBM) for a in list(arrs) + lands], after)
    return {"sems": outs[:3], "ins": outs[3:3 + n], "lands": outs[3 + n:3 + 2 * n], "token": outs[-1]}


def _hgather_forward(h, after, name):
    n = len(h["ins"])

    def body(*refs):
        ins, lnd = refs[:n], refs[n:2 * n]
        send_a, recv_a, loc_sems = refs[2 * n:2 * n + 3]
        send_b, recv_b = refs[2 * n + 4:2 * n + 6]
        token = refs[-1]
        local, s1, a1, s2, _ = _hgather_copies(ins, lnd, send_a, recv_a, send_b, recv_b, loc_sems)
        for cp in s1:
            cp.wait_send()
        for cp in a1:
            cp.wait_recv()
        for cp in local:
            cp.wait()
        for cp in s2:
            cp.start()
        token[...] = jnp.zeros_like(token)

    hbm = lambda a: pltpu.HBM(a.shape, a.dtype)
    outs = pl.pallas_call(
        body, name=name,
        out_shape=(pltpu.SemaphoreType.DMA((3 * n,)), pltpu.SemaphoreType.DMA((3 * n,)),
                   *[hbm(a) for a in list(h["ins"]) + list(h["lands"])], SDS((8, 128), F32)),
        in_specs=[_HBM] * (2 * n) + [_SEM] * 3 + [_ANY],
        out_specs=(_SEM, _SEM, *([_HBM] * (2 * n)), pl.BlockSpec(memory_space=pltpu.VMEM)),
        input_output_aliases={i: 2 + i for i in range(2 * n)},
        compiler_params=pltpu.CompilerParams(has_side_effects=pltpu.SideEffectType.DATAFLOW_SIDE_EFFECTING),
    )(*h["ins"], *h["lands"], *h["sems"], after)
    return {"sems": outs[:2], "ins": outs[2:2 + n], "lands": outs[2 + n:2 + 2 * n], "token": outs[-1]}


def _hgather_wait(h, after, name):
    n = len(h["ins"])

    def body(*refs):
        ins, lnd = refs[:n], refs[n:2 * n]
        send_b, recv_b = refs[2 * n:2 * n + 2]
        _, _, _, s2, a2 = _hgather_copies(ins, lnd, None, None, send_b, recv_b, None)
        for cp in s2:
            cp.wait_send()
        for cp in a2:
            cp.wait_recv()

    hbm = lambda a: pltpu.HBM(a.shape, a.dtype)
    outs = pl.pallas_call(
        body, name=name,
        out_shape=tuple(hbm(a) for a in list(h["ins"]) + list(h["lands"])),
        in_specs=[_HBM] * (2 * n) + [_SEM] * 2 + [_ANY],
        out_specs=tuple([_HBM] * (2 * n)),
        input_output_aliases={i: i for i in range(2 * n)},
        compiler_params=pltpu.CompilerParams(has_side_effects=pltpu.SideEffectType.DATAFLOW_SIDE_EFFECTING),
    )(*h["ins"], *h["lands"], *h["sems"], after)
    return list(outs[n:])


_W_NAMES = ("ffn1_pre_g", "ffn1_post_g", "ffn1_w1", "ffn1_w3", "ffn1_w2", "mix_pre_g", "mix_post_g", "w_in", "conv_a_w",
            "conv_a_b", "pool_w", "pool_scale", "sgu_ln_g", "sgu_ln_b", "sgu_ws", "sgu_b", "conv_d_w", "conv_d_b",
            "conv_d_ln_g", "conv_d_ln_b", "w_branch", "w_gate", "b_gate", "w_o", "xa_pre_g", "xa_post_g", "mem_g",
            "xa_wq", "xa_wk", "xa_wv", "xa_wo", "ffn2_pre_g", "ffn2_post_g", "ffn2_w1", "ffn2_w3", "ffn2_w2")
_REP_NAMES = tuple(n for n, _ in _SP_NAMES) + ("pool_w", "sgu_ws", "sgu_b", "conv_a_w", "conv_d_w")


def _t(w):
    return jnp.swapaxes(w, -1, -2)


def _step(x, mem, loss_target, W, M, V):
    L = W["w_in"].shape[0]
    S, D = x.shape[1], x.shape[2]
    x0 = x.reshape(S, D)
    memf = mem.reshape(mem.shape[1], D)
    me = 4 * lax.axis_index("x") + 2 * lax.axis_index("y") + lax.axis_index("c")
    FS = W["ffn1_w2"].shape[1]
    KA, KD = W["conv_a_w"].shape[1], W["conv_d_w"].shape[1]
    CS = W["conv_a_w"].shape[2]

    pf1 = jnp.concatenate([_t(W["ffn1_w1"]), _t(W["ffn1_w3"]), W["ffn1_w2"]], axis=1).astype(CDT)
    pf2 = jnp.concatenate([_t(W["ffn2_w1"]), _t(W["ffn2_w3"]), W["ffn2_w2"]], axis=1).astype(CDT)
    pma = jnp.concatenate([_t(W["w_in"]), _t(W["w_gate"])], axis=1).astype(CDT)
    pwo = W["w_o"].astype(CDT)
    pxa = jnp.concatenate([W["xa_wq"], W["xa_wk"], W["xa_wv"], W["xa_wo"]], axis=1).astype(CDT)
    wbs = W["w_branch"].astype(CDT)
    cws = jnp.concatenate([W["conv_a_w"], W["conv_d_w"]], axis=1).reshape(-1, 128)
    sp_all = jnp.concatenate([W[n] for n, _ in _SP_NAMES], axis=1)
    bsc_all = W["sgu_b"][..., None]

    (cwg,) = _exchange([cws], False, "gather_conv_w")
    cwf = cwg.reshape(NS, L, KA + KD, CS).transpose(1, 2, 0, 3).reshape(L, KA + KD, NS * CS)

    def gather_start(l, after):
        return _hgather_start([pf1[l], pf2[l], pma[l], pwo[l], pxa[l], wbs[l]], after, f"gather_start_{l}")

    def gather_rest(h, after, tag):
        mid = _hgather_forward(h, after, f"gather_forward_{tag}")
        return _hgather_wait(mid, mid["token"], f"gather_wait_{tag}")

    packs = [None] * L
    first = []
    for k, grp in enumerate(([pf1[0]], [pma[0], pwo[0], wbs[0]], [pxa[0]], [pf2[0]])):
        first.append(_hgather_start(grp, first[-1]["token"] if first else sp_all, f"gather_start_0{'abcd'[k]}"))
    saved = []
    xc = x0
    for l in range(L):
        if l == 0:
            (gf1,) = gather_rest(first[0], sp_all, "0a")
        else:
            gf1, gf2, gma, gwo, gxa, gwb = packs[l]
        sp = sp_all[l:l + 1]
        cwa, cwd = cwf[l, :KA], cwf[l, KA:]
        wp, ws, bsc = W["pool_w"][l], W["sgu_ws"][l], bsc_all[l]
        s = {"x0": xc}
        nxt = gather_start(l + 1, gf1) if 0 < l < L - 1 else None
        xc, s["hb1"], s["a1"], s["b1"], s["y1"] = _ffn_fwd(xc, sp, "ffn1_pre_g", "ffn1_post_g", gf1,
                                                            nxt["token"] if nxt else sp)
        s["x1"] = xc
        if l == 0:
            gma, gwo, gwb = gather_rest(first[1], xc, "0b")
            nxt = gather_start(1, gma) if L > 1 else None
        s["hbm"], s["z"], s["g"] = _mix_in(xc, sp, gma, nxt["token"] if nxt and l == 0 else sp)
        s["ma"] = _mixA_fwd(s["z"], cwa, sp)
        s["mb"] = _mixB_fwd(s["z"], wp, sp)
        s["mc"] = _mixC_fwd(s["z"], ws, bsc, sp)
        s["yd"] = _mixD_conv_fwd(s["z"], cwd, sp)
        xc, s["md"], s["yk"], s["mg"], s["mo"] = _merge_fwd(s["ma"], s["mb"], s["mc"], s["yd"], s["g"], gwb, gwo, xc, sp)
        s["x2"] = xc
        if l == 0:
            (gxa,) = gather_rest(first[2], xc, "0c")
        s["mn"], s["k"], s["v"] = _xa_kv(memf, sp, gxa)
        xc, s["hbx"], s["q"], s["o"], s["po"] = _xa_fwd(xc, s["k"], s["v"], sp, gxa)
        s["x3"] = xc
        if l == 0:
            (gf2,) = gather_rest(first[3], xc, "0d")
            packs[0] = (gf1, gf2, gma, gwo, gxa, gwb)
        mid = _hgather_forward(nxt, xc, f"gather_forward_{l + 1}") if nxt and l > 0 else None
        xc, s["hb2"], s["a2"], s["b2"], s["y2"] = _ffn_fwd(xc, sp, "ffn2_pre_g", "ffn2_post_g", gf2,
                                                            mid["token"] if mid else sp)
        saved.append(s)
        if nxt:
            mid = mid or _hgather_forward(nxt, xc, f"gather_forward_{l + 1}")
            packs[l + 1] = _hgather_wait(mid, xc, f"gather_wait_{l + 1}")

    dx, lpart = _loss_head(xc, loss_target.reshape(S, D))
    loss = lax.psum(lpart[0, 0], ("x", "y", "c"))

    rep = {n: [None] * L for n in _REP_NAMES}
    summed = [dict() for _ in range(L)]
    pending = None
    last = []
    for l in reversed(range(L)):
        gf1, gf2, gma, gwo, gxa, gwb = packs[l]
        sp = sp_all[l:l + 1]
        cwa, cwd = cwf[l, :KA], cwf[l, KA:]
        wp, ws, bsc = W["pool_w"][l], W["sgu_ws"][l], bsc_all[l]
        s = saved[l]

        dx, dyb, da, db, gp = _ffn_bwd_act(dx, s["x3"], s["y2"], s["a2"], s["b2"], sp, "ffn2_pre_g", "ffn2_post_g", gf2,
                                           pending[1]["token"] if pending else sp)
        rep["ffn2_pre_g"][l], rep["ffn2_post_g"][l] = gp[0], gp[1]
        d_f2 = _ffn_bwd_w(s["hb2"], dyb, s["a2"], s["b2"], da, db)
        if l == 0:
            last.append(_exchange_start([d_f2], (True,), dx, "scatter_start_0a"))

        dx, dpo, dq, dk, dv, gp = _xa_bwd_act(dx, s["x2"], s["po"], s["q"], s["k"], s["v"], sp, gxa,
                                              last[-1]["token"] if last else sp)
        rep["xa_pre_g"][l], rep["xa_post_g"][l] = gp[0], gp[1]
        d_xa = _xa_bwd_w(s["hbx"], dq, s["o"], dpo, s["mn"], dk, dv)
        rep["mem_g"][l] = _xa_kv_bwd(memf, dk, dv, sp, gxa)[0]
        if l == 0:
            last.append(_exchange_start([d_xa], (True,), dx, "scatter_start_0b"))

        dmo, dm, dgp, dyk, gp = _merge_bwd_act(dx, s["mo"], s["g"], s["yk"], gwb, gwo, sp, last[-1]["token"] if last else sp)
        rep["mix_post_g"][l] = gp[0]
        d_wb, d_wo = _merge_bwd_w(s["ma"], s["mb"], s["mc"], s["md"], dyk, s["mg"], dmo)
        dza, dcw, gp = _mixA_bwd(s["z"], dm[0], cwa, sp)
        rep["conv_a_w"][l], rep["conv_a_b"][l] = dcw, gp[0]
        dzb, dwp, gp = _mixB_bwd(s["z"], dm[1], wp, sp)
        rep["pool_w"][l], rep["pool_scale"][l] = dwp, gp[0]
        dzc, dws, dbs, gp = _mixC_bwd(s["z"], dm[2], ws, bsc, sp)
        rep["sgu_ws"][l], rep["sgu_b"][l], rep["sgu_ln_g"][l], rep["sgu_ln_b"][l] = dws, dbs[:, :, 0], gp[0], gp[1]
        dyd, gp = _mixD_ln_bwd(dm[3], s["yd"], sp)
        rep["conv_d_ln_g"][l], rep["conv_d_ln_b"][l] = gp[0], gp[1]
        dzd, dcw, gp = _mixD_conv_bwd(s["z"], dyd, cwd)
        rep["conv_d_w"][l], rep["conv_d_b"][l] = dcw, gp[0]
        dz = jnp.concatenate([dza, dzb, dzc, dzd], axis=0)
        dx, gp = _mix_in_bwd_act(dz, dgp, dx, s["x1"], sp, gma)
        rep["mix_pre_g"][l] = gp[0]
        d_ma, dbg = _mix_in_bwd_w(dz, dgp, s["hbm"])
        rep["b_gate"][l] = dbg[:, 0, :].reshape(-1)
        if l == 0:
            last.append(_exchange_start([d_ma, d_wo, d_wb], (True,) * 3, dx, "scatter_start_0c"))

        dx, dyb, da, db, gp = _ffn_bwd_act(dx, s["x0"], s["y1"], s["a1"], s["b1"], sp, "ffn1_pre_g", "ffn1_post_g", gf1,
                                           last[-1]["token"] if last else sp)
        rep["ffn1_pre_g"][l], rep["ffn1_post_g"][l] = gp[0], gp[1]
        d_f1 = _ffn_bwd_w(s["hb1"], dyb, s["a1"], s["b1"], da, db)

        if pending:
            r = _exchange_wait(pending[1], dx, f"scatter_wait_{pending[0]}")
            summed[pending[0]] = dict(zip(("f1", "f2", "ma", "wo", "xa", "wb", "flat"), r))
        flat = jnp.concatenate([rep[n][l].reshape(-1) for n in _REP_NAMES])
        flat = jnp.pad(flat, (0, -flat.size % 1024)).reshape(-1, 128)
        if l == 0:
            last.append(_exchange_start([d_f1, flat], (True, False), dx, "scatter_start_0d"))
        else:
            pending = (l, _exchange_start([d_f1, d_f2, d_ma, d_wo, d_xa, d_wb, flat], (True,) * 6 + (False,), dx,
                                          f"scatter_start_{l}"))

    sums = lambda r: _slot_sum(r.reshape(NS, -1, r.shape[-1]))
    for l in range(1, L):
        summed[l] = {k: sums(v) for k, v in summed[l].items()}
    (r,) = _exchange_wait(last[0], dx, "scatter_wait_0a")
    summed[0]["f2"] = sums(r)
    (r,) = _exchange_wait(last[1], dx, "scatter_wait_0b")
    summed[0]["xa"] = sums(r)
    r = _exchange_wait(last[2], dx, "scatter_wait_0c")
    summed[0]["ma"], summed[0]["wo"], summed[0]["wb"] = (sums(v) for v in r)
    r = _exchange_wait(last[3], [v for d in summed for v in d.values()], "scatter_wait_0d")
    summed[0]["f1"], summed[0]["flat"] = (sums(v) for v in r)

    G = {}
    stk = lambda k: jnp.stack([summed[l][k] for l in range(L)])
    f1, f2, ma_, wo_, xa_, wb_ = (stk(k) for k in ("f1", "f2", "ma", "wo", "xa", "wb"))
    for nm, f in (("ffn1", f1), ("ffn2", f2)):
        G[nm + "_w1"], G[nm + "_w3"], G[nm + "_w2"] = _t(f[:, :FS]), _t(f[:, FS:2 * FS]), f[:, 2 * FS:]
    G["w_in"], G["w_gate"] = _t(ma_[:, :MW]), _t(ma_[:, MW:])
    G["w_o"] = wo_
    G["xa_wq"], G["xa_wk"], G["xa_wv"], G["xa_wo"] = (xa_[:, i * GW:(i + 1) * GW] for i in range(4))
    G["w_branch"] = wb_.reshape(W["w_branch"].shape)

    tot = [summed[l]["flat"].reshape(-1) for l in range(L)]
    off = 0
    for n in _REP_NAMES:
        shape = (KA, NS * CS) if n == "conv_a_w" else (KD, NS * CS) if n == "conv_d_w" else W[n].shape[1:]
        size = 1
        for d in shape:
            size *= d
        G[n] = jnp.stack([tot[l][off:off + size].reshape(shape) for l in range(L)])
        off += size
    for n in ("conv_a_w", "conv_d_w"):
        G[n] = lax.dynamic_slice_in_dim(G[n], me * CS, CS, axis=2)

    deltas, new_m, new_v = {}, {}, {}
    for n in _W_NAMES:
        deltas[n], new_m[n], new_v[n] = _adamw(W[n], G[n], M[n], V[n])
    grad_x = dx.reshape(x.shape)
    return (loss, grad_x, *[G[n] for n in _W_NAMES], *[deltas[n] for n in _W_NAMES],
            *[new_m[n] for n in _W_NAMES], *[new_v[n] for n in _W_NAMES])


def kernel(x, mem, ffn1_pre_g, ffn1_post_g, ffn1_w1, ffn1_w3, ffn1_w2, mix_pre_g, mix_post_g, w_in, conv_a_w, conv_a_b, pool_w, pool_scale, sgu_ln_g, sgu_ln_b, sgu_ws, sgu_b, conv_d_w, conv_d_b, conv_d_ln_g, conv_d_ln_b, w_branch, w_gate, b_gate, w_o, xa_pre_g, xa_post_g, mem_g, xa_wq, xa_wk, xa_wv, xa_wo, ffn2_pre_g, ffn2_post_g, ffn2_w1, ffn2_w3, ffn2_w2, loss_target, m_ffn1_pre_g, m_ffn1_post_g, m_ffn1_w1, m_ffn1_w3, m_ffn1_w2, m_mix_pre_g, m_mix_post_g, m_w_in, m_conv_a_w, m_conv_a_b, m_pool_w, m_pool_scale, m_sgu_ln_g, m_sgu_ln_b, m_sgu_ws, m_sgu_b, m_conv_d_w, m_conv_d_b, m_conv_d_ln_g, m_conv_d_ln_b, m_w_branch, m_w_gate, m_b_gate, m_w_o, m_xa_pre_g, m_xa_post_g, m_mem_g, m_xa_wq, m_xa_wk, m_xa_wv, m_xa_wo, m_ffn2_pre_g, m_ffn2_post_g, m_ffn2_w1, m_ffn2_w3, m_ffn2_w2, v_ffn1_pre_g, v_ffn1_post_g, v_ffn1_w1, v_ffn1_w3, v_ffn1_w2, v_mix_pre_g, v_mix_post_g, v_w_in, v_conv_a_w, v_conv_a_b, v_pool_w, v_pool_scale, v_sgu_ln_g, v_sgu_ln_b, v_sgu_ws, v_sgu_b, v_conv_d_w, v_conv_d_b, v_conv_d_ln_g, v_conv_d_ln_b, v_w_branch, v_w_gate, v_b_gate, v_w_o, v_xa_pre_g, v_xa_post_g, v_mem_g, v_xa_wq, v_xa_wk, v_xa_wv, v_xa_wo, v_ffn2_pre_g, v_ffn2_post_g, v_ffn2_w1, v_ffn2_w3, v_ffn2_w2):
    args = dict(locals())
    W = {n: args[n] for n in _W_NAMES}
    M = {n: args["m_" + n] for n in _W_NAMES}
    V = {n: args["v_" + n] for n in _W_NAMES}
    return _step(x, mem, loss_target, W, M, V)
```

```python
import jax
import jax.numpy as jnp
from jax import lax
from jax.experimental import pallas as pl
from jax.experimental.pallas import tpu as pltpu

F32 = jnp.float32
CDT = jnp.bfloat16
EPS = 1e-6
NS = 8
GW = 128
MW = 512
CHUNK = 64
XA_HEADS = 4
POOL_WINDOWS = (2, 4, 8, 16)
VMEM_LIMIT = 56 * 1024 * 1024
ADAM_LR, ADAM_B1, ADAM_B2, ADAM_EPS, ADAM_WD, ADAM_STEP = 0.001, 0.9, 0.999, 1e-08, 0.01, 10

SDS = jax.ShapeDtypeStruct

_SP_NAMES = (("ffn1_pre_g", 1024), ("ffn1_post_g", 1024), ("mix_pre_g", 1024), ("mix_post_g", 1024),
             ("xa_pre_g", 1024), ("xa_post_g", 1024), ("mem_g", 1024), ("ffn2_pre_g", 1024), ("ffn2_post_g", 1024),
             ("conv_a_b", 512), ("pool_scale", 512), ("sgu_ln_g", 512), ("sgu_ln_b", 512), ("conv_d_b", 512),
             ("conv_d_ln_g", 512), ("conv_d_ln_b", 512), ("b_gate", 4096))
_SP = {}
_off = 0
for _n, _w in _SP_NAMES:
    _SP[_n] = (_off, _w)
    _off += _w
_SP_TOTAL = _off


def _call(body, name, grid, in_specs, out_specs, out_shape, scratch=()):
    return pl.pallas_call(
        body, name=name, grid=grid, in_specs=in_specs, out_specs=out_specs, out_shape=out_shape,
        scratch_shapes=list(scratch),
        compiler_params=pltpu.CompilerParams(dimension_semantics=("arbitrary",) * len(grid),
                                             vmem_limit_bytes=VMEM_LIMIT))


def _nn(a, b):
    return lax.dot_general(a, b, (((1,), (0,)), ((), ())), preferred_element_type=F32)


def _nt(a, b):
    return lax.dot_general(a, b, (((1,), (1,)), ((), ())), preferred_element_type=F32)


def _tn(a, b):
    return lax.dot_general(a, b, (((0,), (0,)), ((), ())), preferred_element_type=F32)


def _rms(x):
    r = lax.rsqrt(jnp.mean(x * x, axis=-1, keepdims=True) + EPS)
    return x * r, r


def _rms_bwd(n, r, g, dout):
    dn = dout * g
    dx = r * (dn - n * jnp.mean(dn * n, axis=-1, keepdims=True))
    return dx, jnp.sum(dout * n, axis=0, keepdims=True)


def _ln(y):
    mu = jnp.mean(y, axis=-1, keepdims=True)
    yc = y - mu
    rs = lax.rsqrt(jnp.mean(yc * yc, axis=-1, keepdims=True) + EPS)
    return yc * rs, rs


def _ln_bwd(xh, rs, dxh):
    return rs * (dxh - jnp.mean(dxh, axis=-1, keepdims=True) - xh * jnp.mean(dxh * xh, axis=-1, keepdims=True))


def _silu_parts(a):
    s = jax.nn.sigmoid(a)
    sl = a * s
    return sl, s + sl * (1.0 - s)


_GELU_C = 0.7978845608028654
_GELU_A = 0.044715


def _gelu(x):
    return 0.5 * x * (1.0 + jnp.tanh(_GELU_C * (x + _GELU_A * x * x * x)))


def _gelu_parts(x):
    t = jnp.tanh(_GELU_C * (x + _GELU_A * x * x * x))
    g = 0.5 * x * (1.0 + t)
    dg = 0.5 * (1.0 + t) + 0.5 * x * (1.0 - t * t) * _GELU_C * (1.0 + 3.0 * _GELU_A * x * x)
    return g, dg


def _spspec(name, width, imap):
    off = _SP[name][0]
    assert off % width == 0
    return pl.BlockSpec((1, width), lambda *a: (0, off // width + imap(*a)))


def _zero(*a):
    return 0


def _row_once(tm, d):
    return pl.BlockSpec((tm, d), lambda i, j: (i, 0), pipeline_mode=pl.Buffered(1))


FFN_SG = 2


def _ffn_fwd(x, sp, pre, post, pf, dep):
    S, D = x.shape
    FS = pf.shape[1] // 3
    TM = min(1024, S)
    SG, NG, W = FFN_SG, NS // FFN_SG, FFN_SG * FS

    def body(x_ref, pg_ref, qg_ref, w1_ref, w3_ref, w2_ref, dep_ref, xo_ref, hb_ref, a_ref, b_ref, y_ref, hb_s, acc):
        j = pl.program_id(1)

        @pl.when(j == 0)
        def _():
            n, _ = _rms(x_ref[...])
            hb = (n * pg_ref[...]).astype(CDT)
            hb_s[...] = hb
            hb_ref[...] = hb
            acc[...] = jnp.zeros_like(acc)

        hb = hb_s[...]
        a = _nt(hb, w1_ref[...].reshape(W, D))
        b = _nt(hb, w3_ref[...].reshape(W, D))
        a_ref[...] = a.astype(CDT)
        b_ref[...] = b.astype(CDT)
        u = (a * jax.nn.sigmoid(a) * b).astype(CDT)
        acc[...] += _nn(u, w2_ref[...].reshape(W, D))

        @pl.when(j == NG - 1)
        def _():
            y = acc[...]
            y_ref[...] = y.astype(CDT)
            n, _ = _rms(y)
            xo_ref[...] = x_ref[...] + 0.5 * (n * qg_ref[...])

    row1 = _row_once(TM, D)
    grp = lambda i, j: (j, i, 0)
    return _call(
        body, "ffn_fwd", (S // TM, NG),
        [row1, _spspec(pre, D, _zero), _spspec(post, D, _zero),
         pl.BlockSpec((SG, FS, D), lambda i, j: (j, 0, 0)), pl.BlockSpec((SG, FS, D), lambda i, j: (j, 1, 0)),
         pl.BlockSpec((SG, FS, D), lambda i, j: (j, 2, 0)), pl.BlockSpec(memory_space=pl.ANY)],
        [row1, row1, pl.BlockSpec((None, TM, W), grp), pl.BlockSpec((None, TM, W), grp), row1],
        [SDS((S, D), F32), SDS((S, D), CDT), SDS((NG, S, W), CDT), SDS((NG, S, W), CDT), SDS((S, D), CDT)],
        [pltpu.VMEM((TM, D), CDT), pltpu.VMEM((TM, D), F32)])(x, sp, sp, pf, pf, pf, dep)


def _ffn_bwd_act(dxo, x, y, a, b, sp, pre, post, pf, dep):
    S, D = x.shape
    FS = pf.shape[1] // 3
    TM = min(512, S)
    SG, NG, W = FFN_SG, NS // FFN_SG, FFN_SG * FS

    def body(dxo_ref, x_ref, y_ref, a_ref, b_ref, pg_ref, qg_ref, w1_ref, w3_ref, w2_ref, dep_ref,
             dx_ref, dyb_ref, da_ref, db_ref, gp_ref, dyb_s, acc):
        i = pl.program_id(0)
        j = pl.program_id(1)

        @pl.when((i == 0) & (j == 0))
        def _():
            gp_ref[...] = jnp.zeros_like(gp_ref)

        @pl.when(j == 0)
        def _():
            n, r = _rms(y_ref[...].astype(F32))
            dy, dg = _rms_bwd(n, r, qg_ref[...], 0.5 * dxo_ref[...])
            dyb = dy.astype(CDT)
            dyb_s[...] = dyb
            dyb_ref[...] = dyb
            gp_ref[1:2, :] += dg
            acc[...] = jnp.zeros_like(acc)

        sl, dsl = _silu_parts(a_ref[...].astype(F32))
        du = _nt(dyb_s[...], w2_ref[...].reshape(W, D))
        db = (du * sl).astype(CDT)
        da = (du * b_ref[...].astype(F32) * dsl).astype(CDT)
        da_ref[...] = da
        db_ref[...] = db
        acc[...] += _nn(da, w1_ref[...].reshape(W, D)) + _nn(db, w3_ref[...].reshape(W, D))

        @pl.when(j == NG - 1)
        def _():
            n, r = _rms(x_ref[...])
            dx, dg = _rms_bwd(n, r, pg_ref[...], acc[...])
            dx_ref[...] = dxo_ref[...] + dx
            gp_ref[0:1, :] += dg

    row = lambda i, j: (i, 0)
    grp = lambda i, j: (j, i, 0)
    return _call(
        body, "ffn_bwd_act", (S // TM, NG),
        [pl.BlockSpec((TM, D), row), pl.BlockSpec((TM, D), row), pl.BlockSpec((TM, D), row),
         pl.BlockSpec((None, TM, W), grp), pl.BlockSpec((None, TM, W), grp),
         _spspec(pre, D, _zero), _spspec(post, D, _zero),
         pl.BlockSpec((SG, FS, D), lambda i, j: (j, 0, 0)), pl.BlockSpec((SG, FS, D), lambda i, j: (j, 1, 0)),
         pl.BlockSpec((SG, FS, D), lambda i, j: (j, 2, 0)), pl.BlockSpec(memory_space=pl.ANY)],
        [pl.BlockSpec((TM, D), row), pl.BlockSpec((TM, D), row), pl.BlockSpec((None, TM, W), grp),
         pl.BlockSpec((None, TM, W), grp), pl.BlockSpec((8, D), lambda i, j: (0, 0))],
        [SDS((S, D), F32), SDS((S, D), CDT), SDS((NG, S, W), CDT), SDS((NG, S, W), CDT), SDS((8, D), F32)],
        [pltpu.VMEM((TM, D), CDT), pltpu.VMEM((TM, D), F32)])(dxo, x, y, a, b, sp, sp, pf, pf, pf, dep)


def _ffn_bwd_w(hb, dyb, a, b, da, db):
    S, D = hb.shape
    SG, NG = FFN_SG, NS // FFN_SG
    W = a.shape[2]
    FS = W // SG
    TK = min(512, S)
    NK = S // TK

    def body(hb_ref, dyb_ref, a_ref, b_ref, da_ref, db_ref, g_ref, acc):
        k = pl.program_id(1)

        @pl.when(k == 0)
        def _():
            acc[...] = jnp.zeros_like(acc)

        af = a_ref[...].astype(F32)
        u = (af * jax.nn.sigmoid(af) * b_ref[...].astype(F32)).astype(CDT)
        hb = hb_ref[...]
        acc[0:W, :] += _tn(da_ref[...], hb)
        acc[W:2 * W, :] += _tn(db_ref[...], hb)
        acc[2 * W:3 * W, :] += _tn(u, dyb_ref[...])

        @pl.when(k == NK - 1)
        def _():
            for s in range(SG):
                for r in range(3):
                    g_ref[s, r * FS:(r + 1) * FS, :] = acc[r * W + s * FS:r * W + (s + 1) * FS, :].astype(CDT)

    row = lambda j, k: (k, 0)
    grp = lambda j, k: (j, k, 0)
    return _call(
        body, "ffn_bwd_w", (NG, NK),
        [pl.BlockSpec((TK, D), row), pl.BlockSpec((TK, D), row)] + [pl.BlockSpec((None, TK, W), grp)] * 4,
        pl.BlockSpec((SG, 3 * FS, D), lambda j, k: (j, 0, 0)),
        SDS((NS, 3 * FS, D), CDT),
        [pltpu.VMEM((3 * W, D), F32)])(hb, dyb, a, b, da, db)


def _mix_in(x, sp, pma, dep):
    S, D = x.shape
    TM = min(1024, S)

    def body(x_ref, pg_ref, bg_ref, wi_ref, wg_ref, dep_ref, hb_ref, z_ref, g_ref, hb_s):
        @pl.when(pl.program_id(1) == 0)
        def _():
            n, _ = _rms(x_ref[...])
            hb = (n * pg_ref[...]).astype(CDT)
            hb_s[...] = hb
            hb_ref[...] = hb

        hb = hb_s[...]
        z_ref[...] = _nt(hb, wi_ref[...]).astype(CDT)
        g_ref[...] = jax.nn.sigmoid(_nt(hb, wg_ref[...]) + bg_ref[...]).astype(CDT)

    return _call(
        body, "mix_in", (S // TM, NS),
        [_row_once(TM, D), _spspec("mix_pre_g", D, _zero), _spspec("b_gate", MW, lambda i, j: j),
         pl.BlockSpec((None, MW, D), lambda i, j: (j, 0, 0)), pl.BlockSpec((None, MW, D), lambda i, j: (j, 1, 0)), _ANY],
        [_row_once(TM, D), pl.BlockSpec((None, TM, MW), lambda i, j: (j, i, 0)),
         pl.BlockSpec((None, TM, MW), lambda i, j: (j // 2, i, j % 2))],
        [SDS((S, D), CDT), SDS((NS, S, MW), CDT), SDS((4, S, D), CDT)],
        [pltpu.VMEM((TM, D), CDT)])(x, sp, sp, pma, pma, dep)


def _mix_in_bwd_act(dz, dgp, dxr, x, sp, pma):
    S, D = x.shape
    TM = min(1024, S)

    def body(dz_ref, dg_ref, dxr_ref, x_ref, pg_ref, wi_ref, wg_ref, dx_ref, gp_ref, acc):
        i = pl.program_id(0)
        j = pl.program_id(1)

        @pl.when((i == 0) & (j == 0))
        def _():
            gp_ref[...] = jnp.zeros_like(gp_ref)

        @pl.when(j == 0)
        def _():
            acc[...] = jnp.zeros_like(acc)

        acc[...] += _nn(dz_ref[...], wi_ref[...]) + _nn(dg_ref[...], wg_ref[...])

        @pl.when(j == NS - 1)
        def _():
            n, r = _rms(x_ref[...])
            dx, dg = _rms_bwd(n, r, pg_ref[...], acc[...])
            dx_ref[...] = dxr_ref[...] + dx
            gp_ref[0:1, :] += dg

    return _call(
        body, "mix_in_bwd_act", (S // TM, NS),
        [pl.BlockSpec((None, TM, MW), lambda i, j: (j, i, 0)), pl.BlockSpec((None, TM, MW), lambda i, j: (j // 2, i, j % 2)),
         _row_once(TM, D), _row_once(TM, D), _spspec("mix_pre_g", D, _zero),
         pl.BlockSpec((None, MW, D), lambda i, j: (j, 0, 0)), pl.BlockSpec((None, MW, D), lambda i, j: (j, 1, 0))],
        [_row_once(TM, D), pl.BlockSpec((8, D), lambda i, j: (0, 0))],
        [SDS((S, D), F32), SDS((8, D), F32)],
        [pltpu.VMEM((TM, D), F32)])(dz, dgp, dxr, x, sp, pma, pma)


def _mix_in_bwd_w(dz, dgp, hb):
    S, D = hb.shape
    TK = min(512, S)
    NK = S // TK

    def body(dz_ref, dg_ref, hb_ref, g_ref, bg_ref, acc):
        k = pl.program_id(1)

        @pl.when(k == 0)
        def _():
            acc[...] = jnp.zeros_like(acc)
            bg_ref[...] = jnp.zeros_like(bg_ref)

        hb = hb_ref[...]
        dg = dg_ref[...]
        acc[0:MW, :] += _tn(dz_ref[...], hb)
        acc[MW:2 * MW, :] += _tn(dg, hb)
        bg_ref[0:1, :] += jnp.sum(dg.astype(F32), axis=0, keepdims=True)

        @pl.when(k == NK - 1)
        def _():
            g_ref[...] = acc[...].astype(CDT)

    return _call(
        body, "mix_in_bwd_w", (NS, NK),
        [pl.BlockSpec((None, TK, MW), lambda j, k: (j, k, 0)), pl.BlockSpec((None, TK, MW), lambda j, k: (j // 2, k, j % 2)),
         pl.BlockSpec((TK, D), lambda j, k: (k, 0))],
        [pl.BlockSpec((None, 2 * MW, D), lambda j, k: (j, 0, 0)), pl.BlockSpec((None, 8, MW), lambda j, k: (j, 0, 0))],
        [SDS((NS, 2 * MW, D), CDT), SDS((NS, 8, MW), F32)],
        [pltpu.VMEM((2 * MW, D), F32)])(dz, dgp, hb)


def _causal_taps(pad_ref, i, ch, halo, k_taps, lanes=slice(None)):
    val = pad_ref[pl.ds(pl.multiple_of(i * ch, 8), ch + halo), lanes]
    out = []
    for k in range(k_taps):
        s = k_taps - 1 - k
        out.append((k, (pltpu.roll(val, s, 0) if s else val)[halo:, :]))
    return out


def _anti_taps(pad_ref, i, ch, halo, k_taps, lanes=slice(None)):
    val = pad_ref[pl.ds(pl.multiple_of(i * ch, 8), ch + halo), lanes]
    n = ch + halo
    out = []
    for k in range(k_taps):
        s = k_taps - 1 - k
        out.append((k, (pltpu.roll(val, n - s, 0) if s else val)[:ch, :]))
    return out


def _conv_geometry(S, k_taps):
    halo = 8 * ((k_taps - 1 + 7) // 8)
    ch = min(256, S)
    return halo, ch, S // ch


def _rows(i, ch):
    return pl.ds(pl.multiple_of(i * ch, ch), ch)


def _mixA_fwd(z, cw, sp):
    S = z.shape[1]
    K = cw.shape[0]
    H, CH, NCH = _conv_geometry(S, K)

    def body(z_ref, w_ref, b_ref, o_ref, pad):
        pad[0:H, :] = jnp.zeros((H, GW), F32)

        def fill(i, c):
            r = _rows(i, CH)
            pad[pl.ds(pl.multiple_of(i * CH + H, 8), CH), :] = z_ref[2, r, :].astype(F32) * z_ref[0, r, :].astype(F32)
            return c

        lax.fori_loop(0, NCH, fill, 0)

        def conv(i, c):
            r = _rows(i, CH)
            acc = jnp.zeros((CH, GW), F32)
            for k, sh in _causal_taps(pad, i, CH, H, K):
                acc = acc + w_ref[k:k + 1, :] * sh
            o_ref[r, :] = (z_ref[1, r, :].astype(F32) * (acc + b_ref[...])).astype(CDT)
            return c

        lax.fori_loop(0, NCH, conv, 0)

    return _call(
        body, "mixA_fwd", (MW // GW,),
        [pl.BlockSpec((3, S, GW), lambda c: (0, 0, c)), pl.BlockSpec((K, GW), lambda c: (0, c)),
         _spspec("conv_a_b", GW, lambda c: c)],
        pl.BlockSpec((S, GW), lambda c: (0, c)), SDS((S, MW), CDT),
        [pltpu.VMEM((H + S, GW), F32)])(z, cw, sp)


def _mixA_bwd(z, dm, cw, sp):
    S = z.shape[1]
    K = cw.shape[0]
    H, CH, NCH = _conv_geometry(S, K)

    def body(z_ref, dm_ref, w_ref, b_ref, dz_ref, dw_ref, db_ref, pad, dpad, dw_s):
        pad[0:H, :] = jnp.zeros((H, GW), F32)
        dpad[pl.ds(S, H), :] = jnp.zeros((H, GW), F32)
        dw_s[...] = jnp.zeros_like(dw_s)
        db_ref[...] = jnp.zeros_like(db_ref)

        def fill(i, c):
            r = _rows(i, CH)
            pad[pl.ds(pl.multiple_of(i * CH + H, 8), CH), :] = z_ref[2, r, :].astype(F32) * z_ref[0, r, :].astype(F32)
            return c

        lax.fori_loop(0, NCH, fill, 0)

        def p1(i, c):
            r = _rows(i, CH)
            taps = _causal_taps(pad, i, CH, H, K)
            acc = jnp.zeros((CH, GW), F32)
            for k, sh in taps:
                acc = acc + w_ref[k:k + 1, :] * sh
            dmf = dm_ref[r, :].astype(F32)
            dz_ref[1, r, :] = (dmf * (acc + b_ref[...])).astype(CDT)
            dc = dmf * z_ref[1, r, :].astype(F32)
            dpad[r, :] = dc
            for k, sh in taps:
                dw_s[k:k + 1, :] += jnp.sum(dc * sh, axis=0, keepdims=True)
            db_ref[0:1, :] += jnp.sum(dc, axis=0, keepdims=True)
            return c

        lax.fori_loop(0, NCH, p1, 0)

        def p2(i, c):
            r = _rows(i, CH)
            dq = jnp.zeros((CH, GW), F32)
            for k, sh in _anti_taps(dpad, i, CH, H, K):
                dq = dq + w_ref[k:k + 1, :] * sh
            dz_ref[0, r, :] = (dq * z_ref[2, r, :].astype(F32)).astype(CDT)
            dz_ref[2, r, :] = (dq * z_ref[0, r, :].astype(F32)).astype(CDT)
            return c

        lax.fori_loop(0, NCH, p2, 0)
        dw_ref[...] = dw_s[0:K, :]

    return _call(
        body, "mixA_bwd", (MW // GW,),
        [pl.BlockSpec((3, S, GW), lambda c: (0, 0, c)), pl.BlockSpec((None, S, GW), lambda c: (0, 0, c)),
         pl.BlockSpec((K, GW), lambda c: (0, c)), _spspec("conv_a_b", GW, lambda c: c)],
        [pl.BlockSpec((3, S, GW), lambda c: (0, 0, c)), pl.BlockSpec((K, GW), lambda c: (0, c)),
         pl.BlockSpec((8, GW), lambda c: (0, c))],
        [SDS((3, S, MW), CDT), SDS((K, MW), F32), SDS((8, MW), F32)],
        [pltpu.VMEM((H + S, GW), F32), pltpu.VMEM((S + H, GW), F32), pltpu.VMEM((8 * ((K + 7) // 8), GW), F32)])(z, dm, cw, sp)


def _mixD_conv_fwd(z, cw, sp):
    S = z.shape[1]
    K = cw.shape[0]
    H, CH, NCH = _conv_geometry(S, K)

    def body(z_ref, w_ref, b_ref, o_ref, pad):
        pad[0:H, :] = jnp.zeros((H, GW), F32)

        def fill(i, c):
            r = _rows(i, CH)
            pad[pl.ds(pl.multiple_of(i * CH + H, 8), CH), :] = (
                z_ref[0, r, :].astype(F32) * jax.nn.sigmoid(z_ref[1, r, :].astype(F32)))
            return c

        lax.fori_loop(0, NCH, fill, 0)

        def conv(i, c):
            acc = jnp.zeros((CH, GW), F32)
            for k, sh in _causal_taps(pad, i, CH, H, K):
                acc = acc + w_ref[k:k + 1, :] * sh
            o_ref[_rows(i, CH), :] = (acc + b_ref[...]).astype(CDT)
            return c

        lax.fori_loop(0, NCH, conv, 0)

    return _call(
        body, "mixD_conv_fwd", (MW // GW,),
        [pl.BlockSpec((2, S, GW), lambda c: (3, 0, c)), pl.BlockSpec((K, GW), lambda c: (0, c)),
         _spspec("conv_d_b", GW, lambda c: c)],
        pl.BlockSpec((S, GW), lambda c: (0, c)), SDS((S, MW), CDT),
        [pltpu.VMEM((H + S, GW), F32)])(z, cw, sp)


def _mixD_conv_bwd(z, dy, cw):
    S = z.shape[1]
    K = cw.shape[0]
    H, CH, NCH = _conv_geometry(S, K)

    def body(z_ref, dy_ref, w_ref, dz_ref, dw_ref, db_ref, pad, dpad, dw_s):
        pad[0:H, :] = jnp.zeros((H, GW), F32)
        dpad[pl.ds(S, H), :] = jnp.zeros((H, GW), F32)
        dw_s[...] = jnp.zeros_like(dw_s)
        db_ref[...] = jnp.zeros_like(db_ref)

        def fill(i, c):
            r = _rows(i, CH)
            pad[pl.ds(pl.multiple_of(i * CH + H, 8), CH), :] = (
                z_ref[0, r, :].astype(F32) * jax.nn.sigmoid(z_ref[1, r, :].astype(F32)))
            dpad[r, :] = dy_ref[r, :].astype(F32)
            return c

        lax.fori_loop(0, NCH, fill, 0)

        def p1(i, c):
            dyf = dy_ref[_rows(i, CH), :].astype(F32)
            for k, sh in _causal_taps(pad, i, CH, H, K):
                dw_s[k:k + 1, :] += jnp.sum(dyf * sh, axis=0, keepdims=True)
            db_ref[0:1, :] += jnp.sum(dyf, axis=0, keepdims=True)
            return c

        lax.fori_loop(0, NCH, p1, 0)

        def p2(i, c):
            r = _rows(i, CH)
            dq = jnp.zeros((CH, GW), F32)
            for k, sh in _anti_taps(dpad, i, CH, H, K):
                dq = dq + w_ref[k:k + 1, :] * sh
            a = z_ref[0, r, :].astype(F32)
            sg = jax.nn.sigmoid(z_ref[1, r, :].astype(F32))
            dz_ref[0, r, :] = (dq * sg).astype(CDT)
            dz_ref[1, r, :] = (dq * a * sg * (1.0 - sg)).astype(CDT)
            return c

        lax.fori_loop(0, NCH, p2, 0)
        dw_ref[...] = dw_s[0:K, :]

    return _call(
        body, "mixD_conv_bwd", (MW // GW,),
        [pl.BlockSpec((2, S, GW), lambda c: (3, 0, c)), pl.BlockSpec((S, GW), lambda c: (0, c)),
         pl.BlockSpec((K, GW), lambda c: (0, c))],
        [pl.BlockSpec((2, S, GW), lambda c: (0, 0, c)), pl.BlockSpec((K, GW), lambda c: (0, c)),
         pl.BlockSpec((8, GW), lambda c: (0, c))],
        [SDS((2, S, MW), CDT), SDS((K, MW), F32), SDS((8, MW), F32)],
        [pltpu.VMEM((H + S, GW), F32), pltpu.VMEM((S + H, GW), F32), pltpu.VMEM((8 * ((K + 7) // 8), GW), F32)])(z, dy, cw)


def _mixD_ln_bwd(dm, yd, sp):
    S = yd.shape[0]
    TM = min(512, S)

    def body(dm_ref, y_ref, lg_ref, lb_ref, dy_ref, gp_ref):
        @pl.when(pl.program_id(0) == 0)
        def _():
            gp_ref[...] = jnp.zeros_like(gp_ref)

        xh, rs = _ln(y_ref[...].astype(F32))
        _, dsl = _silu_parts(xh * lg_ref[...] + lb_ref[...])
        dl = dm_ref[...].astype(F32) * dsl
        gp_ref[0:1, :] += jnp.sum(dl * xh, axis=0, keepdims=True)
        gp_ref[1:2, :] += jnp.sum(dl, axis=0, keepdims=True)
        dy_ref[...] = _ln_bwd(xh, rs, dl * lg_ref[...]).astype(CDT)

    row = lambda i: (i, 0)
    return _call(
        body, "mixD_ln_bwd", (S // TM,),
        [pl.BlockSpec((None, TM, MW), lambda i: (3, i, 0)), pl.BlockSpec((TM, MW), row), _spspec("conv_d_ln_g", MW, _zero),
         _spspec("conv_d_ln_b", MW, _zero)],
        [pl.BlockSpec((TM, MW), row), pl.BlockSpec((8, MW), lambda i: (0, 0))],
        [SDS((S, MW), CDT), SDS((8, MW), F32)])(dm, yd, sp, sp)


def _box_causal(val, g):
    s = val
    for d in range(g + 1):
        s = s + pltpu.roll(s, 1 << d, 0)
    return s


def _box_anti(val, g):
    n = val.shape[0]
    s = val
    for d in range(g + 1):
        s = s + pltpu.roll(s, n - (1 << d), 0)
    return s


def _pool_count(i, ch, win):
    t = lax.broadcasted_iota(jnp.int32, (ch, GW), 0) + (i * ch + 1)
    return jnp.minimum(t, win).astype(F32)


def _mixB_fwd(z, wp, sp):
    S = z.shape[1]
    H, CH = 16, min(256, S)
    NCH = S // CH
    assert POOL_WINDOWS == tuple(2 << g for g in range(4))

    def body(p_ref, wp_ref, sc_ref, o_ref, pad):
        pad[0:H, :] = jnp.zeros((H, MW), F32)

        def fill(i, c):
            pad[pl.ds(pl.multiple_of(i * CH + H, 8), CH), :] = p_ref[_rows(i, CH), :].astype(F32)
            return c

        lax.fori_loop(0, NCH, fill, 0)

        def step(i, c):
            r = _rows(i, CH)
            for g in range(4):
                gs = slice(g * GW, (g + 1) * GW)
                val = pad[pl.ds(pl.multiple_of(i * CH, 8), CH + H), gs]
                pooled = _box_causal(val, g)[H:, :] / _pool_count(i, CH, POOL_WINDOWS[g]) - val[H:, :]
                mixed = _nn(pooled.astype(CDT), wp_ref[g].astype(CDT))
                o_ref[r, gs] = (mixed * sc_ref[:, gs]).astype(CDT)
            return c

        lax.fori_loop(0, NCH, step, 0)

    return _call(
        body, "mixB_fwd", (1,),
        [pl.BlockSpec((None, S, MW), lambda i: (3, 0, 0)), pl.BlockSpec((4, GW, GW), lambda i: (0, 0, 0)),
         _spspec("pool_scale", MW, _zero)],
        pl.BlockSpec((S, MW), lambda i: (0, 0)), SDS((S, MW), CDT),
        [pltpu.VMEM((H + S, MW), F32)])(z, wp, sp)


def _mixB_bwd(z, dm, wp, sp):
    S = z.shape[1]
    H, CH = 16, min(256, S)
    NCH = S // CH

    def body(p_ref, dm_ref, wp_ref, sc_ref, dz_ref, dwp_ref, dsc_ref, pad, rpad):
        pad[0:H, :] = jnp.zeros((H, MW), F32)
        rpad[pl.ds(S, H), :] = jnp.zeros((H, MW), F32)
        dwp_ref[...] = jnp.zeros_like(dwp_ref)
        dsc_ref[...] = jnp.zeros_like(dsc_ref)

        def fill(i, c):
            pad[pl.ds(pl.multiple_of(i * CH + H, 8), CH), :] = p_ref[_rows(i, CH), :].astype(F32)
            return c

        lax.fori_loop(0, NCH, fill, 0)

        def p1(i, c):
            r = _rows(i, CH)
            for g in range(4):
                gs = slice(g * GW, (g + 1) * GW)
                cnt = _pool_count(i, CH, POOL_WINDOWS[g])
                val = pad[pl.ds(pl.multiple_of(i * CH, 8), CH + H), gs]
                pooled = (_box_causal(val, g)[H:, :] / cnt - val[H:, :]).astype(CDT)
                w = wp_ref[g].astype(CDT)
                mixed = _nn(pooled, w)
                dmf = dm_ref[r, gs].astype(F32)
                dsc_ref[0:1, gs] += jnp.sum(dmf * mixed, axis=0, keepdims=True)
                dmx = (dmf * sc_ref[:, gs]).astype(CDT)
                dwp_ref[g] += _tn(pooled, dmx)
                rpad[r, gs] = _nt(dmx, w) / cnt
            return c

        lax.fori_loop(0, NCH, p1, 0)

        def p2(i, c):
            r = _rows(i, CH)
            for g in range(4):
                gs = slice(g * GW, (g + 1) * GW)
                val = rpad[pl.ds(pl.multiple_of(i * CH, 8), CH + H), gs]
                dp = _box_anti(val, g)[:CH, :] - val[:CH, :] * _pool_count(i, CH, POOL_WINDOWS[g])
                dz_ref[r, gs] = dp.astype(CDT)
            return c

        lax.fori_loop(0, NCH, p2, 0)

    return _call(
        body, "mixB_bwd", (1,),
        [pl.BlockSpec((None, S, MW), lambda i: (3, 0, 0)), pl.BlockSpec((None, S, MW), lambda i: (1, 0, 0)),
         pl.BlockSpec((4, GW, GW), lambda i: (0, 0, 0)), _spspec("pool_scale", MW, _zero)],
        [pl.BlockSpec((None, S, MW), lambda i: (0, 0, 0)), pl.BlockSpec((4, GW, GW), lambda i: (0, 0, 0)),
         pl.BlockSpec((8, MW), lambda i: (0, 0))],
        [SDS((1, S, MW), CDT), SDS((4, GW, GW), F32), SDS((8, MW), F32)],
        [pltpu.VMEM((H + S, MW), F32), pltpu.VMEM((S + H, MW), F32)])(z, dm, wp, sp)


def _sgu_mask():
    ci = lax.broadcasted_iota(jnp.int32, (GW, GW), 0) // CHUNK
    cj = lax.broadcasted_iota(jnp.int32, (GW, GW), 1) // CHUNK
    return cj <= ci


def _mixC_fwd(z, ws, bsc, sp):
    S = z.shape[1]
    RB = min(512, S)

    def body(z_ref, lg_ref, lb_ref, ws_ref, bs_ref, o_ref):
        mask = _sgu_mask()
        gu = _gelu(z_ref[0].astype(F32))
        xh, _ = _ln(_gelu(z_ref[1].astype(F32)))
        vn = (xh * lg_ref[...] + lb_ref[...]).astype(CDT)
        for g in range(4):
            gs = slice(g * GW, (g + 1) * GW)
            wm = jnp.where(mask, ws_ref[g], 0.0).astype(CDT)
            for nb in range(RB // GW):
                rs = slice(nb * GW, (nb + 1) * GW)
                mixed = _nn(wm, vn[rs, gs]) + bs_ref[g]
                o_ref[rs, gs] = (gu[rs, gs] * mixed).astype(CDT)

    return _call(
        body, "mixC_fwd", (S // RB,),
        [pl.BlockSpec((2, RB, MW), lambda i: (2, i, 0)), _spspec("sgu_ln_g", MW, _zero), _spspec("sgu_ln_b", MW, _zero),
         pl.BlockSpec((4, GW, GW), lambda i: (0, 0, 0)), pl.BlockSpec((4, GW, 1), lambda i: (0, 0, 0))],
        pl.BlockSpec((RB, MW), lambda i: (i, 0)), SDS((S, MW), CDT))(z, sp, sp, ws, bsc)


def _mixC_bwd(z, dm, ws, bsc, sp):
    S = z.shape[1]
    RB = min(512, S)
    NR = S // RB

    def body(z_ref, dm_ref, lg_ref, lb_ref, ws_ref, bs_ref, dz_ref, dws_ref, dbs_ref, gp_ref, dvn_s):
        i = pl.program_id(0)

        @pl.when(i == 0)
        def _():
            dws_ref[...] = jnp.zeros_like(dws_ref)
            dbs_ref[...] = jnp.zeros_like(dbs_ref)
            gp_ref[...] = jnp.zeros_like(gp_ref)

        mask = _sgu_mask()
        gu, dgu = _gelu_parts(z_ref[0].astype(F32))
        gv, dgv = _gelu_parts(z_ref[1].astype(F32))
        xh, rs_ = _ln(gv)
        vn = (xh * lg_ref[...] + lb_ref[...]).astype(CDT)
        dmf = dm_ref[...].astype(F32)
        for g in range(4):
            gs = slice(g * GW, (g + 1) * GW)
            wm = jnp.where(mask, ws_ref[g], 0.0).astype(CDT)
            for nb in range(RB // GW):
                rs = slice(nb * GW, (nb + 1) * GW)
                vb = vn[rs, gs]
                mixed = _nn(wm, vb) + bs_ref[g]
                dz_ref[0, rs, gs] = (dmf[rs, gs] * mixed * dgu[rs, gs]).astype(CDT)
                dmx = dmf[rs, gs] * gu[rs, gs]
                dbs_ref[g] += dmx
                dmxc = dmx.astype(CDT)
                dws_ref[g] += _nt(dmxc, vb)
                dvn_s[rs, gs] = _tn(wm, dmxc)
        dvn = dvn_s[...]
        gp_ref[0:1, :] += jnp.sum(dvn * xh, axis=0, keepdims=True)
        gp_ref[1:2, :] += jnp.sum(dvn, axis=0, keepdims=True)
        dz_ref[1] = (_ln_bwd(xh, rs_, dvn * lg_ref[...]) * dgv).astype(CDT)

        @pl.when(i == NR - 1)
        def _():
            for g in range(4):
                dws_ref[g] = jnp.where(mask, dws_ref[g], 0.0)
                dbs_ref[g] = jnp.broadcast_to(jnp.sum(dbs_ref[g], axis=1, keepdims=True), (GW, GW))

    full3 = lambda i: (0, 0, 0)
    return _call(
        body, "mixC_bwd", (NR,),
        [pl.BlockSpec((2, RB, MW), lambda i: (2, i, 0)), pl.BlockSpec((None, RB, MW), lambda i: (2, i, 0)),
         _spspec("sgu_ln_g", MW, _zero), _spspec("sgu_ln_b", MW, _zero),
         pl.BlockSpec((4, GW, GW), full3), pl.BlockSpec((4, GW, 1), full3)],
        [pl.BlockSpec((2, RB, MW), lambda i: (0, i, 0)), pl.BlockSpec((4, GW, GW), full3), pl.BlockSpec((4, GW, GW), full3),
         pl.BlockSpec((8, MW), lambda i: (0, 0))],
        [SDS((2, S, MW), CDT), SDS((4, GW, GW), F32), SDS((4, GW, GW), F32), SDS((8, MW), F32)],
        [pltpu.VMEM((RB, MW), F32)])(z, dm, sp, sp, ws, bsc)


def _unpack_wb(wb_ref, wbf):
    for j in range(NS):
        for k in range(4):
            wbf[k, :, j * GW:(j + 1) * GW] = wb_ref[j, k]


def _merge_fwd(ma, mb, mc, yd, g, wb, pwo, x, sp):
    S, D = x.shape
    TM = min(256, S)

    def body(ma_ref, mb_ref, mc_ref, yd_ref, g_ref, wb_ref, wo_ref, x_ref, lg_ref, lb_ref, qg_ref,
             xo_ref, md_ref, yk_ref, mg_ref, mo_ref, wbf):
        @pl.when(pl.program_id(0) == 0)
        def _():
            _unpack_wb(wb_ref, wbf)

        xh, _ = _ln(yd_ref[...].astype(F32))
        sl, _ = _silu_parts(xh * lg_ref[...] + lb_ref[...])
        md = sl.astype(CDT)
        md_ref[...] = md
        merged = jnp.zeros((TM, D), F32)
        for k, m in enumerate((ma_ref[...], mb_ref[...], mc_ref[...], md)):
            yk = _nn(m, wbf[k])
            yk_ref[k] = yk.astype(CDT)
            merged = merged + g_ref[k].astype(F32) * yk
        mgc = merged.astype(CDT)
        mg_ref[...] = mgc
        mo = _nn(mgc, wo_ref[...].reshape(D, D))
        mo_ref[...] = mo.astype(CDT)
        n, _ = _rms(mo)
        xo_ref[...] = x_ref[...] + n * qg_ref[...]

    row = lambda i: (i, 0)
    rowm = pl.BlockSpec((TM, MW), row)
    rowd = pl.BlockSpec((TM, D), row)
    row4 = pl.BlockSpec((4, TM, D), lambda i: (0, i, 0))
    return _call(
        body, "merge_fwd", (S // TM,),
        [rowm, rowm, rowm, rowm, row4, pl.BlockSpec((NS, 4, MW, GW), lambda i: (0, 0, 0, 0)),
         pl.BlockSpec((NS, GW, D), lambda i: (0, 0, 0)), rowd,
         _spspec("conv_d_ln_g", MW, _zero), _spspec("conv_d_ln_b", MW, _zero), _spspec("mix_post_g", D, _zero)],
        [rowd, rowm, row4, rowd, rowd],
        [SDS((S, D), F32), SDS((S, MW), CDT), SDS((4, S, D), CDT), SDS((S, D), CDT), SDS((S, D), CDT)],
        [pltpu.VMEM((4, MW, D), CDT)])(ma, mb, mc, yd, g, wb, pwo, x, sp, sp, sp)


def _merge_bwd_act(dxo, mo, g, yk, wb, pwo, sp, dep):
    S, D = dxo.shape
    TM = min(256, S)

    def body(dxo_ref, mo_ref, g_ref, yk_ref, wb_ref, wo_ref, qg_ref, dep_ref, dmo_ref, dm_ref, dgp_ref, dyk_ref, gp_ref, wbf):
        @pl.when(pl.program_id(0) == 0)
        def _():
            gp_ref[...] = jnp.zeros_like(gp_ref)
            _unpack_wb(wb_ref, wbf)

        n, r = _rms(mo_ref[...].astype(F32))
        dmo, dg = _rms_bwd(n, r, qg_ref[...], dxo_ref[...])
        gp_ref[0:1, :] += dg
        dmoc = dmo.astype(CDT)
        dmo_ref[...] = dmoc
        dmg = _nt(dmoc, wo_ref[...].reshape(D, D))
        for k in range(4):
            gk = g_ref[k].astype(F32)
            dyk = (dmg * gk).astype(CDT)
            dyk_ref[k] = dyk
            dgp_ref[k] = (dmg * yk_ref[k].astype(F32) * gk * (1.0 - gk)).astype(CDT)
            dm_ref[k] = _nt(dyk, wbf[k]).astype(CDT)

    rowd = pl.BlockSpec((TM, D), lambda i: (i, 0))
    row4 = pl.BlockSpec((4, TM, D), lambda i: (0, i, 0))
    return _call(
        body, "merge_bwd_act", (S // TM,),
        [rowd, rowd, row4, row4, pl.BlockSpec((NS, 4, MW, GW), lambda i: (0, 0, 0, 0)),
         pl.BlockSpec((NS, GW, D), lambda i: (0, 0, 0)), _spspec("mix_post_g", D, _zero), _ANY],
        [rowd, pl.BlockSpec((4, TM, MW), lambda i: (0, i, 0)), row4, row4, pl.BlockSpec((8, D), lambda i: (0, 0))],
        [SDS((S, D), CDT), SDS((4, S, MW), CDT), SDS((4, S, D), CDT), SDS((4, S, D), CDT), SDS((8, D), F32)],
        [pltpu.VMEM((4, MW, D), CDT)])(dxo, mo, g, yk, wb, pwo, sp, dep)


def _merge_bwd_w(ma, mb, mc, md, dyk, mg, dmo):
    S, D = dmo.shape
    TK = min(512, S)
    NK = S // TK

    def body(ma_ref, mb_ref, mc_ref, md_ref, dyk_ref, mg_ref, dmo_ref, gwb_ref, gwo_ref, accb, acco):
        k = pl.program_id(0)

        @pl.when(k == 0)
        def _():
            accb[...] = jnp.zeros_like(accb)
            acco[...] = jnp.zeros_like(acco)

        for b, m in enumerate((ma_ref, mb_ref, mc_ref, md_ref)):
            accb[b] += _tn(m[...], dyk_ref[b])
        acco[...] += _tn(mg_ref[...], dmo_ref[...])

        @pl.when(k == NK - 1)
        def _():
            for j in range(NS):
                for b in range(4):
                    gwb_ref[j, b] = accb[b, :, j * GW:(j + 1) * GW].astype(CDT)
                gwo_ref[j] = acco[j * GW:(j + 1) * GW, :].astype(CDT)

    rowm = pl.BlockSpec((TK, MW), lambda k: (k, 0))
    rowd = pl.BlockSpec((TK, D), lambda k: (k, 0))
    return _call(
        body, "merge_bwd_w", (NK,),
        [rowm, rowm, rowm, rowm, pl.BlockSpec((4, TK, D), lambda k: (0, k, 0)), rowd, rowd],
        [pl.BlockSpec((NS, 4, MW, GW), lambda k: (0, 0, 0, 0)), pl.BlockSpec((NS, GW, D), lambda k: (0, 0, 0))],
        [SDS((NS, 4, MW, GW), CDT), SDS((NS, GW, D), CDT)],
        [pltpu.VMEM((4, MW, D), F32), pltpu.VMEM((D, D), F32)])(ma, mb, mc, md, dyk, mg, dmo)


def _xa_kv(mem, sp, pxa):
    M, D = mem.shape

    def body(m_ref, g_ref, wk_ref, wv_ref, mn_ref, k_ref, v_ref):
        n, _ = _rms(m_ref[...])
        mn = (n * g_ref[...]).astype(CDT)
        mn_ref[...] = mn
        k_ref[...] = _nn(mn, wk_ref[...].reshape(D, D)).astype(CDT)
        v_ref[...] = _nn(mn, wv_ref[...].reshape(D, D)).astype(CDT)

    full = pl.BlockSpec((M, D), lambda i: (0, 0))
    return _call(
        body, "xa_kv", (1,),
        [full, _spspec("mem_g", D, _zero), pl.BlockSpec((NS, GW, D), lambda i: (0, 1, 0)),
         pl.BlockSpec((NS, GW, D), lambda i: (0, 2, 0))],
        [full, full, full], [SDS((M, D), CDT)] * 3)(mem, sp, pxa, pxa)


def _softmax(s):
    e = jnp.exp(s - jnp.max(s, axis=-1, keepdims=True))
    return e / jnp.sum(e, axis=-1, keepdims=True)


def _xa_fwd(x, kk, vv, sp, pxa):
    S, D = x.shape
    M = kk.shape[0]
    TM = min(512, S)
    HD = D // XA_HEADS
    scale = HD ** -0.5

    def body(x_ref, k_ref, v_ref, pg_ref, qg_ref, wq_ref, wo_ref, xo_ref, hb_ref, q_ref, o_ref, po_ref):
        n, _ = _rms(x_ref[...])
        hb = (n * pg_ref[...]).astype(CDT)
        hb_ref[...] = hb
        q = _nn(hb, wq_ref[...].reshape(D, D)).astype(CDT)
        q_ref[...] = q
        for h in range(XA_HEADS):
            hs = slice(h * HD, (h + 1) * HD)
            p = _softmax(_nt(q[:, hs], k_ref[:, hs]) * scale)
            o_ref[:, hs] = _nn(p.astype(CDT), v_ref[:, hs]).astype(CDT)
        po = _nn(o_ref[...], wo_ref[...].reshape(D, D))
        po_ref[...] = po.astype(CDT)
        n, _ = _rms(po)
        xo_ref[...] = x_ref[...] + n * qg_ref[...]

    row = pl.BlockSpec((TM, D), lambda i: (i, 0))
    full = pl.BlockSpec((M, D), lambda i: (0, 0))
    return _call(
        body, "xa_fwd", (S // TM,),
        [row, full, full, _spspec("xa_pre_g", D, _zero), _spspec("xa_post_g", D, _zero),
         pl.BlockSpec((NS, GW, D), lambda i: (0, 0, 0)), pl.BlockSpec((NS, GW, D), lambda i: (0, 3, 0))],
        [row] * 5, [SDS((S, D), F32)] + [SDS((S, D), CDT)] * 4)(x, kk, vv, sp, sp, pxa, pxa)


def _xa_bwd_act(dxo, x, po, q, kk, vv, sp, pxa, dep):
    S, D = x.shape
    M = kk.shape[0]
    TM = min(512, S)
    HD = D // XA_HEADS
    scale = HD ** -0.5

    def body(dxo_ref, x_ref, po_ref, q_ref, k_ref, v_ref, pg_ref, qg_ref, wq_ref, wo_ref, dep_ref,
             dx_ref, dpo_ref, dq_ref, dk_ref, dv_ref, gp_ref):
        @pl.when(pl.program_id(0) == 0)
        def _():
            gp_ref[...] = jnp.zeros_like(gp_ref)
            dk_ref[...] = jnp.zeros_like(dk_ref)
            dv_ref[...] = jnp.zeros_like(dv_ref)

        n, r = _rms(po_ref[...].astype(F32))
        dpo, dg = _rms_bwd(n, r, qg_ref[...], dxo_ref[...])
        gp_ref[1:2, :] += dg
        dpoc = dpo.astype(CDT)
        dpo_ref[...] = dpoc
        do = _nt(dpoc, wo_ref[...].reshape(D, D)).astype(CDT)
        for h in range(XA_HEADS):
            hs = slice(h * HD, (h + 1) * HD)
            qh = q_ref[:, hs]
            p = _softmax(_nt(qh, k_ref[:, hs]) * scale)
            pc = p.astype(CDT)
            dv_ref[:, hs] += _tn(pc, do[:, hs])
            dp = _nt(do[:, hs], v_ref[:, hs])
            ds = (p * (dp - jnp.sum(p * dp, axis=-1, keepdims=True)) * scale).astype(CDT)
            dq_ref[:, hs] = _nn(ds, k_ref[:, hs]).astype(CDT)
            dk_ref[:, hs] += _tn(ds, qh)
        dhb = _nt(dq_ref[...], wq_ref[...].reshape(D, D))
        n, r = _rms(x_ref[...])
        dx, dg = _rms_bwd(n, r, pg_ref[...], dhb)
        dx_ref[...] = dxo_ref[...] + dx
        gp_ref[0:1, :] += dg

    row = pl.BlockSpec((TM, D), lambda i: (i, 0))
    full = pl.BlockSpec((M, D), lambda i: (0, 0))
    return _call(
        body, "xa_bwd_act", (S // TM,),
        [row, row, row, row, full, full, _spspec("xa_pre_g", D, _zero), _spspec("xa_post_g", D, _zero),
         pl.BlockSpec((NS, GW, D), lambda i: (0, 0, 0)), pl.BlockSpec((NS, GW, D), lambda i: (0, 3, 0)), _ANY],
        [row, row, row, full, full, pl.BlockSpec((8, D), lambda i: (0, 0))],
        [SDS((S, D), F32), SDS((S, D), CDT), SDS((S, D), CDT), SDS((M, D), F32), SDS((M, D), F32), SDS((8, D), F32)],
    )(dxo, x, po, q, kk, vv, sp, sp, pxa, pxa, dep)


def _xa_bwd_w(hb, dq, o, dpo, mn, dk, dv):
    S, D = hb.shape
    M = mn.shape[0]
    TK = min(512, S)
    NK = S // TK

    def body(hb_ref, dq_ref, o_ref, dpo_ref, mn_ref, dk_ref, dv_ref, g_ref, accq, acco):
        k = pl.program_id(0)

        @pl.when(k == 0)
        def _():
            accq[...] = jnp.zeros_like(accq)
            acco[...] = jnp.zeros_like(acco)

        accq[...] += _tn(hb_ref[...], dq_ref[...])
        acco[...] += _tn(o_ref[...], dpo_ref[...])

        @pl.when(k == NK - 1)
        def _():
            gk = _tn(mn_ref[...], dk_ref[...].astype(CDT))
            gv = _tn(mn_ref[...], dv_ref[...].astype(CDT))
            for j in range(NS):
                rs = slice(j * GW, (j + 1) * GW)
                g_ref[j, 0:GW, :] = accq[rs, :].astype(CDT)
                g_ref[j, GW:2 * GW, :] = gk[rs, :].astype(CDT)
                g_ref[j, 2 * GW:3 * GW, :] = gv[rs, :].astype(CDT)
                g_ref[j, 3 * GW:4 * GW, :] = acco[rs, :].astype(CDT)

    rowb = pl.BlockSpec((TK, D), lambda k: (k, 0))
    full = pl.BlockSpec((M, D), lambda k: (0, 0))
    return _call(
        body, "xa_bwd_w", (NK,),
        [rowb, rowb, rowb, rowb, full, full, full],
        pl.BlockSpec((NS, 4 * GW, D), lambda k: (0, 0, 0)), SDS((NS, 4 * GW, D), CDT),
        [pltpu.VMEM((D, D), F32), pltpu.VMEM((D, D), F32)])(hb, dq, o, dpo, mn, dk, dv)


def _xa_kv_bwd(mem, dk, dv, sp, pxa):
    M, D = mem.shape

    def body(m_ref, dk_ref, dv_ref, wk_ref, wv_ref, gp_ref):
        dmn = _nt(dk_ref[...].astype(CDT), wk_ref[...].reshape(D, D)) + _nt(dv_ref[...].astype(CDT), wv_ref[...].reshape(D, D))
        n, _ = _rms(m_ref[...])
        gp_ref[...] = jnp.zeros_like(gp_ref)
        gp_ref[0:1, :] = jnp.sum(dmn * n, axis=0, keepdims=True)

    full = pl.BlockSpec((M, D), lambda i: (0, 0))
    return _call(
        body, "xa_kv_bwd", (1,),
        [full, full, full, pl.BlockSpec((NS, GW, D), lambda i: (0, 1, 0)), pl.BlockSpec((NS, GW, D), lambda i: (0, 2, 0))],
        pl.BlockSpec((8, D), lambda i: (0, 0)), SDS((8, D), F32))(mem, dk, dv, pxa, pxa)


def _loss_head(y, t):
    S, D = y.shape
    TM = min(512, S)

    def body(y_ref, t_ref, dy_ref, l_ref):
        @pl.when(pl.program_id(0) == 0)
        def _():
            l_ref[...] = jnp.zeros_like(l_ref)

        e = y_ref[...] - t_ref[...]
        dy_ref[...] = e * (1.0 / D)
        l_ref[...] += 0.5 * jnp.sum(jnp.mean(e * e, axis=-1, keepdims=True), axis=0, keepdims=True)

    row = pl.BlockSpec((TM, D), lambda i: (i, 0))
    return _call(body, "loss_head", (S // TM,), [row, row], [row, pl.BlockSpec((8, 128), lambda i: (0, 0))],
                 [SDS((S, D), F32), SDS((8, 128), F32)])(y, t)


def _row_tile(rows, cols, limit=1 << 18, step=8):
    if rows * cols <= limit or rows % step:
        return rows
    best = step
    for t in range(step, rows + 1, step):
        if rows % t == 0 and t * cols <= limit:
            best = t
    return best


def _slot_sum(r):
    _, R, C = r.shape
    TR = _row_tile(R, C * NS, limit=1 << 21, step=16)

    def body(r_ref, o_ref):
        acc = r_ref[0].astype(F32)
        for j in range(1, NS):
            acc = acc + r_ref[j].astype(F32)
        o_ref[...] = acc

    return _call(body, "slot_sum", (R // TR,), [pl.BlockSpec((NS, TR, C), lambda i: (0, i, 0))],
                 pl.BlockSpec((TR, C), lambda i: (i, 0)), SDS((R, C), F32))(r)


def _adamw(w, g, m, v):
    shape = w.shape
    C = shape[-1]
    R = w.size // C
    TR = _row_tile(R, C)
    c1 = 1.0 - ADAM_B1 ** ADAM_STEP
    c2 = 1.0 - ADAM_B2 ** ADAM_STEP

    def body(w_ref, g_ref, m_ref, v_ref, d_ref, nm_ref, nv_ref):
        gg = g_ref[...]
        nm = ADAM_B1 * m_ref[...] + (1.0 - ADAM_B1) * gg
        nv = ADAM_B2 * v_ref[...] + (1.0 - ADAM_B2) * (gg * gg)
        nm_ref[...] = nm
        nv_ref[...] = nv
        d_ref[...] = -ADAM_LR * ((nm / c1) / (jnp.sqrt(nv / c2) + ADAM_EPS) + ADAM_WD * w_ref[...])

    blk = pl.BlockSpec((TR, C), lambda i: (i, 0))
    outs = _call(body, "adamw", (R // TR,), [blk] * 4, [blk] * 3, [SDS((R, C), F32)] * 3)(
        w.reshape(R, C), g.reshape(R, C), m.reshape(R, C), v.reshape(R, C))
    return tuple(o.reshape(shape) for o in outs)


def _exchange(arrs, scatter, name):
    n = len(arrs)
    np_ = NS - 1

    def body(*refs):
        ins, outs = refs[:n], refs[n:2 * n]
        send_sems, recv_sems, loc_sems = refs[2 * n:]
        x, y, c = lax.axis_index("x"), lax.axis_index("y"), lax.axis_index("c")
        me = 4 * x + 2 * y + c
        peers = []
        for f in range(1, NS):
            px = 1 - x if f & 4 else x
            py = 1 - y if f & 2 else y
            pc = 1 - c if f & 1 else c
            peers.append(((px, py, pc), 4 * px + 2 * py + pc))

        def src(a, pid):
            return ins[a].at[pid] if scatter else ins[a]

        local = [pltpu.make_async_copy(src(a, me), outs[a].at[me], loc_sems.at[a]) for a in range(n)]
        for cp in local:
            cp.start()
        sends = []
        for a in range(n):
            for f, (dev, pid) in enumerate(peers):
                sends.append(pltpu.make_async_remote_copy(
                    src_ref=src(a, pid), dst_ref=outs[a].at[me], send_sem=send_sems.at[a * np_ + f],
                    recv_sem=recv_sems.at[a * np_ + f], device_id=dev, device_id_type=pl.DeviceIdType.MESH))
        for cp in sends:
            cp.start()
        for a in range(n):
            for f, (dev, pid) in enumerate(peers):
                pltpu.make_async_remote_copy(
                    src_ref=src(a, pid), dst_ref=outs[a].at[pid], send_sem=send_sems.at[a * np_ + f],
                    recv_sem=recv_sems.at[a * np_ + f], device_id=dev, device_id_type=pl.DeviceIdType.MESH).wait_recv()
        for cp in sends:
            cp.wait_send()
        for cp in local:
            cp.wait()

    out_shape = [SDS(a.shape if scatter else (NS,) + a.shape, a.dtype) for a in arrs]
    anyspec = pl.BlockSpec(memory_space=pl.ANY)
    outs = pl.pallas_call(
        body, name=name, in_specs=[anyspec] * n, out_specs=[anyspec] * n, out_shape=out_shape,
        scratch_shapes=[pltpu.SemaphoreType.DMA((n * np_,)), pltpu.SemaphoreType.DMA((n * np_,)),
                        pltpu.SemaphoreType.DMA((n,))],
        compiler_params=pltpu.CompilerParams(has_side_effects=True))(*arrs)
    return list(outs)


def _peers():
    x, y, c = lax.axis_index("x"), lax.axis_index("y"), lax.axis_index("c")
    out = []
    for f in range(1, NS):
        px = 1 - x if f & 4 else x
        py = 1 - y if f & 2 else y
        pc = 1 - c if f & 1 else c
        out.append(((px, py, pc), 4 * px + 2 * py + pc))
    return 4 * x + 2 * y + c, out


def _exchange_copies(ins, lands, scatter, send_sems, recv_sems, loc_sems):
    me, peers = _peers()
    np_ = NS - 1

    def src(a, pid):
        return ins[a].at[pid] if scatter[a] else ins[a]

    def rcopy(a, f, dev, land_slot):
        return pltpu.make_async_remote_copy(
            src_ref=src(a, peers[f][1]), dst_ref=lands[a].at[land_slot], send_sem=send_sems.at[a * np_ + f],
            recv_sem=recv_sems.at[a * np_ + f], device_id=dev, device_id_type=pl.DeviceIdType.MESH)

    local = [pltpu.make_async_copy(src(a, me), lands[a].at[me], loc_sems.at[a]) for a in range(len(ins))]
    sends = [rcopy(a, f, dev, me) for a in range(len(ins)) for f, (dev, _) in enumerate(peers)]
    arrivals = [rcopy(a, f, dev, pid) for a in range(len(ins)) for f, (dev, pid) in enumerate(peers)]
    return local, sends, arrivals


_HBM = pl.BlockSpec(memory_space=pltpu.HBM)
_SEM = pl.BlockSpec(memory_space=pltpu.SEMAPHORE)
_ANY = pl.BlockSpec(memory_space=pl.ANY)


def _exchange_start(arrs, scatter, after, name):
    n = len(arrs)
    np_ = NS - 1
    lands = [lax.empty(a.shape if sc else (NS,) + a.shape, a.dtype) for a, sc in zip(arrs, scatter)]

    def body(*refs):
        ins, lnd = refs[:n], refs[n:2 * n]
        send_sems, recv_sems, loc_sems = refs[2 * n + 1:2 * n + 4]
        token = refs[-1]
        local, sends, _ = _exchange_copies(ins, lnd, scatter, send_sems, recv_sems, loc_sems)
        for cp in local + sends:
            cp.start()
        token[...] = jnp.zeros_like(token)

    hbm = lambda a: pltpu.HBM(a.shape, a.dtype)
    outs = pl.pallas_call(
        body, name=name,
        out_shape=(pltpu.SemaphoreType.DMA((n * np_,)), pltpu.SemaphoreType.DMA((n * np_,)), pltpu.SemaphoreType.DMA((n,)),
                   *[hbm(a) for a in arrs], *[hbm(a) for a in lands], SDS((8, 128), F32)),
        in_specs=[_HBM] * (2 * n) + [_ANY],
        out_specs=(_SEM, _SEM, _SEM, *([_HBM] * (2 * n)), pl.BlockSpec(memory_space=pltpu.VMEM)),
        input_output_aliases={i: 3 + i for i in range(2 * n)},
        compiler_params=pltpu.CompilerParams(has_side_effects=pltpu.SideEffectType.DATAFLOW_SIDE_EFFECTING),
    )(*[pltpu.with_memory_space_constraint(a, pltpu.HBM) for a in list(arrs) + lands], after)
    return {"sems": outs[:3], "ins": outs[3:3 + n], "lands": outs[3 + n:3 + 2 * n], "token": outs[-1], "scatter": scatter}


def _exchange_wait(h, after, name):
    n = len(h["ins"])
    scatter = h["scatter"]
    after = list(after) if isinstance(after, (list, tuple)) else [after]

    def body(*refs):
        ins, lnd = refs[:n], refs[n:2 * n]
        send_sems, recv_sems, loc_sems = refs[2 * n:2 * n + 3]
        local, sends, arrivals = _exchange_copies(ins, lnd, scatter, send_sems, recv_sems, loc_sems)
        for cp in sends:
            cp.wait_send()
        for cp in arrivals:
            cp.wait_recv()
        for cp in local:
            cp.wait()

    hbm = lambda a: pltpu.HBM(a.shape, a.dtype)
    outs = pl.pallas_call(
        body, name=name,
        out_shape=tuple(hbm(a) for a in list(h["ins"]) + list(h["lands"])),
        in_specs=[_HBM] * (2 * n) + [_SEM] * 3 + [_ANY] * len(after),
        out_specs=tuple([_HBM] * (2 * n)),
        input_output_aliases={i: i for i in range(2 * n)},
        compiler_params=pltpu.CompilerParams(has_side_effects=pltpu.SideEffectType.DATAFLOW_SIDE_EFFECTING),
    )(*h["ins"], *h["lands"], *h["sems"], *after)
    return list(outs[n:])


def _hgather_copies(ins, lands, send_a, recv_a, send_b, recv_b, loc_sems):
    x, y, c = lax.axis_index("x"), lax.axis_index("y"), lax.axis_index("c")
    me = 4 * x + 2 * y + c
    sib = (x, y, 1 - c)
    chips = [(1 - x, y), (x, 1 - y), (1 - x, 1 - y)]
    slot = lambda px, py, pc: 4 * px + 2 * py + pc

    def rcopy(src, dst, ssem, rsem, dev):
        return pltpu.make_async_remote_copy(src_ref=src, dst_ref=dst, send_sem=ssem, recv_sem=rsem, device_id=dev,
                                            device_id_type=pl.DeviceIdType.MESH)

    local, s1, a1, s2, a2 = [], [], [], [], []
    for a in range(len(ins)):
        first = [(sib, slot(x, y, 1 - c))] + [((px, py, c), slot(px, py, c)) for px, py in chips]
        for k, (dev, origin) in enumerate(first if send_a is not None else ()):
            s1.append(rcopy(ins[a], lands[a].at[me], send_a.at[4 * a + k], recv_a.at[4 * a + k], dev))
            a1.append(rcopy(ins[a], lands[a].at[origin], send_a.at[4 * a + k], recv_a.at[4 * a + k], dev))
        if send_a is not None:
            local.append(pltpu.make_async_copy(ins[a], lands[a].at[me], loc_sems.at[a]))
        for k, (px, py) in enumerate(chips if send_b is not None else ()):
            mine, theirs = lands[a].at[slot(px, py, c)], lands[a].at[slot(px, py, 1 - c)]
            s2.append(rcopy(mine, mine, send_b.at[3 * a + k], recv_b.at[3 * a + k], sib))
            a2.append(rcopy(mine, theirs, send_b.at[3 * a + k], recv_b.at[3 * a + k], sib))
    return local, s1, a1, s2, a2


def _hgather_start(arrs, after, name):
    n = len(arrs)
    lands = [lax.empty((NS,) + a.shape, a.dtype) for a in arrs]

    def body(*refs):
        ins, lnd = refs[:n], refs[n:2 * n]
        send_a, recv_a, loc_sems = refs[2 * n + 1:2 * n + 4]
        token = refs[-1]
        local, s1, _, _, _ = _hgather_copies(ins, lnd, send_a, recv_a, None, None, loc_sems)
        for cp in local + s1:
            cp.start()
        token[...] = jnp.zeros_like(token)

    hbm = lambda a: pltpu.HBM(a.shape, a.dtype)
    outs = pl.pallas_call(
        body, name=name,
        out_shape=(pltpu.SemaphoreType.DMA((4 * n,)), pltpu.SemaphoreType.DMA((4 * n,)), pltpu.SemaphoreType.DMA((n,)),
                   *[hbm(a) for a in arrs], *[hbm(a) for a in lands], SDS((8, 128), F32)),
        in_specs=[_HBM] * (2 * n) + [_ANY],
        out_specs=(_SEM, _SEM, _SEM, *([_HBM] * (2 * n)), pl.BlockSpec(memory_space=pltpu.VMEM)),
        input_output_aliases={i: 3 + i for i in range(2 * n)},
        compiler_params=pltpu.CompilerParams(has_side_effects=pltpu.SideEffectType.DATAFLOW_SIDE_EFFECTING),
    )(*[pltpu.with_memory_space_constraint(a, pltpu.HBM) for a in list(arrs) + lands], after)
    return {"sems": outs[:3], "ins": outs[3:3 + n], "lands": outs[3 + n:3 + 2 * n], "token": outs[-1]}


def _hgather_forward(h, after, name):
    n = len(h["ins"])

    def body(*refs):
        ins, lnd = refs[:n], refs[n:2 * n]
        send_a, recv_a, loc_sems = refs[2 * n:2 * n + 3]
        send_b, recv_b = refs[2 * n + 4:2 * n + 6]
        token = refs[-1]
        local, s1, a1, s2, _ = _hgather_copies(ins, lnd, send_a, recv_a, send_b, recv_b, loc_sems)
        for cp in s1:
            cp.wait_send()
        for cp in a1:
            cp.wait_recv()
        for cp in local:
            cp.wait()
        for cp in s2:
            cp.start()
        token[...] = jnp.zeros_like(token)

    hbm = lambda a: pltpu.HBM(a.shape, a.dtype)
    outs = pl.pallas_call(
        body, name=name,
        out_shape=(pltpu.SemaphoreType.DMA((3 * n,)), pltpu.SemaphoreType.DMA((3 * n,)),
                   *[hbm(a) for a in list(h["ins"]) + list(h["lands"])], SDS((8, 128), F32)),
        in_specs=[_HBM] * (2 * n) + [_SEM] * 3 + [_ANY],
        out_specs=(_SEM, _SEM, *([_HBM] * (2 * n)), pl.BlockSpec(memory_space=pltpu.VMEM)),
        input_output_aliases={i: 2 + i for i in range(2 * n)},
        compiler_params=pltpu.CompilerParams(has_side_effects=pltpu.SideEffectType.DATAFLOW_SIDE_EFFECTING),
    )(*h["ins"], *h["lands"], *h["sems"], after)
    return {"sems": outs[:2], "ins": outs[2:2 + n], "lands": outs[2 + n:2 + 2 * n], "token": outs[-1]}


def _hgather_wait(h, after, name):
    n = len(h["ins"])

    def body(*refs):
        ins, lnd = refs[:n], refs[n:2 * n]
        send_b, recv_b = refs[2 * n:2 * n + 2]
        _, _, _, s2, a2 = _hgather_copies(ins, lnd, None, None, send_b, recv_b, None)
        for cp in s2:
            cp.wait_send()
        for cp in a2:
            cp.wait_recv()

    hbm = lambda a: pltpu.HBM(a.shape, a.dtype)
    outs = pl.pallas_call(
        body, name=name,
        out_shape=tuple(hbm(a) for a in list(h["ins"]) + list(h["lands"])),
        in_specs=[_HBM] * (2 * n) + [_SEM] * 2 + [_ANY],
        out_specs=tuple([_HBM] * (2 * n)),
        input_output_aliases={i: i for i in range(2 * n)},
        compiler_params=pltpu.CompilerParams(has_side_effects=pltpu.SideEffectType.DATAFLOW_SIDE_EFFECTING),
    )(*h["ins"], *h["lands"], *h["sems"], after)
    return list(outs[n:])


_W_NAMES = ("ffn1_pre_g", "ffn1_post_g", "ffn1_w1", "ffn1_w3", "ffn1_w2", "mix_pre_g", "mix_post_g", "w_in", "conv_a_w",
            "conv_a_b", "pool_w", "pool_scale", "sgu_ln_g", "sgu_ln_b", "sgu_ws", "sgu_b", "conv_d_w", "conv_d_b",
            "conv_d_ln_g", "conv_d_ln_b", "w_branch", "w_gate", "b_gate", "w_o", "xa_pre_g", "xa_post_g", "mem_g",
            "xa_wq", "xa_wk", "xa_wv", "xa_wo", "ffn2_pre_g", "ffn2_post_g", "ffn2_w1", "ffn2_w3", "ffn2_w2")
_REP_NAMES = tuple(n for n, _ in _SP_NAMES) + ("pool_w", "sgu_ws", "sgu_b", "conv_a_w", "conv_d_w")


def _t(w):
    return jnp.swapaxes(w, -1, -2)


def _step(x, mem, loss_target, W, M, V):
    L = W["w_in"].shape[0]
    S, D = x.shape[1], x.shape[2]
    x0 = x.reshape(S, D)
    memf = mem.reshape(mem.shape[1], D)
    me = 4 * lax.axis_index("x") + 2 * lax.axis_index("y") + lax.axis_index("c")
    FS = W["ffn1_w2"].shape[1]
    KA, KD = W["conv_a_w"].shape[1], W["conv_d_w"].shape[1]
    CS = W["conv_a_w"].shape[2]

    cat = lambda l, parts: jnp.concatenate([(_t(W[n][l]) if tr else W[n][l]) for n, tr in parts], axis=0).astype(CDT)
    pf1 = [cat(l, (("ffn1_w1", 1), ("ffn1_w3", 1), ("ffn1_w2", 0))) for l in range(L)]
    pf2 = [cat(l, (("ffn2_w1", 1), ("ffn2_w3", 1), ("ffn2_w2", 0))) for l in range(L)]
    pma = [cat(l, (("w_in", 1), ("w_gate", 1))) for l in range(L)]
    pwo = [W["w_o"][l].astype(CDT) for l in range(L)]
    pxa = [cat(l, (("xa_wq", 0), ("xa_wk", 0), ("xa_wv", 0), ("xa_wo", 0))) for l in range(L)]
    wbs = [W["w_branch"][l].astype(CDT) for l in range(L)]
    cws = jnp.concatenate([W["conv_a_w"], W["conv_d_w"]], axis=1).reshape(-1, 128)
    sp_all = jnp.concatenate([W[n] for n, _ in _SP_NAMES], axis=1)
    bsc_all = W["sgu_b"][..., None]

    (cwg,) = _exchange([cws], False, "gather_conv_w")
    cwf = cwg.reshape(NS, L, KA + KD, CS).transpose(1, 2, 0, 3).reshape(L, KA + KD, NS * CS)

    def gather_start(l, after):
        return _hgather_start([pf1[l], pf2[l], pma[l], pwo[l], pxa[l], wbs[l]], after, f"gather_start_{l}")

    def gather_rest(h, after, tag):
        mid = _hgather_forward(h, after, f"gather_forward_{tag}")
        return _hgather_wait(mid, mid["token"], f"gather_wait_{tag}")

    packs = [None] * L
    first = [_hgather_start([pf1[0]], sp_all, "gather_start_0a")]
    saved = []
    xc = x0
    for l in range(L):
        if l == 0:
            (gf1,) = gather_rest(first[0], sp_all, "0a")
            first.append(_hgather_start([pma[0], pwo[0], wbs[0]], gf1, "gather_start_0b"))
        else:
            gf1, gf2, gma, gwo, gxa, gwb = packs[l]
        sp = sp_all[l:l + 1]
        cwa, cwd = cwf[l, :KA], cwf[l, KA:]
        wp, ws, bsc = W["pool_w"][l], W["sgu_ws"][l], bsc_all[l]
        s = {"x0": xc}
        nxt = gather_start(l + 1, gf1) if 0 < l < L - 1 else None
        xc, s["hb1"], s["a1"], s["b1"], s["y1"] = _ffn_fwd(xc, sp, "ffn1_pre_g", "ffn1_post_g", gf1,
                                                            (first[1] if l == 0 else nxt)["token"] if l == 0 or nxt else sp)
        s["x1"] = xc
        if l == 0:
            gma, gwo, gwb = gather_rest(first[1], xc, "0b")
            first.append(_hgather_start([pxa[0]], gma, "gather_start_0c"))
            first.append(_hgather_start([pf2[0]], first[2]["token"], "gather_start_0d"))
            nxt = gather_start(1, first[3]["token"]) if L > 1 else None
        s["hbm"], s["z"], s["g"] = _mix_in(xc, sp, gma, (nxt or first[3])["token"] if l == 0 else sp)
        s["ma"] = _mixA_fwd(s["z"], cwa, sp)
        s["mb"] = _mixB_fwd(s["z"], wp, sp)
        s["mc"] = _mixC_fwd(s["z"], ws, bsc, sp)
        s["yd"] = _mixD_conv_fwd(s["z"], cwd, sp)
        xc, s["md"], s["yk"], s["mg"], s["mo"] = _merge_fwd(s["ma"], s["mb"], s["mc"], s["yd"], s["g"], gwb, gwo, xc, sp)
        s["x2"] = xc
        if l == 0:
            (gxa,) = gather_rest(first[2], xc, "0c")
        s["mn"], s["k"], s["v"] = _xa_kv(memf, sp, gxa)
        xc, s["hbx"], s["q"], s["o"], s["po"] = _xa_fwd(xc, s["k"], s["v"], sp, gxa)
        s["x3"] = xc
        if l == 0:
            (gf2,) = gather_rest(first[3], xc, "0d")
            packs[0] = (gf1, gf2, gma, gwo, gxa, gwb)
        mid = _hgather_forward(nxt, xc, f"gather_forward_{l + 1}") if nxt and l > 0 else None
        xc, s["hb2"], s["a2"], s["b2"], s["y2"] = _ffn_fwd(xc, sp, "ffn2_pre_g", "ffn2_post_g", gf2,
                                                            mid["token"] if mid else sp)
        saved.append(s)
        if nxt:
            mid = mid or _hgather_forward(nxt, xc, f"gather_forward_{l + 1}")
            packs[l + 1] = _hgather_wait(mid, xc, f"gather_wait_{l + 1}")

    dx, lpart = _loss_head(xc, loss_target.reshape(S, D))
    loss = lax.psum(lpart[0, 0], ("x", "y", "c"))

    rep = {n: [None] * L for n in _REP_NAMES}
    summed = [dict() for _ in range(L)]
    pending = None
    last = []
    for l in reversed(range(L)):
        gf1, gf2, gma, gwo, gxa, gwb = packs[l]
        sp = sp_all[l:l + 1]
        cwa, cwd = cwf[l, :KA], cwf[l, KA:]
        wp, ws, bsc = W["pool_w"][l], W["sgu_ws"][l], bsc_all[l]
        s = saved[l]

        dx, dyb, da, db, gp = _ffn_bwd_act(dx, s["x3"], s["y2"], s["a2"], s["b2"], sp, "ffn2_pre_g", "ffn2_post_g", gf2,
                                           pending[1]["token"] if pending else sp)
        rep["ffn2_pre_g"][l], rep["ffn2_post_g"][l] = gp[0], gp[1]
        d_f2 = _ffn_bwd_w(s["hb2"], dyb, s["a2"], s["b2"], da, db)
        if l == 0:
            last.append(_exchange_start([d_f2], (True,), dx, "scatter_start_0a"))

        dx, dpo, dq, dk, dv, gp = _xa_bwd_act(dx, s["x2"], s["po"], s["q"], s["k"], s["v"], sp, gxa,
                                              last[-1]["token"] if last else sp)
        rep["xa_pre_g"][l], rep["xa_post_g"][l] = gp[0], gp[1]
        d_xa = _xa_bwd_w(s["hbx"], dq, s["o"], dpo, s["mn"], dk, dv)
        rep["mem_g"][l] = _xa_kv_bwd(memf, dk, dv, sp, gxa)[0]
        if l == 0:
            last.append(_exchange_start([d_xa], (True,), dx, "scatter_start_0b"))

        dmo, dm, dgp, dyk, gp = _merge_bwd_act(dx, s["mo"], s["g"], s["yk"], gwb, gwo, sp, last[-1]["token"] if last else sp)
        rep["mix_post_g"][l] = gp[0]
        d_wb, d_wo = _merge_bwd_w(s["ma"], s["mb"], s["mc"], s["md"], dyk, s["mg"], dmo)
        dza, dcw, gp = _mixA_bwd(s["z"], dm, cwa, sp)
        rep["conv_a_w"][l], rep["conv_a_b"][l] = dcw, gp[0]
        dzb, dwp, gp = _mixB_bwd(s["z"], dm, wp, sp)
        rep["pool_w"][l], rep["pool_scale"][l] = dwp, gp[0]
        dzc, dws, dbs, gp = _mixC_bwd(s["z"], dm, ws, bsc, sp)
        rep["sgu_ws"][l], rep["sgu_b"][l], rep["sgu_ln_g"][l], rep["sgu_ln_b"][l] = dws, dbs[:, :, 0], gp[0], gp[1]
        dyd, gp = _mixD_ln_bwd(dm, s["yd"], sp)
        rep["conv_d_ln_g"][l], rep["conv_d_ln_b"][l] = gp[0], gp[1]
        dzd, dcw, gp = _mixD_conv_bwd(s["z"], dyd, cwd)
        rep["conv_d_w"][l], rep["conv_d_b"][l] = dcw, gp[0]
        dz = jnp.concatenate([dza, dzb, dzc, dzd], axis=0)
        dx, gp = _mix_in_bwd_act(dz, dgp, dx, s["x1"], sp, gma)
        rep["mix_pre_g"][l] = gp[0]
        d_ma, dbg = _mix_in_bwd_w(dz, dgp, s["hbm"])
        rep["b_gate"][l] = dbg[:, 0, :].reshape(-1)
        if l == 0:
            last.append(_exchange_start([d_ma, d_wo, d_wb], (True,) * 3, dx, "scatter_start_0c"))

        dx, dyb, da, db, gp = _ffn_bwd_act(dx, s["x0"], s["y1"], s["a1"], s["b1"], sp, "ffn1_pre_g", "ffn1_post_g", gf1,
                                           last[-1]["token"] if last else sp)
        rep["ffn1_pre_g"][l], rep["ffn1_post_g"][l] = gp[0], gp[1]
        d_f1 = _ffn_bwd_w(s["hb1"], dyb, s["a1"], s["b1"], da, db)

        if pending:
            r = _exchange_wait(pending[1], dx, f"scatter_wait_{pending[0]}")
            summed[pending[0]] = dict(zip(("f1", "f2", "ma", "wo", "xa", "wb", "flat"), r))
        flat = jnp.concatenate([rep[n][l].reshape(-1) for n in _REP_NAMES])
        flat = jnp.pad(flat, (0, -flat.size % 1024)).reshape(-1, 128)
        if l == 0:
            last.append(_exchange_start([d_f1, flat], (True, False), dx, "scatter_start_0d"))
        else:
            pending = (l, _exchange_start([d_f1, d_f2, d_ma, d_wo, d_xa, d_wb, flat], (True,) * 6 + (False,), dx,
                                          f"scatter_start_{l}"))

    sums = lambda r: _slot_sum(r.reshape(NS, -1, r.shape[-1]))
    for l in range(1, L):
        summed[l] = {k: sums(v) for k, v in summed[l].items()}
    (r,) = _exchange_wait(last[0], dx, "scatter_wait_0a")
    summed[0]["f2"] = sums(r)
    (r,) = _exchange_wait(last[1], dx, "scatter_wait_0b")
    summed[0]["xa"] = sums(r)
    r = _exchange_wait(last[2], dx, "scatter_wait_0c")
    summed[0]["ma"], summed[0]["wo"], summed[0]["wb"] = (sums(v) for v in r)

    G, deltas, new_m, new_v = {}, {}, {}, {}
    stk = lambda k: jnp.stack([summed[l][k] for l in range(L)])
    f2, ma_, wo_, xa_, wb_ = (stk(k) for k in ("f2", "ma", "wo", "xa", "wb"))
    G["ffn2_w1"], G["ffn2_w3"], G["ffn2_w2"] = _t(f2[:, :FS]), _t(f2[:, FS:2 * FS]), f2[:, 2 * FS:]
    G["w_in"], G["w_gate"] = _t(ma_[:, :MW]), _t(ma_[:, MW:])
    G["w_o"] = wo_
    G["xa_wq"], G["xa_wk"], G["xa_wv"], G["xa_wo"] = (xa_[:, i * GW:(i + 1) * GW] for i in range(4))
    G["w_branch"] = wb_.reshape(W["w_branch"].shape)
    for n in tuple(G):
        deltas[n], new_m[n], new_v[n] = _adamw(W[n], G[n], M[n], V[n])
    done = [deltas[n] for n in deltas] + [summed[l][k] for l in range(1, L) for k in ("f1", "flat")]
    r = _exchange_wait(last[3], done, "scatter_wait_0d")
    summed[0]["f1"], summed[0]["flat"] = (sums(v) for v in r)
    f1 = stk("f1")
    G["ffn1_w1"], G["ffn1_w3"], G["ffn1_w2"] = _t(f1[:, :FS]), _t(f1[:, FS:2 * FS]), f1[:, 2 * FS:]

    tot = [summed[l]["flat"].reshape(-1) for l in range(L)]
    off = 0
    for n in _REP_NAMES:
        shape = (KA, NS * CS) if n == "conv_a_w" else (KD, NS * CS) if n == "conv_d_w" else W[n].shape[1:]
        size = 1
        for d in shape:
            size *= d
        G[n] = jnp.stack([tot[l][off:off + size].reshape(shape) for l in range(L)])
        off += size
    for n in ("conv_a_w", "conv_d_w"):
        G[n] = lax.dynamic_slice_in_dim(G[n], me * CS, CS, axis=2)

    for n in _W_NAMES:
        if n not in deltas:
            deltas[n], new_m[n], new_v[n] = _adamw(W[n], G[n], M[n], V[n])
    grad_x = dx.reshape(x.shape)
    return (loss, grad_x, *[G[n] for n in _W_NAMES], *[deltas[n] for n in _W_NAMES],
            *[new_m[n] for n in _W_NAMES], *[new_v[n] for n in _W_NAMES])


def kernel(x, mem, ffn1_pre_g, ffn1_post_g, ffn1_w1, ffn1_w3, ffn1_w2, mix_pre_g, mix_post_g, w_in, conv_a_w, conv_a_b, pool_w, pool_scale, sgu_ln_g, sgu_ln_b, sgu_ws, sgu_b, conv_d_w, conv_d_b, conv_d_ln_g, conv_d_ln_b, w_branch, w_gate, b_gate, w_o, xa_pre_g, xa_post_g, mem_g, xa_wq, xa_wk, xa_wv, xa_wo, ffn2_pre_g, ffn2_post_g, ffn2_w1, ffn2_w3, ffn2_w2, loss_target, m_ffn1_pre_g, m_ffn1_post_g, m_ffn1_w1, m_ffn1_w3, m_ffn1_w2, m_mix_pre_g, m_mix_post_g, m_w_in, m_conv_a_w, m_conv_a_b, m_pool_w, m_pool_scale, m_sgu_ln_g, m_sgu_ln_b, m_sgu_ws, m_sgu_b, m_conv_d_w, m_conv_d_b, m_conv_d_ln_g, m_conv_d_ln_b, m_w_branch, m_w_gate, m_b_gate, m_w_o, m_xa_pre_g, m_xa_post_g, m_mem_g, m_xa_wq, m_xa_wk, m_xa_wv, m_xa_wo, m_ffn2_pre_g, m_ffn2_post_g, m_ffn2_w1, m_ffn2_w3, m_ffn2_w2, v_ffn1_pre_g, v_ffn1_post_g, v_ffn1_w1, v_ffn1_w3, v_ffn1_w2, v_mix_pre_g, v_mix_post_g, v_w_in, v_conv_a_w, v_conv_a_b, v_pool_w, v_pool_scale, v_sgu_ln_g, v_sgu_ln_b, v_sgu_ws, v_sgu_b, v_conv_d_w, v_conv_d_b, v_conv_d_ln_g, v_conv_d_ln_b, v_w_branch, v_w_gate, v_b_gate, v_w_o, v_xa_pre_g, v_xa_post_g, v_mem_g, v_xa_wq, v_xa_wk, v_xa_wv, v_xa_wo, v_ffn2_pre_g, v_ffn2_post_g, v_ffn2_w1, v_ffn2_w3, v_ffn2_w2):
    args = dict(locals())
    W = {n: args[n] for n in _W_NAMES}
    M = {n: args["m_" + n] for n in _W_NAMES}
    V = {n: args["v_" + n] for n in _W_NAMES}
    return _step(x, mem, loss_target, W, M, V)
```

```python
import jax
import jax.numpy as jnp
from jax import lax
from jax.experimental import pallas as pl
from jax.experimental.pallas import tpu as pltpu

F32 = jnp.float32
CDT = jnp.bfloat16
EPS = 1e-6
NS = 8
GW = 128
MW = 512
CHUNK = 64
XA_HEADS = 4
POOL_WINDOWS = (2, 4, 8, 16)
VMEM_LIMIT = 56 * 1024 * 1024
ADAM_LR, ADAM_B1, ADAM_B2, ADAM_EPS, ADAM_WD, ADAM_STEP = 0.001, 0.9, 0.999, 1e-08, 0.01, 10

SDS = jax.ShapeDtypeStruct

_SP_NAMES = (("ffn1_pre_g", 1024), ("ffn1_post_g", 1024), ("mix_pre_g", 1024), ("mix_post_g", 1024),
             ("xa_pre_g", 1024), ("xa_post_g", 1024), ("mem_g", 1024), ("ffn2_pre_g", 1024), ("ffn2_post_g", 1024),
             ("conv_a_b", 512), ("pool_scale", 512), ("sgu_ln_g", 512), ("sgu_ln_b", 512), ("conv_d_b", 512),
             ("conv_d_ln_g", 512), ("conv_d_ln_b", 512), ("b_gate", 4096))
_SP = {}
_off = 0
for _n, _w in _SP_NAMES:
    _SP[_n] = (_off, _w)
    _off += _w
_SP_TOTAL = _off


def _call(body, name, grid, in_specs, out_specs, out_shape, scratch=()):
    return pl.pallas_call(
        body, name=name, grid=grid, in_specs=in_specs, out_specs=out_specs, out_shape=out_shape,
        scratch_shapes=list(scratch),
        compiler_params=pltpu.CompilerParams(dimension_semantics=("arbitrary",) * len(grid),
                                             vmem_limit_bytes=VMEM_LIMIT))


def _nn(a, b):
    return lax.dot_general(a, b, (((1,), (0,)), ((), ())), preferred_element_type=F32)


def _nt(a, b):
    return lax.dot_general(a, b, (((1,), (1,)), ((), ())), preferred_element_type=F32)


def _tn(a, b):
    return lax.dot_general(a, b, (((0,), (0,)), ((), ())), preferred_element_type=F32)


def _rms(x):
    r = lax.rsqrt(jnp.mean(x * x, axis=-1, keepdims=True) + EPS)
    return x * r, r


def _rms_bwd(n, r, g, dout):
    dn = dout * g
    dx = r * (dn - n * jnp.mean(dn * n, axis=-1, keepdims=True))
    return dx, jnp.sum(dout * n, axis=0, keepdims=True)


def _ln(y):
    mu = jnp.mean(y, axis=-1, keepdims=True)
    yc = y - mu
    rs = lax.rsqrt(jnp.mean(yc * yc, axis=-1, keepdims=True) + EPS)
    return yc * rs, rs


def _ln_bwd(xh, rs, dxh):
    return rs * (dxh - jnp.mean(dxh, axis=-1, keepdims=True) - xh * jnp.mean(dxh * xh, axis=-1, keepdims=True))


def _silu_parts(a):
    s = jax.nn.sigmoid(a)
    sl = a * s
    return sl, s + sl * (1.0 - s)


_GELU_C = 0.7978845608028654
_GELU_A = 0.044715


def _gelu(x):
    return 0.5 * x * (1.0 + jnp.tanh(_GELU_C * (x + _GELU_A * x * x * x)))


def _gelu_parts(x):
    t = jnp.tanh(_GELU_C * (x + _GELU_A * x * x * x))
    g = 0.5 * x * (1.0 + t)
    dg = 0.5 * (1.0 + t) + 0.5 * x * (1.0 - t * t) * _GELU_C * (1.0 + 3.0 * _GELU_A * x * x)
    return g, dg


def _spspec(name, width, imap):
    off = _SP[name][0]
    assert off % width == 0
    return pl.BlockSpec((1, width), lambda *a: (0, off // width + imap(*a)))


def _zero(*a):
    return 0


def _row_once(tm, d):
    return pl.BlockSpec((tm, d), lambda i, j: (i, 0), pipeline_mode=pl.Buffered(1))


FFN_SG = 2


def _ffn_fwd(x, sp, pre, post, pf, dep):
    S, D = x.shape
    FS = pf.shape[1] // 3
    TM = min(512, S)
    SG, NG, W = FFN_SG, NS // FFN_SG, FFN_SG * FS

    def body(x_ref, pg_ref, qg_ref, w1_ref, w3_ref, w2_ref, dep_ref, xo_ref, hb_ref, a_ref, b_ref, y_ref, hb_s, acc):
        j = pl.program_id(1)

        @pl.when(j == 0)
        def _():
            n, _ = _rms(x_ref[...])
            hb = (n * pg_ref[...]).astype(CDT)
            hb_s[...] = hb
            hb_ref[...] = hb
            acc[...] = jnp.zeros_like(acc)

        hb = hb_s[...]
        a = _nt(hb, w1_ref[...].reshape(W, D))
        b = _nt(hb, w3_ref[...].reshape(W, D))
        a_ref[...] = a.astype(CDT)
        b_ref[...] = b.astype(CDT)
        u = (a * jax.nn.sigmoid(a) * b).astype(CDT)
        acc[...] += _nn(u, w2_ref[...].reshape(W, D))

        @pl.when(j == NG - 1)
        def _():
            y = acc[...]
            y_ref[...] = y.astype(CDT)
            n, _ = _rms(y)
            xo_ref[...] = x_ref[...] + 0.5 * (n * qg_ref[...])

    row1 = pl.BlockSpec((TM, D), lambda i, j: (i, 0))
    grp = lambda i, j: (j, i, 0)
    return _call(
        body, "ffn_fwd", (S // TM, NG),
        [row1, _spspec(pre, D, _zero), _spspec(post, D, _zero),
         pl.BlockSpec((SG, FS, D), lambda i, j: (j, 0, 0)), pl.BlockSpec((SG, FS, D), lambda i, j: (j, 1, 0)),
         pl.BlockSpec((SG, FS, D), lambda i, j: (j, 2, 0)), pl.BlockSpec(memory_space=pl.ANY)],
        [row1, row1, pl.BlockSpec((None, TM, W), grp), pl.BlockSpec((None, TM, W), grp), row1],
        [SDS((S, D), F32), SDS((S, D), CDT), SDS((NG, S, W), CDT), SDS((NG, S, W), CDT), SDS((S, D), CDT)],
        [pltpu.VMEM((TM, D), CDT), pltpu.VMEM((TM, D), F32)])(x, sp, sp, pf, pf, pf, dep)


def _ffn_bwd_act(dxo, x, y, a, b, sp, pre, post, pf, dep):
    S, D = x.shape
    FS = pf.shape[1] // 3
    TM = min(512, S)
    SG, NG, W = FFN_SG, NS // FFN_SG, FFN_SG * FS

    def body(dxo_ref, x_ref, y_ref, a_ref, b_ref, pg_ref, qg_ref, w1_ref, w3_ref, w2_ref, dep_ref,
             dx_ref, dyb_ref, da_ref, db_ref, gp_ref, dyb_s, acc):
        i = pl.program_id(0)
        j = pl.program_id(1)

        @pl.when((i == 0) & (j == 0))
        def _():
            gp_ref[...] = jnp.zeros_like(gp_ref)

        @pl.when(j == 0)
        def _():
            n, r = _rms(y_ref[...].astype(F32))
            dy, dg = _rms_bwd(n, r, qg_ref[...], 0.5 * dxo_ref[...])
            dyb = dy.astype(CDT)
            dyb_s[...] = dyb
            dyb_ref[...] = dyb
            gp_ref[1:2, :] += dg
            acc[...] = jnp.zeros_like(acc)

        sl, dsl = _silu_parts(a_ref[...].astype(F32))
        du = _nt(dyb_s[...], w2_ref[...].reshape(W, D))
        db = (du * sl).astype(CDT)
        da = (du * b_ref[...].astype(F32) * dsl).astype(CDT)
        da_ref[...] = da
        db_ref[...] = db
        acc[...] += _nn(da, w1_ref[...].reshape(W, D)) + _nn(db, w3_ref[...].reshape(W, D))

        @pl.when(j == NG - 1)
        def _():
            n, r = _rms(x_ref[...])
            dx, dg = _rms_bwd(n, r, pg_ref[...], acc[...])
            dx_ref[...] = dxo_ref[...] + dx
            gp_ref[0:1, :] += dg

    row = lambda i, j: (i, 0)
    grp = lambda i, j: (j, i, 0)
    return _call(
        body, "ffn_bwd_act", (S // TM, NG),
        [pl.BlockSpec((TM, D), row), pl.BlockSpec((TM, D), row), pl.BlockSpec((TM, D), row),
         pl.BlockSpec((None, TM, W), grp), pl.BlockSpec((None, TM, W), grp),
         _spspec(pre, D, _zero), _spspec(post, D, _zero),
         pl.BlockSpec((SG, FS, D), lambda i, j: (j, 0, 0)), pl.BlockSpec((SG, FS, D), lambda i, j: (j, 1, 0)),
         pl.BlockSpec((SG, FS, D), lambda i, j: (j, 2, 0)), pl.BlockSpec(memory_space=pl.ANY)],
        [pl.BlockSpec((TM, D), row), pl.BlockSpec((TM, D), row), pl.BlockSpec((None, TM, W), grp),
         pl.BlockSpec((None, TM, W), grp), pl.BlockSpec((8, D), lambda i, j: (0, 0))],
        [SDS((S, D), F32), SDS((S, D), CDT), SDS((NG, S, W), CDT), SDS((NG, S, W), CDT), SDS((8, D), F32)],
        [pltpu.VMEM((TM, D), CDT), pltpu.VMEM((TM, D), F32)])(dxo, x, y, a, b, sp, sp, pf, pf, pf, dep)


def _ffn_bwd_w(hb, dyb, a, b, da, db):
    S, D = hb.shape
    SG, NG = FFN_SG, NS // FFN_SG
    W = a.shape[2]
    FS = W // SG
    TK = min(512, S)
    NK = S // TK

    def body(hb_ref, dyb_ref, a_ref, b_ref, da_ref, db_ref, g_ref, acc):
        k = pl.program_id(1)

        @pl.when(k == 0)
        def _():
            acc[...] = jnp.zeros_like(acc)

        af = a_ref[...].astype(F32)
        u = (af * jax.nn.sigmoid(af) * b_ref[...].astype(F32)).astype(CDT)
        hb = hb_ref[...]
        acc[0:W, :] += _tn(da_ref[...], hb)
        acc[W:2 * W, :] += _tn(db_ref[...], hb)
        acc[2 * W:3 * W, :] += _tn(u, dyb_ref[...])

        @pl.when(k == NK - 1)
        def _():
            for s in range(SG):
                for r in range(3):
                    g_ref[s, r * FS:(r + 1) * FS, :] = acc[r * W + s * FS:r * W + (s + 1) * FS, :].astype(CDT)

    row = lambda j, k: (k, 0)
    grp = lambda j, k: (j, k, 0)
    return _call(
        body, "ffn_bwd_w", (NG, NK),
        [pl.BlockSpec((TK, D), row), pl.BlockSpec((TK, D), row)] + [pl.BlockSpec((None, TK, W), grp)] * 4,
        pl.BlockSpec((SG, 3 * FS, D), lambda j, k: (j, 0, 0)),
        SDS((NS, 3 * FS, D), CDT),
        [pltpu.VMEM((3 * W, D), F32)])(hb, dyb, a, b, da, db)


def _mix_in(x, sp, pma, dep):
    S, D = x.shape
    TM = min(1024, S)

    def body(x_ref, pg_ref, bg_ref, wi_ref, wg_ref, dep_ref, hb_ref, z_ref, g_ref, hb_s):
        @pl.when(pl.program_id(1) == 0)
        def _():
            n, _ = _rms(x_ref[...])
            hb = (n * pg_ref[...]).astype(CDT)
            hb_s[...] = hb
            hb_ref[...] = hb

        hb = hb_s[...]
        z_ref[...] = _nt(hb, wi_ref[...]).astype(CDT)
        g_ref[...] = jax.nn.sigmoid(_nt(hb, wg_ref[...]) + bg_ref[...]).astype(CDT)

    return _call(
        body, "mix_in", (S // TM, NS),
        [_row_once(TM, D), _spspec("mix_pre_g", D, _zero), _spspec("b_gate", MW, lambda i, j: j),
         pl.BlockSpec((None, MW, D), lambda i, j: (j, 0, 0)), pl.BlockSpec((None, MW, D), lambda i, j: (j, 1, 0)), _ANY],
        [_row_once(TM, D), pl.BlockSpec((None, TM, MW), lambda i, j: (j, i, 0)),
         pl.BlockSpec((None, TM, MW), lambda i, j: (j // 2, i, j % 2))],
        [SDS((S, D), CDT), SDS((NS, S, MW), CDT), SDS((4, S, D), CDT)],
        [pltpu.VMEM((TM, D), CDT)])(x, sp, sp, pma, pma, dep)


def _mix_in_bwd_act(dz, dgp, dxr, x, sp, pma):
    S, D = x.shape
    TM = min(1024, S)

    def body(dz_ref, dg_ref, dxr_ref, x_ref, pg_ref, wi_ref, wg_ref, dx_ref, gp_ref, acc):
        i = pl.program_id(0)
        j = pl.program_id(1)

        @pl.when((i == 0) & (j == 0))
        def _():
            gp_ref[...] = jnp.zeros_like(gp_ref)

        @pl.when(j == 0)
        def _():
            acc[...] = jnp.zeros_like(acc)

        acc[...] += _nn(dz_ref[...], wi_ref[...]) + _nn(dg_ref[...], wg_ref[...])

        @pl.when(j == NS - 1)
        def _():
            n, r = _rms(x_ref[...])
            dx, dg = _rms_bwd(n, r, pg_ref[...], acc[...])
            dx_ref[...] = dxr_ref[...] + dx
            gp_ref[0:1, :] += dg

    return _call(
        body, "mix_in_bwd_act", (S // TM, NS),
        [pl.BlockSpec((None, TM, MW), lambda i, j: (j, i, 0)), pl.BlockSpec((None, TM, MW), lambda i, j: (j // 2, i, j % 2)),
         _row_once(TM, D), _row_once(TM, D), _spspec("mix_pre_g", D, _zero),
         pl.BlockSpec((None, MW, D), lambda i, j: (j, 0, 0)), pl.BlockSpec((None, MW, D), lambda i, j: (j, 1, 0))],
        [_row_once(TM, D), pl.BlockSpec((8, D), lambda i, j: (0, 0))],
        [SDS((S, D), F32), SDS((8, D), F32)],
        [pltpu.VMEM((TM, D), F32)])(dz, dgp, dxr, x, sp, pma, pma)


def _mix_in_bwd_w(dz, dgp, hb):
    S, D = hb.shape
    TK = min(512, S)
    NK = S // TK

    def body(dz_ref, dg_ref, hb_ref, g_ref, bg_ref, acc):
        k = pl.program_id(1)

        @pl.when(k == 0)
        def _():
            acc[...] = jnp.zeros_like(acc)
            bg_ref[...] = jnp.zeros_like(bg_ref)

        hb = hb_ref[...]
        dg = dg_ref[...]
        acc[0:MW, :] += _tn(dz_ref[...], hb)
        acc[MW:2 * MW, :] += _tn(dg, hb)
        bg_ref[0:1, :] += jnp.sum(dg.astype(F32), axis=0, keepdims=True)

        @pl.when(k == NK - 1)
        def _():
            g_ref[...] = acc[...].astype(CDT)

    return _call(
        body, "mix_in_bwd_w", (NS, NK),
        [pl.BlockSpec((None, TK, MW), lambda j, k: (j, k, 0)), pl.BlockSpec((None, TK, MW), lambda j, k: (j // 2, k, j % 2)),
         pl.BlockSpec((TK, D), lambda j, k: (k, 0))],
        [pl.BlockSpec((None, 2 * MW, D), lambda j, k: (j, 0, 0)), pl.BlockSpec((None, 8, MW), lambda j, k: (j, 0, 0))],
        [SDS((NS, 2 * MW, D), CDT), SDS((NS, 8, MW), F32)],
        [pltpu.VMEM((2 * MW, D), F32)])(dz, dgp, hb)


def _causal_taps(pad_ref, i, ch, halo, k_taps, lanes=slice(None)):
    val = pad_ref[pl.ds(pl.multiple_of(i * ch, 8), ch + halo), lanes]
    out = []
    for k in range(k_taps):
        s = k_taps - 1 - k
        out.append((k, (pltpu.roll(val, s, 0) if s else val)[halo:, :]))
    return out


def _anti_taps(pad_ref, i, ch, halo, k_taps, lanes=slice(None)):
    val = pad_ref[pl.ds(pl.multiple_of(i * ch, 8), ch + halo), lanes]
    n = ch + halo
    out = []
    for k in range(k_taps):
        s = k_taps - 1 - k
        out.append((k, (pltpu.roll(val, n - s, 0) if s else val)[:ch, :]))
    return out


def _conv_geometry(S, k_taps):
    halo = 8 * ((k_taps - 1 + 7) // 8)
    ch = min(256, S)
    return halo, ch, S // ch


def _rows(i, ch):
    return pl.ds(pl.multiple_of(i * ch, ch), ch)


def _mixA_fwd(z, cw, sp):
    S = z.shape[1]
    K = cw.shape[0]
    H, CH, NCH = _conv_geometry(S, K)

    def body(z_ref, w_ref, b_ref, o_ref, pad):
        pad[0:H, :] = jnp.zeros((H, GW), F32)

        def fill(i, c):
            r = _rows(i, CH)
            pad[pl.ds(pl.multiple_of(i * CH + H, 8), CH), :] = z_ref[2, r, :].astype(F32) * z_ref[0, r, :].astype(F32)
            return c

        lax.fori_loop(0, NCH, fill, 0)

        def conv(i, c):
            r = _rows(i, CH)
            acc = jnp.zeros((CH, GW), F32)
            for k, sh in _causal_taps(pad, i, CH, H, K):
                acc = acc + w_ref[k:k + 1, :] * sh
            o_ref[r, :] = (z_ref[1, r, :].astype(F32) * (acc + b_ref[...])).astype(CDT)
            return c

        lax.fori_loop(0, NCH, conv, 0)

    return _call(
        body, "mixA_fwd", (MW // GW,),
        [pl.BlockSpec((3, S, GW), lambda c: (0, 0, c)), pl.BlockSpec((K, GW), lambda c: (0, c)),
         _spspec("conv_a_b", GW, lambda c: c)],
        pl.BlockSpec((S, GW), lambda c: (0, c)), SDS((S, MW), CDT),
        [pltpu.VMEM((H + S, GW), F32)])(z, cw, sp)


def _mixA_bwd(z, dm, cw, sp):
    S = z.shape[1]
    K = cw.shape[0]
    H, CH, NCH = _conv_geometry(S, K)

    def body(z_ref, dm_ref, w_ref, b_ref, dz_ref, dw_ref, db_ref, pad, dpad, dw_s):
        pad[0:H, :] = jnp.zeros((H, GW), F32)
        dpad[pl.ds(S, H), :] = jnp.zeros((H, GW), F32)
        dw_s[...] = jnp.zeros_like(dw_s)
        db_ref[...] = jnp.zeros_like(db_ref)

        def fill(i, c):
            r = _rows(i, CH)
            pad[pl.ds(pl.multiple_of(i * CH + H, 8), CH), :] = z_ref[2, r, :].astype(F32) * z_ref[0, r, :].astype(F32)
            return c

        lax.fori_loop(0, NCH, fill, 0)

        def p1(i, c):
            r = _rows(i, CH)
            taps = _causal_taps(pad, i, CH, H, K)
            acc = jnp.zeros((CH, GW), F32)
            for k, sh in taps:
                acc = acc + w_ref[k:k + 1, :] * sh
            dmf = dm_ref[r, :].astype(F32)
            dz_ref[1, r, :] = (dmf * (acc + b_ref[...])).astype(CDT)
            dc = dmf * z_ref[1, r, :].astype(F32)
            dpad[r, :] = dc
            for k, sh in taps:
                dw_s[k:k + 1, :] += jnp.sum(dc * sh, axis=0, keepdims=True)
            db_ref[0:1, :] += jnp.sum(dc, axis=0, keepdims=True)
            return c

        lax.fori_loop(0, NCH, p1, 0)

        def p2(i, c):
            r = _rows(i, CH)
            dq = jnp.zeros((CH, GW), F32)
            for k, sh in _anti_taps(dpad, i, CH, H, K):
                dq = dq + w_ref[k:k + 1, :] * sh
            dz_ref[0, r, :] = (dq * z_ref[2, r, :].astype(F32)).astype(CDT)
            dz_ref[2, r, :] = (dq * z_ref[0, r, :].astype(F32)).astype(CDT)
            return c

        lax.fori_loop(0, NCH, p2, 0)
        dw_ref[...] = dw_s[0:K, :]

    return _call(
        body, "mixA_bwd", (MW // GW,),
        [pl.BlockSpec((3, S, GW), lambda c: (0, 0, c)), pl.BlockSpec((None, S, GW), lambda c: (0, 0, c)),
         pl.BlockSpec((K, GW), lambda c: (0, c)), _spspec("conv_a_b", GW, lambda c: c)],
        [pl.BlockSpec((3, S, GW), lambda c: (0, 0, c)), pl.BlockSpec((K, GW), lambda c: (0, c)),
         pl.BlockSpec((8, GW), lambda c: (0, c))],
        [SDS((3, S, MW), CDT), SDS((K, MW), F32), SDS((8, MW), F32)],
        [pltpu.VMEM((H + S, GW), F32), pltpu.VMEM((S + H, GW), F32), pltpu.VMEM((8 * ((K + 7) // 8), GW), F32)])(z, dm, cw, sp)


def _mixD_conv_fwd(z, cw, sp):
    S = z.shape[1]
    K = cw.shape[0]
    H, CH, NCH = _conv_geometry(S, K)

    def body(z_ref, w_ref, b_ref, o_ref, pad):
        pad[0:H, :] = jnp.zeros((H, GW), F32)

        def fill(i, c):
            r = _rows(i, CH)
            pad[pl.ds(pl.multiple_of(i * CH + H, 8), CH), :] = (
                z_ref[0, r, :].astype(F32) * jax.nn.sigmoid(z_ref[1, r, :].astype(F32)))
            return c

        lax.fori_loop(0, NCH, fill, 0)

        def conv(i, c):
            acc = jnp.zeros((CH, GW), F32)
            for k, sh in _causal_taps(pad, i, CH, H, K):
                acc = acc + w_ref[k:k + 1, :] * sh
            o_ref[_rows(i, CH), :] = (acc + b_ref[...]).astype(CDT)
            return c

        lax.fori_loop(0, NCH, conv, 0)

    return _call(
        body, "mixD_conv_fwd", (MW // GW,),
        [pl.BlockSpec((2, S, GW), lambda c: (3, 0, c)), pl.BlockSpec((K, GW), lambda c: (0, c)),
         _spspec("conv_d_b", GW, lambda c: c)],
        pl.BlockSpec((S, GW), lambda c: (0, c)), SDS((S, MW), CDT),
        [pltpu.VMEM((H + S, GW), F32)])(z, cw, sp)


def _mixD_conv_bwd(z, dy, cw):
    S = z.shape[1]
    K = cw.shape[0]
    H, CH, NCH = _conv_geometry(S, K)

    def body(z_ref, dy_ref, w_ref, dz_ref, dw_ref, db_ref, pad, dpad, dw_s):
        pad[0:H, :] = jnp.zeros((H, GW), F32)
        dpad[pl.ds(S, H), :] = jnp.zeros((H, GW), F32)
        dw_s[...] = jnp.zeros_like(dw_s)
        db_ref[...] = jnp.zeros_like(db_ref)

        def fill(i, c):
            r = _rows(i, CH)
            pad[pl.ds(pl.multiple_of(i * CH + H, 8), CH), :] = (
                z_ref[0, r, :].astype(F32) * jax.nn.sigmoid(z_ref[1, r, :].astype(F32)))
            dpad[r, :] = dy_ref[r, :].astype(F32)
            return c

        lax.fori_loop(0, NCH, fill, 0)

        def p1(i, c):
            dyf = dy_ref[_rows(i, CH), :].astype(F32)
            for k, sh in _causal_taps(pad, i, CH, H, K):
                dw_s[k:k + 1, :] += jnp.sum(dyf * sh, axis=0, keepdims=True)
            db_ref[0:1, :] += jnp.sum(dyf, axis=0, keepdims=True)
            return c

        lax.fori_loop(0, NCH, p1, 0)

        def p2(i, c):
            r = _rows(i, CH)
            dq = jnp.zeros((CH, GW), F32)
            for k, sh in _anti_taps(dpad, i, CH, H, K):
                dq = dq + w_ref[k:k + 1, :] * sh
            a = z_ref[0, r, :].astype(F32)
            sg = jax.nn.sigmoid(z_ref[1, r, :].astype(F32))
            dz_ref[0, r, :] = (dq * sg).astype(CDT)
            dz_ref[1, r, :] = (dq * a * sg * (1.0 - sg)).astype(CDT)
            return c

        lax.fori_loop(0, NCH, p2, 0)
        dw_ref[...] = dw_s[0:K, :]

    return _call(
        body, "mixD_conv_bwd", (MW // GW,),
        [pl.BlockSpec((2, S, GW), lambda c: (3, 0, c)), pl.BlockSpec((S, GW), lambda c: (0, c)),
         pl.BlockSpec((K, GW), lambda c: (0, c))],
        [pl.BlockSpec((2, S, GW), lambda c: (0, 0, c)), pl.BlockSpec((K, GW), lambda c: (0, c)),
         pl.BlockSpec((8, GW), lambda c: (0, c))],
        [SDS((2, S, MW), CDT), SDS((K, MW), F32), SDS((8, MW), F32)],
        [pltpu.VMEM((H + S, GW), F32), pltpu.VMEM((S + H, GW), F32), pltpu.VMEM((8 * ((K + 7) // 8), GW), F32)])(z, dy, cw)


def _mixD_ln_bwd(dm, yd, sp):
    S = yd.shape[0]
    TM = min(512, S)

    def body(dm_ref, y_ref, lg_ref, lb_ref, dy_ref, gp_ref):
        @pl.when(pl.program_id(0) == 0)
        def _():
            gp_ref[...] = jnp.zeros_like(gp_ref)

        xh, rs = _ln(y_ref[...].astype(F32))
        _, dsl = _silu_parts(xh * lg_ref[...] + lb_ref[...])
        dl = dm_ref[...].astype(F32) * dsl
        gp_ref[0:1, :] += jnp.sum(dl * xh, axis=0, keepdims=True)
        gp_ref[1:2, :] += jnp.sum(dl, axis=0, keepdims=True)
        dy_ref[...] = _ln_bwd(xh, rs, dl * lg_ref[...]).astype(CDT)

    row = lambda i: (i, 0)
    return _call(
        body, "mixD_ln_bwd", (S // TM,),
        [pl.BlockSpec((None, TM, MW), lambda i: (3, i, 0)), pl.BlockSpec((TM, MW), row), _spspec("conv_d_ln_g", MW, _zero),
         _spspec("conv_d_ln_b", MW, _zero)],
        [pl.BlockSpec((TM, MW), row), pl.BlockSpec((8, MW), lambda i: (0, 0))],
        [SDS((S, MW), CDT), SDS((8, MW), F32)])(dm, yd, sp, sp)


def _box_causal(val, g):
    s = val
    for d in range(g + 1):
        s = s + pltpu.roll(s, 1 << d, 0)
    return s


def _box_anti(val, g):
    n = val.shape[0]
    s = val
    for d in range(g + 1):
        s = s + pltpu.roll(s, n - (1 << d), 0)
    return s


def _pool_count(i, ch, win):
    t = lax.broadcasted_iota(jnp.int32, (ch, GW), 0) + (i * ch + 1)
    return jnp.minimum(t, win).astype(F32)


def _mixB_fwd(z, wp, sp):
    S = z.shape[1]
    H, CH = 16, min(256, S)
    NCH = S // CH
    assert POOL_WINDOWS == tuple(2 << g for g in range(4))

    def body(p_ref, wp_ref, sc_ref, o_ref, pad):
        pad[0:H, :] = jnp.zeros((H, MW), F32)

        def fill(i, c):
            pad[pl.ds(pl.multiple_of(i * CH + H, 8), CH), :] = p_ref[_rows(i, CH), :].astype(F32)
            return c

        lax.fori_loop(0, NCH, fill, 0)

        def step(i, c):
            r = _rows(i, CH)
            for g in range(4):
                gs = slice(g * GW, (g + 1) * GW)
                val = pad[pl.ds(pl.multiple_of(i * CH, 8), CH + H), gs]
                pooled = _box_causal(val, g)[H:, :] / _pool_count(i, CH, POOL_WINDOWS[g]) - val[H:, :]
                mixed = _nn(pooled.astype(CDT), wp_ref[g].astype(CDT))
                o_ref[r, gs] = (mixed * sc_ref[:, gs]).astype(CDT)
            return c

        lax.fori_loop(0, NCH, step, 0)

    return _call(
        body, "mixB_fwd", (1,),
        [pl.BlockSpec((None, S, MW), lambda i: (3, 0, 0)), pl.BlockSpec((4, GW, GW), lambda i: (0, 0, 0)),
         _spspec("pool_scale", MW, _zero)],
        pl.BlockSpec((S, MW), lambda i: (0, 0)), SDS((S, MW), CDT),
        [pltpu.VMEM((H + S, MW), F32)])(z, wp, sp)


def _mixB_bwd(z, dm, wp, sp):
    S = z.shape[1]
    H, CH = 16, min(256, S)
    NCH = S // CH

    def body(p_ref, dm_ref, wp_ref, sc_ref, dz_ref, dwp_ref, dsc_ref, pad, rpad):
        pad[0:H, :] = jnp.zeros((H, MW), F32)
        rpad[pl.ds(S, H), :] = jnp.zeros((H, MW), F32)
        dwp_ref[...] = jnp.zeros_like(dwp_ref)
        dsc_ref[...] = jnp.zeros_like(dsc_ref)

        def fill(i, c):
            pad[pl.ds(pl.multiple_of(i * CH + H, 8), CH), :] = p_ref[_rows(i, CH), :].astype(F32)
            return c

        lax.fori_loop(0, NCH, fill, 0)

        def p1(i, c):
            r = _rows(i, CH)
            for g in range(4):
                gs = slice(g * GW, (g + 1) * GW)
                cnt = _pool_count(i, CH, POOL_WINDOWS[g])
                val = pad[pl.ds(pl.multiple_of(i * CH, 8), CH + H), gs]
                pooled = (_box_causal(val, g)[H:, :] / cnt - val[H:, :]).astype(CDT)
                w = wp_ref[g].astype(CDT)
                mixed = _nn(pooled, w)
                dmf = dm_ref[r, gs].astype(F32)
                dsc_ref[0:1, gs] += jnp.sum(dmf * mixed, axis=0, keepdims=True)
                dmx = (dmf * sc_ref[:, gs]).astype(CDT)
                dwp_ref[g] += _tn(pooled, dmx)
                rpad[r, gs] = _nt(dmx, w) / cnt
            return c

        lax.fori_loop(0, NCH, p1, 0)

        def p2(i, c):
            r = _rows(i, CH)
            for g in range(4):
                gs = slice(g * GW, (g + 1) * GW)
                val = rpad[pl.ds(pl.multiple_of(i * CH, 8), CH + H), gs]
                dp = _box_anti(val, g)[:CH, :] - val[:CH, :] * _pool_count(i, CH, POOL_WINDOWS[g])
                dz_ref[r, gs] = dp.astype(CDT)
            return c

        lax.fori_loop(0, NCH, p2, 0)

    return _call(
        body, "mixB_bwd", (1,),
        [pl.BlockSpec((None, S, MW), lambda i: (3, 0, 0)), pl.BlockSpec((None, S, MW), lambda i: (1, 0, 0)),
         pl.BlockSpec((4, GW, GW), lambda i: (0, 0, 0)), _spspec("pool_scale", MW, _zero)],
        [pl.BlockSpec((None, S, MW), lambda i: (0, 0, 0)), pl.BlockSpec((4, GW, GW), lambda i: (0, 0, 0)),
         pl.BlockSpec((8, MW), lambda i: (0, 0))],
        [SDS((1, S, MW), CDT), SDS((4, GW, GW), F32), SDS((8, MW), F32)],
        [pltpu.VMEM((H + S, MW), F32), pltpu.VMEM((S + H, MW), F32)])(z, dm, wp, sp)


def _sgu_mask():
    ci = lax.broadcasted_iota(jnp.int32, (GW, GW), 0) // CHUNK
    cj = lax.broadcasted_iota(jnp.int32, (GW, GW), 1) // CHUNK
    return cj <= ci


def _mixC_fwd(z, ws, bsc, sp):
    S = z.shape[1]
    RB = min(512, S)

    def body(z_ref, lg_ref, lb_ref, ws_ref, bs_ref, o_ref):
        mask = _sgu_mask()
        gu = _gelu(z_ref[0].astype(F32))
        xh, _ = _ln(_gelu(z_ref[1].astype(F32)))
        vn = (xh * lg_ref[...] + lb_ref[...]).astype(CDT)
        for g in range(4):
            gs = slice(g * GW, (g + 1) * GW)
            wm = jnp.where(mask, ws_ref[g], 0.0).astype(CDT)
            for nb in range(RB // GW):
                rs = slice(nb * GW, (nb + 1) * GW)
                mixed = _nn(wm, vn[rs, gs]) + bs_ref[g]
                o_ref[rs, gs] = (gu[rs, gs] * mixed).astype(CDT)

    return _call(
        body, "mixC_fwd", (S // RB,),
        [pl.BlockSpec((2, RB, MW), lambda i: (2, i, 0)), _spspec("sgu_ln_g", MW, _zero), _spspec("sgu_ln_b", MW, _zero),
         pl.BlockSpec((4, GW, GW), lambda i: (0, 0, 0)), pl.BlockSpec((4, GW, 1), lambda i: (0, 0, 0))],
        pl.BlockSpec((RB, MW), lambda i: (i, 0)), SDS((S, MW), CDT))(z, sp, sp, ws, bsc)


def _mixC_bwd(z, dm, ws, bsc, sp):
    S = z.shape[1]
    RB = min(512, S)
    NR = S // RB

    def body(z_ref, dm_ref, lg_ref, lb_ref, ws_ref, bs_ref, dz_ref, dws_ref, dbs_ref, gp_ref, dvn_s):
        i = pl.program_id(0)

        @pl.when(i == 0)
        def _():
            dws_ref[...] = jnp.zeros_like(dws_ref)
            dbs_ref[...] = jnp.zeros_like(dbs_ref)
            gp_ref[...] = jnp.zeros_like(gp_ref)

        mask = _sgu_mask()
        gu, dgu = _gelu_parts(z_ref[0].astype(F32))
        gv, dgv = _gelu_parts(z_ref[1].astype(F32))
        xh, rs_ = _ln(gv)
        vn = (xh * lg_ref[...] + lb_ref[...]).astype(CDT)
        dmf = dm_ref[...].astype(F32)
        for g in range(4):
            gs = slice(g * GW, (g + 1) * GW)
            wm = jnp.where(mask, ws_ref[g], 0.0).astype(CDT)
            for nb in range(RB // GW):
                rs = slice(nb * GW, (nb + 1) * GW)
                vb = vn[rs, gs]
                mixed = _nn(wm, vb) + bs_ref[g]
                dz_ref[0, rs, gs] = (dmf[rs, gs] * mixed * dgu[rs, gs]).astype(CDT)
                dmx = dmf[rs, gs] * gu[rs, gs]
                dbs_ref[g] += dmx
                dmxc = dmx.astype(CDT)
                dws_ref[g] += _nt(dmxc, vb)
                dvn_s[rs, gs] = _tn(wm, dmxc)
        dvn = dvn_s[...]
        gp_ref[0:1, :] += jnp.sum(dvn * xh, axis=0, keepdims=True)
        gp_ref[1:2, :] += jnp.sum(dvn, axis=0, keepdims=True)
        dz_ref[1] = (_ln_bwd(xh, rs_, dvn * lg_ref[...]) * dgv).astype(CDT)

        @pl.when(i == NR - 1)
        def _():
            for g in range(4):
                dws_ref[g] = jnp.where(mask, dws_ref[g], 0.0)
                dbs_ref[g] = jnp.broadcast_to(jnp.sum(dbs_ref[g], axis=1, keepdims=True), (GW, GW))

    full3 = lambda i: (0, 0, 0)
    return _call(
        body, "mixC_bwd", (NR,),
        [pl.BlockSpec((2, RB, MW), lambda i: (2, i, 0)), pl.BlockSpec((None, RB, MW), lambda i: (2, i, 0)),
         _spspec("sgu_ln_g", MW, _zero), _spspec("sgu_ln_b", MW, _zero),
         pl.BlockSpec((4, GW, GW), full3), pl.BlockSpec((4, GW, 1), full3)],
        [pl.BlockSpec((2, RB, MW), lambda i: (0, i, 0)), pl.BlockSpec((4, GW, GW), full3), pl.BlockSpec((4, GW, GW), full3),
         pl.BlockSpec((8, MW), lambda i: (0, 0))],
        [SDS((2, S, MW), CDT), SDS((4, GW, GW), F32), SDS((4, GW, GW), F32), SDS((8, MW), F32)],
        [pltpu.VMEM((RB, MW), F32)])(z, dm, sp, sp, ws, bsc)


def _unpack_wb(wb_ref, wbf):
    for j in range(NS):
        for k in range(4):
            wbf[k, :, j * GW:(j + 1) * GW] = wb_ref[j, k]


def _merge_fwd(ma, mb, mc, yd, g, wb, pwo, x, sp):
    S, D = x.shape
    TM = min(256, S)

    def body(ma_ref, mb_ref, mc_ref, yd_ref, g_ref, wb_ref, wo_ref, x_ref, lg_ref, lb_ref, qg_ref,
             xo_ref, md_ref, yk_ref, mg_ref, mo_ref, wbf):
        @pl.when(pl.program_id(0) == 0)
        def _():
            _unpack_wb(wb_ref, wbf)

        xh, _ = _ln(yd_ref[...].astype(F32))
        sl, _ = _silu_parts(xh * lg_ref[...] + lb_ref[...])
        md = sl.astype(CDT)
        md_ref[...] = md
        merged = jnp.zeros((TM, D), F32)
        for k, m in enumerate((ma_ref[...], mb_ref[...], mc_ref[...], md)):
            yk = _nn(m, wbf[k])
            yk_ref[k] = yk.astype(CDT)
            merged = merged + g_ref[k].astype(F32) * yk
        mgc = merged.astype(CDT)
        mg_ref[...] = mgc
        mo = _nn(mgc, wo_ref[...].reshape(D, D))
        mo_ref[...] = mo.astype(CDT)
        n, _ = _rms(mo)
        xo_ref[...] = x_ref[...] + n * qg_ref[...]

    row = lambda i: (i, 0)
    rowm = pl.BlockSpec((TM, MW), row)
    rowd = pl.BlockSpec((TM, D), row)
    row4 = pl.BlockSpec((4, TM, D), lambda i: (0, i, 0))
    return _call(
        body, "merge_fwd", (S // TM,),
        [rowm, rowm, rowm, rowm, row4, pl.BlockSpec((NS, 4, MW, GW), lambda i: (0, 0, 0, 0)),
         pl.BlockSpec((NS, GW, D), lambda i: (0, 0, 0)), rowd,
         _spspec("conv_d_ln_g", MW, _zero), _spspec("conv_d_ln_b", MW, _zero), _spspec("mix_post_g", D, _zero)],
        [rowd, rowm, row4, rowd, rowd],
        [SDS((S, D), F32), SDS((S, MW), CDT), SDS((4, S, D), CDT), SDS((S, D), CDT), SDS((S, D), CDT)],
        [pltpu.VMEM((4, MW, D), CDT)])(ma, mb, mc, yd, g, wb, pwo, x, sp, sp, sp)


def _merge_bwd_act(dxo, mo, g, yk, wb, pwo, sp, dep):
    S, D = dxo.shape
    TM = min(256, S)

    def body(dxo_ref, mo_ref, g_ref, yk_ref, wb_ref, wo_ref, qg_ref, dep_ref, dmo_ref, dm_ref, dgp_ref, dyk_ref, gp_ref, wbf):
        @pl.when(pl.program_id(0) == 0)
        def _():
            gp_ref[...] = jnp.zeros_like(gp_ref)
            _unpack_wb(wb_ref, wbf)

        n, r = _rms(mo_ref[...].astype(F32))
        dmo, dg = _rms_bwd(n, r, qg_ref[...], dxo_ref[...])
        gp_ref[0:1, :] += dg
        dmoc = dmo.astype(CDT)
        dmo_ref[...] = dmoc
        dmg = _nt(dmoc, wo_ref[...].reshape(D, D))
        for k in range(4):
            gk = g_ref[k].astype(F32)
            dyk = (dmg * gk).astype(CDT)
            dyk_ref[k] = dyk
            dgp_ref[k] = (dmg * yk_ref[k].astype(F32) * gk * (1.0 - gk)).astype(CDT)
            dm_ref[k] = _nt(dyk, wbf[k]).astype(CDT)

    rowd = pl.BlockSpec((TM, D), lambda i: (i, 0))
    row4 = pl.BlockSpec((4, TM, D), lambda i: (0, i, 0))
    return _call(
        body, "merge_bwd_act", (S // TM,),
        [rowd, rowd, row4, row4, pl.BlockSpec((NS, 4, MW, GW), lambda i: (0, 0, 0, 0)),
         pl.BlockSpec((NS, GW, D), lambda i: (0, 0, 0)), _spspec("mix_post_g", D, _zero), _ANY],
        [rowd, pl.BlockSpec((4, TM, MW), lambda i: (0, i, 0)), row4, row4, pl.BlockSpec((8, D), lambda i: (0, 0))],
        [SDS((S, D), CDT), SDS((4, S, MW), CDT), SDS((4, S, D), CDT), SDS((4, S, D), CDT), SDS((8, D), F32)],
        [pltpu.VMEM((4, MW, D), CDT)])(dxo, mo, g, yk, wb, pwo, sp, dep)


def _merge_bwd_w(ma, mb, mc, md, dyk, mg, dmo):
    S, D = dmo.shape
    TK = min(512, S)
    NK = S // TK

    def body(ma_ref, mb_ref, mc_ref, md_ref, dyk_ref, mg_ref, dmo_ref, gwb_ref, gwo_ref, accb, acco):
        k = pl.program_id(0)

        @pl.when(k == 0)
        def _():
            accb[...] = jnp.zeros_like(accb)
            acco[...] = jnp.zeros_like(acco)

        for b, m in enumerate((ma_ref, mb_ref, mc_ref, md_ref)):
            accb[b] += _tn(m[...], dyk_ref[b])
        acco[...] += _tn(mg_ref[...], dmo_ref[...])

        @pl.when(k == NK - 1)
        def _():
            for j in range(NS):
                for b in range(4):
                    gwb_ref[j, b] = accb[b, :, j * GW:(j + 1) * GW].astype(CDT)
                gwo_ref[j] = acco[j * GW:(j + 1) * GW, :].astype(CDT)

    rowm = pl.BlockSpec((TK, MW), lambda k: (k, 0))
    rowd = pl.BlockSpec((TK, D), lambda k: (k, 0))
    return _call(
        body, "merge_bwd_w", (NK,),
        [rowm, rowm, rowm, rowm, pl.BlockSpec((4, TK, D), lambda k: (0, k, 0)), rowd, rowd],
        [pl.BlockSpec((NS, 4, MW, GW), lambda k: (0, 0, 0, 0)), pl.BlockSpec((NS, GW, D), lambda k: (0, 0, 0))],
        [SDS((NS, 4, MW, GW), CDT), SDS((NS, GW, D), CDT)],
        [pltpu.VMEM((4, MW, D), F32), pltpu.VMEM((D, D), F32)])(ma, mb, mc, md, dyk, mg, dmo)


def _xa_kv(mem, sp, pxa):
    M, D = mem.shape

    def body(m_ref, g_ref, wk_ref, wv_ref, mn_ref, k_ref, v_ref):
        n, _ = _rms(m_ref[...])
        mn = (n * g_ref[...]).astype(CDT)
        mn_ref[...] = mn
        k_ref[...] = _nn(mn, wk_ref[...].reshape(D, D)).astype(CDT)
        v_ref[...] = _nn(mn, wv_ref[...].reshape(D, D)).astype(CDT)

    full = pl.BlockSpec((M, D), lambda i: (0, 0))
    return _call(
        body, "xa_kv", (1,),
        [full, _spspec("mem_g", D, _zero), pl.BlockSpec((NS, GW, D), lambda i: (0, 1, 0)),
         pl.BlockSpec((NS, GW, D), lambda i: (0, 2, 0))],
        [full, full, full], [SDS((M, D), CDT)] * 3)(mem, sp, pxa, pxa)


def _softmax(s):
    e = jnp.exp(s - jnp.max(s, axis=-1, keepdims=True))
    return e / jnp.sum(e, axis=-1, keepdims=True)


def _xa_fwd(x, kk, vv, sp, pxa):
    S, D = x.shape
    M = kk.shape[0]
    TM = min(512, S)
    HD = D // XA_HEADS
    scale = HD ** -0.5

    def body(x_ref, k_ref, v_ref, pg_ref, qg_ref, wq_ref, wo_ref, xo_ref, hb_ref, q_ref, o_ref, po_ref):
        n, _ = _rms(x_ref[...])
        hb = (n * pg_ref[...]).astype(CDT)
        hb_ref[...] = hb
        q = _nn(hb, wq_ref[...].reshape(D, D)).astype(CDT)
        q_ref[...] = q
        for h in range(XA_HEADS):
            hs = slice(h * HD, (h + 1) * HD)
            p = _softmax(_nt(q[:, hs], k_ref[:, hs]) * scale)
            o_ref[:, hs] = _nn(p.astype(CDT), v_ref[:, hs]).astype(CDT)
        po = _nn(o_ref[...], wo_ref[...].reshape(D, D))
        po_ref[...] = po.astype(CDT)
        n, _ = _rms(po)
        xo_ref[...] = x_ref[...] + n * qg_ref[...]

    row = pl.BlockSpec((TM, D), lambda i: (i, 0))
    full = pl.BlockSpec((M, D), lambda i: (0, 0))
    return _call(
        body, "xa_fwd", (S // TM,),
        [row, full, full, _spspec("xa_pre_g", D, _zero), _spspec("xa_post_g", D, _zero),
         pl.BlockSpec((NS, GW, D), lambda i: (0, 0, 0)), pl.BlockSpec((NS, GW, D), lambda i: (0, 3, 0))],
        [row] * 5, [SDS((S, D), F32)] + [SDS((S, D), CDT)] * 4)(x, kk, vv, sp, sp, pxa, pxa)


def _xa_bwd_act(dxo, x, po, q, kk, vv, sp, pxa, dep):
    S, D = x.shape
    M = kk.shape[0]
    TM = min(512, S)
    HD = D // XA_HEADS
    scale = HD ** -0.5

    def body(dxo_ref, x_ref, po_ref, q_ref, k_ref, v_ref, pg_ref, qg_ref, wq_ref, wo_ref, dep_ref,
             dx_ref, dpo_ref, dq_ref, dk_ref, dv_ref, gp_ref):
        @pl.when(pl.program_id(0) == 0)
        def _():
            gp_ref[...] = jnp.zeros_like(gp_ref)
            dk_ref[...] = jnp.zeros_like(dk_ref)
            dv_ref[...] = jnp.zeros_like(dv_ref)

        n, r = _rms(po_ref[...].astype(F32))
        dpo, dg = _rms_bwd(n, r, qg_ref[...], dxo_ref[...])
        gp_ref[1:2, :] += dg
        dpoc = dpo.astype(CDT)
        dpo_ref[...] = dpoc
        do = _nt(dpoc, wo_ref[...].reshape(D, D)).astype(CDT)
        for h in range(XA_HEADS):
            hs = slice(h * HD, (h + 1) * HD)
            qh = q_ref[:, hs]
            p = _softmax(_nt(qh, k_ref[:, hs]) * scale)
            pc = p.astype(CDT)
            dv_ref[:, hs] += _tn(pc, do[:, hs])
            dp = _nt(do[:, hs], v_ref[:, hs])
            ds = (p * (dp - jnp.sum(p * dp, axis=-1, keepdims=True)) * scale).astype(CDT)
            dq_ref[:, hs] = _nn(ds, k_ref[:, hs]).astype(CDT)
            dk_ref[:, hs] += _tn(ds, qh)
        dhb = _nt(dq_ref[...], wq_ref[...].reshape(D, D))
        n, r = _rms(x_ref[...])
        dx, dg = _rms_bwd(n, r, pg_ref[...], dhb)
        dx_ref[...] = dxo_ref[...] + dx
        gp_ref[0:1, :] += dg

    row = pl.BlockSpec((TM, D), lambda i: (i, 0))
    full = pl.BlockSpec((M, D), lambda i: (0, 0))
    return _call(
        body, "xa_bwd_act", (S // TM,),
        [row, row, row, row, full, full, _spspec("xa_pre_g", D, _zero), _spspec("xa_post_g", D, _zero),
         pl.BlockSpec((NS, GW, D), lambda i: (0, 0, 0)), pl.BlockSpec((NS, GW, D), lambda i: (0, 3, 0)), _ANY],
        [row, row, row, full, full, pl.BlockSpec((8, D), lambda i: (0, 0))],
        [SDS((S, D), F32), SDS((S, D), CDT), SDS((S, D), CDT), SDS((M, D), F32), SDS((M, D), F32), SDS((8, D), F32)],
    )(dxo, x, po, q, kk, vv, sp, sp, pxa, pxa, dep)


def _xa_bwd_w(hb, dq, o, dpo, mn, dk, dv):
    S, D = hb.shape
    M = mn.shape[0]
    TK = min(512, S)
    NK = S // TK

    def body(hb_ref, dq_ref, o_ref, dpo_ref, mn_ref, dk_ref, dv_ref, g_ref, accq, acco):
        k = pl.program_id(0)

        @pl.when(k == 0)
        def _():
            accq[...] = jnp.zeros_like(accq)
            acco[...] = jnp.zeros_like(acco)

        accq[...] += _tn(hb_ref[...], dq_ref[...])
        acco[...] += _tn(o_ref[...], dpo_ref[...])

        @pl.when(k == NK - 1)
        def _():
            gk = _tn(mn_ref[...], dk_ref[...].astype(CDT))
            gv = _tn(mn_ref[...], dv_ref[...].astype(CDT))
            for j in range(NS):
                rs = slice(j * GW, (j + 1) * GW)
                g_ref[j, 0:GW, :] = accq[rs, :].astype(CDT)
                g_ref[j, GW:2 * GW, :] = gk[rs, :].astype(CDT)
                g_ref[j, 2 * GW:3 * GW, :] = gv[rs, :].astype(CDT)
                g_ref[j, 3 * GW:4 * GW, :] = acco[rs, :].astype(CDT)

    rowb = pl.BlockSpec((TK, D), lambda k: (k, 0))
    full = pl.BlockSpec((M, D), lambda k: (0, 0))
    return _call(
        body, "xa_bwd_w", (NK,),
        [rowb, rowb, rowb, rowb, full, full, full],
        pl.BlockSpec((NS, 4 * GW, D), lambda k: (0, 0, 0)), SDS((NS, 4 * GW, D), CDT),
        [pltpu.VMEM((D, D), F32), pltpu.VMEM((D, D), F32)])(hb, dq, o, dpo, mn, dk, dv)


def _xa_kv_bwd(mem, dk, dv, sp, pxa):
    M, D = mem.shape

    def body(m_ref, dk_ref, dv_ref, wk_ref, wv_ref, gp_ref):
        dmn = _nt(dk_ref[...].astype(CDT), wk_ref[...].reshape(D, D)) + _nt(dv_ref[...].astype(CDT), wv_ref[...].reshape(D, D))
        n, _ = _rms(m_ref[...])
        gp_ref[...] = jnp.zeros_like(gp_ref)
        gp_ref[0:1, :] = jnp.sum(dmn * n, axis=0, keepdims=True)

    full = pl.BlockSpec((M, D), lambda i: (0, 0))
    return _call(
        body, "xa_kv_bwd", (1,),
        [full, full, full, pl.BlockSpec((NS, GW, D), lambda i: (0, 1, 0)), pl.BlockSpec((NS, GW, D), lambda i: (0, 2, 0))],
        pl.BlockSpec((8, D), lambda i: (0, 0)), SDS((8, D), F32))(mem, dk, dv, pxa, pxa)


def _loss_head(y, t):
    S, D = y.shape
    TM = min(512, S)

    def body(y_ref, t_ref, dy_ref, l_ref):
        @pl.when(pl.program_id(0) == 0)
        def _():
            l_ref[...] = jnp.zeros_like(l_ref)

        e = y_ref[...] - t_ref[...]
        dy_ref[...] = e * (1.0 / D)
        l_ref[...] += 0.5 * jnp.sum(jnp.mean(e * e, axis=-1, keepdims=True), axis=0, keepdims=True)

    row = pl.BlockSpec((TM, D), lambda i: (i, 0))
    return _call(body, "loss_head", (S // TM,), [row, row], [row, pl.BlockSpec((8, 128), lambda i: (0, 0))],
                 [SDS((S, D), F32), SDS((8, 128), F32)])(y, t)


def _row_tile(rows, cols, limit=1 << 18, step=8):
    if rows * cols <= limit or rows % step:
        return rows
    best = step
    for t in range(step, rows + 1, step):
        if rows % t == 0 and t * cols <= limit:
            best = t
    return best


def _slot_sum(r):
    _, R, C = r.shape
    TR = _row_tile(R, C * NS, limit=1 << 21, step=16)

    def body(r_ref, o_ref):
        acc = r_ref[0].astype(F32)
        for j in range(1, NS):
            acc = acc + r_ref[j].astype(F32)
        o_ref[...] = acc

    return _call(body, "slot_sum", (R // TR,), [pl.BlockSpec((NS, TR, C), lambda i: (0, i, 0))],
                 pl.BlockSpec((TR, C), lambda i: (i, 0)), SDS((R, C), F32))(r)


def _adamw(w, g, m, v):
    shape = w.shape
    C = shape[-1]
    R = w.size // C
    TR = _row_tile(R, C)
    c1 = 1.0 - ADAM_B1 ** ADAM_STEP
    c2 = 1.0 - ADAM_B2 ** ADAM_STEP

    def body(w_ref, g_ref, m_ref, v_ref, d_ref, nm_ref, nv_ref):
        gg = g_ref[...]
        nm = ADAM_B1 * m_ref[...] + (1.0 - ADAM_B1) * gg
        nv = ADAM_B2 * v_ref[...] + (1.0 - ADAM_B2) * (gg * gg)
        nm_ref[...] = nm
        nv_ref[...] = nv
        d_ref[...] = -ADAM_LR * ((nm / c1) / (jnp.sqrt(nv / c2) + ADAM_EPS) + ADAM_WD * w_ref[...])

    blk = pl.BlockSpec((TR, C), lambda i: (i, 0))
    outs = _call(body, "adamw", (R // TR,), [blk] * 4, [blk] * 3, [SDS((R, C), F32)] * 3)(
        w.reshape(R, C), g.reshape(R, C), m.reshape(R, C), v.reshape(R, C))
    return tuple(o.reshape(shape) for o in outs)


def _adamw_block(w, gs, m, v, gblock):
    L, R, C = w.shape
    c1 = 1.0 - ADAM_B1 ** ADAM_STEP
    c2 = 1.0 - ADAM_B2 ** ADAM_STEP

    def body(w_ref, g_ref, m_ref, v_ref, go_ref, d_ref, nm_ref, nv_ref):
        gg = g_ref[...]
        go_ref[...] = gg
        nm = ADAM_B1 * m_ref[...] + (1.0 - ADAM_B1) * gg
        nv = ADAM_B2 * v_ref[...] + (1.0 - ADAM_B2) * (gg * gg)
        nm_ref[...] = nm
        nv_ref[...] = nv
        d_ref[...] = -ADAM_LR * ((nm / c1) / (jnp.sqrt(nv / c2) + ADAM_EPS) + ADAM_WD * w_ref[...])

    blk = pl.BlockSpec((None, R, C), lambda l: (l, 0, 0))
    return _call(body, "adamw_block", (L,), [blk, pl.BlockSpec((None, R, C), lambda l: (l, gblock, 0)), blk, blk],
                 [blk] * 4, [SDS((L, R, C), F32)] * 4)(w, gs, m, v)


def _slot_sum_into(stacked, r, l):
    _, R, C = r.shape
    TR = _row_tile(R, C * NS, limit=1 << 21, step=16)

    def body(r_ref, s_ref, o_ref):
        acc = r_ref[0].astype(F32)
        for j in range(1, NS):
            acc = acc + r_ref[j].astype(F32)
        o_ref[...] = acc

    return pl.pallas_call(
        body, name="slot_sum_into", grid=(R // TR,),
        in_specs=[pl.BlockSpec((NS, TR, C), lambda i: (0, i, 0)), _ANY],
        out_specs=pl.BlockSpec((None, TR, C), lambda i: (l, i, 0)), out_shape=SDS(stacked.shape, F32),
        input_output_aliases={1: 0},
        compiler_params=pltpu.CompilerParams(dimension_semantics=("arbitrary",), vmem_limit_bytes=VMEM_LIMIT))(r, stacked)


def _exchange(arrs, scatter, name):
    n = len(arrs)
    np_ = NS - 1

    def body(*refs):
        ins, outs = refs[:n], refs[n:2 * n]
        send_sems, recv_sems, loc_sems = refs[2 * n:]
        x, y, c = lax.axis_index("x"), lax.axis_index("y"), lax.axis_index("c")
        me = 4 * x + 2 * y + c
        peers = []
        for f in range(1, NS):
            px = 1 - x if f & 4 else x
            py = 1 - y if f & 2 else y
            pc = 1 - c if f & 1 else c
            peers.append(((px, py, pc), 4 * px + 2 * py + pc))

        def src(a, pid):
            return ins[a].at[pid] if scatter else ins[a]

        local = [pltpu.make_async_copy(src(a, me), outs[a].at[me], loc_sems.at[a]) for a in range(n)]
        for cp in local:
            cp.start()
        sends = []
        for a in range(n):
            for f, (dev, pid) in enumerate(peers):
                sends.append(pltpu.make_async_remote_copy(
                    src_ref=src(a, pid), dst_ref=outs[a].at[me], send_sem=send_sems.at[a * np_ + f],
                    recv_sem=recv_sems.at[a * np_ + f], device_id=dev, device_id_type=pl.DeviceIdType.MESH))
        for cp in sends:
            cp.start()
        for a in range(n):
            for f, (dev, pid) in enumerate(peers):
                pltpu.make_async_remote_copy(
                    src_ref=src(a, pid), dst_ref=outs[a].at[pid], send_sem=send_sems.at[a * np_ + f],
                    recv_sem=recv_sems.at[a * np_ + f], device_id=dev, device_id_type=pl.DeviceIdType.MESH).wait_recv()
        for cp in sends:
            cp.wait_send()
        for cp in local:
            cp.wait()

    out_shape = [SDS(a.shape if scatter else (NS,) + a.shape, a.dtype) for a in arrs]
    anyspec = pl.BlockSpec(memory_space=pl.ANY)
    outs = pl.pallas_call(
        body, name=name, in_specs=[anyspec] * n, out_specs=[anyspec] * n, out_shape=out_shape,
        scratch_shapes=[pltpu.SemaphoreType.DMA((n * np_,)), pltpu.SemaphoreType.DMA((n * np_,)),
                        pltpu.SemaphoreType.DMA((n,))],
        compiler_params=pltpu.CompilerParams(has_side_effects=True))(*arrs)
    return list(outs)


def _peers():
    x, y, c = lax.axis_index("x"), lax.axis_index("y"), lax.axis_index("c")
    out = []
    for f in range(1, NS):
        px = 1 - x if f & 4 else x
        py = 1 - y if f & 2 else y
        pc = 1 - c if f & 1 else c
        out.append(((px, py, pc), 4 * px + 2 * py + pc))
    return 4 * x + 2 * y + c, out


def _exchange_copies(ins, lands, scatter, send_sems, recv_sems, loc_sems):
    me, peers = _peers()
    np_ = NS - 1

    def src(a, pid):
        return ins[a].at[pid] if scatter[a] else ins[a]

    def rcopy(a, f, dev, land_slot):
        return pltpu.make_async_remote_copy(
            src_ref=src(a, peers[f][1]), dst_ref=lands[a].at[land_slot], send_sem=send_sems.at[a * np_ + f],
            recv_sem=recv_sems.at[a * np_ + f], device_id=dev, device_id_type=pl.DeviceIdType.MESH)

    local = [pltpu.make_async_copy(src(a, me), lands[a].at[me], loc_sems.at[a]) for a in range(len(ins))]
    sends = [rcopy(a, f, dev, me) for a in range(len(ins)) for f, (dev, _) in enumerate(peers)]
    arrivals = [rcopy(a, f, dev, pid) for a in range(len(ins)) for f, (dev, pid) in enumerate(peers)]
    return local, sends, arrivals


_HBM = pl.BlockSpec(memory_space=pltpu.HBM)
_SEM = pl.BlockSpec(memory_space=pltpu.SEMAPHORE)
_ANY = pl.BlockSpec(memory_space=pl.ANY)


def _exchange_start(arrs, scatter, after, name):
    n = len(arrs)
    np_ = NS - 1
    lands = [lax.empty(a.shape if sc else (NS,) + a.shape, a.dtype) for a, sc in zip(arrs, scatter)]

    def body(*refs):
        ins, lnd = refs[:n], refs[n:2 * n]
        send_sems, recv_sems, loc_sems = refs[2 * n + 1:2 * n + 4]
        token = refs[-1]
        local, sends, _ = _exchange_copies(ins, lnd, scatter, send_sems, recv_sems, loc_sems)
        for cp in local + sends:
            cp.start()
        token[...] = jnp.zeros_like(token)

    hbm = lambda a: pltpu.HBM(a.shape, a.dtype)
    outs = pl.pallas_call(
        body, name=name,
        out_shape=(pltpu.SemaphoreType.DMA((n * np_,)), pltpu.SemaphoreType.DMA((n * np_,)), pltpu.SemaphoreType.DMA((n,)),
                   *[hbm(a) for a in arrs], *[hbm(a) for a in lands], SDS((8, 128), F32)),
        in_specs=[_HBM] * (2 * n) + [_ANY],
        out_specs=(_SEM, _SEM, _SEM, *([_HBM] * (2 * n)), pl.BlockSpec(memory_space=pltpu.VMEM)),
        input_output_aliases={i: 3 + i for i in range(2 * n)},
        compiler_params=pltpu.CompilerParams(has_side_effects=pltpu.SideEffectType.DATAFLOW_SIDE_EFFECTING),
    )(*[pltpu.with_memory_space_constraint(a, pltpu.HBM) for a in list(arrs) + lands], after)
    return {"sems": outs[:3], "ins": outs[3:3 + n], "lands": outs[3 + n:3 + 2 * n], "token": outs[-1], "scatter": scatter}


def _exchange_wait(h, after, name):
    n = len(h["ins"])
    scatter = h["scatter"]
    after = list(after) if isinstance(after, (list, tuple)) else [after]

    def body(*refs):
        ins, lnd = refs[:n], refs[n:2 * n]
        send_sems, recv_sems, loc_sems = refs[2 * n:2 * n + 3]
        local, sends, arrivals = _exchange_copies(ins, lnd, scatter, send_sems, recv_sems, loc_sems)
        for cp in sends:
            cp.wait_send()
        for cp in arrivals:
            cp.wait_recv()
        for cp in local:
            cp.wait()

    hbm = lambda a: pltpu.HBM(a.shape, a.dtype)
    outs = pl.pallas_call(
        body, name=name,
        out_shape=tuple(hbm(a) for a in list(h["ins"]) + list(h["lands"])),
        in_specs=[_HBM] * (2 * n) + [_SEM] * 3 + [_ANY] * len(after),
        out_specs=tuple([_HBM] * (2 * n)),
        input_output_aliases={i: i for i in range(2 * n)},
        compiler_params=pltpu.CompilerParams(has_side_effects=pltpu.SideEffectType.DATAFLOW_SIDE_EFFECTING),
    )(*h["ins"], *h["lands"], *h["sems"], *after)
    return list(outs[n:])


def _hgather_copies(ins, lands, send_a, recv_a, send_b, recv_b, loc_sems):
    x, y, c = lax.axis_index("x"), lax.axis_index("y"), lax.axis_index("c")
    me = 4 * x + 2 * y + c
    sib = (x, y, 1 - c)
    chips = [(1 - x, y), (x, 1 - y), (1 - x, 1 - y)]
    slot = lambda px, py, pc: 4 * px + 2 * py + pc

    def rcopy(src, dst, ssem, rsem, dev):
        return pltpu.make_async_remote_copy(src_ref=src, dst_ref=dst, send_sem=ssem, recv_sem=rsem, device_id=dev,
                                            device_id_type=pl.DeviceIdType.MESH)

    local, s1, a1, s2, a2 = [], [], [], [], []
    for a in range(len(ins)):
        first = [(sib, slot(x, y, 1 - c))] + [((px, py, c), slot(px, py, c)) for px, py in chips]
        for k, (dev, origin) in enumerate(first if send_a is not None else ()):
            s1.append(rcopy(ins[a], lands[a].at[me], send_a.at[4 * a + k], recv_a.at[4 * a + k], dev))
            a1.append(rcopy(ins[a], lands[a].at[origin], send_a.at[4 * a + k], recv_a.at[4 * a + k], dev))
        if send_a is not None:
            local.append(pltpu.make_async_copy(ins[a], lands[a].at[me], loc_sems.at[a]))
        for k, (px, py) in enumerate(chips if send_b is not None else ()):
            mine, theirs = lands[a].at[slot(px, py, c)], lands[a].at[slot(px, py, 1 - c)]
            s2.append(rcopy(mine, mine, send_b.at[3 * a + k], recv_b.at[3 * a + k], sib))
            a2.append(rcopy(mine, theirs, send_b.at[3 * a + k], recv_b.at[3 * a + k], sib))
    return local, s1, a1, s2, a2


def _hgather_start(arrs, after, name):
    n = len(arrs)
    lands = [lax.empty((NS,) + a.shape, a.dtype) for a in arrs]

    def body(*refs):
        ins, lnd = refs[:n], refs[n:2 * n]
        send_a, recv_a, loc_sems = refs[2 * n + 1:2 * n + 4]
        token = refs[-1]
        local, s1, _, _, _ = _hgather_copies(ins, lnd, send_a, recv_a, None, None, loc_sems)
        for cp in local + s1:
            cp.start()
        token[...] = jnp.zeros_like(token)

    hbm = lambda a: pltpu.HBM(a.shape, a.dtype)
    outs = pl.pallas_call(
        body, name=name,
        out_shape=(pltpu.SemaphoreType.DMA((4 * n,)), pltpu.SemaphoreType.DMA((4 * n,)), pltpu.SemaphoreType.DMA((n,)),
                   *[hbm(a) for a in arrs], *[hbm(a) for a in lands], SDS((8, 128), F32)),
        in_specs=[_HBM] * (2 * n) + [_ANY],
        out_specs=(_SEM, _SEM, _SEM, *([_HBM] * (2 * n)), pl.BlockSpec(memory_space=pltpu.VMEM)),
        input_output_aliases={i: 3 + i for i in range(2 * n)},
        compiler_params=pltpu.CompilerParams(has_side_effects=pltpu.SideEffectType.DATAFLOW_SIDE_EFFECTING),
    )(*[pltpu.with_memory_space_constraint(a, pltpu.HBM) for a in list(arrs) + lands], after)
    return {"sems": outs[:3], "ins": outs[3:3 + n], "lands": outs[3 + n:3 + 2 * n], "token": outs[-1]}


def _hgather_forward(h, after, name):
    n = len(h["ins"])

    def body(*refs):
        ins, lnd = refs[:n], refs[n:2 * n]
        send_a, recv_a, loc_sems = refs[2 * n:2 * n + 3]
        send_b, recv_b = refs[2 * n + 4:2 * n + 6]
        token = refs[-1]
        local, s1, a1, s2, _ = _hgather_copies(ins, lnd, send_a, recv_a, send_b, recv_b, loc_sems)
        for cp in s1:
            cp.wait_send()
        for cp in a1:
            cp.wait_recv()
        for cp in local:
            cp.wait()
        for cp in s2:
            cp.start()
        token[...] = jnp.zeros_like(token)

    hbm = lambda a: pltpu.HBM(a.shape, a.dtype)
    outs = pl.pallas_call(
        body, name=name,
        out_shape=(pltpu.SemaphoreType.DMA((3 * n,)), pltpu.SemaphoreType.DMA((3 * n,)),
                   *[hbm(a) for a in list(h["ins"]) + list(h["lands"])], SDS((8, 128), F32)),
        in_specs=[_HBM] * (2 * n) + [_SEM] * 3 + [_ANY],
        out_specs=(_SEM, _SEM, *([_HBM] * (2 * n)), pl.BlockSpec(memory_space=pltpu.VMEM)),
        input_output_aliases={i: 2 + i for i in range(2 * n)},
        compiler_params=pltpu.CompilerParams(has_side_effects=pltpu.SideEffectType.DATAFLOW_SIDE_EFFECTING),
    )(*h["ins"], *h["lands"], *h["sems"], after)
    return {"sems": outs[:2], "ins": outs[2:2 + n], "lands": outs[2 + n:2 + 2 * n], "token": outs[-1]}


def _hgather_wait(h, after, name):
    n = len(h["ins"])

    def body(*refs):
        ins, lnd = refs[:n], refs[n:2 * n]
        send_b, recv_b = refs[2 * n:2 * n + 2]
        _, _, _, s2, a2 = _hgather_copies(ins, lnd, None, None, send_b, recv_b, None)
        for cp in s2:
            cp.wait_send()
        for cp in a2:
            cp.wait_recv()

    hbm = lambda a: pltpu.HBM(a.shape, a.dtype)
    outs = pl.pallas_call(
        body, name=name,
        out_shape=tuple(hbm(a) for a in list(h["ins"]) + list(h["lands"])),
        in_specs=[_HBM] * (2 * n) + [_SEM] * 2 + [_ANY],
        out_specs=tuple([_HBM] * (2 * n)),
        input_output_aliases={i: i for i in range(2 * n)},
        compiler_params=pltpu.CompilerParams(has_side_effects=pltpu.SideEffectType.DATAFLOW_SIDE_EFFECTING),
    )(*h["ins"], *h["lands"], *h["sems"], after)
    return list(outs[n:])


_W_NAMES = ("ffn1_pre_g", "ffn1_post_g", "ffn1_w1", "ffn1_w3", "ffn1_w2", "mix_pre_g", "mix_post_g", "w_in", "conv_a_w",
            "conv_a_b", "pool_w", "pool_scale", "sgu_ln_g", "sgu_ln_b", "sgu_ws", "sgu_b", "conv_d_w", "conv_d_b",
            "conv_d_ln_g", "conv_d_ln_b", "w_branch", "w_gate", "b_gate", "w_o", "xa_pre_g", "xa_post_g", "mem_g",
            "xa_wq", "xa_wk", "xa_wv", "xa_wo", "ffn2_pre_g", "ffn2_post_g", "ffn2_w1", "ffn2_w3", "ffn2_w2")
_REP_NAMES = tuple(n for n, _ in _SP_NAMES) + ("pool_w", "sgu_ws", "sgu_b", "conv_a_w", "conv_d_w")


def _t(w):
    return jnp.swapaxes(w, -1, -2)


def _step(x, mem, loss_target, W, M, V):
    L = W["w_in"].shape[0]
    S, D = x.shape[1], x.shape[2]
    x0 = x.reshape(S, D)
    memf = mem.reshape(mem.shape[1], D)
    me = 4 * lax.axis_index("x") + 2 * lax.axis_index("y") + lax.axis_index("c")
    FS = W["ffn1_w2"].shape[1]
    KA, KD = W["conv_a_w"].shape[1], W["conv_d_w"].shape[1]
    CS = W["conv_a_w"].shape[2]

    cat = lambda l, parts: jnp.concatenate([(_t(W[n][l]) if tr else W[n][l]) for n, tr in parts], axis=0).astype(CDT)
    pf1 = [cat(l, (("ffn1_w1", 1), ("ffn1_w3", 1), ("ffn1_w2", 0))) for l in range(L)]
    pf2 = [cat(l, (("ffn2_w1", 1), ("ffn2_w3", 1), ("ffn2_w2", 0))) for l in range(L)]
    pma = [cat(l, (("w_in", 1), ("w_gate", 1))) for l in range(L)]
    pwo = [W["w_o"][l].astype(CDT) for l in range(L)]
    pxa = [cat(l, (("xa_wq", 0), ("xa_wk", 0), ("xa_wv", 0), ("xa_wo", 0))) for l in range(L)]
    wbs = [W["w_branch"][l].astype(CDT) for l in range(L)]
    cws = jnp.concatenate([W["conv_a_w"], W["conv_d_w"]], axis=1).reshape(-1, 128)
    sp_all = jnp.concatenate([W[n] for n, _ in _SP_NAMES], axis=1)
    bsc_all = W["sgu_b"][..., None]

    (cwg,) = _exchange([cws], False, "gather_conv_w")
    cwf = cwg.reshape(NS, L, KA + KD, CS).transpose(1, 2, 0, 3).reshape(L, KA + KD, NS * CS)

    def gather_start(l, after):
        return _hgather_start([pf1[l], pf2[l], pma[l], pwo[l], pxa[l], wbs[l]], after, f"gather_start_{l}")

    def gather_rest(h, after, tag):
        mid = _hgather_forward(h, after, f"gather_forward_{tag}")
        return _hgather_wait(mid, mid["token"], f"gather_wait_{tag}")

    packs = [None] * L
    first = [_hgather_start([pf1[0]], sp_all, "gather_start_0a")]
    saved = []
    xc = x0
    for l in range(L):
        if l == 0:
            (gf1,) = gather_rest(first[0], sp_all, "0a")
            first.append(_hgather_start([pma[0], pwo[0], wbs[0]], gf1, "gather_start_0b"))
        else:
            gf1, gf2, gma, gwo, gxa, gwb = packs[l]
        sp = sp_all[l:l + 1]
        cwa, cwd = cwf[l, :KA], cwf[l, KA:]
        wp, ws, bsc = W["pool_w"][l], W["sgu_ws"][l], bsc_all[l]
        s = {"x0": xc}
        nxt = gather_start(l + 1, gf1) if 0 < l < L - 1 else None
        xc, s["hb1"], s["a1"], s["b1"], s["y1"] = _ffn_fwd(xc, sp, "ffn1_pre_g", "ffn1_post_g", gf1,
                                                            (first[1] if l == 0 else nxt)["token"] if l == 0 or nxt else sp)
        s["x1"] = xc
        if l == 0:
            gma, gwo, gwb = gather_rest(first[1], xc, "0b")
            first.append(_hgather_start([pxa[0]], gma, "gather_start_0c"))
            first.append(_hgather_start([pf2[0]], first[2]["token"], "gather_start_0d"))
            nxt = gather_start(1, first[3]["token"]) if L > 1 else None
        s["hbm"], s["z"], s["g"] = _mix_in(xc, sp, gma, (nxt or first[3])["token"] if l == 0 else sp)
        s["ma"] = _mixA_fwd(s["z"], cwa, sp)
        s["mb"] = _mixB_fwd(s["z"], wp, sp)
        s["mc"] = _mixC_fwd(s["z"], ws, bsc, sp)
        s["yd"] = _mixD_conv_fwd(s["z"], cwd, sp)
        xc, s["md"], s["yk"], s["mg"], s["mo"] = _merge_fwd(s["ma"], s["mb"], s["mc"], s["yd"], s["g"], gwb, gwo, xc, sp)
        s["x2"] = xc
        if l == 0:
            (gxa,) = gather_rest(first[2], xc, "0c")
        s["mn"], s["k"], s["v"] = _xa_kv(memf, sp, gxa)
        xc, s["hbx"], s["q"], s["o"], s["po"] = _xa_fwd(xc, s["k"], s["v"], sp, gxa)
        s["x3"] = xc
        if l == 0:
            (gf2,) = gather_rest(first[3], xc, "0d")
            packs[0] = (gf1, gf2, gma, gwo, gxa, gwb)
        mid = _hgather_forward(nxt, xc, f"gather_forward_{l + 1}") if nxt and l > 0 else None
        xc, s["hb2"], s["a2"], s["b2"], s["y2"] = _ffn_fwd(xc, sp, "ffn2_pre_g", "ffn2_post_g", gf2,
                                                            mid["token"] if mid else sp)
        saved.append(s)
        if nxt:
            mid = mid or _hgather_forward(nxt, xc, f"gather_forward_{l + 1}")
            packs[l + 1] = _hgather_wait(mid, xc, f"gather_wait_{l + 1}")

    dx, lpart = _loss_head(xc, loss_target.reshape(S, D))
    loss = lax.psum(lpart[0, 0], ("x", "y", "c"))

    rep = {n: [None] * L for n in _REP_NAMES}
    summed = [dict() for _ in range(L)]
    pending = None
    last = []
    for l in reversed(range(L)):
        gf1, gf2, gma, gwo, gxa, gwb = packs[l]
        sp = sp_all[l:l + 1]
        cwa, cwd = cwf[l, :KA], cwf[l, KA:]
        wp, ws, bsc = W["pool_w"][l], W["sgu_ws"][l], bsc_all[l]
        s = saved[l]

        dx, dyb, da, db, gp = _ffn_bwd_act(dx, s["x3"], s["y2"], s["a2"], s["b2"], sp, "ffn2_pre_g", "ffn2_post_g", gf2,
                                           pending[1]["token"] if pending else sp)
        rep["ffn2_pre_g"][l], rep["ffn2_post_g"][l] = gp[0], gp[1]
        d_f2 = _ffn_bwd_w(s["hb2"], dyb, s["a2"], s["b2"], da, db)
        if l == 0:
            last.append(_exchange_start([d_f2], (True,), dx, "scatter_start_0a"))

        dx, dpo, dq, dk, dv, gp = _xa_bwd_act(dx, s["x2"], s["po"], s["q"], s["k"], s["v"], sp, gxa,
                                              last[-1]["token"] if last else sp)
        rep["xa_pre_g"][l], rep["xa_post_g"][l] = gp[0], gp[1]
        d_xa = _xa_bwd_w(s["hbx"], dq, s["o"], dpo, s["mn"], dk, dv)
        rep["mem_g"][l] = _xa_kv_bwd(memf, dk, dv, sp, gxa)[0]
        if l == 0:
            last.append(_exchange_start([d_xa], (True,), dx, "scatter_start_0b"))

        dmo, dm, dgp, dyk, gp = _merge_bwd_act(dx, s["mo"], s["g"], s["yk"], gwb, gwo, sp, last[-1]["token"] if last else sp)
        rep["mix_post_g"][l] = gp[0]
        d_wb, d_wo = _merge_bwd_w(s["ma"], s["mb"], s["mc"], s["md"], dyk, s["mg"], dmo)
        dza, dcw, gp = _mixA_bwd(s["z"], dm, cwa, sp)
        rep["conv_a_w"][l], rep["conv_a_b"][l] = dcw, gp[0]
        dzb, dwp, gp = _mixB_bwd(s["z"], dm, wp, sp)
        rep["pool_w"][l], rep["pool_scale"][l] = dwp, gp[0]
        dzc, dws, dbs, gp = _mixC_bwd(s["z"], dm, ws, bsc, sp)
        rep["sgu_ws"][l], rep["sgu_b"][l], rep["sgu_ln_g"][l], rep["sgu_ln_b"][l] = dws, dbs[:, :, 0], gp[0], gp[1]
        dyd, gp = _mixD_ln_bwd(dm, s["yd"], sp)
        rep["conv_d_ln_g"][l], rep["conv_d_ln_b"][l] = gp[0], gp[1]
        dzd, dcw, gp = _mixD_conv_bwd(s["z"], dyd, cwd)
        rep["conv_d_w"][l], rep["conv_d_b"][l] = dcw, gp[0]
        dz = jnp.concatenate([dza, dzb, dzc, dzd], axis=0)
        dx, gp = _mix_in_bwd_act(dz, dgp, dx, s["x1"], sp, gma)
        rep["mix_pre_g"][l] = gp[0]
        d_ma, dbg = _mix_in_bwd_w(dz, dgp, s["hbm"])
        rep["b_gate"][l] = dbg[:, 0, :].reshape(-1)
        if l == 0:
            last.append(_exchange_start([d_ma, d_wo, d_wb], (True,) * 3, dx, "scatter_start_0c"))

        dx, dyb, da, db, gp = _ffn_bwd_act(dx, s["x0"], s["y1"], s["a1"], s["b1"], sp, "ffn1_pre_g", "ffn1_post_g", gf1,
                                           last[-1]["token"] if last else sp)
        rep["ffn1_pre_g"][l], rep["ffn1_post_g"][l] = gp[0], gp[1]
        d_f1 = _ffn_bwd_w(s["hb1"], dyb, s["a1"], s["b1"], da, db)

        if pending:
            r = _exchange_wait(pending[1], dx, f"scatter_wait_{pending[0]}")
            summed[pending[0]] = dict(zip(("f1", "f2", "ma", "wo", "xa", "wb", "flat"), r))
        flat = jnp.concatenate([rep[n][l].reshape(-1) for n in _REP_NAMES])
        flat = jnp.pad(flat, (0, -flat.size % 1024)).reshape(-1, 128)
        if l == 0:
            last.append(_exchange_start([d_f1, flat], (True, False), dx, "scatter_start_0d"))
        else:
            pending = (l, _exchange_start([d_f1, d_f2, d_ma, d_wo, d_xa, d_wb, flat], (True,) * 6 + (False,), dx,
                                          f"scatter_start_{l}"))

    shape = {"f1": (3 * FS, D), "f2": (3 * FS, D), "ma": (2 * MW, D), "wo": (GW, D), "xa": (4 * GW, D),
             "wb": (4 * MW, GW), "flat": tuple(flat.shape)}
    stk = {k: lax.empty((L,) + s, F32) for k, s in shape.items()}

    def land(k, r, l):
        stk[k] = _slot_sum_into(stk[k], r.reshape((NS,) + shape[k]), l)

    for l in range(1, L):
        for k, r in summed[l].items():
            land(k, r, l)
    (r,) = _exchange_wait(last[0], dx, "scatter_wait_0a")
    land("f2", r, 0)
    (r,) = _exchange_wait(last[1], dx, "scatter_wait_0b")
    land("xa", r, 0)

    G, deltas, new_m, new_v = {}, {}, {}, {}

    def update_block(n, k, blk, transposed):
        tr = _t if transposed else (lambda a: a)
        out = _adamw_block(tr(W[n]), stk[k], tr(M[n]), tr(V[n]), blk)
        G[n], deltas[n], new_m[n], new_v[n] = (tr(a) for a in out)
        return deltas[n]

    def update(n):
        deltas[n], new_m[n], new_v[n] = _adamw(W[n], G[n], M[n], V[n])
        return deltas[n]

    done = [update_block("ffn2_w1", "f2", 0, True), update_block("ffn2_w3", "f2", 1, True),
            update_block("ffn2_w2", "f2", 2, False)]
    done += [update_block(n, "xa", i, False) for i, n in enumerate(("xa_wq", "xa_wk", "xa_wv", "xa_wo"))]
    r = _exchange_wait(last[2], done + [stk[k] for k in ("f1", "ma", "wo", "wb", "flat")], "scatter_wait_0c")
    for k, v in zip(("ma", "wo", "wb"), r):
        land(k, v, 0)
    G["w_in"], G["w_gate"] = _t(stk["ma"][:, :MW]), _t(stk["ma"][:, MW:])
    G["w_branch"] = stk["wb"].reshape(W["w_branch"].shape)
    done = [update("w_in"), update("w_gate"), update("w_branch"), update_block("w_o", "wo", 0, False)]
    r = _exchange_wait(last[3], done, "scatter_wait_0d")
    land("f1", r[0], 0)
    land("flat", r[1], 0)
    update_block("ffn1_w1", "f1", 0, True)
    update_block("ffn1_w3", "f1", 1, True)
    update_block("ffn1_w2", "f1", 2, False)

    tot = [stk["flat"][l].reshape(-1) for l in range(L)]
    off = 0
    for n in _REP_NAMES:
        shape = (KA, NS * CS) if n == "conv_a_w" else (KD, NS * CS) if n == "conv_d_w" else W[n].shape[1:]
        size = 1
        for d in shape:
            size *= d
        G[n] = jnp.stack([tot[l][off:off + size].reshape(shape) for l in range(L)])
        off += size
    for n in ("conv_a_w", "conv_d_w"):
        G[n] = lax.dynamic_slice_in_dim(G[n], me * CS, CS, axis=2)

    for n in _W_NAMES:
        if n not in deltas:
            deltas[n], new_m[n], new_v[n] = _adamw(W[n], G[n], M[n], V[n])
    grad_x = dx.reshape(x.shape)
    return (loss, grad_x, *[G[n] for n in _W_NAMES], *[deltas[n] for n in _W_NAMES],
            *[new_m[n] for n in _W_NAMES], *[new_v[n] for n in _W_NAMES])


def kernel(x, mem, ffn1_pre_g, ffn1_post_g, ffn1_w1, ffn1_w3, ffn1_w2, mix_pre_g, mix_post_g, w_in, conv_a_w, conv_a_b, pool_w, pool_scale, sgu_ln_g, sgu_ln_b, sgu_ws, sgu_b, conv_d_w, conv_d_b, conv_d_ln_g, conv_d_ln_b, w_branch, w_gate, b_gate, w_o, xa_pre_g, xa_post_g, mem_g, xa_wq, xa_wk, xa_wv, xa_wo, ffn2_pre_g, ffn2_post_g, ffn2_w1, ffn2_w3, ffn2_w2, loss_target, m_ffn1_pre_g, m_ffn1_post_g, m_ffn1_w1, m_ffn1_w3, m_ffn1_w2, m_mix_pre_g, m_mix_post_g, m_w_in, m_conv_a_w, m_conv_a_b, m_pool_w, m_pool_scale, m_sgu_ln_g, m_sgu_ln_b, m_sgu_ws, m_sgu_b, m_conv_d_w, m_conv_d_b, m_conv_d_ln_g, m_conv_d_ln_b, m_w_branch, m_w_gate, m_b_gate, m_w_o, m_xa_pre_g, m_xa_post_g, m_mem_g, m_xa_wq, m_xa_wk, m_xa_wv, m_xa_wo, m_ffn2_pre_g, m_ffn2_post_g, m_ffn2_w1, m_ffn2_w3, m_ffn2_w2, v_ffn1_pre_g, v_ffn1_post_g, v_ffn1_w1, v_ffn1_w3, v_ffn1_w2, v_mix_pre_g, v_mix_post_g, v_w_in, v_conv_a_w, v_conv_a_b, v_pool_w, v_pool_scale, v_sgu_ln_g, v_sgu_ln_b, v_sgu_ws, v_sgu_b, v_conv_d_w, v_conv_d_b, v_conv_d_ln_g, v_conv_d_ln_b, v_w_branch, v_w_gate, v_b_gate, v_w_o, v_xa_pre_g, v_xa_post_g, v_mem_g, v_xa_wq, v_xa_wk, v_xa_wv, v_xa_wo, v_ffn2_pre_g, v_ffn2_post_g, v_ffn2_w1, v_ffn2_w3, v_ffn2_w2):
    args = dict(locals())
    W = {n: args[n] for n in _W_NAMES}
    M = {n: args["m_" + n] for n in _W_NAMES}
    V = {n: args["v_" + n] for n in _W_NAMES}
    return _step(x, mem, loss_target, W, M, V)
```

```python
import jax
import jax.numpy as jnp
from jax import lax
from jax.experimental import pallas as pl
from jax.experimental.pallas import tpu as pltpu

F32 = jnp.float32
CDT = jnp.bfloat16
EPS = 1e-6
NS = 8
GW = 128
MW = 512
CHUNK = 64
XA_HEADS = 4
POOL_WINDOWS = (2, 4, 8, 16)
VMEM_LIMIT = 56 * 1024 * 1024
ADAM_LR, ADAM_B1, ADAM_B2, ADAM_EPS, ADAM_WD, ADAM_STEP = 0.001, 0.9, 0.999, 1e-08, 0.01, 10

SDS = jax.ShapeDtypeStruct

_SP_NAMES = (("ffn1_pre_g", 1024), ("ffn1_post_g", 1024), ("mix_pre_g", 1024), ("mix_post_g", 1024),
             ("xa_pre_g", 1024), ("xa_post_g", 1024), ("mem_g", 1024), ("ffn2_pre_g", 1024), ("ffn2_post_g", 1024),
             ("conv_a_b", 512), ("pool_scale", 512), ("sgu_ln_g", 512), ("sgu_ln_b", 512), ("conv_d_b", 512),
             ("conv_d_ln_g", 512), ("conv_d_ln_b", 512), ("b_gate", 4096))
_SP = {}
_off = 0
for _n, _w in _SP_NAMES:
    _SP[_n] = (_off, _w)
    _off += _w
_SP_TOTAL = _off


def _call(body, name, grid, in_specs, out_specs, out_shape, scratch=()):
    return pl.pallas_call(
        body, name=name, grid=grid, in_specs=in_specs, out_specs=out_specs, out_shape=out_shape,
        scratch_shapes=list(scratch),
        compiler_params=pltpu.CompilerParams(dimension_semantics=("arbitrary",) * len(grid),
                                             vmem_limit_bytes=VMEM_LIMIT))


def _nn(a, b):
    return lax.dot_general(a, b, (((1,), (0,)), ((), ())), preferred_element_type=F32)


def _nt(a, b):
    return lax.dot_general(a, b, (((1,), (1,)), ((), ())), preferred_element_type=F32)


def _tn(a, b):
    return lax.dot_general(a, b, (((0,), (0,)), ((), ())), preferred_element_type=F32)


def _rms(x):
    r = lax.rsqrt(jnp.mean(x * x, axis=-1, keepdims=True) + EPS)
    return x * r, r


def _rms_bwd(n, r, g, dout):
    dn = dout * g
    dx = r * (dn - n * jnp.mean(dn * n, axis=-1, keepdims=True))
    return dx, jnp.sum(dout * n, axis=0, keepdims=True)


def _ln(y):
    mu = jnp.mean(y, axis=-1, keepdims=True)
    yc = y - mu
    rs = lax.rsqrt(jnp.mean(yc * yc, axis=-1, keepdims=True) + EPS)
    return yc * rs, rs


def _ln_bwd(xh, rs, dxh):
    return rs * (dxh - jnp.mean(dxh, axis=-1, keepdims=True) - xh * jnp.mean(dxh * xh, axis=-1, keepdims=True))


def _silu_parts(a):
    s = jax.nn.sigmoid(a)
    sl = a * s
    return sl, s + sl * (1.0 - s)


_GELU_C = 0.7978845608028654
_GELU_A = 0.044715


def _gelu(x):
    return 0.5 * x * (1.0 + jnp.tanh(_GELU_C * (x + _GELU_A * x * x * x)))


def _gelu_parts(x):
    t = jnp.tanh(_GELU_C * (x + _GELU_A * x * x * x))
    g = 0.5 * x * (1.0 + t)
    dg = 0.5 * (1.0 + t) + 0.5 * x * (1.0 - t * t) * _GELU_C * (1.0 + 3.0 * _GELU_A * x * x)
    return g, dg


def _spspec(name, width, imap):
    off = _SP[name][0]
    assert off % width == 0
    return pl.BlockSpec((1, width), lambda *a: (0, off // width + imap(*a)))


def _zero(*a):
    return 0


def _row_once(tm, d):
    return pl.BlockSpec((tm, d), lambda i, j: (i, 0), pipeline_mode=pl.Buffered(1))


FFN_SG = 2


def _ffn_fwd(x, sp, pre, post, pf, dep):
    S, D = x.shape
    FS = pf.shape[1] // 3
    TM = min(512, S)
    SG, NG, W = FFN_SG, NS // FFN_SG, FFN_SG * FS

    def body(x_ref, pg_ref, qg_ref, w1_ref, w3_ref, w2_ref, dep_ref, xo_ref, hb_ref, a_ref, b_ref, y_ref, hb_s, acc):
        j = pl.program_id(1)

        @pl.when(j == 0)
        def _():
            n, _ = _rms(x_ref[...])
            hb = (n * pg_ref[...]).astype(CDT)
            hb_s[...] = hb
            hb_ref[...] = hb
            acc[...] = jnp.zeros_like(acc)

        hb = hb_s[...]
        a = _nt(hb, w1_ref[...].reshape(W, D))
        b = _nt(hb, w3_ref[...].reshape(W, D))
        a_ref[...] = a.astype(CDT)
        b_ref[...] = b.astype(CDT)
        u = (a * jax.nn.sigmoid(a) * b).astype(CDT)
        acc[...] += _nn(u, w2_ref[...].reshape(W, D))

        @pl.when(j == NG - 1)
        def _():
            y = acc[...]
            y_ref[...] = y.astype(CDT)
            n, _ = _rms(y)
            xo_ref[...] = x_ref[...] + 0.5 * (n * qg_ref[...])

    row1 = pl.BlockSpec((TM, D), lambda i, j: (i, 0))
    grp = lambda i, j: (j, i, 0)
    return _call(
        body, "ffn_fwd", (S // TM, NG),
        [row1, _spspec(pre, D, _zero), _spspec(post, D, _zero),
         pl.BlockSpec((SG, FS, D), lambda i, j: (j, 0, 0)), pl.BlockSpec((SG, FS, D), lambda i, j: (j, 1, 0)),
         pl.BlockSpec((SG, FS, D), lambda i, j: (j, 2, 0)), pl.BlockSpec(memory_space=pl.ANY)],
        [row1, row1, pl.BlockSpec((None, TM, W), grp), pl.BlockSpec((None, TM, W), grp), row1],
        [SDS((S, D), F32), SDS((S, D), CDT), SDS((NG, S, W), CDT), SDS((NG, S, W), CDT), SDS((S, D), CDT)],
        [pltpu.VMEM((TM, D), CDT), pltpu.VMEM((TM, D), F32)])(x, sp, sp, pf, pf, pf, dep)


def _ffn_bwd_act(dxo, x, y, a, b, sp, pre, post, pf, dep):
    S, D = x.shape
    FS = pf.shape[1] // 3
    TM = min(512, S)
    SG, NG, W = FFN_SG, NS // FFN_SG, FFN_SG * FS

    def body(dxo_ref, x_ref, y_ref, a_ref, b_ref, pg_ref, qg_ref, w1_ref, w3_ref, w2_ref, dep_ref,
             dx_ref, dyb_ref, da_ref, db_ref, gp_ref, dyb_s, acc):
        i = pl.program_id(0)
        j = pl.program_id(1)

        @pl.when((i == 0) & (j == 0))
        def _():
            gp_ref[...] = jnp.zeros_like(gp_ref)

        @pl.when(j == 0)
        def _():
            n, r = _rms(y_ref[...].astype(F32))
            dy, dg = _rms_bwd(n, r, qg_ref[...], 0.5 * dxo_ref[...])
            dyb = dy.astype(CDT)
            dyb_s[...] = dyb
            dyb_ref[...] = dyb
            gp_ref[1:2, :] += dg
            acc[...] = jnp.zeros_like(acc)

        sl, dsl = _silu_parts(a_ref[...].astype(F32))
        du = _nt(dyb_s[...], w2_ref[...].reshape(W, D))
        db = (du * sl).astype(CDT)
        da = (du * b_ref[...].astype(F32) * dsl).astype(CDT)
        da_ref[...] = da
        db_ref[...] = db
        acc[...] += _nn(da, w1_ref[...].reshape(W, D)) + _nn(db, w3_ref[...].reshape(W, D))

        @pl.when(j == NG - 1)
        def _():
            n, r = _rms(x_ref[...])
            dx, dg = _rms_bwd(n, r, pg_ref[...], acc[...])
            dx_ref[...] = dxo_ref[...] + dx
            gp_ref[0:1, :] += dg

    row = lambda i, j: (i, 0)
    grp = lambda i, j: (j, i, 0)
    return _call(
        body, "ffn_bwd_act", (S // TM, NG),
        [pl.BlockSpec((TM, D), row), pl.BlockSpec((TM, D), row), pl.BlockSpec((TM, D), row),
         pl.BlockSpec((None, TM, W), grp), pl.BlockSpec((None, TM, W), grp),
         _spspec(pre, D, _zero), _spspec(post, D, _zero),
         pl.BlockSpec((SG, FS, D), lambda i, j: (j, 0, 0)), pl.BlockSpec((SG, FS, D), lambda i, j: (j, 1, 0)),
         pl.BlockSpec((SG, FS, D), lambda i, j: (j, 2, 0)), pl.BlockSpec(memory_space=pl.ANY)],
        [pl.BlockSpec((TM, D), row), pl.BlockSpec((TM, D), row), pl.BlockSpec((None, TM, W), grp),
         pl.BlockSpec((None, TM, W), grp), pl.BlockSpec((8, D), lambda i, j: (0, 0))],
        [SDS((S, D), F32), SDS((S, D), CDT), SDS((NG, S, W), CDT), SDS((NG, S, W), CDT), SDS((8, D), F32)],
        [pltpu.VMEM((TM, D), CDT), pltpu.VMEM((TM, D), F32)])(dxo, x, y, a, b, sp, sp, pf, pf, pf, dep)


def _ffn_bwd_w(hb, dyb, a, b, da, db):
    S, D = hb.shape
    SG, NG = FFN_SG, NS // FFN_SG
    W = a.shape[2]
    FS = W // SG
    TK = min(512, S)
    NK = S // TK

    def body(hb_ref, dyb_ref, a_ref, b_ref, da_ref, db_ref, g_ref, acc):
        k = pl.program_id(1)

        @pl.when(k == 0)
        def _():
            acc[...] = jnp.zeros_like(acc)

        af = a_ref[...].astype(F32)
        u = (af * jax.nn.sigmoid(af) * b_ref[...].astype(F32)).astype(CDT)
        hb = hb_ref[...]
        acc[0:W, :] += _tn(da_ref[...], hb)
        acc[W:2 * W, :] += _tn(db_ref[...], hb)
        acc[2 * W:3 * W, :] += _tn(u, dyb_ref[...])

        @pl.when(k == NK - 1)
        def _():
            for s in range(SG):
                for r in range(3):
                    g_ref[s, r * FS:(r + 1) * FS, :] = acc[r * W + s * FS:r * W + (s + 1) * FS, :].astype(CDT)

    row = lambda j, k: (k, 0)
    grp = lambda j, k: (j, k, 0)
    return _call(
        body, "ffn_bwd_w", (NG, NK),
        [pl.BlockSpec((TK, D), row), pl.BlockSpec((TK, D), row)] + [pl.BlockSpec((None, TK, W), grp)] * 4,
        pl.BlockSpec((SG, 3 * FS, D), lambda j, k: (j, 0, 0)),
        SDS((NS, 3 * FS, D), CDT),
        [pltpu.VMEM((3 * W, D), F32)])(hb, dyb, a, b, da, db)


def _mix_in(x, sp, pma, dep):
    S, D = x.shape
    TM = min(1024, S)

    def body(x_ref, pg_ref, bg_ref, wi_ref, wg_ref, dep_ref, hb_ref, z_ref, g_ref, hb_s):
        @pl.when(pl.program_id(1) == 0)
        def _():
            n, _ = _rms(x_ref[...])
            hb = (n * pg_ref[...]).astype(CDT)
            hb_s[...] = hb
            hb_ref[...] = hb

        hb = hb_s[...]
        z_ref[...] = _nt(hb, wi_ref[...]).astype(CDT)
        g_ref[...] = jax.nn.sigmoid(_nt(hb, wg_ref[...]) + bg_ref[...]).astype(CDT)

    return _call(
        body, "mix_in", (S // TM, NS),
        [_row_once(TM, D), _spspec("mix_pre_g", D, _zero), _spspec("b_gate", MW, lambda i, j: j),
         pl.BlockSpec((None, MW, D), lambda i, j: (j, 0, 0)), pl.BlockSpec((None, MW, D), lambda i, j: (j, 1, 0)), _ANY],
        [_row_once(TM, D), pl.BlockSpec((None, TM, MW), lambda i, j: (j, i, 0)),
         pl.BlockSpec((None, TM, MW), lambda i, j: (j // 2, i, j % 2))],
        [SDS((S, D), CDT), SDS((NS, S, MW), CDT), SDS((4, S, D), CDT)],
        [pltpu.VMEM((TM, D), CDT)])(x, sp, sp, pma, pma, dep)


def _mix_in_bwd_act(dz, dgp, dxr, x, sp, pma):
    S, D = x.shape
    TM = min(1024, S)

    def body(dz_ref, dg_ref, dxr_ref, x_ref, pg_ref, wi_ref, wg_ref, dx_ref, gp_ref, acc):
        i = pl.program_id(0)
        j = pl.program_id(1)

        @pl.when((i == 0) & (j == 0))
        def _():
            gp_ref[...] = jnp.zeros_like(gp_ref)

        @pl.when(j == 0)
        def _():
            acc[...] = jnp.zeros_like(acc)

        acc[...] += _nn(dz_ref[...], wi_ref[...]) + _nn(dg_ref[...], wg_ref[...])

        @pl.when(j == NS - 1)
        def _():
            n, r = _rms(x_ref[...])
            dx, dg = _rms_bwd(n, r, pg_ref[...], acc[...])
            dx_ref[...] = dxr_ref[...] + dx
            gp_ref[0:1, :] += dg

    return _call(
        body, "mix_in_bwd_act", (S // TM, NS),
        [pl.BlockSpec((None, TM, MW), lambda i, j: (j, i, 0)), pl.BlockSpec((None, TM, MW), lambda i, j: (j // 2, i, j % 2)),
         _row_once(TM, D), _row_once(TM, D), _spspec("mix_pre_g", D, _zero),
         pl.BlockSpec((None, MW, D), lambda i, j: (j, 0, 0)), pl.BlockSpec((None, MW, D), lambda i, j: (j, 1, 0))],
        [_row_once(TM, D), pl.BlockSpec((8, D), lambda i, j: (0, 0))],
        [SDS((S, D), F32), SDS((8, D), F32)],
        [pltpu.VMEM((TM, D), F32)])(dz, dgp, dxr, x, sp, pma, pma)


def _mix_in_bwd_w(dz, dgp, hb):
    S, D = hb.shape
    TK = min(512, S)
    NK = S // TK

    def body(dz_ref, dg_ref, hb_ref, g_ref, bg_ref, acc):
        k = pl.program_id(1)

        @pl.when(k == 0)
        def _():
            acc[...] = jnp.zeros_like(acc)
            bg_ref[...] = jnp.zeros_like(bg_ref)

        hb = hb_ref[...]
        dg = dg_ref[...]
        acc[0:MW, :] += _tn(dz_ref[...], hb)
        acc[MW:2 * MW, :] += _tn(dg, hb)
        bg_ref[0:1, :] += jnp.sum(dg.astype(F32), axis=0, keepdims=True)

        @pl.when(k == NK - 1)
        def _():
            g_ref[...] = acc[...].astype(CDT)

    return _call(
        body, "mix_in_bwd_w", (NS, NK),
        [pl.BlockSpec((None, TK, MW), lambda j, k: (j, k, 0)), pl.BlockSpec((None, TK, MW), lambda j, k: (j // 2, k, j % 2)),
         pl.BlockSpec((TK, D), lambda j, k: (k, 0))],
        [pl.BlockSpec((None, 2 * MW, D), lambda j, k: (j, 0, 0)), pl.BlockSpec((None, 8, MW), lambda j, k: (j, 0, 0))],
        [SDS((NS, 2 * MW, D), CDT), SDS((NS, 8, MW), F32)],
        [pltpu.VMEM((2 * MW, D), F32)])(dz, dgp, hb)


def _causal_taps(pad_ref, i, ch, halo, k_taps, lanes=slice(None)):
    val = pad_ref[pl.ds(pl.multiple_of(i * ch, 8), ch + halo), lanes]
    base = {}
    out = []
    for k in range(k_taps):
        q, r = divmod(k_taps - 1 - k, 8)
        if r not in base:
            base[r] = pltpu.roll(val, r, 0) if r else val
        out.append((k, base[r][halo - 8 * q:halo - 8 * q + ch, :]))
    return out


def _anti_taps(pad_ref, i, ch, halo, k_taps, lanes=slice(None)):
    val = pad_ref[pl.ds(pl.multiple_of(i * ch, 8), ch + halo), lanes]
    n = ch + halo
    base = {}
    out = []
    for k in range(k_taps):
        q, r = divmod(k_taps - 1 - k, 8)
        if r not in base:
            base[r] = pltpu.roll(val, n - r, 0) if r else val
        out.append((k, base[r][8 * q:8 * q + ch, :]))
    return out


def _conv_geometry(S, k_taps):
    halo = 8 * ((k_taps - 1 + 7) // 8)
    ch = min(256, S)
    return halo, ch, S // ch


def _rows(i, ch):
    return pl.ds(pl.multiple_of(i * ch, ch), ch)


def _mixA_fwd(z, cw, sp):
    S = z.shape[1]
    K = cw.shape[0]
    H, CH, NCH = _conv_geometry(S, K)

    def body(z_ref, w_ref, b_ref, o_ref, pad):
        pad[0:H, :] = jnp.zeros((H, GW), F32)

        def fill(i, c):
            r = _rows(i, CH)
            pad[pl.ds(pl.multiple_of(i * CH + H, 8), CH), :] = z_ref[2, r, :].astype(F32) * z_ref[0, r, :].astype(F32)
            return c

        lax.fori_loop(0, NCH, fill, 0)

        def conv(i, c):
            r = _rows(i, CH)
            acc = jnp.zeros((CH, GW), F32)
            for k, sh in _causal_taps(pad, i, CH, H, K):
                acc = acc + w_ref[k:k + 1, :] * sh
            o_ref[r, :] = (z_ref[1, r, :].astype(F32) * (acc + b_ref[...])).astype(CDT)
            return c

        lax.fori_loop(0, NCH, conv, 0)

    return _call(
        body, "mixA_fwd", (MW // GW,),
        [pl.BlockSpec((3, S, GW), lambda c: (0, 0, c)), pl.BlockSpec((K, GW), lambda c: (0, c)),
         _spspec("conv_a_b", GW, lambda c: c)],
        pl.BlockSpec((S, GW), lambda c: (0, c)), SDS((S, MW), CDT),
        [pltpu.VMEM((H + S, GW), F32)])(z, cw, sp)


def _mixA_bwd(z, dm, cw, sp):
    S = z.shape[1]
    K = cw.shape[0]
    H, CH, NCH = _conv_geometry(S, K)

    def body(z_ref, dm_ref, w_ref, b_ref, dz_ref, dw_ref, db_ref, pad, dpad, dw_s):
        pad[0:H, :] = jnp.zeros((H, GW), F32)
        dpad[pl.ds(S, H), :] = jnp.zeros((H, GW), F32)
        dw_s[...] = jnp.zeros_like(dw_s)
        db_ref[...] = jnp.zeros_like(db_ref)

        def fill(i, c):
            r = _rows(i, CH)
            pad[pl.ds(pl.multiple_of(i * CH + H, 8), CH), :] = z_ref[2, r, :].astype(F32) * z_ref[0, r, :].astype(F32)
            return c

        lax.fori_loop(0, NCH, fill, 0)

        def p1(i, c):
            r = _rows(i, CH)
            taps = _causal_taps(pad, i, CH, H, K)
            acc = jnp.zeros((CH, GW), F32)
            for k, sh in taps:
                acc = acc + w_ref[k:k + 1, :] * sh
            dmf = dm_ref[r, :].astype(F32)
            dz_ref[1, r, :] = (dmf * (acc + b_ref[...])).astype(CDT)
            dc = dmf * z_ref[1, r, :].astype(F32)
            dpad[r, :] = dc
            for k, sh in taps:
                dw_s[k:k + 1, :] += jnp.sum(dc * sh, axis=0, keepdims=True)
            db_ref[0:1, :] += jnp.sum(dc, axis=0, keepdims=True)
            return c

        lax.fori_loop(0, NCH, p1, 0)

        def p2(i, c):
            r = _rows(i, CH)
            dq = jnp.zeros((CH, GW), F32)
            for k, sh in _anti_taps(dpad, i, CH, H, K):
                dq = dq + w_ref[k:k + 1, :] * sh
            dz_ref[0, r, :] = (dq * z_ref[2, r, :].astype(F32)).astype(CDT)
            dz_ref[2, r, :] = (dq * z_ref[0, r, :].astype(F32)).astype(CDT)
            return c

        lax.fori_loop(0, NCH, p2, 0)
        dw_ref[...] = dw_s[0:K, :]

    return _call(
        body, "mixA_bwd", (MW // GW,),
        [pl.BlockSpec((3, S, GW), lambda c: (0, 0, c)), pl.BlockSpec((None, S, GW), lambda c: (0, 0, c)),
         pl.BlockSpec((K, GW), lambda c: (0, c)), _spspec("conv_a_b", GW, lambda c: c)],
        [pl.BlockSpec((3, S, GW), lambda c: (0, 0, c)), pl.BlockSpec((K, GW), lambda c: (0, c)),
         pl.BlockSpec((8, GW), lambda c: (0, c))],
        [SDS((3, S, MW), CDT), SDS((K, MW), F32), SDS((8, MW), F32)],
        [pltpu.VMEM((H + S, GW), F32), pltpu.VMEM((S + H, GW), F32), pltpu.VMEM((8 * ((K + 7) // 8), GW), F32)])(z, dm, cw, sp)


def _mixD_conv_fwd(z, cw, sp):
    S = z.shape[1]
    K = cw.shape[0]
    H, CH, NCH = _conv_geometry(S, K)

    def body(z_ref, w_ref, b_ref, o_ref, pad):
        pad[0:H, :] = jnp.zeros((H, GW), F32)

        def fill(i, c):
            r = _rows(i, CH)
            pad[pl.ds(pl.multiple_of(i * CH + H, 8), CH), :] = (
                z_ref[0, r, :].astype(F32) * jax.nn.sigmoid(z_ref[1, r, :].astype(F32)))
            return c

        lax.fori_loop(0, NCH, fill, 0)

        def conv(i, c):
            acc = jnp.zeros((CH, GW), F32)
            for k, sh in _causal_taps(pad, i, CH, H, K):
                acc = acc + w_ref[k:k + 1, :] * sh
            o_ref[_rows(i, CH), :] = (acc + b_ref[...]).astype(CDT)
            return c

        lax.fori_loop(0, NCH, conv, 0)

    return _call(
        body, "mixD_conv_fwd", (MW // GW,),
        [pl.BlockSpec((2, S, GW), lambda c: (3, 0, c)), pl.BlockSpec((K, GW), lambda c: (0, c)),
         _spspec("conv_d_b", GW, lambda c: c)],
        pl.BlockSpec((S, GW), lambda c: (0, c)), SDS((S, MW), CDT),
        [pltpu.VMEM((H + S, GW), F32)])(z, cw, sp)


def _mixD_conv_bwd(z, dy, cw):
    S = z.shape[1]
    K = cw.shape[0]
    H, CH, NCH = _conv_geometry(S, K)

    def body(z_ref, dy_ref, w_ref, dz_ref, dw_ref, db_ref, pad, dpad, dw_s):
        pad[0:H, :] = jnp.zeros((H, GW), F32)
        dpad[pl.ds(S, H), :] = jnp.zeros((H, GW), F32)
        dw_s[...] = jnp.zeros_like(dw_s)
        db_ref[...] = jnp.zeros_like(db_ref)

        def fill(i, c):
            r = _rows(i, CH)
            pad[pl.ds(pl.multiple_of(i * CH + H, 8), CH), :] = (
                z_ref[0, r, :].astype(F32) * jax.nn.sigmoid(z_ref[1, r, :].astype(F32)))
            dpad[r, :] = dy_ref[r, :].astype(F32)
            return c

        lax.fori_loop(0, NCH, fill, 0)

        def p1(i, c):
            dyf = dy_ref[_rows(i, CH), :].astype(F32)
            for k, sh in _causal_taps(pad, i, CH, H, K):
                dw_s[k:k + 1, :] += jnp.sum(dyf * sh, axis=0, keepdims=True)
            db_ref[0:1, :] += jnp.sum(dyf, axis=0, keepdims=True)
            return c

        lax.fori_loop(0, NCH, p1, 0)

        def p2(i, c):
            r = _rows(i, CH)
            dq = jnp.zeros((CH, GW), F32)
            for k, sh in _anti_taps(dpad, i, CH, H, K):
                dq = dq + w_ref[k:k + 1, :] * sh
            a = z_ref[0, r, :].astype(F32)
            sg = jax.nn.sigmoid(z_ref[1, r, :].astype(F32))
            dz_ref[0, r, :] = (dq * sg).astype(CDT)
            dz_ref[1, r, :] = (dq * a * sg * (1.0 - sg)).astype(CDT)
            return c

        lax.fori_loop(0, NCH, p2, 0)
        dw_ref[...] = dw_s[0:K, :]

    return _call(
        body, "mixD_conv_bwd", (MW // GW,),
        [pl.BlockSpec((2, S, GW), lambda c: (3, 0, c)), pl.BlockSpec((S, GW), lambda c: (0, c)),
         pl.BlockSpec((K, GW), lambda c: (0, c))],
        [pl.BlockSpec((2, S, GW), lambda c: (0, 0, c)), pl.BlockSpec((K, GW), lambda c: (0, c)),
         pl.BlockSpec((8, GW), lambda c: (0, c))],
        [SDS((2, S, MW), CDT), SDS((K, MW), F32), SDS((8, MW), F32)],
        [pltpu.VMEM((H + S, GW), F32), pltpu.VMEM((S + H, GW), F32), pltpu.VMEM((8 * ((K + 7) // 8), GW), F32)])(z, dy, cw)


def _mixD_ln_bwd(dm, yd, sp):
    S = yd.shape[0]
    TM = min(512, S)

    def body(dm_ref, y_ref, lg_ref, lb_ref, dy_ref, gp_ref):
        @pl.when(pl.program_id(0) == 0)
        def _():
            gp_ref[...] = jnp.zeros_like(gp_ref)

        xh, rs = _ln(y_ref[...].astype(F32))
        _, dsl = _silu_parts(xh * lg_ref[...] + lb_ref[...])
        dl = dm_ref[...].astype(F32) * dsl
        gp_ref[0:1, :] += jnp.sum(dl * xh, axis=0, keepdims=True)
        gp_ref[1:2, :] += jnp.sum(dl, axis=0, keepdims=True)
        dy_ref[...] = _ln_bwd(xh, rs, dl * lg_ref[...]).astype(CDT)

    row = lambda i: (i, 0)
    return _call(
        body, "mixD_ln_bwd", (S // TM,),
        [pl.BlockSpec((None, TM, MW), lambda i: (3, i, 0)), pl.BlockSpec((TM, MW), row), _spspec("conv_d_ln_g", MW, _zero),
         _spspec("conv_d_ln_b", MW, _zero)],
        [pl.BlockSpec((TM, MW), row), pl.BlockSpec((8, MW), lambda i: (0, 0))],
        [SDS((S, MW), CDT), SDS((8, MW), F32)])(dm, yd, sp, sp)


def _box_causal(val, g):
    s = val
    for d in range(g + 1):
        s = s + pltpu.roll(s, 1 << d, 0)
    return s


def _box_anti(val, g):
    n = val.shape[0]
    s = val
    for d in range(g + 1):
        s = s + pltpu.roll(s, n - (1 << d), 0)
    return s


def _pool_count(i, ch, win):
    t = lax.broadcasted_iota(jnp.int32, (ch, GW), 0) + (i * ch + 1)
    return jnp.minimum(t, win).astype(F32)


def _mixB_fwd(z, wp, sp):
    S = z.shape[1]
    H, CH = 16, min(256, S)
    NCH = S // CH
    assert POOL_WINDOWS == tuple(2 << g for g in range(4))

    def body(p_ref, wp_ref, sc_ref, o_ref, pad):
        pad[0:H, :] = jnp.zeros((H, MW), F32)

        def fill(i, c):
            pad[pl.ds(pl.multiple_of(i * CH + H, 8), CH), :] = p_ref[_rows(i, CH), :].astype(F32)
            return c

        lax.fori_loop(0, NCH, fill, 0)

        def step(i, c):
            r = _rows(i, CH)
            for g in range(4):
                gs = slice(g * GW, (g + 1) * GW)
                val = pad[pl.ds(pl.multiple_of(i * CH, 8), CH + H), gs]
                pooled = _box_causal(val, g)[H:, :] / _pool_count(i, CH, POOL_WINDOWS[g]) - val[H:, :]
                mixed = _nn(pooled.astype(CDT), wp_ref[g].astype(CDT))
                o_ref[r, gs] = (mixed * sc_ref[:, gs]).astype(CDT)
            return c

        lax.fori_loop(0, NCH, step, 0)

    return _call(
        body, "mixB_fwd", (1,),
        [pl.BlockSpec((None, S, MW), lambda i: (3, 0, 0)), pl.BlockSpec((4, GW, GW), lambda i: (0, 0, 0)),
         _spspec("pool_scale", MW, _zero)],
        pl.BlockSpec((S, MW), lambda i: (0, 0)), SDS((S, MW), CDT),
        [pltpu.VMEM((H + S, MW), F32)])(z, wp, sp)


def _mixB_bwd(z, dm, wp, sp):
    S = z.shape[1]
    H, CH = 16, min(256, S)
    NCH = S // CH

    def body(p_ref, dm_ref, wp_ref, sc_ref, dz_ref, dwp_ref, dsc_ref, pad, rpad):
        pad[0:H, :] = jnp.zeros((H, MW), F32)
        rpad[pl.ds(S, H), :] = jnp.zeros((H, MW), F32)
        dwp_ref[...] = jnp.zeros_like(dwp_ref)
        dsc_ref[...] = jnp.zeros_like(dsc_ref)

        def fill(i, c):
            pad[pl.ds(pl.multiple_of(i * CH + H, 8), CH), :] = p_ref[_rows(i, CH), :].astype(F32)
            return c

        lax.fori_loop(0, NCH, fill, 0)

        def p1(i, c):
            r = _rows(i, CH)
            for g in range(4):
                gs = slice(g * GW, (g + 1) * GW)
                cnt = _pool_count(i, CH, POOL_WINDOWS[g])
                val = pad[pl.ds(pl.multiple_of(i * CH, 8), CH + H), gs]
                pooled = (_box_causal(val, g)[H:, :] / cnt - val[H:, :]).astype(CDT)
                w = wp_ref[g].astype(CDT)
                mixed = _nn(pooled, w)
                dmf = dm_ref[r, gs].astype(F32)
                dsc_ref[0:1, gs] += jnp.sum(dmf * mixed, axis=0, keepdims=True)
                dmx = (dmf * sc_ref[:, gs]).astype(CDT)
                dwp_ref[g] += _tn(pooled, dmx)
                rpad[r, gs] = _nt(dmx, w) / cnt
            return c

        lax.fori_loop(0, NCH, p1, 0)

        def p2(i, c):
            r = _rows(i, CH)
            for g in range(4):
                gs = slice(g * GW, (g + 1) * GW)
                val = rpad[pl.ds(pl.multiple_of(i * CH, 8), CH + H), gs]
                dp = _box_anti(val, g)[:CH, :] - val[:CH, :] * _pool_count(i, CH, POOL_WINDOWS[g])
                dz_ref[r, gs] = dp.astype(CDT)
            return c

        lax.fori_loop(0, NCH, p2, 0)

    return _call(
        body, "mixB_bwd", (1,),
        [pl.BlockSpec((None, S, MW), lambda i: (3, 0, 0)), pl.BlockSpec((None, S, MW), lambda i: (1, 0, 0)),
         pl.BlockSpec((4, GW, GW), lambda i: (0, 0, 0)), _spspec("pool_scale", MW, _zero)],
        [pl.BlockSpec((None, S, MW), lambda i: (0, 0, 0)), pl.BlockSpec((4, GW, GW), lambda i: (0, 0, 0)),
         pl.BlockSpec((8, MW), lambda i: (0, 0))],
        [SDS((1, S, MW), CDT), SDS((4, GW, GW), F32), SDS((8, MW), F32)],
        [pltpu.VMEM((H + S, MW), F32), pltpu.VMEM((S + H, MW), F32)])(z, dm, wp, sp)


def _sgu_mask():
    ci = lax.broadcasted_iota(jnp.int32, (GW, GW), 0) // CHUNK
    cj = lax.broadcasted_iota(jnp.int32, (GW, GW), 1) // CHUNK
    return cj <= ci


def _mixC_fwd(z, ws, bsc, sp):
    S = z.shape[1]
    RB = min(512, S)

    def body(z_ref, lg_ref, lb_ref, ws_ref, bs_ref, o_ref):
        mask = _sgu_mask()
        gu = _gelu(z_ref[0].astype(F32))
        xh, _ = _ln(_gelu(z_ref[1].astype(F32)))
        vn = (xh * lg_ref[...] + lb_ref[...]).astype(CDT)
        for g in range(4):
            gs = slice(g * GW, (g + 1) * GW)
            wm = jnp.where(mask, ws_ref[g], 0.0).astype(CDT)
            for nb in range(RB // GW):
                rs = slice(nb * GW, (nb + 1) * GW)
                mixed = _nn(wm, vn[rs, gs]) + bs_ref[g]
                o_ref[rs, gs] = (gu[rs, gs] * mixed).astype(CDT)

    return _call(
        body, "mixC_fwd", (S // RB,),
        [pl.BlockSpec((2, RB, MW), lambda i: (2, i, 0)), _spspec("sgu_ln_g", MW, _zero), _spspec("sgu_ln_b", MW, _zero),
         pl.BlockSpec((4, GW, GW), lambda i: (0, 0, 0)), pl.BlockSpec((4, GW, 1), lambda i: (0, 0, 0))],
        pl.BlockSpec((RB, MW), lambda i: (i, 0)), SDS((S, MW), CDT))(z, sp, sp, ws, bsc)


def _mixC_bwd(z, dm, ws, bsc, sp):
    S = z.shape[1]
    RB = min(512, S)
    NR = S // RB

    def body(z_ref, dm_ref, lg_ref, lb_ref, ws_ref, bs_ref, dz_ref, dws_ref, dbs_ref, gp_ref, dvn_s):
        i = pl.program_id(0)

        @pl.when(i == 0)
        def _():
            dws_ref[...] = jnp.zeros_like(dws_ref)
            dbs_ref[...] = jnp.zeros_like(dbs_ref)
            gp_ref[...] = jnp.zeros_like(gp_ref)

        mask = _sgu_mask()
        gu, dgu = _gelu_parts(z_ref[0].astype(F32))
        gv, dgv = _gelu_parts(z_ref[1].astype(F32))
        xh, rs_ = _ln(gv)
        vn = (xh * lg_ref[...] + lb_ref[...]).astype(CDT)
        dmf = dm_ref[...].astype(F32)
        for g in range(4):
            gs = slice(g * GW, (g + 1) * GW)
            wm = jnp.where(mask, ws_ref[g], 0.0).astype(CDT)
            for nb in range(RB // GW):
                rs = slice(nb * GW, (nb + 1) * GW)
                vb = vn[rs, gs]
                mixed = _nn(wm, vb) + bs_ref[g]
                dz_ref[0, rs, gs] = (dmf[rs, gs] * mixed * dgu[rs, gs]).astype(CDT)
                dmx = dmf[rs, gs] * gu[rs, gs]
                dbs_ref[g] += dmx
                dmxc = dmx.astype(CDT)
                dws_ref[g] += _nt(dmxc, vb)
                dvn_s[rs, gs] = _tn(wm, dmxc)
        dvn = dvn_s[...]
        gp_ref[0:1, :] += jnp.sum(dvn * xh, axis=0, keepdims=True)
        gp_ref[1:2, :] += jnp.sum(dvn, axis=0, keepdims=True)
        dz_ref[1] = (_ln_bwd(xh, rs_, dvn * lg_ref[...]) * dgv).astype(CDT)

        @pl.when(i == NR - 1)
        def _():
            for g in range(4):
                dws_ref[g] = jnp.where(mask, dws_ref[g], 0.0)
                dbs_ref[g] = jnp.broadcast_to(jnp.sum(dbs_ref[g], axis=1, keepdims=True), (GW, GW))

    full3 = lambda i: (0, 0, 0)
    return _call(
        body, "mixC_bwd", (NR,),
        [pl.BlockSpec((2, RB, MW), lambda i: (2, i, 0)), pl.BlockSpec((None, RB, MW), lambda i: (2, i, 0)),
         _spspec("sgu_ln_g", MW, _zero), _spspec("sgu_ln_b", MW, _zero),
         pl.BlockSpec((4, GW, GW), full3), pl.BlockSpec((4, GW, 1), full3)],
        [pl.BlockSpec((2, RB, MW), lambda i: (0, i, 0)), pl.BlockSpec((4, GW, GW), full3), pl.BlockSpec((4, GW, GW), full3),
         pl.BlockSpec((8, MW), lambda i: (0, 0))],
        [SDS((2, S, MW), CDT), SDS((4, GW, GW), F32), SDS((4, GW, GW), F32), SDS((8, MW), F32)],
        [pltpu.VMEM((RB, MW), F32)])(z, dm, sp, sp, ws, bsc)


def _unpack_wb(wb_ref, wbf):
    for j in range(NS):
        for k in range(4):
            wbf[k, :, j * GW:(j + 1) * GW] = wb_ref[j, k]


def _merge_fwd(ma, mb, mc, yd, g, wb, pwo, x, sp):
    S, D = x.shape
    TM = min(256, S)

    def body(ma_ref, mb_ref, mc_ref, yd_ref, g_ref, wb_ref, wo_ref, x_ref, lg_ref, lb_ref, qg_ref,
             xo_ref, md_ref, yk_ref, mg_ref, mo_ref, wbf):
        @pl.when(pl.program_id(0) == 0)
        def _():
            _unpack_wb(wb_ref, wbf)

        xh, _ = _ln(yd_ref[...].astype(F32))
        sl, _ = _silu_parts(xh * lg_ref[...] + lb_ref[...])
        md = sl.astype(CDT)
        md_ref[...] = md
        merged = jnp.zeros((TM, D), F32)
        for k, m in enumerate((ma_ref[...], mb_ref[...], mc_ref[...], md)):
            yk = _nn(m, wbf[k])
            yk_ref[k] = yk.astype(CDT)
            merged = merged + g_ref[k].astype(F32) * yk
        mgc = merged.astype(CDT)
        mg_ref[...] = mgc
        mo = _nn(mgc, wo_ref[...].reshape(D, D))
        mo_ref[...] = mo.astype(CDT)
        n, _ = _rms(mo)
        xo_ref[...] = x_ref[...] + n * qg_ref[...]

    row = lambda i: (i, 0)
    rowm = pl.BlockSpec((TM, MW), row)
    rowd = pl.BlockSpec((TM, D), row)
    row4 = pl.BlockSpec((4, TM, D), lambda i: (0, i, 0))
    return _call(
        body, "merge_fwd", (S // TM,),
        [rowm, rowm, rowm, rowm, row4, pl.BlockSpec((NS, 4, MW, GW), lambda i: (0, 0, 0, 0)),
         pl.BlockSpec((NS, GW, D), lambda i: (0, 0, 0)), rowd,
         _spspec("conv_d_ln_g", MW, _zero), _spspec("conv_d_ln_b", MW, _zero), _spspec("mix_post_g", D, _zero)],
        [rowd, rowm, row4, rowd, rowd],
        [SDS((S, D), F32), SDS((S, MW), CDT), SDS((4, S, D), CDT), SDS((S, D), CDT), SDS((S, D), CDT)],
        [pltpu.VMEM((4, MW, D), CDT)])(ma, mb, mc, yd, g, wb, pwo, x, sp, sp, sp)


def _merge_bwd_act(dxo, mo, g, yk, wb, pwo, sp, dep):
    S, D = dxo.shape
    TM = min(256, S)

    def body(dxo_ref, mo_ref, g_ref, yk_ref, wb_ref, wo_ref, qg_ref, dep_ref, dmo_ref, dm_ref, dgp_ref, dyk_ref, gp_ref, wbf):
        @pl.when(pl.program_id(0) == 0)
        def _():
            gp_ref[...] = jnp.zeros_like(gp_ref)
            _unpack_wb(wb_ref, wbf)

        n, r = _rms(mo_ref[...].astype(F32))
        dmo, dg = _rms_bwd(n, r, qg_ref[...], dxo_ref[...])
        gp_ref[0:1, :] += dg
        dmoc = dmo.astype(CDT)
        dmo_ref[...] = dmoc
        dmg = _nt(dmoc, wo_ref[...].reshape(D, D))
        for k in range(4):
            gk = g_ref[k].astype(F32)
            dyk = (dmg * gk).astype(CDT)
            dyk_ref[k] = dyk
            dgp_ref[k] = (dmg * yk_ref[k].astype(F32) * gk * (1.0 - gk)).astype(CDT)
            dm_ref[k] = _nt(dyk, wbf[k]).astype(CDT)

    rowd = pl.BlockSpec((TM, D), lambda i: (i, 0))
    row4 = pl.BlockSpec((4, TM, D), lambda i: (0, i, 0))
    return _call(
        body, "merge_bwd_act", (S // TM,),
        [rowd, rowd, row4, row4, pl.BlockSpec((NS, 4, MW, GW), lambda i: (0, 0, 0, 0)),
         pl.BlockSpec((NS, GW, D), lambda i: (0, 0, 0)), _spspec("mix_post_g", D, _zero), _ANY],
        [rowd, pl.BlockSpec((4, TM, MW), lambda i: (0, i, 0)), row4, row4, pl.BlockSpec((8, D), lambda i: (0, 0))],
        [SDS((S, D), CDT), SDS((4, S, MW), CDT), SDS((4, S, D), CDT), SDS((4, S, D), CDT), SDS((8, D), F32)],
        [pltpu.VMEM((4, MW, D), CDT)])(dxo, mo, g, yk, wb, pwo, sp, dep)


def _merge_bwd_w(ma, mb, mc, md, dyk, mg, dmo):
    S, D = dmo.shape
    TK = min(512, S)
    NK = S // TK

    def body(ma_ref, mb_ref, mc_ref, md_ref, dyk_ref, mg_ref, dmo_ref, gwb_ref, gwo_ref, accb, acco):
        k = pl.program_id(0)

        @pl.when(k == 0)
        def _():
            accb[...] = jnp.zeros_like(accb)
            acco[...] = jnp.zeros_like(acco)

        for b, m in enumerate((ma_ref, mb_ref, mc_ref, md_ref)):
            accb[b] += _tn(m[...], dyk_ref[b])
        acco[...] += _tn(mg_ref[...], dmo_ref[...])

        @pl.when(k == NK - 1)
        def _():
            for j in range(NS):
                for b in range(4):
                    gwb_ref[j, b] = accb[b, :, j * GW:(j + 1) * GW].astype(CDT)
                gwo_ref[j] = acco[j * GW:(j + 1) * GW, :].astype(CDT)

    rowm = pl.BlockSpec((TK, MW), lambda k: (k, 0))
    rowd = pl.BlockSpec((TK, D), lambda k: (k, 0))
    return _call(
        body, "merge_bwd_w", (NK,),
        [rowm, rowm, rowm, rowm, pl.BlockSpec((4, TK, D), lambda k: (0, k, 0)), rowd, rowd],
        [pl.BlockSpec((NS, 4, MW, GW), lambda k: (0, 0, 0, 0)), pl.BlockSpec((NS, GW, D), lambda k: (0, 0, 0))],
        [SDS((NS, 4, MW, GW), CDT), SDS((NS, GW, D), CDT)],
        [pltpu.VMEM((4, MW, D), F32), pltpu.VMEM((D, D), F32)])(ma, mb, mc, md, dyk, mg, dmo)


def _xa_kv(mem, sp, pxa):
    M, D = mem.shape

    def body(m_ref, g_ref, wk_ref, wv_ref, mn_ref, k_ref, v_ref):
        n, _ = _rms(m_ref[...])
        mn = (n * g_ref[...]).astype(CDT)
        mn_ref[...] = mn
        k_ref[...] = _nn(mn, wk_ref[...].reshape(D, D)).astype(CDT)
        v_ref[...] = _nn(mn, wv_ref[...].reshape(D, D)).astype(CDT)

    full = pl.BlockSpec((M, D), lambda i: (0, 0))
    return _call(
        body, "xa_kv", (1,),
        [full, _spspec("mem_g", D, _zero), pl.BlockSpec((NS, GW, D), lambda i: (0, 1, 0)),
         pl.BlockSpec((NS, GW, D), lambda i: (0, 2, 0))],
        [full, full, full], [SDS((M, D), CDT)] * 3)(mem, sp, pxa, pxa)


def _softmax(s):
    e = jnp.exp(s - jnp.max(s, axis=-1, keepdims=True))
    return e / jnp.sum(e, axis=-1, keepdims=True)


def _xa_fwd(x, kk, vv, sp, pxa):
    S, D = x.shape
    M = kk.shape[0]
    TM = min(512, S)
    HD = D // XA_HEADS
    scale = HD ** -0.5

    def body(x_ref, k_ref, v_ref, pg_ref, qg_ref, wq_ref, wo_ref, xo_ref, hb_ref, q_ref, o_ref, po_ref):
        n, _ = _rms(x_ref[...])
        hb = (n * pg_ref[...]).astype(CDT)
        hb_ref[...] = hb
        q = _nn(hb, wq_ref[...].reshape(D, D)).astype(CDT)
        q_ref[...] = q
        for h in range(XA_HEADS):
            hs = slice(h * HD, (h + 1) * HD)
            p = _softmax(_nt(q[:, hs], k_ref[:, hs]) * scale)
            o_ref[:, hs] = _nn(p.astype(CDT), v_ref[:, hs]).astype(CDT)
        po = _nn(o_ref[...], wo_ref[...].reshape(D, D))
        po_ref[...] = po.astype(CDT)
        n, _ = _rms(po)
        xo_ref[...] = x_ref[...] + n * qg_ref[...]

    row = pl.BlockSpec((TM, D), lambda i: (i, 0))
    full = pl.BlockSpec((M, D), lambda i: (0, 0))
    return _call(
        body, "xa_fwd", (S // TM,),
        [row, full, full, _spspec("xa_pre_g", D, _zero), _spspec("xa_post_g", D, _zero),
         pl.BlockSpec((NS, GW, D), lambda i: (0, 0, 0)), pl.BlockSpec((NS, GW, D), lambda i: (0, 3, 0))],
        [row] * 5, [SDS((S, D), F32)] + [SDS((S, D), CDT)] * 4)(x, kk, vv, sp, sp, pxa, pxa)


def _xa_bwd_act(dxo, x, po, q, kk, vv, sp, pxa, dep):
    S, D = x.shape
    M = kk.shape[0]
    TM = min(512, S)
    HD = D // XA_HEADS
    scale = HD ** -0.5

    def body(dxo_ref, x_ref, po_ref, q_ref, k_ref, v_ref, pg_ref, qg_ref, wq_ref, wo_ref, dep_ref,
             dx_ref, dpo_ref, dq_ref, dk_ref, dv_ref, gp_ref):
        @pl.when(pl.program_id(0) == 0)
        def _():
            gp_ref[...] = jnp.zeros_like(gp_ref)
            dk_ref[...] = jnp.zeros_like(dk_ref)
            dv_ref[...] = jnp.zeros_like(dv_ref)

        n, r = _rms(po_ref[...].astype(F32))
        dpo, dg = _rms_bwd(n, r, qg_ref[...], dxo_ref[...])
        gp_ref[1:2, :] += dg
        dpoc = dpo.astype(CDT)
        dpo_ref[...] = dpoc
        do = _nt(dpoc, wo_ref[...].reshape(D, D)).astype(CDT)
        for h in range(XA_HEADS):
            hs = slice(h * HD, (h + 1) * HD)
            qh = q_ref[:, hs]
            p = _softmax(_nt(qh, k_ref[:, hs]) * scale)
            pc = p.astype(CDT)
            dv_ref[:, hs] += _tn(pc, do[:, hs])
            dp = _nt(do[:, hs], v_ref[:, hs])
            ds = (p * (dp - jnp.sum(p * dp, axis=-1, keepdims=True)) * scale).astype(CDT)
            dq_ref[:, hs] = _nn(ds, k_ref[:, hs]).astype(CDT)
            dk_ref[:, hs] += _tn(ds, qh)
        dhb = _nt(dq_ref[...], wq_ref[...].reshape(D, D))
        n, r = _rms(x_ref[...])
        dx, dg = _rms_bwd(n, r, pg_ref[...], dhb)
        dx_ref[...] = dxo_ref[...] + dx
        gp_ref[0:1, :] += dg

    row = pl.BlockSpec((TM, D), lambda i: (i, 0))
    full = pl.BlockSpec((M, D), lambda i: (0, 0))
    return _call(
        body, "xa_bwd_act", (S // TM,),
        [row, row, row, row, full, full, _spspec("xa_pre_g", D, _zero), _spspec("xa_post_g", D, _zero),
         pl.BlockSpec((NS, GW, D), lambda i: (0, 0, 0)), pl.BlockSpec((NS, GW, D), lambda i: (0, 3, 0)), _ANY],
        [row, row, row, full, full, pl.BlockSpec((8, D), lambda i: (0, 0))],
        [SDS((S, D), F32), SDS((S, D), CDT), SDS((S, D), CDT), SDS((M, D), F32), SDS((M, D), F32), SDS((8, D), F32)],
    )(dxo, x, po, q, kk, vv, sp, sp, pxa, pxa, dep)


def _xa_bwd_w(hb, dq, o, dpo, mn, dk, dv):
    S, D = hb.shape
    M = mn.shape[0]
    TK = min(512, S)
    NK = S // TK

    def body(hb_ref, dq_ref, o_ref, dpo_ref, mn_ref, dk_ref, dv_ref, g_ref, accq, acco):
        k = pl.program_id(0)

        @pl.when(k == 0)
        def _():
            accq[...] = jnp.zeros_like(accq)
            acco[...] = jnp.zeros_like(acco)

        accq[...] += _tn(hb_ref[...], dq_ref[...])
        acco[...] += _tn(o_ref[...], dpo_ref[...])

        @pl.when(k == NK - 1)
        def _():
            gk = _tn(mn_ref[...], dk_ref[...].astype(CDT))
            gv = _tn(mn_ref[...], dv_ref[...].astype(CDT))
            for j in range(NS):
                rs = slice(j * GW, (j + 1) * GW)
                g_ref[j, 0:GW, :] = accq[rs, :].astype(CDT)
                g_ref[j, GW:2 * GW, :] = gk[rs, :].astype(CDT)
                g_ref[j, 2 * GW:3 * GW, :] = gv[rs, :].astype(CDT)
                g_ref[j, 3 * GW:4 * GW, :] = acco[rs, :].astype(CDT)

    rowb = pl.BlockSpec((TK, D), lambda k: (k, 0))
    full = pl.BlockSpec((M, D), lambda k: (0, 0))
    return _call(
        body, "xa_bwd_w", (NK,),
        [rowb, rowb, rowb, rowb, full, full, full],
        pl.BlockSpec((NS, 4 * GW, D), lambda k: (0, 0, 0)), SDS((NS, 4 * GW, D), CDT),
        [pltpu.VMEM((D, D), F32), pltpu.VMEM((D, D), F32)])(hb, dq, o, dpo, mn, dk, dv)


def _xa_kv_bwd(mem, dk, dv, sp, pxa):
    M, D = mem.shape

    def body(m_ref, dk_ref, dv_ref, wk_ref, wv_ref, gp_ref):
        dmn = _nt(dk_ref[...].astype(CDT), wk_ref[...].reshape(D, D)) + _nt(dv_ref[...].astype(CDT), wv_ref[...].reshape(D, D))
        n, _ = _rms(m_ref[...])
        gp_ref[...] = jnp.zeros_like(gp_ref)
        gp_ref[0:1, :] = jnp.sum(dmn * n, axis=0, keepdims=True)

    full = pl.BlockSpec((M, D), lambda i: (0, 0))
    return _call(
        body, "xa_kv_bwd", (1,),
        [full, full, full, pl.BlockSpec((NS, GW, D), lambda i: (0, 1, 0)), pl.BlockSpec((NS, GW, D), lambda i: (0, 2, 0))],
        pl.BlockSpec((8, D), lambda i: (0, 0)), SDS((8, D), F32))(mem, dk, dv, pxa, pxa)


def _loss_head(y, t):
    S, D = y.shape
    TM = min(512, S)

    def body(y_ref, t_ref, dy_ref, l_ref):
        @pl.when(pl.program_id(0) == 0)
        def _():
            l_ref[...] = jnp.zeros_like(l_ref)

        e = y_ref[...] - t_ref[...]
        dy_ref[...] = e * (1.0 / D)
        l_ref[...] += 0.5 * jnp.sum(jnp.mean(e * e, axis=-1, keepdims=True), axis=0, keepdims=True)

    row = pl.BlockSpec((TM, D), lambda i: (i, 0))
    return _call(body, "loss_head", (S // TM,), [row, row], [row, pl.BlockSpec((8, 128), lambda i: (0, 0))],
                 [SDS((S, D), F32), SDS((8, 128), F32)])(y, t)


def _row_tile(rows, cols, limit=1 << 18, step=8):
    if rows * cols <= limit or rows % step:
        return rows
    best = step
    for t in range(step, rows + 1, step):
        if rows % t == 0 and t * cols <= limit:
            best = t
    return best


def _slot_sum(r):
    _, R, C = r.shape
    TR = _row_tile(R, C * NS, limit=1 << 21, step=16)

    def body(r_ref, o_ref):
        acc = r_ref[0].astype(F32)
        for j in range(1, NS):
            acc = acc + r_ref[j].astype(F32)
        o_ref[...] = acc

    return _call(body, "slot_sum", (R // TR,), [pl.BlockSpec((NS, TR, C), lambda i: (0, i, 0))],
                 pl.BlockSpec((TR, C), lambda i: (i, 0)), SDS((R, C), F32))(r)


def _adamw(w, g, m, v):
    shape = w.shape
    C = shape[-1]
    R = w.size // C
    TR = _row_tile(R, C)
    c1 = 1.0 - ADAM_B1 ** ADAM_STEP
    c2 = 1.0 - ADAM_B2 ** ADAM_STEP

    def body(w_ref, g_ref, m_ref, v_ref, d_ref, nm_ref, nv_ref):
        gg = g_ref[...]
        nm = ADAM_B1 * m_ref[...] + (1.0 - ADAM_B1) * gg
        nv = ADAM_B2 * v_ref[...] + (1.0 - ADAM_B2) * (gg * gg)
        nm_ref[...] = nm
        nv_ref[...] = nv
        d_ref[...] = -ADAM_LR * ((nm / c1) / (jnp.sqrt(nv / c2) + ADAM_EPS) + ADAM_WD * w_ref[...])

    blk = pl.BlockSpec((TR, C), lambda i: (i, 0))
    outs = _call(body, "adamw", (R // TR,), [blk] * 4, [blk] * 3, [SDS((R, C), F32)] * 3)(
        w.reshape(R, C), g.reshape(R, C), m.reshape(R, C), v.reshape(R, C))
    return tuple(o.reshape(shape) for o in outs)


def _adamw_block(w, gs, m, v, gblock):
    L, R, C = w.shape
    c1 = 1.0 - ADAM_B1 ** ADAM_STEP
    c2 = 1.0 - ADAM_B2 ** ADAM_STEP

    def body(w_ref, g_ref, m_ref, v_ref, go_ref, d_ref, nm_ref, nv_ref):
        gg = g_ref[...]
        go_ref[...] = gg
        nm = ADAM_B1 * m_ref[...] + (1.0 - ADAM_B1) * gg
        nv = ADAM_B2 * v_ref[...] + (1.0 - ADAM_B2) * (gg * gg)
        nm_ref[...] = nm
        nv_ref[...] = nv
        d_ref[...] = -ADAM_LR * ((nm / c1) / (jnp.sqrt(nv / c2) + ADAM_EPS) + ADAM_WD * w_ref[...])

    blk = pl.BlockSpec((None, R, C), lambda l: (l, 0, 0))
    return _call(body, "adamw_block", (L,), [blk, pl.BlockSpec((None, R, C), lambda l: (l, gblock, 0)), blk, blk],
                 [blk] * 4, [SDS((L, R, C), F32)] * 4)(w, gs, m, v)


def _slot_sum_into(stacked, r, l):
    _, R, C = r.shape
    TR = _row_tile(R, C * NS, limit=1 << 21, step=16)

    def body(r_ref, s_ref, o_ref):
        acc = r_ref[0].astype(F32)
        for j in range(1, NS):
            acc = acc + r_ref[j].astype(F32)
        o_ref[...] = acc

    return pl.pallas_call(
        body, name="slot_sum_into", grid=(R // TR,),
        in_specs=[pl.BlockSpec((NS, TR, C), lambda i: (0, i, 0)), _ANY],
        out_specs=pl.BlockSpec((None, TR, C), lambda i: (l, i, 0)), out_shape=SDS(stacked.shape, F32),
        input_output_aliases={1: 0},
        compiler_params=pltpu.CompilerParams(dimension_semantics=("arbitrary",), vmem_limit_bytes=VMEM_LIMIT))(r, stacked)


def _exchange(arrs, scatter, name):
    n = len(arrs)
    np_ = NS - 1

    def body(*refs):
        ins, outs = refs[:n], refs[n:2 * n]
        send_sems, recv_sems, loc_sems = refs[2 * n:]
        x, y, c = lax.axis_index("x"), lax.axis_index("y"), lax.axis_index("c")
        me = 4 * x + 2 * y + c
        peers = []
        for f in range(1, NS):
            px = 1 - x if f & 4 else x
            py = 1 - y if f & 2 else y
            pc = 1 - c if f & 1 else c
            peers.append(((px, py, pc), 4 * px + 2 * py + pc))

        def src(a, pid):
            return ins[a].at[pid] if scatter else ins[a]

        local = [pltpu.make_async_copy(src(a, me), outs[a].at[me], loc_sems.at[a]) for a in range(n)]
        for cp in local:
            cp.start()
        sends = []
        for a in range(n):
            for f, (dev, pid) in enumerate(peers):
                sends.append(pltpu.make_async_remote_copy(
                    src_ref=src(a, pid), dst_ref=outs[a].at[me], send_sem=send_sems.at[a * np_ + f],
                    recv_sem=recv_sems.at[a * np_ + f], device_id=dev, device_id_type=pl.DeviceIdType.MESH))
        for cp in sends:
            cp.start()
        for a in range(n):
            for f, (dev, pid) in enumerate(peers):
                pltpu.make_async_remote_copy(
                    src_ref=src(a, pid), dst_ref=outs[a].at[pid], send_sem=send_sems.at[a * np_ + f],
                    recv_sem=recv_sems.at[a * np_ + f], device_id=dev, device_id_type=pl.DeviceIdType.MESH).wait_recv()
        for cp in sends:
            cp.wait_send()
        for cp in local:
            cp.wait()

    out_shape = [SDS(a.shape if scatter else (NS,) + a.shape, a.dtype) for a in arrs]
    anyspec = pl.BlockSpec(memory_space=pl.ANY)
    outs = pl.pallas_call(
        body, name=name, in_specs=[anyspec] * n, out_specs=[anyspec] * n, out_shape=out_shape,
        scratch_shapes=[pltpu.SemaphoreType.DMA((n * np_,)), pltpu.SemaphoreType.DMA((n * np_,)),
                        pltpu.SemaphoreType.DMA((n,))],
        compiler_params=pltpu.CompilerParams(has_side_effects=True))(*arrs)
    return list(outs)


def _peers():
    x, y, c = lax.axis_index("x"), lax.axis_index("y"), lax.axis_index("c")
    out = []
    for f in range(1, NS):
        px = 1 - x if f & 4 else x
        py = 1 - y if f & 2 else y
        pc = 1 - c if f & 1 else c
        out.append(((px, py, pc), 4 * px + 2 * py + pc))
    return 4 * x + 2 * y + c, out


def _exchange_copies(ins, lands, scatter, send_sems, recv_sems, loc_sems):
    me, peers = _peers()
    np_ = NS - 1

    def src(a, pid):
        return ins[a].at[pid] if scatter[a] else ins[a]

    def rcopy(a, f, dev, land_slot):
        return pltpu.make_async_remote_copy(
            src_ref=src(a, peers[f][1]), dst_ref=lands[a].at[land_slot], send_sem=send_sems.at[a * np_ + f],
            recv_sem=recv_sems.at[a * np_ + f], device_id=dev, device_id_type=pl.DeviceIdType.MESH)

    local = [pltpu.make_async_copy(src(a, me), lands[a].at[me], loc_sems.at[a]) for a in range(len(ins))]
    sends = [rcopy(a, f, dev, me) for a in range(len(ins)) for f, (dev, _) in enumerate(peers)]
    arrivals = [rcopy(a, f, dev, pid) for a in range(len(ins)) for f, (dev, pid) in enumerate(peers)]
    return local, sends, arrivals


_HBM = pl.BlockSpec(memory_space=pltpu.HBM)
_SEM = pl.BlockSpec(memory_space=pltpu.SEMAPHORE)
_ANY = pl.BlockSpec(memory_space=pl.ANY)


def _exchange_start(arrs, scatter, after, name):
    n = len(arrs)
    np_ = NS - 1
    lands = [lax.empty(a.shape if sc else (NS,) + a.shape, a.dtype) for a, sc in zip(arrs, scatter)]

    def body(*refs):
        ins, lnd = refs[:n], refs[n:2 * n]
        send_sems, recv_sems, loc_sems = refs[2 * n + 1:2 * n + 4]
        token = refs[-1]
        local, sends, _ = _exchange_copies(ins, lnd, scatter, send_sems, recv_sems, loc_sems)
        for cp in local + sends:
            cp.start()
        token[...] = jnp.zeros_like(token)

    hbm = lambda a: pltpu.HBM(a.shape, a.dtype)
    outs = pl.pallas_call(
        body, name=name,
        out_shape=(pltpu.SemaphoreType.DMA((n * np_,)), pltpu.SemaphoreType.DMA((n * np_,)), pltpu.SemaphoreType.DMA((n,)),
                   *[hbm(a) for a in arrs], *[hbm(a) for a in lands], SDS((8, 128), F32)),
        in_specs=[_HBM] * (2 * n) + [_ANY],
        out_specs=(_SEM, _SEM, _SEM, *([_HBM] * (2 * n)), pl.BlockSpec(memory_space=pltpu.VMEM)),
        input_output_aliases={i: 3 + i for i in range(2 * n)},
        compiler_params=pltpu.CompilerParams(has_side_effects=pltpu.SideEffectType.DATAFLOW_SIDE_EFFECTING),
    )(*[pltpu.with_memory_space_constraint(a, pltpu.HBM) for a in list(arrs) + lands], after)
    return {"sems": outs[:3], "ins": outs[3:3 + n], "lands": outs[3 + n:3 + 2 * n], "token": outs[-1], "scatter": scatter}


def _exchange_wait(h, after, name):
    n = len(h["ins"])
    scatter = h["scatter"]
    after = list(after) if isinstance(after, (list, tuple)) else [after]

    def body(*refs):
        ins, lnd = refs[:n], refs[n:2 * n]
        send_sems, recv_sems, loc_sems = refs[2 * n:2 * n + 3]
        local, sends, arrivals = _exchange_copies(ins, lnd, scatter, send_sems, recv_sems, loc_sems)
        for cp in sends:
            cp.wait_send()
        for cp in arrivals:
            cp.wait_recv()
        for cp in local:
            cp.wait()

    hbm = lambda a: pltpu.HBM(a.shape, a.dtype)
    outs = pl.pallas_call(
        body, name=name,
        out_shape=tuple(hbm(a) for a in list(h["ins"]) + list(h["lands"])),
        in_specs=[_HBM] * (2 * n) + [_SEM] * 3 + [_ANY] * len(after),
        out_specs=tuple([_HBM] * (2 * n)),
        input_output_aliases={i: i for i in range(2 * n)},
        compiler_params=pltpu.CompilerParams(has_side_effects=pltpu.SideEffectType.DATAFLOW_SIDE_EFFECTING),
    )(*h["ins"], *h["lands"], *h["sems"], *after)
    return list(outs[n:])


def _hgather_copies(ins, lands, send_a, recv_a, send_b, recv_b, loc_sems):
    x, y, c = lax.axis_index("x"), lax.axis_index("y"), lax.axis_index("c")
    me = 4 * x + 2 * y + c
    sib = (x, y, 1 - c)
    chips = [(1 - x, y), (x, 1 - y), (1 - x, 1 - y)]
    slot = lambda px, py, pc: 4 * px + 2 * py + pc

    def rcopy(src, dst, ssem, rsem, dev):
        return pltpu.make_async_remote_copy(src_ref=src, dst_ref=dst, send_sem=ssem, recv_sem=rsem, device_id=dev,
                                            device_id_type=pl.DeviceIdType.MESH)

    local, s1, a1, s2, a2 = [], [], [], [], []
    for a in range(len(ins)):
        first = [(sib, slot(x, y, 1 - c))] + [((px, py, c), slot(px, py, c)) for px, py in chips]
        for k, (dev, origin) in enumerate(first if send_a is not None else ()):
            s1.append(rcopy(ins[a], lands[a].at[me], send_a.at[4 * a + k], recv_a.at[4 * a + k], dev))
            a1.append(rcopy(ins[a], lands[a].at[origin], send_a.at[4 * a + k], recv_a.at[4 * a + k], dev))
        if send_a is not None:
            local.append(pltpu.make_async_copy(ins[a], lands[a].at[me], loc_sems.at[a]))
        for k, (px, py) in enumerate(chips if send_b is not None else ()):
            mine, theirs = lands[a].at[slot(px, py, c)], lands[a].at[slot(px, py, 1 - c)]
            s2.append(rcopy(mine, mine, send_b.at[3 * a + k], recv_b.at[3 * a + k], sib))
            a2.append(rcopy(mine, theirs, send_b.at[3 * a + k], recv_b.at[3 * a + k], sib))
    return local, s1, a1, s2, a2


def _hgather_start(arrs, after, name):
    n = len(arrs)
    lands = [lax.empty((NS,) + a.shape, a.dtype) for a in arrs]

    def body(*refs):
        ins, lnd = refs[:n], refs[n:2 * n]
        send_a, recv_a, loc_sems = refs[2 * n + 1:2 * n + 4]
        token = refs[-1]
        local, s1, _, _, _ = _hgather_copies(ins, lnd, send_a, recv_a, None, None, loc_sems)
        for cp in local + s1:
            cp.start()
        token[...] = jnp.zeros_like(token)

    hbm = lambda a: pltpu.HBM(a.shape, a.dtype)
    outs = pl.pallas_call(
        body, name=name,
        out_shape=(pltpu.SemaphoreType.DMA((4 * n,)), pltpu.SemaphoreType.DMA((4 * n,)), pltpu.SemaphoreType.DMA((n,)),
                   *[hbm(a) for a in arrs], *[hbm(a) for a in lands], SDS((8, 128), F32)),
        in_specs=[_HBM] * (2 * n) + [_ANY],
        out_specs=(_SEM, _SEM, _SEM, *([_HBM] * (2 * n)), pl.BlockSpec(memory_space=pltpu.VMEM)),
        input_output_aliases={i: 3 + i for i in range(2 * n)},
        compiler_params=pltpu.CompilerParams(has_side_effects=pltpu.SideEffectType.DATAFLOW_SIDE_EFFECTING),
    )(*[pltpu.with_memory_space_constraint(a, pltpu.HBM) for a in list(arrs) + lands], after)
    return {"sems": outs[:3], "ins": outs[3:3 + n], "lands": outs[3 + n:3 + 2 * n], "token": outs[-1]}


def _hgather_forward(h, after, name):
    n = len(h["ins"])

    def body(*refs):
        ins, lnd = refs[:n], refs[n:2 * n]
        send_a, recv_a, loc_sems = refs[2 * n:2 * n + 3]
        send_b, recv_b = refs[2 * n + 4:2 * n + 6]
        token = refs[-1]
        local, s1, a1, s2, _ = _hgather_copies(ins, lnd, send_a, recv_a, send_b, recv_b, loc_sems)
        for cp in s1:
            cp.wait_send()
        for cp in a1:
            cp.wait_recv()
        for cp in local:
            cp.wait()
        for cp in s2:
            cp.start()
        token[...] = jnp.zeros_like(token)

    hbm = lambda a: pltpu.HBM(a.shape, a.dtype)
    outs = pl.pallas_call(
        body, name=name,
        out_shape=(pltpu.SemaphoreType.DMA((3 * n,)), pltpu.SemaphoreType.DMA((3 * n,)),
                   *[hbm(a) for a in list(h["ins"]) + list(h["lands"])], SDS((8, 128), F32)),
        in_specs=[_HBM] * (2 * n) + [_SEM] * 3 + [_ANY],
        out_specs=(_SEM, _SEM, *([_HBM] * (2 * n)), pl.BlockSpec(memory_space=pltpu.VMEM)),
        input_output_aliases={i: 2 + i for i in range(2 * n)},
        compiler_params=pltpu.CompilerParams(has_side_effects=pltpu.SideEffectType.DATAFLOW_SIDE_EFFECTING),
    )(*h["ins"], *h["lands"], *h["sems"], after)
    return {"sems": outs[:2], "ins": outs[2:2 + n], "lands": outs[2 + n:2 + 2 * n], "token": outs[-1]}


def _hgather_wait(h, after, name):
    n = len(h["ins"])

    def body(*refs):
        ins, lnd = refs[:n], refs[n:2 * n]
        send_b, recv_b = refs[2 * n:2 * n + 2]
        _, _, _, s2, a2 = _hgather_copies(ins, lnd, None, None, send_b, recv_b, None)
        for cp in s2:
            cp.wait_send()
        for cp in a2:
            cp.wait_recv()

    hbm = lambda a: pltpu.HBM(a.shape, a.dtype)
    outs = pl.pallas_call(
        body, name=name,
        out_shape=tuple(hbm(a) for a in list(h["ins"]) + list(h["lands"])),
        in_specs=[_HBM] * (2 * n) + [_SEM] * 2 + [_ANY],
        out_specs=tuple([_HBM] * (2 * n)),
        input_output_aliases={i: i for i in range(2 * n)},
        compiler_params=pltpu.CompilerParams(has_side_effects=pltpu.SideEffectType.DATAFLOW_SIDE_EFFECTING),
    )(*h["ins"], *h["lands"], *h["sems"], after)
    return list(outs[n:])


_W_NAMES = ("ffn1_pre_g", "ffn1_post_g", "ffn1_w1", "ffn1_w3", "ffn1_w2", "mix_pre_g", "mix_post_g", "w_in", "conv_a_w",
            "conv_a_b", "pool_w", "pool_scale", "sgu_ln_g", "sgu_ln_b", "sgu_ws", "sgu_b", "conv_d_w", "conv_d_b",
            "conv_d_ln_g", "conv_d_ln_b", "w_branch", "w_gate", "b_gate", "w_o", "xa_pre_g", "xa_post_g", "mem_g",
            "xa_wq", "xa_wk", "xa_wv", "xa_wo", "ffn2_pre_g", "ffn2_post_g", "ffn2_w1", "ffn2_w3", "ffn2_w2")
_REP_NAMES = tuple(n for n, _ in _SP_NAMES) + ("pool_w", "sgu_ws", "sgu_b", "conv_a_w", "conv_d_w")


def _t(w):
    return jnp.swapaxes(w, -1, -2)


def _step(x, mem, loss_target, W, M, V):
    L = W["w_in"].shape[0]
    S, D = x.shape[1], x.shape[2]
    x0 = x.reshape(S, D)
    memf = mem.reshape(mem.shape[1], D)
    me = 4 * lax.axis_index("x") + 2 * lax.axis_index("y") + lax.axis_index("c")
    FS = W["ffn1_w2"].shape[1]
    KA, KD = W["conv_a_w"].shape[1], W["conv_d_w"].shape[1]
    CS = W["conv_a_w"].shape[2]

    cat = lambda l, parts: jnp.concatenate([(_t(W[n][l]) if tr else W[n][l]) for n, tr in parts], axis=0).astype(CDT)
    pf1 = [cat(l, (("ffn1_w1", 1), ("ffn1_w3", 1), ("ffn1_w2", 0))) for l in range(L)]
    pf2 = [cat(l, (("ffn2_w1", 1), ("ffn2_w3", 1), ("ffn2_w2", 0))) for l in range(L)]
    pma = [cat(l, (("w_in", 1), ("w_gate", 1))) for l in range(L)]
    pwo = [W["w_o"][l].astype(CDT) for l in range(L)]
    pxa = [cat(l, (("xa_wq", 0), ("xa_wk", 0), ("xa_wv", 0), ("xa_wo", 0))) for l in range(L)]
    wbs = [W["w_branch"][l].astype(CDT) for l in range(L)]
    cws = jnp.concatenate([W["conv_a_w"], W["conv_d_w"]], axis=1).reshape(-1, 128)
    sp_all = jnp.concatenate([W[n] for n, _ in _SP_NAMES], axis=1)
    bsc_all = W["sgu_b"][..., None]

    (cwg,) = _exchange([cws], False, "gather_conv_w")
    cwf = cwg.reshape(NS, L, KA + KD, CS).transpose(1, 2, 0, 3).reshape(L, KA + KD, NS * CS)

    def gather_start(l, after):
        return _hgather_start([pf1[l], pf2[l], pma[l], pwo[l], pxa[l], wbs[l]], after, f"gather_start_{l}")

    def gather_rest(h, after, tag):
        mid = _hgather_forward(h, after, f"gather_forward_{tag}")
        return _hgather_wait(mid, mid["token"], f"gather_wait_{tag}")

    packs = [None] * L
    first = [_hgather_start([pf1[0]], cwg, "gather_start_0a")]
    saved = []
    xc = x0
    for l in range(L):
        if l == 0:
            (gf1,) = gather_rest(first[0], sp_all, "0a")
            first.append(_hgather_start([pma[0], pwo[0], wbs[0]], gf1, "gather_start_0b"))
        else:
            gf1, gf2, gma, gwo, gxa, gwb = packs[l]
        sp = sp_all[l:l + 1]
        cwa, cwd = cwf[l, :KA], cwf[l, KA:]
        wp, ws, bsc = W["pool_w"][l], W["sgu_ws"][l], bsc_all[l]
        s = {"x0": xc}
        nxt = gather_start(l + 1, gf1) if 0 < l < L - 1 else None
        xc, s["hb1"], s["a1"], s["b1"], s["y1"] = _ffn_fwd(xc, sp, "ffn1_pre_g", "ffn1_post_g", gf1,
                                                            (first[1] if l == 0 else nxt)["token"] if l == 0 or nxt else sp)
        s["x1"] = xc
        if l == 0:
            gma, gwo, gwb = gather_rest(first[1], xc, "0b")
            first.append(_hgather_start([pxa[0]], gma, "gather_start_0c"))
            first.append(_hgather_start([pf2[0]], first[2]["token"], "gather_start_0d"))
            nxt = gather_start(1, first[3]["token"]) if L > 1 else None
        s["hbm"], s["z"], s["g"] = _mix_in(xc, sp, gma, (nxt or first[3])["token"] if l == 0 else sp)
        s["ma"] = _mixA_fwd(s["z"], cwa, sp)
        s["mb"] = _mixB_fwd(s["z"], wp, sp)
        s["mc"] = _mixC_fwd(s["z"], ws, bsc, sp)
        s["yd"] = _mixD_conv_fwd(s["z"], cwd, sp)
        xc, s["md"], s["yk"], s["mg"], s["mo"] = _merge_fwd(s["ma"], s["mb"], s["mc"], s["yd"], s["g"], gwb, gwo, xc, sp)
        s["x2"] = xc
        if l == 0:
            (gxa,) = gather_rest(first[2], xc, "0c")
        s["mn"], s["k"], s["v"] = _xa_kv(memf, sp, gxa)
        xc, s["hbx"], s["q"], s["o"], s["po"] = _xa_fwd(xc, s["k"], s["v"], sp, gxa)
        s["x3"] = xc
        if l == 0:
            (gf2,) = gather_rest(first[3], xc, "0d")
            packs[0] = (gf1, gf2, gma, gwo, gxa, gwb)
        mid = _hgather_forward(nxt, xc, f"gather_forward_{l + 1}") if nxt and l > 0 else None
        xc, s["hb2"], s["a2"], s["b2"], s["y2"] = _ffn_fwd(xc, sp, "ffn2_pre_g", "ffn2_post_g", gf2,
                                                            mid["token"] if mid else sp)
        saved.append(s)
        if nxt:
            mid = mid or _hgather_forward(nxt, xc, f"gather_forward_{l + 1}")
            packs[l + 1] = _hgather_wait(mid, xc, f"gather_wait_{l + 1}")

    dx, lpart = _loss_head(xc, loss_target.reshape(S, D))
    loss = lax.psum(lpart[0, 0], ("x", "y", "c"))

    rep = {n: [None] * L for n in _REP_NAMES}
    summed = [dict() for _ in range(L)]
    pending = None
    last = []
    for l in reversed(range(L)):
        gf1, gf2, gma, gwo, gxa, gwb = packs[l]
        sp = sp_all[l:l + 1]
        cwa, cwd = cwf[l, :KA], cwf[l, KA:]
        wp, ws, bsc = W["pool_w"][l], W["sgu_ws"][l], bsc_all[l]
        s = saved[l]

        dx, dyb, da, db, gp = _ffn_bwd_act(dx, s["x3"], s["y2"], s["a2"], s["b2"], sp, "ffn2_pre_g", "ffn2_post_g", gf2,
                                           pending[1]["token"] if pending else sp)
        rep["ffn2_pre_g"][l], rep["ffn2_post_g"][l] = gp[0], gp[1]
        d_f2 = _ffn_bwd_w(s["hb2"], dyb, s["a2"], s["b2"], da, db)
        if l == 0:
            last.append(_exchange_start([d_f2], (True,), dx, "scatter_start_0a"))

        dx, dpo, dq, dk, dv, gp = _xa_bwd_act(dx, s["x2"], s["po"], s["q"], s["k"], s["v"], sp, gxa,
                                              last[-1]["token"] if last else sp)
        rep["xa_pre_g"][l], rep["xa_post_g"][l] = gp[0], gp[1]
        d_xa = _xa_bwd_w(s["hbx"], dq, s["o"], dpo, s["mn"], dk, dv)
        rep["mem_g"][l] = _xa_kv_bwd(memf, dk, dv, sp, gxa)[0]
        if l == 0:
            last.append(_exchange_start([d_xa], (True,), dx, "scatter_start_0b"))

        dmo, dm, dgp, dyk, gp = _merge_bwd_act(dx, s["mo"], s["g"], s["yk"], gwb, gwo, sp, last[-1]["token"] if last else sp)
        rep["mix_post_g"][l] = gp[0]
        d_wb, d_wo = _merge_bwd_w(s["ma"], s["mb"], s["mc"], s["md"], dyk, s["mg"], dmo)
        dza, dcw, gp = _mixA_bwd(s["z"], dm, cwa, sp)
        rep["conv_a_w"][l], rep["conv_a_b"][l] = dcw, gp[0]
        dzb, dwp, gp = _mixB_bwd(s["z"], dm, wp, sp)
        rep["pool_w"][l], rep["pool_scale"][l] = dwp, gp[0]
        dzc, dws, dbs, gp = _mixC_bwd(s["z"], dm, ws, bsc, sp)
        rep["sgu_ws"][l], rep["sgu_b"][l], rep["sgu_ln_g"][l], rep["sgu_ln_b"][l] = dws, dbs[:, :, 0], gp[0], gp[1]
        dyd, gp = _mixD_ln_bwd(dm, s["yd"], sp)
        rep["conv_d_ln_g"][l], rep["conv_d_ln_b"][l] = gp[0], gp[1]
        dzd, dcw, gp = _mixD_conv_bwd(s["z"], dyd, cwd)
        rep["conv_d_w"][l], rep["conv_d_b"][l] = dcw, gp[0]
        dz = jnp.concatenate([dza, dzb, dzc, dzd], axis=0)
        dx, gp = _mix_in_bwd_act(dz, dgp, dx, s["x1"], sp, gma)
        rep["mix_pre_g"][l] = gp[0]
        d_ma, dbg = _mix_in_bwd_w(dz, dgp, s["hbm"])
        rep["b_gate"][l] = dbg[:, 0, :].reshape(-1)
        if l == 0:
            last.append(_exchange_start([d_ma, d_wo, d_wb], (True,) * 3, dx, "scatter_start_0c"))

        dx, dyb, da, db, gp = _ffn_bwd_act(dx, s["x0"], s["y1"], s["a1"], s["b1"], sp, "ffn1_pre_g", "ffn1_post_g", gf1,
                                           last[-1]["token"] if last else sp)
        rep["ffn1_pre_g"][l], rep["ffn1_post_g"][l] = gp[0], gp[1]
        d_f1 = _ffn_bwd_w(s["hb1"], dyb, s["a1"], s["b1"], da, db)

        if pending:
            r = _exchange_wait(pending[1], dx, f"scatter_wait_{pending[0]}")
            summed[pending[0]] = dict(zip(("f1", "f2", "ma", "wo", "xa", "wb", "flat"), r))
        flat = jnp.concatenate([rep[n][l].reshape(-1) for n in _REP_NAMES])
        flat = jnp.pad(flat, (0, -flat.size % 2048)).reshape(-1, 128).astype(CDT)
        if l == 0:
            last.append(_exchange_start([d_f1, flat], (True, False), dx, "scatter_start_0d"))
        else:
            pending = (l, _exchange_start([d_f1, d_f2, d_ma, d_wo, d_xa, d_wb, flat], (True,) * 6 + (False,), dx,
                                          f"scatter_start_{l}"))

    shape = {"f1": (3 * FS, D), "f2": (3 * FS, D), "ma": (2 * MW, D), "wo": (GW, D), "xa": (4 * GW, D),
             "wb": (4 * MW, GW), "flat": tuple(flat.shape)}
    stk = {k: lax.empty((L,) + s, F32) for k, s in shape.items()}

    def land(k, r, l):
        stk[k] = _slot_sum_into(stk[k], r.reshape((NS,) + shape[k]), l)

    for l in range(1, L):
        for k, r in summed[l].items():
            land(k, r, l)
    (r,) = _exchange_wait(last[0], dx, "scatter_wait_0a")
    land("f2", r, 0)
    (r,) = _exchange_wait(last[1], dx, "scatter_wait_0b")
    land("xa", r, 0)

    G, deltas, new_m, new_v = {}, {}, {}, {}

    def update_block(n, k, blk, transposed):
        tr = _t if transposed else (lambda a: a)
        out = _adamw_block(tr(W[n]), stk[k], tr(M[n]), tr(V[n]), blk)
        G[n], deltas[n], new_m[n], new_v[n] = (tr(a) for a in out)
        return deltas[n]

    def update(n):
        deltas[n], new_m[n], new_v[n] = _adamw(W[n], G[n], M[n], V[n])
        return deltas[n]

    done = [update_block("ffn2_w1", "f2", 0, True), update_block("ffn2_w3", "f2", 1, True),
            update_block("ffn2_w2", "f2", 2, False)]
    done += [update_block(n, "xa", i, False) for i, n in enumerate(("xa_wq", "xa_wk", "xa_wv", "xa_wo"))]
    r = _exchange_wait(last[2], done + [stk[k] for k in ("f1", "ma", "wo", "wb", "flat")], "scatter_wait_0c")
    for k, v in zip(("ma", "wo", "wb"), r):
        land(k, v, 0)
    G["w_in"], G["w_gate"] = _t(stk["ma"][:, :MW]), _t(stk["ma"][:, MW:])
    G["w_branch"] = stk["wb"].reshape(W["w_branch"].shape)
    done = [update("w_in"), update("w_gate"), update("w_branch"), update_block("w_o", "wo", 0, False)]
    r = _exchange_wait(last[3], done, "scatter_wait_0d")
    land("f1", r[0], 0)
    land("flat", r[1], 0)
    update_block("ffn1_w1", "f1", 0, True)
    update_block("ffn1_w3", "f1", 1, True)
    update_block("ffn1_w2", "f1", 2, False)

    tot = [stk["flat"][l].reshape(-1) for l in range(L)]
    off = 0
    for n in _REP_NAMES:
        shape = (KA, NS * CS) if n == "conv_a_w" else (KD, NS * CS) if n == "conv_d_w" else W[n].shape[1:]
        size = 1
        for d in shape:
            size *= d
        G[n] = jnp.stack([tot[l][off:off + size].reshape(shape) for l in range(L)])
        off += size
    for n in ("conv_a_w", "conv_d_w"):
        G[n] = lax.dynamic_slice_in_dim(G[n], me * CS, CS, axis=2)

    for n in _W_NAMES:
        if n not in deltas:
            deltas[n], new_m[n], new_v[n] = _adamw(W[n], G[n], M[n], V[n])
    grad_x = dx.reshape(x.shape)
    return (loss, grad_x, *[G[n] for n in _W_NAMES], *[deltas[n] for n in _W_NAMES],
            *[new_m[n] for n in _W_NAMES], *[new_v[n] for n in _W_NAMES])


def kernel(x, mem, ffn1_pre_g, ffn1_post_g, ffn1_w1, ffn1_w3, ffn1_w2, mix_pre_g, mix_post_g, w_in, conv_a_w, conv_a_b, pool_w, pool_scale, sgu_ln_g, sgu_ln_b, sgu_ws, sgu_b, conv_d_w, conv_d_b, conv_d_ln_g, conv_d_ln_b, w_branch, w_gate, b_gate, w_o, xa_pre_g, xa_post_g, mem_g, xa_wq, xa_wk, xa_wv, xa_wo, ffn2_pre_g, ffn2_post_g, ffn2_w1, ffn2_w3, ffn2_w2, loss_target, m_ffn1_pre_g, m_ffn1_post_g, m_ffn1_w1, m_ffn1_w3, m_ffn1_w2, m_mix_pre_g, m_mix_post_g, m_w_in, m_conv_a_w, m_conv_a_b, m_pool_w, m_pool_scale, m_sgu_ln_g, m_sgu_ln_b, m_sgu_ws, m_sgu_b, m_conv_d_w, m_conv_d_b, m_conv_d_ln_g, m_conv_d_ln_b, m_w_branch, m_w_gate, m_b_gate, m_w_o, m_xa_pre_g, m_xa_post_g, m_mem_g, m_xa_wq, m_xa_wk, m_xa_wv, m_xa_wo, m_ffn2_pre_g, m_ffn2_post_g, m_ffn2_w1, m_ffn2_w3, m_ffn2_w2, v_ffn1_pre_g, v_ffn1_post_g, v_ffn1_w1, v_ffn1_w3, v_ffn1_w2, v_mix_pre_g, v_mix_post_g, v_w_in, v_conv_a_w, v_conv_a_b, v_pool_w, v_pool_scale, v_sgu_ln_g, v_sgu_ln_b, v_sgu_ws, v_sgu_b, v_conv_d_w, v_conv_d_b, v_conv_d_ln_g, v_conv_d_ln_b, v_w_branch, v_w_gate, v_b_gate, v_w_o, v_xa_pre_g, v_xa_post_g, v_mem_g, v_xa_wq, v_xa_wk, v_xa_wv, v_xa_wo, v_ffn2_pre_g, v_ffn2_post_g, v_ffn2_w1, v_ffn2_w3, v_ffn2_w2):
    args = dict(locals())
    W = {n: args[n] for n in _W_NAMES}
    M = {n: args["m_" + n] for n in _W_NAMES}
    V = {n: args["v_" + n] for n in _W_NAMES}
    return _step(x, mem, loss_target, W, M, V)
```

```python
import jax
import jax.numpy as jnp
from jax import lax
from jax.experimental import pallas as pl
from jax.experimental.pallas import tpu as pltpu

F32 = jnp.float32
CDT = jnp.bfloat16
EPS = 1e-6
NS = 8
GW = 128
MW = 512
CHUNK = 64
XA_HEADS = 4
POOL_WINDOWS = (2, 4, 8, 16)
VMEM_LIMIT = 56 * 1024 * 1024
ADAM_LR, ADAM_B1, ADAM_B2, ADAM_EPS, ADAM_WD, ADAM_STEP = 0.001, 0.9, 0.999, 1e-08, 0.01, 10

SDS = jax.ShapeDtypeStruct

_SP_NAMES = (("ffn1_pre_g", 1024), ("ffn1_post_g", 1024), ("mix_pre_g", 1024), ("mix_post_g", 1024),
             ("xa_pre_g", 1024), ("xa_post_g", 1024), ("mem_g", 1024), ("ffn2_pre_g", 1024), ("ffn2_post_g", 1024),
             ("conv_a_b", 512), ("pool_scale", 512), ("sgu_ln_g", 512), ("sgu_ln_b", 512), ("conv_d_b", 512),
             ("conv_d_ln_g", 512), ("conv_d_ln_b", 512), ("b_gate", 4096))
_SP = {}
_off = 0
for _n, _w in _SP_NAMES:
    _SP[_n] = (_off, _w)
    _off += _w
_SP_TOTAL = _off


def _call(body, name, grid, in_specs, out_specs, out_shape, scratch=(), aliases=None):
    return pl.pallas_call(
        body, name=name, grid=grid, in_specs=in_specs, out_specs=out_specs, out_shape=out_shape,
        scratch_shapes=list(scratch), input_output_aliases=aliases or {},
        compiler_params=pltpu.CompilerParams(dimension_semantics=("arbitrary",) * len(grid),
                                             vmem_limit_bytes=VMEM_LIMIT))


def _nn(a, b):
    return lax.dot_general(a, b, (((1,), (0,)), ((), ())), preferred_element_type=F32)


def _nt(a, b):
    return lax.dot_general(a, b, (((1,), (1,)), ((), ())), preferred_element_type=F32)


def _tn(a, b):
    return lax.dot_general(a, b, (((0,), (0,)), ((), ())), preferred_element_type=F32)


def _rms(x):
    r = lax.rsqrt(jnp.mean(x * x, axis=-1, keepdims=True) + EPS)
    return x * r, r


def _rms_bwd(n, r, g, dout):
    dn = dout * g
    dx = r * (dn - n * jnp.mean(dn * n, axis=-1, keepdims=True))
    return dx, jnp.sum(dout * n, axis=0, keepdims=True)


def _ln(y):
    mu = jnp.mean(y, axis=-1, keepdims=True)
    yc = y - mu
    rs = lax.rsqrt(jnp.mean(yc * yc, axis=-1, keepdims=True) + EPS)
    return yc * rs, rs


def _ln_bwd(xh, rs, dxh):
    return rs * (dxh - jnp.mean(dxh, axis=-1, keepdims=True) - xh * jnp.mean(dxh * xh, axis=-1, keepdims=True))


def _silu_parts(a):
    s = jax.nn.sigmoid(a)
    sl = a * s
    return sl, s + sl * (1.0 - s)


_GELU_C = 0.7978845608028654
_GELU_A = 0.044715


def _gelu(x):
    return 0.5 * x * (1.0 + jnp.tanh(_GELU_C * (x + _GELU_A * x * x * x)))


def _gelu_parts(x):
    t = jnp.tanh(_GELU_C * (x + _GELU_A * x * x * x))
    g = 0.5 * x * (1.0 + t)
    dg = 0.5 * (1.0 + t) + 0.5 * x * (1.0 - t * t) * _GELU_C * (1.0 + 3.0 * _GELU_A * x * x)
    return g, dg


def _spspec(name, width, imap):
    off = _SP[name][0]
    assert off % width == 0
    return pl.BlockSpec((1, width), lambda *a: (0, off // width + imap(*a)))


def _zero(*a):
    return 0


def _row_once(tm, d):
    return pl.BlockSpec((tm, d), lambda i, j: (i, 0), pipeline_mode=pl.Buffered(1))


FFN_SG = 2


def _ffn_fwd(x, sp, pre, post, pf, dep):
    S, D = x.shape
    FS = pf.shape[1] // 3
    TM = min(512, S)
    SG, NG, W = FFN_SG, NS // FFN_SG, FFN_SG * FS

    def body(x_ref, pg_ref, qg_ref, w1_ref, w3_ref, w2_ref, dep_ref, xo_ref, hb_ref, a_ref, b_ref, y_ref, hb_s, acc):
        j = pl.program_id(1)

        @pl.when(j == 0)
        def _():
            n, _ = _rms(x_ref[...])
            hb = (n * pg_ref[...]).astype(CDT)
            hb_s[...] = hb
            hb_ref[...] = hb
            acc[...] = jnp.zeros_like(acc)

        hb = hb_s[...]
        a = _nt(hb, w1_ref[...].reshape(W, D))
        b = _nt(hb, w3_ref[...].reshape(W, D))
        a_ref[...] = a.astype(CDT)
        b_ref[...] = b.astype(CDT)
        u = (a * jax.nn.sigmoid(a) * b).astype(CDT)
        acc[...] += _nn(u, w2_ref[...].reshape(W, D))

        @pl.when(j == NG - 1)
        def _():
            y = acc[...]
            y_ref[...] = y.astype(CDT)
            n, _ = _rms(y)
            xo_ref[...] = x_ref[...] + 0.5 * (n * qg_ref[...])

    row1 = pl.BlockSpec((TM, D), lambda i, j: (i, 0))
    grp = lambda i, j: (j, i, 0)
    return _call(
        body, "ffn_fwd", (S // TM, NG),
        [row1, _spspec(pre, D, _zero), _spspec(post, D, _zero),
         pl.BlockSpec((SG, FS, D), lambda i, j: (j, 0, 0)), pl.BlockSpec((SG, FS, D), lambda i, j: (j, 1, 0)),
         pl.BlockSpec((SG, FS, D), lambda i, j: (j, 2, 0)), pl.BlockSpec(memory_space=pl.ANY)],
        [row1, row1, pl.BlockSpec((None, TM, W), grp), pl.BlockSpec((None, TM, W), grp), row1],
        [SDS((S, D), F32), SDS((S, D), CDT), SDS((NG, S, W), CDT), SDS((NG, S, W), CDT), SDS((S, D), CDT)],
        [pltpu.VMEM((TM, D), CDT), pltpu.VMEM((TM, D), F32)])(x, sp, sp, pf, pf, pf, dep)


def _ffn_bwd_act(dxo, x, y, a, b, sp, pre, post, pf, dep):
    S, D = x.shape
    FS = pf.shape[1] // 3
    TM = min(512, S)
    SG, NG, W = FFN_SG, NS // FFN_SG, FFN_SG * FS

    def body(dxo_ref, x_ref, y_ref, a_ref, b_ref, pg_ref, qg_ref, w1_ref, w3_ref, w2_ref, dep_ref,
             dx_ref, dyb_ref, da_ref, db_ref, gp_ref, dyb_s, acc):
        i = pl.program_id(0)
        j = pl.program_id(1)

        @pl.when((i == 0) & (j == 0))
        def _():
            gp_ref[...] = jnp.zeros_like(gp_ref)

        @pl.when(j == 0)
        def _():
            n, r = _rms(y_ref[...].astype(F32))
            dy, dg = _rms_bwd(n, r, qg_ref[...], 0.5 * dxo_ref[...])
            dyb = dy.astype(CDT)
            dyb_s[...] = dyb
            dyb_ref[...] = dyb
            gp_ref[1:2, :] += dg
            acc[...] = jnp.zeros_like(acc)

        sl, dsl = _silu_parts(a_ref[...].astype(F32))
        du = _nt(dyb_s[...], w2_ref[...].reshape(W, D))
        db = (du * sl).astype(CDT)
        da = (du * b_ref[...].astype(F32) * dsl).astype(CDT)
        da_ref[...] = da
        db_ref[...] = db
        acc[...] += _nn(da, w1_ref[...].reshape(W, D)) + _nn(db, w3_ref[...].reshape(W, D))

        @pl.when(j == NG - 1)
        def _():
            n, r = _rms(x_ref[...])
            dx, dg = _rms_bwd(n, r, pg_ref[...], acc[...])
            dx_ref[...] = dxo_ref[...] + dx
            gp_ref[0:1, :] += dg

    row = lambda i, j: (i, 0)
    grp = lambda i, j: (j, i, 0)
    return _call(
        body, "ffn_bwd_act", (S // TM, NG),
        [pl.BlockSpec((TM, D), row), pl.BlockSpec((TM, D), row), pl.BlockSpec((TM, D), row),
         pl.BlockSpec((None, TM, W), grp), pl.BlockSpec((None, TM, W), grp),
         _spspec(pre, D, _zero), _spspec(post, D, _zero),
         pl.BlockSpec((SG, FS, D), lambda i, j: (j, 0, 0)), pl.BlockSpec((SG, FS, D), lambda i, j: (j, 1, 0)),
         pl.BlockSpec((SG, FS, D), lambda i, j: (j, 2, 0)), pl.BlockSpec(memory_space=pl.ANY)],
        [pl.BlockSpec((TM, D), row), pl.BlockSpec((TM, D), row), pl.BlockSpec((None, TM, W), grp),
         pl.BlockSpec((None, TM, W), grp), pl.BlockSpec((8, D), lambda i, j: (0, 0))],
        [SDS((S, D), F32), SDS((S, D), CDT), SDS((NG, S, W), CDT), SDS((NG, S, W), CDT), SDS((8, D), F32)],
        [pltpu.VMEM((TM, D), CDT), pltpu.VMEM((TM, D), F32)])(dxo, x, y, a, b, sp, sp, pf, pf, pf, dep)


def _ffn_bwd_w(hb, dyb, a, b, da, db, dep):
    S, D = hb.shape
    SG, NG = FFN_SG, NS // FFN_SG
    W = a.shape[2]
    FS = W // SG
    TK = min(512, S)
    NK = S // TK

    def body(hb_ref, dyb_ref, a_ref, b_ref, da_ref, db_ref, dep_ref, g_ref, acc):
        k = pl.program_id(1)

        @pl.when(k == 0)
        def _():
            acc[...] = jnp.zeros_like(acc)

        af = a_ref[...].astype(F32)
        u = (af * jax.nn.sigmoid(af) * b_ref[...].astype(F32)).astype(CDT)
        hb = hb_ref[...]
        acc[0:W, :] += _tn(da_ref[...], hb)
        acc[W:2 * W, :] += _tn(db_ref[...], hb)
        acc[2 * W:3 * W, :] += _tn(u, dyb_ref[...])

        @pl.when(k == NK - 1)
        def _():
            for s in range(SG):
                for r in range(3):
                    g_ref[s, r * FS:(r + 1) * FS, :] = acc[r * W + s * FS:r * W + (s + 1) * FS, :].astype(CDT)

    row = lambda j, k: (k, 0)
    grp = lambda j, k: (j, k, 0)
    return _call(
        body, "ffn_bwd_w", (NG, NK),
        [pl.BlockSpec((TK, D), row), pl.BlockSpec((TK, D), row)] + [pl.BlockSpec((None, TK, W), grp)] * 4 + [_ANY],
        pl.BlockSpec((SG, 3 * FS, D), lambda j, k: (j, 0, 0)),
        SDS((NS, 3 * FS, D), CDT),
        [pltpu.VMEM((3 * W, D), F32)])(hb, dyb, a, b, da, db, dep)


def _mix_in(x, sp, pma, dep):
    S, D = x.shape
    TM = min(1024, S)

    def body(x_ref, pg_ref, bg_ref, wi_ref, wg_ref, dep_ref, hb_ref, z_ref, g_ref, hb_s):
        @pl.when(pl.program_id(1) == 0)
        def _():
            n, _ = _rms(x_ref[...])
            hb = (n * pg_ref[...]).astype(CDT)
            hb_s[...] = hb
            hb_ref[...] = hb

        hb = hb_s[...]
        z_ref[...] = _nt(hb, wi_ref[...]).astype(CDT)
        g_ref[...] = jax.nn.sigmoid(_nt(hb, wg_ref[...]) + bg_ref[...]).astype(CDT)

    return _call(
        body, "mix_in", (S // TM, NS),
        [_row_once(TM, D), _spspec("mix_pre_g", D, _zero), _spspec("b_gate", MW, lambda i, j: j),
         pl.BlockSpec((None, MW, D), lambda i, j: (j, 0, 0)), pl.BlockSpec((None, MW, D), lambda i, j: (j, 1, 0)), _ANY],
        [_row_once(TM, D), pl.BlockSpec((None, TM, MW), lambda i, j: (j, i, 0)),
         pl.BlockSpec((None, TM, MW), lambda i, j: (j // 2, i, j % 2))],
        [SDS((S, D), CDT), SDS((NS, S, MW), CDT), SDS((4, S, D), CDT)],
        [pltpu.VMEM((TM, D), CDT)])(x, sp, sp, pma, pma, dep)


def _mix_in_bwd_act(dz, dgp, dxr, x, sp, pma):
    S, D = x.shape
    TM = min(1024, S)

    def body(dz_ref, dg_ref, dxr_ref, x_ref, pg_ref, wi_ref, wg_ref, dx_ref, gp_ref, acc):
        i = pl.program_id(0)
        j = pl.program_id(1)

        @pl.when((i == 0) & (j == 0))
        def _():
            gp_ref[...] = jnp.zeros_like(gp_ref)

        @pl.when(j == 0)
        def _():
            acc[...] = jnp.zeros_like(acc)

        acc[...] += _nn(dz_ref[...], wi_ref[...]) + _nn(dg_ref[...], wg_ref[...])

        @pl.when(j == NS - 1)
        def _():
            n, r = _rms(x_ref[...])
            dx, dg = _rms_bwd(n, r, pg_ref[...], acc[...])
            dx_ref[...] = dxr_ref[...] + dx
            gp_ref[0:1, :] += dg

    return _call(
        body, "mix_in_bwd_act", (S // TM, NS),
        [pl.BlockSpec((None, TM, MW), lambda i, j: (j, i, 0)), pl.BlockSpec((None, TM, MW), lambda i, j: (j // 2, i, j % 2)),
         _row_once(TM, D), _row_once(TM, D), _spspec("mix_pre_g", D, _zero),
         pl.BlockSpec((None, MW, D), lambda i, j: (j, 0, 0)), pl.BlockSpec((None, MW, D), lambda i, j: (j, 1, 0))],
        [_row_once(TM, D), pl.BlockSpec((8, D), lambda i, j: (0, 0))],
        [SDS((S, D), F32), SDS((8, D), F32)],
        [pltpu.VMEM((TM, D), F32)])(dz, dgp, dxr, x, sp, pma, pma)


def _mix_in_bwd_w(dz, dgp, hb):
    S, D = hb.shape
    TK = min(512, S)
    NK = S // TK

    def body(dz_ref, dg_ref, hb_ref, g_ref, bg_ref, acc):
        k = pl.program_id(1)

        @pl.when(k == 0)
        def _():
            acc[...] = jnp.zeros_like(acc)
            bg_ref[...] = jnp.zeros_like(bg_ref)

        hb = hb_ref[...]
        dg = dg_ref[...]
        acc[0:MW, :] += _tn(dz_ref[...], hb)
        acc[MW:2 * MW, :] += _tn(dg, hb)
        bg_ref[0:1, :] += jnp.sum(dg.astype(F32), axis=0, keepdims=True)

        @pl.when(k == NK - 1)
        def _():
            g_ref[...] = acc[...].astype(CDT)

    return _call(
        body, "mix_in_bwd_w", (NS, NK),
        [pl.BlockSpec((None, TK, MW), lambda j, k: (j, k, 0)), pl.BlockSpec((None, TK, MW), lambda j, k: (j // 2, k, j % 2)),
         pl.BlockSpec((TK, D), lambda j, k: (k, 0))],
        [pl.BlockSpec((None, 2 * MW, D), lambda j, k: (j, 0, 0)), pl.BlockSpec((None, 8, MW), lambda j, k: (j, 0, 0))],
        [SDS((NS, 2 * MW, D), CDT), SDS((NS, 8, MW), F32)],
        [pltpu.VMEM((2 * MW, D), F32)])(dz, dgp, hb)


def _causal_taps(pad_ref, i, ch, halo, k_taps, lanes=slice(None)):
    val = pad_ref[pl.ds(pl.multiple_of(i * ch, 8), ch + halo), lanes]
    base = {}
    out = []
    for k in range(k_taps):
        q, r = divmod(k_taps - 1 - k, 8)
        if r not in base:
            base[r] = pltpu.roll(val, r, 0) if r else val
        out.append((k, base[r][halo - 8 * q:halo - 8 * q + ch, :]))
    return out


def _anti_taps(pad_ref, i, ch, halo, k_taps, lanes=slice(None)):
    val = pad_ref[pl.ds(pl.multiple_of(i * ch, 8), ch + halo), lanes]
    n = ch + halo
    base = {}
    out = []
    for k in range(k_taps):
        q, r = divmod(k_taps - 1 - k, 8)
        if r not in base:
            base[r] = pltpu.roll(val, n - r, 0) if r else val
        out.append((k, base[r][8 * q:8 * q + ch, :]))
    return out


def _conv_geometry(S, k_taps):
    halo = 8 * ((k_taps - 1 + 7) // 8)
    ch = min(256, S)
    return halo, ch, S // ch


def _rows(i, ch):
    return pl.ds(pl.multiple_of(i * ch, ch), ch)


def _mixA_fwd(z, cw, sp):
    S = z.shape[1]
    K = cw.shape[0]
    H, CH, NCH = _conv_geometry(S, K)

    def body(z_ref, w_ref, b_ref, o_ref, pad):
        pad[0:H, :] = jnp.zeros((H, GW), F32)

        def fill(i, c):
            r = _rows(i, CH)
            pad[pl.ds(pl.multiple_of(i * CH + H, 8), CH), :] = z_ref[2, r, :].astype(F32) * z_ref[0, r, :].astype(F32)
            return c

        lax.fori_loop(0, NCH, fill, 0)

        def conv(i, c):
            r = _rows(i, CH)
            acc = jnp.zeros((CH, GW), F32)
            for k, sh in _causal_taps(pad, i, CH, H, K):
                acc = acc + w_ref[k:k + 1, :] * sh
            o_ref[r, :] = (z_ref[1, r, :].astype(F32) * (acc + b_ref[...])).astype(CDT)
            return c

        lax.fori_loop(0, NCH, conv, 0)

    return _call(
        body, "mixA_fwd", (MW // GW,),
        [pl.BlockSpec((3, S, GW), lambda c: (0, 0, c)), pl.BlockSpec((K, GW), lambda c: (0, c)),
         _spspec("conv_a_b", GW, lambda c: c)],
        pl.BlockSpec((S, GW), lambda c: (0, c)), SDS((S, MW), CDT),
        [pltpu.VMEM((H + S, GW), F32)])(z, cw, sp)


def _mixA_bwd(z, dm, cw, sp):
    S = z.shape[1]
    K = cw.shape[0]
    H, CH, NCH = _conv_geometry(S, K)

    def body(z_ref, dm_ref, w_ref, b_ref, dz_ref, dw_ref, db_ref, pad, dpad, dw_s):
        pad[0:H, :] = jnp.zeros((H, GW), F32)
        dpad[pl.ds(S, H), :] = jnp.zeros((H, GW), F32)
        dw_s[...] = jnp.zeros_like(dw_s)
        db_ref[...] = jnp.zeros_like(db_ref)

        def fill(i, c):
            r = _rows(i, CH)
            pad[pl.ds(pl.multiple_of(i * CH + H, 8), CH), :] = z_ref[2, r, :].astype(F32) * z_ref[0, r, :].astype(F32)
            return c

        lax.fori_loop(0, NCH, fill, 0)

        def p1(i, c):
            r = _rows(i, CH)
            taps = _causal_taps(pad, i, CH, H, K)
            acc = jnp.zeros((CH, GW), F32)
            for k, sh in taps:
                acc = acc + w_ref[k:k + 1, :] * sh
            dmf = dm_ref[r, :].astype(F32)
            dz_ref[1, r, :] = (dmf * (acc + b_ref[...])).astype(CDT)
            dc = dmf * z_ref[1, r, :].astype(F32)
            dpad[r, :] = dc
            for k, sh in taps:
                dw_s[k:k + 1, :] += jnp.sum(dc * sh, axis=0, keepdims=True)
            db_ref[0:1, :] += jnp.sum(dc, axis=0, keepdims=True)
            return c

        lax.fori_loop(0, NCH, p1, 0)

        def p2(i, c):
            r = _rows(i, CH)
            dq = jnp.zeros((CH, GW), F32)
            for k, sh in _anti_taps(dpad, i, CH, H, K):
                dq = dq + w_ref[k:k + 1, :] * sh
            dz_ref[0, r, :] = (dq * z_ref[2, r, :].astype(F32)).astype(CDT)
            dz_ref[2, r, :] = (dq * z_ref[0, r, :].astype(F32)).astype(CDT)
            return c

        lax.fori_loop(0, NCH, p2, 0)
        dw_ref[...] = dw_s[0:K, :]

    return _call(
        body, "mixA_bwd", (MW // GW,),
        [pl.BlockSpec((3, S, GW), lambda c: (0, 0, c)), pl.BlockSpec((None, S, GW), lambda c: (0, 0, c)),
         pl.BlockSpec((K, GW), lambda c: (0, c)), _spspec("conv_a_b", GW, lambda c: c)],
        [pl.BlockSpec((3, S, GW), lambda c: (0, 0, c)), pl.BlockSpec((K, GW), lambda c: (0, c)),
         pl.BlockSpec((8, GW), lambda c: (0, c))],
        [SDS((NS, S, MW), CDT), SDS((K, MW), F32), SDS((8, MW), F32)],
        [pltpu.VMEM((H + S, GW), F32), pltpu.VMEM((S + H, GW), F32), pltpu.VMEM((8 * ((K + 7) // 8), GW), F32)])(z, dm, cw, sp)


def _mixD_conv_fwd(z, cw, sp):
    S = z.shape[1]
    K = cw.shape[0]
    H, CH, NCH = _conv_geometry(S, K)

    def body(z_ref, w_ref, b_ref, o_ref, pad):
        pad[0:H, :] = jnp.zeros((H, GW), F32)

        def fill(i, c):
            r = _rows(i, CH)
            pad[pl.ds(pl.multiple_of(i * CH + H, 8), CH), :] = (
                z_ref[0, r, :].astype(F32) * jax.nn.sigmoid(z_ref[1, r, :].astype(F32)))
            return c

        lax.fori_loop(0, NCH, fill, 0)

        def conv(i, c):
            acc = jnp.zeros((CH, GW), F32)
            for k, sh in _causal_taps(pad, i, CH, H, K):
                acc = acc + w_ref[k:k + 1, :] * sh
            o_ref[_rows(i, CH), :] = (acc + b_ref[...]).astype(CDT)
            return c

        lax.fori_loop(0, NCH, conv, 0)

    return _call(
        body, "mixD_conv_fwd", (MW // GW,),
        [pl.BlockSpec((2, S, GW), lambda c: (3, 0, c)), pl.BlockSpec((K, GW), lambda c: (0, c)),
         _spspec("conv_d_b", GW, lambda c: c)],
        pl.BlockSpec((S, GW), lambda c: (0, c)), SDS((S, MW), CDT),
        [pltpu.VMEM((H + S, GW), F32)])(z, cw, sp)


def _mixD_conv_bwd(z, dy, cw, dz):
    S = z.shape[1]
    K = cw.shape[0]
    H, CH, NCH = _conv_geometry(S, K)

    def body(z_ref, dy_ref, w_ref, dzin_ref, dz_ref, dw_ref, db_ref, pad, dpad, dw_s):
        pad[0:H, :] = jnp.zeros((H, GW), F32)
        dpad[pl.ds(S, H), :] = jnp.zeros((H, GW), F32)
        dw_s[...] = jnp.zeros_like(dw_s)
        db_ref[...] = jnp.zeros_like(db_ref)

        def fill(i, c):
            r = _rows(i, CH)
            pad[pl.ds(pl.multiple_of(i * CH + H, 8), CH), :] = (
                z_ref[0, r, :].astype(F32) * jax.nn.sigmoid(z_ref[1, r, :].astype(F32)))
            dpad[r, :] = dy_ref[r, :].astype(F32)
            return c

        lax.fori_loop(0, NCH, fill, 0)

        def p1(i, c):
            dyf = dy_ref[_rows(i, CH), :].astype(F32)
            for k, sh in _causal_taps(pad, i, CH, H, K):
                dw_s[k:k + 1, :] += jnp.sum(dyf * sh, axis=0, keepdims=True)
            db_ref[0:1, :] += jnp.sum(dyf, axis=0, keepdims=True)
            return c

        lax.fori_loop(0, NCH, p1, 0)

        def p2(i, c):
            r = _rows(i, CH)
            dq = jnp.zeros((CH, GW), F32)
            for k, sh in _anti_taps(dpad, i, CH, H, K):
                dq = dq + w_ref[k:k + 1, :] * sh
            a = z_ref[0, r, :].astype(F32)
            sg = jax.nn.sigmoid(z_ref[1, r, :].astype(F32))
            dz_ref[0, r, :] = (dq * sg).astype(CDT)
            dz_ref[1, r, :] = (dq * a * sg * (1.0 - sg)).astype(CDT)
            return c

        lax.fori_loop(0, NCH, p2, 0)
        dw_ref[...] = dw_s[0:K, :]

    return _call(
        body, "mixD_conv_bwd", (MW // GW,),
        [pl.BlockSpec((2, S, GW), lambda c: (3, 0, c)), pl.BlockSpec((S, GW), lambda c: (0, c)),
         pl.BlockSpec((K, GW), lambda c: (0, c)), _ANY],
        [pl.BlockSpec((2, S, GW), lambda c: (3, 0, c)), pl.BlockSpec((K, GW), lambda c: (0, c)),
         pl.BlockSpec((8, GW), lambda c: (0, c))],
        [SDS((NS, S, MW), CDT), SDS((K, MW), F32), SDS((8, MW), F32)],
        [pltpu.VMEM((H + S, GW), F32), pltpu.VMEM((S + H, GW), F32), pltpu.VMEM((8 * ((K + 7) // 8), GW), F32)],
        aliases={3: 0})(z, dy, cw, dz)


def _mixD_ln_bwd(dm, yd, sp):
    S = yd.shape[0]
    TM = min(512, S)

    def body(dm_ref, y_ref, lg_ref, lb_ref, dy_ref, gp_ref):
        @pl.when(pl.program_id(0) == 0)
        def _():
            gp_ref[...] = jnp.zeros_like(gp_ref)

        xh, rs = _ln(y_ref[...].astype(F32))
        _, dsl = _silu_parts(xh * lg_ref[...] + lb_ref[...])
        dl = dm_ref[...].astype(F32) * dsl
        gp_ref[0:1, :] += jnp.sum(dl * xh, axis=0, keepdims=True)
        gp_ref[1:2, :] += jnp.sum(dl, axis=0, keepdims=True)
        dy_ref[...] = _ln_bwd(xh, rs, dl * lg_ref[...]).astype(CDT)

    row = lambda i: (i, 0)
    return _call(
        body, "mixD_ln_bwd", (S // TM,),
        [pl.BlockSpec((None, TM, MW), lambda i: (3, i, 0)), pl.BlockSpec((TM, MW), row), _spspec("conv_d_ln_g", MW, _zero),
         _spspec("conv_d_ln_b", MW, _zero)],
        [pl.BlockSpec((TM, MW), row), pl.BlockSpec((8, MW), lambda i: (0, 0))],
        [SDS((S, MW), CDT), SDS((8, MW), F32)])(dm, yd, sp, sp)


def _box_causal(val, g):
    s = val
    for d in range(g + 1):
        s = s + pltpu.roll(s, 1 << d, 0)
    return s


def _box_anti(val, g):
    n = val.shape[0]
    s = val
    for d in range(g + 1):
        s = s + pltpu.roll(s, n - (1 << d), 0)
    return s


def _pool_count(i, ch, win):
    t = lax.broadcasted_iota(jnp.int32, (ch, GW), 0) + (i * ch + 1)
    return jnp.minimum(t, win).astype(F32)


def _mixB_fwd(z, wp, sp):
    S = z.shape[1]
    H, CH = 16, min(256, S)
    NCH = S // CH
    assert POOL_WINDOWS == tuple(2 << g for g in range(4))

    def body(p_ref, wp_ref, sc_ref, o_ref, pad):
        pad[0:H, :] = jnp.zeros((H, MW), F32)

        def fill(i, c):
            pad[pl.ds(pl.multiple_of(i * CH + H, 8), CH), :] = p_ref[_rows(i, CH), :].astype(F32)
            return c

        lax.fori_loop(0, NCH, fill, 0)

        def step(i, c):
            r = _rows(i, CH)
            for g in range(4):
                gs = slice(g * GW, (g + 1) * GW)
                val = pad[pl.ds(pl.multiple_of(i * CH, 8), CH + H), gs]
                pooled = _box_causal(val, g)[H:, :] / _pool_count(i, CH, POOL_WINDOWS[g]) - val[H:, :]
                mixed = _nn(pooled.astype(CDT), wp_ref[g].astype(CDT))
                o_ref[r, gs] = (mixed * sc_ref[:, gs]).astype(CDT)
            return c

        lax.fori_loop(0, NCH, step, 0)

    return _call(
        body, "mixB_fwd", (1,),
        [pl.BlockSpec((None, S, MW), lambda i: (3, 0, 0)), pl.BlockSpec((4, GW, GW), lambda i: (0, 0, 0)),
         _spspec("pool_scale", MW, _zero)],
        pl.BlockSpec((S, MW), lambda i: (0, 0)), SDS((S, MW), CDT),
        [pltpu.VMEM((H + S, MW), F32)])(z, wp, sp)


def _mixB_bwd(z, dm, wp, sp, dz):
    S = z.shape[1]
    H, CH = 16, min(256, S)
    NCH = S // CH

    def body(p_ref, dm_ref, wp_ref, sc_ref, dzin_ref, dz_ref, dwp_ref, dsc_ref, pad, rpad):
        pad[0:H, :] = jnp.zeros((H, MW), F32)
        rpad[pl.ds(S, H), :] = jnp.zeros((H, MW), F32)
        dwp_ref[...] = jnp.zeros_like(dwp_ref)
        dsc_ref[...] = jnp.zeros_like(dsc_ref)

        def fill(i, c):
            pad[pl.ds(pl.multiple_of(i * CH + H, 8), CH), :] = p_ref[_rows(i, CH), :].astype(F32)
            return c

        lax.fori_loop(0, NCH, fill, 0)

        def p1(i, c):
            r = _rows(i, CH)
            for g in range(4):
                gs = slice(g * GW, (g + 1) * GW)
                cnt = _pool_count(i, CH, POOL_WINDOWS[g])
                val = pad[pl.ds(pl.multiple_of(i * CH, 8), CH + H), gs]
                pooled = (_box_causal(val, g)[H:, :] / cnt - val[H:, :]).astype(CDT)
                w = wp_ref[g].astype(CDT)
                mixed = _nn(pooled, w)
                dmf = dm_ref[r, gs].astype(F32)
                dsc_ref[0:1, gs] += jnp.sum(dmf * mixed, axis=0, keepdims=True)
                dmx = (dmf * sc_ref[:, gs]).astype(CDT)
                dwp_ref[g] += _tn(pooled, dmx)
                rpad[r, gs] = _nt(dmx, w) / cnt
            return c

        lax.fori_loop(0, NCH, p1, 0)

        def p2(i, c):
            r = _rows(i, CH)
            for g in range(4):
                gs = slice(g * GW, (g + 1) * GW)
                val = rpad[pl.ds(pl.multiple_of(i * CH, 8), CH + H), gs]
                dp = _box_anti(val, g)[:CH, :] - val[:CH, :] * _pool_count(i, CH, POOL_WINDOWS[g])
                dz_ref[r, gs] = dp.astype(CDT)
            return c

        lax.fori_loop(0, NCH, p2, 0)

    return _call(
        body, "mixB_bwd", (1,),
        [pl.BlockSpec((None, S, MW), lambda i: (3, 0, 0)), pl.BlockSpec((None, S, MW), lambda i: (1, 0, 0)),
         pl.BlockSpec((4, GW, GW), lambda i: (0, 0, 0)), _spspec("pool_scale", MW, _zero), _ANY],
        [pl.BlockSpec((None, S, MW), lambda i: (3, 0, 0)), pl.BlockSpec((4, GW, GW), lambda i: (0, 0, 0)),
         pl.BlockSpec((8, MW), lambda i: (0, 0))],
        [SDS((NS, S, MW), CDT), SDS((4, GW, GW), F32), SDS((8, MW), F32)],
        [pltpu.VMEM((H + S, MW), F32), pltpu.VMEM((S + H, MW), F32)], aliases={4: 0})(z, dm, wp, sp, dz)


def _sgu_mask():
    ci = lax.broadcasted_iota(jnp.int32, (GW, GW), 0) // CHUNK
    cj = lax.broadcasted_iota(jnp.int32, (GW, GW), 1) // CHUNK
    return cj <= ci


def _mixC_fwd(z, ws, bsc, sp):
    S = z.shape[1]
    RB = min(512, S)

    def body(z_ref, lg_ref, lb_ref, ws_ref, bs_ref, o_ref):
        mask = _sgu_mask()
        gu = _gelu(z_ref[0].astype(F32))
        xh, _ = _ln(_gelu(z_ref[1].astype(F32)))
        vn = (xh * lg_ref[...] + lb_ref[...]).astype(CDT)
        for g in range(4):
            gs = slice(g * GW, (g + 1) * GW)
            wm = jnp.where(mask, ws_ref[g], 0.0).astype(CDT)
            for nb in range(RB // GW):
                rs = slice(nb * GW, (nb + 1) * GW)
                mixed = _nn(wm, vn[rs, gs]) + bs_ref[g]
                o_ref[rs, gs] = (gu[rs, gs] * mixed).astype(CDT)

    return _call(
        body, "mixC_fwd", (S // RB,),
        [pl.BlockSpec((2, RB, MW), lambda i: (2, i, 0)), _spspec("sgu_ln_g", MW, _zero), _spspec("sgu_ln_b", MW, _zero),
         pl.BlockSpec((4, GW, GW), lambda i: (0, 0, 0)), pl.BlockSpec((4, GW, 1), lambda i: (0, 0, 0))],
        pl.BlockSpec((RB, MW), lambda i: (i, 0)), SDS((S, MW), CDT))(z, sp, sp, ws, bsc)


def _mixC_bwd(z, dm, ws, bsc, sp, dz):
    S = z.shape[1]
    RB = min(512, S)
    NR = S // RB

    def body(z_ref, dm_ref, lg_ref, lb_ref, ws_ref, bs_ref, dzin_ref, dz_ref, dws_ref, dbs_ref, gp_ref, dvn_s):
        i = pl.program_id(0)

        @pl.when(i == 0)
        def _():
            dws_ref[...] = jnp.zeros_like(dws_ref)
            dbs_ref[...] = jnp.zeros_like(dbs_ref)
            gp_ref[...] = jnp.zeros_like(gp_ref)

        mask = _sgu_mask()
        gu, dgu = _gelu_parts(z_ref[0].astype(F32))
        gv, dgv = _gelu_parts(z_ref[1].astype(F32))
        xh, rs_ = _ln(gv)
        vn = (xh * lg_ref[...] + lb_ref[...]).astype(CDT)
        dmf = dm_ref[...].astype(F32)
        for g in range(4):
            gs = slice(g * GW, (g + 1) * GW)
            wm = jnp.where(mask, ws_ref[g], 0.0).astype(CDT)
            for nb in range(RB // GW):
                rs = slice(nb * GW, (nb + 1) * GW)
                vb = vn[rs, gs]
                mixed = _nn(wm, vb) + bs_ref[g]
                dz_ref[0, rs, gs] = (dmf[rs, gs] * mixed * dgu[rs, gs]).astype(CDT)
                dmx = dmf[rs, gs] * gu[rs, gs]
                dbs_ref[g] += dmx
                dmxc = dmx.astype(CDT)
                dws_ref[g] += _nt(dmxc, vb)
                dvn_s[rs, gs] = _tn(wm, dmxc)
        dvn = dvn_s[...]
        gp_ref[0:1, :] += jnp.sum(dvn * xh, axis=0, keepdims=True)
        gp_ref[1:2, :] += jnp.sum(dvn, axis=0, keepdims=True)
        dz_ref[1] = (_ln_bwd(xh, rs_, dvn * lg_ref[...]) * dgv).astype(CDT)

        @pl.when(i == NR - 1)
        def _():
            for g in range(4):
                dws_ref[g] = jnp.where(mask, dws_ref[g], 0.0)
                dbs_ref[g] = jnp.broadcast_to(jnp.sum(dbs_ref[g], axis=1, keepdims=True), (GW, GW))

    full3 = lambda i: (0, 0, 0)
    return _call(
        body, "mixC_bwd", (NR,),
        [pl.BlockSpec((2, RB, MW), lambda i: (2, i, 0)), pl.BlockSpec((None, RB, MW), lambda i: (2, i, 0)),
         _spspec("sgu_ln_g", MW, _zero), _spspec("sgu_ln_b", MW, _zero),
         pl.BlockSpec((4, GW, GW), full3), pl.BlockSpec((4, GW, 1), full3), _ANY],
        [pl.BlockSpec((2, RB, MW), lambda i: (2, i, 0)), pl.BlockSpec((4, GW, GW), full3), pl.BlockSpec((4, GW, GW), full3),
         pl.BlockSpec((8, MW), lambda i: (0, 0))],
        [SDS((NS, S, MW), CDT), SDS((4, GW, GW), F32), SDS((4, GW, GW), F32), SDS((8, MW), F32)],
        [pltpu.VMEM((RB, MW), F32)], aliases={6: 0})(z, dm, sp, sp, ws, bsc, dz)


def _unpack_wb(wb_ref, wbf):
    for j in range(NS):
        for k in range(4):
            wbf[k, :, j * GW:(j + 1) * GW] = wb_ref[j, k]


def _merge_fwd(ma, mb, mc, yd, g, wb, pwo, x, sp):
    S, D = x.shape
    TM = min(256, S)

    def body(ma_ref, mb_ref, mc_ref, yd_ref, g_ref, wb_ref, wo_ref, x_ref, lg_ref, lb_ref, qg_ref,
             xo_ref, md_ref, yk_ref, mg_ref, mo_ref, wbf):
        @pl.when(pl.program_id(0) == 0)
        def _():
            _unpack_wb(wb_ref, wbf)

        xh, _ = _ln(yd_ref[...].astype(F32))
        sl, _ = _silu_parts(xh * lg_ref[...] + lb_ref[...])
        md = sl.astype(CDT)
        md_ref[...] = md
        merged = jnp.zeros((TM, D), F32)
        for k, m in enumerate((ma_ref[...], mb_ref[...], mc_ref[...], md)):
            yk = _nn(m, wbf[k])
            yk_ref[k] = yk.astype(CDT)
            merged = merged + g_ref[k].astype(F32) * yk
        mgc = merged.astype(CDT)
        mg_ref[...] = mgc
        mo = _nn(mgc, wo_ref[...].reshape(D, D))
        mo_ref[...] = mo.astype(CDT)
        n, _ = _rms(mo)
        xo_ref[...] = x_ref[...] + n * qg_ref[...]

    row = lambda i: (i, 0)
    rowm = pl.BlockSpec((TM, MW), row)
    rowd = pl.BlockSpec((TM, D), row)
    row4 = pl.BlockSpec((4, TM, D), lambda i: (0, i, 0))
    return _call(
        body, "merge_fwd", (S // TM,),
        [rowm, rowm, rowm, rowm, row4, pl.BlockSpec((NS, 4, MW, GW), lambda i: (0, 0, 0, 0)),
         pl.BlockSpec((NS, GW, D), lambda i: (0, 0, 0)), rowd,
         _spspec("conv_d_ln_g", MW, _zero), _spspec("conv_d_ln_b", MW, _zero), _spspec("mix_post_g", D, _zero)],
        [rowd, rowm, row4, rowd, rowd],
        [SDS((S, D), F32), SDS((S, MW), CDT), SDS((4, S, D), CDT), SDS((S, D), CDT), SDS((S, D), CDT)],
        [pltpu.VMEM((4, MW, D), CDT)])(ma, mb, mc, yd, g, wb, pwo, x, sp, sp, sp)


def _merge_bwd_act(dxo, mo, g, yk, wb, pwo, sp, dep):
    S, D = dxo.shape
    TM = min(256, S)

    def body(dxo_ref, mo_ref, g_ref, yk_ref, wb_ref, wo_ref, qg_ref, dep_ref, dmo_ref, dm_ref, dgp_ref, dyk_ref, gp_ref, wbf):
        @pl.when(pl.program_id(0) == 0)
        def _():
            gp_ref[...] = jnp.zeros_like(gp_ref)
            _unpack_wb(wb_ref, wbf)

        n, r = _rms(mo_ref[...].astype(F32))
        dmo, dg = _rms_bwd(n, r, qg_ref[...], dxo_ref[...])
        gp_ref[0:1, :] += dg
        dmoc = dmo.astype(CDT)
        dmo_ref[...] = dmoc
        dmg = _nt(dmoc, wo_ref[...].reshape(D, D))
        for k in range(4):
            gk = g_ref[k].astype(F32)
            dyk = (dmg * gk).astype(CDT)
            dyk_ref[k] = dyk
            dgp_ref[k] = (dmg * yk_ref[k].astype(F32) * gk * (1.0 - gk)).astype(CDT)
            dm_ref[k] = _nt(dyk, wbf[k]).astype(CDT)

    rowd = pl.BlockSpec((TM, D), lambda i: (i, 0))
    row4 = pl.BlockSpec((4, TM, D), lambda i: (0, i, 0))
    return _call(
        body, "merge_bwd_act", (S // TM,),
        [rowd, rowd, row4, row4, pl.BlockSpec((NS, 4, MW, GW), lambda i: (0, 0, 0, 0)),
         pl.BlockSpec((NS, GW, D), lambda i: (0, 0, 0)), _spspec("mix_post_g", D, _zero), _ANY],
        [rowd, pl.BlockSpec((4, TM, MW), lambda i: (0, i, 0)), row4, row4, pl.BlockSpec((8, D), lambda i: (0, 0))],
        [SDS((S, D), CDT), SDS((4, S, MW), CDT), SDS((4, S, D), CDT), SDS((4, S, D), CDT), SDS((8, D), F32)],
        [pltpu.VMEM((4, MW, D), CDT)])(dxo, mo, g, yk, wb, pwo, sp, dep)


def _merge_bwd_w(ma, mb, mc, md, dyk, mg, dmo):
    S, D = dmo.shape
    TK = min(512, S)
    NK = S // TK

    def body(ma_ref, mb_ref, mc_ref, md_ref, dyk_ref, mg_ref, dmo_ref, gwb_ref, gwo_ref, accb, acco):
        k = pl.program_id(0)

        @pl.when(k == 0)
        def _():
            accb[...] = jnp.zeros_like(accb)
            acco[...] = jnp.zeros_like(acco)

        for b, m in enumerate((ma_ref, mb_ref, mc_ref, md_ref)):
            accb[b] += _tn(m[...], dyk_ref[b])
        acco[...] += _tn(mg_ref[...], dmo_ref[...])

        @pl.when(k == NK - 1)
        def _():
            for j in range(NS):
                for b in range(4):
                    gwb_ref[j, b] = accb[b, :, j * GW:(j + 1) * GW].astype(CDT)
                gwo_ref[j] = acco[j * GW:(j + 1) * GW, :].astype(CDT)

    rowm = pl.BlockSpec((TK, MW), lambda k: (k, 0))
    rowd = pl.BlockSpec((TK, D), lambda k: (k, 0))
    return _call(
        body, "merge_bwd_w", (NK,),
        [rowm, rowm, rowm, rowm, pl.BlockSpec((4, TK, D), lambda k: (0, k, 0)), rowd, rowd],
        [pl.BlockSpec((NS, 4, MW, GW), lambda k: (0, 0, 0, 0)), pl.BlockSpec((NS, GW, D), lambda k: (0, 0, 0))],
        [SDS((NS, 4, MW, GW), CDT), SDS((NS, GW, D), CDT)],
        [pltpu.VMEM((4, MW, D), F32), pltpu.VMEM((D, D), F32)])(ma, mb, mc, md, dyk, mg, dmo)


def _xa_kv(mem, sp, pxa):
    M, D = mem.shape

    def body(m_ref, g_ref, wk_ref, wv_ref, mn_ref, k_ref, v_ref):
        n, _ = _rms(m_ref[...])
        mn = (n * g_ref[...]).astype(CDT)
        mn_ref[...] = mn
        k_ref[...] = _nn(mn, wk_ref[...].reshape(D, D)).astype(CDT)
        v_ref[...] = _nn(mn, wv_ref[...].reshape(D, D)).astype(CDT)

    full = pl.BlockSpec((M, D), lambda i: (0, 0))
    return _call(
        body, "xa_kv", (1,),
        [full, _spspec("mem_g", D, _zero), pl.BlockSpec((NS, GW, D), lambda i: (0, 1, 0)),
         pl.BlockSpec((NS, GW, D), lambda i: (0, 2, 0))],
        [full, full, full], [SDS((M, D), CDT)] * 3)(mem, sp, pxa, pxa)


def _softmax(s):
    e = jnp.exp(s - jnp.max(s, axis=-1, keepdims=True))
    return e / jnp.sum(e, axis=-1, keepdims=True)


def _xa_fwd(x, kk, vv, sp, pxa):
    S, D = x.shape
    M = kk.shape[0]
    TM = min(512, S)
    HD = D // XA_HEADS
    scale = HD ** -0.5

    def body(x_ref, k_ref, v_ref, pg_ref, qg_ref, wq_ref, wo_ref, xo_ref, hb_ref, q_ref, o_ref, po_ref):
        n, _ = _rms(x_ref[...])
        hb = (n * pg_ref[...]).astype(CDT)
        hb_ref[...] = hb
        q = _nn(hb, wq_ref[...].reshape(D, D)).astype(CDT)
        q_ref[...] = q
        for h in range(XA_HEADS):
            hs = slice(h * HD, (h + 1) * HD)
            p = _softmax(_nt(q[:, hs], k_ref[:, hs]) * scale)
            o_ref[:, hs] = _nn(p.astype(CDT), v_ref[:, hs]).astype(CDT)
        po = _nn(o_ref[...], wo_ref[...].reshape(D, D))
        po_ref[...] = po.astype(CDT)
        n, _ = _rms(po)
        xo_ref[...] = x_ref[...] + n * qg_ref[...]

    row = pl.BlockSpec((TM, D), lambda i: (i, 0))
    full = pl.BlockSpec((M, D), lambda i: (0, 0))
    return _call(
        body, "xa_fwd", (S // TM,),
        [row, full, full, _spspec("xa_pre_g", D, _zero), _spspec("xa_post_g", D, _zero),
         pl.BlockSpec((NS, GW, D), lambda i: (0, 0, 0)), pl.BlockSpec((NS, GW, D), lambda i: (0, 3, 0))],
        [row] * 5, [SDS((S, D), F32)] + [SDS((S, D), CDT)] * 4)(x, kk, vv, sp, sp, pxa, pxa)


def _xa_bwd_act(dxo, x, po, q, kk, vv, sp, pxa, dep):
    S, D = x.shape
    M = kk.shape[0]
    TM = min(512, S)
    HD = D // XA_HEADS
    scale = HD ** -0.5

    def body(dxo_ref, x_ref, po_ref, q_ref, k_ref, v_ref, pg_ref, qg_ref, wq_ref, wo_ref, dep_ref,
             dx_ref, dpo_ref, dq_ref, dk_ref, dv_ref, gp_ref):
        @pl.when(pl.program_id(0) == 0)
        def _():
            gp_ref[...] = jnp.zeros_like(gp_ref)
            dk_ref[...] = jnp.zeros_like(dk_ref)
            dv_ref[...] = jnp.zeros_like(dv_ref)

        n, r = _rms(po_ref[...].astype(F32))
        dpo, dg = _rms_bwd(n, r, qg_ref[...], dxo_ref[...])
        gp_ref[1:2, :] += dg
        dpoc = dpo.astype(CDT)
        dpo_ref[...] = dpoc
        do = _nt(dpoc, wo_ref[...].reshape(D, D)).astype(CDT)
        for h in range(XA_HEADS):
            hs = slice(h * HD, (h + 1) * HD)
            qh = q_ref[:, hs]
            p = _softmax(_nt(qh, k_ref[:, hs]) * scale)
            pc = p.astype(CDT)
            dv_ref[:, hs] += _tn(pc, do[:, hs])
            dp = _nt(do[:, hs], v_ref[:, hs])
            ds = (p * (dp - jnp.sum(p * dp, axis=-1, keepdims=True)) * scale).astype(CDT)
            dq_ref[:, hs] = _nn(ds, k_ref[:, hs]).astype(CDT)
            dk_ref[:, hs] += _tn(ds, qh)
        dhb = _nt(dq_ref[...], wq_ref[...].reshape(D, D))
        n, r = _rms(x_ref[...])
        dx, dg = _rms_bwd(n, r, pg_ref[...], dhb)
        dx_ref[...] = dxo_ref[...] + dx
        gp_ref[0:1, :] += dg

    row = pl.BlockSpec((TM, D), lambda i: (i, 0))
    full = pl.BlockSpec((M, D), lambda i: (0, 0))
    return _call(
        body, "xa_bwd_act", (S // TM,),
        [row, row, row, row, full, full, _spspec("xa_pre_g", D, _zero), _spspec("xa_post_g", D, _zero),
         pl.BlockSpec((NS, GW, D), lambda i: (0, 0, 0)), pl.BlockSpec((NS, GW, D), lambda i: (0, 3, 0)), _ANY],
        [row, row, row, full, full, pl.BlockSpec((8, D), lambda i: (0, 0))],
        [SDS((S, D), F32), SDS((S, D), CDT), SDS((S, D), CDT), SDS((M, D), F32), SDS((M, D), F32), SDS((8, D), F32)],
    )(dxo, x, po, q, kk, vv, sp, sp, pxa, pxa, dep)


def _xa_bwd_w(hb, dq, o, dpo, mn, dk, dv):
    S, D = hb.shape
    M = mn.shape[0]
    TK = min(512, S)
    NK = S // TK

    def body(hb_ref, dq_ref, o_ref, dpo_ref, mn_ref, dk_ref, dv_ref, g_ref, accq, acco):
        k = pl.program_id(0)

        @pl.when(k == 0)
        def _():
            accq[...] = jnp.zeros_like(accq)
            acco[...] = jnp.zeros_like(acco)

        accq[...] += _tn(hb_ref[...], dq_ref[...])
        acco[...] += _tn(o_ref[...], dpo_ref[...])

        @pl.when(k == NK - 1)
        def _():
            gk = _tn(mn_ref[...], dk_ref[...].astype(CDT))
            gv = _tn(mn_ref[...], dv_ref[...].astype(CDT))
            for j in range(NS):
                rs = slice(j * GW, (j + 1) * GW)
                g_ref[j, 0:GW, :] = accq[rs, :].astype(CDT)
                g_ref[j, GW:2 * GW, :] = gk[rs, :].astype(CDT)
                g_ref[j, 2 * GW:3 * GW, :] = gv[rs, :].astype(CDT)
                g_ref[j, 3 * GW:4 * GW, :] = acco[rs, :].astype(CDT)

    rowb = pl.BlockSpec((TK, D), lambda k: (k, 0))
    full = pl.BlockSpec((M, D), lambda k: (0, 0))
    return _call(
        body, "xa_bwd_w", (NK,),
        [rowb, rowb, rowb, rowb, full, full, full],
        pl.BlockSpec((NS, 4 * GW, D), lambda k: (0, 0, 0)), SDS((NS, 4 * GW, D), CDT),
        [pltpu.VMEM((D, D), F32), pltpu.VMEM((D, D), F32)])(hb, dq, o, dpo, mn, dk, dv)


def _xa_kv_bwd(mem, dk, dv, sp, pxa):
    M, D = mem.shape

    def body(m_ref, dk_ref, dv_ref, wk_ref, wv_ref, gp_ref):
        dmn = _nt(dk_ref[...].astype(CDT), wk_ref[...].reshape(D, D)) + _nt(dv_ref[...].astype(CDT), wv_ref[...].reshape(D, D))
        n, _ = _rms(m_ref[...])
        gp_ref[...] = jnp.zeros_like(gp_ref)
        gp_ref[0:1, :] = jnp.sum(dmn * n, axis=0, keepdims=True)

    full = pl.BlockSpec((M, D), lambda i: (0, 0))
    return _call(
        body, "xa_kv_bwd", (1,),
        [full, full, full, pl.BlockSpec((NS, GW, D), lambda i: (0, 1, 0)), pl.BlockSpec((NS, GW, D), lambda i: (0, 2, 0))],
        pl.BlockSpec((8, D), lambda i: (0, 0)), SDS((8, D), F32))(mem, dk, dv, pxa, pxa)


def _loss_head(y, t):
    S, D = y.shape
    TM = min(512, S)

    def body(y_ref, t_ref, dy_ref, l_ref):
        @pl.when(pl.program_id(0) == 0)
        def _():
            l_ref[...] = jnp.zeros_like(l_ref)

        e = y_ref[...] - t_ref[...]
        dy_ref[...] = e * (1.0 / D)
        l_ref[...] += 0.5 * jnp.sum(jnp.mean(e * e, axis=-1, keepdims=True), axis=0, keepdims=True)

    row = pl.BlockSpec((TM, D), lambda i: (i, 0))
    return _call(body, "loss_head", (S // TM,), [row, row], [row, pl.BlockSpec((8, 128), lambda i: (0, 0))],
                 [SDS((S, D), F32), SDS((8, 128), F32)])(y, t)


def _row_tile(rows, cols, limit=1 << 18, step=8):
    if rows * cols <= limit or rows % step:
        return rows
    best = step
    for t in range(step, rows + 1, step):
        if rows % t == 0 and t * cols <= limit:
            best = t
    return best


def _adamw(w, g, m, v):
    shape = w.shape
    C = shape[-1]
    R = w.size // C
    TR = _row_tile(R, C)
    c1 = 1.0 - ADAM_B1 ** ADAM_STEP
    c2 = 1.0 - ADAM_B2 ** ADAM_STEP

    def body(w_ref, g_ref, m_ref, v_ref, d_ref, nm_ref, nv_ref):
        gg = g_ref[...]
        nm = ADAM_B1 * m_ref[...] + (1.0 - ADAM_B1) * gg
        nv = ADAM_B2 * v_ref[...] + (1.0 - ADAM_B2) * (gg * gg)
        nm_ref[...] = nm
        nv_ref[...] = nv
        d_ref[...] = -ADAM_LR * ((nm / c1) / (jnp.sqrt(nv / c2) + ADAM_EPS) + ADAM_WD * w_ref[...])

    blk = pl.BlockSpec((TR, C), lambda i: (i, 0))
    outs = _call(body, "adamw", (R // TR,), [blk] * 4, [blk] * 3, [SDS((R, C), F32)] * 3)(
        w.reshape(R, C), g.reshape(R, C), m.reshape(R, C), v.reshape(R, C))
    return tuple(o.reshape(shape) for o in outs)


def _adamw_block(w, gs, m, v, gblock):
    L, R, C = w.shape
    c1 = 1.0 - ADAM_B1 ** ADAM_STEP
    c2 = 1.0 - ADAM_B2 ** ADAM_STEP

    def body(w_ref, g_ref, m_ref, v_ref, go_ref, d_ref, nm_ref, nv_ref):
        gg = g_ref[...]
        go_ref[...] = gg
        nm = ADAM_B1 * m_ref[...] + (1.0 - ADAM_B1) * gg
        nv = ADAM_B2 * v_ref[...] + (1.0 - ADAM_B2) * (gg * gg)
        nm_ref[...] = nm
        nv_ref[...] = nv
        d_ref[...] = -ADAM_LR * ((nm / c1) / (jnp.sqrt(nv / c2) + ADAM_EPS) + ADAM_WD * w_ref[...])

    blk = pl.BlockSpec((None, R, C), lambda l: (l, 0, 0))
    return _call(body, "adamw_block", (L,), [blk, pl.BlockSpec((None, R, C), lambda l: (l, gblock, 0)), blk, blk],
                 [blk] * 4, [SDS((L, R, C), F32)] * 4)(w, gs, m, v)


def _slot_sum_into(stacked, r, l):
    _, R, C = r.shape
    TR = _row_tile(R, C * NS, limit=1 << 21, step=16)

    def body(r_ref, s_ref, o_ref):
        acc = r_ref[0].astype(F32)
        for j in range(1, NS):
            acc = acc + r_ref[j].astype(F32)
        o_ref[...] = acc

    return pl.pallas_call(
        body, name="slot_sum_into", grid=(R // TR,),
        in_specs=[pl.BlockSpec((NS, TR, C), lambda i: (0, i, 0)), _ANY],
        out_specs=pl.BlockSpec((None, TR, C), lambda i: (l, i, 0)), out_shape=SDS(stacked.shape, F32),
        input_output_aliases={1: 0},
        compiler_params=pltpu.CompilerParams(dimension_semantics=("arbitrary",), vmem_limit_bytes=VMEM_LIMIT))(r, stacked)


def _exchange(arrs, scatter, name):
    n = len(arrs)
    np_ = NS - 1

    def body(*refs):
        ins, outs = refs[:n], refs[n:2 * n]
        send_sems, recv_sems, loc_sems = refs[2 * n:]
        x, y, c = lax.axis_index("x"), lax.axis_index("y"), lax.axis_index("c")
        me = 4 * x + 2 * y + c
        peers = []
        for f in range(1, NS):
            px = 1 - x if f & 4 else x
            py = 1 - y if f & 2 else y
            pc = 1 - c if f & 1 else c
            peers.append(((px, py, pc), 4 * px + 2 * py + pc))

        def src(a, pid):
            return ins[a].at[pid] if scatter else ins[a]

        local = [pltpu.make_async_copy(src(a, me), outs[a].at[me], loc_sems.at[a]) for a in range(n)]
        for cp in local:
            cp.start()
        sends = []
        for a in range(n):
            for f, (dev, pid) in enumerate(peers):
                sends.append(pltpu.make_async_remote_copy(
                    src_ref=src(a, pid), dst_ref=outs[a].at[me], send_sem=send_sems.at[a * np_ + f],
                    recv_sem=recv_sems.at[a * np_ + f], device_id=dev, device_id_type=pl.DeviceIdType.MESH))
        for cp in sends:
            cp.start()
        for a in range(n):
            for f, (dev, pid) in enumerate(peers):
                pltpu.make_async_remote_copy(
                    src_ref=src(a, pid), dst_ref=outs[a].at[pid], send_sem=send_sems.at[a * np_ + f],
                    recv_sem=recv_sems.at[a * np_ + f], device_id=dev, device_id_type=pl.DeviceIdType.MESH).wait_recv()
        for cp in sends:
            cp.wait_send()
        for cp in local:
            cp.wait()

    out_shape = [SDS(a.shape if scatter else (NS,) + a.shape, a.dtype) for a in arrs]
    anyspec = pl.BlockSpec(memory_space=pl.ANY)
    outs = pl.pallas_call(
        body, name=name, in_specs=[anyspec] * n, out_specs=[anyspec] * n, out_shape=out_shape,
        scratch_shapes=[pltpu.SemaphoreType.DMA((n * np_,)), pltpu.SemaphoreType.DMA((n * np_,)),
                        pltpu.SemaphoreType.DMA((n,))],
        compiler_params=pltpu.CompilerParams(has_side_effects=True))(*arrs)
    return list(outs)


def _peers():
    x, y, c = lax.axis_index("x"), lax.axis_index("y"), lax.axis_index("c")
    out = []
    for f in range(1, NS):
        px = 1 - x if f & 4 else x
        py = 1 - y if f & 2 else y
        pc = 1 - c if f & 1 else c
        out.append(((px, py, pc), 4 * px + 2 * py + pc))
    return 4 * x + 2 * y + c, out


def _exchange_copies(ins, lands, scatter, send_sems, recv_sems, loc_sems):
    me, peers = _peers()
    np_ = NS - 1

    def src(a, pid):
        return ins[a].at[pid] if scatter[a] else ins[a]

    def rcopy(a, f, dev, land_slot):
        return pltpu.make_async_remote_copy(
            src_ref=src(a, peers[f][1]), dst_ref=lands[a].at[land_slot], send_sem=send_sems.at[a * np_ + f],
            recv_sem=recv_sems.at[a * np_ + f], device_id=dev, device_id_type=pl.DeviceIdType.MESH)

    local = [pltpu.make_async_copy(src(a, me), lands[a].at[me], loc_sems.at[a]) for a in range(len(ins))]
    sends = [rcopy(a, f, dev, me) for a in range(len(ins)) for f, (dev, _) in enumerate(peers)]
    arrivals = [rcopy(a, f, dev, pid) for a in range(len(ins)) for f, (dev, pid) in enumerate(peers)]
    return local, sends, arrivals


_HBM = pl.BlockSpec(memory_space=pltpu.HBM)
_SEM = pl.BlockSpec(memory_space=pltpu.SEMAPHORE)
_ANY = pl.BlockSpec(memory_space=pl.ANY)


def _exchange_start(arrs, scatter, after, name):
    n = len(arrs)
    np_ = NS - 1
    lands = [lax.empty(a.shape if sc else (NS,) + a.shape, a.dtype) for a, sc in zip(arrs, scatter)]

    def body(*refs):
        ins, lnd = refs[:n], refs[n:2 * n]
        send_sems, recv_sems, loc_sems = refs[2 * n + 1:2 * n + 4]
        token = refs[-1]
        local, sends, _ = _exchange_copies(ins, lnd, scatter, send_sems, recv_sems, loc_sems)
        for cp in local + sends:
            cp.start()
        token[...] = jnp.zeros_like(token)

    hbm = lambda a: pltpu.HBM(a.shape, a.dtype)
    outs = pl.pallas_call(
        body, name=name,
        out_shape=(pltpu.SemaphoreType.DMA((n * np_,)), pltpu.SemaphoreType.DMA((n * np_,)), pltpu.SemaphoreType.DMA((n,)),
                   *[hbm(a) for a in arrs], *[hbm(a) for a in lands], SDS((8, 128), F32)),
        in_specs=[_HBM] * (2 * n) + [_ANY],
        out_specs=(_SEM, _SEM, _SEM, *([_HBM] * (2 * n)), pl.BlockSpec(memory_space=pltpu.VMEM)),
        input_output_aliases={i: 3 + i for i in range(2 * n)},
        compiler_params=pltpu.CompilerParams(has_side_effects=pltpu.SideEffectType.DATAFLOW_SIDE_EFFECTING),
    )(*[pltpu.with_memory_space_constraint(a, pltpu.HBM) for a in list(arrs) + lands], after)
    return {"sems": outs[:3], "ins": outs[3:3 + n], "lands": outs[3 + n:3 + 2 * n], "token": outs[-1], "scatter": scatter}


def _exchange_wait(h, after, name):
    n = len(h["ins"])
    scatter = h["scatter"]
    after = list(after) if isinstance(after, (list, tuple)) else [after]

    def body(*refs):
        ins, lnd = refs[:n], refs[n:2 * n]
        send_sems, recv_sems, loc_sems = refs[2 * n:2 * n + 3]
        local, sends, arrivals = _exchange_copies(ins, lnd, scatter, send_sems, recv_sems, loc_sems)
        for cp in sends:
            cp.wait_send()
        for cp in arrivals:
            cp.wait_recv()
        for cp in local:
            cp.wait()

    hbm = lambda a: pltpu.HBM(a.shape, a.dtype)
    outs = pl.pallas_call(
        body, name=name,
        out_shape=tuple(hbm(a) for a in list(h["ins"]) + list(h["lands"])),
        in_specs=[_HBM] * (2 * n) + [_SEM] * 3 + [_ANY] * len(after),
        out_specs=tuple([_HBM] * (2 * n)),
        input_output_aliases={i: i for i in range(2 * n)},
        compiler_params=pltpu.CompilerParams(has_side_effects=pltpu.SideEffectType.DATAFLOW_SIDE_EFFECTING),
    )(*h["ins"], *h["lands"], *h["sems"], *after)
    return list(outs[n:])


def _hgather_copies(ins, lands, send_a, recv_a, send_b, recv_b, loc_sems):
    x, y, c = lax.axis_index("x"), lax.axis_index("y"), lax.axis_index("c")
    me = 4 * x + 2 * y + c
    sib = (x, y, 1 - c)
    chips = [(1 - x, y), (x, 1 - y), (1 - x, 1 - y)]
    slot = lambda px, py, pc: 4 * px + 2 * py + pc

    def rcopy(src, dst, ssem, rsem, dev):
        return pltpu.make_async_remote_copy(src_ref=src, dst_ref=dst, send_sem=ssem, recv_sem=rsem, device_id=dev,
                                            device_id_type=pl.DeviceIdType.MESH)

    local, s1, a1, s2, a2 = [], [], [], [], []
    for a in range(len(ins)):
        first = [(sib, slot(x, y, 1 - c))] + [((px, py, c), slot(px, py, c)) for px, py in chips]
        for k, (dev, origin) in enumerate(first if send_a is not None else ()):
            s1.append(rcopy(ins[a], lands[a].at[me], send_a.at[4 * a + k], recv_a.at[4 * a + k], dev))
            a1.append(rcopy(ins[a], lands[a].at[origin], send_a.at[4 * a + k], recv_a.at[4 * a + k], dev))
        if send_a is not None:
            local.append(pltpu.make_async_copy(ins[a], lands[a].at[me], loc_sems.at[a]))
        for k, (px, py) in enumerate(chips if send_b is not None else ()):
            mine, theirs = lands[a].at[slot(px, py, c)], lands[a].at[slot(px, py, 1 - c)]
            s2.append(rcopy(mine, mine, send_b.at[3 * a + k], recv_b.at[3 * a + k], sib))
            a2.append(rcopy(mine, theirs, send_b.at[3 * a + k], recv_b.at[3 * a + k], sib))
    return local, s1, a1, s2, a2


def _hgather_start(arrs, after, name):
    n = len(arrs)
    lands = [lax.empty((NS,) + a.shape, a.dtype) for a in arrs]

    def body(*refs):
        ins, lnd = refs[:n], refs[n:2 * n]
        send_a, recv_a, loc_sems = refs[2 * n + 1:2 * n + 4]
        token = refs[-1]
        local, s1, _, _, _ = _hgather_copies(ins, lnd, send_a, recv_a, None, None, loc_sems)
        for cp in local + s1:
            cp.start()
        token[...] = jnp.zeros_like(token)

    hbm = lambda a: pltpu.HBM(a.shape, a.dtype)
    outs = pl.pallas_call(
        body, name=name,
        out_shape=(pltpu.SemaphoreType.DMA((4 * n,)), pltpu.SemaphoreType.DMA((4 * n,)), pltpu.SemaphoreType.DMA((n,)),
                   *[hbm(a) for a in arrs], *[hbm(a) for a in lands], SDS((8, 128), F32)),
        in_specs=[_HBM] * (2 * n) + [_ANY],
        out_specs=(_SEM, _SEM, _SEM, *([_HBM] * (2 * n)), pl.BlockSpec(memory_space=pltpu.VMEM)),
        input_output_aliases={i: 3 + i for i in range(2 * n)},
        compiler_params=pltpu.CompilerParams(has_side_effects=pltpu.SideEffectType.DATAFLOW_SIDE_EFFECTING),
    )(*[pltpu.with_memory_space_constraint(a, pltpu.HBM) for a in list(arrs) + lands], after)
    return {"sems": outs[:3], "ins": outs[3:3 + n], "lands": outs[3 + n:3 + 2 * n], "token": outs[-1]}


def _hgather_forward(h, after, name):
    n = len(h["ins"])

    def body(*refs):
        ins, lnd = refs[:n], refs[n:2 * n]
        send_a, recv_a, loc_sems = refs[2 * n:2 * n + 3]
        send_b, recv_b = refs[2 * n + 4:2 * n + 6]
        token = refs[-1]
        local, s1, a1, s2, _ = _hgather_copies(ins, lnd, send_a, recv_a, send_b, recv_b, loc_sems)
        for cp in s1:
            cp.wait_send()
        for cp in a1:
            cp.wait_recv()
        for cp in local:
            cp.wait()
        for cp in s2:
            cp.start()
        token[...] = jnp.zeros_like(token)

    hbm = lambda a: pltpu.HBM(a.shape, a.dtype)
    outs = pl.pallas_call(
        body, name=name,
        out_shape=(pltpu.SemaphoreType.DMA((3 * n,)), pltpu.SemaphoreType.DMA((3 * n,)),
                   *[hbm(a) for a in list(h["ins"]) + list(h["lands"])], SDS((8, 128), F32)),
        in_specs=[_HBM] * (2 * n) + [_SEM] * 3 + [_ANY],
        out_specs=(_SEM, _SEM, *([_HBM] * (2 * n)), pl.BlockSpec(memory_space=pltpu.VMEM)),
        input_output_aliases={i: 2 + i for i in range(2 * n)},
        compiler_params=pltpu.CompilerParams(has_side_effects=pltpu.SideEffectType.DATAFLOW_SIDE_EFFECTING),
    )(*h["ins"], *h["lands"], *h["sems"], after)
    return {"sems": outs[:2], "ins": outs[2:2 + n], "lands": outs[2 + n:2 + 2 * n], "token": outs[-1]}


def _hgather_wait(h, after, name):
    n = len(h["ins"])

    def body(*refs):
        ins, lnd = refs[:n], refs[n:2 * n]
        send_b, recv_b = refs[2 * n:2 * n + 2]
        _, _, _, s2, a2 = _hgather_copies(ins, lnd, None, None, send_b, recv_b, None)
        for cp in s2:
            cp.wait_send()
        for cp in a2:
            cp.wait_recv()

    hbm = lambda a: pltpu.HBM(a.shape, a.dtype)
    outs = pl.pallas_call(
        body, name=name,
        out_shape=tuple(hbm(a) for a in list(h["ins"]) + list(h["lands"])),
        in_specs=[_HBM] * (2 * n) + [_SEM] * 2 + [_ANY],
        out_specs=tuple([_HBM] * (2 * n)),
        input_output_aliases={i: i for i in range(2 * n)},
        compiler_params=pltpu.CompilerParams(has_side_effects=pltpu.SideEffectType.DATAFLOW_SIDE_EFFECTING),
    )(*h["ins"], *h["lands"], *h["sems"], after)
    return list(outs[n:])


_W_NAMES = ("ffn1_pre_g", "ffn1_post_g", "ffn1_w1", "ffn1_w3", "ffn1_w2", "mix_pre_g", "mix_post_g", "w_in", "conv_a_w",
            "conv_a_b", "pool_w", "pool_scale", "sgu_ln_g", "sgu_ln_b", "sgu_ws", "sgu_b", "conv_d_w", "conv_d_b",
            "conv_d_ln_g", "conv_d_ln_b", "w_branch", "w_gate", "b_gate", "w_o", "xa_pre_g", "xa_post_g", "mem_g",
            "xa_wq", "xa_wk", "xa_wv", "xa_wo", "ffn2_pre_g", "ffn2_post_g", "ffn2_w1", "ffn2_w3", "ffn2_w2")
_REP_NAMES = tuple(n for n, _ in _SP_NAMES) + ("pool_w", "sgu_ws", "sgu_b", "conv_a_w", "conv_d_w")


def _t(w):
    return jnp.swapaxes(w, -1, -2)


def _step(x, mem, loss_target, W, M, V):
    L = W["w_in"].shape[0]
    S, D = x.shape[1], x.shape[2]
    x0 = x.reshape(S, D)
    memf = mem.reshape(mem.shape[1], D)
    me = 4 * lax.axis_index("x") + 2 * lax.axis_index("y") + lax.axis_index("c")
    FS = W["ffn1_w2"].shape[1]
    KA, KD = W["conv_a_w"].shape[1], W["conv_d_w"].shape[1]
    CS = W["conv_a_w"].shape[2]

    cat = lambda l, parts: jnp.concatenate([(_t(W[n][l]) if tr else W[n][l]) for n, tr in parts], axis=0).astype(CDT)
    pf1 = [cat(l, (("ffn1_w1", 1), ("ffn1_w3", 1), ("ffn1_w2", 0))) for l in range(L)]
    pf2 = [cat(l, (("ffn2_w1", 1), ("ffn2_w3", 1), ("ffn2_w2", 0))) for l in range(L)]
    pma = [cat(l, (("w_in", 1), ("w_gate", 1))) for l in range(L)]
    pwo = [W["w_o"][l].astype(CDT) for l in range(L)]
    pxa = [cat(l, (("xa_wq", 0), ("xa_wk", 0), ("xa_wv", 0), ("xa_wo", 0))) for l in range(L)]
    wbs = [W["w_branch"][l].astype(CDT) for l in range(L)]
    cws = jnp.concatenate([W["conv_a_w"], W["conv_d_w"]], axis=1).reshape(-1, 128)
    sp_all = jnp.concatenate([W[n] for n, _ in _SP_NAMES], axis=1)
    bsc_all = W["sgu_b"][..., None]

    (cwg,) = _exchange([cws], False, "gather_conv_w")
    cwf = cwg.reshape(NS, L, KA + KD, CS).transpose(1, 2, 0, 3).reshape(L, KA + KD, NS * CS)

    def gather_start(l, after):
        return _hgather_start([pf1[l], pf2[l], pma[l], pwo[l], pxa[l], wbs[l]], after, f"gather_start_{l}")

    def gather_rest(h, after, tag):
        mid = _hgather_forward(h, after, f"gather_forward_{tag}")
        return _hgather_wait(mid, mid["token"], f"gather_wait_{tag}")

    packs = [None] * L
    first = [_hgather_start([pf1[0]], cwg, "gather_start_0a")]
    saved = []
    xc = x0
    for l in range(L):
        if l == 0:
            (gf1,) = gather_rest(first[0], sp_all, "0a")
            first.append(_hgather_start([pma[0], pwo[0], wbs[0]], gf1, "gather_start_0b"))
        else:
            gf1, gf2, gma, gwo, gxa, gwb = packs[l]
        sp = sp_all[l:l + 1]
        cwa, cwd = cwf[l, :KA], cwf[l, KA:]
        wp, ws, bsc = W["pool_w"][l], W["sgu_ws"][l], bsc_all[l]
        s = {"x0": xc}
        nxt = gather_start(l + 1, gf1) if 0 < l < L - 1 else None
        xc, s["hb1"], s["a1"], s["b1"], s["y1"] = _ffn_fwd(xc, sp, "ffn1_pre_g", "ffn1_post_g", gf1,
                                                            (first[1] if l == 0 else nxt)["token"] if l == 0 or nxt else sp)
        s["x1"] = xc
        if l == 0:
            gma, gwo, gwb = gather_rest(first[1], xc, "0b")
            first.append(_hgather_start([pxa[0]], gma, "gather_start_0c"))
            first.append(_hgather_start([pf2[0]], first[2]["token"], "gather_start_0d"))
            nxt = gather_start(1, first[3]["token"]) if L > 1 else None
        s["hbm"], s["z"], s["g"] = _mix_in(xc, sp, gma, (nxt or first[3])["token"] if l == 0 else sp)
        s["ma"] = _mixA_fwd(s["z"], cwa, sp)
        s["mb"] = _mixB_fwd(s["z"], wp, sp)
        s["mc"] = _mixC_fwd(s["z"], ws, bsc, sp)
        s["yd"] = _mixD_conv_fwd(s["z"], cwd, sp)
        xc, s["md"], s["yk"], s["mg"], s["mo"] = _merge_fwd(s["ma"], s["mb"], s["mc"], s["yd"], s["g"], gwb, gwo, xc, sp)
        s["x2"] = xc
        if l == 0:
            (gxa,) = gather_rest(first[2], xc, "0c")
        s["mn"], s["k"], s["v"] = _xa_kv(memf, sp, gxa)
        xc, s["hbx"], s["q"], s["o"], s["po"] = _xa_fwd(xc, s["k"], s["v"], sp, gxa)
        s["x3"] = xc
        if l == 0:
            (gf2,) = gather_rest(first[3], xc, "0d")
            packs[0] = (gf1, gf2, gma, gwo, gxa, gwb)
        mid = _hgather_forward(nxt, xc, f"gather_forward_{l + 1}") if nxt and l > 0 else None
        xc, s["hb2"], s["a2"], s["b2"], s["y2"] = _ffn_fwd(xc, sp, "ffn2_pre_g", "ffn2_post_g", gf2,
                                                            mid["token"] if mid else sp)
        saved.append(s)
        if nxt:
            mid = mid or _hgather_forward(nxt, xc, f"gather_forward_{l + 1}")
            packs[l + 1] = _hgather_wait(mid, xc, f"gather_wait_{l + 1}")

    dx, lpart = _loss_head(xc, loss_target.reshape(S, D))
    loss = lax.psum(lpart[0, 0], ("x", "y", "c"))

    rep = {n: [None] * L for n in _REP_NAMES}
    summed = [dict() for _ in range(L)]
    pending = None
    last = []
    for l in reversed(range(L)):
        gf1, gf2, gma, gwo, gxa, gwb = packs[l]
        sp = sp_all[l:l + 1]
        cwa, cwd = cwf[l, :KA], cwf[l, KA:]
        wp, ws, bsc = W["pool_w"][l], W["sgu_ws"][l], bsc_all[l]
        s = saved[l]

        dx, dyb, da, db, gp = _ffn_bwd_act(dx, s["x3"], s["y2"], s["a2"], s["b2"], sp, "ffn2_pre_g", "ffn2_post_g", gf2,
                                           pending[1]["token"] if pending else sp)
        rep["ffn2_pre_g"][l], rep["ffn2_post_g"][l] = gp[0], gp[1]
        d_f2 = _ffn_bwd_w(s["hb2"], dyb, s["a2"], s["b2"], da, db, sp)
        if l == 0:
            last.append(_exchange_start([d_f2], (True,), dx, "scatter_start_0a"))

        dx, dpo, dq, dk, dv, gp = _xa_bwd_act(dx, s["x2"], s["po"], s["q"], s["k"], s["v"], sp, gxa,
                                              last[-1]["token"] if last else sp)
        rep["xa_pre_g"][l], rep["xa_post_g"][l] = gp[0], gp[1]
        d_xa = _xa_bwd_w(s["hbx"], dq, s["o"], dpo, s["mn"], dk, dv)
        rep["mem_g"][l] = _xa_kv_bwd(memf, dk, dv, sp, gxa)[0]
        if l == 0:
            last.append(_exchange_start([d_xa], (True,), dx, "scatter_start_0b"))

        dmo, dm, dgp, dyk, gp = _merge_bwd_act(dx, s["mo"], s["g"], s["yk"], gwb, gwo, sp, last[-1]["token"] if last else sp)
        rep["mix_post_g"][l] = gp[0]
        d_wb, d_wo = _merge_bwd_w(s["ma"], s["mb"], s["mc"], s["md"], dyk, s["mg"], dmo)
        dz, dcw, gp = _mixA_bwd(s["z"], dm, cwa, sp)
        rep["conv_a_w"][l], rep["conv_a_b"][l] = dcw, gp[0]
        dz, dwp, gp = _mixB_bwd(s["z"], dm, wp, sp, dz)
        rep["pool_w"][l], rep["pool_scale"][l] = dwp, gp[0]
        dz, dws, dbs, gp = _mixC_bwd(s["z"], dm, ws, bsc, sp, dz)
        rep["sgu_ws"][l], rep["sgu_b"][l], rep["sgu_ln_g"][l], rep["sgu_ln_b"][l] = dws, dbs[:, :, 0], gp[0], gp[1]
        dyd, gp = _mixD_ln_bwd(dm, s["yd"], sp)
        rep["conv_d_ln_g"][l], rep["conv_d_ln_b"][l] = gp[0], gp[1]
        dz, dcw, gp = _mixD_conv_bwd(s["z"], dyd, cwd, dz)
        rep["conv_d_w"][l], rep["conv_d_b"][l] = dcw, gp[0]
        dx, gp = _mix_in_bwd_act(dz, dgp, dx, s["x1"], sp, gma)
        rep["mix_pre_g"][l] = gp[0]
        d_ma, dbg = _mix_in_bwd_w(dz, dgp, s["hbm"])
        rep["b_gate"][l] = dbg[:, 0, :].reshape(-1)
        if l == 0:
            last.append(_exchange_start([d_ma, d_wo, d_wb], (True,) * 3, dx, "scatter_start_0c"))

        dx, dyb, da, db, gp = _ffn_bwd_act(dx, s["x0"], s["y1"], s["a1"], s["b1"], sp, "ffn1_pre_g", "ffn1_post_g", gf1,
                                           last[-1]["token"] if last else sp)
        rep["ffn1_pre_g"][l], rep["ffn1_post_g"][l] = gp[0], gp[1]
        flat = jnp.concatenate([rep[n][l].reshape(-1) for n in _REP_NAMES])
        flat = jnp.pad(flat, (0, -flat.size % 2048)).reshape(-1, 128).astype(CDT)
        if l == 0:
            last.append(_exchange_start([flat], (False,), dx, "scatter_start_0d"))
        d_f1 = _ffn_bwd_w(s["hb1"], dyb, s["a1"], s["b1"], da, db, last[-1]["token"] if last else sp)

        if pending:
            r = _exchange_wait(pending[1], dx, f"scatter_wait_{pending[0]}")
            summed[pending[0]] = dict(zip(("f1", "f2", "ma", "wo", "xa", "wb", "flat"), r))
        if l == 0:
            last.append(_exchange_start([d_f1], (True,), dx, "scatter_start_0e"))
        else:
            pending = (l, _exchange_start([d_f1, d_f2, d_ma, d_wo, d_xa, d_wb, flat], (True,) * 6 + (False,), dx,
                                          f"scatter_start_{l}"))

    pack_shape = {"f1": (3 * FS, D), "f2": (3 * FS, D), "ma": (2 * MW, D), "wo": (GW, D), "xa": (4 * GW, D),
                  "wb": (4 * MW, GW), "flat": tuple(flat.shape)}
    stk = {k: lax.empty((L,) + s, F32) for k, s in pack_shape.items()}

    def land(k, r, l):
        stk[k] = _slot_sum_into(stk[k], r.reshape((NS,) + pack_shape[k]), l)

    for l in range(1, L):
        for k, r in summed[l].items():
            land(k, r, l)
    (r,) = _exchange_wait(last[0], dx, "scatter_wait_0a")
    land("f2", r, 0)
    (r,) = _exchange_wait(last[1], dx, "scatter_wait_0b")
    land("xa", r, 0)

    G, deltas, new_m, new_v = {}, {}, {}, {}

    def update_block(n, k, blk, transposed):
        tr = _t if transposed else (lambda a: a)
        out = _adamw_block(tr(W[n]), stk[k], tr(M[n]), tr(V[n]), blk)
        G[n], deltas[n], new_m[n], new_v[n] = (tr(a) for a in out)
        return deltas[n]

    def update(n):
        deltas[n], new_m[n], new_v[n] = _adamw(W[n], G[n], M[n], V[n])
        return deltas[n]

    done = [update_block("ffn2_w1", "f2", 0, True), update_block("ffn2_w3", "f2", 1, True),
            update_block("ffn2_w2", "f2", 2, False)]
    done += [update_block(n, "xa", i, False) for i, n in enumerate(("xa_wq", "xa_wk", "xa_wv", "xa_wo"))]
    r = _exchange_wait(last[2], done + [stk[k] for k in ("f1", "ma", "wo", "wb", "flat")], "scatter_wait_0c")
    for k, v in zip(("ma", "wo", "wb"), r):
        land(k, v, 0)
    G["w_in"], G["w_gate"] = _t(stk["ma"][:, :MW]), _t(stk["ma"][:, MW:])
    G["w_branch"] = stk["wb"].reshape(W["w_branch"].shape)
    done = [update("w_in"), update("w_gate"), update("w_branch"), update_block("w_o", "wo", 0, False)]
    (r,) = _exchange_wait(last[3], done, "scatter_wait_0d")
    land("flat", r, 0)

    tot = [stk["flat"][l].reshape(-1) for l in range(L)]
    off = 0
    for n in _REP_NAMES:
        shape = (KA, NS * CS) if n == "conv_a_w" else (KD, NS * CS) if n == "conv_d_w" else W[n].shape[1:]
        size = 1
        for d in shape:
            size *= d
        G[n] = jnp.stack([tot[l][off:off + size].reshape(shape) for l in range(L)])
        off += size
    for n in ("conv_a_w", "conv_d_w"):
        G[n] = lax.dynamic_slice_in_dim(G[n], me * CS, CS, axis=2)
    done = [update(n) for n in _REP_NAMES]

    (r,) = _exchange_wait(last[4], done, "scatter_wait_0e")
    land("f1", r, 0)
    update_block("ffn1_w1", "f1", 0, True)
    update_block("ffn1_w3", "f1", 1, True)
    update_block("ffn1_w2", "f1", 2, False)
    grad_x = dx.reshape(x.shape)
    return (loss, grad_x, *[G[n] for n in _W_NAMES], *[deltas[n] for n in _W_NAMES],
            *[new_m[n] for n in _W_NAMES], *[new_v[n] for n in _W_NAMES])


def kernel(x, mem, ffn1_pre_g, ffn1_post_g, ffn1_w1, ffn1_w3, ffn1_w2, mix_pre_g, mix_post_g, w_in, conv_a_w, conv_a_b, pool_w, pool_scale, sgu_ln_g, sgu_ln_b, sgu_ws, sgu_b, conv_d_w, conv_d_b, conv_d_ln_g, conv_d_ln_b, w_branch, w_gate, b_gate, w_o, xa_pre_g, xa_post_g, mem_g, xa_wq, xa_wk, xa_wv, xa_wo, ffn2_pre_g, ffn2_post_g, ffn2_w1, ffn2_w3, ffn2_w2, loss_target, m_ffn1_pre_g, m_ffn1_post_g, m_ffn1_w1, m_ffn1_w3, m_ffn1_w2, m_mix_pre_g, m_mix_post_g, m_w_in, m_conv_a_w, m_conv_a_b, m_pool_w, m_pool_scale, m_sgu_ln_g, m_sgu_ln_b, m_sgu_ws, m_sgu_b, m_conv_d_w, m_conv_d_b, m_conv_d_ln_g, m_conv_d_ln_b, m_w_branch, m_w_gate, m_b_gate, m_w_o, m_xa_pre_g, m_xa_post_g, m_mem_g, m_xa_wq, m_xa_wk, m_xa_wv, m_xa_wo, m_ffn2_pre_g, m_ffn2_post_g, m_ffn2_w1, m_ffn2_w3, m_ffn2_w2, v_ffn1_pre_g, v_ffn1_post_g, v_ffn1_w1, v_ffn1_w3, v_ffn1_w2, v_mix_pre_g, v_mix_post_g, v_w_in, v_conv_a_w, v_conv_a_b, v_pool_w, v_pool_scale, v_sgu_ln_g, v_sgu_ln_b, v_sgu_ws, v_sgu_b, v_conv_d_w, v_conv_d_b, v_conv_d_ln_g, v_conv_d_ln_b, v_w_branch, v_w_gate, v_b_gate, v_w_o, v_xa_pre_g, v_xa_post_g, v_mem_g, v_xa_wq, v_xa_wk, v_xa_wv, v_xa_wo, v_ffn2_pre_g, v_ffn2_post_g, v_ffn2_w1, v_ffn2_w3, v_ffn2_w2):
    args = dict(locals())
    W = {n: args[n] for n in _W_NAMES}
    M = {n: args["m_" + n] for n in _W_NAMES}
    V = {n: args["v_" + n] for n in _W_NAMES}
    return _step(x, mem, loss_target, W, M, V)
```

```python
import jax
import jax.numpy as jnp
from jax import lax
from jax.experimental import pallas as pl
from jax.experimental.pallas import tpu as pltpu

F32 = jnp.float32
CDT = jnp.bfloat16
EPS = 1e-6
NS = 8
GW = 128
MW = 512
CHUNK = 64
XA_HEADS = 4
POOL_WINDOWS = (2, 4, 8, 16)
VMEM_LIMIT = 56 * 1024 * 1024
ADAM_LR, ADAM_B1, ADAM_B2, ADAM_EPS, ADAM_WD, ADAM_STEP = 0.001, 0.9, 0.999, 1e-08, 0.01, 10

SDS = jax.ShapeDtypeStruct

_SP_NAMES = (("ffn1_pre_g", 1024), ("ffn1_post_g", 1024), ("mix_pre_g", 1024), ("mix_post_g", 1024),
             ("xa_pre_g", 1024), ("xa_post_g", 1024), ("mem_g", 1024), ("ffn2_pre_g", 1024), ("ffn2_post_g", 1024),
             ("conv_a_b", 512), ("pool_scale", 512), ("sgu_ln_g", 512), ("sgu_ln_b", 512), ("conv_d_b", 512),
             ("conv_d_ln_g", 512), ("conv_d_ln_b", 512), ("b_gate", 4096))
_SP = {}
_off = 0
for _n, _w in _SP_NAMES:
    _SP[_n] = (_off, _w)
    _off += _w
_SP_TOTAL = _off


def _call(body, name, grid, in_specs, out_specs, out_shape, scratch=(), aliases=None):
    return pl.pallas_call(
        body, name=name, grid=grid, in_specs=in_specs, out_specs=out_specs, out_shape=out_shape,
        scratch_shapes=list(scratch), input_output_aliases=aliases or {},
        compiler_params=pltpu.CompilerParams(dimension_semantics=("arbitrary",) * len(grid),
                                             vmem_limit_bytes=VMEM_LIMIT))


def _nn(a, b):
    return lax.dot_general(a, b, (((1,), (0,)), ((), ())), preferred_element_type=F32)


def _nt(a, b):
    return lax.dot_general(a, b, (((1,), (1,)), ((), ())), preferred_element_type=F32)


def _tn(a, b):
    return lax.dot_general(a, b, (((0,), (0,)), ((), ())), preferred_element_type=F32)


def _rms(x):
    r = lax.rsqrt(jnp.mean(x * x, axis=-1, keepdims=True) + EPS)
    return x * r, r


def _rms_bwd(n, r, g, dout):
    dn = dout * g
    dx = r * (dn - n * jnp.mean(dn * n, axis=-1, keepdims=True))
    return dx, jnp.sum(dout * n, axis=0, keepdims=True)


def _ln(y):
    mu = jnp.mean(y, axis=-1, keepdims=True)
    yc = y - mu
    rs = lax.rsqrt(jnp.mean(yc * yc, axis=-1, keepdims=True) + EPS)
    return yc * rs, rs


def _ln_bwd(xh, rs, dxh):
    return rs * (dxh - jnp.mean(dxh, axis=-1, keepdims=True) - xh * jnp.mean(dxh * xh, axis=-1, keepdims=True))


def _silu_parts(a):
    s = jax.nn.sigmoid(a)
    sl = a * s
    return sl, s + sl * (1.0 - s)


_GELU_C = 0.7978845608028654
_GELU_A = 0.044715


def _gelu(x):
    return 0.5 * x * (1.0 + jnp.tanh(_GELU_C * (x + _GELU_A * x * x * x)))


def _gelu_parts(x):
    t = jnp.tanh(_GELU_C * (x + _GELU_A * x * x * x))
    g = 0.5 * x * (1.0 + t)
    dg = 0.5 * (1.0 + t) + 0.5 * x * (1.0 - t * t) * _GELU_C * (1.0 + 3.0 * _GELU_A * x * x)
    return g, dg


def _spspec(name, width, imap):
    off = _SP[name][0]
    assert off % width == 0
    return pl.BlockSpec((1, width), lambda *a: (0, off // width + imap(*a)))


def _zero(*a):
    return 0


def _row_once(tm, d):
    return pl.BlockSpec((tm, d), lambda i, j: (i, 0), pipeline_mode=pl.Buffered(1))


FFN_SG = 2


def _ffn_fwd(x, sp, pre, post, pf, dep):
    S, D = x.shape
    FS = pf.shape[1] // 3
    TM = min(512, S)
    SG, NG, W = FFN_SG, NS // FFN_SG, FFN_SG * FS

    def body(x_ref, pg_ref, qg_ref, w1_ref, w3_ref, w2_ref, dep_ref, xo_ref, hb_ref, a_ref, b_ref, y_ref, hb_s, acc):
        j = pl.program_id(1)

        @pl.when(j == 0)
        def _():
            n, _ = _rms(x_ref[...])
            hb = (n * pg_ref[...]).astype(CDT)
            hb_s[...] = hb
            hb_ref[...] = hb
            acc[...] = jnp.zeros_like(acc)

        hb = hb_s[...]
        a = _nt(hb, w1_ref[...].reshape(W, D))
        b = _nt(hb, w3_ref[...].reshape(W, D))
        a_ref[...] = a.astype(CDT)
        b_ref[...] = b.astype(CDT)
        u = (a * jax.nn.sigmoid(a) * b).astype(CDT)
        acc[...] += _nn(u, w2_ref[...].reshape(W, D))

        @pl.when(j == NG - 1)
        def _():
            y = acc[...]
            y_ref[...] = y.astype(CDT)
            n, _ = _rms(y)
            xo_ref[...] = x_ref[...] + 0.5 * (n * qg_ref[...])

    row1 = pl.BlockSpec((TM, D), lambda i, j: (i, 0))
    grp = lambda i, j: (j, i, 0)
    return _call(
        body, "ffn_fwd", (S // TM, NG),
        [row1, _spspec(pre, D, _zero), _spspec(post, D, _zero),
         pl.BlockSpec((SG, FS, D), lambda i, j: (j, 0, 0)), pl.BlockSpec((SG, FS, D), lambda i, j: (j, 1, 0)),
         pl.BlockSpec((SG, FS, D), lambda i, j: (j, 2, 0)), pl.BlockSpec(memory_space=pl.ANY)],
        [row1, row1, pl.BlockSpec((None, TM, W), grp), pl.BlockSpec((None, TM, W), grp), row1],
        [SDS((S, D), F32), SDS((S, D), CDT), SDS((NG, S, W), CDT), SDS((NG, S, W), CDT), SDS((S, D), CDT)],
        [pltpu.VMEM((TM, D), CDT), pltpu.VMEM((TM, D), F32)])(x, sp, sp, pf, pf, pf, dep)


def _ffn_bwd_act(dxo, x, y, a, b, sp, pre, post, pf, dep):
    S, D = x.shape
    FS = pf.shape[1] // 3
    TM = min(512, S)
    SG, NG, W = FFN_SG, NS // FFN_SG, FFN_SG * FS

    def body(dxo_ref, x_ref, y_ref, a_ref, b_ref, pg_ref, qg_ref, w1_ref, w3_ref, w2_ref, dep_ref,
             dx_ref, dyb_ref, da_ref, db_ref, gp_ref, dyb_s, acc):
        i = pl.program_id(0)
        j = pl.program_id(1)

        @pl.when((i == 0) & (j == 0))
        def _():
            gp_ref[...] = jnp.zeros_like(gp_ref)

        @pl.when(j == 0)
        def _():
            n, r = _rms(y_ref[...].astype(F32))
            dy, dg = _rms_bwd(n, r, qg_ref[...], 0.5 * dxo_ref[...])
            dyb = dy.astype(CDT)
            dyb_s[...] = dyb
            dyb_ref[...] = dyb
            gp_ref[1:2, :] += dg
            acc[...] = jnp.zeros_like(acc)

        sl, dsl = _silu_parts(a_ref[...].astype(F32))
        du = _nt(dyb_s[...], w2_ref[...].reshape(W, D))
        db = (du * sl).astype(CDT)
        da = (du * b_ref[...].astype(F32) * dsl).astype(CDT)
        da_ref[...] = da
        db_ref[...] = db
        acc[...] += _nn(da, w1_ref[...].reshape(W, D)) + _nn(db, w3_ref[...].reshape(W, D))

        @pl.when(j == NG - 1)
        def _():
            n, r = _rms(x_ref[...])
            dx, dg = _rms_bwd(n, r, pg_ref[...], acc[...])
            dx_ref[...] = dxo_ref[...] + dx
            gp_ref[0:1, :] += dg

    row = lambda i, j: (i, 0)
    grp = lambda i, j: (j, i, 0)
    return _call(
        body, "ffn_bwd_act", (S // TM, NG),
        [pl.BlockSpec((TM, D), row), pl.BlockSpec((TM, D), row), pl.BlockSpec((TM, D), row),
         pl.BlockSpec((None, TM, W), grp), pl.BlockSpec((None, TM, W), grp),
         _spspec(pre, D, _zero), _spspec(post, D, _zero),
         pl.BlockSpec((SG, FS, D), lambda i, j: (j, 0, 0)), pl.BlockSpec((SG, FS, D), lambda i, j: (j, 1, 0)),
         pl.BlockSpec((SG, FS, D), lambda i, j: (j, 2, 0)), pl.BlockSpec(memory_space=pl.ANY)],
        [pl.BlockSpec((TM, D), row), pl.BlockSpec((TM, D), row), pl.BlockSpec((None, TM, W), grp),
         pl.BlockSpec((None, TM, W), grp), pl.BlockSpec((8, D), lambda i, j: (0, 0))],
        [SDS((S, D), F32), SDS((S, D), CDT), SDS((NG, S, W), CDT), SDS((NG, S, W), CDT), SDS((8, D), F32)],
        [pltpu.VMEM((TM, D), CDT), pltpu.VMEM((TM, D), F32)])(dxo, x, y, a, b, sp, sp, pf, pf, pf, dep)


def _ffn_bwd_w(hb, dyb, a, b, da, db, dep):
    S, D = hb.shape
    SG, NG = FFN_SG, NS // FFN_SG
    W = a.shape[2]
    FS = W // SG
    TK = min(512, S)
    NK = S // TK

    def body(hb_ref, dyb_ref, a_ref, b_ref, da_ref, db_ref, dep_ref, g_ref, acc):
        k = pl.program_id(1)

        @pl.when(k == 0)
        def _():
            acc[...] = jnp.zeros_like(acc)

        af = a_ref[...].astype(F32)
        u = (af * jax.nn.sigmoid(af) * b_ref[...].astype(F32)).astype(CDT)
        hb = hb_ref[...]
        acc[0:W, :] += _tn(da_ref[...], hb)
        acc[W:2 * W, :] += _tn(db_ref[...], hb)
        acc[2 * W:3 * W, :] += _tn(u, dyb_ref[...])

        @pl.when(k == NK - 1)
        def _():
            for s in range(SG):
                for r in range(3):
                    g_ref[s, r * FS:(r + 1) * FS, :] = acc[r * W + s * FS:r * W + (s + 1) * FS, :].astype(CDT)

    row = lambda j, k: (k, 0)
    grp = lambda j, k: (j, k, 0)
    return _call(
        body, "ffn_bwd_w", (NG, NK),
        [pl.BlockSpec((TK, D), row), pl.BlockSpec((TK, D), row)] + [pl.BlockSpec((None, TK, W), grp)] * 4 + [_ANY],
        pl.BlockSpec((SG, 3 * FS, D), lambda j, k: (j, 0, 0)),
        SDS((NS, 3 * FS, D), CDT),
        [pltpu.VMEM((3 * W, D), F32)])(hb, dyb, a, b, da, db, dep)


def _mix_in(x, sp, pma, dep):
    S, D = x.shape
    TM = min(1024, S)

    def body(x_ref, pg_ref, bg_ref, wi_ref, wg_ref, dep_ref, hb_ref, z_ref, g_ref, hb_s):
        @pl.when(pl.program_id(1) == 0)
        def _():
            n, _ = _rms(x_ref[...])
            hb = (n * pg_ref[...]).astype(CDT)
            hb_s[...] = hb
            hb_ref[...] = hb

        hb = hb_s[...]
        z_ref[...] = _nt(hb, wi_ref[...]).astype(CDT)
        g_ref[...] = jax.nn.sigmoid(_nt(hb, wg_ref[...]) + bg_ref[...]).astype(CDT)

    return _call(
        body, "mix_in", (S // TM, NS),
        [_row_once(TM, D), _spspec("mix_pre_g", D, _zero), _spspec("b_gate", MW, lambda i, j: j),
         pl.BlockSpec((None, MW, D), lambda i, j: (j, 0, 0)), pl.BlockSpec((None, MW, D), lambda i, j: (j, 1, 0)), _ANY],
        [_row_once(TM, D), pl.BlockSpec((None, TM, MW), lambda i, j: (j, i, 0)),
         pl.BlockSpec((None, TM, MW), lambda i, j: (j // 2, i, j % 2))],
        [SDS((S, D), CDT), SDS((NS, S, MW), CDT), SDS((4, S, D), CDT)],
        [pltpu.VMEM((TM, D), CDT)])(x, sp, sp, pma, pma, dep)


def _mix_in_bwd_act(dz, dgp, dxr, x, sp, pma):
    S, D = x.shape
    TM = min(1024, S)

    def body(dz_ref, dg_ref, dxr_ref, x_ref, pg_ref, wi_ref, wg_ref, dx_ref, gp_ref, acc):
        i = pl.program_id(0)
        j = pl.program_id(1)

        @pl.when((i == 0) & (j == 0))
        def _():
            gp_ref[...] = jnp.zeros_like(gp_ref)

        @pl.when(j == 0)
        def _():
            acc[...] = jnp.zeros_like(acc)

        acc[...] += _nn(dz_ref[...], wi_ref[...]) + _nn(dg_ref[...], wg_ref[...])

        @pl.when(j == NS - 1)
        def _():
            n, r = _rms(x_ref[...])
            dx, dg = _rms_bwd(n, r, pg_ref[...], acc[...])
            dx_ref[...] = dxr_ref[...] + dx
            gp_ref[0:1, :] += dg

    return _call(
        body, "mix_in_bwd_act", (S // TM, NS),
        [pl.BlockSpec((None, TM, MW), lambda i, j: (j, i, 0)), pl.BlockSpec((None, TM, MW), lambda i, j: (j // 2, i, j % 2)),
         _row_once(TM, D), _row_once(TM, D), _spspec("mix_pre_g", D, _zero),
         pl.BlockSpec((None, MW, D), lambda i, j: (j, 0, 0)), pl.BlockSpec((None, MW, D), lambda i, j: (j, 1, 0))],
        [_row_once(TM, D), pl.BlockSpec((8, D), lambda i, j: (0, 0))],
        [SDS((S, D), F32), SDS((8, D), F32)],
        [pltpu.VMEM((TM, D), F32)])(dz, dgp, dxr, x, sp, pma, pma)


def _mix_in_bwd_w(dz, dgp, hb):
    S, D = hb.shape
    TK = min(512, S)
    NK = S // TK

    def body(dz_ref, dg_ref, hb_ref, g_ref, bg_ref, acc):
        k = pl.program_id(1)

        @pl.when(k == 0)
        def _():
            acc[...] = jnp.zeros_like(acc)
            bg_ref[...] = jnp.zeros_like(bg_ref)

        hb = hb_ref[...]
        dg = dg_ref[...]
        acc[0:MW, :] += _tn(dz_ref[...], hb)
        acc[MW:2 * MW, :] += _tn(dg, hb)
        bg_ref[0:1, :] += jnp.sum(dg.astype(F32), axis=0, keepdims=True)

        @pl.when(k == NK - 1)
        def _():
            g_ref[...] = acc[...].astype(CDT)

    return _call(
        body, "mix_in_bwd_w", (NS, NK),
        [pl.BlockSpec((None, TK, MW), lambda j, k: (j, k, 0)), pl.BlockSpec((None, TK, MW), lambda j, k: (j // 2, k, j % 2)),
         pl.BlockSpec((TK, D), lambda j, k: (k, 0))],
        [pl.BlockSpec((None, 2 * MW, D), lambda j, k: (j, 0, 0)), pl.BlockSpec((None, 8, MW), lambda j, k: (j, 0, 0))],
        [SDS((NS, 2 * MW, D), CDT), SDS((NS, 8, MW), F32)],
        [pltpu.VMEM((2 * MW, D), F32)])(dz, dgp, hb)


def _causal_taps(pad_ref, i, ch, halo, k_taps, lanes=slice(None)):
    val = pad_ref[pl.ds(pl.multiple_of(i * ch, 8), ch + halo), lanes]
    base = {}
    out = []
    for k in range(k_taps):
        q, r = divmod(k_taps - 1 - k, 8)
        if r not in base:
            base[r] = pltpu.roll(val, r, 0) if r else val
        out.append((k, base[r][halo - 8 * q:halo - 8 * q + ch, :]))
    return out


def _anti_taps(pad_ref, i, ch, halo, k_taps, lanes=slice(None)):
    val = pad_ref[pl.ds(pl.multiple_of(i * ch, 8), ch + halo), lanes]
    n = ch + halo
    base = {}
    out = []
    for k in range(k_taps):
        q, r = divmod(k_taps - 1 - k, 8)
        if r not in base:
            base[r] = pltpu.roll(val, n - r, 0) if r else val
        out.append((k, base[r][8 * q:8 * q + ch, :]))
    return out


def _conv_geometry(S, k_taps):
    halo = 8 * ((k_taps - 1 + 7) // 8)
    ch = min(256, S)
    return halo, ch, S // ch


def _rows(i, ch):
    return pl.ds(pl.multiple_of(i * ch, ch), ch)


def _mixA_fwd(z, cw, sp):
    S = z.shape[1]
    K = cw.shape[0]
    H, CH, NCH = _conv_geometry(S, K)

    def body(z_ref, w_ref, b_ref, o_ref, pad):
        pad[0:H, :] = jnp.zeros((H, GW), F32)

        def fill(i, c):
            r = _rows(i, CH)
            pad[pl.ds(pl.multiple_of(i * CH + H, 8), CH), :] = z_ref[2, r, :].astype(F32) * z_ref[0, r, :].astype(F32)
            return c

        lax.fori_loop(0, NCH, fill, 0)

        def conv(i, c):
            r = _rows(i, CH)
            acc = jnp.zeros((CH, GW), F32)
            for k, sh in _causal_taps(pad, i, CH, H, K):
                acc = acc + w_ref[k:k + 1, :] * sh
            o_ref[r, :] = (z_ref[1, r, :].astype(F32) * (acc + b_ref[...])).astype(CDT)
            return c

        lax.fori_loop(0, NCH, conv, 0)

    return _call(
        body, "mixA_fwd", (MW // GW,),
        [pl.BlockSpec((3, S, GW), lambda c: (0, 0, c)), pl.BlockSpec((K, GW), lambda c: (0, c)),
         _spspec("conv_a_b", GW, lambda c: c)],
        pl.BlockSpec((S, GW), lambda c: (0, c)), SDS((S, MW), CDT),
        [pltpu.VMEM((H + S, GW), F32)])(z, cw, sp)


def _mixA_bwd(z, dm, cw, sp):
    S = z.shape[1]
    K = cw.shape[0]
    H, CH, NCH = _conv_geometry(S, K)

    def body(z_ref, dm_ref, w_ref, b_ref, dz_ref, dw_ref, db_ref, pad, dpad, dw_s):
        pad[0:H, :] = jnp.zeros((H, GW), F32)
        dpad[pl.ds(S, H), :] = jnp.zeros((H, GW), F32)
        dw_s[...] = jnp.zeros_like(dw_s)
        db_ref[...] = jnp.zeros_like(db_ref)

        def fill(i, c):
            r = _rows(i, CH)
            pad[pl.ds(pl.multiple_of(i * CH + H, 8), CH), :] = z_ref[2, r, :].astype(F32) * z_ref[0, r, :].astype(F32)
            return c

        lax.fori_loop(0, NCH, fill, 0)

        def p1(i, c):
            r = _rows(i, CH)
            taps = _causal_taps(pad, i, CH, H, K)
            acc = jnp.zeros((CH, GW), F32)
            for k, sh in taps:
                acc = acc + w_ref[k:k + 1, :] * sh
            dmf = dm_ref[r, :].astype(F32)
            dz_ref[1, r, :] = (dmf * (acc + b_ref[...])).astype(CDT)
            dc = dmf * z_ref[1, r, :].astype(F32)
            dpad[r, :] = dc
            for k, sh in taps:
                dw_s[k:k + 1, :] += jnp.sum(dc * sh, axis=0, keepdims=True)
            db_ref[0:1, :] += jnp.sum(dc, axis=0, keepdims=True)
            return c

        lax.fori_loop(0, NCH, p1, 0)

        def p2(i, c):
            r = _rows(i, CH)
            dq = jnp.zeros((CH, GW), F32)
            for k, sh in _anti_taps(dpad, i, CH, H, K):
                dq = dq + w_ref[k:k + 1, :] * sh
            dz_ref[0, r, :] = (dq * z_ref[2, r, :].astype(F32)).astype(CDT)
            dz_ref[2, r, :] = (dq * z_ref[0, r, :].astype(F32)).astype(CDT)
            return c

        lax.fori_loop(0, NCH, p2, 0)
        dw_ref[...] = dw_s[0:K, :]

    return _call(
        body, "mixA_bwd", (MW // GW,),
        [pl.BlockSpec((3, S, GW), lambda c: (0, 0, c)), pl.BlockSpec((None, S, GW), lambda c: (0, 0, c)),
         pl.BlockSpec((K, GW), lambda c: (0, c)), _spspec("conv_a_b", GW, lambda c: c)],
        [pl.BlockSpec((3, S, GW), lambda c: (0, 0, c)), pl.BlockSpec((K, GW), lambda c: (0, c)),
         pl.BlockSpec((8, GW), lambda c: (0, c))],
        [SDS((NS, S, MW), CDT), SDS((K, MW), F32), SDS((8, MW), F32)],
        [pltpu.VMEM((H + S, GW), F32), pltpu.VMEM((S + H, GW), F32), pltpu.VMEM((8 * ((K + 7) // 8), GW), F32)])(z, dm, cw, sp)


def _mixD_conv_fwd(z, cw, sp):
    S = z.shape[1]
    K = cw.shape[0]
    H, CH, NCH = _conv_geometry(S, K)

    def body(z_ref, w_ref, b_ref, o_ref, pad):
        pad[0:H, :] = jnp.zeros((H, GW), F32)

        def fill(i, c):
            r = _rows(i, CH)
            pad[pl.ds(pl.multiple_of(i * CH + H, 8), CH), :] = (
                z_ref[0, r, :].astype(F32) * jax.nn.sigmoid(z_ref[1, r, :].astype(F32)))
            return c

        lax.fori_loop(0, NCH, fill, 0)

        def conv(i, c):
            acc = jnp.zeros((CH, GW), F32)
            for k, sh in _causal_taps(pad, i, CH, H, K):
                acc = acc + w_ref[k:k + 1, :] * sh
            o_ref[_rows(i, CH), :] = (acc + b_ref[...]).astype(CDT)
            return c

        lax.fori_loop(0, NCH, conv, 0)

    return _call(
        body, "mixD_conv_fwd", (MW // GW,),
        [pl.BlockSpec((2, S, GW), lambda c: (3, 0, c)), pl.BlockSpec((K, GW), lambda c: (0, c)),
         _spspec("conv_d_b", GW, lambda c: c)],
        pl.BlockSpec((S, GW), lambda c: (0, c)), SDS((S, MW), CDT),
        [pltpu.VMEM((H + S, GW), F32)])(z, cw, sp)


def _mixD_conv_bwd(z, dy, cw, dz):
    S = z.shape[1]
    K = cw.shape[0]
    H, CH, NCH = _conv_geometry(S, K)

    def body(z_ref, dy_ref, w_ref, dzin_ref, dz_ref, dw_ref, db_ref, pad, dpad, dw_s):
        pad[0:H, :] = jnp.zeros((H, GW), F32)
        dpad[pl.ds(S, H), :] = jnp.zeros((H, GW), F32)
        dw_s[...] = jnp.zeros_like(dw_s)
        db_ref[...] = jnp.zeros_like(db_ref)

        def fill(i, c):
            r = _rows(i, CH)
            pad[pl.ds(pl.multiple_of(i * CH + H, 8), CH), :] = (
                z_ref[0, r, :].astype(F32) * jax.nn.sigmoid(z_ref[1, r, :].astype(F32)))
            dpad[r, :] = dy_ref[r, :].astype(F32)
            return c

        lax.fori_loop(0, NCH, fill, 0)

        def p1(i, c):
            dyf = dy_ref[_rows(i, CH), :].astype(F32)
            for k, sh in _causal_taps(pad, i, CH, H, K):
                dw_s[k:k + 1, :] += jnp.sum(dyf * sh, axis=0, keepdims=True)
            db_ref[0:1, :] += jnp.sum(dyf, axis=0, keepdims=True)
            return c

        lax.fori_loop(0, NCH, p1, 0)

        def p2(i, c):
            r = _rows(i, CH)
            dq = jnp.zeros((CH, GW), F32)
            for k, sh in _anti_taps(dpad, i, CH, H, K):
                dq = dq + w_ref[k:k + 1, :] * sh
            a = z_ref[0, r, :].astype(F32)
            sg = jax.nn.sigmoid(z_ref[1, r, :].astype(F32))
            dz_ref[0, r, :] = (dq * sg).astype(CDT)
            dz_ref[1, r, :] = (dq * a * sg * (1.0 - sg)).astype(CDT)
            return c

        lax.fori_loop(0, NCH, p2, 0)
        dw_ref[...] = dw_s[0:K, :]

    return _call(
        body, "mixD_conv_bwd", (MW // GW,),
        [pl.BlockSpec((2, S, GW), lambda c: (3, 0, c)), pl.BlockSpec((S, GW), lambda c: (0, c)),
         pl.BlockSpec((K, GW), lambda c: (0, c)), _ANY],
        [pl.BlockSpec((2, S, GW), lambda c: (3, 0, c)), pl.BlockSpec((K, GW), lambda c: (0, c)),
         pl.BlockSpec((8, GW), lambda c: (0, c))],
        [SDS((NS, S, MW), CDT), SDS((K, MW), F32), SDS((8, MW), F32)],
        [pltpu.VMEM((H + S, GW), F32), pltpu.VMEM((S + H, GW), F32), pltpu.VMEM((8 * ((K + 7) // 8), GW), F32)],
        aliases={3: 0})(z, dy, cw, dz)


def _mixD_ln_bwd(dm, yd, sp):
    S = yd.shape[0]
    TM = min(512, S)

    def body(dm_ref, y_ref, lg_ref, lb_ref, dy_ref, gp_ref):
        @pl.when(pl.program_id(0) == 0)
        def _():
            gp_ref[...] = jnp.zeros_like(gp_ref)

        xh, rs = _ln(y_ref[...].astype(F32))
        _, dsl = _silu_parts(xh * lg_ref[...] + lb_ref[...])
        dl = dm_ref[...].astype(F32) * dsl
        gp_ref[0:1, :] += jnp.sum(dl * xh, axis=0, keepdims=True)
        gp_ref[1:2, :] += jnp.sum(dl, axis=0, keepdims=True)
        dy_ref[...] = _ln_bwd(xh, rs, dl * lg_ref[...]).astype(CDT)

    row = lambda i: (i, 0)
    return _call(
        body, "mixD_ln_bwd", (S // TM,),
        [pl.BlockSpec((None, TM, MW), lambda i: (3, i, 0)), pl.BlockSpec((TM, MW), row), _spspec("conv_d_ln_g", MW, _zero),
         _spspec("conv_d_ln_b", MW, _zero)],
        [pl.BlockSpec((TM, MW), row), pl.BlockSpec((8, MW), lambda i: (0, 0))],
        [SDS((S, MW), CDT), SDS((8, MW), F32)])(dm, yd, sp, sp)


def _box_causal(val, g):
    s = val
    for d in range(g + 1):
        s = s + pltpu.roll(s, 1 << d, 0)
    return s


def _box_anti(val, g):
    n = val.shape[0]
    s = val
    for d in range(g + 1):
        s = s + pltpu.roll(s, n - (1 << d), 0)
    return s


def _pool_count(i, ch, win):
    t = lax.broadcasted_iota(jnp.int32, (ch, GW), 0) + (i * ch + 1)
    return jnp.minimum(t, win).astype(F32)


def _mixB_fwd(z, wp, sp):
    S = z.shape[1]
    H, CH = 16, min(256, S)
    NCH = S // CH
    assert POOL_WINDOWS == tuple(2 << g for g in range(4))

    def body(p_ref, wp_ref, sc_ref, o_ref, pad):
        pad[0:H, :] = jnp.zeros((H, MW), F32)

        def fill(i, c):
            pad[pl.ds(pl.multiple_of(i * CH + H, 8), CH), :] = p_ref[_rows(i, CH), :].astype(F32)
            return c

        lax.fori_loop(0, NCH, fill, 0)

        def step(i, c):
            r = _rows(i, CH)
            for g in range(4):
                gs = slice(g * GW, (g + 1) * GW)
                val = pad[pl.ds(pl.multiple_of(i * CH, 8), CH + H), gs]
                pooled = _box_causal(val, g)[H:, :] / _pool_count(i, CH, POOL_WINDOWS[g]) - val[H:, :]
                mixed = _nn(pooled.astype(CDT), wp_ref[g].astype(CDT))
                o_ref[r, gs] = (mixed * sc_ref[:, gs]).astype(CDT)
            return c

        lax.fori_loop(0, NCH, step, 0)

    return _call(
        body, "mixB_fwd", (1,),
        [pl.BlockSpec((None, S, MW), lambda i: (3, 0, 0)), pl.BlockSpec((4, GW, GW), lambda i: (0, 0, 0)),
         _spspec("pool_scale", MW, _zero)],
        pl.BlockSpec((S, MW), lambda i: (0, 0)), SDS((S, MW), CDT),
        [pltpu.VMEM((H + S, MW), F32)])(z, wp, sp)


def _mixB_bwd(z, dm, wp, sp, dz):
    S = z.shape[1]
    H, CH = 16, min(256, S)
    NCH = S // CH

    def body(p_ref, dm_ref, wp_ref, sc_ref, dzin_ref, dz_ref, dwp_ref, dsc_ref, pad, rpad):
        pad[0:H, :] = jnp.zeros((H, MW), F32)
        rpad[pl.ds(S, H), :] = jnp.zeros((H, MW), F32)
        dwp_ref[...] = jnp.zeros_like(dwp_ref)
        dsc_ref[...] = jnp.zeros_like(dsc_ref)

        def fill(i, c):
            pad[pl.ds(pl.multiple_of(i * CH + H, 8), CH), :] = p_ref[_rows(i, CH), :].astype(F32)
            return c

        lax.fori_loop(0, NCH, fill, 0)

        def p1(i, c):
            r = _rows(i, CH)
            for g in range(4):
                gs = slice(g * GW, (g + 1) * GW)
                cnt = _pool_count(i, CH, POOL_WINDOWS[g])
                val = pad[pl.ds(pl.multiple_of(i * CH, 8), CH + H), gs]
                pooled = (_box_causal(val, g)[H:, :] / cnt - val[H:, :]).astype(CDT)
                w = wp_ref[g].astype(CDT)
                mixed = _nn(pooled, w)
                dmf = dm_ref[r, gs].astype(F32)
                dsc_ref[0:1, gs] += jnp.sum(dmf * mixed, axis=0, keepdims=True)
                dmx = (dmf * sc_ref[:, gs]).astype(CDT)
                dwp_ref[g] += _tn(pooled, dmx)
                rpad[r, gs] = _nt(dmx, w) / cnt
            return c

        lax.fori_loop(0, NCH, p1, 0)

        def p2(i, c):
            r = _rows(i, CH)
            for g in range(4):
                gs = slice(g * GW, (g + 1) * GW)
                val = rpad[pl.ds(pl.multiple_of(i * CH, 8), CH + H), gs]
                dp = _box_anti(val, g)[:CH, :] - val[:CH, :] * _pool_count(i, CH, POOL_WINDOWS[g])
                dz_ref[r, gs] = dp.astype(CDT)
            return c

        lax.fori_loop(0, NCH, p2, 0)

    return _call(
        body, "mixB_bwd", (1,),
        [pl.BlockSpec((None, S, MW), lambda i: (3, 0, 0)), pl.BlockSpec((None, S, MW), lambda i: (1, 0, 0)),
         pl.BlockSpec((4, GW, GW), lambda i: (0, 0, 0)), _spspec("pool_scale", MW, _zero), _ANY],
        [pl.BlockSpec((None, S, MW), lambda i: (3, 0, 0)), pl.BlockSpec((4, GW, GW), lambda i: (0, 0, 0)),
         pl.BlockSpec((8, MW), lambda i: (0, 0))],
        [SDS((NS, S, MW), CDT), SDS((4, GW, GW), F32), SDS((8, MW), F32)],
        [pltpu.VMEM((H + S, MW), F32), pltpu.VMEM((S + H, MW), F32)], aliases={4: 0})(z, dm, wp, sp, dz)


def _sgu_mask():
    ci = lax.broadcasted_iota(jnp.int32, (GW, GW), 0) // CHUNK
    cj = lax.broadcasted_iota(jnp.int32, (GW, GW), 1) // CHUNK
    return cj <= ci


def _mixC_fwd(z, ws, bsc, sp):
    S = z.shape[1]
    RB = min(512, S)

    def body(z_ref, lg_ref, lb_ref, ws_ref, bs_ref, o_ref):
        mask = _sgu_mask()
        gu = _gelu(z_ref[0].astype(F32))
        xh, _ = _ln(_gelu(z_ref[1].astype(F32)))
        vn = (xh * lg_ref[...] + lb_ref[...]).astype(CDT)
        for g in range(4):
            gs = slice(g * GW, (g + 1) * GW)
            wm = jnp.where(mask, ws_ref[g], 0.0).astype(CDT)
            for nb in range(RB // GW):
                rs = slice(nb * GW, (nb + 1) * GW)
                mixed = _nn(wm, vn[rs, gs]) + bs_ref[g]
                o_ref[rs, gs] = (gu[rs, gs] * mixed).astype(CDT)

    return _call(
        body, "mixC_fwd", (S // RB,),
        [pl.BlockSpec((2, RB, MW), lambda i: (2, i, 0)), _spspec("sgu_ln_g", MW, _zero), _spspec("sgu_ln_b", MW, _zero),
         pl.BlockSpec((4, GW, GW), lambda i: (0, 0, 0)), pl.BlockSpec((4, GW, 1), lambda i: (0, 0, 0))],
        pl.BlockSpec((RB, MW), lambda i: (i, 0)), SDS((S, MW), CDT))(z, sp, sp, ws, bsc)


def _mixC_bwd(z, dm, ws, bsc, sp, dz):
    S = z.shape[1]
    RB = min(512, S)
    NR = S // RB

    def body(z_ref, dm_ref, lg_ref, lb_ref, ws_ref, bs_ref, dzin_ref, dz_ref, dws_ref, dbs_ref, gp_ref, dvn_s):
        i = pl.program_id(0)

        @pl.when(i == 0)
        def _():
            dws_ref[...] = jnp.zeros_like(dws_ref)
            dbs_ref[...] = jnp.zeros_like(dbs_ref)
            gp_ref[...] = jnp.zeros_like(gp_ref)

        mask = _sgu_mask()
        gu, dgu = _gelu_parts(z_ref[0].astype(F32))
        gv, dgv = _gelu_parts(z_ref[1].astype(F32))
        xh, rs_ = _ln(gv)
        vn = (xh * lg_ref[...] + lb_ref[...]).astype(CDT)
        dmf = dm_ref[...].astype(F32)
        for g in range(4):
            gs = slice(g * GW, (g + 1) * GW)
            wm = jnp.where(mask, ws_ref[g], 0.0).astype(CDT)
            for nb in range(RB // GW):
                rs = slice(nb * GW, (nb + 1) * GW)
                vb = vn[rs, gs]
                mixed = _nn(wm, vb) + bs_ref[g]
                dz_ref[0, rs, gs] = (dmf[rs, gs] * mixed * dgu[rs, gs]).astype(CDT)
                dmx = dmf[rs, gs] * gu[rs, gs]
                dbs_ref[g] += dmx
                dmxc = dmx.astype(CDT)
                dws_ref[g] += _nt(dmxc, vb)
                dvn_s[rs, gs] = _tn(wm, dmxc)
        dvn = dvn_s[...]
        gp_ref[0:1, :] += jnp.sum(dvn * xh, axis=0, keepdims=True)
        gp_ref[1:2, :] += jnp.sum(dvn, axis=0, keepdims=True)
        dz_ref[1] = (_ln_bwd(xh, rs_, dvn * lg_ref[...]) * dgv).astype(CDT)

        @pl.when(i == NR - 1)
        def _():
            for g in range(4):
                dws_ref[g] = jnp.where(mask, dws_ref[g], 0.0)
                dbs_ref[g] = jnp.broadcast_to(jnp.sum(dbs_ref[g], axis=1, keepdims=True), (GW, GW))

    full3 = lambda i: (0, 0, 0)
    return _call(
        body, "mixC_bwd", (NR,),
        [pl.BlockSpec((2, RB, MW), lambda i: (2, i, 0)), pl.BlockSpec((None, RB, MW), lambda i: (2, i, 0)),
         _spspec("sgu_ln_g", MW, _zero), _spspec("sgu_ln_b", MW, _zero),
         pl.BlockSpec((4, GW, GW), full3), pl.BlockSpec((4, GW, 1), full3), _ANY],
        [pl.BlockSpec((2, RB, MW), lambda i: (2, i, 0)), pl.BlockSpec((4, GW, GW), full3), pl.BlockSpec((4, GW, GW), full3),
         pl.BlockSpec((8, MW), lambda i: (0, 0))],
        [SDS((NS, S, MW), CDT), SDS((4, GW, GW), F32), SDS((4, GW, GW), F32), SDS((8, MW), F32)],
        [pltpu.VMEM((RB, MW), F32)], aliases={6: 0})(z, dm, sp, sp, ws, bsc, dz)


def _unpack_wb(wb_ref, wbf):
    for j in range(NS):
        for k in range(4):
            wbf[k, :, j * GW:(j + 1) * GW] = wb_ref[j, k]


def _merge_fwd(ma, mb, mc, yd, g, wb, pwo, x, sp):
    S, D = x.shape
    TM = min(256, S)

    def body(ma_ref, mb_ref, mc_ref, yd_ref, g_ref, wb_ref, wo_ref, x_ref, lg_ref, lb_ref, qg_ref,
             xo_ref, md_ref, yk_ref, mg_ref, mo_ref, wbf):
        @pl.when(pl.program_id(0) == 0)
        def _():
            _unpack_wb(wb_ref, wbf)

        xh, _ = _ln(yd_ref[...].astype(F32))
        sl, _ = _silu_parts(xh * lg_ref[...] + lb_ref[...])
        md = sl.astype(CDT)
        md_ref[...] = md
        merged = jnp.zeros((TM, D), F32)
        for k, m in enumerate((ma_ref[...], mb_ref[...], mc_ref[...], md)):
            yk = _nn(m, wbf[k])
            yk_ref[k] = yk.astype(CDT)
            merged = merged + g_ref[k].astype(F32) * yk
        mgc = merged.astype(CDT)
        mg_ref[...] = mgc
        mo = _nn(mgc, wo_ref[...].reshape(D, D))
        mo_ref[...] = mo.astype(CDT)
        n, _ = _rms(mo)
        xo_ref[...] = x_ref[...] + n * qg_ref[...]

    row = lambda i: (i, 0)
    rowm = pl.BlockSpec((TM, MW), row)
    rowd = pl.BlockSpec((TM, D), row)
    row4 = pl.BlockSpec((4, TM, D), lambda i: (0, i, 0))
    return _call(
        body, "merge_fwd", (S // TM,),
        [rowm, rowm, rowm, rowm, row4, pl.BlockSpec((NS, 4, MW, GW), lambda i: (0, 0, 0, 0)),
         pl.BlockSpec((NS, GW, D), lambda i: (0, 0, 0)), rowd,
         _spspec("conv_d_ln_g", MW, _zero), _spspec("conv_d_ln_b", MW, _zero), _spspec("mix_post_g", D, _zero)],
        [rowd, rowm, row4, rowd, rowd],
        [SDS((S, D), F32), SDS((S, MW), CDT), SDS((4, S, D), CDT), SDS((S, D), CDT), SDS((S, D), CDT)],
        [pltpu.VMEM((4, MW, D), CDT)])(ma, mb, mc, yd, g, wb, pwo, x, sp, sp, sp)


def _merge_bwd_act(dxo, mo, g, yk, wb, pwo, sp, dep):
    S, D = dxo.shape
    TM = min(256, S)

    def body(dxo_ref, mo_ref, g_ref, yk_ref, wb_ref, wo_ref, qg_ref, dep_ref, dmo_ref, dm_ref, dgp_ref, dyk_ref, gp_ref, wbf):
        @pl.when(pl.program_id(0) == 0)
        def _():
            gp_ref[...] = jnp.zeros_like(gp_ref)
            _unpack_wb(wb_ref, wbf)

        n, r = _rms(mo_ref[...].astype(F32))
        dmo, dg = _rms_bwd(n, r, qg_ref[...], dxo_ref[...])
        gp_ref[0:1, :] += dg
        dmoc = dmo.astype(CDT)
        dmo_ref[...] = dmoc
        dmg = _nt(dmoc, wo_ref[...].reshape(D, D))
        for k in range(4):
            gk = g_ref[k].astype(F32)
            dyk = (dmg * gk).astype(CDT)
            dyk_ref[k] = dyk
            dgp_ref[k] = (dmg * yk_ref[k].astype(F32) * gk * (1.0 - gk)).astype(CDT)
            dm_ref[k] = _nt(dyk, wbf[k]).astype(CDT)

    rowd = pl.BlockSpec((TM, D), lambda i: (i, 0))
    row4 = pl.BlockSpec((4, TM, D), lambda i: (0, i, 0))
    return _call(
        body, "merge_bwd_act", (S // TM,),
        [rowd, rowd, row4, row4, pl.BlockSpec((NS, 4, MW, GW), lambda i: (0, 0, 0, 0)),
         pl.BlockSpec((NS, GW, D), lambda i: (0, 0, 0)), _spspec("mix_post_g", D, _zero), _ANY],
        [rowd, pl.BlockSpec((4, TM, MW), lambda i: (0, i, 0)), row4, row4, pl.BlockSpec((8, D), lambda i: (0, 0))],
        [SDS((S, D), CDT), SDS((4, S, MW), CDT), SDS((4, S, D), CDT), SDS((4, S, D), CDT), SDS((8, D), F32)],
        [pltpu.VMEM((4, MW, D), CDT)])(dxo, mo, g, yk, wb, pwo, sp, dep)


def _merge_bwd_w(ma, mb, mc, md, dyk, mg, dmo):
    S, D = dmo.shape
    TK = min(512, S)
    NK = S // TK

    def body(ma_ref, mb_ref, mc_ref, md_ref, dyk_ref, mg_ref, dmo_ref, gwb_ref, gwo_ref, accb, acco):
        k = pl.program_id(0)

        @pl.when(k == 0)
        def _():
            accb[...] = jnp.zeros_like(accb)
            acco[...] = jnp.zeros_like(acco)

        for b, m in enumerate((ma_ref, mb_ref, mc_ref, md_ref)):
            accb[b] += _tn(m[...], dyk_ref[b])
        acco[...] += _tn(mg_ref[...], dmo_ref[...])

        @pl.when(k == NK - 1)
        def _():
            for j in range(NS):
                for b in range(4):
                    gwb_ref[j, b] = accb[b, :, j * GW:(j + 1) * GW].astype(CDT)
                gwo_ref[j] = acco[j * GW:(j + 1) * GW, :].astype(CDT)

    rowm = pl.BlockSpec((TK, MW), lambda k: (k, 0))
    rowd = pl.BlockSpec((TK, D), lambda k: (k, 0))
    return _call(
        body, "merge_bwd_w", (NK,),
        [rowm, rowm, rowm, rowm, pl.BlockSpec((4, TK, D), lambda k: (0, k, 0)), rowd, rowd],
        [pl.BlockSpec((NS, 4, MW, GW), lambda k: (0, 0, 0, 0)), pl.BlockSpec((NS, GW, D), lambda k: (0, 0, 0))],
        [SDS((NS, 4, MW, GW), CDT), SDS((NS, GW, D), CDT)],
        [pltpu.VMEM((4, MW, D), F32), pltpu.VMEM((D, D), F32)])(ma, mb, mc, md, dyk, mg, dmo)


def _xa_kv(mem, sp, pxa):
    M, D = mem.shape

    def body(m_ref, g_ref, wk_ref, wv_ref, mn_ref, k_ref, v_ref):
        n, _ = _rms(m_ref[...])
        mn = (n * g_ref[...]).astype(CDT)
        mn_ref[...] = mn
        k_ref[...] = _nn(mn, wk_ref[...].reshape(D, D)).astype(CDT)
        v_ref[...] = _nn(mn, wv_ref[...].reshape(D, D)).astype(CDT)

    full = pl.BlockSpec((M, D), lambda i: (0, 0))
    return _call(
        body, "xa_kv", (1,),
        [full, _spspec("mem_g", D, _zero), pl.BlockSpec((NS, GW, D), lambda i: (0, 1, 0)),
         pl.BlockSpec((NS, GW, D), lambda i: (0, 2, 0))],
        [full, full, full], [SDS((M, D), CDT)] * 3)(mem, sp, pxa, pxa)


def _softmax(s):
    e = jnp.exp(s - jnp.max(s, axis=-1, keepdims=True))
    return e / jnp.sum(e, axis=-1, keepdims=True)


def _xa_fwd(x, kk, vv, sp, pxa):
    S, D = x.shape
    M = kk.shape[0]
    TM = min(512, S)
    HD = D // XA_HEADS
    scale = HD ** -0.5

    def body(x_ref, k_ref, v_ref, pg_ref, qg_ref, wq_ref, wo_ref, xo_ref, hb_ref, q_ref, o_ref, po_ref):
        n, _ = _rms(x_ref[...])
        hb = (n * pg_ref[...]).astype(CDT)
        hb_ref[...] = hb
        q = _nn(hb, wq_ref[...].reshape(D, D)).astype(CDT)
        q_ref[...] = q
        for h in range(XA_HEADS):
            hs = slice(h * HD, (h + 1) * HD)
            p = _softmax(_nt(q[:, hs], k_ref[:, hs]) * scale)
            o_ref[:, hs] = _nn(p.astype(CDT), v_ref[:, hs]).astype(CDT)
        po = _nn(o_ref[...], wo_ref[...].reshape(D, D))
        po_ref[...] = po.astype(CDT)
        n, _ = _rms(po)
        xo_ref[...] = x_ref[...] + n * qg_ref[...]

    row = pl.BlockSpec((TM, D), lambda i: (i, 0))
    full = pl.BlockSpec((M, D), lambda i: (0, 0))
    return _call(
        body, "xa_fwd", (S // TM,),
        [row, full, full, _spspec("xa_pre_g", D, _zero), _spspec("xa_post_g", D, _zero),
         pl.BlockSpec((NS, GW, D), lambda i: (0, 0, 0)), pl.BlockSpec((NS, GW, D), lambda i: (0, 3, 0))],
        [row] * 5, [SDS((S, D), F32)] + [SDS((S, D), CDT)] * 4)(x, kk, vv, sp, sp, pxa, pxa)


def _xa_bwd_act(dxo, x, po, q, kk, vv, sp, pxa, dep):
    S, D = x.shape
    M = kk.shape[0]
    TM = min(512, S)
    HD = D // XA_HEADS
    scale = HD ** -0.5

    def body(dxo_ref, x_ref, po_ref, q_ref, k_ref, v_ref, pg_ref, qg_ref, wq_ref, wo_ref, dep_ref,
             dx_ref, dpo_ref, dq_ref, dk_ref, dv_ref, gp_ref):
        @pl.when(pl.program_id(0) == 0)
        def _():
            gp_ref[...] = jnp.zeros_like(gp_ref)
            dk_ref[...] = jnp.zeros_like(dk_ref)
            dv_ref[...] = jnp.zeros_like(dv_ref)

        n, r = _rms(po_ref[...].astype(F32))
        dpo, dg = _rms_bwd(n, r, qg_ref[...], dxo_ref[...])
        gp_ref[1:2, :] += dg
        dpoc = dpo.astype(CDT)
        dpo_ref[...] = dpoc
        do = _nt(dpoc, wo_ref[...].reshape(D, D)).astype(CDT)
        for h in range(XA_HEADS):
            hs = slice(h * HD, (h + 1) * HD)
            qh = q_ref[:, hs]
            p = _softmax(_nt(qh, k_ref[:, hs]) * scale)
            pc = p.astype(CDT)
            dv_ref[:, hs] += _tn(pc, do[:, hs])
            dp = _nt(do[:, hs], v_ref[:, hs])
            ds = (p * (dp - jnp.sum(p * dp, axis=-1, keepdims=True)) * scale).astype(CDT)
            dq_ref[:, hs] = _nn(ds, k_ref[:, hs]).astype(CDT)
            dk_ref[:, hs] += _tn(ds, qh)
        dhb = _nt(dq_ref[...], wq_ref[...].reshape(D, D))
        n, r = _rms(x_ref[...])
        dx, dg = _rms_bwd(n, r, pg_ref[...], dhb)
        dx_ref[...] = dxo_ref[...] + dx
        gp_ref[0:1, :] += dg

    row = pl.BlockSpec((TM, D), lambda i: (i, 0))
    full = pl.BlockSpec((M, D), lambda i: (0, 0))
    return _call(
        body, "xa_bwd_act", (S // TM,),
        [row, row, row, row, full, full, _spspec("xa_pre_g", D, _zero), _spspec("xa_post_g", D, _zero),
         pl.BlockSpec((NS, GW, D), lambda i: (0, 0, 0)), pl.BlockSpec((NS, GW, D), lambda i: (0, 3, 0)), _ANY],
        [row, row, row, full, full, pl.BlockSpec((8, D), lambda i: (0, 0))],
        [SDS((S, D), F32), SDS((S, D), CDT), SDS((S, D), CDT), SDS((M, D), F32), SDS((M, D), F32), SDS((8, D), F32)],
    )(dxo, x, po, q, kk, vv, sp, sp, pxa, pxa, dep)


def _xa_bwd_w(hb, dq, o, dpo, mn, dk, dv):
    S, D = hb.shape
    M = mn.shape[0]
    TK = min(512, S)
    NK = S // TK

    def body(hb_ref, dq_ref, o_ref, dpo_ref, mn_ref, dk_ref, dv_ref, g_ref, accq, acco):
        k = pl.program_id(0)

        @pl.when(k == 0)
        def _():
            accq[...] = jnp.zeros_like(accq)
            acco[...] = jnp.zeros_like(acco)

        accq[...] += _tn(hb_ref[...], dq_ref[...])
        acco[...] += _tn(o_ref[...], dpo_ref[...])

        @pl.when(k == NK - 1)
        def _():
            gk = _tn(mn_ref[...], dk_ref[...].astype(CDT))
            gv = _tn(mn_ref[...], dv_ref[...].astype(CDT))
            for j in range(NS):
                rs = slice(j * GW, (j + 1) * GW)
                g_ref[j, 0:GW, :] = accq[rs, :].astype(CDT)
                g_ref[j, GW:2 * GW, :] = gk[rs, :].astype(CDT)
                g_ref[j, 2 * GW:3 * GW, :] = gv[rs, :].astype(CDT)
                g_ref[j, 3 * GW:4 * GW, :] = acco[rs, :].astype(CDT)

    rowb = pl.BlockSpec((TK, D), lambda k: (k, 0))
    full = pl.BlockSpec((M, D), lambda k: (0, 0))
    return _call(
        body, "xa_bwd_w", (NK,),
        [rowb, rowb, rowb, rowb, full, full, full],
        pl.BlockSpec((NS, 4 * GW, D), lambda k: (0, 0, 0)), SDS((NS, 4 * GW, D), CDT),
        [pltpu.VMEM((D, D), F32), pltpu.VMEM((D, D), F32)])(hb, dq, o, dpo, mn, dk, dv)


def _xa_kv_bwd(mem, dk, dv, sp, pxa):
    M, D = mem.shape

    def body(m_ref, dk_ref, dv_ref, wk_ref, wv_ref, gp_ref):
        dmn = _nt(dk_ref[...].astype(CDT), wk_ref[...].reshape(D, D)) + _nt(dv_ref[...].astype(CDT), wv_ref[...].reshape(D, D))
        n, _ = _rms(m_ref[...])
        gp_ref[...] = jnp.zeros_like(gp_ref)
        gp_ref[0:1, :] = jnp.sum(dmn * n, axis=0, keepdims=True)

    full = pl.BlockSpec((M, D), lambda i: (0, 0))
    return _call(
        body, "xa_kv_bwd", (1,),
        [full, full, full, pl.BlockSpec((NS, GW, D), lambda i: (0, 1, 0)), pl.BlockSpec((NS, GW, D), lambda i: (0, 2, 0))],
        pl.BlockSpec((8, D), lambda i: (0, 0)), SDS((8, D), F32))(mem, dk, dv, pxa, pxa)


def _loss_head(y, t):
    S, D = y.shape
    TM = min(512, S)

    def body(y_ref, t_ref, dy_ref, l_ref):
        @pl.when(pl.program_id(0) == 0)
        def _():
            l_ref[...] = jnp.zeros_like(l_ref)

        e = y_ref[...] - t_ref[...]
        dy_ref[...] = e * (1.0 / D)
        l_ref[...] += 0.5 * jnp.sum(jnp.mean(e * e, axis=-1, keepdims=True), axis=0, keepdims=True)

    row = pl.BlockSpec((TM, D), lambda i: (i, 0))
    return _call(body, "loss_head", (S // TM,), [row, row], [row, pl.BlockSpec((8, 128), lambda i: (0, 0))],
                 [SDS((S, D), F32), SDS((8, 128), F32)])(y, t)


def _row_tile(rows, cols, limit=1 << 18, step=8):
    if rows * cols <= limit or rows % step:
        return rows
    best = step
    for t in range(step, rows + 1, step):
        if rows % t == 0 and t * cols <= limit:
            best = t
    return best


def _adamw(w, g, m, v):
    shape = w.shape
    C = shape[-1]
    R = w.size // C
    TR = _row_tile(R, C)
    c1 = 1.0 - ADAM_B1 ** ADAM_STEP
    c2 = 1.0 - ADAM_B2 ** ADAM_STEP

    def body(w_ref, g_ref, m_ref, v_ref, d_ref, nm_ref, nv_ref):
        gg = g_ref[...]
        nm = ADAM_B1 * m_ref[...] + (1.0 - ADAM_B1) * gg
        nv = ADAM_B2 * v_ref[...] + (1.0 - ADAM_B2) * (gg * gg)
        nm_ref[...] = nm
        nv_ref[...] = nv
        d_ref[...] = -ADAM_LR * ((nm / c1) / (jnp.sqrt(nv / c2) + ADAM_EPS) + ADAM_WD * w_ref[...])

    blk = pl.BlockSpec((TR, C), lambda i: (i, 0))
    outs = _call(body, "adamw", (R // TR,), [blk] * 4, [blk] * 3, [SDS((R, C), F32)] * 3)(
        w.reshape(R, C), g.reshape(R, C), m.reshape(R, C), v.reshape(R, C))
    return tuple(o.reshape(shape) for o in outs)


def _adamw_block(w, gs, m, v, gblock):
    L, R, C = w.shape
    c1 = 1.0 - ADAM_B1 ** ADAM_STEP
    c2 = 1.0 - ADAM_B2 ** ADAM_STEP

    def body(w_ref, g_ref, m_ref, v_ref, go_ref, d_ref, nm_ref, nv_ref):
        gg = g_ref[...]
        go_ref[...] = gg
        nm = ADAM_B1 * m_ref[...] + (1.0 - ADAM_B1) * gg
        nv = ADAM_B2 * v_ref[...] + (1.0 - ADAM_B2) * (gg * gg)
        nm_ref[...] = nm
        nv_ref[...] = nv
        d_ref[...] = -ADAM_LR * ((nm / c1) / (jnp.sqrt(nv / c2) + ADAM_EPS) + ADAM_WD * w_ref[...])

    blk = pl.BlockSpec((None, R, C), lambda l: (l, 0, 0))
    return _call(body, "adamw_block", (L,), [blk, pl.BlockSpec((None, R, C), lambda l: (l, gblock, 0)), blk, blk],
                 [blk] * 4, [SDS((L, R, C), F32)] * 4)(w, gs, m, v)


def _slot_sum_into(stacked, r, l):
    _, R, C = r.shape
    TR = _row_tile(R, C * NS, limit=1 << 21, step=16)

    def body(r_ref, s_ref, o_ref):
        acc = r_ref[0].astype(F32)
        for j in range(1, NS):
            acc = acc + r_ref[j].astype(F32)
        o_ref[...] = acc

    return pl.pallas_call(
        body, name="slot_sum_into", grid=(R // TR,),
        in_specs=[pl.BlockSpec((NS, TR, C), lambda i: (0, i, 0)), _ANY],
        out_specs=pl.BlockSpec((None, TR, C), lambda i: (l, i, 0)), out_shape=SDS(stacked.shape, F32),
        input_output_aliases={1: 0},
        compiler_params=pltpu.CompilerParams(dimension_semantics=("arbitrary",), vmem_limit_bytes=VMEM_LIMIT))(r, stacked)


def _exchange(arrs, scatter, name):
    n = len(arrs)
    np_ = NS - 1

    def body(*refs):
        ins, outs = refs[:n], refs[n:2 * n]
        send_sems, recv_sems, loc_sems = refs[2 * n:]
        x, y, c = lax.axis_index("x"), lax.axis_index("y"), lax.axis_index("c")
        me = 4 * x + 2 * y + c
        peers = []
        for f in range(1, NS):
            px = 1 - x if f & 4 else x
            py = 1 - y if f & 2 else y
            pc = 1 - c if f & 1 else c
            peers.append(((px, py, pc), 4 * px + 2 * py + pc))

        def src(a, pid):
            return ins[a].at[pid] if scatter else ins[a]

        local = [pltpu.make_async_copy(src(a, me), outs[a].at[me], loc_sems.at[a]) for a in range(n)]
        for cp in local:
            cp.start()
        sends = []
        for a in range(n):
            for f, (dev, pid) in enumerate(peers):
                sends.append(pltpu.make_async_remote_copy(
                    src_ref=src(a, pid), dst_ref=outs[a].at[me], send_sem=send_sems.at[a * np_ + f],
                    recv_sem=recv_sems.at[a * np_ + f], device_id=dev, device_id_type=pl.DeviceIdType.MESH))
        for cp in sends:
            cp.start()
        for a in range(n):
            for f, (dev, pid) in enumerate(peers):
                pltpu.make_async_remote_copy(
                    src_ref=src(a, pid), dst_ref=outs[a].at[pid], send_sem=send_sems.at[a * np_ + f],
                    recv_sem=recv_sems.at[a * np_ + f], device_id=dev, device_id_type=pl.DeviceIdType.MESH).wait_recv()
        for cp in sends:
            cp.wait_send()
        for cp in local:
            cp.wait()

    out_shape = [SDS(a.shape if scatter else (NS,) + a.shape, a.dtype) for a in arrs]
    anyspec = pl.BlockSpec(memory_space=pl.ANY)
    outs = pl.pallas_call(
        body, name=name, in_specs=[anyspec] * n, out_specs=[anyspec] * n, out_shape=out_shape,
        scratch_shapes=[pltpu.SemaphoreType.DMA((n * np_,)), pltpu.SemaphoreType.DMA((n * np_,)),
                        pltpu.SemaphoreType.DMA((n,))],
        compiler_params=pltpu.CompilerParams(has_side_effects=True))(*arrs)
    return list(outs)


def _peers():
    x, y, c = lax.axis_index("x"), lax.axis_index("y"), lax.axis_index("c")
    out = []
    for f in range(1, NS):
        px = 1 - x if f & 4 else x
        py = 1 - y if f & 2 else y
        pc = 1 - c if f & 1 else c
        out.append(((px, py, pc), 4 * px + 2 * py + pc))
    return 4 * x + 2 * y + c, out


def _exchange_copies(ins, lands, scatter, send_sems, recv_sems, loc_sems):
    me, peers = _peers()
    np_ = NS - 1

    def src(a, pid):
        return ins[a].at[pid] if scatter[a] else ins[a]

    def rcopy(a, f, dev, land_slot):
        return pltpu.make_async_remote_copy(
            src_ref=src(a, peers[f][1]), dst_ref=lands[a].at[land_slot], send_sem=send_sems.at[a * np_ + f],
            recv_sem=recv_sems.at[a * np_ + f], device_id=dev, device_id_type=pl.DeviceIdType.MESH)

    local = [pltpu.make_async_copy(src(a, me), lands[a].at[me], loc_sems.at[a]) for a in range(len(ins))]
    sends = [rcopy(a, f, dev, me) for a in range(len(ins)) for f, (dev, _) in enumerate(peers)]
    arrivals = [rcopy(a, f, dev, pid) for a in range(len(ins)) for f, (dev, pid) in enumerate(peers)]
    return local, sends, arrivals


_HBM = pl.BlockSpec(memory_space=pltpu.HBM)
_SEM = pl.BlockSpec(memory_space=pltpu.SEMAPHORE)
_ANY = pl.BlockSpec(memory_space=pl.ANY)


def _exchange_start(arrs, scatter, after, name):
    n = len(arrs)
    np_ = NS - 1
    lands = [lax.empty(a.shape if sc else (NS,) + a.shape, a.dtype) for a, sc in zip(arrs, scatter)]

    def body(*refs):
        ins, lnd = refs[:n], refs[n:2 * n]
        send_sems, recv_sems, loc_sems = refs[2 * n + 1:2 * n + 4]
        token = refs[-1]
        local, sends, _ = _exchange_copies(ins, lnd, scatter, send_sems, recv_sems, loc_sems)
        for cp in local + sends:
            cp.start()
        token[...] = jnp.zeros_like(token)

    hbm = lambda a: pltpu.HBM(a.shape, a.dtype)
    outs = pl.pallas_call(
        body, name=name,
        out_shape=(pltpu.SemaphoreType.DMA((n * np_,)), pltpu.SemaphoreType.DMA((n * np_,)), pltpu.SemaphoreType.DMA((n,)),
                   *[hbm(a) for a in arrs], *[hbm(a) for a in lands], SDS((8, 128), F32)),
        in_specs=[_HBM] * (2 * n) + [_ANY],
        out_specs=(_SEM, _SEM, _SEM, *([_HBM] * (2 * n)), pl.BlockSpec(memory_space=pltpu.VMEM)),
        input_output_aliases={i: 3 + i for i in range(2 * n)},
        compiler_params=pltpu.CompilerParams(has_side_effects=pltpu.SideEffectType.DATAFLOW_SIDE_EFFECTING),
    )(*[pltpu.with_memory_space_constraint(a, pltpu.HBM) for a in list(arrs) + lands], after)
    return {"sems": outs[:3], "ins": outs[3:3 + n], "lands": outs[3 + n:3 + 2 * n], "token": outs[-1], "scatter": scatter}


def _exchange_wait(h, after, name):
    n = len(h["ins"])
    scatter = h["scatter"]
    after = list(after) if isinstance(after, (list, tuple)) else [after]

    def body(*refs):
        ins, lnd = refs[:n], refs[n:2 * n]
        send_sems, recv_sems, loc_sems = refs[2 * n:2 * n + 3]
        local, sends, arrivals = _exchange_copies(ins, lnd, scatter, send_sems, recv_sems, loc_sems)
        for cp in sends:
            cp.wait_send()
        for cp in arrivals:
            cp.wait_recv()
        for cp in local:
            cp.wait()

    hbm = lambda a: pltpu.HBM(a.shape, a.dtype)
    outs = pl.pallas_call(
        body, name=name,
        out_shape=tuple(hbm(a) for a in list(h["ins"]) + list(h["lands"])),
        in_specs=[_HBM] * (2 * n) + [_SEM] * 3 + [_ANY] * len(after),
        out_specs=tuple([_HBM] * (2 * n)),
        input_output_aliases={i: i for i in range(2 * n)},
        compiler_params=pltpu.CompilerParams(has_side_effects=pltpu.SideEffectType.DATAFLOW_SIDE_EFFECTING),
    )(*h["ins"], *h["lands"], *h["sems"], *after)
    return list(outs[n:])


def _hgather_copies(ins, lands, send_a, recv_a, send_b, recv_b, loc_sems):
    x, y, c = lax.axis_index("x"), lax.axis_index("y"), lax.axis_index("c")
    me = 4 * x + 2 * y + c
    sib = (x, y, 1 - c)
    chips = [(1 - x, y), (x, 1 - y), (1 - x, 1 - y)]
    slot = lambda px, py, pc: 4 * px + 2 * py + pc

    def rcopy(src, dst, ssem, rsem, dev):
        return pltpu.make_async_remote_copy(src_ref=src, dst_ref=dst, send_sem=ssem, recv_sem=rsem, device_id=dev,
                                            device_id_type=pl.DeviceIdType.MESH)

    local, s1, a1, s2, a2 = [], [], [], [], []
    for a in range(len(ins)):
        first = [(sib, slot(x, y, 1 - c))] + [((px, py, c), slot(px, py, c)) for px, py in chips]
        for k, (dev, origin) in enumerate(first if send_a is not None else ()):
            s1.append(rcopy(ins[a], lands[a].at[me], send_a.at[4 * a + k], recv_a.at[4 * a + k], dev))
            a1.append(rcopy(ins[a], lands[a].at[origin], send_a.at[4 * a + k], recv_a.at[4 * a + k], dev))
        if send_a is not None:
            local.append(pltpu.make_async_copy(ins[a], lands[a].at[me], loc_sems.at[a]))
        for k, (px, py) in enumerate(chips if send_b is not None else ()):
            mine, theirs = lands[a].at[slot(px, py, c)], lands[a].at[slot(px, py, 1 - c)]
            s2.append(rcopy(mine, mine, send_b.at[3 * a + k], recv_b.at[3 * a + k], sib))
            a2.append(rcopy(mine, theirs, send_b.at[3 * a + k], recv_b.at[3 * a + k], sib))
    return local, s1, a1, s2, a2


def _hgather_start(arrs, after, name):
    n = len(arrs)
    lands = [lax.empty((NS,) + a.shape, a.dtype) for a in arrs]

    def body(*refs):
        ins, lnd = refs[:n], refs[n:2 * n]
        send_a, recv_a, loc_sems = refs[2 * n + 1:2 * n + 4]
        token = refs[-1]
        local, s1, _, _, _ = _hgather_copies(ins, lnd, send_a, recv_a, None, None, loc_sems)
        for cp in local + s1:
            cp.start()
        token[...] = jnp.zeros_like(token)

    hbm = lambda a: pltpu.HBM(a.shape, a.dtype)
    outs = pl.pallas_call(
        body, name=name,
        out_shape=(pltpu.SemaphoreType.DMA((4 * n,)), pltpu.SemaphoreType.DMA((4 * n,)), pltpu.SemaphoreType.DMA((n,)),
                   *[hbm(a) for a in arrs], *[hbm(a) for a in lands], SDS((8, 128), F32)),
        in_specs=[_HBM] * (2 * n) + [_ANY],
        out_specs=(_SEM, _SEM, _SEM, *([_HBM] * (2 * n)), pl.BlockSpec(memory_space=pltpu.VMEM)),
        input_output_aliases={i: 3 + i for i in range(2 * n)},
        compiler_params=pltpu.CompilerParams(has_side_effects=pltpu.SideEffectType.DATAFLOW_SIDE_EFFECTING),
    )(*[pltpu.with_memory_space_constraint(a, pltpu.HBM) for a in list(arrs) + lands], after)
    return {"sems": outs[:3], "ins": outs[3:3 + n], "lands": outs[3 + n:3 + 2 * n], "token": outs[-1]}


def _hgather_forward(h, after, name):
    n = len(h["ins"])
    after = list(after) if isinstance(after, (list, tuple)) else [after]
    na = len(after)

    def body(*refs):
        ins, lnd = refs[:n], refs[n:2 * n]
        send_a, recv_a, loc_sems = refs[2 * n:2 * n + 3]
        send_b, recv_b = refs[2 * n + 3 + na:2 * n + 5 + na]
        token = refs[-1]
        local, s1, a1, s2, _ = _hgather_copies(ins, lnd, send_a, recv_a, send_b, recv_b, loc_sems)
        for cp in s1:
            cp.wait_send()
        for cp in a1:
            cp.wait_recv()
        for cp in local:
            cp.wait()
        for cp in s2:
            cp.start()
        token[...] = jnp.zeros_like(token)

    hbm = lambda a: pltpu.HBM(a.shape, a.dtype)
    outs = pl.pallas_call(
        body, name=name,
        out_shape=(pltpu.SemaphoreType.DMA((3 * n,)), pltpu.SemaphoreType.DMA((3 * n,)),
                   *[hbm(a) for a in list(h["ins"]) + list(h["lands"])], SDS((8, 128), F32)),
        in_specs=[_HBM] * (2 * n) + [_SEM] * 3 + [_ANY] * na,
        out_specs=(_SEM, _SEM, *([_HBM] * (2 * n)), pl.BlockSpec(memory_space=pltpu.VMEM)),
        input_output_aliases={i: 2 + i for i in range(2 * n)},
        compiler_params=pltpu.CompilerParams(has_side_effects=pltpu.SideEffectType.DATAFLOW_SIDE_EFFECTING),
    )(*h["ins"], *h["lands"], *h["sems"], *after)
    return {"sems": outs[:2], "ins": outs[2:2 + n], "lands": outs[2 + n:2 + 2 * n], "token": outs[-1]}


def _hgather_wait(h, after, name):
    n = len(h["ins"])

    def body(*refs):
        ins, lnd = refs[:n], refs[n:2 * n]
        send_b, recv_b = refs[2 * n:2 * n + 2]
        _, _, _, s2, a2 = _hgather_copies(ins, lnd, None, None, send_b, recv_b, None)
        for cp in s2:
            cp.wait_send()
        for cp in a2:
            cp.wait_recv()

    hbm = lambda a: pltpu.HBM(a.shape, a.dtype)
    outs = pl.pallas_call(
        body, name=name,
        out_shape=tuple(hbm(a) for a in list(h["ins"]) + list(h["lands"])),
        in_specs=[_HBM] * (2 * n) + [_SEM] * 2 + [_ANY],
        out_specs=tuple([_HBM] * (2 * n)),
        input_output_aliases={i: i for i in range(2 * n)},
        compiler_params=pltpu.CompilerParams(has_side_effects=pltpu.SideEffectType.DATAFLOW_SIDE_EFFECTING),
    )(*h["ins"], *h["lands"], *h["sems"], after)
    return list(outs[n:])


_W_NAMES = ("ffn1_pre_g", "ffn1_post_g", "ffn1_w1", "ffn1_w3", "ffn1_w2", "mix_pre_g", "mix_post_g", "w_in", "conv_a_w",
            "conv_a_b", "pool_w", "pool_scale", "sgu_ln_g", "sgu_ln_b", "sgu_ws", "sgu_b", "conv_d_w", "conv_d_b",
            "conv_d_ln_g", "conv_d_ln_b", "w_branch", "w_gate", "b_gate", "w_o", "xa_pre_g", "xa_post_g", "mem_g",
            "xa_wq", "xa_wk", "xa_wv", "xa_wo", "ffn2_pre_g", "ffn2_post_g", "ffn2_w1", "ffn2_w3", "ffn2_w2")
_REP_NAMES = tuple(n for n, _ in _SP_NAMES) + ("pool_w", "sgu_ws", "sgu_b", "conv_a_w", "conv_d_w")


def _t(w):
    return jnp.swapaxes(w, -1, -2)


def _step(x, mem, loss_target, W, M, V):
    L = W["w_in"].shape[0]
    S, D = x.shape[1], x.shape[2]
    x0 = x.reshape(S, D)
    memf = mem.reshape(mem.shape[1], D)
    me = 4 * lax.axis_index("x") + 2 * lax.axis_index("y") + lax.axis_index("c")
    FS = W["ffn1_w2"].shape[1]
    KA, KD = W["conv_a_w"].shape[1], W["conv_d_w"].shape[1]
    CS = W["conv_a_w"].shape[2]

    cat = lambda l, parts: jnp.concatenate([(_t(W[n][l]) if tr else W[n][l]) for n, tr in parts], axis=0).astype(CDT)
    pf1 = [cat(l, (("ffn1_w1", 1), ("ffn1_w3", 1), ("ffn1_w2", 0))) for l in range(L)]
    pf2 = [cat(l, (("ffn2_w1", 1), ("ffn2_w3", 1), ("ffn2_w2", 0))) for l in range(L)]
    pma = [cat(l, (("w_in", 1), ("w_gate", 1))) for l in range(L)]
    pwo = [W["w_o"][l].astype(CDT) for l in range(L)]
    pxa = [cat(l, (("xa_wq", 0), ("xa_wk", 0), ("xa_wv", 0), ("xa_wo", 0))) for l in range(L)]
    wbs = [W["w_branch"][l].astype(CDT) for l in range(L)]
    cws = jnp.concatenate([W["conv_a_w"], W["conv_d_w"]], axis=1).reshape(-1, 128)
    sp_all = jnp.concatenate([W[n] for n, _ in _SP_NAMES], axis=1)
    bsc_all = W["sgu_b"][..., None]

    (cwg,) = _exchange([cws], False, "gather_conv_w")
    cwf = cwg.reshape(NS, L, KA + KD, CS).transpose(1, 2, 0, 3).reshape(L, KA + KD, NS * CS)

    def gather_start(l, after):
        return _hgather_start([pf1[l], pf2[l], pma[l], pwo[l], pxa[l], wbs[l]], after, f"gather_start_{l}")

    def gather_rest(h, after, tag):
        mid = _hgather_forward(h, after, f"gather_forward_{tag}")
        return _hgather_wait(mid, mid["token"], f"gather_wait_{tag}")

    packs = [None] * L
    first = [_hgather_start([pf1[0]], cwg, "gather_start_0a")]
    saved = []
    xc = x0
    for l in range(L):
        if l == 0:
            rest = [p[k] for p in (pf1, pf2, pma, pwo, pxa, wbs) for k in range(L) if not (p is pf1 and k == 0)]
            (gf1,) = gather_rest(first[0], [sp_all, bsc_all] + rest, "0a")
            first.append(_hgather_start([pma[0], pwo[0], wbs[0]], gf1, "gather_start_0b"))
        else:
            gf1, gf2, gma, gwo, gxa, gwb = packs[l]
        sp = sp_all[l:l + 1]
        cwa, cwd = cwf[l, :KA], cwf[l, KA:]
        wp, ws, bsc = W["pool_w"][l], W["sgu_ws"][l], bsc_all[l]
        s = {"x0": xc}
        nxt = gather_start(l + 1, gf1) if 0 < l < L - 1 else None
        xc, s["hb1"], s["a1"], s["b1"], s["y1"] = _ffn_fwd(xc, sp, "ffn1_pre_g", "ffn1_post_g", gf1,
                                                            (first[1] if l == 0 else nxt)["token"] if l == 0 or nxt else sp)
        s["x1"] = xc
        if l == 0:
            gma, gwo, gwb = gather_rest(first[1], xc, "0b")
            first.append(_hgather_start([pxa[0]], gma, "gather_start_0c"))
            first.append(_hgather_start([pf2[0]], first[2]["token"], "gather_start_0d"))
            nxt = gather_start(1, first[3]["token"]) if L > 1 else None
        s["hbm"], s["z"], s["g"] = _mix_in(xc, sp, gma, (nxt or first[3])["token"] if l == 0 else sp)
        s["ma"] = _mixA_fwd(s["z"], cwa, sp)
        s["mb"] = _mixB_fwd(s["z"], wp, sp)
        s["mc"] = _mixC_fwd(s["z"], ws, bsc, sp)
        s["yd"] = _mixD_conv_fwd(s["z"], cwd, sp)
        xc, s["md"], s["yk"], s["mg"], s["mo"] = _merge_fwd(s["ma"], s["mb"], s["mc"], s["yd"], s["g"], gwb, gwo, xc, sp)
        s["x2"] = xc
        if l == 0:
            (gxa,) = gather_rest(first[2], xc, "0c")
        s["mn"], s["k"], s["v"] = _xa_kv(memf, sp, gxa)
        xc, s["hbx"], s["q"], s["o"], s["po"] = _xa_fwd(xc, s["k"], s["v"], sp, gxa)
        s["x3"] = xc
        if l == 0:
            (gf2,) = gather_rest(first[3], xc, "0d")
            packs[0] = (gf1, gf2, gma, gwo, gxa, gwb)
        mid = _hgather_forward(nxt, xc, f"gather_forward_{l + 1}") if nxt and l > 0 else None
        xc, s["hb2"], s["a2"], s["b2"], s["y2"] = _ffn_fwd(xc, sp, "ffn2_pre_g", "ffn2_post_g", gf2,
                                                            mid["token"] if mid else sp)
        saved.append(s)
        if nxt:
            mid = mid or _hgather_forward(nxt, xc, f"gather_forward_{l + 1}")
            packs[l + 1] = _hgather_wait(mid, xc, f"gather_wait_{l + 1}")

    dx, lpart = _loss_head(xc, loss_target.reshape(S, D))
    loss = lax.psum(lpart[0, 0], ("x", "y", "c"))

    rep = {n: [None] * L for n in _REP_NAMES}
    summed = [dict() for _ in range(L)]
    pending = None
    last = []
    for l in reversed(range(L)):
        gf1, gf2, gma, gwo, gxa, gwb = packs[l]
        sp = sp_all[l:l + 1]
        cwa, cwd = cwf[l, :KA], cwf[l, KA:]
        wp, ws, bsc = W["pool_w"][l], W["sgu_ws"][l], bsc_all[l]
        s = saved[l]

        dx, dyb, da, db, gp = _ffn_bwd_act(dx, s["x3"], s["y2"], s["a2"], s["b2"], sp, "ffn2_pre_g", "ffn2_post_g", gf2,
                                           pending[1]["token"] if pending else sp)
        rep["ffn2_pre_g"][l], rep["ffn2_post_g"][l] = gp[0], gp[1]
        d_f2 = _ffn_bwd_w(s["hb2"], dyb, s["a2"], s["b2"], da, db, sp)
        if l == 0:
            last.append(_exchange_start([d_f2], (True,), dx, "scatter_start_0a"))

        dx, dpo, dq, dk, dv, gp = _xa_bwd_act(dx, s["x2"], s["po"], s["q"], s["k"], s["v"], sp, gxa,
                                              last[-1]["token"] if last else sp)
        rep["xa_pre_g"][l], rep["xa_post_g"][l] = gp[0], gp[1]
        d_xa = _xa_bwd_w(s["hbx"], dq, s["o"], dpo, s["mn"], dk, dv)
        rep["mem_g"][l] = _xa_kv_bwd(memf, dk, dv, sp, gxa)[0]
        if l == 0:
            last.append(_exchange_start([d_xa], (True,), dx, "scatter_start_0b"))

        dmo, dm, dgp, dyk, gp = _merge_bwd_act(dx, s["mo"], s["g"], s["yk"], gwb, gwo, sp, last[-1]["token"] if last else sp)
        rep["mix_post_g"][l] = gp[0]
        d_wb, d_wo = _merge_bwd_w(s["ma"], s["mb"], s["mc"], s["md"], dyk, s["mg"], dmo)
        dz, dcw, gp = _mixA_bwd(s["z"], dm, cwa, sp)
        rep["conv_a_w"][l], rep["conv_a_b"][l] = dcw, gp[0]
        dz, dwp, gp = _mixB_bwd(s["z"], dm, wp, sp, dz)
        rep["pool_w"][l], rep["pool_scale"][l] = dwp, gp[0]
        dz, dws, dbs, gp = _mixC_bwd(s["z"], dm, ws, bsc, sp, dz)
        rep["sgu_ws"][l], rep["sgu_b"][l], rep["sgu_ln_g"][l], rep["sgu_ln_b"][l] = dws, dbs[:, :, 0], gp[0], gp[1]
        dyd, gp = _mixD_ln_bwd(dm, s["yd"], sp)
        rep["conv_d_ln_g"][l], rep["conv_d_ln_b"][l] = gp[0], gp[1]
        dz, dcw, gp = _mixD_conv_bwd(s["z"], dyd, cwd, dz)
        rep["conv_d_w"][l], rep["conv_d_b"][l] = dcw, gp[0]
        dx, gp = _mix_in_bwd_act(dz, dgp, dx, s["x1"], sp, gma)
        rep["mix_pre_g"][l] = gp[0]
        d_ma, dbg = _mix_in_bwd_w(dz, dgp, s["hbm"])
        rep["b_gate"][l] = dbg[:, 0, :].reshape(-1)
        if l == 0:
            last.append(_exchange_start([d_ma, d_wo, d_wb], (True,) * 3, dx, "scatter_start_0c"))

        dx, dyb, da, db, gp = _ffn_bwd_act(dx, s["x0"], s["y1"], s["a1"], s["b1"], sp, "ffn1_pre_g", "ffn1_post_g", gf1,
                                           last[-1]["token"] if last else sp)
        rep["ffn1_pre_g"][l], rep["ffn1_post_g"][l] = gp[0], gp[1]
        flat = jnp.concatenate([rep[n][l].reshape(-1) for n in _REP_NAMES])
        flat = jnp.pad(flat, (0, -flat.size % 2048)).reshape(-1, 128).astype(CDT)
        if l == 0:
            last.append(_exchange_start([flat], (False,), dx, "scatter_start_0d"))
        d_f1 = _ffn_bwd_w(s["hb1"], dyb, s["a1"], s["b1"], da, db, last[-1]["token"] if last else sp)

        if pending:
            r = _exchange_wait(pending[1], dx, f"scatter_wait_{pending[0]}")
            summed[pending[0]] = dict(zip(("f1", "f2", "ma", "wo", "xa", "wb", "flat"), r))
        if l == 0:
            last.append(_exchange_start([d_f1], (True,), dx, "scatter_start_0e"))
        else:
            pending = (l, _exchange_start([d_f1, d_f2, d_ma, d_wo, d_xa, d_wb, flat], (True,) * 6 + (False,), dx,
                                          f"scatter_start_{l}"))

    pack_shape = {"f1": (3 * FS, D), "f2": (3 * FS, D), "ma": (2 * MW, D), "wo": (GW, D), "xa": (4 * GW, D),
                  "wb": (4 * MW, GW), "flat": tuple(flat.shape)}
    stk = {k: lax.empty((L,) + s, F32) for k, s in pack_shape.items()}

    def land(k, r, l):
        stk[k] = _slot_sum_into(stk[k], r.reshape((NS,) + pack_shape[k]), l)

    for l in range(1, L):
        for k, r in summed[l].items():
            land(k, r, l)
    (r,) = _exchange_wait(last[0], dx, "scatter_wait_0a")
    land("f2", r, 0)
    (r,) = _exchange_wait(last[1], dx, "scatter_wait_0b")
    land("xa", r, 0)

    G, deltas, new_m, new_v = {}, {}, {}, {}

    def update_block(n, k, blk, transposed):
        tr = _t if transposed else (lambda a: a)
        out = _adamw_block(tr(W[n]), stk[k], tr(M[n]), tr(V[n]), blk)
        G[n], deltas[n], new_m[n], new_v[n] = (tr(a) for a in out)
        return deltas[n]

    def update(n):
        deltas[n], new_m[n], new_v[n] = _adamw(W[n], G[n], M[n], V[n])
        return deltas[n]

    done = [update_block("ffn2_w1", "f2", 0, True), update_block("ffn2_w3", "f2", 1, True),
            update_block("ffn2_w2", "f2", 2, False)]
    done += [update_block(n, "xa", i, False) for i, n in enumerate(("xa_wq", "xa_wk", "xa_wv", "xa_wo"))]
    r = _exchange_wait(last[2], done + [stk[k] for k in ("f1", "ma", "wo", "wb", "flat")], "scatter_wait_0c")
    for k, v in zip(("ma", "wo", "wb"), r):
        land(k, v, 0)
    G["w_in"], G["w_gate"] = _t(stk["ma"][:, :MW]), _t(stk["ma"][:, MW:])
    G["w_branch"] = stk["wb"].reshape(W["w_branch"].shape)
    done = [update("w_in"), update("w_gate"), update("w_branch"), update_block("w_o", "wo", 0, False)]
    (r,) = _exchange_wait(last[3], done, "scatter_wait_0d")
    land("flat", r, 0)

    tot = [stk["flat"][l].reshape(-1) for l in range(L)]
    off = 0
    for n in _REP_NAMES:
        shape = (KA, NS * CS) if n == "conv_a_w" else (KD, NS * CS) if n == "conv_d_w" else W[n].shape[1:]
        size = 1
        for d in shape:
            size *= d
        G[n] = jnp.stack([tot[l][off:off + size].reshape(shape) for l in range(L)])
        off += size
    for n in ("conv_a_w", "conv_d_w"):
        G[n] = lax.dynamic_slice_in_dim(G[n], me * CS, CS, axis=2)
    done = [update(n) for n in _REP_NAMES]

    (r,) = _exchange_wait(last[4], done, "scatter_wait_0e")
    land("f1", r, 0)
    update_block("ffn1_w1", "f1", 0, True)
    update_block("ffn1_w3", "f1", 1, True)
    update_block("ffn1_w2", "f1", 2, False)
    grad_x = dx.reshape(x.shape)
    return (loss, grad_x, *[G[n] for n in _W_NAMES], *[deltas[n] for n in _W_NAMES],
            *[new_m[n] for n in _W_NAMES], *[new_v[n] for n in _W_NAMES])


def kernel(x, mem, ffn1_pre_g, ffn1_post_g, ffn1_w1, ffn1_w3, ffn1_w2, mix_pre_g, mix_post_g, w_in, conv_a_w, conv_a_b, pool_w, pool_scale, sgu_ln_g, sgu_ln_b, sgu_ws, sgu_b, conv_d_w, conv_d_b, conv_d_ln_g, conv_d_ln_b, w_branch, w_gate, b_gate, w_o, xa_pre_g, xa_post_g, mem_g, xa_wq, xa_wk, xa_wv, xa_wo, ffn2_pre_g, ffn2_post_g, ffn2_w1, ffn2_w3, ffn2_w2, loss_target, m_ffn1_pre_g, m_ffn1_post_g, m_ffn1_w1, m_ffn1_w3, m_ffn1_w2, m_mix_pre_g, m_mix_post_g, m_w_in, m_conv_a_w, m_conv_a_b, m_pool_w, m_pool_scale, m_sgu_ln_g, m_sgu_ln_b, m_sgu_ws, m_sgu_b, m_conv_d_w, m_conv_d_b, m_conv_d_ln_g, m_conv_d_ln_b, m_w_branch, m_w_gate, m_b_gate, m_w_o, m_xa_pre_g, m_xa_post_g, m_mem_g, m_xa_wq, m_xa_wk, m_xa_wv, m_xa_wo, m_ffn2_pre_g, m_ffn2_post_g, m_ffn2_w1, m_ffn2_w3, m_ffn2_w2, v_ffn1_pre_g, v_ffn1_post_g, v_ffn1_w1, v_ffn1_w3, v_ffn1_w2, v_mix_pre_g, v_mix_post_g, v_w_in, v_conv_a_w, v_conv_a_b, v_pool_w, v_pool_scale, v_sgu_ln_g, v_sgu_ln_b, v_sgu_ws, v_sgu_b, v_conv_d_w, v_conv_d_b, v_conv_d_ln_g, v_conv_d_ln_b, v_w_branch, v_w_gate, v_b_gate, v_w_o, v_xa_pre_g, v_xa_post_g, v_mem_g, v_xa_wq, v_xa_wk, v_xa_wv, v_xa_wo, v_ffn2_pre_g, v_ffn2_post_g, v_ffn2_w1, v_ffn2_w3, v_ffn2_w2):
    args = dict(locals())
    W = {n: args[n] for n in _W_NAMES}
    M = {n: args["m_" + n] for n in _W_NAMES}
    V = {n: args["v_" + n] for n in _W_NAMES}
    return _step(x, mem, loss_target, W, M, V)
```

```python
import jax
import jax.numpy as jnp
from jax import lax
from jax.experimental import pallas as pl
from jax.experimental.pallas import tpu as pltpu

F32 = jnp.float32
CDT = jnp.bfloat16
EPS = 1e-6
NS = 8
GW = 128
MW = 512
CHUNK = 64
XA_HEADS = 4
POOL_WINDOWS = (2, 4, 8, 16)
VMEM_LIMIT = 56 * 1024 * 1024
ADAM_LR, ADAM_B1, ADAM_B2, ADAM_EPS, ADAM_WD, ADAM_STEP = 0.001, 0.9, 0.999, 1e-08, 0.01, 10

SDS = jax.ShapeDtypeStruct

_SP_NAMES = (("ffn1_pre_g", 1024), ("ffn1_post_g", 1024), ("mix_pre_g", 1024), ("mix_post_g", 1024),
             ("xa_pre_g", 1024), ("xa_post_g", 1024), ("mem_g", 1024), ("ffn2_pre_g", 1024), ("ffn2_post_g", 1024),
             ("conv_a_b", 512), ("pool_scale", 512), ("sgu_ln_g", 512), ("sgu_ln_b", 512), ("conv_d_b", 512),
             ("conv_d_ln_g", 512), ("conv_d_ln_b", 512), ("b_gate", 4096))
_SP = {}
_off = 0
for _n, _w in _SP_NAMES:
    _SP[_n] = (_off, _w)
    _off += _w
_SP_TOTAL = _off


def _call(body, name, grid, in_specs, out_specs, out_shape, scratch=(), aliases=None):
    return pl.pallas_call(
        body, name=name, grid=grid, in_specs=in_specs, out_specs=out_specs, out_shape=out_shape,
        scratch_shapes=list(scratch), input_output_aliases=aliases or {},
        compiler_params=pltpu.CompilerParams(dimension_semantics=("arbitrary",) * len(grid),
                                             vmem_limit_bytes=VMEM_LIMIT))


def _nn(a, b):
    return lax.dot_general(a, b, (((1,), (0,)), ((), ())), preferred_element_type=F32)


def _nt(a, b):
    return lax.dot_general(a, b, (((1,), (1,)), ((), ())), preferred_element_type=F32)


def _tn(a, b):
    return lax.dot_general(a, b, (((0,), (0,)), ((), ())), preferred_element_type=F32)


def _rms(x):
    r = lax.rsqrt(jnp.mean(x * x, axis=-1, keepdims=True) + EPS)
    return x * r, r


def _rms_bwd(n, r, g, dout):
    dn = dout * g
    dx = r * (dn - n * jnp.mean(dn * n, axis=-1, keepdims=True))
    return dx, jnp.sum(dout * n, axis=0, keepdims=True)


def _ln(y):
    mu = jnp.mean(y, axis=-1, keepdims=True)
    yc = y - mu
    rs = lax.rsqrt(jnp.mean(yc * yc, axis=-1, keepdims=True) + EPS)
    return yc * rs, rs


def _ln_bwd(xh, rs, dxh):
    return rs * (dxh - jnp.mean(dxh, axis=-1, keepdims=True) - xh * jnp.mean(dxh * xh, axis=-1, keepdims=True))


def _silu_parts(a):
    s = jax.nn.sigmoid(a)
    sl = a * s
    return sl, s + sl * (1.0 - s)


_GELU_C = 0.7978845608028654
_GELU_A = 0.044715


def _gelu(x):
    return 0.5 * x * (1.0 + jnp.tanh(_GELU_C * (x + _GELU_A * x * x * x)))


def _gelu_parts(x):
    t = jnp.tanh(_GELU_C * (x + _GELU_A * x * x * x))
    g = 0.5 * x * (1.0 + t)
    dg = 0.5 * (1.0 + t) + 0.5 * x * (1.0 - t * t) * _GELU_C * (1.0 + 3.0 * _GELU_A * x * x)
    return g, dg


def _spspec(name, width, imap):
    off = _SP[name][0]
    assert off % width == 0
    return pl.BlockSpec((1, width), lambda *a: (0, off // width + imap(*a)))


def _zero(*a):
    return 0


def _row_once(tm, d):
    return pl.BlockSpec((tm, d), lambda i, j: (i, 0), pipeline_mode=pl.Buffered(1))


FFN_SG = 2


def _ffn_fwd(x, sp, pre, post, pf, dep):
    S, D = x.shape
    FS = pf.shape[1] // 3
    TM = min(512, S)
    SG, NG, W = FFN_SG, NS // FFN_SG, FFN_SG * FS

    def body(x_ref, pg_ref, qg_ref, w1_ref, w3_ref, w2_ref, dep_ref, xo_ref, hb_ref, a_ref, b_ref, y_ref, hb_s, acc):
        j = pl.program_id(1)

        @pl.when(j == 0)
        def _():
            n, _ = _rms(x_ref[...])
            hb = (n * pg_ref[...]).astype(CDT)
            hb_s[...] = hb
            hb_ref[...] = hb
            acc[...] = jnp.zeros_like(acc)

        hb = hb_s[...]
        a = _nt(hb, w1_ref[...].reshape(W, D))
        b = _nt(hb, w3_ref[...].reshape(W, D))
        a_ref[...] = a.astype(CDT)
        b_ref[...] = b.astype(CDT)
        u = (a * jax.nn.sigmoid(a) * b).astype(CDT)
        acc[...] += _nn(u, w2_ref[...].reshape(W, D))

        @pl.when(j == NG - 1)
        def _():
            y = acc[...]
            y_ref[...] = y.astype(CDT)
            n, _ = _rms(y)
            xo_ref[...] = x_ref[...] + 0.5 * (n * qg_ref[...])

    row1 = pl.BlockSpec((TM, D), lambda i, j: (i, 0))
    grp = lambda i, j: (j, i, 0)
    return _call(
        body, "ffn_fwd", (S // TM, NG),
        [row1, _spspec(pre, D, _zero), _spspec(post, D, _zero),
         pl.BlockSpec((SG, FS, D), lambda i, j: (j, 0, 0)), pl.BlockSpec((SG, FS, D), lambda i, j: (j, 1, 0)),
         pl.BlockSpec((SG, FS, D), lambda i, j: (j, 2, 0)), pl.BlockSpec(memory_space=pl.ANY)],
        [row1, row1, pl.BlockSpec((None, TM, W), grp), pl.BlockSpec((None, TM, W), grp), row1],
        [SDS((S, D), F32), SDS((S, D), CDT), SDS((NG, S, W), CDT), SDS((NG, S, W), CDT), SDS((S, D), CDT)],
        [pltpu.VMEM((TM, D), CDT), pltpu.VMEM((TM, D), F32)])(x, sp, sp, pf, pf, pf, dep)


def _ffn_bwd_act(dxo, x, y, a, b, sp, pre, post, pf, dep):
    S, D = x.shape
    FS = pf.shape[1] // 3
    TM = min(512, S)
    SG, NG, W = FFN_SG, NS // FFN_SG, FFN_SG * FS

    def body(dxo_ref, x_ref, y_ref, a_ref, b_ref, pg_ref, qg_ref, w1_ref, w3_ref, w2_ref, dep_ref,
             dx_ref, dyb_ref, da_ref, db_ref, gp_ref, dyb_s, acc):
        i = pl.program_id(0)
        j = pl.program_id(1)

        @pl.when((i == 0) & (j == 0))
        def _():
            gp_ref[...] = jnp.zeros_like(gp_ref)

        @pl.when(j == 0)
        def _():
            n, r = _rms(y_ref[...].astype(F32))
            dy, dg = _rms_bwd(n, r, qg_ref[...], 0.5 * dxo_ref[...])
            dyb = dy.astype(CDT)
            dyb_s[...] = dyb
            dyb_ref[...] = dyb
            gp_ref[1:2, :] += dg
            acc[...] = jnp.zeros_like(acc)

        sl, dsl = _silu_parts(a_ref[...].astype(F32))
        du = _nt(dyb_s[...], w2_ref[...].reshape(W, D))
        db = (du * sl).astype(CDT)
        da = (du * b_ref[...].astype(F32) * dsl).astype(CDT)
        da_ref[...] = da
        db_ref[...] = db
        acc[...] += _nn(da, w1_ref[...].reshape(W, D)) + _nn(db, w3_ref[...].reshape(W, D))

        @pl.when(j == NG - 1)
        def _():
            n, r = _rms(x_ref[...])
            dx, dg = _rms_bwd(n, r, pg_ref[...], acc[...])
            dx_ref[...] = dxo_ref[...] + dx
            gp_ref[0:1, :] += dg

    row = lambda i, j: (i, 0)
    grp = lambda i, j: (j, i, 0)
    return _call(
        body, "ffn_bwd_act", (S // TM, NG),
        [pl.BlockSpec((TM, D), row), pl.BlockSpec((TM, D), row), pl.BlockSpec((TM, D), row),
         pl.BlockSpec((None, TM, W), grp), pl.BlockSpec((None, TM, W), grp),
         _spspec(pre, D, _zero), _spspec(post, D, _zero),
         pl.BlockSpec((SG, FS, D), lambda i, j: (j, 0, 0)), pl.BlockSpec((SG, FS, D), lambda i, j: (j, 1, 0)),
         pl.BlockSpec((SG, FS, D), lambda i, j: (j, 2, 0)), pl.BlockSpec(memory_space=pl.ANY)],
        [pl.BlockSpec((TM, D), row), pl.BlockSpec((TM, D), row), pl.BlockSpec((None, TM, W), grp),
         pl.BlockSpec((None, TM, W), grp), pl.BlockSpec((8, D), lambda i, j: (0, 0))],
        [SDS((S, D), F32), SDS((S, D), CDT), SDS((NG, S, W), CDT), SDS((NG, S, W), CDT), SDS((8, D), F32)],
        [pltpu.VMEM((TM, D), CDT), pltpu.VMEM((TM, D), F32)])(dxo, x, y, a, b, sp, sp, pf, pf, pf, dep)


def _ffn_bwd_w(hb, dyb, a, b, da, db, dep):
    S, D = hb.shape
    SG, NG = FFN_SG, NS // FFN_SG
    W = a.shape[2]
    FS = W // SG
    TK = min(1024, S)
    NK = S // TK

    def body(hb_ref, dyb_ref, a_ref, b_ref, da_ref, db_ref, dep_ref, g_ref, acc):
        k = pl.program_id(1)

        @pl.when(k == 0)
        def _():
            acc[...] = jnp.zeros_like(acc)

        af = a_ref[...].astype(F32)
        u = (af * jax.nn.sigmoid(af) * b_ref[...].astype(F32)).astype(CDT)
        hb = hb_ref[...]
        acc[0:W, :] += _tn(da_ref[...], hb)
        acc[W:2 * W, :] += _tn(db_ref[...], hb)
        acc[2 * W:3 * W, :] += _tn(u, dyb_ref[...])

        @pl.when(k == NK - 1)
        def _():
            for s in range(SG):
                for r in range(3):
                    g_ref[s, r * FS:(r + 1) * FS, :] = acc[r * W + s * FS:r * W + (s + 1) * FS, :].astype(CDT)

    row = lambda j, k: (k, 0)
    grp = lambda j, k: (j, k, 0)
    return _call(
        body, "ffn_bwd_w", (NG, NK),
        [pl.BlockSpec((TK, D), row), pl.BlockSpec((TK, D), row)] + [pl.BlockSpec((None, TK, W), grp)] * 4 + [_ANY],
        pl.BlockSpec((SG, 3 * FS, D), lambda j, k: (j, 0, 0)),
        SDS((NS, 3 * FS, D), CDT),
        [pltpu.VMEM((3 * W, D), F32)])(hb, dyb, a, b, da, db, dep)


def _mix_in(x, sp, pma, dep):
    S, D = x.shape
    TM = min(1024, S)

    def body(x_ref, pg_ref, bg_ref, wi_ref, wg_ref, dep_ref, hb_ref, z_ref, g_ref, hb_s):
        @pl.when(pl.program_id(1) == 0)
        def _():
            n, _ = _rms(x_ref[...])
            hb = (n * pg_ref[...]).astype(CDT)
            hb_s[...] = hb
            hb_ref[...] = hb

        hb = hb_s[...]
        z_ref[...] = _nt(hb, wi_ref[...]).astype(CDT)
        g_ref[...] = jax.nn.sigmoid(_nt(hb, wg_ref[...]) + bg_ref[...]).astype(CDT)

    return _call(
        body, "mix_in", (S // TM, NS),
        [_row_once(TM, D), _spspec("mix_pre_g", D, _zero), _spspec("b_gate", MW, lambda i, j: j),
         pl.BlockSpec((None, MW, D), lambda i, j: (j, 0, 0)), pl.BlockSpec((None, MW, D), lambda i, j: (j, 1, 0)), _ANY],
        [_row_once(TM, D), pl.BlockSpec((None, TM, MW), lambda i, j: (j, i, 0)),
         pl.BlockSpec((None, TM, MW), lambda i, j: (j // 2, i, j % 2))],
        [SDS((S, D), CDT), SDS((NS, S, MW), CDT), SDS((4, S, D), CDT)],
        [pltpu.VMEM((TM, D), CDT)])(x, sp, sp, pma, pma, dep)


def _mix_in_bwd_act(dz, dgp, dxr, x, sp, pma):
    S, D = x.shape
    TM = min(1024, S)

    def body(dz_ref, dg_ref, dxr_ref, x_ref, pg_ref, wi_ref, wg_ref, dx_ref, gp_ref, acc):
        i = pl.program_id(0)
        j = pl.program_id(1)

        @pl.when((i == 0) & (j == 0))
        def _():
            gp_ref[...] = jnp.zeros_like(gp_ref)

        @pl.when(j == 0)
        def _():
            acc[...] = jnp.zeros_like(acc)

        acc[...] += _nn(dz_ref[...], wi_ref[...]) + _nn(dg_ref[...], wg_ref[...])

        @pl.when(j == NS - 1)
        def _():
            n, r = _rms(x_ref[...])
            dx, dg = _rms_bwd(n, r, pg_ref[...], acc[...])
            dx_ref[...] = dxr_ref[...] + dx
            gp_ref[0:1, :] += dg

    return _call(
        body, "mix_in_bwd_act", (S // TM, NS),
        [pl.BlockSpec((None, TM, MW), lambda i, j: (j, i, 0)), pl.BlockSpec((None, TM, MW), lambda i, j: (j // 2, i, j % 2)),
         _row_once(TM, D), _row_once(TM, D), _spspec("mix_pre_g", D, _zero),
         pl.BlockSpec((None, MW, D), lambda i, j: (j, 0, 0)), pl.BlockSpec((None, MW, D), lambda i, j: (j, 1, 0))],
        [_row_once(TM, D), pl.BlockSpec((8, D), lambda i, j: (0, 0))],
        [SDS((S, D), F32), SDS((8, D), F32)],
        [pltpu.VMEM((TM, D), F32)])(dz, dgp, dxr, x, sp, pma, pma)


def _mix_in_bwd_w(dz, dgp, hb):
    S, D = hb.shape
    TK = min(2048, S)
    NK = S // TK

    def body(dz_ref, dg_ref, hb_ref, g_ref, bg_ref, acc):
        k = pl.program_id(1)

        @pl.when(k == 0)
        def _():
            acc[...] = jnp.zeros_like(acc)
            bg_ref[...] = jnp.zeros_like(bg_ref)

        hb = hb_ref[...]
        dg = dg_ref[...]
        acc[0:MW, :] += _tn(dz_ref[...], hb)
        acc[MW:2 * MW, :] += _tn(dg, hb)
        bg_ref[0:1, :] += jnp.sum(dg.astype(F32), axis=0, keepdims=True)

        @pl.when(k == NK - 1)
        def _():
            g_ref[...] = acc[...].astype(CDT)

    return _call(
        body, "mix_in_bwd_w", (NS, NK),
        [pl.BlockSpec((None, TK, MW), lambda j, k: (j, k, 0)), pl.BlockSpec((None, TK, MW), lambda j, k: (j // 2, k, j % 2)),
         pl.BlockSpec((TK, D), lambda j, k: (k, 0))],
        [pl.BlockSpec((None, 2 * MW, D), lambda j, k: (j, 0, 0)), pl.BlockSpec((None, 8, MW), lambda j, k: (j, 0, 0))],
        [SDS((NS, 2 * MW, D), CDT), SDS((NS, 8, MW), F32)],
        [pltpu.VMEM((2 * MW, D), F32)])(dz, dgp, hb)


def _causal_taps(pad_ref, i, ch, halo, k_taps, lanes=slice(None)):
    val = pad_ref[pl.ds(pl.multiple_of(i * ch, 8), ch + halo), lanes]
    base = {}
    out = []
    for k in range(k_taps):
        q, r = divmod(k_taps - 1 - k, 8)
        if r not in base:
            base[r] = pltpu.roll(val, r, 0) if r else val
        out.append((k, base[r][halo - 8 * q:halo - 8 * q + ch, :]))
    return out


def _anti_taps(pad_ref, i, ch, halo, k_taps, lanes=slice(None)):
    val = pad_ref[pl.ds(pl.multiple_of(i * ch, 8), ch + halo), lanes]
    n = ch + halo
    base = {}
    out = []
    for k in range(k_taps):
        q, r = divmod(k_taps - 1 - k, 8)
        if r not in base:
            base[r] = pltpu.roll(val, n - r, 0) if r else val
        out.append((k, base[r][8 * q:8 * q + ch, :]))
    return out


def _conv_geometry(S, k_taps):
    halo = 8 * ((k_taps - 1 + 7) // 8)
    ch = min(256, S)
    return halo, ch, S // ch


def _rows(i, ch):
    return pl.ds(pl.multiple_of(i * ch, ch), ch)


def _mixA_fwd(z, cw, sp):
    S = z.shape[1]
    K = cw.shape[0]
    H, CH, NCH = _conv_geometry(S, K)

    def body(z_ref, w_ref, b_ref, o_ref, pad):
        pad[0:H, :] = jnp.zeros((H, GW), F32)

        def fill(i, c):
            r = _rows(i, CH)
            pad[pl.ds(pl.multiple_of(i * CH + H, 8), CH), :] = z_ref[2, r, :].astype(F32) * z_ref[0, r, :].astype(F32)
            return c

        lax.fori_loop(0, NCH, fill, 0)

        def conv(i, c):
            r = _rows(i, CH)
            acc = jnp.zeros((CH, GW), F32)
            for k, sh in _causal_taps(pad, i, CH, H, K):
                acc = acc + w_ref[k:k + 1, :] * sh
            o_ref[r, :] = (z_ref[1, r, :].astype(F32) * (acc + b_ref[...])).astype(CDT)
            return c

        lax.fori_loop(0, NCH, conv, 0)

    return _call(
        body, "mixA_fwd", (MW // GW,),
        [pl.BlockSpec((3, S, GW), lambda c: (0, 0, c)), pl.BlockSpec((K, GW), lambda c: (0, c)),
         _spspec("conv_a_b", GW, lambda c: c)],
        pl.BlockSpec((S, GW), lambda c: (0, c)), SDS((S, MW), CDT),
        [pltpu.VMEM((H + S, GW), F32)])(z, cw, sp)


def _mixA_bwd(z, dm, cw, sp):
    S = z.shape[1]
    K = cw.shape[0]
    H, CH, NCH = _conv_geometry(S, K)

    def body(z_ref, dm_ref, w_ref, b_ref, dz_ref, dw_ref, db_ref, pad, dpad, dw_s):
        pad[0:H, :] = jnp.zeros((H, GW), F32)
        dpad[pl.ds(S, H), :] = jnp.zeros((H, GW), F32)
        dw_s[...] = jnp.zeros_like(dw_s)
        db_ref[...] = jnp.zeros_like(db_ref)

        def fill(i, c):
            r = _rows(i, CH)
            pad[pl.ds(pl.multiple_of(i * CH + H, 8), CH), :] = z_ref[2, r, :].astype(F32) * z_ref[0, r, :].astype(F32)
            return c

        lax.fori_loop(0, NCH, fill, 0)

        def p1(i, c):
            r = _rows(i, CH)
            taps = _causal_taps(pad, i, CH, H, K)
            acc = jnp.zeros((CH, GW), F32)
            for k, sh in taps:
                acc = acc + w_ref[k:k + 1, :] * sh
            dmf = dm_ref[r, :].astype(F32)
            dz_ref[1, r, :] = (dmf * (acc + b_ref[...])).astype(CDT)
            dc = dmf * z_ref[1, r, :].astype(F32)
            dpad[r, :] = dc
            for k, sh in taps:
                dw_s[k:k + 1, :] += jnp.sum(dc * sh, axis=0, keepdims=True)
            db_ref[0:1, :] += jnp.sum(dc, axis=0, keepdims=True)
            return c

        lax.fori_loop(0, NCH, p1, 0)

        def p2(i, c):
            r = _rows(i, CH)
            dq = jnp.zeros((CH, GW), F32)
            for k, sh in _anti_taps(dpad, i, CH, H, K):
                dq = dq + w_ref[k:k + 1, :] * sh
            dz_ref[0, r, :] = (dq * z_ref[2, r, :].astype(F32)).astype(CDT)
            dz_ref[2, r, :] = (dq * z_ref[0, r, :].astype(F32)).astype(CDT)
            return c

        lax.fori_loop(0, NCH, p2, 0)
        dw_ref[...] = dw_s[0:K, :]

    return _call(
        body, "mixA_bwd", (MW // GW,),
        [pl.BlockSpec((3, S, GW), lambda c: (0, 0, c)), pl.BlockSpec((None, S, GW), lambda c: (0, 0, c)),
         pl.BlockSpec((K, GW), lambda c: (0, c)), _spspec("conv_a_b", GW, lambda c: c)],
        [pl.BlockSpec((3, S, GW), lambda c: (0, 0, c)), pl.BlockSpec((K, GW), lambda c: (0, c)),
         pl.BlockSpec((8, GW), lambda c: (0, c))],
        [SDS((NS, S, MW), CDT), SDS((K, MW), F32), SDS((8, MW), F32)],
        [pltpu.VMEM((H + S, GW), F32), pltpu.VMEM((S + H, GW), F32), pltpu.VMEM((8 * ((K + 7) // 8), GW), F32)])(z, dm, cw, sp)


def _mixD_conv_fwd(z, cw, sp):
    S = z.shape[1]
    K = cw.shape[0]
    H, CH, NCH = _conv_geometry(S, K)

    def body(z_ref, w_ref, b_ref, o_ref, pad):
        pad[0:H, :] = jnp.zeros((H, GW), F32)

        def fill(i, c):
            r = _rows(i, CH)
            pad[pl.ds(pl.multiple_of(i * CH + H, 8), CH), :] = (
                z_ref[0, r, :].astype(F32) * jax.nn.sigmoid(z_ref[1, r, :].astype(F32)))
            return c

        lax.fori_loop(0, NCH, fill, 0)

        def conv(i, c):
            acc = jnp.zeros((CH, GW), F32)
            for k, sh in _causal_taps(pad, i, CH, H, K):
                acc = acc + w_ref[k:k + 1, :] * sh
            o_ref[_rows(i, CH), :] = (acc + b_ref[...]).astype(CDT)
            return c

        lax.fori_loop(0, NCH, conv, 0)

    return _call(
        body, "mixD_conv_fwd", (MW // GW,),
        [pl.BlockSpec((2, S, GW), lambda c: (3, 0, c)), pl.BlockSpec((K, GW), lambda c: (0, c)),
         _spspec("conv_d_b", GW, lambda c: c)],
        pl.BlockSpec((S, GW), lambda c: (0, c)), SDS((S, MW), CDT),
        [pltpu.VMEM((H + S, GW), F32)])(z, cw, sp)


def _mixD_conv_bwd(z, dy, cw, dz):
    S = z.shape[1]
    K = cw.shape[0]
    H, CH, NCH = _conv_geometry(S, K)

    def body(z_ref, dy_ref, w_ref, dzin_ref, dz_ref, dw_ref, db_ref, pad, dpad, dw_s):
        pad[0:H, :] = jnp.zeros((H, GW), F32)
        dpad[pl.ds(S, H), :] = jnp.zeros((H, GW), F32)
        dw_s[...] = jnp.zeros_like(dw_s)
        db_ref[...] = jnp.zeros_like(db_ref)

        def fill(i, c):
            r = _rows(i, CH)
            pad[pl.ds(pl.multiple_of(i * CH + H, 8), CH), :] = (
                z_ref[0, r, :].astype(F32) * jax.nn.sigmoid(z_ref[1, r, :].astype(F32)))
            dpad[r, :] = dy_ref[r, :].astype(F32)
            return c

        lax.fori_loop(0, NCH, fill, 0)

        def p1(i, c):
            dyf = dy_ref[_rows(i, CH), :].astype(F32)
            for k, sh in _causal_taps(pad, i, CH, H, K):
                dw_s[k:k + 1, :] += jnp.sum(dyf * sh, axis=0, keepdims=True)
            db_ref[0:1, :] += jnp.sum(dyf, axis=0, keepdims=True)
            return c

        lax.fori_loop(0, NCH, p1, 0)

        def p2(i, c):
            r = _rows(i, CH)
            dq = jnp.zeros((CH, GW), F32)
            for k, sh in _anti_taps(dpad, i, CH, H, K):
                dq = dq + w_ref[k:k + 1, :] * sh
            a = z_ref[0, r, :].astype(F32)
            sg = jax.nn.sigmoid(z_ref[1, r, :].astype(F32))
            dz_ref[0, r, :] = (dq * sg).astype(CDT)
            dz_ref[1, r, :] = (dq * a * sg * (1.0 - sg)).astype(CDT)
            return c

        lax.fori_loop(0, NCH, p2, 0)
        dw_ref[...] = dw_s[0:K, :]

    return _call(
        body, "mixD_conv_bwd", (MW // GW,),
        [pl.BlockSpec((2, S, GW), lambda c: (3, 0, c)), pl.BlockSpec((S, GW), lambda c: (0, c)),
         pl.BlockSpec((K, GW), lambda c: (0, c)), _ANY],
        [pl.BlockSpec((2, S, GW), lambda c: (3, 0, c)), pl.BlockSpec((K, GW), lambda c: (0, c)),
         pl.BlockSpec((8, GW), lambda c: (0, c))],
        [SDS((NS, S, MW), CDT), SDS((K, MW), F32), SDS((8, MW), F32)],
        [pltpu.VMEM((H + S, GW), F32), pltpu.VMEM((S + H, GW), F32), pltpu.VMEM((8 * ((K + 7) // 8), GW), F32)],
        aliases={3: 0})(z, dy, cw, dz)


def _mixD_ln_bwd(dm, yd, sp):
    S = yd.shape[0]
    TM = min(512, S)

    def body(dm_ref, y_ref, lg_ref, lb_ref, dy_ref, gp_ref):
        @pl.when(pl.program_id(0) == 0)
        def _():
            gp_ref[...] = jnp.zeros_like(gp_ref)

        xh, rs = _ln(y_ref[...].astype(F32))
        _, dsl = _silu_parts(xh * lg_ref[...] + lb_ref[...])
        dl = dm_ref[...].astype(F32) * dsl
        gp_ref[0:1, :] += jnp.sum(dl * xh, axis=0, keepdims=True)
        gp_ref[1:2, :] += jnp.sum(dl, axis=0, keepdims=True)
        dy_ref[...] = _ln_bwd(xh, rs, dl * lg_ref[...]).astype(CDT)

    row = lambda i: (i, 0)
    return _call(
        body, "mixD_ln_bwd", (S // TM,),
        [pl.BlockSpec((None, TM, MW), lambda i: (3, i, 0)), pl.BlockSpec((TM, MW), row), _spspec("conv_d_ln_g", MW, _zero),
         _spspec("conv_d_ln_b", MW, _zero)],
        [pl.BlockSpec((TM, MW), row), pl.BlockSpec((8, MW), lambda i: (0, 0))],
        [SDS((S, MW), CDT), SDS((8, MW), F32)])(dm, yd, sp, sp)


def _box_causal(val, g):
    s = val
    for d in range(g + 1):
        s = s + pltpu.roll(s, 1 << d, 0)
    return s


def _box_anti(val, g):
    n = val.shape[0]
    s = val
    for d in range(g + 1):
        s = s + pltpu.roll(s, n - (1 << d), 0)
    return s


def _pool_count(i, ch, win):
    t = lax.broadcasted_iota(jnp.int32, (ch, GW), 0) + (i * ch + 1)
    return jnp.minimum(t, win).astype(F32)


def _mixB_fwd(z, wp, sp):
    S = z.shape[1]
    H, CH = 16, min(256, S)
    NCH = S // CH
    assert POOL_WINDOWS == tuple(2 << g for g in range(4))

    def body(p_ref, wp_ref, sc_ref, o_ref, pad):
        pad[0:H, :] = jnp.zeros((H, MW), F32)

        def fill(i, c):
            pad[pl.ds(pl.multiple_of(i * CH + H, 8), CH), :] = p_ref[_rows(i, CH), :].astype(F32)
            return c

        lax.fori_loop(0, NCH, fill, 0)

        def step(i, c):
            r = _rows(i, CH)
            for g in range(4):
                gs = slice(g * GW, (g + 1) * GW)
                val = pad[pl.ds(pl.multiple_of(i * CH, 8), CH + H), gs]
                pooled = _box_causal(val, g)[H:, :] / _pool_count(i, CH, POOL_WINDOWS[g]) - val[H:, :]
                mixed = _nn(pooled.astype(CDT), wp_ref[g].astype(CDT))
                o_ref[r, gs] = (mixed * sc_ref[:, gs]).astype(CDT)
            return c

        lax.fori_loop(0, NCH, step, 0)

    return _call(
        body, "mixB_fwd", (1,),
        [pl.BlockSpec((None, S, MW), lambda i: (3, 0, 0)), pl.BlockSpec((4, GW, GW), lambda i: (0, 0, 0)),
         _spspec("pool_scale", MW, _zero)],
        pl.BlockSpec((S, MW), lambda i: (0, 0)), SDS((S, MW), CDT),
        [pltpu.VMEM((H + S, MW), F32)])(z, wp, sp)


def _mixB_bwd(z, dm, wp, sp, dz):
    S = z.shape[1]
    H, CH = 16, min(256, S)
    NCH = S // CH

    def body(p_ref, dm_ref, wp_ref, sc_ref, dzin_ref, dz_ref, dwp_ref, dsc_ref, pad, rpad):
        pad[0:H, :] = jnp.zeros((H, MW), F32)
        rpad[pl.ds(S, H), :] = jnp.zeros((H, MW), F32)
        dwp_ref[...] = jnp.zeros_like(dwp_ref)
        dsc_ref[...] = jnp.zeros_like(dsc_ref)

        def fill(i, c):
            pad[pl.ds(pl.multiple_of(i * CH + H, 8), CH), :] = p_ref[_rows(i, CH), :].astype(F32)
            return c

        lax.fori_loop(0, NCH, fill, 0)

        def p1(i, c):
            r = _rows(i, CH)
            for g in range(4):
                gs = slice(g * GW, (g + 1) * GW)
                cnt = _pool_count(i, CH, POOL_WINDOWS[g])
                val = pad[pl.ds(pl.multiple_of(i * CH, 8), CH + H), gs]
                pooled = (_box_causal(val, g)[H:, :] / cnt - val[H:, :]).astype(CDT)
                w = wp_ref[g].astype(CDT)
                mixed = _nn(pooled, w)
                dmf = dm_ref[r, gs].astype(F32)
                dsc_ref[0:1, gs] += jnp.sum(dmf * mixed, axis=0, keepdims=True)
                dmx = (dmf * sc_ref[:, gs]).astype(CDT)
                dwp_ref[g] += _tn(pooled, dmx)
                rpad[r, gs] = _nt(dmx, w) / cnt
            return c

        lax.fori_loop(0, NCH, p1, 0)

        def p2(i, c):
            r = _rows(i, CH)
            for g in range(4):
                gs = slice(g * GW, (g + 1) * GW)
                val = rpad[pl.ds(pl.multiple_of(i * CH, 8), CH + H), gs]
                dp = _box_anti(val, g)[:CH, :] - val[:CH, :] * _pool_count(i, CH, POOL_WINDOWS[g])
                dz_ref[r, gs] = dp.astype(CDT)
            return c

        lax.fori_loop(0, NCH, p2, 0)

    return _call(
        body, "mixB_bwd", (1,),
        [pl.BlockSpec((None, S, MW), lambda i: (3, 0, 0)), pl.BlockSpec((None, S, MW), lambda i: (1, 0, 0)),
         pl.BlockSpec((4, GW, GW), lambda i: (0, 0, 0)), _spspec("pool_scale", MW, _zero), _ANY],
        [pl.BlockSpec((None, S, MW), lambda i: (3, 0, 0)), pl.BlockSpec((4, GW, GW), lambda i: (0, 0, 0)),
         pl.BlockSpec((8, MW), lambda i: (0, 0))],
        [SDS((NS, S, MW), CDT), SDS((4, GW, GW), F32), SDS((8, MW), F32)],
        [pltpu.VMEM((H + S, MW), F32), pltpu.VMEM((S + H, MW), F32)], aliases={4: 0})(z, dm, wp, sp, dz)


def _sgu_mask():
    ci = lax.broadcasted_iota(jnp.int32, (GW, GW), 0) // CHUNK
    cj = lax.broadcasted_iota(jnp.int32, (GW, GW), 1) // CHUNK
    return cj <= ci


def _mixC_fwd(z, ws, bsc, sp):
    S = z.shape[1]
    RB = min(512, S)

    def body(z_ref, lg_ref, lb_ref, ws_ref, bs_ref, o_ref):
        mask = _sgu_mask()
        gu = _gelu(z_ref[0].astype(F32))
        xh, _ = _ln(_gelu(z_ref[1].astype(F32)))
        vn = (xh * lg_ref[...] + lb_ref[...]).astype(CDT)
        for g in range(4):
            gs = slice(g * GW, (g + 1) * GW)
            wm = jnp.where(mask, ws_ref[g], 0.0).astype(CDT)
            for nb in range(RB // GW):
                rs = slice(nb * GW, (nb + 1) * GW)
                mixed = _nn(wm, vn[rs, gs]) + bs_ref[g]
                o_ref[rs, gs] = (gu[rs, gs] * mixed).astype(CDT)

    return _call(
        body, "mixC_fwd", (S // RB,),
        [pl.BlockSpec((2, RB, MW), lambda i: (2, i, 0)), _spspec("sgu_ln_g", MW, _zero), _spspec("sgu_ln_b", MW, _zero),
         pl.BlockSpec((4, GW, GW), lambda i: (0, 0, 0)), pl.BlockSpec((4, GW, 1), lambda i: (0, 0, 0))],
        pl.BlockSpec((RB, MW), lambda i: (i, 0)), SDS((S, MW), CDT))(z, sp, sp, ws, bsc)


def _mixC_bwd(z, dm, ws, bsc, sp, dz):
    S = z.shape[1]
    RB = min(512, S)
    NR = S // RB

    def body(z_ref, dm_ref, lg_ref, lb_ref, ws_ref, bs_ref, dzin_ref, dz_ref, dws_ref, dbs_ref, gp_ref, dvn_s):
        i = pl.program_id(0)

        @pl.when(i == 0)
        def _():
            dws_ref[...] = jnp.zeros_like(dws_ref)
            dbs_ref[...] = jnp.zeros_like(dbs_ref)
            gp_ref[...] = jnp.zeros_like(gp_ref)

        mask = _sgu_mask()
        gu, dgu = _gelu_parts(z_ref[0].astype(F32))
        gv, dgv = _gelu_parts(z_ref[1].astype(F32))
        xh, rs_ = _ln(gv)
        vn = (xh * lg_ref[...] + lb_ref[...]).astype(CDT)
        dmf = dm_ref[...].astype(F32)
        for g in range(4):
            gs = slice(g * GW, (g + 1) * GW)
            wm = jnp.where(mask, ws_ref[g], 0.0).astype(CDT)
            for nb in range(RB // GW):
                rs = slice(nb * GW, (nb + 1) * GW)
                vb = vn[rs, gs]
                mixed = _nn(wm, vb) + bs_ref[g]
                dz_ref[0, rs, gs] = (dmf[rs, gs] * mixed * dgu[rs, gs]).astype(CDT)
                dmx = dmf[rs, gs] * gu[rs, gs]
                dbs_ref[g] += dmx
                dmxc = dmx.astype(CDT)
                dws_ref[g] += _nt(dmxc, vb)
                dvn_s[rs, gs] = _tn(wm, dmxc)
        dvn = dvn_s[...]
        gp_ref[0:1, :] += jnp.sum(dvn * xh, axis=0, keepdims=True)
        gp_ref[1:2, :] += jnp.sum(dvn, axis=0, keepdims=True)
        dz_ref[1] = (_ln_bwd(xh, rs_, dvn * lg_ref[...]) * dgv).astype(CDT)

        @pl.when(i == NR - 1)
        def _():
            for g in range(4):
                dws_ref[g] = jnp.where(mask, dws_ref[g], 0.0)
                dbs_ref[g] = jnp.broadcast_to(jnp.sum(dbs_ref[g], axis=1, keepdims=True), (GW, GW))

    full3 = lambda i: (0, 0, 0)
    return _call(
        body, "mixC_bwd", (NR,),
        [pl.BlockSpec((2, RB, MW), lambda i: (2, i, 0)), pl.BlockSpec((None, RB, MW), lambda i: (2, i, 0)),
         _spspec("sgu_ln_g", MW, _zero), _spspec("sgu_ln_b", MW, _zero),
         pl.BlockSpec((4, GW, GW), full3), pl.BlockSpec((4, GW, 1), full3), _ANY],
        [pl.BlockSpec((2, RB, MW), lambda i: (2, i, 0)), pl.BlockSpec((4, GW, GW), full3), pl.BlockSpec((4, GW, GW), full3),
         pl.BlockSpec((8, MW), lambda i: (0, 0))],
        [SDS((NS, S, MW), CDT), SDS((4, GW, GW), F32), SDS((4, GW, GW), F32), SDS((8, MW), F32)],
        [pltpu.VMEM((RB, MW), F32)], aliases={6: 0})(z, dm, sp, sp, ws, bsc, dz)


def _unpack_wb(wb_ref, wbf):
    for j in range(NS):
        for k in range(4):
            wbf[k, :, j * GW:(j + 1) * GW] = wb_ref[j, k]


def _merge_fwd(ma, mb, mc, yd, g, wb, pwo, x, sp):
    S, D = x.shape
    TM = min(256, S)

    def body(ma_ref, mb_ref, mc_ref, yd_ref, g_ref, wb_ref, wo_ref, x_ref, lg_ref, lb_ref, qg_ref,
             xo_ref, md_ref, yk_ref, mg_ref, mo_ref, wbf):
        @pl.when(pl.program_id(0) == 0)
        def _():
            _unpack_wb(wb_ref, wbf)

        xh, _ = _ln(yd_ref[...].astype(F32))
        sl, _ = _silu_parts(xh * lg_ref[...] + lb_ref[...])
        md = sl.astype(CDT)
        md_ref[...] = md
        merged = jnp.zeros((TM, D), F32)
        for k, m in enumerate((ma_ref[...], mb_ref[...], mc_ref[...], md)):
            yk = _nn(m, wbf[k])
            yk_ref[k] = yk.astype(CDT)
            merged = merged + g_ref[k].astype(F32) * yk
        mgc = merged.astype(CDT)
        mg_ref[...] = mgc
        mo = _nn(mgc, wo_ref[...].reshape(D, D))
        mo_ref[...] = mo.astype(CDT)
        n, _ = _rms(mo)
        xo_ref[...] = x_ref[...] + n * qg_ref[...]

    row = lambda i: (i, 0)
    rowm = pl.BlockSpec((TM, MW), row)
    rowd = pl.BlockSpec((TM, D), row)
    row4 = pl.BlockSpec((4, TM, D), lambda i: (0, i, 0))
    return _call(
        body, "merge_fwd", (S // TM,),
        [rowm, rowm, rowm, rowm, row4, pl.BlockSpec((NS, 4, MW, GW), lambda i: (0, 0, 0, 0)),
         pl.BlockSpec((NS, GW, D), lambda i: (0, 0, 0)), rowd,
         _spspec("conv_d_ln_g", MW, _zero), _spspec("conv_d_ln_b", MW, _zero), _spspec("mix_post_g", D, _zero)],
        [rowd, rowm, row4, rowd, rowd],
        [SDS((S, D), F32), SDS((S, MW), CDT), SDS((4, S, D), CDT), SDS((S, D), CDT), SDS((S, D), CDT)],
        [pltpu.VMEM((4, MW, D), CDT)])(ma, mb, mc, yd, g, wb, pwo, x, sp, sp, sp)


def _merge_bwd_act(dxo, mo, g, yk, wb, pwo, sp, dep):
    S, D = dxo.shape
    TM = min(256, S)

    def body(dxo_ref, mo_ref, g_ref, yk_ref, wb_ref, wo_ref, qg_ref, dep_ref, dmo_ref, dm_ref, dgp_ref, dyk_ref, gp_ref, wbf):
        @pl.when(pl.program_id(0) == 0)
        def _():
            gp_ref[...] = jnp.zeros_like(gp_ref)
            _unpack_wb(wb_ref, wbf)

        n, r = _rms(mo_ref[...].astype(F32))
        dmo, dg = _rms_bwd(n, r, qg_ref[...], dxo_ref[...])
        gp_ref[0:1, :] += dg
        dmoc = dmo.astype(CDT)
        dmo_ref[...] = dmoc
        dmg = _nt(dmoc, wo_ref[...].reshape(D, D))
        for k in range(4):
            gk = g_ref[k].astype(F32)
            dyk = (dmg * gk).astype(CDT)
            dyk_ref[k] = dyk
            dgp_ref[k] = (dmg * yk_ref[k].astype(F32) * gk * (1.0 - gk)).astype(CDT)
            dm_ref[k] = _nt(dyk, wbf[k]).astype(CDT)

    rowd = pl.BlockSpec((TM, D), lambda i: (i, 0))
    row4 = pl.BlockSpec((4, TM, D), lambda i: (0, i, 0))
    return _call(
        body, "merge_bwd_act", (S // TM,),
        [rowd, rowd, row4, row4, pl.BlockSpec((NS, 4, MW, GW), lambda i: (0, 0, 0, 0)),
         pl.BlockSpec((NS, GW, D), lambda i: (0, 0, 0)), _spspec("mix_post_g", D, _zero), _ANY],
        [rowd, pl.BlockSpec((4, TM, MW), lambda i: (0, i, 0)), row4, row4, pl.BlockSpec((8, D), lambda i: (0, 0))],
        [SDS((S, D), CDT), SDS((4, S, MW), CDT), SDS((4, S, D), CDT), SDS((4, S, D), CDT), SDS((8, D), F32)],
        [pltpu.VMEM((4, MW, D), CDT)])(dxo, mo, g, yk, wb, pwo, sp, dep)


def _merge_bwd_w(ma, mb, mc, md, dyk, mg, dmo):
    S, D = dmo.shape
    TK = min(512, S)
    NK = S // TK

    def body(ma_ref, mb_ref, mc_ref, md_ref, dyk_ref, mg_ref, dmo_ref, gwb_ref, gwo_ref, accb, acco):
        k = pl.program_id(0)

        @pl.when(k == 0)
        def _():
            accb[...] = jnp.zeros_like(accb)
            acco[...] = jnp.zeros_like(acco)

        for b, m in enumerate((ma_ref, mb_ref, mc_ref, md_ref)):
            accb[b] += _tn(m[...], dyk_ref[b])
        acco[...] += _tn(mg_ref[...], dmo_ref[...])

        @pl.when(k == NK - 1)
        def _():
            for j in range(NS):
                for b in range(4):
                    gwb_ref[j, b] = accb[b, :, j * GW:(j + 1) * GW].astype(CDT)
                gwo_ref[j] = acco[j * GW:(j + 1) * GW, :].astype(CDT)

    rowm = pl.BlockSpec((TK, MW), lambda k: (k, 0))
    rowd = pl.BlockSpec((TK, D), lambda k: (k, 0))
    return _call(
        body, "merge_bwd_w", (NK,),
        [rowm, rowm, rowm, rowm, pl.BlockSpec((4, TK, D), lambda k: (0, k, 0)), rowd, rowd],
        [pl.BlockSpec((NS, 4, MW, GW), lambda k: (0, 0, 0, 0)), pl.BlockSpec((NS, GW, D), lambda k: (0, 0, 0))],
        [SDS((NS, 4, MW, GW), CDT), SDS((NS, GW, D), CDT)],
        [pltpu.VMEM((4, MW, D), F32), pltpu.VMEM((D, D), F32)])(ma, mb, mc, md, dyk, mg, dmo)


def _xa_kv(mem, sp, pxa):
    M, D = mem.shape

    def body(m_ref, g_ref, wk_ref, wv_ref, mn_ref, k_ref, v_ref):
        n, _ = _rms(m_ref[...])
        mn = (n * g_ref[...]).astype(CDT)
        mn_ref[...] = mn
        k_ref[...] = _nn(mn, wk_ref[...].reshape(D, D)).astype(CDT)
        v_ref[...] = _nn(mn, wv_ref[...].reshape(D, D)).astype(CDT)

    full = pl.BlockSpec((M, D), lambda i: (0, 0))
    return _call(
        body, "xa_kv", (1,),
        [full, _spspec("mem_g", D, _zero), pl.BlockSpec((NS, GW, D), lambda i: (0, 1, 0)),
         pl.BlockSpec((NS, GW, D), lambda i: (0, 2, 0))],
        [full, full, full], [SDS((M, D), CDT)] * 3)(mem, sp, pxa, pxa)


def _softmax(s):
    e = jnp.exp(s - jnp.max(s, axis=-1, keepdims=True))
    return e / jnp.sum(e, axis=-1, keepdims=True)


def _xa_fwd(x, kk, vv, sp, pxa):
    S, D = x.shape
    M = kk.shape[0]
    TM = min(512, S)
    HD = D // XA_HEADS
    scale = HD ** -0.5

    def body(x_ref, k_ref, v_ref, pg_ref, qg_ref, wq_ref, wo_ref, xo_ref, hb_ref, q_ref, o_ref, po_ref):
        n, _ = _rms(x_ref[...])
        hb = (n * pg_ref[...]).astype(CDT)
        hb_ref[...] = hb
        q = _nn(hb, wq_ref[...].reshape(D, D)).astype(CDT)
        q_ref[...] = q
        for h in range(XA_HEADS):
            hs = slice(h * HD, (h + 1) * HD)
            p = _softmax(_nt(q[:, hs], k_ref[:, hs]) * scale)
            o_ref[:, hs] = _nn(p.astype(CDT), v_ref[:, hs]).astype(CDT)
        po = _nn(o_ref[...], wo_ref[...].reshape(D, D))
        po_ref[...] = po.astype(CDT)
        n, _ = _rms(po)
        xo_ref[...] = x_ref[...] + n * qg_ref[...]

    row = pl.BlockSpec((TM, D), lambda i: (i, 0))
    full = pl.BlockSpec((M, D), lambda i: (0, 0))
    return _call(
        body, "xa_fwd", (S // TM,),
        [row, full, full, _spspec("xa_pre_g", D, _zero), _spspec("xa_post_g", D, _zero),
         pl.BlockSpec((NS, GW, D), lambda i: (0, 0, 0)), pl.BlockSpec((NS, GW, D), lambda i: (0, 3, 0))],
        [row] * 5, [SDS((S, D), F32)] + [SDS((S, D), CDT)] * 4)(x, kk, vv, sp, sp, pxa, pxa)


def _xa_bwd_act(dxo, x, po, q, kk, vv, sp, pxa, dep):
    S, D = x.shape
    M = kk.shape[0]
    TM = min(512, S)
    HD = D // XA_HEADS
    scale = HD ** -0.5

    def body(dxo_ref, x_ref, po_ref, q_ref, k_ref, v_ref, pg_ref, qg_ref, wq_ref, wo_ref, dep_ref,
             dx_ref, dpo_ref, dq_ref, dk_ref, dv_ref, gp_ref):
        @pl.when(pl.program_id(0) == 0)
        def _():
            gp_ref[...] = jnp.zeros_like(gp_ref)
            dk_ref[...] = jnp.zeros_like(dk_ref)
            dv_ref[...] = jnp.zeros_like(dv_ref)

        n, r = _rms(po_ref[...].astype(F32))
        dpo, dg = _rms_bwd(n, r, qg_ref[...], dxo_ref[...])
        gp_ref[1:2, :] += dg
        dpoc = dpo.astype(CDT)
        dpo_ref[...] = dpoc
        do = _nt(dpoc, wo_ref[...].reshape(D, D)).astype(CDT)
        for h in range(XA_HEADS):
            hs = slice(h * HD, (h + 1) * HD)
            qh = q_ref[:, hs]
            p = _softmax(_nt(qh, k_ref[:, hs]) * scale)
            pc = p.astype(CDT)
            dv_ref[:, hs] += _tn(pc, do[:, hs])
            dp = _nt(do[:, hs], v_ref[:, hs])
            ds = (p * (dp - jnp.sum(p * dp, axis=-1, keepdims=True)) * scale).astype(CDT)
            dq_ref[:, hs] = _nn(ds, k_ref[:, hs]).astype(CDT)
            dk_ref[:, hs] += _tn(ds, qh)
        dhb = _nt(dq_ref[...], wq_ref[...].reshape(D, D))
        n, r = _rms(x_ref[...])
        dx, dg = _rms_bwd(n, r, pg_ref[...], dhb)
        dx_ref[...] = dxo_ref[...] + dx
        gp_ref[0:1, :] += dg

    row = pl.BlockSpec((TM, D), lambda i: (i, 0))
    full = pl.BlockSpec((M, D), lambda i: (0, 0))
    return _call(
        body, "xa_bwd_act", (S // TM,),
        [row, row, row, row, full, full, _spspec("xa_pre_g", D, _zero), _spspec("xa_post_g", D, _zero),
         pl.BlockSpec((NS, GW, D), lambda i: (0, 0, 0)), pl.BlockSpec((NS, GW, D), lambda i: (0, 3, 0)), _ANY],
        [row, row, row, full, full, pl.BlockSpec((8, D), lambda i: (0, 0))],
        [SDS((S, D), F32), SDS((S, D), CDT), SDS((S, D), CDT), SDS((M, D), F32), SDS((M, D), F32), SDS((8, D), F32)],
    )(dxo, x, po, q, kk, vv, sp, sp, pxa, pxa, dep)


def _xa_bwd_w(hb, dq, o, dpo, mn, dk, dv):
    S, D = hb.shape
    M = mn.shape[0]
    TK = min(1024, S)
    NK = S // TK

    def body(hb_ref, dq_ref, o_ref, dpo_ref, mn_ref, dk_ref, dv_ref, g_ref, accq, acco):
        k = pl.program_id(0)

        @pl.when(k == 0)
        def _():
            accq[...] = jnp.zeros_like(accq)
            acco[...] = jnp.zeros_like(acco)

        accq[...] += _tn(hb_ref[...], dq_ref[...])
        acco[...] += _tn(o_ref[...], dpo_ref[...])

        @pl.when(k == NK - 1)
        def _():
            gk = _tn(mn_ref[...], dk_ref[...].astype(CDT))
            gv = _tn(mn_ref[...], dv_ref[...].astype(CDT))
            for j in range(NS):
                rs = slice(j * GW, (j + 1) * GW)
                g_ref[j, 0:GW, :] = accq[rs, :].astype(CDT)
                g_ref[j, GW:2 * GW, :] = gk[rs, :].astype(CDT)
                g_ref[j, 2 * GW:3 * GW, :] = gv[rs, :].astype(CDT)
                g_ref[j, 3 * GW:4 * GW, :] = acco[rs, :].astype(CDT)

    rowb = pl.BlockSpec((TK, D), lambda k: (k, 0))
    full = pl.BlockSpec((M, D), lambda k: (0, 0))
    return _call(
        body, "xa_bwd_w", (NK,),
        [rowb, rowb, rowb, rowb, full, full, full],
        pl.BlockSpec((NS, 4 * GW, D), lambda k: (0, 0, 0)), SDS((NS, 4 * GW, D), CDT),
        [pltpu.VMEM((D, D), F32), pltpu.VMEM((D, D), F32)])(hb, dq, o, dpo, mn, dk, dv)


def _xa_kv_bwd(mem, dk, dv, sp, pxa):
    M, D = mem.shape

    def body(m_ref, dk_ref, dv_ref, wk_ref, wv_ref, gp_ref):
        dmn = _nt(dk_ref[...].astype(CDT), wk_ref[...].reshape(D, D)) + _nt(dv_ref[...].astype(CDT), wv_ref[...].reshape(D, D))
        n, _ = _rms(m_ref[...])
        gp_ref[...] = jnp.zeros_like(gp_ref)
        gp_ref[0:1, :] = jnp.sum(dmn * n, axis=0, keepdims=True)

    full = pl.BlockSpec((M, D), lambda i: (0, 0))
    return _call(
        body, "xa_kv_bwd", (1,),
        [full, full, full, pl.BlockSpec((NS, GW, D), lambda i: (0, 1, 0)), pl.BlockSpec((NS, GW, D), lambda i: (0, 2, 0))],
        pl.BlockSpec((8, D), lambda i: (0, 0)), SDS((8, D), F32))(mem, dk, dv, pxa, pxa)


def _loss_head(y, t):
    S, D = y.shape
    TM = min(512, S)

    def body(y_ref, t_ref, dy_ref, l_ref):
        @pl.when(pl.program_id(0) == 0)
        def _():
            l_ref[...] = jnp.zeros_like(l_ref)

        e = y_ref[...] - t_ref[...]
        dy_ref[...] = e * (1.0 / D)
        l_ref[...] += 0.5 * jnp.sum(jnp.mean(e * e, axis=-1, keepdims=True), axis=0, keepdims=True)

    row = pl.BlockSpec((TM, D), lambda i: (i, 0))
    return _call(body, "loss_head", (S // TM,), [row, row], [row, pl.BlockSpec((8, 128), lambda i: (0, 0))],
                 [SDS((S, D), F32), SDS((8, 128), F32)])(y, t)


def _row_tile(rows, cols, limit=1 << 18, step=8):
    if rows * cols <= limit or rows % step:
        return rows
    best = step
    for t in range(step, rows + 1, step):
        if rows % t == 0 and t * cols <= limit:
            best = t
    return best


def _adamw(w, g, m, v):
    shape = w.shape
    C = shape[-1]
    R = w.size // C
    TR = _row_tile(R, C)
    c1 = 1.0 - ADAM_B1 ** ADAM_STEP
    c2 = 1.0 - ADAM_B2 ** ADAM_STEP

    def body(w_ref, g_ref, m_ref, v_ref, d_ref, nm_ref, nv_ref):
        gg = g_ref[...]
        nm = ADAM_B1 * m_ref[...] + (1.0 - ADAM_B1) * gg
        nv = ADAM_B2 * v_ref[...] + (1.0 - ADAM_B2) * (gg * gg)
        nm_ref[...] = nm
        nv_ref[...] = nv
        d_ref[...] = -ADAM_LR * ((nm / c1) / (jnp.sqrt(nv / c2) + ADAM_EPS) + ADAM_WD * w_ref[...])

    blk = pl.BlockSpec((TR, C), lambda i: (i, 0))
    outs = _call(body, "adamw", (R // TR,), [blk] * 4, [blk] * 3, [SDS((R, C), F32)] * 3)(
        w.reshape(R, C), g.reshape(R, C), m.reshape(R, C), v.reshape(R, C))
    return tuple(o.reshape(shape) for o in outs)


def _adamw_block(w, gs, m, v, gblock):
    L, R, C = w.shape
    c1 = 1.0 - ADAM_B1 ** ADAM_STEP
    c2 = 1.0 - ADAM_B2 ** ADAM_STEP

    def body(w_ref, g_ref, m_ref, v_ref, go_ref, d_ref, nm_ref, nv_ref):
        gg = g_ref[...]
        go_ref[...] = gg
        nm = ADAM_B1 * m_ref[...] + (1.0 - ADAM_B1) * gg
        nv = ADAM_B2 * v_ref[...] + (1.0 - ADAM_B2) * (gg * gg)
        nm_ref[...] = nm
        nv_ref[...] = nv
        d_ref[...] = -ADAM_LR * ((nm / c1) / (jnp.sqrt(nv / c2) + ADAM_EPS) + ADAM_WD * w_ref[...])

    blk = pl.BlockSpec((None, R, C), lambda l: (l, 0, 0))
    return _call(body, "adamw_block", (L,), [blk, pl.BlockSpec((None, R, C), lambda l: (l, gblock, 0)), blk, blk],
                 [blk] * 4, [SDS((L, R, C), F32)] * 4)(w, gs, m, v)


def _slot_sum_into(stacked, r, l):
    _, R, C = r.shape
    TR = _row_tile(R, C * NS, limit=1 << 21, step=16)

    def body(r_ref, s_ref, o_ref):
        acc = r_ref[0].astype(F32)
        for j in range(1, NS):
            acc = acc + r_ref[j].astype(F32)
        o_ref[...] = acc

    return pl.pallas_call(
        body, name="slot_sum_into", grid=(R // TR,),
        in_specs=[pl.BlockSpec((NS, TR, C), lambda i: (0, i, 0)), _ANY],
        out_specs=pl.BlockSpec((None, TR, C), lambda i: (l, i, 0)), out_shape=SDS(stacked.shape, F32),
        input_output_aliases={1: 0},
        compiler_params=pltpu.CompilerParams(dimension_semantics=("arbitrary",), vmem_limit_bytes=VMEM_LIMIT))(r, stacked)


def _exchange(arrs, scatter, name):
    n = len(arrs)
    np_ = NS - 1

    def body(*refs):
        ins, outs = refs[:n], refs[n:2 * n]
        send_sems, recv_sems, loc_sems = refs[2 * n:]
        x, y, c = lax.axis_index("x"), lax.axis_index("y"), lax.axis_index("c")
        me = 4 * x + 2 * y + c
        peers = []
        for f in range(1, NS):
            px = 1 - x if f & 4 else x
            py = 1 - y if f & 2 else y
            pc = 1 - c if f & 1 else c
            peers.append(((px, py, pc), 4 * px + 2 * py + pc))

        def src(a, pid):
            return ins[a].at[pid] if scatter else ins[a]

        local = [pltpu.make_async_copy(src(a, me), outs[a].at[me], loc_sems.at[a]) for a in range(n)]
        for cp in local:
            cp.start()
        sends = []
        for a in range(n):
            for f, (dev, pid) in enumerate(peers):
                sends.append(pltpu.make_async_remote_copy(
                    src_ref=src(a, pid), dst_ref=outs[a].at[me], send_sem=send_sems.at[a * np_ + f],
                    recv_sem=recv_sems.at[a * np_ + f], device_id=dev, device_id_type=pl.DeviceIdType.MESH))
        for cp in sends:
            cp.start()
        for a in range(n):
            for f, (dev, pid) in enumerate(peers):
                pltpu.make_async_remote_copy(
                    src_ref=src(a, pid), dst_ref=outs[a].at[pid], send_sem=send_sems.at[a * np_ + f],
                    recv_sem=recv_sems.at[a * np_ + f], device_id=dev, device_id_type=pl.DeviceIdType.MESH).wait_recv()
        for cp in sends:
            cp.wait_send()
        for cp in local:
            cp.wait()

    out_shape = [SDS(a.shape if scatter else (NS,) + a.shape, a.dtype) for a in arrs]
    anyspec = pl.BlockSpec(memory_space=pl.ANY)
    outs = pl.pallas_call(
        body, name=name, in_specs=[anyspec] * n, out_specs=[anyspec] * n, out_shape=out_shape,
        scratch_shapes=[pltpu.SemaphoreType.DMA((n * np_,)), pltpu.SemaphoreType.DMA((n * np_,)),
                        pltpu.SemaphoreType.DMA((n,))],
        compiler_params=pltpu.CompilerParams(has_side_effects=True))(*arrs)
    return list(outs)


def _peers():
    x, y, c = lax.axis_index("x"), lax.axis_index("y"), lax.axis_index("c")
    out = []
    for f in range(1, NS):
        px = 1 - x if f & 4 else x
        py = 1 - y if f & 2 else y
        pc = 1 - c if f & 1 else c
        out.append(((px, py, pc), 4 * px + 2 * py + pc))
    return 4 * x + 2 * y + c, out


def _exchange_copies(ins, lands, scatter, send_sems, recv_sems, loc_sems):
    me, peers = _peers()
    np_ = NS - 1

    def src(a, pid):
        return ins[a].at[pid] if scatter[a] else ins[a]

    def rcopy(a, f, dev, land_slot):
        return pltpu.make_async_remote_copy(
            src_ref=src(a, peers[f][1]), dst_ref=lands[a].at[land_slot], send_sem=send_sems.at[a * np_ + f],
            recv_sem=recv_sems.at[a * np_ + f], device_id=dev, device_id_type=pl.DeviceIdType.MESH)

    local = [pltpu.make_async_copy(src(a, me), lands[a].at[me], loc_sems.at[a]) for a in range(len(ins))]
    sends = [rcopy(a, f, dev, me) for a in range(len(ins)) for f, (dev, _) in enumerate(peers)]
    arrivals = [rcopy(a, f, dev, pid) for a in range(len(ins)) for f, (dev, pid) in enumerate(peers)]
    return local, sends, arrivals


_HBM = pl.BlockSpec(memory_space=pltpu.HBM)
_SEM = pl.BlockSpec(memory_space=pltpu.SEMAPHORE)
_ANY = pl.BlockSpec(memory_space=pl.ANY)


def _exchange_start(arrs, scatter, after, name):
    n = len(arrs)
    np_ = NS - 1
    lands = [lax.empty(a.shape if sc else (NS,) + a.shape, a.dtype) for a, sc in zip(arrs, scatter)]

    def body(*refs):
        ins, lnd = refs[:n], refs[n:2 * n]
        send_sems, recv_sems, loc_sems = refs[2 * n + 1:2 * n + 4]
        token = refs[-1]
        local, sends, _ = _exchange_copies(ins, lnd, scatter, send_sems, recv_sems, loc_sems)
        for cp in local + sends:
            cp.start()
        token[...] = jnp.zeros_like(token)

    hbm = lambda a: pltpu.HBM(a.shape, a.dtype)
    outs = pl.pallas_call(
        body, name=name,
        out_shape=(pltpu.SemaphoreType.DMA((n * np_,)), pltpu.SemaphoreType.DMA((n * np_,)), pltpu.SemaphoreType.DMA((n,)),
                   *[hbm(a) for a in arrs], *[hbm(a) for a in lands], SDS((8, 128), F32)),
        in_specs=[_HBM] * (2 * n) + [_ANY],
        out_specs=(_SEM, _SEM, _SEM, *([_HBM] * (2 * n)), pl.BlockSpec(memory_space=pltpu.VMEM)),
        input_output_aliases={i: 3 + i for i in range(2 * n)},
        compiler_params=pltpu.CompilerParams(has_side_effects=pltpu.SideEffectType.DATAFLOW_SIDE_EFFECTING),
    )(*[pltpu.with_memory_space_constraint(a, pltpu.HBM) for a in list(arrs) + lands], after)
    return {"sems": outs[:3], "ins": outs[3:3 + n], "lands": outs[3 + n:3 + 2 * n], "token": outs[-1], "scatter": scatter}


def _exchange_wait(h, after, name):
    n = len(h["ins"])
    scatter = h["scatter"]
    after = list(after) if isinstance(after, (list, tuple)) else [after]

    def body(*refs):
        ins, lnd = refs[:n], refs[n:2 * n]
        send_sems, recv_sems, loc_sems = refs[2 * n:2 * n + 3]
        local, sends, arrivals = _exchange_copies(ins, lnd, scatter, send_sems, recv_sems, loc_sems)
        for cp in sends:
            cp.wait_send()
        for cp in arrivals:
            cp.wait_recv()
        for cp in local:
            cp.wait()

    hbm = lambda a: pltpu.HBM(a.shape, a.dtype)
    outs = pl.pallas_call(
        body, name=name,
        out_shape=tuple(hbm(a) for a in list(h["ins"]) + list(h["lands"])),
        in_specs=[_HBM] * (2 * n) + [_SEM] * 3 + [_ANY] * len(after),
        out_specs=tuple([_HBM] * (2 * n)),
        input_output_aliases={i: i for i in range(2 * n)},
        compiler_params=pltpu.CompilerParams(has_side_effects=pltpu.SideEffectType.DATAFLOW_SIDE_EFFECTING),
    )(*h["ins"], *h["lands"], *h["sems"], *after)
    return list(outs[n:])


def _hgather_copies(ins, lands, send_a, recv_a, send_b, recv_b, loc_sems):
    x, y, c = lax.axis_index("x"), lax.axis_index("y"), lax.axis_index("c")
    me = 4 * x + 2 * y + c
    sib = (x, y, 1 - c)
    chips = [(1 - x, y), (x, 1 - y), (1 - x, 1 - y)]
    slot = lambda px, py, pc: 4 * px + 2 * py + pc

    def rcopy(src, dst, ssem, rsem, dev):
        return pltpu.make_async_remote_copy(src_ref=src, dst_ref=dst, send_sem=ssem, recv_sem=rsem, device_id=dev,
                                            device_id_type=pl.DeviceIdType.MESH)

    local, s1, a1, s2, a2 = [], [], [], [], []
    for a in range(len(ins)):
        first = [(sib, slot(x, y, 1 - c))] + [((px, py, c), slot(px, py, c)) for px, py in chips]
        for k, (dev, origin) in enumerate(first if send_a is not None else ()):
            s1.append(rcopy(ins[a], lands[a].at[me], send_a.at[4 * a + k], recv_a.at[4 * a + k], dev))
            a1.append(rcopy(ins[a], lands[a].at[origin], send_a.at[4 * a + k], recv_a.at[4 * a + k], dev))
        if send_a is not None:
            local.append(pltpu.make_async_copy(ins[a], lands[a].at[me], loc_sems.at[a]))
        for k, (px, py) in enumerate(chips if send_b is not None else ()):
            mine, theirs = lands[a].at[slot(px, py, c)], lands[a].at[slot(px, py, 1 - c)]
            s2.append(rcopy(mine, mine, send_b.at[3 * a + k], recv_b.at[3 * a + k], sib))
            a2.append(rcopy(mine, theirs, send_b.at[3 * a + k], recv_b.at[3 * a + k], sib))
    return local, s1, a1, s2, a2


def _hgather_start(arrs, after, name):
    n = len(arrs)
    lands = [lax.empty((NS,) + a.shape, a.dtype) for a in arrs]

    def body(*refs):
        ins, lnd = refs[:n], refs[n:2 * n]
        send_a, recv_a, loc_sems = refs[2 * n + 1:2 * n + 4]
        token = refs[-1]
        local, s1, _, _, _ = _hgather_copies(ins, lnd, send_a, recv_a, None, None, loc_sems)
        for cp in local + s1:
            cp.start()
        token[...] = jnp.zeros_like(token)

    hbm = lambda a: pltpu.HBM(a.shape, a.dtype)
    outs = pl.pallas_call(
        body, name=name,
        out_shape=(pltpu.SemaphoreType.DMA((4 * n,)), pltpu.SemaphoreType.DMA((4 * n,)), pltpu.SemaphoreType.DMA((n,)),
                   *[hbm(a) for a in arrs], *[hbm(a) for a in lands], SDS((8, 128), F32)),
        in_specs=[_HBM] * (2 * n) + [_ANY],
        out_specs=(_SEM, _SEM, _SEM, *([_HBM] * (2 * n)), pl.BlockSpec(memory_space=pltpu.VMEM)),
        input_output_aliases={i: 3 + i for i in range(2 * n)},
        compiler_params=pltpu.CompilerParams(has_side_effects=pltpu.SideEffectType.DATAFLOW_SIDE_EFFECTING),
    )(*[pltpu.with_memory_space_constraint(a, pltpu.HBM) for a in list(arrs) + lands], after)
    return {"sems": outs[:3], "ins": outs[3:3 + n], "lands": outs[3 + n:3 + 2 * n], "token": outs[-1]}


def _hgather_forward(h, after, name):
    n = len(h["ins"])
    after = list(after) if isinstance(after, (list, tuple)) else [after]
    na = len(after)

    def body(*refs):
        ins, lnd = refs[:n], refs[n:2 * n]
        send_a, recv_a, loc_sems = refs[2 * n:2 * n + 3]
        send_b, recv_b = refs[2 * n + 3 + na:2 * n + 5 + na]
        token = refs[-1]
        local, s1, a1, s2, _ = _hgather_copies(ins, lnd, send_a, recv_a, send_b, recv_b, loc_sems)
        for cp in s1:
            cp.wait_send()
        for cp in a1:
            cp.wait_recv()
        for cp in local:
            cp.wait()
        for cp in s2:
            cp.start()
        token[...] = jnp.zeros_like(token)

    hbm = lambda a: pltpu.HBM(a.shape, a.dtype)
    outs = pl.pallas_call(
        body, name=name,
        out_shape=(pltpu.SemaphoreType.DMA((3 * n,)), pltpu.SemaphoreType.DMA((3 * n,)),
                   *[hbm(a) for a in list(h["ins"]) + list(h["lands"])], SDS((8, 128), F32)),
        in_specs=[_HBM] * (2 * n) + [_SEM] * 3 + [_ANY] * na,
        out_specs=(_SEM, _SEM, *([_HBM] * (2 * n)), pl.BlockSpec(memory_space=pltpu.VMEM)),
        input_output_aliases={i: 2 + i for i in range(2 * n)},
        compiler_params=pltpu.CompilerParams(has_side_effects=pltpu.SideEffectType.DATAFLOW_SIDE_EFFECTING),
    )(*h["ins"], *h["lands"], *h["sems"], *after)
    return {"sems": outs[:2], "ins": outs[2:2 + n], "lands": outs[2 + n:2 + 2 * n], "token": outs[-1]}


def _hgather_wait(h, after, name):
    n = len(h["ins"])

    def body(*refs):
        ins, lnd = refs[:n], refs[n:2 * n]
        send_b, recv_b = refs[2 * n:2 * n + 2]
        _, _, _, s2, a2 = _hgather_copies(ins, lnd, None, None, send_b, recv_b, None)
        for cp in s2:
            cp.wait_send()
        for cp in a2:
            cp.wait_recv()

    hbm = lambda a: pltpu.HBM(a.shape, a.dtype)
    outs = pl.pallas_call(
        body, name=name,
        out_shape=tuple(hbm(a) for a in list(h["ins"]) + list(h["lands"])),
        in_specs=[_HBM] * (2 * n) + [_SEM] * 2 + [_ANY],
        out_specs=tuple([_HBM] * (2 * n)),
        input_output_aliases={i: i for i in range(2 * n)},
        compiler_params=pltpu.CompilerParams(has_side_effects=pltpu.SideEffectType.DATAFLOW_SIDE_EFFECTING),
    )(*h["ins"], *h["lands"], *h["sems"], after)
    return list(outs[n:])


_W_NAMES = ("ffn1_pre_g", "ffn1_post_g", "ffn1_w1", "ffn1_w3", "ffn1_w2", "mix_pre_g", "mix_post_g", "w_in", "conv_a_w",
            "conv_a_b", "pool_w", "pool_scale", "sgu_ln_g", "sgu_ln_b", "sgu_ws", "sgu_b", "conv_d_w", "conv_d_b",
            "conv_d_ln_g", "conv_d_ln_b", "w_branch", "w_gate", "b_gate", "w_o", "xa_pre_g", "xa_post_g", "mem_g",
            "xa_wq", "xa_wk", "xa_wv", "xa_wo", "ffn2_pre_g", "ffn2_post_g", "ffn2_w1", "ffn2_w3", "ffn2_w2")
_REP_NAMES = tuple(n for n, _ in _SP_NAMES) + ("pool_w", "sgu_ws", "sgu_b", "conv_a_w", "conv_d_w")


def _t(w):
    return jnp.swapaxes(w, -1, -2)


def _step(x, mem, loss_target, W, M, V):
    L = W["w_in"].shape[0]
    S, D = x.shape[1], x.shape[2]
    x0 = x.reshape(S, D)
    memf = mem.reshape(mem.shape[1], D)
    me = 4 * lax.axis_index("x") + 2 * lax.axis_index("y") + lax.axis_index("c")
    FS = W["ffn1_w2"].shape[1]
    KA, KD = W["conv_a_w"].shape[1], W["conv_d_w"].shape[1]
    CS = W["conv_a_w"].shape[2]

    cat = lambda l, parts: jnp.concatenate([(_t(W[n][l]) if tr else W[n][l]) for n, tr in parts], axis=0).astype(CDT)
    pf1 = [cat(l, (("ffn1_w1", 1), ("ffn1_w3", 1), ("ffn1_w2", 0))) for l in range(L)]
    pf2 = [cat(l, (("ffn2_w1", 1), ("ffn2_w3", 1), ("ffn2_w2", 0))) for l in range(L)]
    pma = [cat(l, (("w_in", 1), ("w_gate", 1))) for l in range(L)]
    pwo = [W["w_o"][l].astype(CDT) for l in range(L)]
    pxa = [cat(l, (("xa_wq", 0), ("xa_wk", 0), ("xa_wv", 0), ("xa_wo", 0))) for l in range(L)]
    wbs = [W["w_branch"][l].astype(CDT) for l in range(L)]
    cws = jnp.concatenate([W["conv_a_w"], W["conv_d_w"]], axis=1).reshape(-1, 128)
    sp_all = jnp.concatenate([W[n] for n, _ in _SP_NAMES], axis=1)
    bsc_all = W["sgu_b"][..., None]

    (cwg,) = _exchange([cws], False, "gather_conv_w")
    cwf = cwg.reshape(NS, L, KA + KD, CS).transpose(1, 2, 0, 3).reshape(L, KA + KD, NS * CS)

    def gather_start(l, after):
        return _hgather_start([pf1[l], pf2[l], pma[l], pwo[l], pxa[l], wbs[l]], after, f"gather_start_{l}")

    def gather_rest(h, after, tag):
        mid = _hgather_forward(h, after, f"gather_forward_{tag}")
        return _hgather_wait(mid, mid["token"], f"gather_wait_{tag}")

    packs = [None] * L
    first = [_hgather_start([pf1[0]], cwg, "gather_start_0a")]
    saved = []
    xc = x0
    for l in range(L):
        if l == 0:
            rest = [p[k] for p in (pf1, pf2, pma, pwo, pxa, wbs) for k in range(L) if not (p is pf1 and k == 0)]
            (gf1,) = gather_rest(first[0], [sp_all, bsc_all] + rest, "0a")
            first.append(_hgather_start([pma[0], pwo[0], wbs[0]], gf1, "gather_start_0b"))
        else:
            gf1, gf2, gma, gwo, gxa, gwb = packs[l]
        sp = sp_all[l:l + 1]
        cwa, cwd = cwf[l, :KA], cwf[l, KA:]
        wp, ws, bsc = W["pool_w"][l], W["sgu_ws"][l], bsc_all[l]
        s = {"x0": xc}
        nxt = gather_start(l + 1, gf1) if 0 < l < L - 1 else None
        xc, s["hb1"], s["a1"], s["b1"], s["y1"] = _ffn_fwd(xc, sp, "ffn1_pre_g", "ffn1_post_g", gf1,
                                                            (first[1] if l == 0 else nxt)["token"] if l == 0 or nxt else sp)
        s["x1"] = xc
        if l == 0:
            gma, gwo, gwb = gather_rest(first[1], xc, "0b")
            first.append(_hgather_start([pxa[0]], gma, "gather_start_0c"))
            first.append(_hgather_start([pf2[0]], first[2]["token"], "gather_start_0d"))
            nxt = gather_start(1, first[3]["token"]) if L > 1 else None
        s["hbm"], s["z"], s["g"] = _mix_in(xc, sp, gma, (nxt or first[3])["token"] if l == 0 else sp)
        s["ma"] = _mixA_fwd(s["z"], cwa, sp)
        s["mb"] = _mixB_fwd(s["z"], wp, sp)
        s["mc"] = _mixC_fwd(s["z"], ws, bsc, sp)
        s["yd"] = _mixD_conv_fwd(s["z"], cwd, sp)
        xc, s["md"], s["yk"], s["mg"], s["mo"] = _merge_fwd(s["ma"], s["mb"], s["mc"], s["yd"], s["g"], gwb, gwo, xc, sp)
        s["x2"] = xc
        if l == 0:
            (gxa,) = gather_rest(first[2], xc, "0c")
        s["mn"], s["k"], s["v"] = _xa_kv(memf, sp, gxa)
        xc, s["hbx"], s["q"], s["o"], s["po"] = _xa_fwd(xc, s["k"], s["v"], sp, gxa)
        s["x3"] = xc
        if l == 0:
            (gf2,) = gather_rest(first[3], xc, "0d")
            packs[0] = (gf1, gf2, gma, gwo, gxa, gwb)
        mid = _hgather_forward(nxt, xc, f"gather_forward_{l + 1}") if nxt and l > 0 else None
        xc, s["hb2"], s["a2"], s["b2"], s["y2"] = _ffn_fwd(xc, sp, "ffn2_pre_g", "ffn2_post_g", gf2,
                                                            mid["token"] if mid else sp)
        saved.append(s)
        if nxt:
            mid = mid or _hgather_forward(nxt, xc, f"gather_forward_{l + 1}")
            packs[l + 1] = _hgather_wait(mid, xc, f"gather_wait_{l + 1}")

    dx, lpart = _loss_head(xc, loss_target.reshape(S, D))
    loss = lax.psum(lpart[0, 0], ("x", "y", "c"))

    rep = {n: [None] * L for n in _REP_NAMES}
    summed = [dict() for _ in range(L)]
    pending = None
    last = []
    for l in reversed(range(L)):
        gf1, gf2, gma, gwo, gxa, gwb = packs[l]
        sp = sp_all[l:l + 1]
        cwa, cwd = cwf[l, :KA], cwf[l, KA:]
        wp, ws, bsc = W["pool_w"][l], W["sgu_ws"][l], bsc_all[l]
        s = saved[l]

        dx, dyb, da, db, gp = _ffn_bwd_act(dx, s["x3"], s["y2"], s["a2"], s["b2"], sp, "ffn2_pre_g", "ffn2_post_g", gf2,
                                           pending[1]["token"] if pending else sp)
        rep["ffn2_pre_g"][l], rep["ffn2_post_g"][l] = gp[0], gp[1]
        d_f2 = _ffn_bwd_w(s["hb2"], dyb, s["a2"], s["b2"], da, db, sp)
        if l == 0:
            last.append(_exchange_start([d_f2], (True,), dx, "scatter_start_0a"))

        dx, dpo, dq, dk, dv, gp = _xa_bwd_act(dx, s["x2"], s["po"], s["q"], s["k"], s["v"], sp, gxa,
                                              last[-1]["token"] if last else sp)
        rep["xa_pre_g"][l], rep["xa_post_g"][l] = gp[0], gp[1]
        d_xa = _xa_bwd_w(s["hbx"], dq, s["o"], dpo, s["mn"], dk, dv)
        rep["mem_g"][l] = _xa_kv_bwd(memf, dk, dv, sp, gxa)[0]
        if l == 0:
            last.append(_exchange_start([d_xa], (True,), dx, "scatter_start_0b"))

        dmo, dm, dgp, dyk, gp = _merge_bwd_act(dx, s["mo"], s["g"], s["yk"], gwb, gwo, sp, last[-1]["token"] if last else sp)
        rep["mix_post_g"][l] = gp[0]
        d_wb, d_wo = _merge_bwd_w(s["ma"], s["mb"], s["mc"], s["md"], dyk, s["mg"], dmo)
        dz, dcw, gp = _mixA_bwd(s["z"], dm, cwa, sp)
        rep["conv_a_w"][l], rep["conv_a_b"][l] = dcw, gp[0]
        dz, dwp, gp = _mixB_bwd(s["z"], dm, wp, sp, dz)
        rep["pool_w"][l], rep["pool_scale"][l] = dwp, gp[0]
        dz, dws, dbs, gp = _mixC_bwd(s["z"], dm, ws, bsc, sp, dz)
        rep["sgu_ws"][l], rep["sgu_b"][l], rep["sgu_ln_g"][l], rep["sgu_ln_b"][l] = dws, dbs[:, :, 0], gp[0], gp[1]
        dyd, gp = _mixD_ln_bwd(dm, s["yd"], sp)
        rep["conv_d_ln_g"][l], rep["conv_d_ln_b"][l] = gp[0], gp[1]
        dz, dcw, gp = _mixD_conv_bwd(s["z"], dyd, cwd, dz)
        rep["conv_d_w"][l], rep["conv_d_b"][l] = dcw, gp[0]
        dx, gp = _mix_in_bwd_act(dz, dgp, dx, s["x1"], sp, gma)
        rep["mix_pre_g"][l] = gp[0]
        d_ma, dbg = _mix_in_bwd_w(dz, dgp, s["hbm"])
        rep["b_gate"][l] = dbg[:, 0, :].reshape(-1)
        if l == 0:
            last.append(_exchange_start([d_ma, d_wo, d_wb], (True,) * 3, dx, "scatter_start_0c"))

        dx, dyb, da, db, gp = _ffn_bwd_act(dx, s["x0"], s["y1"], s["a1"], s["b1"], sp, "ffn1_pre_g", "ffn1_post_g", gf1,
                                           last[-1]["token"] if last else sp)
        rep["ffn1_pre_g"][l], rep["ffn1_post_g"][l] = gp[0], gp[1]
        flat = jnp.concatenate([rep[n][l].reshape(-1) for n in _REP_NAMES])
        flat = jnp.pad(flat, (0, -flat.size % 2048)).reshape(-1, 128).astype(CDT)
        if l == 0:
            last.append(_exchange_start([flat], (False,), dx, "scatter_start_0d"))
        d_f1 = _ffn_bwd_w(s["hb1"], dyb, s["a1"], s["b1"], da, db, last[-1]["token"] if last else sp)

        if pending:
            r = _exchange_wait(pending[1], dx, f"scatter_wait_{pending[0]}")
            summed[pending[0]] = dict(zip(("f1", "f2", "ma", "wo", "xa", "wb", "flat"), r))
        if l == 0:
            last.append(_exchange_start([d_f1], (True,), dx, "scatter_start_0e"))
        else:
            pending = (l, _exchange_start([d_f1, d_f2, d_ma, d_wo, d_xa, d_wb, flat], (True,) * 6 + (False,), dx,
                                          f"scatter_start_{l}"))

    pack_shape = {"f1": (3 * FS, D), "f2": (3 * FS, D), "ma": (2 * MW, D), "wo": (GW, D), "xa": (4 * GW, D),
                  "wb": (4 * MW, GW), "flat": tuple(flat.shape)}
    stk = {k: lax.empty((L,) + s, F32) for k, s in pack_shape.items()}

    def land(k, r, l):
        stk[k] = _slot_sum_into(stk[k], r.reshape((NS,) + pack_shape[k]), l)

    for l in range(1, L):
        for k, r in summed[l].items():
            land(k, r, l)
    (r,) = _exchange_wait(last[0], dx, "scatter_wait_0a")
    land("f2", r, 0)
    (r,) = _exchange_wait(last[1], dx, "scatter_wait_0b")
    land("xa", r, 0)

    G, deltas, new_m, new_v = {}, {}, {}, {}

    def update_block(n, k, blk, transposed):
        tr = _t if transposed else (lambda a: a)
        out = _adamw_block(tr(W[n]), stk[k], tr(M[n]), tr(V[n]), blk)
        G[n], deltas[n], new_m[n], new_v[n] = (tr(a) for a in out)
        return deltas[n]

    def update(n):
        deltas[n], new_m[n], new_v[n] = _adamw(W[n], G[n], M[n], V[n])
        return deltas[n]

    done = [update_block("ffn2_w1", "f2", 0, True), update_block("ffn2_w3", "f2", 1, True),
            update_block("ffn2_w2", "f2", 2, False)]
    done += [update_block(n, "xa", i, False) for i, n in enumerate(("xa_wq", "xa_wk", "xa_wv", "xa_wo"))]
    r = _exchange_wait(last[2], done + [stk[k] for k in ("f1", "ma", "wo", "wb", "flat")], "scatter_wait_0c")
    for k, v in zip(("ma", "wo", "wb"), r):
        land(k, v, 0)
    G["w_in"], G["w_gate"] = _t(stk["ma"][:, :MW]), _t(stk["ma"][:, MW:])
    G["w_branch"] = stk["wb"].reshape(W["w_branch"].shape)
    done = [update("w_in"), update("w_gate"), update("w_branch"), update_block("w_o", "wo", 0, False)]
    (r,) = _exchange_wait(last[3], done, "scatter_wait_0d")
    land("flat", r, 0)

    tot = [stk["flat"][l].reshape(-1) for l in range(L)]
    off = 0
    for n in _REP_NAMES:
        shape = (KA, NS * CS) if n == "conv_a_w" else (KD, NS * CS) if n == "conv_d_w" else W[n].shape[1:]
        size = 1
        for d in shape:
            size *= d
        G[n] = jnp.stack([tot[l][off:off + size].reshape(shape) for l in range(L)])
        off += size
    for n in ("conv_a_w", "conv_d_w"):
        G[n] = lax.dynamic_slice_in_dim(G[n], me * CS, CS, axis=2)
    done = [update(n) for n in _REP_NAMES]

    (r,) = _exchange_wait(last[4], done, "scatter_wait_0e")
    land("f1", r, 0)
    update_block("ffn1_w1", "f1", 0, True)
    update_block("ffn1_w3", "f1", 1, True)
    update_block("ffn1_w2", "f1", 2, False)
    grad_x = dx.reshape(x.shape)
    return (loss, grad_x, *[G[n] for n in _W_NAMES], *[deltas[n] for n in _W_NAMES],
            *[new_m[n] for n in _W_NAMES], *[new_v[n] for n in _W_NAMES])


def kernel(x, mem, ffn1_pre_g, ffn1_post_g, ffn1_w1, ffn1_w3, ffn1_w2, mix_pre_g, mix_post_g, w_in, conv_a_w, conv_a_b, pool_w, pool_scale, sgu_ln_g, sgu_ln_b, sgu_ws, sgu_b, conv_d_w, conv_d_b, conv_d_ln_g, conv_d_ln_b, w_branch, w_gate, b_gate, w_o, xa_pre_g, xa_post_g, mem_g, xa_wq, xa_wk, xa_wv, xa_wo, ffn2_pre_g, ffn2_post_g, ffn2_w1, ffn2_w3, ffn2_w2, loss_target, m_ffn1_pre_g, m_ffn1_post_g, m_ffn1_w1, m_ffn1_w3, m_ffn1_w2, m_mix_pre_g, m_mix_post_g, m_w_in, m_conv_a_w, m_conv_a_b, m_pool_w, m_pool_scale, m_sgu_ln_g, m_sgu_ln_b, m_sgu_ws, m_sgu_b, m_conv_d_w, m_conv_d_b, m_conv_d_ln_g, m_conv_d_ln_b, m_w_branch, m_w_gate, m_b_gate, m_w_o, m_xa_pre_g, m_xa_post_g, m_mem_g, m_xa_wq, m_xa_wk, m_xa_wv, m_xa_wo, m_ffn2_pre_g, m_ffn2_post_g, m_ffn2_w1, m_ffn2_w3, m_ffn2_w2, v_ffn1_pre_g, v_ffn1_post_g, v_ffn1_w1, v_ffn1_w3, v_ffn1_w2, v_mix_pre_g, v_mix_post_g, v_w_in, v_conv_a_w, v_conv_a_b, v_pool_w, v_pool_scale, v_sgu_ln_g, v_sgu_ln_b, v_sgu_ws, v_sgu_b, v_conv_d_w, v_conv_d_b, v_conv_d_ln_g, v_conv_d_ln_b, v_w_branch, v_w_gate, v_b_gate, v_w_o, v_xa_pre_g, v_xa_post_g, v_mem_g, v_xa_wq, v_xa_wk, v_xa_wv, v_xa_wo, v_ffn2_pre_g, v_ffn2_post_g, v_ffn2_w1, v_ffn2_w3, v_ffn2_w2):
    args = dict(locals())
    W = {n: args[n] for n in _W_NAMES}
    M = {n: args["m_" + n] for n in _W_NAMES}
    V = {n: args["v_" + n] for n in _W_NAMES}
    return _step(x, mem, loss_target, W, M, V)
```

```python
import jax
import jax.numpy as jnp
from jax import lax
from jax.experimental import pallas as pl
from jax.experimental.pallas import tpu as pltpu

F32 = jnp.float32
CDT = jnp.bfloat16
EPS = 1e-6
NS = 8
GW = 128
MW = 512
CHUNK = 64
XA_HEADS = 4
POOL_WINDOWS = (2, 4, 8, 16)
VMEM_LIMIT = 56 * 1024 * 1024
ADAM_LR, ADAM_B1, ADAM_B2, ADAM_EPS, ADAM_WD, ADAM_STEP = 0.001, 0.9, 0.999, 1e-08, 0.01, 10

SDS = jax.ShapeDtypeStruct

_SP_NAMES = (("ffn1_pre_g", 1024), ("ffn1_post_g", 1024), ("mix_pre_g", 1024), ("mix_post_g", 1024),
             ("xa_pre_g", 1024), ("xa_post_g", 1024), ("mem_g", 1024), ("ffn2_pre_g", 1024), ("ffn2_post_g", 1024),
             ("conv_a_b", 512), ("pool_scale", 512), ("sgu_ln_g", 512), ("sgu_ln_b", 512), ("conv_d_b", 512),
             ("conv_d_ln_g", 512), ("conv_d_ln_b", 512), ("b_gate", 4096))
_SP = {}
_off = 0
for _n, _w in _SP_NAMES:
    _SP[_n] = (_off, _w)
    _off += _w
_SP_TOTAL = _off


def _call(body, name, grid, in_specs, out_specs, out_shape, scratch=(), aliases=None):
    return pl.pallas_call(
        body, name=name, grid=grid, in_specs=in_specs, out_specs=out_specs, out_shape=out_shape,
        scratch_shapes=list(scratch), input_output_aliases=aliases or {},
        compiler_params=pltpu.CompilerParams(dimension_semantics=("arbitrary",) * len(grid),
                                             vmem_limit_bytes=VMEM_LIMIT))


def _nn(a, b):
    return lax.dot_general(a, b, (((1,), (0,)), ((), ())), preferred_element_type=F32)


def _nt(a, b):
    return lax.dot_general(a, b, (((1,), (1,)), ((), ())), preferred_element_type=F32)


def _tn(a, b):
    return lax.dot_general(a, b, (((0,), (0,)), ((), ())), preferred_element_type=F32)


def _rms(x):
    r = lax.rsqrt(jnp.mean(x * x, axis=-1, keepdims=True) + EPS)
    return x * r, r


def _rms_bwd(n, r, g, dout):
    dn = dout * g
    dx = r * (dn - n * jnp.mean(dn * n, axis=-1, keepdims=True))
    return dx, jnp.sum(dout * n, axis=0, keepdims=True)


def _ln(y):
    mu = jnp.mean(y, axis=-1, keepdims=True)
    yc = y - mu
    rs = lax.rsqrt(jnp.mean(yc * yc, axis=-1, keepdims=True) + EPS)
    return yc * rs, rs


def _ln_bwd(xh, rs, dxh):
    return rs * (dxh - jnp.mean(dxh, axis=-1, keepdims=True) - xh * jnp.mean(dxh * xh, axis=-1, keepdims=True))


def _silu_parts(a):
    s = jax.nn.sigmoid(a)
    sl = a * s
    return sl, s + sl * (1.0 - s)


_GELU_C = 0.7978845608028654
_GELU_A = 0.044715


def _gelu(x):
    return 0.5 * x * (1.0 + jnp.tanh(_GELU_C * (x + _GELU_A * x * x * x)))


def _gelu_parts(x):
    t = jnp.tanh(_GELU_C * (x + _GELU_A * x * x * x))
    g = 0.5 * x * (1.0 + t)
    dg = 0.5 * (1.0 + t) + 0.5 * x * (1.0 - t * t) * _GELU_C * (1.0 + 3.0 * _GELU_A * x * x)
    return g, dg


def _spspec(name, width, imap):
    off = _SP[name][0]
    assert off % width == 0
    return pl.BlockSpec((1, width), lambda *a: (0, off // width + imap(*a)))


def _zero(*a):
    return 0


def _row_once(tm, d):
    return pl.BlockSpec((tm, d), lambda i, j: (i, 0), pipeline_mode=pl.Buffered(1))


FFN_SG = 2


def _ffn_fwd(x, sp, pre, post, pf, dep):
    S, D = x.shape
    FS = pf.shape[1] // 3
    TM = min(512, S)
    SG, NG, W = FFN_SG, NS // FFN_SG, FFN_SG * FS

    def body(x_ref, pg_ref, qg_ref, w1_ref, w3_ref, w2_ref, dep_ref, xo_ref, hb_ref, a_ref, b_ref, y_ref, hb_s, acc):
        j = pl.program_id(1)

        @pl.when(j == 0)
        def _():
            n, _ = _rms(x_ref[...])
            hb = (n * pg_ref[...]).astype(CDT)
            hb_s[...] = hb
            hb_ref[...] = hb
            acc[...] = jnp.zeros_like(acc)

        hb = hb_s[...]
        a = _nt(hb, w1_ref[...].reshape(W, D))
        b = _nt(hb, w3_ref[...].reshape(W, D))
        a_ref[...] = a.astype(CDT)
        b_ref[...] = b.astype(CDT)
        u = (a * jax.nn.sigmoid(a) * b).astype(CDT)
        acc[...] += _nn(u, w2_ref[...].reshape(W, D))

        @pl.when(j == NG - 1)
        def _():
            y = acc[...]
            y_ref[...] = y.astype(CDT)
            n, _ = _rms(y)
            xo_ref[...] = x_ref[...] + 0.5 * (n * qg_ref[...])

    row1 = pl.BlockSpec((TM, D), lambda i, j: (i, 0))
    grp = lambda i, j: (j, i, 0)
    return _call(
        body, "ffn_fwd", (S // TM, NG),
        [row1, _spspec(pre, D, _zero), _spspec(post, D, _zero),
         pl.BlockSpec((SG, FS, D), lambda i, j: (j, 0, 0)), pl.BlockSpec((SG, FS, D), lambda i, j: (j, 1, 0)),
         pl.BlockSpec((SG, FS, D), lambda i, j: (j, 2, 0)), pl.BlockSpec(memory_space=pl.ANY)],
        [row1, row1, pl.BlockSpec((None, TM, W), grp), pl.BlockSpec((None, TM, W), grp), row1],
        [SDS((S, D), F32), SDS((S, D), CDT), SDS((NG, S, W), CDT), SDS((NG, S, W), CDT), SDS((S, D), CDT)],
        [pltpu.VMEM((TM, D), CDT), pltpu.VMEM((TM, D), F32)])(x, sp, sp, pf, pf, pf, dep)


def _ffn_bwd_act(dxo, x, y, a, b, sp, pre, post, pf, dep):
    S, D = x.shape
    FS = pf.shape[1] // 3
    TM = min(512, S)
    SG, NG, W = FFN_SG, NS // FFN_SG, FFN_SG * FS

    def body(dxo_ref, x_ref, y_ref, a_ref, b_ref, pg_ref, qg_ref, w1_ref, w3_ref, w2_ref, dep_ref,
             dx_ref, dyb_ref, da_ref, db_ref, gp_ref, dyb_s, acc):
        i = pl.program_id(0)
        j = pl.program_id(1)

        @pl.when((i == 0) & (j == 0))
        def _():
            gp_ref[...] = jnp.zeros_like(gp_ref)

        @pl.when(j == 0)
        def _():
            n, r = _rms(y_ref[...].astype(F32))
            dy, dg = _rms_bwd(n, r, qg_ref[...], 0.5 * dxo_ref[...])
            dyb = dy.astype(CDT)
            dyb_s[...] = dyb
            dyb_ref[...] = dyb
            gp_ref[1:2, :] += dg
            acc[...] = jnp.zeros_like(acc)

        sl, dsl = _silu_parts(a_ref[...].astype(F32))
        du = _nt(dyb_s[...], w2_ref[...].reshape(W, D))
        db = (du * sl).astype(CDT)
        da = (du * b_ref[...].astype(F32) * dsl).astype(CDT)
        da_ref[...] = da
        db_ref[...] = db
        acc[...] += _nn(da, w1_ref[...].reshape(W, D)) + _nn(db, w3_ref[...].reshape(W, D))

        @pl.when(j == NG - 1)
        def _():
            n, r = _rms(x_ref[...])
            dx, dg = _rms_bwd(n, r, pg_ref[...], acc[...])
            dx_ref[...] = dxo_ref[...] + dx
            gp_ref[0:1, :] += dg

    row = lambda i, j: (i, 0)
    grp = lambda i, j: (j, i, 0)
    return _call(
        body, "ffn_bwd_act", (S // TM, NG),
        [pl.BlockSpec((TM, D), row), pl.BlockSpec((TM, D), row), pl.BlockSpec((TM, D), row),
         pl.BlockSpec((None, TM, W), grp), pl.BlockSpec((None, TM, W), grp),
         _spspec(pre, D, _zero), _spspec(post, D, _zero),
         pl.BlockSpec((SG, FS, D), lambda i, j: (j, 0, 0)), pl.BlockSpec((SG, FS, D), lambda i, j: (j, 1, 0)),
         pl.BlockSpec((SG, FS, D), lambda i, j: (j, 2, 0)), pl.BlockSpec(memory_space=pl.ANY)],
        [pl.BlockSpec((TM, D), row), pl.BlockSpec((TM, D), row), pl.BlockSpec((None, TM, W), grp),
         pl.BlockSpec((None, TM, W), grp), pl.BlockSpec((8, D), lambda i, j: (0, 0))],
        [SDS((S, D), F32), SDS((S, D), CDT), SDS((NG, S, W), CDT), SDS((NG, S, W), CDT), SDS((8, D), F32)],
        [pltpu.VMEM((TM, D), CDT), pltpu.VMEM((TM, D), F32)])(dxo, x, y, a, b, sp, sp, pf, pf, pf, dep)


def _ffn_bwd_w(hb, dyb, a, b, da, db, dep):
    S, D = hb.shape
    SG, NG = FFN_SG, NS // FFN_SG
    W = a.shape[2]
    FS = W // SG
    TK = min(1024, S)
    NK = S // TK

    def body(hb_ref, dyb_ref, a_ref, b_ref, da_ref, db_ref, dep_ref, g_ref, acc):
        k = pl.program_id(1)

        @pl.when(k == 0)
        def _():
            acc[...] = jnp.zeros_like(acc)

        af = a_ref[...].astype(F32)
        u = (af * jax.nn.sigmoid(af) * b_ref[...].astype(F32)).astype(CDT)
        hb = hb_ref[...]
        acc[0:W, :] += _tn(da_ref[...], hb)
        acc[W:2 * W, :] += _tn(db_ref[...], hb)
        acc[2 * W:3 * W, :] += _tn(u, dyb_ref[...])

        @pl.when(k == NK - 1)
        def _():
            for s in range(SG):
                for r in range(3):
                    g_ref[s, r * FS:(r + 1) * FS, :] = acc[r * W + s * FS:r * W + (s + 1) * FS, :].astype(CDT)

    row = lambda j, k: (k, 0)
    grp = lambda j, k: (j, k, 0)
    return _call(
        body, "ffn_bwd_w", (NG, NK),
        [pl.BlockSpec((TK, D), row), pl.BlockSpec((TK, D), row)] + [pl.BlockSpec((None, TK, W), grp)] * 4 + [_ANY],
        pl.BlockSpec((SG, 3 * FS, D), lambda j, k: (j, 0, 0)),
        SDS((NS, 3 * FS, D), CDT),
        [pltpu.VMEM((3 * W, D), F32)])(hb, dyb, a, b, da, db, dep)


def _mix_in(x, sp, pma, dep):
    S, D = x.shape
    TM = min(1024, S)

    def body(x_ref, pg_ref, bg_ref, wi_ref, wg_ref, dep_ref, hb_ref, z_ref, g_ref, hb_s):
        @pl.when(pl.program_id(1) == 0)
        def _():
            n, _ = _rms(x_ref[...])
            hb = (n * pg_ref[...]).astype(CDT)
            hb_s[...] = hb
            hb_ref[...] = hb

        hb = hb_s[...]
        z_ref[...] = _nt(hb, wi_ref[...]).astype(CDT)
        g_ref[...] = jax.nn.sigmoid(_nt(hb, wg_ref[...]) + bg_ref[...]).astype(CDT)

    return _call(
        body, "mix_in", (S // TM, NS),
        [_row_once(TM, D), _spspec("mix_pre_g", D, _zero), _spspec("b_gate", MW, lambda i, j: j),
         pl.BlockSpec((None, MW, D), lambda i, j: (j, 0, 0)), pl.BlockSpec((None, MW, D), lambda i, j: (j, 1, 0)), _ANY],
        [_row_once(TM, D), pl.BlockSpec((None, TM, MW), lambda i, j: (j, i, 0)),
         pl.BlockSpec((None, TM, MW), lambda i, j: (j // 2, i, j % 2))],
        [SDS((S, D), CDT), SDS((NS, S, MW), CDT), SDS((4, S, D), CDT)],
        [pltpu.VMEM((TM, D), CDT)])(x, sp, sp, pma, pma, dep)


def _mix_in_bwd_act(dz, dgp, dxr, x, sp, pma):
    S, D = x.shape
    TM = min(1024, S)

    def body(dz_ref, dg_ref, dxr_ref, x_ref, pg_ref, w_ref, dx_ref, gp_ref, acc):
        i = pl.program_id(0)
        j = pl.program_id(1)

        @pl.when((i == 0) & (j == 0))
        def _():
            gp_ref[...] = jnp.zeros_like(gp_ref)

        @pl.when(j == 0)
        def _():
            acc[...] = jnp.zeros_like(acc)

        acc[...] += _nn(jnp.concatenate([dz_ref[...], dg_ref[...]], axis=1), w_ref[...])

        @pl.when(j == NS - 1)
        def _():
            n, r = _rms(x_ref[...])
            dx, dg = _rms_bwd(n, r, pg_ref[...], acc[...])
            dx_ref[...] = dxr_ref[...] + dx
            gp_ref[0:1, :] += dg

    return _call(
        body, "mix_in_bwd_act", (S // TM, NS),
        [pl.BlockSpec((None, TM, MW), lambda i, j: (j, i, 0)), pl.BlockSpec((None, TM, MW), lambda i, j: (j // 2, i, j % 2)),
         _row_once(TM, D), _row_once(TM, D), _spspec("mix_pre_g", D, _zero),
         pl.BlockSpec((None, 2 * MW, D), lambda i, j: (j, 0, 0))],
        [_row_once(TM, D), pl.BlockSpec((8, D), lambda i, j: (0, 0))],
        [SDS((S, D), F32), SDS((8, D), F32)],
        [pltpu.VMEM((TM, D), F32)])(dz, dgp, dxr, x, sp, pma)


def _mix_in_bwd_w(dz, dgp, hb):
    S, D = hb.shape
    TK = min(2048, S)
    NK = S // TK

    def body(dz_ref, dg_ref, hb_ref, g_ref, bg_ref, acc):
        k = pl.program_id(1)

        @pl.when(k == 0)
        def _():
            acc[...] = jnp.zeros_like(acc)
            bg_ref[...] = jnp.zeros_like(bg_ref)

        hb = hb_ref[...]
        dg = dg_ref[...]
        acc[0:MW, :] += _tn(dz_ref[...], hb)
        acc[MW:2 * MW, :] += _tn(dg, hb)
        bg_ref[0:1, :] += jnp.sum(dg.astype(F32), axis=0, keepdims=True)

        @pl.when(k == NK - 1)
        def _():
            g_ref[...] = acc[...].astype(CDT)

    return _call(
        body, "mix_in_bwd_w", (NS, NK),
        [pl.BlockSpec((None, TK, MW), lambda j, k: (j, k, 0)), pl.BlockSpec((None, TK, MW), lambda j, k: (j // 2, k, j % 2)),
         pl.BlockSpec((TK, D), lambda j, k: (k, 0))],
        [pl.BlockSpec((None, 2 * MW, D), lambda j, k: (j, 0, 0)), pl.BlockSpec((None, 8, MW), lambda j, k: (j, 0, 0))],
        [SDS((NS, 2 * MW, D), CDT), SDS((NS, 8, MW), F32)],
        [pltpu.VMEM((2 * MW, D), F32)])(dz, dgp, hb)


def _causal_taps(pad_ref, i, ch, halo, k_taps, lanes=slice(None)):
    val = pad_ref[pl.ds(pl.multiple_of(i * ch, 8), ch + halo), lanes]
    base = {}
    out = []
    for k in range(k_taps):
        q, r = divmod(k_taps - 1 - k, 8)
        if r not in base:
            base[r] = pltpu.roll(val, r, 0) if r else val
        out.append((k, base[r][halo - 8 * q:halo - 8 * q + ch, :]))
    return out


def _anti_taps(pad_ref, i, ch, halo, k_taps, lanes=slice(None)):
    val = pad_ref[pl.ds(pl.multiple_of(i * ch, 8), ch + halo), lanes]
    n = ch + halo
    base = {}
    out = []
    for k in range(k_taps):
        q, r = divmod(k_taps - 1 - k, 8)
        if r not in base:
            base[r] = pltpu.roll(val, n - r, 0) if r else val
        out.append((k, base[r][8 * q:8 * q + ch, :]))
    return out


def _conv_geometry(S, k_taps):
    halo = 8 * ((k_taps - 1 + 7) // 8)
    ch = min(256, S)
    return halo, ch, S // ch


def _rows(i, ch):
    return pl.ds(pl.multiple_of(i * ch, ch), ch)


def _mixA_fwd(z, cw, sp):
    S = z.shape[1]
    K = cw.shape[0]
    H, CH, NCH = _conv_geometry(S, K)

    def body(z_ref, w_ref, b_ref, o_ref, pad):
        pad[0:H, :] = jnp.zeros((H, GW), F32)

        def fill(i, c):
            r = _rows(i, CH)
            pad[pl.ds(pl.multiple_of(i * CH + H, 8), CH), :] = z_ref[2, r, :].astype(F32) * z_ref[0, r, :].astype(F32)
            return c

        lax.fori_loop(0, NCH, fill, 0)

        def conv(i, c):
            r = _rows(i, CH)
            acc = jnp.zeros((CH, GW), F32)
            for k, sh in _causal_taps(pad, i, CH, H, K):
                acc = acc + w_ref[k:k + 1, :] * sh
            o_ref[r, :] = (z_ref[1, r, :].astype(F32) * (acc + b_ref[...])).astype(CDT)
            return c

        lax.fori_loop(0, NCH, conv, 0)

    return _call(
        body, "mixA_fwd", (MW // GW,),
        [pl.BlockSpec((3, S, GW), lambda c: (0, 0, c)), pl.BlockSpec((K, GW), lambda c: (0, c)),
         _spspec("conv_a_b", GW, lambda c: c)],
        pl.BlockSpec((S, GW), lambda c: (0, c)), SDS((S, MW), CDT),
        [pltpu.VMEM((H + S, GW), F32)])(z, cw, sp)


def _mixA_bwd(z, dm, cw, sp):
    S = z.shape[1]
    K = cw.shape[0]
    H, CH, NCH = _conv_geometry(S, K)

    def body(z_ref, dm_ref, w_ref, b_ref, dz_ref, dw_ref, db_ref, pad, dpad, dw_s):
        pad[0:H, :] = jnp.zeros((H, GW), F32)
        dpad[pl.ds(S, H), :] = jnp.zeros((H, GW), F32)
        dw_s[...] = jnp.zeros_like(dw_s)
        db_ref[...] = jnp.zeros_like(db_ref)

        def fill(i, c):
            r = _rows(i, CH)
            pad[pl.ds(pl.multiple_of(i * CH + H, 8), CH), :] = z_ref[2, r, :].astype(F32) * z_ref[0, r, :].astype(F32)
            return c

        lax.fori_loop(0, NCH, fill, 0)

        def p1(i, c):
            r = _rows(i, CH)
            taps = _causal_taps(pad, i, CH, H, K)
            acc = jnp.zeros((CH, GW), F32)
            for k, sh in taps:
                acc = acc + w_ref[k:k + 1, :] * sh
            dmf = dm_ref[r, :].astype(F32)
            dz_ref[1, r, :] = (dmf * (acc + b_ref[...])).astype(CDT)
            dc = dmf * z_ref[1, r, :].astype(F32)
            dpad[r, :] = dc
            for k, sh in taps:
                dw_s[k:k + 1, :] += jnp.sum(dc * sh, axis=0, keepdims=True)
            db_ref[0:1, :] += jnp.sum(dc, axis=0, keepdims=True)
            return c

        lax.fori_loop(0, NCH, p1, 0)

        def p2(i, c):
            r = _rows(i, CH)
            dq = jnp.zeros((CH, GW), F32)
            for k, sh in _anti_taps(dpad, i, CH, H, K):
                dq = dq + w_ref[k:k + 1, :] * sh
            dz_ref[0, r, :] = (dq * z_ref[2, r, :].astype(F32)).astype(CDT)
            dz_ref[2, r, :] = (dq * z_ref[0, r, :].astype(F32)).astype(CDT)
            return c

        lax.fori_loop(0, NCH, p2, 0)
        dw_ref[...] = dw_s[0:K, :]

    return _call(
        body, "mixA_bwd", (MW // GW,),
        [pl.BlockSpec((3, S, GW), lambda c: (0, 0, c)), pl.BlockSpec((None, S, GW), lambda c: (0, 0, c)),
         pl.BlockSpec((K, GW), lambda c: (0, c)), _spspec("conv_a_b", GW, lambda c: c)],
        [pl.BlockSpec((3, S, GW), lambda c: (0, 0, c)), pl.BlockSpec((K, GW), lambda c: (0, c)),
         pl.BlockSpec((8, GW), lambda c: (0, c))],
        [SDS((NS, S, MW), CDT), SDS((K, MW), F32), SDS((8, MW), F32)],
        [pltpu.VMEM((H + S, GW), F32), pltpu.VMEM((S + H, GW), F32), pltpu.VMEM((8 * ((K + 7) // 8), GW), F32)])(z, dm, cw, sp)


def _mixD_conv_fwd(z, cw, sp):
    S = z.shape[1]
    K = cw.shape[0]
    H, CH, NCH = _conv_geometry(S, K)

    def body(z_ref, w_ref, b_ref, o_ref, pad):
        pad[0:H, :] = jnp.zeros((H, GW), F32)

        def fill(i, c):
            r = _rows(i, CH)
            pad[pl.ds(pl.multiple_of(i * CH + H, 8), CH), :] = (
                z_ref[0, r, :].astype(F32) * jax.nn.sigmoid(z_ref[1, r, :].astype(F32)))
            return c

        lax.fori_loop(0, NCH, fill, 0)

        def conv(i, c):
            acc = jnp.zeros((CH, GW), F32)
            for k, sh in _causal_taps(pad, i, CH, H, K):
                acc = acc + w_ref[k:k + 1, :] * sh
            o_ref[_rows(i, CH), :] = (acc + b_ref[...]).astype(CDT)
            return c

        lax.fori_loop(0, NCH, conv, 0)

    return _call(
        body, "mixD_conv_fwd", (MW // GW,),
        [pl.BlockSpec((2, S, GW), lambda c: (3, 0, c)), pl.BlockSpec((K, GW), lambda c: (0, c)),
         _spspec("conv_d_b", GW, lambda c: c)],
        pl.BlockSpec((S, GW), lambda c: (0, c)), SDS((S, MW), CDT),
        [pltpu.VMEM((H + S, GW), F32)])(z, cw, sp)


def _mixD_conv_bwd(z, dy, cw, dz):
    S = z.shape[1]
    K = cw.shape[0]
    H, CH, NCH = _conv_geometry(S, K)

    def body(z_ref, dy_ref, w_ref, dzin_ref, dz_ref, dw_ref, db_ref, pad, dpad, dw_s):
        pad[0:H, :] = jnp.zeros((H, GW), F32)
        dpad[pl.ds(S, H), :] = jnp.zeros((H, GW), F32)
        dw_s[...] = jnp.zeros_like(dw_s)
        db_ref[...] = jnp.zeros_like(db_ref)

        def fill(i, c):
            r = _rows(i, CH)
            pad[pl.ds(pl.multiple_of(i * CH + H, 8), CH), :] = (
                z_ref[0, r, :].astype(F32) * jax.nn.sigmoid(z_ref[1, r, :].astype(F32)))
            dpad[r, :] = dy_ref[r, :].astype(F32)
            return c

        lax.fori_loop(0, NCH, fill, 0)

        def p1(i, c):
            dyf = dy_ref[_rows(i, CH), :].astype(F32)
            for k, sh in _causal_taps(pad, i, CH, H, K):
                dw_s[k:k + 1, :] += jnp.sum(dyf * sh, axis=0, keepdims=True)
            db_ref[0:1, :] += jnp.sum(dyf, axis=0, keepdims=True)
            return c

        lax.fori_loop(0, NCH, p1, 0)

        def p2(i, c):
            r = _rows(i, CH)
            dq = jnp.zeros((CH, GW), F32)
            for k, sh in _anti_taps(dpad, i, CH, H, K):
                dq = dq + w_ref[k:k + 1, :] * sh
            a = z_ref[0, r, :].astype(F32)
            sg = jax.nn.sigmoid(z_ref[1, r, :].astype(F32))
            dz_ref[0, r, :] = (dq * sg).astype(CDT)
            dz_ref[1, r, :] = (dq * a * sg * (1.0 - sg)).astype(CDT)
            return c

        lax.fori_loop(0, NCH, p2, 0)
        dw_ref[...] = dw_s[0:K, :]

    return _call(
        body, "mixD_conv_bwd", (MW // GW,),
        [pl.BlockSpec((2, S, GW), lambda c: (3, 0, c)), pl.BlockSpec((S, GW), lambda c: (0, c)),
         pl.BlockSpec((K, GW), lambda c: (0, c)), _ANY],
        [pl.BlockSpec((2, S, GW), lambda c: (3, 0, c)), pl.BlockSpec((K, GW), lambda c: (0, c)),
         pl.BlockSpec((8, GW), lambda c: (0, c))],
        [SDS((NS, S, MW), CDT), SDS((K, MW), F32), SDS((8, MW), F32)],
        [pltpu.VMEM((H + S, GW), F32), pltpu.VMEM((S + H, GW), F32), pltpu.VMEM((8 * ((K + 7) // 8), GW), F32)],
        aliases={3: 0})(z, dy, cw, dz)


def _mixD_ln_bwd(dm, yd, sp):
    S = yd.shape[0]
    TM = min(512, S)

    def body(dm_ref, y_ref, lg_ref, lb_ref, dy_ref, gp_ref):
        @pl.when(pl.program_id(0) == 0)
        def _():
            gp_ref[...] = jnp.zeros_like(gp_ref)

        xh, rs = _ln(y_ref[...].astype(F32))
        _, dsl = _silu_parts(xh * lg_ref[...] + lb_ref[...])
        dl = dm_ref[...].astype(F32) * dsl
        gp_ref[0:1, :] += jnp.sum(dl * xh, axis=0, keepdims=True)
        gp_ref[1:2, :] += jnp.sum(dl, axis=0, keepdims=True)
        dy_ref[...] = _ln_bwd(xh, rs, dl * lg_ref[...]).astype(CDT)

    row = lambda i: (i, 0)
    return _call(
        body, "mixD_ln_bwd", (S // TM,),
        [pl.BlockSpec((None, TM, MW), lambda i: (3, i, 0)), pl.BlockSpec((TM, MW), row), _spspec("conv_d_ln_g", MW, _zero),
         _spspec("conv_d_ln_b", MW, _zero)],
        [pl.BlockSpec((TM, MW), row), pl.BlockSpec((8, MW), lambda i: (0, 0))],
        [SDS((S, MW), CDT), SDS((8, MW), F32)])(dm, yd, sp, sp)


def _box_causal(val, g):
    s = val
    for d in range(g + 1):
        s = s + pltpu.roll(s, 1 << d, 0)
    return s


def _box_anti(val, g):
    n = val.shape[0]
    s = val
    for d in range(g + 1):
        s = s + pltpu.roll(s, n - (1 << d), 0)
    return s


def _pool_count(i, ch, win):
    t = lax.broadcasted_iota(jnp.int32, (ch, GW), 0) + (i * ch + 1)
    return jnp.minimum(t, win).astype(F32)


def _mixB_fwd(z, wp, sp):
    S = z.shape[1]
    H, CH = 16, min(256, S)
    NCH = S // CH
    assert POOL_WINDOWS == tuple(2 << g for g in range(4))

    def body(p_ref, wp_ref, sc_ref, o_ref, pad):
        pad[0:H, :] = jnp.zeros((H, MW), F32)

        def fill(i, c):
            pad[pl.ds(pl.multiple_of(i * CH + H, 8), CH), :] = p_ref[_rows(i, CH), :].astype(F32)
            return c

        lax.fori_loop(0, NCH, fill, 0)

        def step(i, c):
            r = _rows(i, CH)
            for g in range(4):
                gs = slice(g * GW, (g + 1) * GW)
                val = pad[pl.ds(pl.multiple_of(i * CH, 8), CH + H), gs]
                pooled = _box_causal(val, g)[H:, :] / _pool_count(i, CH, POOL_WINDOWS[g]) - val[H:, :]
                mixed = _nn(pooled.astype(CDT), wp_ref[g].astype(CDT))
                o_ref[r, gs] = (mixed * sc_ref[:, gs]).astype(CDT)
            return c

        lax.fori_loop(0, NCH, step, 0)

    return _call(
        body, "mixB_fwd", (1,),
        [pl.BlockSpec((None, S, MW), lambda i: (3, 0, 0)), pl.BlockSpec((4, GW, GW), lambda i: (0, 0, 0)),
         _spspec("pool_scale", MW, _zero)],
        pl.BlockSpec((S, MW), lambda i: (0, 0)), SDS((S, MW), CDT),
        [pltpu.VMEM((H + S, MW), F32)])(z, wp, sp)


def _mixB_bwd(z, dm, wp, sp, dz):
    S = z.shape[1]
    H, CH = 16, min(256, S)
    NCH = S // CH

    def body(p_ref, dm_ref, wp_ref, sc_ref, dzin_ref, dz_ref, dwp_ref, dsc_ref, pad, rpad):
        pad[0:H, :] = jnp.zeros((H, MW), F32)
        rpad[pl.ds(S, H), :] = jnp.zeros((H, MW), F32)
        dwp_ref[...] = jnp.zeros_like(dwp_ref)
        dsc_ref[...] = jnp.zeros_like(dsc_ref)

        def fill(i, c):
            pad[pl.ds(pl.multiple_of(i * CH + H, 8), CH), :] = p_ref[_rows(i, CH), :].astype(F32)
            return c

        lax.fori_loop(0, NCH, fill, 0)

        def p1(i, c):
            r = _rows(i, CH)
            for g in range(4):
                gs = slice(g * GW, (g + 1) * GW)
                cnt = _pool_count(i, CH, POOL_WINDOWS[g])
                val = pad[pl.ds(pl.multiple_of(i * CH, 8), CH + H), gs]
                pooled = (_box_causal(val, g)[H:, :] / cnt - val[H:, :]).astype(CDT)
                w = wp_ref[g].astype(CDT)
                mixed = _nn(pooled, w)
                dmf = dm_ref[r, gs].astype(F32)
                dsc_ref[0:1, gs] += jnp.sum(dmf * mixed, axis=0, keepdims=True)
                dmx = (dmf * sc_ref[:, gs]).astype(CDT)
                dwp_ref[g] += _tn(pooled, dmx)
                rpad[r, gs] = _nt(dmx, w) / cnt
            return c

        lax.fori_loop(0, NCH, p1, 0)

        def p2(i, c):
            r = _rows(i, CH)
            for g in range(4):
                gs = slice(g * GW, (g + 1) * GW)
                val = rpad[pl.ds(pl.multiple_of(i * CH, 8), CH + H), gs]
                dp = _box_anti(val, g)[:CH, :] - val[:CH, :] * _pool_count(i, CH, POOL_WINDOWS[g])
                dz_ref[r, gs] = dp.astype(CDT)
            return c

        lax.fori_loop(0, NCH, p2, 0)

    return _call(
        body, "mixB_bwd", (1,),
        [pl.BlockSpec((None, S, MW), lambda i: (3, 0, 0)), pl.BlockSpec((None, S, MW), lambda i: (1, 0, 0)),
         pl.BlockSpec((4, GW, GW), lambda i: (0, 0, 0)), _spspec("pool_scale", MW, _zero), _ANY],
        [pl.BlockSpec((None, S, MW), lambda i: (3, 0, 0)), pl.BlockSpec((4, GW, GW), lambda i: (0, 0, 0)),
         pl.BlockSpec((8, MW), lambda i: (0, 0))],
        [SDS((NS, S, MW), CDT), SDS((4, GW, GW), F32), SDS((8, MW), F32)],
        [pltpu.VMEM((H + S, MW), F32), pltpu.VMEM((S + H, MW), F32)], aliases={4: 0})(z, dm, wp, sp, dz)


def _sgu_mask():
    ci = lax.broadcasted_iota(jnp.int32, (GW, GW), 0) // CHUNK
    cj = lax.broadcasted_iota(jnp.int32, (GW, GW), 1) // CHUNK
    return cj <= ci


def _mixC_fwd(z, ws, bsc, sp):
    S = z.shape[1]
    RB = min(512, S)

    def body(z_ref, lg_ref, lb_ref, ws_ref, bs_ref, o_ref):
        mask = _sgu_mask()
        gu = _gelu(z_ref[0].astype(F32))
        xh, _ = _ln(_gelu(z_ref[1].astype(F32)))
        vn = (xh * lg_ref[...] + lb_ref[...]).astype(CDT)
        for g in range(4):
            gs = slice(g * GW, (g + 1) * GW)
            wm = jnp.where(mask, ws_ref[g], 0.0).astype(CDT)
            for nb in range(RB // GW):
                rs = slice(nb * GW, (nb + 1) * GW)
                mixed = _nn(wm, vn[rs, gs]) + bs_ref[g]
                o_ref[rs, gs] = (gu[rs, gs] * mixed).astype(CDT)

    return _call(
        body, "mixC_fwd", (S // RB,),
        [pl.BlockSpec((2, RB, MW), lambda i: (2, i, 0)), _spspec("sgu_ln_g", MW, _zero), _spspec("sgu_ln_b", MW, _zero),
         pl.BlockSpec((4, GW, GW), lambda i: (0, 0, 0)), pl.BlockSpec((4, GW, 1), lambda i: (0, 0, 0))],
        pl.BlockSpec((RB, MW), lambda i: (i, 0)), SDS((S, MW), CDT))(z, sp, sp, ws, bsc)


def _mixC_bwd(z, dm, ws, bsc, sp, dz):
    S = z.shape[1]
    RB = min(512, S)
    NR = S // RB

    def body(z_ref, dm_ref, lg_ref, lb_ref, ws_ref, bs_ref, dzin_ref, dz_ref, dws_ref, dbs_ref, gp_ref, dvn_s):
        i = pl.program_id(0)

        @pl.when(i == 0)
        def _():
            dws_ref[...] = jnp.zeros_like(dws_ref)
            dbs_ref[...] = jnp.zeros_like(dbs_ref)
            gp_ref[...] = jnp.zeros_like(gp_ref)

        mask = _sgu_mask()
        gu, dgu = _gelu_parts(z_ref[0].astype(F32))
        gv, dgv = _gelu_parts(z_ref[1].astype(F32))
        xh, rs_ = _ln(gv)
        vn = (xh * lg_ref[...] + lb_ref[...]).astype(CDT)
        dmf = dm_ref[...].astype(F32)
        for g in range(4):
            gs = slice(g * GW, (g + 1) * GW)
            wm = jnp.where(mask, ws_ref[g], 0.0).astype(CDT)
            for nb in range(RB // GW):
                rs = slice(nb * GW, (nb + 1) * GW)
                vb = vn[rs, gs]
                mixed = _nn(wm, vb) + bs_ref[g]
                dz_ref[0, rs, gs] = (dmf[rs, gs] * mixed * dgu[rs, gs]).astype(CDT)
                dmx = dmf[rs, gs] * gu[rs, gs]
                dbs_ref[g] += dmx
                dmxc = dmx.astype(CDT)
                dws_ref[g] += _nt(dmxc, vb)
                dvn_s[rs, gs] = _tn(wm, dmxc)
        dvn = dvn_s[...]
        gp_ref[0:1, :] += jnp.sum(dvn * xh, axis=0, keepdims=True)
        gp_ref[1:2, :] += jnp.sum(dvn, axis=0, keepdims=True)
        dz_ref[1] = (_ln_bwd(xh, rs_, dvn * lg_ref[...]) * dgv).astype(CDT)

        @pl.when(i == NR - 1)
        def _():
            for g in range(4):
                dws_ref[g] = jnp.where(mask, dws_ref[g], 0.0)
                dbs_ref[g] = jnp.broadcast_to(jnp.sum(dbs_ref[g], axis=1, keepdims=True), (GW, GW))

    full3 = lambda i: (0, 0, 0)
    return _call(
        body, "mixC_bwd", (NR,),
        [pl.BlockSpec((2, RB, MW), lambda i: (2, i, 0)), pl.BlockSpec((None, RB, MW), lambda i: (2, i, 0)),
         _spspec("sgu_ln_g", MW, _zero), _spspec("sgu_ln_b", MW, _zero),
         pl.BlockSpec((4, GW, GW), full3), pl.BlockSpec((4, GW, 1), full3), _ANY],
        [pl.BlockSpec((2, RB, MW), lambda i: (2, i, 0)), pl.BlockSpec((4, GW, GW), full3), pl.BlockSpec((4, GW, GW), full3),
         pl.BlockSpec((8, MW), lambda i: (0, 0))],
        [SDS((NS, S, MW), CDT), SDS((4, GW, GW), F32), SDS((4, GW, GW), F32), SDS((8, MW), F32)],
        [pltpu.VMEM((RB, MW), F32)], aliases={6: 0})(z, dm, sp, sp, ws, bsc, dz)


def _unpack_wb(wb_ref, wbf):
    for j in range(NS):
        for k in range(4):
            wbf[k, :, j * GW:(j + 1) * GW] = wb_ref[j, k]


def _merge_fwd(ma, mb, mc, yd, g, wb, pwo, x, sp):
    S, D = x.shape
    TM = min(512, S)

    def body(ma_ref, mb_ref, mc_ref, yd_ref, g_ref, wb_ref, wo_ref, x_ref, lg_ref, lb_ref, qg_ref,
             xo_ref, md_ref, yk_ref, mg_ref, mo_ref, wbf):
        @pl.when(pl.program_id(0) == 0)
        def _():
            _unpack_wb(wb_ref, wbf)

        xh, _ = _ln(yd_ref[...].astype(F32))
        sl, _ = _silu_parts(xh * lg_ref[...] + lb_ref[...])
        md = sl.astype(CDT)
        md_ref[...] = md
        merged = jnp.zeros((TM, D), F32)
        for k, m in enumerate((ma_ref[...], mb_ref[...], mc_ref[...], md)):
            yk = _nn(m, wbf[k])
            yk_ref[k] = yk.astype(CDT)
            merged = merged + g_ref[k].astype(F32) * yk
        mgc = merged.astype(CDT)
        mg_ref[...] = mgc
        mo = _nn(mgc, wo_ref[...].reshape(D, D))
        mo_ref[...] = mo.astype(CDT)
        n, _ = _rms(mo)
        xo_ref[...] = x_ref[...] + n * qg_ref[...]

    row = lambda i: (i, 0)
    rowm = pl.BlockSpec((TM, MW), row)
    rowd = pl.BlockSpec((TM, D), row)
    row4 = pl.BlockSpec((4, TM, D), lambda i: (0, i, 0))
    return _call(
        body, "merge_fwd", (S // TM,),
        [rowm, rowm, rowm, rowm, row4, pl.BlockSpec((NS, 4, MW, GW), lambda i: (0, 0, 0, 0), pipeline_mode=pl.Buffered(1)),
         pl.BlockSpec((NS, GW, D), lambda i: (0, 0, 0), pipeline_mode=pl.Buffered(1)), rowd,
         _spspec("conv_d_ln_g", MW, _zero), _spspec("conv_d_ln_b", MW, _zero), _spspec("mix_post_g", D, _zero)],
        [rowd, rowm, row4, rowd, rowd],
        [SDS((S, D), F32), SDS((S, MW), CDT), SDS((4, S, D), CDT), SDS((S, D), CDT), SDS((S, D), CDT)],
        [pltpu.VMEM((4, MW, D), CDT)])(ma, mb, mc, yd, g, wb, pwo, x, sp, sp, sp)


def _merge_bwd_act(dxo, mo, g, yk, wb, pwo, sp, dep):
    S, D = dxo.shape
    TM = min(256, S)

    def body(dxo_ref, mo_ref, g_ref, yk_ref, wb_ref, wo_ref, qg_ref, dep_ref, dmo_ref, dm_ref, dgp_ref, dyk_ref, gp_ref, wbf):
        @pl.when(pl.program_id(0) == 0)
        def _():
            gp_ref[...] = jnp.zeros_like(gp_ref)
            _unpack_wb(wb_ref, wbf)

        n, r = _rms(mo_ref[...].astype(F32))
        dmo, dg = _rms_bwd(n, r, qg_ref[...], dxo_ref[...])
        gp_ref[0:1, :] += dg
        dmoc = dmo.astype(CDT)
        dmo_ref[...] = dmoc
        dmg = _nt(dmoc, wo_ref[...].reshape(D, D))
        for k in range(4):
            gk = g_ref[k].astype(F32)
            dyk = (dmg * gk).astype(CDT)
            dyk_ref[k] = dyk
            dgp_ref[k] = (dmg * yk_ref[k].astype(F32) * gk * (1.0 - gk)).astype(CDT)
            dm_ref[k] = _nt(dyk, wbf[k]).astype(CDT)

    rowd = pl.BlockSpec((TM, D), lambda i: (i, 0))
    row4 = pl.BlockSpec((4, TM, D), lambda i: (0, i, 0))
    return _call(
        body, "merge_bwd_act", (S // TM,),
        [rowd, rowd, row4, row4, pl.BlockSpec((NS, 4, MW, GW), lambda i: (0, 0, 0, 0), pipeline_mode=pl.Buffered(1)),
         pl.BlockSpec((NS, GW, D), lambda i: (0, 0, 0), pipeline_mode=pl.Buffered(1)), _spspec("mix_post_g", D, _zero), _ANY],
        [rowd, pl.BlockSpec((4, TM, MW), lambda i: (0, i, 0)), row4, row4, pl.BlockSpec((8, D), lambda i: (0, 0))],
        [SDS((S, D), CDT), SDS((4, S, MW), CDT), SDS((4, S, D), CDT), SDS((4, S, D), CDT), SDS((8, D), F32)],
        [pltpu.VMEM((4, MW, D), CDT)])(dxo, mo, g, yk, wb, pwo, sp, dep)


def _merge_bwd_w(ma, mb, mc, md, dyk, mg, dmo):
    S, D = dmo.shape
    TK = min(512, S)
    NK = S // TK

    def body(ma_ref, mb_ref, mc_ref, md_ref, dyk_ref, mg_ref, dmo_ref, gwb_ref, gwo_ref, accb, acco):
        k = pl.program_id(0)

        @pl.when(k == 0)
        def _():
            accb[...] = jnp.zeros_like(accb)
            acco[...] = jnp.zeros_like(acco)

        for b, m in enumerate((ma_ref, mb_ref, mc_ref, md_ref)):
            accb[b] += _tn(m[...], dyk_ref[b])
        acco[...] += _tn(mg_ref[...], dmo_ref[...])

        @pl.when(k == NK - 1)
        def _():
            for j in range(NS):
                for b in range(4):
                    gwb_ref[j, b] = accb[b, :, j * GW:(j + 1) * GW].astype(CDT)
                gwo_ref[j] = acco[j * GW:(j + 1) * GW, :].astype(CDT)

    rowm = pl.BlockSpec((TK, MW), lambda k: (k, 0))
    rowd = pl.BlockSpec((TK, D), lambda k: (k, 0))
    return _call(
        body, "merge_bwd_w", (NK,),
        [rowm, rowm, rowm, rowm, pl.BlockSpec((4, TK, D), lambda k: (0, k, 0)), rowd, rowd],
        [pl.BlockSpec((NS, 4, MW, GW), lambda k: (0, 0, 0, 0)), pl.BlockSpec((NS, GW, D), lambda k: (0, 0, 0))],
        [SDS((NS, 4, MW, GW), CDT), SDS((NS, GW, D), CDT)],
        [pltpu.VMEM((4, MW, D), F32), pltpu.VMEM((D, D), F32)])(ma, mb, mc, md, dyk, mg, dmo)


def _xa_kv(mem, sp, pxa):
    M, D = mem.shape

    def body(m_ref, g_ref, wk_ref, wv_ref, mn_ref, k_ref, v_ref):
        n, _ = _rms(m_ref[...])
        mn = (n * g_ref[...]).astype(CDT)
        mn_ref[...] = mn
        k_ref[...] = _nn(mn, wk_ref[...].reshape(D, D)).astype(CDT)
        v_ref[...] = _nn(mn, wv_ref[...].reshape(D, D)).astype(CDT)

    full = pl.BlockSpec((M, D), lambda i: (0, 0))
    return _call(
        body, "xa_kv", (1,),
        [full, _spspec("mem_g", D, _zero), pl.BlockSpec((NS, GW, D), lambda i: (0, 1, 0)),
         pl.BlockSpec((NS, GW, D), lambda i: (0, 2, 0))],
        [full, full, full], [SDS((M, D), CDT)] * 3)(mem, sp, pxa, pxa)


def _softmax(s):
    e = jnp.exp(s - jnp.max(s, axis=-1, keepdims=True))
    return e / jnp.sum(e, axis=-1, keepdims=True)


def _xa_fwd(x, kk, vv, sp, pxa):
    S, D = x.shape
    M = kk.shape[0]
    TM = min(512, S)
    HD = D // XA_HEADS
    scale = HD ** -0.5

    def body(x_ref, k_ref, v_ref, pg_ref, qg_ref, wq_ref, wo_ref, xo_ref, hb_ref, q_ref, o_ref, po_ref):
        n, _ = _rms(x_ref[...])
        hb = (n * pg_ref[...]).astype(CDT)
        hb_ref[...] = hb
        q = _nn(hb, wq_ref[...].reshape(D, D)).astype(CDT)
        q_ref[...] = q
        for h in range(XA_HEADS):
            hs = slice(h * HD, (h + 1) * HD)
            p = _softmax(_nt(q[:, hs], k_ref[:, hs]) * scale)
            o_ref[:, hs] = _nn(p.astype(CDT), v_ref[:, hs]).astype(CDT)
        po = _nn(o_ref[...], wo_ref[...].reshape(D, D))
        po_ref[...] = po.astype(CDT)
        n, _ = _rms(po)
        xo_ref[...] = x_ref[...] + n * qg_ref[...]

    row = pl.BlockSpec((TM, D), lambda i: (i, 0))
    full = pl.BlockSpec((M, D), lambda i: (0, 0))
    return _call(
        body, "xa_fwd", (S // TM,),
        [row, full, full, _spspec("xa_pre_g", D, _zero), _spspec("xa_post_g", D, _zero),
         pl.BlockSpec((NS, GW, D), lambda i: (0, 0, 0)), pl.BlockSpec((NS, GW, D), lambda i: (0, 3, 0))],
        [row] * 5, [SDS((S, D), F32)] + [SDS((S, D), CDT)] * 4)(x, kk, vv, sp, sp, pxa, pxa)


def _xa_bwd_act(dxo, x, po, q, kk, vv, sp, pxa, dep):
    S, D = x.shape
    M = kk.shape[0]
    TM = min(512, S)
    HD = D // XA_HEADS
    scale = HD ** -0.5

    def body(dxo_ref, x_ref, po_ref, q_ref, k_ref, v_ref, pg_ref, qg_ref, wq_ref, wo_ref, dep_ref,
             dx_ref, dpo_ref, dq_ref, dk_ref, dv_ref, gp_ref):
        @pl.when(pl.program_id(0) == 0)
        def _():
            gp_ref[...] = jnp.zeros_like(gp_ref)
            dk_ref[...] = jnp.zeros_like(dk_ref)
            dv_ref[...] = jnp.zeros_like(dv_ref)

        n, r = _rms(po_ref[...].astype(F32))
        dpo, dg = _rms_bwd(n, r, qg_ref[...], dxo_ref[...])
        gp_ref[1:2, :] += dg
        dpoc = dpo.astype(CDT)
        dpo_ref[...] = dpoc
        do = _nt(dpoc, wo_ref[...].reshape(D, D)).astype(CDT)
        for h in range(XA_HEADS):
            hs = slice(h * HD, (h + 1) * HD)
            qh = q_ref[:, hs]
            p = _softmax(_nt(qh, k_ref[:, hs]) * scale)
            pc = p.astype(CDT)
            dv_ref[:, hs] += _tn(pc, do[:, hs])
            dp = _nt(do[:, hs], v_ref[:, hs])
            ds = (p * (dp - jnp.sum(p * dp, axis=-1, keepdims=True)) * scale).astype(CDT)
            dq_ref[:, hs] = _nn(ds, k_ref[:, hs]).astype(CDT)
            dk_ref[:, hs] += _tn(ds, qh)
        dhb = _nt(dq_ref[...], wq_ref[...].reshape(D, D))
        n, r = _rms(x_ref[...])
        dx, dg = _rms_bwd(n, r, pg_ref[...], dhb)
        dx_ref[...] = dxo_ref[...] + dx
        gp_ref[0:1, :] += dg

    row = pl.BlockSpec((TM, D), lambda i: (i, 0))
    full = pl.BlockSpec((M, D), lambda i: (0, 0))
    return _call(
        body, "xa_bwd_act", (S // TM,),
        [row, row, row, row, full, full, _spspec("xa_pre_g", D, _zero), _spspec("xa_post_g", D, _zero),
         pl.BlockSpec((NS, GW, D), lambda i: (0, 0, 0)), pl.BlockSpec((NS, GW, D), lambda i: (0, 3, 0)), _ANY],
        [row, row, row, full, full, pl.BlockSpec((8, D), lambda i: (0, 0))],
        [SDS((S, D), F32), SDS((S, D), CDT), SDS((S, D), CDT), SDS((M, D), F32), SDS((M, D), F32), SDS((8, D), F32)],
    )(dxo, x, po, q, kk, vv, sp, sp, pxa, pxa, dep)


def _xa_bwd_w(hb, dq, o, dpo, mn, dk, dv):
    S, D = hb.shape
    M = mn.shape[0]
    TK = min(1024, S)
    NK = S // TK

    def body(hb_ref, dq_ref, o_ref, dpo_ref, mn_ref, dk_ref, dv_ref, g_ref, accq, acco):
        k = pl.program_id(0)

        @pl.when(k == 0)
        def _():
            accq[...] = jnp.zeros_like(accq)
            acco[...] = jnp.zeros_like(acco)

        accq[...] += _tn(hb_ref[...], dq_ref[...])
        acco[...] += _tn(o_ref[...], dpo_ref[...])

        @pl.when(k == NK - 1)
        def _():
            gk = _tn(mn_ref[...], dk_ref[...].astype(CDT))
            gv = _tn(mn_ref[...], dv_ref[...].astype(CDT))
            for j in range(NS):
                rs = slice(j * GW, (j + 1) * GW)
                g_ref[j, 0:GW, :] = accq[rs, :].astype(CDT)
                g_ref[j, GW:2 * GW, :] = gk[rs, :].astype(CDT)
                g_ref[j, 2 * GW:3 * GW, :] = gv[rs, :].astype(CDT)
                g_ref[j, 3 * GW:4 * GW, :] = acco[rs, :].astype(CDT)

    rowb = pl.BlockSpec((TK, D), lambda k: (k, 0))
    full = pl.BlockSpec((M, D), lambda k: (0, 0))
    return _call(
        body, "xa_bwd_w", (NK,),
        [rowb, rowb, rowb, rowb, full, full, full],
        pl.BlockSpec((NS, 4 * GW, D), lambda k: (0, 0, 0)), SDS((NS, 4 * GW, D), CDT),
        [pltpu.VMEM((D, D), F32), pltpu.VMEM((D, D), F32)])(hb, dq, o, dpo, mn, dk, dv)


def _xa_kv_bwd(mem, dk, dv, sp, pxa):
    M, D = mem.shape

    def body(m_ref, dk_ref, dv_ref, wk_ref, wv_ref, gp_ref):
        dmn = _nt(dk_ref[...].astype(CDT), wk_ref[...].reshape(D, D)) + _nt(dv_ref[...].astype(CDT), wv_ref[...].reshape(D, D))
        n, _ = _rms(m_ref[...])
        gp_ref[...] = jnp.zeros_like(gp_ref)
        gp_ref[0:1, :] = jnp.sum(dmn * n, axis=0, keepdims=True)

    full = pl.BlockSpec((M, D), lambda i: (0, 0))
    return _call(
        body, "xa_kv_bwd", (1,),
        [full, full, full, pl.BlockSpec((NS, GW, D), lambda i: (0, 1, 0)), pl.BlockSpec((NS, GW, D), lambda i: (0, 2, 0))],
        pl.BlockSpec((8, D), lambda i: (0, 0)), SDS((8, D), F32))(mem, dk, dv, pxa, pxa)


def _loss_head(y, t):
    S, D = y.shape
    TM = min(512, S)

    def body(y_ref, t_ref, dy_ref, l_ref):
        @pl.when(pl.program_id(0) == 0)
        def _():
            l_ref[...] = jnp.zeros_like(l_ref)

        e = y_ref[...] - t_ref[...]
        dy_ref[...] = e * (1.0 / D)
        l_ref[...] += 0.5 * jnp.sum(jnp.mean(e * e, axis=-1, keepdims=True), axis=0, keepdims=True)

    row = pl.BlockSpec((TM, D), lambda i: (i, 0))
    return _call(body, "loss_head", (S // TM,), [row, row], [row, pl.BlockSpec((8, 128), lambda i: (0, 0))],
                 [SDS((S, D), F32), SDS((8, 128), F32)])(y, t)


def _row_tile(rows, cols, limit=1 << 18, step=8):
    if rows * cols <= limit or rows % step:
        return rows
    best = step
    for t in range(step, rows + 1, step):
        if rows % t == 0 and t * cols <= limit:
            best = t
    return best


def _adamw(w, g, m, v):
    shape = w.shape
    C = shape[-1]
    R = w.size // C
    TR = _row_tile(R, C)
    c1 = 1.0 - ADAM_B1 ** ADAM_STEP
    c2 = 1.0 - ADAM_B2 ** ADAM_STEP

    def body(w_ref, g_ref, m_ref, v_ref, d_ref, nm_ref, nv_ref):
        gg = g_ref[...]
        nm = ADAM_B1 * m_ref[...] + (1.0 - ADAM_B1) * gg
        nv = ADAM_B2 * v_ref[...] + (1.0 - ADAM_B2) * (gg * gg)
        nm_ref[...] = nm
        nv_ref[...] = nv
        d_ref[...] = -ADAM_LR * ((nm / c1) / (jnp.sqrt(nv / c2) + ADAM_EPS) + ADAM_WD * w_ref[...])

    blk = pl.BlockSpec((TR, C), lambda i: (i, 0))
    outs = _call(body, "adamw", (R // TR,), [blk] * 4, [blk] * 3, [SDS((R, C), F32)] * 3)(
        w.reshape(R, C), g.reshape(R, C), m.reshape(R, C), v.reshape(R, C))
    return tuple(o.reshape(shape) for o in outs)


def _adamw_block(w, gs, m, v, gblock):
    L, R, C = w.shape
    c1 = 1.0 - ADAM_B1 ** ADAM_STEP
    c2 = 1.0 - ADAM_B2 ** ADAM_STEP

    def body(w_ref, g_ref, m_ref, v_ref, go_ref, d_ref, nm_ref, nv_ref):
        gg = g_ref[...]
        go_ref[...] = gg
        nm = ADAM_B1 * m_ref[...] + (1.0 - ADAM_B1) * gg
        nv = ADAM_B2 * v_ref[...] + (1.0 - ADAM_B2) * (gg * gg)
        nm_ref[...] = nm
        nv_ref[...] = nv
        d_ref[...] = -ADAM_LR * ((nm / c1) / (jnp.sqrt(nv / c2) + ADAM_EPS) + ADAM_WD * w_ref[...])

    blk = pl.BlockSpec((None, R, C), lambda l: (l, 0, 0))
    return _call(body, "adamw_block", (L,), [blk, pl.BlockSpec((None, R, C), lambda l: (l, gblock, 0)), blk, blk],
                 [blk] * 4, [SDS((L, R, C), F32)] * 4)(w, gs, m, v)


def _slot_sum_into(stacked, r, l):
    _, R, C = r.shape
    TR = _row_tile(R, C * NS, limit=1 << 21, step=16)

    def body(r_ref, s_ref, o_ref):
        acc = r_ref[0].astype(F32)
        for j in range(1, NS):
            acc = acc + r_ref[j].astype(F32)
        o_ref[...] = acc

    return pl.pallas_call(
        body, name="slot_sum_into", grid=(R // TR,),
        in_specs=[pl.BlockSpec((NS, TR, C), lambda i: (0, i, 0)), _ANY],
        out_specs=pl.BlockSpec((None, TR, C), lambda i: (l, i, 0)), out_shape=SDS(stacked.shape, F32),
        input_output_aliases={1: 0},
        compiler_params=pltpu.CompilerParams(dimension_semantics=("arbitrary",), vmem_limit_bytes=VMEM_LIMIT))(r, stacked)


def _exchange(arrs, scatter, name):
    n = len(arrs)
    np_ = NS - 1

    def body(*refs):
        ins, outs = refs[:n], refs[n:2 * n]
        send_sems, recv_sems, loc_sems = refs[2 * n:]
        x, y, c = lax.axis_index("x"), lax.axis_index("y"), lax.axis_index("c")
        me = 4 * x + 2 * y + c
        peers = []
        for f in range(1, NS):
            px = 1 - x if f & 4 else x
            py = 1 - y if f & 2 else y
            pc = 1 - c if f & 1 else c
            peers.append(((px, py, pc), 4 * px + 2 * py + pc))

        def src(a, pid):
            return ins[a].at[pid] if scatter else ins[a]

        local = [pltpu.make_async_copy(src(a, me), outs[a].at[me], loc_sems.at[a]) for a in range(n)]
        for cp in local:
            cp.start()
        sends = []
        for a in range(n):
            for f, (dev, pid) in enumerate(peers):
                sends.append(pltpu.make_async_remote_copy(
                    src_ref=src(a, pid), dst_ref=outs[a].at[me], send_sem=send_sems.at[a * np_ + f],
                    recv_sem=recv_sems.at[a * np_ + f], device_id=dev, device_id_type=pl.DeviceIdType.MESH))
        for cp in sends:
            cp.start()
        for a in range(n):
            for f, (dev, pid) in enumerate(peers):
                pltpu.make_async_remote_copy(
                    src_ref=src(a, pid), dst_ref=outs[a].at[pid], send_sem=send_sems.at[a * np_ + f],
                    recv_sem=recv_sems.at[a * np_ + f], device_id=dev, device_id_type=pl.DeviceIdType.MESH).wait_recv()
        for cp in sends:
            cp.wait_send()
        for cp in local:
            cp.wait()

    out_shape = [SDS(a.shape if scatter else (NS,) + a.shape, a.dtype) for a in arrs]
    anyspec = pl.BlockSpec(memory_space=pl.ANY)
    outs = pl.pallas_call(
        body, name=name, in_specs=[anyspec] * n, out_specs=[anyspec] * n, out_shape=out_shape,
        scratch_shapes=[pltpu.SemaphoreType.DMA((n * np_,)), pltpu.SemaphoreType.DMA((n * np_,)),
                        pltpu.SemaphoreType.DMA((n,))],
        compiler_params=pltpu.CompilerParams(has_side_effects=True))(*arrs)
    return list(outs)


def _peers():
    x, y, c = lax.axis_index("x"), lax.axis_index("y"), lax.axis_index("c")
    out = []
    for f in range(1, NS):
        px = 1 - x if f & 4 else x
        py = 1 - y if f & 2 else y
        pc = 1 - c if f & 1 else c
        out.append(((px, py, pc), 4 * px + 2 * py + pc))
    return 4 * x + 2 * y + c, out


def _exchange_copies(ins, lands, scatter, send_sems, recv_sems, loc_sems):
    me, peers = _peers()
    np_ = NS - 1

    def src(a, pid):
        return ins[a].at[pid] if scatter[a] else ins[a]

    def rcopy(a, f, dev, land_slot):
        return pltpu.make_async_remote_copy(
            src_ref=src(a, peers[f][1]), dst_ref=lands[a].at[land_slot], send_sem=send_sems.at[a * np_ + f],
            recv_sem=recv_sems.at[a * np_ + f], device_id=dev, device_id_type=pl.DeviceIdType.MESH)

    local = [pltpu.make_async_copy(src(a, me), lands[a].at[me], loc_sems.at[a]) for a in range(len(ins))]
    sends = [rcopy(a, f, dev, me) for a in range(len(ins)) for f, (dev, _) in enumerate(peers)]
    arrivals = [rcopy(a, f, dev, pid) for a in range(len(ins)) for f, (dev, pid) in enumerate(peers)]
    return local, sends, arrivals


_HBM = pl.BlockSpec(memory_space=pltpu.HBM)
_SEM = pl.BlockSpec(memory_space=pltpu.SEMAPHORE)
_ANY = pl.BlockSpec(memory_space=pl.ANY)


def _exchange_start(arrs, scatter, after, name):
    n = len(arrs)
    np_ = NS - 1
    lands = [lax.empty(a.shape if sc else (NS,) + a.shape, a.dtype) for a, sc in zip(arrs, scatter)]

    def body(*refs):
        ins, lnd = refs[:n], refs[n:2 * n]
        send_sems, recv_sems, loc_sems = refs[2 * n + 1:2 * n + 4]
        token = refs[-1]
        local, sends, _ = _exchange_copies(ins, lnd, scatter, send_sems, recv_sems, loc_sems)
        for cp in local + sends:
            cp.start()
        token[...] = jnp.zeros_like(token)

    hbm = lambda a: pltpu.HBM(a.shape, a.dtype)
    outs = pl.pallas_call(
        body, name=name,
        out_shape=(pltpu.SemaphoreType.DMA((n * np_,)), pltpu.SemaphoreType.DMA((n * np_,)), pltpu.SemaphoreType.DMA((n,)),
                   *[hbm(a) for a in arrs], *[hbm(a) for a in lands], SDS((8, 128), F32)),
        in_specs=[_HBM] * (2 * n) + [_ANY],
        out_specs=(_SEM, _SEM, _SEM, *([_HBM] * (2 * n)), pl.BlockSpec(memory_space=pltpu.VMEM)),
        input_output_aliases={i: 3 + i for i in range(2 * n)},
        compiler_params=pltpu.CompilerParams(has_side_effects=pltpu.SideEffectType.DATAFLOW_SIDE_EFFECTING),
    )(*[pltpu.with_memory_space_constraint(a, pltpu.HBM) for a in list(arrs) + lands], after)
    return {"sems": outs[:3], "ins": outs[3:3 + n], "lands": outs[3 + n:3 + 2 * n], "token": outs[-1], "scatter": scatter}


def _exchange_wait(h, after, name):
    n = len(h["ins"])
    scatter = h["scatter"]
    after = list(after) if isinstance(after, (list, tuple)) else [after]

    def body(*refs):
        ins, lnd = refs[:n], refs[n:2 * n]
        send_sems, recv_sems, loc_sems = refs[2 * n:2 * n + 3]
        local, sends, arrivals = _exchange_copies(ins, lnd, scatter, send_sems, recv_sems, loc_sems)
        for cp in sends:
            cp.wait_send()
        for cp in arrivals:
            cp.wait_recv()
        for cp in local:
            cp.wait()

    hbm = lambda a: pltpu.HBM(a.shape, a.dtype)
    outs = pl.pallas_call(
        body, name=name,
        out_shape=tuple(hbm(a) for a in list(h["ins"]) + list(h["lands"])),
        in_specs=[_HBM] * (2 * n) + [_SEM] * 3 + [_ANY] * len(after),
        out_specs=tuple([_HBM] * (2 * n)),
        input_output_aliases={i: i for i in range(2 * n)},
        compiler_params=pltpu.CompilerParams(has_side_effects=pltpu.SideEffectType.DATAFLOW_SIDE_EFFECTING),
    )(*h["ins"], *h["lands"], *h["sems"], *after)
    return list(outs[n:])


def _hgather_copies(ins, lands, send_a, recv_a, send_b, recv_b, loc_sems):
    x, y, c = lax.axis_index("x"), lax.axis_index("y"), lax.axis_index("c")
    me = 4 * x + 2 * y + c
    sib = (x, y, 1 - c)
    chips = [(1 - x, y), (x, 1 - y), (1 - x, 1 - y)]
    slot = lambda px, py, pc: 4 * px + 2 * py + pc

    def rcopy(src, dst, ssem, rsem, dev):
        return pltpu.make_async_remote_copy(src_ref=src, dst_ref=dst, send_sem=ssem, recv_sem=rsem, device_id=dev,
                                            device_id_type=pl.DeviceIdType.MESH)

    local, s1, a1, s2, a2 = [], [], [], [], []
    for a in range(len(ins)):
        first = [(sib, slot(x, y, 1 - c))] + [((px, py, c), slot(px, py, c)) for px, py in chips]
        for k, (dev, origin) in enumerate(first if send_a is not None else ()):
            s1.append(rcopy(ins[a], lands[a].at[me], send_a.at[4 * a + k], recv_a.at[4 * a + k], dev))
            a1.append(rcopy(ins[a], lands[a].at[origin], send_a.at[4 * a + k], recv_a.at[4 * a + k], dev))
        if send_a is not None:
            local.append(pltpu.make_async_copy(ins[a], lands[a].at[me], loc_sems.at[a]))
        for k, (px, py) in enumerate(chips if send_b is not None else ()):
            mine, theirs = lands[a].at[slot(px, py, c)], lands[a].at[slot(px, py, 1 - c)]
            s2.append(rcopy(mine, mine, send_b.at[3 * a + k], recv_b.at[3 * a + k], sib))
            a2.append(rcopy(mine, theirs, send_b.at[3 * a + k], recv_b.at[3 * a + k], sib))
    return local, s1, a1, s2, a2


def _hgather_start(arrs, after, name):
    n = len(arrs)
    lands = [lax.empty((NS,) + a.shape, a.dtype) for a in arrs]

    def body(*refs):
        ins, lnd = refs[:n], refs[n:2 * n]
        send_a, recv_a, loc_sems = refs[2 * n + 1:2 * n + 4]
        token = refs[-1]
        local, s1, _, _, _ = _hgather_copies(ins, lnd, send_a, recv_a, None, None, loc_sems)
        for cp in local + s1:
            cp.start()
        token[...] = jnp.zeros_like(token)

    hbm = lambda a: pltpu.HBM(a.shape, a.dtype)
    outs = pl.pallas_call(
        body, name=name,
        out_shape=(pltpu.SemaphoreType.DMA((4 * n,)), pltpu.SemaphoreType.DMA((4 * n,)), pltpu.SemaphoreType.DMA((n,)),
                   *[hbm(a) for a in arrs], *[hbm(a) for a in lands], SDS((8, 128), F32)),
        in_specs=[_HBM] * (2 * n) + [_ANY],
        out_specs=(_SEM, _SEM, _SEM, *([_HBM] * (2 * n)), pl.BlockSpec(memory_space=pltpu.VMEM)),
        input_output_aliases={i: 3 + i for i in range(2 * n)},
        compiler_params=pltpu.CompilerParams(has_side_effects=pltpu.SideEffectType.DATAFLOW_SIDE_EFFECTING),
    )(*[pltpu.with_memory_space_constraint(a, pltpu.HBM) for a in list(arrs) + lands], after)
    return {"sems": outs[:3], "ins": outs[3:3 + n], "lands": outs[3 + n:3 + 2 * n], "token": outs[-1]}


def _hgather_forward(h, after, name):
    n = len(h["ins"])
    after = list(after) if isinstance(after, (list, tuple)) else [after]
    na = len(after)

    def body(*refs):
        ins, lnd = refs[:n], refs[n:2 * n]
        send_a, recv_a, loc_sems = refs[2 * n:2 * n + 3]
        send_b, recv_b = refs[2 * n + 3 + na:2 * n + 5 + na]
        token = refs[-1]
        local, s1, a1, s2, _ = _hgather_copies(ins, lnd, send_a, recv_a, send_b, recv_b, loc_sems)
        for cp in s1:
            cp.wait_send()
        for cp in a1:
            cp.wait_recv()
        for cp in local:
            cp.wait()
        for cp in s2:
            cp.start()
        token[...] = jnp.zeros_like(token)

    hbm = lambda a: pltpu.HBM(a.shape, a.dtype)
    outs = pl.pallas_call(
        body, name=name,
        out_shape=(pltpu.SemaphoreType.DMA((3 * n,)), pltpu.SemaphoreType.DMA((3 * n,)),
                   *[hbm(a) for a in list(h["ins"]) + list(h["lands"])], SDS((8, 128), F32)),
        in_specs=[_HBM] * (2 * n) + [_SEM] * 3 + [_ANY] * na,
        out_specs=(_SEM, _SEM, *([_HBM] * (2 * n)), pl.BlockSpec(memory_space=pltpu.VMEM)),
        input_output_aliases={i: 2 + i for i in range(2 * n)},
        compiler_params=pltpu.CompilerParams(has_side_effects=pltpu.SideEffectType.DATAFLOW_SIDE_EFFECTING),
    )(*h["ins"], *h["lands"], *h["sems"], *after)
    return {"sems": outs[:2], "ins": outs[2:2 + n], "lands": outs[2 + n:2 + 2 * n], "token": outs[-1]}


def _hgather_wait(h, after, name):
    n = len(h["ins"])

    def body(*refs):
        ins, lnd = refs[:n], refs[n:2 * n]
        send_b, recv_b = refs[2 * n:2 * n + 2]
        _, _, _, s2, a2 = _hgather_copies(ins, lnd, None, None, send_b, recv_b, None)
        for cp in s2:
            cp.wait_send()
        for cp in a2:
            cp.wait_recv()

    hbm = lambda a: pltpu.HBM(a.shape, a.dtype)
    outs = pl.pallas_call(
        body, name=name,
        out_shape=tuple(hbm(a) for a in list(h["ins"]) + list(h["lands"])),
        in_specs=[_HBM] * (2 * n) + [_SEM] * 2 + [_ANY],
        out_specs=tuple([_HBM] * (2 * n)),
        input_output_aliases={i: i for i in range(2 * n)},
        compiler_params=pltpu.CompilerParams(has_side_effects=pltpu.SideEffectType.DATAFLOW_SIDE_EFFECTING),
    )(*h["ins"], *h["lands"], *h["sems"], after)
    return list(outs[n:])


_W_NAMES = ("ffn1_pre_g", "ffn1_post_g", "ffn1_w1", "ffn1_w3", "ffn1_w2", "mix_pre_g", "mix_post_g", "w_in", "conv_a_w",
            "conv_a_b", "pool_w", "pool_scale", "sgu_ln_g", "sgu_ln_b", "sgu_ws", "sgu_b", "conv_d_w", "conv_d_b",
            "conv_d_ln_g", "conv_d_ln_b", "w_branch", "w_gate", "b_gate", "w_o", "xa_pre_g", "xa_post_g", "mem_g",
            "xa_wq", "xa_wk", "xa_wv", "xa_wo", "ffn2_pre_g", "ffn2_post_g", "ffn2_w1", "ffn2_w3", "ffn2_w2")
_REP_NAMES = tuple(n for n, _ in _SP_NAMES) + ("pool_w", "sgu_ws", "sgu_b", "conv_a_w", "conv_d_w")


def _t(w):
    return jnp.swapaxes(w, -1, -2)


def _step(x, mem, loss_target, W, M, V):
    L = W["w_in"].shape[0]
    S, D = x.shape[1], x.shape[2]
    x0 = x.reshape(S, D)
    memf = mem.reshape(mem.shape[1], D)
    me = 4 * lax.axis_index("x") + 2 * lax.axis_index("y") + lax.axis_index("c")
    FS = W["ffn1_w2"].shape[1]
    KA, KD = W["conv_a_w"].shape[1], W["conv_d_w"].shape[1]
    CS = W["conv_a_w"].shape[2]

    cat = lambda l, parts: jnp.concatenate([(_t(W[n][l]) if tr else W[n][l]) for n, tr in parts], axis=0).astype(CDT)
    pf1 = [cat(l, (("ffn1_w1", 1), ("ffn1_w3", 1), ("ffn1_w2", 0))) for l in range(L)]
    pf2 = [cat(l, (("ffn2_w1", 1), ("ffn2_w3", 1), ("ffn2_w2", 0))) for l in range(L)]
    pma = [cat(l, (("w_in", 1), ("w_gate", 1))) for l in range(L)]
    pwo = [W["w_o"][l].astype(CDT) for l in range(L)]
    pxa = [cat(l, (("xa_wq", 0), ("xa_wk", 0), ("xa_wv", 0), ("xa_wo", 0))) for l in range(L)]
    wbs = [W["w_branch"][l].astype(CDT) for l in range(L)]
    cws = jnp.concatenate([W["conv_a_w"], W["conv_d_w"]], axis=1).reshape(-1, 128)
    sp_all = jnp.concatenate([W[n] for n, _ in _SP_NAMES], axis=1)
    bsc_all = W["sgu_b"][..., None]

    (cwg,) = _exchange([cws], False, "gather_conv_w")
    cwf = cwg.reshape(NS, L, KA + KD, CS).transpose(1, 2, 0, 3).reshape(L, KA + KD, NS * CS)

    def gather_start(l, after):
        return _hgather_start([pf1[l], pf2[l], pma[l], pwo[l], pxa[l], wbs[l]], after, f"gather_start_{l}")

    def gather_rest(h, after, tag):
        mid = _hgather_forward(h, after, f"gather_forward_{tag}")
        return _hgather_wait(mid, mid["token"], f"gather_wait_{tag}")

    packs = [None] * L
    first = [_hgather_start([pf1[0]], cwg, "gather_start_0a")]
    saved = []
    xc = x0
    for l in range(L):
        if l == 0:
            rest = [p[k] for p in (pf1, pf2, pma, pwo, pxa, wbs) for k in range(L) if not (p is pf1 and k == 0)]
            (gf1,) = gather_rest(first[0], [sp_all, bsc_all] + rest, "0a")
            first.append(_hgather_start([pma[0], pwo[0], wbs[0]], gf1, "gather_start_0b"))
        else:
            gf1, gf2, gma, gwo, gxa, gwb = packs[l]
        sp = sp_all[l:l + 1]
        cwa, cwd = cwf[l, :KA], cwf[l, KA:]
        wp, ws, bsc = W["pool_w"][l], W["sgu_ws"][l], bsc_all[l]
        s = {"x0": xc}
        nxt = gather_start(l + 1, gf1) if 0 < l < L - 1 else None
        xc, s["hb1"], s["a1"], s["b1"], s["y1"] = _ffn_fwd(xc, sp, "ffn1_pre_g", "ffn1_post_g", gf1,
                                                            (first[1] if l == 0 else nxt)["token"] if l == 0 or nxt else sp)
        s["x1"] = xc
        if l == 0:
            gma, gwo, gwb = gather_rest(first[1], xc, "0b")
            first.append(_hgather_start([pxa[0]], gma, "gather_start_0c"))
            first.append(_hgather_start([pf2[0]], first[2]["token"], "gather_start_0d"))
            nxt = gather_start(1, first[3]["token"]) if L > 1 else None
        s["hbm"], s["z"], s["g"] = _mix_in(xc, sp, gma, (nxt or first[3])["token"] if l == 0 else sp)
        s["ma"] = _mixA_fwd(s["z"], cwa, sp)
        s["mb"] = _mixB_fwd(s["z"], wp, sp)
        s["mc"] = _mixC_fwd(s["z"], ws, bsc, sp)
        s["yd"] = _mixD_conv_fwd(s["z"], cwd, sp)
        xc, s["md"], s["yk"], s["mg"], s["mo"] = _merge_fwd(s["ma"], s["mb"], s["mc"], s["yd"], s["g"], gwb, gwo, xc, sp)
        s["x2"] = xc
        if l == 0:
            (gxa,) = gather_rest(first[2], xc, "0c")
        s["mn"], s["k"], s["v"] = _xa_kv(memf, sp, gxa)
        xc, s["hbx"], s["q"], s["o"], s["po"] = _xa_fwd(xc, s["k"], s["v"], sp, gxa)
        s["x3"] = xc
        if l == 0:
            (gf2,) = gather_rest(first[3], xc, "0d")
            packs[0] = (gf1, gf2, gma, gwo, gxa, gwb)
        mid = _hgather_forward(nxt, xc, f"gather_forward_{l + 1}") if nxt and l > 0 else None
        xc, s["hb2"], s["a2"], s["b2"], s["y2"] = _ffn_fwd(xc, sp, "ffn2_pre_g", "ffn2_post_g", gf2,
                                                            mid["token"] if mid else sp)
        saved.append(s)
        if nxt:
            mid = mid or _hgather_forward(nxt, xc, f"gather_forward_{l + 1}")
            packs[l + 1] = _hgather_wait(mid, xc, f"gather_wait_{l + 1}")

    dx, lpart = _loss_head(xc, loss_target.reshape(S, D))
    loss = lax.psum(lpart[0, 0], ("x", "y", "c"))

    rep = {n: [None] * L for n in _REP_NAMES}
    summed = [dict() for _ in range(L)]
    pending = None
    last = []
    for l in reversed(range(L)):
        gf1, gf2, gma, gwo, gxa, gwb = packs[l]
        sp = sp_all[l:l + 1]
        cwa, cwd = cwf[l, :KA], cwf[l, KA:]
        wp, ws, bsc = W["pool_w"][l], W["sgu_ws"][l], bsc_all[l]
        s = saved[l]

        dx, dyb, da, db, gp = _ffn_bwd_act(dx, s["x3"], s["y2"], s["a2"], s["b2"], sp, "ffn2_pre_g", "ffn2_post_g", gf2,
                                           pending[1]["token"] if pending else sp)
        rep["ffn2_pre_g"][l], rep["ffn2_post_g"][l] = gp[0], gp[1]
        d_f2 = _ffn_bwd_w(s["hb2"], dyb, s["a2"], s["b2"], da, db, sp)
        if l == 0:
            last.append(_exchange_start([d_f2], (True,), dx, "scatter_start_0a"))

        dx, dpo, dq, dk, dv, gp = _xa_bwd_act(dx, s["x2"], s["po"], s["q"], s["k"], s["v"], sp, gxa,
                                              last[-1]["token"] if last else sp)
        rep["xa_pre_g"][l], rep["xa_post_g"][l] = gp[0], gp[1]
        d_xa = _xa_bwd_w(s["hbx"], dq, s["o"], dpo, s["mn"], dk, dv)
        rep["mem_g"][l] = _xa_kv_bwd(memf, dk, dv, sp, gxa)[0]
        if l == 0:
            last.append(_exchange_start([d_xa], (True,), dx, "scatter_start_0b"))

        dmo, dm, dgp, dyk, gp = _merge_bwd_act(dx, s["mo"], s["g"], s["yk"], gwb, gwo, sp, last[-1]["token"] if last else sp)
        rep["mix_post_g"][l] = gp[0]
        d_wb, d_wo = _merge_bwd_w(s["ma"], s["mb"], s["mc"], s["md"], dyk, s["mg"], dmo)
        dz, dcw, gp = _mixA_bwd(s["z"], dm, cwa, sp)
        rep["conv_a_w"][l], rep["conv_a_b"][l] = dcw, gp[0]
        dz, dwp, gp = _mixB_bwd(s["z"], dm, wp, sp, dz)
        rep["pool_w"][l], rep["pool_scale"][l] = dwp, gp[0]
        dz, dws, dbs, gp = _mixC_bwd(s["z"], dm, ws, bsc, sp, dz)
        rep["sgu_ws"][l], rep["sgu_b"][l], rep["sgu_ln_g"][l], rep["sgu_ln_b"][l] = dws, dbs[:, :, 0], gp[0], gp[1]
        dyd, gp = _mixD_ln_bwd(dm, s["yd"], sp)
        rep["conv_d_ln_g"][l], rep["conv_d_ln_b"][l] = gp[0], gp[1]
        dz, dcw, gp = _mixD_conv_bwd(s["z"], dyd, cwd, dz)
        rep["conv_d_w"][l], rep["conv_d_b"][l] = dcw, gp[0]
        dx, gp = _mix_in_bwd_act(dz, dgp, dx, s["x1"], sp, gma)
        rep["mix_pre_g"][l] = gp[0]
        d_ma, dbg = _mix_in_bwd_w(dz, dgp, s["hbm"])
        rep["b_gate"][l] = dbg[:, 0, :].reshape(-1)
        if l == 0:
            last.append(_exchange_start([d_ma, d_wo, d_wb], (True,) * 3, dx, "scatter_start_0c"))

        dx, dyb, da, db, gp = _ffn_bwd_act(dx, s["x0"], s["y1"], s["a1"], s["b1"], sp, "ffn1_pre_g", "ffn1_post_g", gf1,
                                           last[-1]["token"] if last else sp)
        rep["ffn1_pre_g"][l], rep["ffn1_post_g"][l] = gp[0], gp[1]
        flat = jnp.concatenate([rep[n][l].reshape(-1) for n in _REP_NAMES])
        flat = jnp.pad(flat, (0, -flat.size % 2048)).reshape(-1, 128).astype(CDT)
        if l == 0:
            last.append(_exchange_start([flat], (False,), dx, "scatter_start_0d"))
        d_f1 = _ffn_bwd_w(s["hb1"], dyb, s["a1"], s["b1"], da, db, last[-1]["token"] if last else sp)

        if pending:
            r = _exchange_wait(pending[1], dx, f"scatter_wait_{pending[0]}")
            summed[pending[0]] = dict(zip(("f1", "f2", "ma", "wo", "xa", "wb", "flat"), r))
        if l == 0:
            last.append(_exchange_start([d_f1], (True,), dx, "scatter_start_0e"))
        else:
            pending = (l, _exchange_start([d_f1, d_f2, d_ma, d_wo, d_xa, d_wb, flat], (True,) * 6 + (False,), dx,
                                          f"scatter_start_{l}"))

    pack_shape = {"f1": (3 * FS, D), "f2": (3 * FS, D), "ma": (2 * MW, D), "wo": (GW, D), "xa": (4 * GW, D),
                  "wb": (4 * MW, GW), "flat": tuple(flat.shape)}
    stk = {k: lax.empty((L,) + s, F32) for k, s in pack_shape.items()}

    def land(k, r, l):
        stk[k] = _slot_sum_into(stk[k], r.reshape((NS,) + pack_shape[k]), l)

    for l in range(1, L):
        for k, r in summed[l].items():
            land(k, r, l)
    (r,) = _exchange_wait(last[0], dx, "scatter_wait_0a")
    land("f2", r, 0)
    (r,) = _exchange_wait(last[1], dx, "scatter_wait_0b")
    land("xa", r, 0)

    G, deltas, new_m, new_v = {}, {}, {}, {}

    def update_block(n, k, blk, transposed):
        tr = _t if transposed else (lambda a: a)
        out = _adamw_block(tr(W[n]), stk[k], tr(M[n]), tr(V[n]), blk)
        G[n], deltas[n], new_m[n], new_v[n] = (tr(a) for a in out)
        return deltas[n]

    def update(n):
        deltas[n], new_m[n], new_v[n] = _adamw(W[n], G[n], M[n], V[n])
        return deltas[n]

    done = [update_block("ffn2_w1", "f2", 0, True), update_block("ffn2_w3", "f2", 1, True),
            update_block("ffn2_w2", "f2", 2, False)]
    done += [update_block(n, "xa", i, False) for i, n in enumerate(("xa_wq", "xa_wk", "xa_wv", "xa_wo"))]
    r = _exchange_wait(last[2], done + [stk[k] for k in ("f1", "ma", "wo", "wb", "flat")], "scatter_wait_0c")
    for k, v in zip(("ma", "wo", "wb"), r):
        land(k, v, 0)
    G["w_in"], G["w_gate"] = _t(stk["ma"][:, :MW]), _t(stk["ma"][:, MW:])
    G["w_branch"] = stk["wb"].reshape(W["w_branch"].shape)
    done = [update("w_in"), update("w_gate"), update("w_branch"), update_block("w_o", "wo", 0, False)]
    (r,) = _exchange_wait(last[3], done, "scatter_wait_0d")
    land("flat", r, 0)

    tot = [stk["flat"][l].reshape(-1) for l in range(L)]
    off = 0
    for n in _REP_NAMES:
        shape = (KA, NS * CS) if n == "conv_a_w" else (KD, NS * CS) if n == "conv_d_w" else W[n].shape[1:]
        size = 1
        for d in shape:
            size *= d
        G[n] = jnp.stack([tot[l][off:off + size].reshape(shape) for l in range(L)])
        off += size
    for n in ("conv_a_w", "conv_d_w"):
        G[n] = lax.dynamic_slice_in_dim(G[n], me * CS, CS, axis=2)
    done = [update(n) for n in _REP_NAMES]

    (r,) = _exchange_wait(last[4], done, "scatter_wait_0e")
    land("f1", r, 0)
    update_block("ffn1_w1", "f1", 0, True)
    update_block("ffn1_w3", "f1", 1, True)
    update_block("ffn1_w2", "f1", 2, False)
    grad_x = dx.reshape(x.shape)
    return (loss, grad_x, *[G[n] for n in _W_NAMES], *[deltas[n] for n in _W_NAMES],
            *[new_m[n] for n in _W_NAMES], *[new_v[n] for n in _W_NAMES])


def kernel(x, mem, ffn1_pre_g, ffn1_post_g, ffn1_w1, ffn1_w3, ffn1_w2, mix_pre_g, mix_post_g, w_in, conv_a_w, conv_a_b, pool_w, pool_scale, sgu_ln_g, sgu_ln_b, sgu_ws, sgu_b, conv_d_w, conv_d_b, conv_d_ln_g, conv_d_ln_b, w_branch, w_gate, b_gate, w_o, xa_pre_g, xa_post_g, mem_g, xa_wq, xa_wk, xa_wv, xa_wo, ffn2_pre_g, ffn2_post_g, ffn2_w1, ffn2_w3, ffn2_w2, loss_target, m_ffn1_pre_g, m_ffn1_post_g, m_ffn1_w1, m_ffn1_w3, m_ffn1_w2, m_mix_pre_g, m_mix_post_g, m_w_in, m_conv_a_w, m_conv_a_b, m_pool_w, m_pool_scale, m_sgu_ln_g, m_sgu_ln_b, m_sgu_ws, m_sgu_b, m_conv_d_w, m_conv_d_b, m_conv_d_ln_g, m_conv_d_ln_b, m_w_branch, m_w_gate, m_b_gate, m_w_o, m_xa_pre_g, m_xa_post_g, m_mem_g, m_xa_wq, m_xa_wk, m_xa_wv, m_xa_wo, m_ffn2_pre_g, m_ffn2_post_g, m_ffn2_w1, m_ffn2_w3, m_ffn2_w2, v_ffn1_pre_g, v_ffn1_post_g, v_ffn1_w1, v_ffn1_w3, v_ffn1_w2, v_mix_pre_g, v_mix_post_g, v_w_in, v_conv_a_w, v_conv_a_b, v_pool_w, v_pool_scale, v_sgu_ln_g, v_sgu_ln_b, v_sgu_ws, v_sgu_b, v_conv_d_w, v_conv_d_b, v_conv_d_ln_g, v_conv_d_ln_b, v_w_branch, v_w_gate, v_b_gate, v_w_o, v_xa_pre_g, v_xa_post_g, v_mem_g, v_xa_wq, v_xa_wk, v_xa_wv, v_xa_wo, v_ffn2_pre_g, v_ffn2_post_g, v_ffn2_w1, v_ffn2_w3, v_ffn2_w2):
    args = dict(locals())
    W = {n: args[n] for n in _W_NAMES}
    M = {n: args["m_" + n] for n in _W_NAMES}
    V = {n: args["v_" + n] for n in _W_NAMES}
    return _step(x, mem, loss_target, W, M, V)
```

```python
import jax
import jax.numpy as jnp
from jax import lax
from jax.experimental import pallas as pl
from jax.experimental.pallas import tpu as pltpu

F32 = jnp.float32
CDT = jnp.bfloat16
EPS = 1e-6
NS = 8
GW = 128
MW = 512
CHUNK = 64
XA_HEADS = 4
POOL_WINDOWS = (2, 4, 8, 16)
VMEM_LIMIT = 56 * 1024 * 1024
ADAM_LR, ADAM_B1, ADAM_B2, ADAM_EPS, ADAM_WD, ADAM_STEP = 0.001, 0.9, 0.999, 1e-08, 0.01, 10

SDS = jax.ShapeDtypeStruct

_SP_NAMES = (("ffn1_pre_g", 1024), ("ffn1_post_g", 1024), ("mix_pre_g", 1024), ("mix_post_g", 1024),
             ("xa_pre_g", 1024), ("xa_post_g", 1024), ("mem_g", 1024), ("ffn2_pre_g", 1024), ("ffn2_post_g", 1024),
             ("conv_a_b", 512), ("pool_scale", 512), ("sgu_ln_g", 512), ("sgu_ln_b", 512), ("conv_d_b", 512),
             ("conv_d_ln_g", 512), ("conv_d_ln_b", 512), ("b_gate", 4096))
_SP = {}
_off = 0
for _n, _w in _SP_NAMES:
    _SP[_n] = (_off, _w)
    _off += _w
_SP_TOTAL = _off


def _call(body, name, grid, in_specs, out_specs, out_shape, scratch=(), aliases=None):
    return pl.pallas_call(
        body, name=name, grid=grid, in_specs=in_specs, out_specs=out_specs, out_shape=out_shape,
        scratch_shapes=list(scratch), input_output_aliases=aliases or {},
        compiler_params=pltpu.CompilerParams(dimension_semantics=("arbitrary",) * len(grid),
                                             vmem_limit_bytes=VMEM_LIMIT))


def _nn(a, b):
    return lax.dot_general(a, b, (((1,), (0,)), ((), ())), preferred_element_type=F32)


def _nt(a, b):
    return lax.dot_general(a, b, (((1,), (1,)), ((), ())), preferred_element_type=F32)


def _tn(a, b):
    return lax.dot_general(a, b, (((0,), (0,)), ((), ())), preferred_element_type=F32)


def _rms(x):
    r = lax.rsqrt(jnp.mean(x * x, axis=-1, keepdims=True) + EPS)
    return x * r, r


def _rms_bwd(n, r, g, dout):
    dn = dout * g
    dx = r * (dn - n * jnp.mean(dn * n, axis=-1, keepdims=True))
    return dx, jnp.sum(dout * n, axis=0, keepdims=True)


def _ln(y):
    mu = jnp.mean(y, axis=-1, keepdims=True)
    yc = y - mu
    rs = lax.rsqrt(jnp.mean(yc * yc, axis=-1, keepdims=True) + EPS)
    return yc * rs, rs


def _ln_bwd(xh, rs, dxh):
    return rs * (dxh - jnp.mean(dxh, axis=-1, keepdims=True) - xh * jnp.mean(dxh * xh, axis=-1, keepdims=True))


def _silu_parts(a):
    s = jax.nn.sigmoid(a)
    sl = a * s
    return sl, s + sl * (1.0 - s)


_GELU_C = 0.7978845608028654
_GELU_A = 0.044715


def _gelu(x):
    return 0.5 * x * (1.0 + jnp.tanh(_GELU_C * (x + _GELU_A * x * x * x)))


def _gelu_parts(x):
    t = jnp.tanh(_GELU_C * (x + _GELU_A * x * x * x))
    g = 0.5 * x * (1.0 + t)
    dg = 0.5 * (1.0 + t) + 0.5 * x * (1.0 - t * t) * _GELU_C * (1.0 + 3.0 * _GELU_A * x * x)
    return g, dg


def _spspec(name, width, imap):
    off = _SP[name][0]
    assert off % width == 0
    return pl.BlockSpec((1, width), lambda *a: (0, off // width + imap(*a)))


def _zero(*a):
    return 0


def _row_once(tm, d):
    return pl.BlockSpec((tm, d), lambda i, j: (i, 0), pipeline_mode=pl.Buffered(1))


FFN_SG = 2


def _ffn_fwd(x, sp, pre, post, pf, dep):
    S, D = x.shape
    FS = pf.shape[1] // 3
    TM = min(512, S)
    SG, NG, W = FFN_SG, NS // FFN_SG, FFN_SG * FS

    def body(x_ref, pg_ref, qg_ref, w1_ref, w3_ref, w2_ref, dep_ref, xo_ref, hb_ref, a_ref, b_ref, y_ref, hb_s, acc):
        j = pl.program_id(1)

        @pl.when(j == 0)
        def _():
            n, _ = _rms(x_ref[...])
            hb = (n * pg_ref[...]).astype(CDT)
            hb_s[...] = hb
            hb_ref[...] = hb
            acc[...] = jnp.zeros_like(acc)

        hb = hb_s[...]
        a = _nt(hb, w1_ref[...].reshape(W, D))
        b = _nt(hb, w3_ref[...].reshape(W, D))
        a_ref[...] = a.astype(CDT)
        b_ref[...] = b.astype(CDT)
        u = (a * jax.nn.sigmoid(a) * b).astype(CDT)
        acc[...] += _nn(u, w2_ref[...].reshape(W, D))

        @pl.when(j == NG - 1)
        def _():
            y = acc[...]
            y_ref[...] = y.astype(CDT)
            n, _ = _rms(y)
            xo_ref[...] = x_ref[...] + 0.5 * (n * qg_ref[...])

    row1 = pl.BlockSpec((TM, D), lambda i, j: (i, 0))
    grp = lambda i, j: (j, i, 0)
    return _call(
        body, "ffn_fwd", (S // TM, NG),
        [row1, _spspec(pre, D, _zero), _spspec(post, D, _zero),
         pl.BlockSpec((SG, FS, D), lambda i, j: (j, 0, 0)), pl.BlockSpec((SG, FS, D), lambda i, j: (j, 1, 0)),
         pl.BlockSpec((SG, FS, D), lambda i, j: (j, 2, 0)), pl.BlockSpec(memory_space=pl.ANY)],
        [row1, row1, pl.BlockSpec((None, TM, W), grp), pl.BlockSpec((None, TM, W), grp), row1],
        [SDS((S, D), F32), SDS((S, D), CDT), SDS((NG, S, W), CDT), SDS((NG, S, W), CDT), SDS((S, D), CDT)],
        [pltpu.VMEM((TM, D), CDT), pltpu.VMEM((TM, D), F32)])(x, sp, sp, pf, pf, pf, dep)


def _ffn_bwd_act(dxo, x, y, a, b, sp, pre, post, pf, dep):
    S, D = x.shape
    FS = pf.shape[1] // 3
    TM = min(512, S)
    SG, NG, W = FFN_SG, NS // FFN_SG, FFN_SG * FS

    def body(dxo_ref, x_ref, y_ref, a_ref, b_ref, pg_ref, qg_ref, w1_ref, w3_ref, w2_ref, dep_ref,
             dx_ref, dyb_ref, da_ref, db_ref, gp_ref, dyb_s, acc):
        i = pl.program_id(0)
        j = pl.program_id(1)

        @pl.when((i == 0) & (j == 0))
        def _():
            gp_ref[...] = jnp.zeros_like(gp_ref)

        @pl.when(j == 0)
        def _():
            n, r = _rms(y_ref[...].astype(F32))
            dy, dg = _rms_bwd(n, r, qg_ref[...], 0.5 * dxo_ref[...])
            dyb = dy.astype(CDT)
            dyb_s[...] = dyb
            dyb_ref[...] = dyb
            gp_ref[1:2, :] += dg
            acc[...] = jnp.zeros_like(acc)

        sl, dsl = _silu_parts(a_ref[...].astype(F32))
        du = _nt(dyb_s[...], w2_ref[...].reshape(W, D))
        db = (du * sl).astype(CDT)
        da = (du * b_ref[...].astype(F32) * dsl).astype(CDT)
        da_ref[...] = da
        db_ref[...] = db
        acc[...] += _nn(da, w1_ref[...].reshape(W, D)) + _nn(db, w3_ref[...].reshape(W, D))

        @pl.when(j == NG - 1)
        def _():
            n, r = _rms(x_ref[...])
            dx, dg = _rms_bwd(n, r, pg_ref[...], acc[...])
            dx_ref[...] = dxo_ref[...] + dx
            gp_ref[0:1, :] += dg

    row = lambda i, j: (i, 0)
    grp = lambda i, j: (j, i, 0)
    return _call(
        body, "ffn_bwd_act", (S // TM, NG),
        [pl.BlockSpec((TM, D), row), pl.BlockSpec((TM, D), row), pl.BlockSpec((TM, D), row),
         pl.BlockSpec((None, TM, W), grp), pl.BlockSpec((None, TM, W), grp),
         _spspec(pre, D, _zero), _spspec(post, D, _zero),
         pl.BlockSpec((SG, FS, D), lambda i, j: (j, 0, 0)), pl.BlockSpec((SG, FS, D), lambda i, j: (j, 1, 0)),
         pl.BlockSpec((SG, FS, D), lambda i, j: (j, 2, 0)), pl.BlockSpec(memory_space=pl.ANY)],
        [pl.BlockSpec((TM, D), row), pl.BlockSpec((TM, D), row), pl.BlockSpec((None, TM, W), grp),
         pl.BlockSpec((None, TM, W), grp), pl.BlockSpec((8, D), lambda i, j: (0, 0))],
        [SDS((S, D), F32), SDS((S, D), CDT), SDS((NG, S, W), CDT), SDS((NG, S, W), CDT), SDS((8, D), F32)],
        [pltpu.VMEM((TM, D), CDT), pltpu.VMEM((TM, D), F32)])(dxo, x, y, a, b, sp, sp, pf, pf, pf, dep)


def _ffn_bwd_w(hb, dyb, a, b, da, db, dep):
    S, D = hb.shape
    SG, NG = FFN_SG, NS // FFN_SG
    W = a.shape[2]
    FS = W // SG
    TK = min(1024, S)
    NK = S // TK

    def body(hb_ref, dyb_ref, a_ref, b_ref, da_ref, db_ref, dep_ref, g_ref, acc):
        k = pl.program_id(1)

        @pl.when(k == 0)
        def _():
            acc[...] = jnp.zeros_like(acc)

        af = a_ref[...].astype(F32)
        u = (af * jax.nn.sigmoid(af) * b_ref[...].astype(F32)).astype(CDT)
        hb = hb_ref[...]
        acc[0:W, :] += _tn(da_ref[...], hb)
        acc[W:2 * W, :] += _tn(db_ref[...], hb)
        acc[2 * W:3 * W, :] += _tn(u, dyb_ref[...])

        @pl.when(k == NK - 1)
        def _():
            for s in range(SG):
                for r in range(3):
                    g_ref[s, r * FS:(r + 1) * FS, :] = acc[r * W + s * FS:r * W + (s + 1) * FS, :].astype(CDT)

    row = lambda j, k: (k, 0)
    grp = lambda j, k: (j, k, 0)
    return _call(
        body, "ffn_bwd_w", (NG, NK),
        [pl.BlockSpec((TK, D), row), pl.BlockSpec((TK, D), row)] + [pl.BlockSpec((None, TK, W), grp)] * 4 + [_ANY],
        pl.BlockSpec((SG, 3 * FS, D), lambda j, k: (j, 0, 0)),
        SDS((NS, 3 * FS, D), CDT),
        [pltpu.VMEM((3 * W, D), F32)])(hb, dyb, a, b, da, db, dep)


def _mix_in(x, sp, pma, dep):
    S, D = x.shape
    TM = min(1024, S)

    def body(x_ref, pg_ref, bg_ref, wi_ref, wg_ref, dep_ref, hb_ref, z_ref, g_ref, hb_s):
        @pl.when(pl.program_id(1) == 0)
        def _():
            n, _ = _rms(x_ref[...])
            hb = (n * pg_ref[...]).astype(CDT)
            hb_s[...] = hb
            hb_ref[...] = hb

        hb = hb_s[...]
        z_ref[...] = _nt(hb, wi_ref[...]).astype(CDT)
        g_ref[...] = jax.nn.sigmoid(_nt(hb, wg_ref[...]) + bg_ref[...]).astype(CDT)

    return _call(
        body, "mix_in", (S // TM, NS),
        [_row_once(TM, D), _spspec("mix_pre_g", D, _zero), _spspec("b_gate", MW, lambda i, j: j),
         pl.BlockSpec((None, MW, D), lambda i, j: (j, 0, 0)), pl.BlockSpec((None, MW, D), lambda i, j: (j, 1, 0)), _ANY],
        [_row_once(TM, D), pl.BlockSpec((None, TM, MW), lambda i, j: (j, i, 0)),
         pl.BlockSpec((None, TM, MW), lambda i, j: (j // 2, i, j % 2))],
        [SDS((S, D), CDT), SDS((NS, S, MW), CDT), SDS((4, S, D), CDT)],
        [pltpu.VMEM((TM, D), CDT)])(x, sp, sp, pma, pma, dep)


def _mix_in_bwd_act(dz, dgp, dxr, x, sp, pma):
    S, D = x.shape
    TM = min(1024, S)

    def body(dz_ref, dg_ref, dxr_ref, x_ref, pg_ref, w_ref, dx_ref, gp_ref, acc):
        i = pl.program_id(0)
        j = pl.program_id(1)

        @pl.when((i == 0) & (j == 0))
        def _():
            gp_ref[...] = jnp.zeros_like(gp_ref)

        @pl.when(j == 0)
        def _():
            acc[...] = jnp.zeros_like(acc)

        acc[...] += _nn(jnp.concatenate([dz_ref[...], dg_ref[...]], axis=1), w_ref[...])

        @pl.when(j == NS - 1)
        def _():
            n, r = _rms(x_ref[...])
            dx, dg = _rms_bwd(n, r, pg_ref[...], acc[...])
            dx_ref[...] = dxr_ref[...] + dx
            gp_ref[0:1, :] += dg

    return _call(
        body, "mix_in_bwd_act", (S // TM, NS),
        [pl.BlockSpec((None, TM, MW), lambda i, j: (j, i, 0)), pl.BlockSpec((None, TM, MW), lambda i, j: (j // 2, i, j % 2)),
         _row_once(TM, D), _row_once(TM, D), _spspec("mix_pre_g", D, _zero),
         pl.BlockSpec((None, 2 * MW, D), lambda i, j: (j, 0, 0))],
        [_row_once(TM, D), pl.BlockSpec((8, D), lambda i, j: (0, 0))],
        [SDS((S, D), F32), SDS((8, D), F32)],
        [pltpu.VMEM((TM, D), F32)])(dz, dgp, dxr, x, sp, pma)


def _mix_in_bwd_w(dz, dgp, hb):
    S, D = hb.shape
    TK = min(2048, S)
    NK = S // TK

    def body(dz_ref, dg_ref, hb_ref, g_ref, bg_ref, acc):
        k = pl.program_id(1)

        @pl.when(k == 0)
        def _():
            acc[...] = jnp.zeros_like(acc)
            bg_ref[...] = jnp.zeros_like(bg_ref)

        hb = hb_ref[...]
        dg = dg_ref[...]
        acc[0:MW, :] += _tn(dz_ref[...], hb)
        acc[MW:2 * MW, :] += _tn(dg, hb)
        bg_ref[0:1, :] += jnp.sum(dg.astype(F32), axis=0, keepdims=True)

        @pl.when(k == NK - 1)
        def _():
            g_ref[...] = acc[...].astype(CDT)

    return _call(
        body, "mix_in_bwd_w", (NS, NK),
        [pl.BlockSpec((None, TK, MW), lambda j, k: (j, k, 0)), pl.BlockSpec((None, TK, MW), lambda j, k: (j // 2, k, j % 2)),
         pl.BlockSpec((TK, D), lambda j, k: (k, 0))],
        [pl.BlockSpec((None, 2 * MW, D), lambda j, k: (j, 0, 0)), pl.BlockSpec((None, 8, MW), lambda j, k: (j, 0, 0))],
        [SDS((NS, 2 * MW, D), CDT), SDS((NS, 8, MW), F32)],
        [pltpu.VMEM((2 * MW, D), F32)])(dz, dgp, hb)


def _causal_taps(pad_ref, i, ch, halo, k_taps, lanes=slice(None)):
    val = pad_ref[pl.ds(pl.multiple_of(i * ch, 8), ch + halo), lanes]
    base = {}
    out = []
    for k in range(k_taps):
        q, r = divmod(k_taps - 1 - k, 8)
        if r not in base:
            base[r] = pltpu.roll(val, r, 0) if r else val
        out.append((k, base[r][halo - 8 * q:halo - 8 * q + ch, :]))
    return out


def _anti_taps(pad_ref, i, ch, halo, k_taps, lanes=slice(None)):
    val = pad_ref[pl.ds(pl.multiple_of(i * ch, 8), ch + halo), lanes]
    n = ch + halo
    base = {}
    out = []
    for k in range(k_taps):
        q, r = divmod(k_taps - 1 - k, 8)
        if r not in base:
            base[r] = pltpu.roll(val, n - r, 0) if r else val
        out.append((k, base[r][8 * q:8 * q + ch, :]))
    return out


def _conv_geometry(S, k_taps):
    halo = 8 * ((k_taps - 1 + 7) // 8)
    ch = min(256, S)
    return halo, ch, S // ch


def _rows(i, ch):
    return pl.ds(pl.multiple_of(i * ch, ch), ch)


def _mixA_fwd(z, cw, sp):
    S = z.shape[1]
    K = cw.shape[0]
    H, CH, NCH = _conv_geometry(S, K)

    def body(z_ref, w_ref, b_ref, o_ref, pad):
        pad[0:H, :] = jnp.zeros((H, GW), F32)

        def fill(i, c):
            r = _rows(i, CH)
            pad[pl.ds(pl.multiple_of(i * CH + H, 8), CH), :] = z_ref[2, r, :].astype(F32) * z_ref[0, r, :].astype(F32)
            return c

        lax.fori_loop(0, NCH, fill, 0)

        def conv(i, c):
            r = _rows(i, CH)
            acc = jnp.zeros((CH, GW), F32)
            for k, sh in _causal_taps(pad, i, CH, H, K):
                acc = acc + w_ref[k:k + 1, :] * sh
            o_ref[r, :] = (z_ref[1, r, :].astype(F32) * (acc + b_ref[...])).astype(CDT)
            return c

        lax.fori_loop(0, NCH, conv, 0)

    return _call(
        body, "mixA_fwd", (MW // GW,),
        [pl.BlockSpec((3, S, GW), lambda c: (0, 0, c)), pl.BlockSpec((K, GW), lambda c: (0, c)),
         _spspec("conv_a_b", GW, lambda c: c)],
        pl.BlockSpec((S, GW), lambda c: (0, c)), SDS((S, MW), CDT),
        [pltpu.VMEM((H + S, GW), F32)])(z, cw, sp)


def _mixA_bwd(z, dm, cw, sp):
    S = z.shape[1]
    K = cw.shape[0]
    H, CH, NCH = _conv_geometry(S, K)

    def body(z_ref, dm_ref, w_ref, b_ref, dz_ref, dw_ref, db_ref, pad, dpad, dw_s):
        pad[0:H, :] = jnp.zeros((H, GW), F32)
        dpad[pl.ds(S, H), :] = jnp.zeros((H, GW), F32)
        dw_s[...] = jnp.zeros_like(dw_s)
        db_ref[...] = jnp.zeros_like(db_ref)

        def fill(i, c):
            r = _rows(i, CH)
            pad[pl.ds(pl.multiple_of(i * CH + H, 8), CH), :] = z_ref[2, r, :].astype(F32) * z_ref[0, r, :].astype(F32)
            return c

        lax.fori_loop(0, NCH, fill, 0)

        def p1(i, c):
            r = _rows(i, CH)
            taps = _causal_taps(pad, i, CH, H, K)
            acc = jnp.zeros((CH, GW), F32)
            for k, sh in taps:
                acc = acc + w_ref[k:k + 1, :] * sh
            dmf = dm_ref[r, :].astype(F32)
            dz_ref[1, r, :] = (dmf * (acc + b_ref[...])).astype(CDT)
            dc = dmf * z_ref[1, r, :].astype(F32)
            dpad[r, :] = dc
            for k, sh in taps:
                dw_s[k:k + 1, :] += jnp.sum(dc * sh, axis=0, keepdims=True)
            db_ref[0:1, :] += jnp.sum(dc, axis=0, keepdims=True)
            return c

        lax.fori_loop(0, NCH, p1, 0)

        def p2(i, c):
            r = _rows(i, CH)
            dq = jnp.zeros((CH, GW), F32)
            for k, sh in _anti_taps(dpad, i, CH, H, K):
                dq = dq + w_ref[k:k + 1, :] * sh
            dz_ref[0, r, :] = (dq * z_ref[2, r, :].astype(F32)).astype(CDT)
            dz_ref[2, r, :] = (dq * z_ref[0, r, :].astype(F32)).astype(CDT)
            return c

        lax.fori_loop(0, NCH, p2, 0)
        dw_ref[...] = dw_s[0:K, :]

    return _call(
        body, "mixA_bwd", (MW // GW,),
        [pl.BlockSpec((3, S, GW), lambda c: (0, 0, c)), pl.BlockSpec((None, S, GW), lambda c: (0, 0, c)),
         pl.BlockSpec((K, GW), lambda c: (0, c)), _spspec("conv_a_b", GW, lambda c: c)],
        [pl.BlockSpec((3, S, GW), lambda c: (0, 0, c)), pl.BlockSpec((K, GW), lambda c: (0, c)),
         pl.BlockSpec((8, GW), lambda c: (0, c))],
        [SDS((NS, S, MW), CDT), SDS((K, MW), F32), SDS((8, MW), F32)],
        [pltpu.VMEM((H + S, GW), F32), pltpu.VMEM((S + H, GW), F32), pltpu.VMEM((8 * ((K + 7) // 8), GW), F32)])(z, dm, cw, sp)


def _mixD_conv_fwd(z, cw, sp):
    S = z.shape[1]
    K = cw.shape[0]
    H, CH, NCH = _conv_geometry(S, K)

    def body(z_ref, w_ref, b_ref, o_ref, pad):
        pad[0:H, :] = jnp.zeros((H, GW), F32)

        def fill(i, c):
            r = _rows(i, CH)
            pad[pl.ds(pl.multiple_of(i * CH + H, 8), CH), :] = (
                z_ref[0, r, :].astype(F32) * jax.nn.sigmoid(z_ref[1, r, :].astype(F32)))
            return c

        lax.fori_loop(0, NCH, fill, 0)

        def conv(i, c):
            acc = jnp.zeros((CH, GW), F32)
            for k, sh in _causal_taps(pad, i, CH, H, K):
                acc = acc + w_ref[k:k + 1, :] * sh
            o_ref[_rows(i, CH), :] = (acc + b_ref[...]).astype(CDT)
            return c

        lax.fori_loop(0, NCH, conv, 0)

    return _call(
        body, "mixD_conv_fwd", (MW // GW,),
        [pl.BlockSpec((2, S, GW), lambda c: (3, 0, c)), pl.BlockSpec((K, GW), lambda c: (0, c)),
         _spspec("conv_d_b", GW, lambda c: c)],
        pl.BlockSpec((S, GW), lambda c: (0, c)), SDS((S, MW), CDT),
        [pltpu.VMEM((H + S, GW), F32)])(z, cw, sp)


def _mixD_conv_bwd(z, dy, cw, dz):
    S = z.shape[1]
    K = cw.shape[0]
    H, CH, NCH = _conv_geometry(S, K)

    def body(z_ref, dy_ref, w_ref, dzin_ref, dz_ref, dw_ref, db_ref, pad, dpad, dw_s):
        pad[0:H, :] = jnp.zeros((H, GW), F32)
        dpad[pl.ds(S, H), :] = jnp.zeros((H, GW), F32)
        dw_s[...] = jnp.zeros_like(dw_s)
        db_ref[...] = jnp.zeros_like(db_ref)

        def fill(i, c):
            r = _rows(i, CH)
            pad[pl.ds(pl.multiple_of(i * CH + H, 8), CH), :] = (
                z_ref[0, r, :].astype(F32) * jax.nn.sigmoid(z_ref[1, r, :].astype(F32)))
            dpad[r, :] = dy_ref[r, :].astype(F32)
            return c

        lax.fori_loop(0, NCH, fill, 0)

        def p1(i, c):
            dyf = dy_ref[_rows(i, CH), :].astype(F32)
            for k, sh in _causal_taps(pad, i, CH, H, K):
                dw_s[k:k + 1, :] += jnp.sum(dyf * sh, axis=0, keepdims=True)
            db_ref[0:1, :] += jnp.sum(dyf, axis=0, keepdims=True)
            return c

        lax.fori_loop(0, NCH, p1, 0)

        def p2(i, c):
            r = _rows(i, CH)
            dq = jnp.zeros((CH, GW), F32)
            for k, sh in _anti_taps(dpad, i, CH, H, K):
                dq = dq + w_ref[k:k + 1, :] * sh
            a = z_ref[0, r, :].astype(F32)
            sg = jax.nn.sigmoid(z_ref[1, r, :].astype(F32))
            dz_ref[0, r, :] = (dq * sg).astype(CDT)
            dz_ref[1, r, :] = (dq * a * sg * (1.0 - sg)).astype(CDT)
            return c

        lax.fori_loop(0, NCH, p2, 0)
        dw_ref[...] = dw_s[0:K, :]

    return _call(
        body, "mixD_conv_bwd", (MW // GW,),
        [pl.BlockSpec((2, S, GW), lambda c: (3, 0, c)), pl.BlockSpec((S, GW), lambda c: (0, c)),
         pl.BlockSpec((K, GW), lambda c: (0, c)), _ANY],
        [pl.BlockSpec((2, S, GW), lambda c: (3, 0, c)), pl.BlockSpec((K, GW), lambda c: (0, c)),
         pl.BlockSpec((8, GW), lambda c: (0, c))],
        [SDS((NS, S, MW), CDT), SDS((K, MW), F32), SDS((8, MW), F32)],
        [pltpu.VMEM((H + S, GW), F32), pltpu.VMEM((S + H, GW), F32), pltpu.VMEM((8 * ((K + 7) // 8), GW), F32)],
        aliases={3: 0})(z, dy, cw, dz)


def _mixD_ln_bwd(dm, yd, sp):
    S = yd.shape[0]
    TM = min(512, S)

    def body(dm_ref, y_ref, lg_ref, lb_ref, dy_ref, gp_ref):
        @pl.when(pl.program_id(0) == 0)
        def _():
            gp_ref[...] = jnp.zeros_like(gp_ref)

        xh, rs = _ln(y_ref[...].astype(F32))
        _, dsl = _silu_parts(xh * lg_ref[...] + lb_ref[...])
        dl = dm_ref[...].astype(F32) * dsl
        gp_ref[0:1, :] += jnp.sum(dl * xh, axis=0, keepdims=True)
        gp_ref[1:2, :] += jnp.sum(dl, axis=0, keepdims=True)
        dy_ref[...] = _ln_bwd(xh, rs, dl * lg_ref[...]).astype(CDT)

    row = lambda i: (i, 0)
    return _call(
        body, "mixD_ln_bwd", (S // TM,),
        [pl.BlockSpec((None, TM, MW), lambda i: (3, i, 0)), pl.BlockSpec((TM, MW), row), _spspec("conv_d_ln_g", MW, _zero),
         _spspec("conv_d_ln_b", MW, _zero)],
        [pl.BlockSpec((TM, MW), row), pl.BlockSpec((8, MW), lambda i: (0, 0))],
        [SDS((S, MW), CDT), SDS((8, MW), F32)])(dm, yd, sp, sp)


def _box_causal(val, g):
    s = val
    for d in range(g + 1):
        s = s + pltpu.roll(s, 1 << d, 0)
    return s


def _box_anti(val, g):
    n = val.shape[0]
    s = val
    for d in range(g + 1):
        s = s + pltpu.roll(s, n - (1 << d), 0)
    return s


def _pool_count(i, ch, win):
    t = lax.broadcasted_iota(jnp.int32, (ch, GW), 0) + (i * ch + 1)
    return jnp.minimum(t, win).astype(F32)


def _mixB_fwd(z, wp, sp):
    S = z.shape[1]
    H, CH = 16, min(256, S)
    NCH = S // CH
    assert POOL_WINDOWS == tuple(2 << g for g in range(4))

    def body(p_ref, wp_ref, sc_ref, o_ref, pad):
        pad[0:H, :] = jnp.zeros((H, MW), F32)

        def fill(i, c):
            pad[pl.ds(pl.multiple_of(i * CH + H, 8), CH), :] = p_ref[_rows(i, CH), :].astype(F32)
            return c

        lax.fori_loop(0, NCH, fill, 0)

        def step(i, c):
            r = _rows(i, CH)
            for g in range(4):
                gs = slice(g * GW, (g + 1) * GW)
                val = pad[pl.ds(pl.multiple_of(i * CH, 8), CH + H), gs]
                pooled = _box_causal(val, g)[H:, :] / _pool_count(i, CH, POOL_WINDOWS[g]) - val[H:, :]
                mixed = _nn(pooled.astype(CDT), wp_ref[g].astype(CDT))
                o_ref[r, gs] = (mixed * sc_ref[:, gs]).astype(CDT)
            return c

        lax.fori_loop(0, NCH, step, 0)

    return _call(
        body, "mixB_fwd", (1,),
        [pl.BlockSpec((None, S, MW), lambda i: (3, 0, 0)), pl.BlockSpec((4, GW, GW), lambda i: (0, 0, 0)),
         _spspec("pool_scale", MW, _zero)],
        pl.BlockSpec((S, MW), lambda i: (0, 0)), SDS((S, MW), CDT),
        [pltpu.VMEM((H + S, MW), F32)])(z, wp, sp)


def _mixB_bwd(z, dm, wp, sp, dz):
    S = z.shape[1]
    H, CH = 16, min(256, S)
    NCH = S // CH

    def body(p_ref, dm_ref, wp_ref, sc_ref, dzin_ref, dz_ref, dwp_ref, dsc_ref, pad, rpad):
        pad[0:H, :] = jnp.zeros((H, MW), F32)
        rpad[pl.ds(S, H), :] = jnp.zeros((H, MW), F32)
        dwp_ref[...] = jnp.zeros_like(dwp_ref)
        dsc_ref[...] = jnp.zeros_like(dsc_ref)

        def fill(i, c):
            pad[pl.ds(pl.multiple_of(i * CH + H, 8), CH), :] = p_ref[_rows(i, CH), :].astype(F32)
            return c

        lax.fori_loop(0, NCH, fill, 0)

        def p1(i, c):
            r = _rows(i, CH)
            for g in range(4):
                gs = slice(g * GW, (g + 1) * GW)
                cnt = _pool_count(i, CH, POOL_WINDOWS[g])
                val = pad[pl.ds(pl.multiple_of(i * CH, 8), CH + H), gs]
                pooled = (_box_causal(val, g)[H:, :] / cnt - val[H:, :]).astype(CDT)
                w = wp_ref[g].astype(CDT)
                mixed = _nn(pooled, w)
                dmf = dm_ref[r, gs].astype(F32)
                dsc_ref[0:1, gs] += jnp.sum(dmf * mixed, axis=0, keepdims=True)
                dmx = (dmf * sc_ref[:, gs]).astype(CDT)
                dwp_ref[g] += _tn(pooled, dmx)
                rpad[r, gs] = _nt(dmx, w) / cnt
            return c

        lax.fori_loop(0, NCH, p1, 0)

        def p2(i, c):
            r = _rows(i, CH)
            for g in range(4):
                gs = slice(g * GW, (g + 1) * GW)
                val = rpad[pl.ds(pl.multiple_of(i * CH, 8), CH + H), gs]
                dp = _box_anti(val, g)[:CH, :] - val[:CH, :] * _pool_count(i, CH, POOL_WINDOWS[g])
                dz_ref[r, gs] = dp.astype(CDT)
            return c

        lax.fori_loop(0, NCH, p2, 0)

    return _call(
        body, "mixB_bwd", (1,),
        [pl.BlockSpec((None, S, MW), lambda i: (3, 0, 0)), pl.BlockSpec((None, S, MW), lambda i: (1, 0, 0)),
         pl.BlockSpec((4, GW, GW), lambda i: (0, 0, 0)), _spspec("pool_scale", MW, _zero), _ANY],
        [pl.BlockSpec((None, S, MW), lambda i: (3, 0, 0)), pl.BlockSpec((4, GW, GW), lambda i: (0, 0, 0)),
         pl.BlockSpec((8, MW), lambda i: (0, 0))],
        [SDS((NS, S, MW), CDT), SDS((4, GW, GW), F32), SDS((8, MW), F32)],
        [pltpu.VMEM((H + S, MW), F32), pltpu.VMEM((S + H, MW), F32)], aliases={4: 0})(z, dm, wp, sp, dz)


def _sgu_mask():
    ci = lax.broadcasted_iota(jnp.int32, (GW, GW), 0) // CHUNK
    cj = lax.broadcasted_iota(jnp.int32, (GW, GW), 1) // CHUNK
    return cj <= ci


def _mixC_fwd(z, ws, bsc, sp):
    S = z.shape[1]
    RB = min(512, S)

    def body(z_ref, lg_ref, lb_ref, ws_ref, bs_ref, o_ref):
        mask = _sgu_mask()
        gu = _gelu(z_ref[0].astype(F32))
        xh, _ = _ln(_gelu(z_ref[1].astype(F32)))
        vn = (xh * lg_ref[...] + lb_ref[...]).astype(CDT)
        for g in range(4):
            gs = slice(g * GW, (g + 1) * GW)
            wm = jnp.where(mask, ws_ref[g], 0.0).astype(CDT)
            for nb in range(RB // GW):
                rs = slice(nb * GW, (nb + 1) * GW)
                mixed = _nn(wm, vn[rs, gs]) + bs_ref[g]
                o_ref[rs, gs] = (gu[rs, gs] * mixed).astype(CDT)

    return _call(
        body, "mixC_fwd", (S // RB,),
        [pl.BlockSpec((2, RB, MW), lambda i: (2, i, 0)), _spspec("sgu_ln_g", MW, _zero), _spspec("sgu_ln_b", MW, _zero),
         pl.BlockSpec((4, GW, GW), lambda i: (0, 0, 0)), pl.BlockSpec((4, GW, 1), lambda i: (0, 0, 0))],
        pl.BlockSpec((RB, MW), lambda i: (i, 0)), SDS((S, MW), CDT))(z, sp, sp, ws, bsc)


def _mixC_bwd(z, dm, ws, bsc, sp, dz):
    S = z.shape[1]
    RB = min(512, S)
    NR = S // RB

    def body(z_ref, dm_ref, lg_ref, lb_ref, ws_ref, bs_ref, dzin_ref, dz_ref, dws_ref, dbs_ref, gp_ref, dvn_s):
        i = pl.program_id(0)

        @pl.when(i == 0)
        def _():
            dws_ref[...] = jnp.zeros_like(dws_ref)
            dbs_ref[...] = jnp.zeros_like(dbs_ref)
            gp_ref[...] = jnp.zeros_like(gp_ref)

        mask = _sgu_mask()
        gu, dgu = _gelu_parts(z_ref[0].astype(F32))
        gv, dgv = _gelu_parts(z_ref[1].astype(F32))
        xh, rs_ = _ln(gv)
        vn = (xh * lg_ref[...] + lb_ref[...]).astype(CDT)
        dmf = dm_ref[...].astype(F32)
        for g in range(4):
            gs = slice(g * GW, (g + 1) * GW)
            wm = jnp.where(mask, ws_ref[g], 0.0).astype(CDT)
            for nb in range(RB // GW):
                rs = slice(nb * GW, (nb + 1) * GW)
                vb = vn[rs, gs]
                mixed = _nn(wm, vb) + bs_ref[g]
                dz_ref[0, rs, gs] = (dmf[rs, gs] * mixed * dgu[rs, gs]).astype(CDT)
                dmx = dmf[rs, gs] * gu[rs, gs]
                dbs_ref[g] += dmx
                dmxc = dmx.astype(CDT)
                dws_ref[g] += _nt(dmxc, vb)
                dvn_s[rs, gs] = _tn(wm, dmxc)
        dvn = dvn_s[...]
        gp_ref[0:1, :] += jnp.sum(dvn * xh, axis=0, keepdims=True)
        gp_ref[1:2, :] += jnp.sum(dvn, axis=0, keepdims=True)
        dz_ref[1] = (_ln_bwd(xh, rs_, dvn * lg_ref[...]) * dgv).astype(CDT)

        @pl.when(i == NR - 1)
        def _():
            for g in range(4):
                dws_ref[g] = jnp.where(mask, dws_ref[g], 0.0)
                dbs_ref[g] = jnp.broadcast_to(jnp.sum(dbs_ref[g], axis=1, keepdims=True), (GW, GW))

    full3 = lambda i: (0, 0, 0)
    return _call(
        body, "mixC_bwd", (NR,),
        [pl.BlockSpec((2, RB, MW), lambda i: (2, i, 0)), pl.BlockSpec((None, RB, MW), lambda i: (2, i, 0)),
         _spspec("sgu_ln_g", MW, _zero), _spspec("sgu_ln_b", MW, _zero),
         pl.BlockSpec((4, GW, GW), full3), pl.BlockSpec((4, GW, 1), full3), _ANY],
        [pl.BlockSpec((2, RB, MW), lambda i: (2, i, 0)), pl.BlockSpec((4, GW, GW), full3), pl.BlockSpec((4, GW, GW), full3),
         pl.BlockSpec((8, MW), lambda i: (0, 0))],
        [SDS((NS, S, MW), CDT), SDS((4, GW, GW), F32), SDS((4, GW, GW), F32), SDS((8, MW), F32)],
        [pltpu.VMEM((RB, MW), F32)], aliases={6: 0})(z, dm, sp, sp, ws, bsc, dz)


def _unpack_wb(wb_ref, wbf):
    for j in range(NS):
        for k in range(4):
            wbf[k, :, j * GW:(j + 1) * GW] = wb_ref[j, k]


def _merge_fwd(ma, mb, mc, yd, g, wb, pwo, x, sp):
    S, D = x.shape
    TM = min(512, S)

    def body(ma_ref, mb_ref, mc_ref, yd_ref, g_ref, wb_ref, wo_ref, x_ref, lg_ref, lb_ref, qg_ref,
             xo_ref, md_ref, yk_ref, mg_ref, mo_ref, wbf):
        @pl.when(pl.program_id(0) == 0)
        def _():
            _unpack_wb(wb_ref, wbf)

        xh, _ = _ln(yd_ref[...].astype(F32))
        sl, _ = _silu_parts(xh * lg_ref[...] + lb_ref[...])
        md = sl.astype(CDT)
        md_ref[...] = md
        merged = jnp.zeros((TM, D), F32)
        for k, m in enumerate((ma_ref[...], mb_ref[...], mc_ref[...], md)):
            yk = _nn(m, wbf[k])
            yk_ref[k] = yk.astype(CDT)
            merged = merged + g_ref[k].astype(F32) * yk
        mgc = merged.astype(CDT)
        mg_ref[...] = mgc
        mo = _nn(mgc, wo_ref[...].reshape(D, D))
        mo_ref[...] = mo.astype(CDT)
        n, _ = _rms(mo)
        xo_ref[...] = x_ref[...] + n * qg_ref[...]

    row = lambda i: (i, 0)
    rowm = pl.BlockSpec((TM, MW), row)
    rowd = pl.BlockSpec((TM, D), row)
    row4 = pl.BlockSpec((4, TM, D), lambda i: (0, i, 0))
    return _call(
        body, "merge_fwd", (S // TM,),
        [rowm, rowm, rowm, rowm, row4, pl.BlockSpec((NS, 4, MW, GW), lambda i: (0, 0, 0, 0), pipeline_mode=pl.Buffered(1)),
         pl.BlockSpec((NS, GW, D), lambda i: (0, 0, 0), pipeline_mode=pl.Buffered(1)), rowd,
         _spspec("conv_d_ln_g", MW, _zero), _spspec("conv_d_ln_b", MW, _zero), _spspec("mix_post_g", D, _zero)],
        [rowd, rowm, row4, rowd, rowd],
        [SDS((S, D), F32), SDS((S, MW), CDT), SDS((4, S, D), CDT), SDS((S, D), CDT), SDS((S, D), CDT)],
        [pltpu.VMEM((4, MW, D), CDT)])(ma, mb, mc, yd, g, wb, pwo, x, sp, sp, sp)


def _merge_bwd_act(dxo, mo, g, yk, wb, pwo, sp, dep):
    S, D = dxo.shape
    TM = min(256, S)

    def body(dxo_ref, mo_ref, g_ref, yk_ref, wb_ref, wo_ref, qg_ref, dep_ref, dmo_ref, dm_ref, dgp_ref, dyk_ref, gp_ref, wbf):
        @pl.when(pl.program_id(0) == 0)
        def _():
            gp_ref[...] = jnp.zeros_like(gp_ref)
            _unpack_wb(wb_ref, wbf)

        n, r = _rms(mo_ref[...].astype(F32))
        dmo, dg = _rms_bwd(n, r, qg_ref[...], dxo_ref[...])
        gp_ref[0:1, :] += dg
        dmoc = dmo.astype(CDT)
        dmo_ref[...] = dmoc
        dmg = _nt(dmoc, wo_ref[...].reshape(D, D))
        for k in range(4):
            gk = g_ref[k].astype(F32)
            dyk = (dmg * gk).astype(CDT)
            dyk_ref[k] = dyk
            dgp_ref[k] = (dmg * yk_ref[k].astype(F32) * gk * (1.0 - gk)).astype(CDT)
            dm_ref[k] = _nt(dyk, wbf[k]).astype(CDT)

    rowd = pl.BlockSpec((TM, D), lambda i: (i, 0))
    row4 = pl.BlockSpec((4, TM, D), lambda i: (0, i, 0))
    return _call(
        body, "merge_bwd_act", (S // TM,),
        [rowd, rowd, row4, row4, pl.BlockSpec((NS, 4, MW, GW), lambda i: (0, 0, 0, 0), pipeline_mode=pl.Buffered(1)),
         pl.BlockSpec((NS, GW, D), lambda i: (0, 0, 0), pipeline_mode=pl.Buffered(1)), _spspec("mix_post_g", D, _zero), _ANY],
        [rowd, pl.BlockSpec((4, TM, MW), lambda i: (0, i, 0)), row4, row4, pl.BlockSpec((8, D), lambda i: (0, 0))],
        [SDS((S, D), CDT), SDS((4, S, MW), CDT), SDS((4, S, D), CDT), SDS((4, S, D), CDT), SDS((8, D), F32)],
        [pltpu.VMEM((4, MW, D), CDT)])(dxo, mo, g, yk, wb, pwo, sp, dep)


def _merge_bwd_w(ma, mb, mc, md, dyk, mg, dmo):
    S, D = dmo.shape
    TK = min(1024, S)
    NK = S // TK

    def body(ma_ref, mb_ref, mc_ref, md_ref, dyk_ref, mg_ref, dmo_ref, gwb_ref, gwo_ref, accb, acco):
        k = pl.program_id(0)

        @pl.when(k == 0)
        def _():
            accb[...] = jnp.zeros_like(accb)
            acco[...] = jnp.zeros_like(acco)

        for b, m in enumerate((ma_ref, mb_ref, mc_ref, md_ref)):
            accb[b] += _tn(m[...], dyk_ref[b])
        acco[...] += _tn(mg_ref[...], dmo_ref[...])

        @pl.when(k == NK - 1)
        def _():
            for j in range(NS):
                for b in range(4):
                    gwb_ref[j, b] = accb[b, :, j * GW:(j + 1) * GW].astype(CDT)
                gwo_ref[j] = acco[j * GW:(j + 1) * GW, :].astype(CDT)

    rowm = pl.BlockSpec((TK, MW), lambda k: (k, 0))
    rowd = pl.BlockSpec((TK, D), lambda k: (k, 0))
    return _call(
        body, "merge_bwd_w", (NK,),
        [rowm, rowm, rowm, rowm, pl.BlockSpec((4, TK, D), lambda k: (0, k, 0)), rowd, rowd],
        [pl.BlockSpec((NS, 4, MW, GW), lambda k: (0, 0, 0, 0), pipeline_mode=pl.Buffered(1)),
         pl.BlockSpec((NS, GW, D), lambda k: (0, 0, 0), pipeline_mode=pl.Buffered(1))],
        [SDS((NS, 4, MW, GW), CDT), SDS((NS, GW, D), CDT)],
        [pltpu.VMEM((4, MW, D), F32), pltpu.VMEM((D, D), F32)])(ma, mb, mc, md, dyk, mg, dmo)


def _xa_kv(mem, sp, pxa):
    M, D = mem.shape

    def body(m_ref, g_ref, wk_ref, wv_ref, mn_ref, k_ref, v_ref):
        n, _ = _rms(m_ref[...])
        mn = (n * g_ref[...]).astype(CDT)
        mn_ref[...] = mn
        k_ref[...] = _nn(mn, wk_ref[...].reshape(D, D)).astype(CDT)
        v_ref[...] = _nn(mn, wv_ref[...].reshape(D, D)).astype(CDT)

    full = pl.BlockSpec((M, D), lambda i: (0, 0))
    return _call(
        body, "xa_kv", (1,),
        [full, _spspec("mem_g", D, _zero), pl.BlockSpec((NS, GW, D), lambda i: (0, 1, 0)),
         pl.BlockSpec((NS, GW, D), lambda i: (0, 2, 0))],
        [full, full, full], [SDS((M, D), CDT)] * 3)(mem, sp, pxa, pxa)


def _softmax(s):
    e = jnp.exp(s - jnp.max(s, axis=-1, keepdims=True))
    return e / jnp.sum(e, axis=-1, keepdims=True)


def _xa_fwd(x, kk, vv, sp, pxa):
    S, D = x.shape
    M = kk.shape[0]
    TM = min(512, S)
    HD = D // XA_HEADS
    scale = HD ** -0.5

    def body(x_ref, k_ref, v_ref, pg_ref, qg_ref, wq_ref, wo_ref, xo_ref, hb_ref, q_ref, o_ref, po_ref):
        n, _ = _rms(x_ref[...])
        hb = (n * pg_ref[...]).astype(CDT)
        hb_ref[...] = hb
        q = _nn(hb, wq_ref[...].reshape(D, D)).astype(CDT)
        q_ref[...] = q
        for h in range(XA_HEADS):
            hs = slice(h * HD, (h + 1) * HD)
            p = _softmax(_nt(q[:, hs], k_ref[:, hs]) * scale)
            o_ref[:, hs] = _nn(p.astype(CDT), v_ref[:, hs]).astype(CDT)
        po = _nn(o_ref[...], wo_ref[...].reshape(D, D))
        po_ref[...] = po.astype(CDT)
        n, _ = _rms(po)
        xo_ref[...] = x_ref[...] + n * qg_ref[...]

    row = pl.BlockSpec((TM, D), lambda i: (i, 0))
    full = pl.BlockSpec((M, D), lambda i: (0, 0))
    return _call(
        body, "xa_fwd", (S // TM,),
        [row, full, full, _spspec("xa_pre_g", D, _zero), _spspec("xa_post_g", D, _zero),
         pl.BlockSpec((NS, GW, D), lambda i: (0, 0, 0)), pl.BlockSpec((NS, GW, D), lambda i: (0, 3, 0))],
        [row] * 5, [SDS((S, D), F32)] + [SDS((S, D), CDT)] * 4)(x, kk, vv, sp, sp, pxa, pxa)


def _xa_bwd_act(dxo, x, po, q, kk, vv, sp, pxa, dep):
    S, D = x.shape
    M = kk.shape[0]
    TM = min(512, S)
    HD = D // XA_HEADS
    scale = HD ** -0.5

    def body(dxo_ref, x_ref, po_ref, q_ref, k_ref, v_ref, pg_ref, qg_ref, wq_ref, wo_ref, dep_ref,
             dx_ref, dpo_ref, dq_ref, dk_ref, dv_ref, gp_ref):
        @pl.when(pl.program_id(0) == 0)
        def _():
            gp_ref[...] = jnp.zeros_like(gp_ref)
            dk_ref[...] = jnp.zeros_like(dk_ref)
            dv_ref[...] = jnp.zeros_like(dv_ref)

        n, r = _rms(po_ref[...].astype(F32))
        dpo, dg = _rms_bwd(n, r, qg_ref[...], dxo_ref[...])
        gp_ref[1:2, :] += dg
        dpoc = dpo.astype(CDT)
        dpo_ref[...] = dpoc
        do = _nt(dpoc, wo_ref[...].reshape(D, D)).astype(CDT)
        for h in range(XA_HEADS):
            hs = slice(h * HD, (h + 1) * HD)
            qh = q_ref[:, hs]
            p = _softmax(_nt(qh, k_ref[:, hs]) * scale)
            pc = p.astype(CDT)
            dv_ref[:, hs] += _tn(pc, do[:, hs])
            dp = _nt(do[:, hs], v_ref[:, hs])
            ds = (p * (dp - jnp.sum(p * dp, axis=-1, keepdims=True)) * scale).astype(CDT)
            dq_ref[:, hs] = _nn(ds, k_ref[:, hs]).astype(CDT)
            dk_ref[:, hs] += _tn(ds, qh)
        dhb = _nt(dq_ref[...], wq_ref[...].reshape(D, D))
        n, r = _rms(x_ref[...])
        dx, dg = _rms_bwd(n, r, pg_ref[...], dhb)
        dx_ref[...] = dxo_ref[...] + dx
        gp_ref[0:1, :] += dg

    row = pl.BlockSpec((TM, D), lambda i: (i, 0))
    full = pl.BlockSpec((M, D), lambda i: (0, 0))
    return _call(
        body, "xa_bwd_act", (S // TM,),
        [row, row, row, row, full, full, _spspec("xa_pre_g", D, _zero), _spspec("xa_post_g", D, _zero),
         pl.BlockSpec((NS, GW, D), lambda i: (0, 0, 0)), pl.BlockSpec((NS, GW, D), lambda i: (0, 3, 0)), _ANY],
        [row, row, row, full, full, pl.BlockSpec((8, D), lambda i: (0, 0))],
        [SDS((S, D), F32), SDS((S, D), CDT), SDS((S, D), CDT), SDS((M, D), F32), SDS((M, D), F32), SDS((8, D), F32)],
    )(dxo, x, po, q, kk, vv, sp, sp, pxa, pxa, dep)


def _xa_bwd_w(hb, dq, o, dpo, mn, dk, dv):
    S, D = hb.shape
    M = mn.shape[0]
    TK = min(1024, S)
    NK = S // TK

    def body(hb_ref, dq_ref, o_ref, dpo_ref, mn_ref, dk_ref, dv_ref, g_ref, accq, acco):
        k = pl.program_id(0)

        @pl.when(k == 0)
        def _():
            accq[...] = jnp.zeros_like(accq)
            acco[...] = jnp.zeros_like(acco)

        accq[...] += _tn(hb_ref[...], dq_ref[...])
        acco[...] += _tn(o_ref[...], dpo_ref[...])

        @pl.when(k == NK - 1)
        def _():
            gk = _tn(mn_ref[...], dk_ref[...].astype(CDT))
            gv = _tn(mn_ref[...], dv_ref[...].astype(CDT))
            for j in range(NS):
                rs = slice(j * GW, (j + 1) * GW)
                g_ref[j, 0:GW, :] = accq[rs, :].astype(CDT)
                g_ref[j, GW:2 * GW, :] = gk[rs, :].astype(CDT)
                g_ref[j, 2 * GW:3 * GW, :] = gv[rs, :].astype(CDT)
                g_ref[j, 3 * GW:4 * GW, :] = acco[rs, :].astype(CDT)

    rowb = pl.BlockSpec((TK, D), lambda k: (k, 0))
    full = pl.BlockSpec((M, D), lambda k: (0, 0))
    return _call(
        body, "xa_bwd_w", (NK,),
        [rowb, rowb, rowb, rowb, full, full, full],
        pl.BlockSpec((NS, 4 * GW, D), lambda k: (0, 0, 0)), SDS((NS, 4 * GW, D), CDT),
        [pltpu.VMEM((D, D), F32), pltpu.VMEM((D, D), F32)])(hb, dq, o, dpo, mn, dk, dv)


def _xa_kv_bwd(mem, dk, dv, sp, pxa):
    M, D = mem.shape

    def body(m_ref, dk_ref, dv_ref, wk_ref, wv_ref, gp_ref):
        dmn = _nt(dk_ref[...].astype(CDT), wk_ref[...].reshape(D, D)) + _nt(dv_ref[...].astype(CDT), wv_ref[...].reshape(D, D))
        n, _ = _rms(m_ref[...])
        gp_ref[...] = jnp.zeros_like(gp_ref)
        gp_ref[0:1, :] = jnp.sum(dmn * n, axis=0, keepdims=True)

    full = pl.BlockSpec((M, D), lambda i: (0, 0))
    return _call(
        body, "xa_kv_bwd", (1,),
        [full, full, full, pl.BlockSpec((NS, GW, D), lambda i: (0, 1, 0)), pl.BlockSpec((NS, GW, D), lambda i: (0, 2, 0))],
        pl.BlockSpec((8, D), lambda i: (0, 0)), SDS((8, D), F32))(mem, dk, dv, pxa, pxa)


def _loss_head(y, t):
    S, D = y.shape
    TM = min(512, S)

    def body(y_ref, t_ref, dy_ref, l_ref):
        @pl.when(pl.program_id(0) == 0)
        def _():
            l_ref[...] = jnp.zeros_like(l_ref)

        e = y_ref[...] - t_ref[...]
        dy_ref[...] = e * (1.0 / D)
        l_ref[...] += 0.5 * jnp.sum(jnp.mean(e * e, axis=-1, keepdims=True), axis=0, keepdims=True)

    row = pl.BlockSpec((TM, D), lambda i: (i, 0))
    return _call(body, "loss_head", (S // TM,), [row, row], [row, pl.BlockSpec((8, 128), lambda i: (0, 0))],
                 [SDS((S, D), F32), SDS((8, 128), F32)])(y, t)


def _row_tile(rows, cols, limit=1 << 18, step=8):
    if rows * cols <= limit or rows % step:
        return rows
    best = step
    for t in range(step, rows + 1, step):
        if rows % t == 0 and t * cols <= limit:
            best = t
    return best


def _adamw(w, g, m, v):
    shape = w.shape
    C = shape[-1]
    R = w.size // C
    TR = _row_tile(R, C)
    c1 = 1.0 - ADAM_B1 ** ADAM_STEP
    c2 = 1.0 - ADAM_B2 ** ADAM_STEP

    def body(w_ref, g_ref, m_ref, v_ref, d_ref, nm_ref, nv_ref):
        gg = g_ref[...]
        nm = ADAM_B1 * m_ref[...] + (1.0 - ADAM_B1) * gg
        nv = ADAM_B2 * v_ref[...] + (1.0 - ADAM_B2) * (gg * gg)
        nm_ref[...] = nm
        nv_ref[...] = nv
        d_ref[...] = -ADAM_LR * ((nm / c1) / (jnp.sqrt(nv / c2) + ADAM_EPS) + ADAM_WD * w_ref[...])

    blk = pl.BlockSpec((TR, C), lambda i: (i, 0))
    outs = _call(body, "adamw", (R // TR,), [blk] * 4, [blk] * 3, [SDS((R, C), F32)] * 3)(
        w.reshape(R, C), g.reshape(R, C), m.reshape(R, C), v.reshape(R, C))
    return tuple(o.reshape(shape) for o in outs)


def _adamw_block(w, gs, m, v, gblock):
    L, R, C = w.shape
    c1 = 1.0 - ADAM_B1 ** ADAM_STEP
    c2 = 1.0 - ADAM_B2 ** ADAM_STEP

    def body(w_ref, g_ref, m_ref, v_ref, go_ref, d_ref, nm_ref, nv_ref):
        gg = g_ref[...]
        go_ref[...] = gg
        nm = ADAM_B1 * m_ref[...] + (1.0 - ADAM_B1) * gg
        nv = ADAM_B2 * v_ref[...] + (1.0 - ADAM_B2) * (gg * gg)
        nm_ref[...] = nm
        nv_ref[...] = nv
        d_ref[...] = -ADAM_LR * ((nm / c1) / (jnp.sqrt(nv / c2) + ADAM_EPS) + ADAM_WD * w_ref[...])

    blk = pl.BlockSpec((None, R, C), lambda l: (l, 0, 0))
    return _call(body, "adamw_block", (L,), [blk, pl.BlockSpec((None, R, C), lambda l: (l, gblock, 0)), blk, blk],
                 [blk] * 4, [SDS((L, R, C), F32)] * 4)(w, gs, m, v)


def _slot_sum_into(stacked, r, l):
    _, R, C = r.shape
    TR = _row_tile(R, C * NS, limit=1 << 21, step=16)

    def body(r_ref, s_ref, o_ref):
        acc = r_ref[0].astype(F32)
        for j in range(1, NS):
            acc = acc + r_ref[j].astype(F32)
        o_ref[...] = acc

    return pl.pallas_call(
        body, name="slot_sum_into", grid=(R // TR,),
        in_specs=[pl.BlockSpec((NS, TR, C), lambda i: (0, i, 0)), _ANY],
        out_specs=pl.BlockSpec((None, TR, C), lambda i: (l, i, 0)), out_shape=SDS(stacked.shape, F32),
        input_output_aliases={1: 0},
        compiler_params=pltpu.CompilerParams(dimension_semantics=("arbitrary",), vmem_limit_bytes=VMEM_LIMIT))(r, stacked)


def _exchange(arrs, scatter, name):
    n = len(arrs)
    np_ = NS - 1

    def body(*refs):
        ins, outs = refs[:n], refs[n:2 * n]
        send_sems, recv_sems, loc_sems = refs[2 * n:]
        x, y, c = lax.axis_index("x"), lax.axis_index("y"), lax.axis_index("c")
        me = 4 * x + 2 * y + c
        peers = []
        for f in range(1, NS):
            px = 1 - x if f & 4 else x
            py = 1 - y if f & 2 else y
            pc = 1 - c if f & 1 else c
            peers.append(((px, py, pc), 4 * px + 2 * py + pc))

        def src(a, pid):
            return ins[a].at[pid] if scatter else ins[a]

        local = [pltpu.make_async_copy(src(a, me), outs[a].at[me], loc_sems.at[a]) for a in range(n)]
        for cp in local:
            cp.start()
        sends = []
        for a in range(n):
            for f, (dev, pid) in enumerate(peers):
                sends.append(pltpu.make_async_remote_copy(
                    src_ref=src(a, pid), dst_ref=outs[a].at[me], send_sem=send_sems.at[a * np_ + f],
                    recv_sem=recv_sems.at[a * np_ + f], device_id=dev, device_id_type=pl.DeviceIdType.MESH))
        for cp in sends:
            cp.start()
        for a in range(n):
            for f, (dev, pid) in enumerate(peers):
                pltpu.make_async_remote_copy(
                    src_ref=src(a, pid), dst_ref=outs[a].at[pid], send_sem=send_sems.at[a * np_ + f],
                    recv_sem=recv_sems.at[a * np_ + f], device_id=dev, device_id_type=pl.DeviceIdType.MESH).wait_recv()
        for cp in sends:
            cp.wait_send()
        for cp in local:
            cp.wait()

    out_shape = [SDS(a.shape if scatter else (NS,) + a.shape, a.dtype) for a in arrs]
    anyspec = pl.BlockSpec(memory_space=pl.ANY)
    outs = pl.pallas_call(
        body, name=name, in_specs=[anyspec] * n, out_specs=[anyspec] * n, out_shape=out_shape,
        scratch_shapes=[pltpu.SemaphoreType.DMA((n * np_,)), pltpu.SemaphoreType.DMA((n * np_,)),
                        pltpu.SemaphoreType.DMA((n,))],
        compiler_params=pltpu.CompilerParams(has_side_effects=True))(*arrs)
    return list(outs)


def _peers():
    x, y, c = lax.axis_index("x"), lax.axis_index("y"), lax.axis_index("c")
    out = []
    for f in range(1, NS):
        px = 1 - x if f & 4 else x
        py = 1 - y if f & 2 else y
        pc = 1 - c if f & 1 else c
        out.append(((px, py, pc), 4 * px + 2 * py + pc))
    return 4 * x + 2 * y + c, out


def _exchange_copies(ins, lands, scatter, send_sems, recv_sems, loc_sems):
    me, peers = _peers()
    np_ = NS - 1

    def src(a, pid):
        return ins[a].at[pid] if scatter[a] else ins[a]

    def rcopy(a, f, dev, land_slot):
        return pltpu.make_async_remote_copy(
            src_ref=src(a, peers[f][1]), dst_ref=lands[a].at[land_slot], send_sem=send_sems.at[a * np_ + f],
            recv_sem=recv_sems.at[a * np_ + f], device_id=dev, device_id_type=pl.DeviceIdType.MESH)

    local = [pltpu.make_async_copy(src(a, me), lands[a].at[me], loc_sems.at[a]) for a in range(len(ins))]
    sends = [rcopy(a, f, dev, me) for a in range(len(ins)) for f, (dev, _) in enumerate(peers)]
    arrivals = [rcopy(a, f, dev, pid) for a in range(len(ins)) for f, (dev, pid) in enumerate(peers)]
    return local, sends, arrivals


_HBM = pl.BlockSpec(memory_space=pltpu.HBM)
_SEM = pl.BlockSpec(memory_space=pltpu.SEMAPHORE)
_ANY = pl.BlockSpec(memory_space=pl.ANY)


def _exchange_start(arrs, scatter, after, name):
    n = len(arrs)
    np_ = NS - 1
    lands = [lax.empty(a.shape if sc else (NS,) + a.shape, a.dtype) for a, sc in zip(arrs, scatter)]

    def body(*refs):
        ins, lnd = refs[:n], refs[n:2 * n]
        send_sems, recv_sems, loc_sems = refs[2 * n + 1:2 * n + 4]
        token = refs[-1]
        local, sends, _ = _exchange_copies(ins, lnd, scatter, send_sems, recv_sems, loc_sems)
        for cp in local + sends:
            cp.start()
        token[...] = jnp.zeros_like(token)

    hbm = lambda a: pltpu.HBM(a.shape, a.dtype)
    outs = pl.pallas_call(
        body, name=name,
        out_shape=(pltpu.SemaphoreType.DMA((n * np_,)), pltpu.SemaphoreType.DMA((n * np_,)), pltpu.SemaphoreType.DMA((n,)),
                   *[hbm(a) for a in arrs], *[hbm(a) for a in lands], SDS((8, 128), F32)),
        in_specs=[_HBM] * (2 * n) + [_ANY],
        out_specs=(_SEM, _SEM, _SEM, *([_HBM] * (2 * n)), pl.BlockSpec(memory_space=pltpu.VMEM)),
        input_output_aliases={i: 3 + i for i in range(2 * n)},
        compiler_params=pltpu.CompilerParams(has_side_effects=pltpu.SideEffectType.DATAFLOW_SIDE_EFFECTING),
    )(*[pltpu.with_memory_space_constraint(a, pltpu.HBM) for a in list(arrs) + lands], after)
    return {"sems": outs[:3], "ins": outs[3:3 + n], "lands": outs[3 + n:3 + 2 * n], "token": outs[-1], "scatter": scatter}


def _exchange_wait(h, after, name):
    n = len(h["ins"])
    scatter = h["scatter"]
    after = list(after) if isinstance(after, (list, tuple)) else [after]

    def body(*refs):
        ins, lnd = refs[:n], refs[n:2 * n]
        send_sems, recv_sems, loc_sems = refs[2 * n:2 * n + 3]
        local, sends, arrivals = _exchange_copies(ins, lnd, scatter, send_sems, recv_sems, loc_sems)
        for cp in sends:
            cp.wait_send()
        for cp in arrivals:
            cp.wait_recv()
        for cp in local:
            cp.wait()

    hbm = lambda a: pltpu.HBM(a.shape, a.dtype)
    outs = pl.pallas_call(
        body, name=name,
        out_shape=tuple(hbm(a) for a in list(h["ins"]) + list(h["lands"])),
        in_specs=[_HBM] * (2 * n) + [_SEM] * 3 + [_ANY] * len(after),
        out_specs=tuple([_HBM] * (2 * n)),
        input_output_aliases={i: i for i in range(2 * n)},
        compiler_params=pltpu.CompilerParams(has_side_effects=pltpu.SideEffectType.DATAFLOW_SIDE_EFFECTING),
    )(*h["ins"], *h["lands"], *h["sems"], *after)
    return list(outs[n:])


def _hgather_copies(ins, lands, send_a, recv_a, send_b, recv_b, loc_sems):
    x, y, c = lax.axis_index("x"), lax.axis_index("y"), lax.axis_index("c")
    me = 4 * x + 2 * y + c
    sib = (x, y, 1 - c)
    chips = [(1 - x, y), (x, 1 - y), (1 - x, 1 - y)]
    slot = lambda px, py, pc: 4 * px + 2 * py + pc

    def rcopy(src, dst, ssem, rsem, dev):
        return pltpu.make_async_remote_copy(src_ref=src, dst_ref=dst, send_sem=ssem, recv_sem=rsem, device_id=dev,
                                            device_id_type=pl.DeviceIdType.MESH)

    local, s1, a1, s2, a2 = [], [], [], [], []
    for a in range(len(ins)):
        first = [(sib, slot(x, y, 1 - c))] + [((px, py, c), slot(px, py, c)) for px, py in chips]
        for k, (dev, origin) in enumerate(first if send_a is not None else ()):
            s1.append(rcopy(ins[a], lands[a].at[me], send_a.at[4 * a + k], recv_a.at[4 * a + k], dev))
            a1.append(rcopy(ins[a], lands[a].at[origin], send_a.at[4 * a + k], recv_a.at[4 * a + k], dev))
        if send_a is not None:
            local.append(pltpu.make_async_copy(ins[a], lands[a].at[me], loc_sems.at[a]))
        for k, (px, py) in enumerate(chips if send_b is not None else ()):
            mine, theirs = lands[a].at[slot(px, py, c)], lands[a].at[slot(px, py, 1 - c)]
            s2.append(rcopy(mine, mine, send_b.at[3 * a + k], recv_b.at[3 * a + k], sib))
            a2.append(rcopy(mine, theirs, send_b.at[3 * a + k], recv_b.at[3 * a + k], sib))
    return local, s1, a1, s2, a2


def _hgather_start(arrs, after, name):
    n = len(arrs)
    lands = [lax.empty((NS,) + a.shape, a.dtype) for a in arrs]

    def body(*refs):
        ins, lnd = refs[:n], refs[n:2 * n]
        send_a, recv_a, loc_sems = refs[2 * n + 1:2 * n + 4]
        token = refs[-1]
        local, s1, _, _, _ = _hgather_copies(ins, lnd, send_a, recv_a, None, None, loc_sems)
        for cp in local + s1:
            cp.start()
        token[...] = jnp.zeros_like(token)

    hbm = lambda a: pltpu.HBM(a.shape, a.dtype)
    outs = pl.pallas_call(
        body, name=name,
        out_shape=(pltpu.SemaphoreType.DMA((4 * n,)), pltpu.SemaphoreType.DMA((4 * n,)), pltpu.SemaphoreType.DMA((n,)),
                   *[hbm(a) for a in arrs], *[hbm(a) for a in lands], SDS((8, 128), F32)),
        in_specs=[_HBM] * (2 * n) + [_ANY],
        out_specs=(_SEM, _SEM, _SEM, *([_HBM] * (2 * n)), pl.BlockSpec(memory_space=pltpu.VMEM)),
        input_output_aliases={i: 3 + i for i in range(2 * n)},
        compiler_params=pltpu.CompilerParams(has_side_effects=pltpu.SideEffectType.DATAFLOW_SIDE_EFFECTING),
    )(*[pltpu.with_memory_space_constraint(a, pltpu.HBM) for a in list(arrs) + lands], after)
    return {"sems": outs[:3], "ins": outs[3:3 + n], "lands": outs[3 + n:3 + 2 * n], "token": outs[-1]}


def _hgather_forward(h, after, name):
    n = len(h["ins"])
    after = list(after) if isinstance(after, (list, tuple)) else [after]
    na = len(after)

    def body(*refs):
        ins, lnd = refs[:n], refs[n:2 * n]
        send_a, recv_a, loc_sems = refs[2 * n:2 * n + 3]
        send_b, recv_b = refs[2 * n + 3 + na:2 * n + 5 + na]
        token = refs[-1]
        local, s1, a1, s2, _ = _hgather_copies(ins, lnd, send_a, recv_a, send_b, recv_b, loc_sems)
        for cp in s1:
            cp.wait_send()
        for cp in a1:
            cp.wait_recv()
        for cp in local:
            cp.wait()
        for cp in s2:
            cp.start()
        token[...] = jnp.zeros_like(token)

    hbm = lambda a: pltpu.HBM(a.shape, a.dtype)
    outs = pl.pallas_call(
        body, name=name,
        out_shape=(pltpu.SemaphoreType.DMA((3 * n,)), pltpu.SemaphoreType.DMA((3 * n,)),
                   *[hbm(a) for a in list(h["ins"]) + list(h["lands"])], SDS((8, 128), F32)),
        in_specs=[_HBM] * (2 * n) + [_SEM] * 3 + [_ANY] * na,
        out_specs=(_SEM, _SEM, *([_HBM] * (2 * n)), pl.BlockSpec(memory_space=pltpu.VMEM)),
        input_output_aliases={i: 2 + i for i in range(2 * n)},
        compiler_params=pltpu.CompilerParams(has_side_effects=pltpu.SideEffectType.DATAFLOW_SIDE_EFFECTING),
    )(*h["ins"], *h["lands"], *h["sems"], *after)
    return {"sems": outs[:2], "ins": outs[2:2 + n], "lands": outs[2 + n:2 + 2 * n], "token": outs[-1]}


def _hgather_wait(h, after, name):
    n = len(h["ins"])

    def body(*refs):
        ins, lnd = refs[:n], refs[n:2 * n]
        send_b, recv_b = refs[2 * n:2 * n + 2]
        _, _, _, s2, a2 = _hgather_copies(ins, lnd, None, None, send_b, recv_b, None)
        for cp in s2:
            cp.wait_send()
        for cp in a2:
            cp.wait_recv()

    hbm = lambda a: pltpu.HBM(a.shape, a.dtype)
    outs = pl.pallas_call(
        body, name=name,
        out_shape=tuple(hbm(a) for a in list(h["ins"]) + list(h["lands"])),
        in_specs=[_HBM] * (2 * n) + [_SEM] * 2 + [_ANY],
        out_specs=tuple([_HBM] * (2 * n)),
        input_output_aliases={i: i for i in range(2 * n)},
        compiler_params=pltpu.CompilerParams(has_side_effects=pltpu.SideEffectType.DATAFLOW_SIDE_EFFECTING),
    )(*h["ins"], *h["lands"], *h["sems"], after)
    return list(outs[n:])


_W_NAMES = ("ffn1_pre_g", "ffn1_post_g", "ffn1_w1", "ffn1_w3", "ffn1_w2", "mix_pre_g", "mix_post_g", "w_in", "conv_a_w",
            "conv_a_b", "pool_w", "pool_scale", "sgu_ln_g", "sgu_ln_b", "sgu_ws", "sgu_b", "conv_d_w", "conv_d_b",
            "conv_d_ln_g", "conv_d_ln_b", "w_branch", "w_gate", "b_gate", "w_o", "xa_pre_g", "xa_post_g", "mem_g",
            "xa_wq", "xa_wk", "xa_wv", "xa_wo", "ffn2_pre_g", "ffn2_post_g", "ffn2_w1", "ffn2_w3", "ffn2_w2")
_REP_NAMES = tuple(n for n, _ in _SP_NAMES) + ("pool_w", "sgu_ws", "sgu_b", "conv_a_w", "conv_d_w")


def _t(w):
    return jnp.swapaxes(w, -1, -2)


def _step(x, mem, loss_target, W, M, V):
    L = W["w_in"].shape[0]
    S, D = x.shape[1], x.shape[2]
    x0 = x.reshape(S, D)
    memf = mem.reshape(mem.shape[1], D)
    me = 4 * lax.axis_index("x") + 2 * lax.axis_index("y") + lax.axis_index("c")
    FS = W["ffn1_w2"].shape[1]
    KA, KD = W["conv_a_w"].shape[1], W["conv_d_w"].shape[1]
    CS = W["conv_a_w"].shape[2]

    cat = lambda l, parts: jnp.concatenate([(_t(W[n][l]) if tr else W[n][l]) for n, tr in parts], axis=0).astype(CDT)
    pf1 = [cat(l, (("ffn1_w1", 1), ("ffn1_w3", 1), ("ffn1_w2", 0))) for l in range(L)]
    pf2 = [cat(l, (("ffn2_w1", 1), ("ffn2_w3", 1), ("ffn2_w2", 0))) for l in range(L)]
    pma = [cat(l, (("w_in", 1), ("w_gate", 1))) for l in range(L)]
    pwo = [W["w_o"][l].astype(CDT) for l in range(L)]
    pxa = [cat(l, (("xa_wq", 0), ("xa_wk", 0), ("xa_wv", 0), ("xa_wo", 0))) for l in range(L)]
    wbs = [W["w_branch"][l].astype(CDT) for l in range(L)]
    cws = jnp.concatenate([W["conv_a_w"], W["conv_d_w"]], axis=1).reshape(-1, 128)
    sp_all = jnp.concatenate([W[n] for n, _ in _SP_NAMES], axis=1)
    bsc_all = W["sgu_b"][..., None]

    (cwg,) = _exchange([cws], False, "gather_conv_w")
    cwf = cwg.reshape(NS, L, KA + KD, CS).transpose(1, 2, 0, 3).reshape(L, KA + KD, NS * CS)

    def gather_start(l, after):
        return _hgather_start([pf1[l], pf2[l], pma[l], pwo[l], pxa[l], wbs[l]], after, f"gather_start_{l}")

    def gather_rest(h, after, tag):
        mid = _hgather_forward(h, after, f"gather_forward_{tag}")
        return _hgather_wait(mid, mid["token"], f"gather_wait_{tag}")

    packs = [None] * L
    first = [_hgather_start([pf1[0]], cwg, "gather_start_0a")]
    saved = []
    xc = x0
    for l in range(L):
        if l == 0:
            rest = [p[k] for p in (pf1, pf2, pma, pwo, pxa, wbs) for k in range(L) if not (p is pf1 and k == 0)]
            (gf1,) = gather_rest(first[0], [sp_all, bsc_all] + rest, "0a")
            first.append(_hgather_start([pma[0], pwo[0], wbs[0]], gf1, "gather_start_0b"))
        else:
            gf1, gf2, gma, gwo, gxa, gwb = packs[l]
        sp = sp_all[l:l + 1]
        cwa, cwd = cwf[l, :KA], cwf[l, KA:]
        wp, ws, bsc = W["pool_w"][l], W["sgu_ws"][l], bsc_all[l]
        s = {"x0": xc}
        nxt = gather_start(l + 1, gf1) if 0 < l < L - 1 else None
        xc, s["hb1"], s["a1"], s["b1"], s["y1"] = _ffn_fwd(xc, sp, "ffn1_pre_g", "ffn1_post_g", gf1,
                                                            (first[1] if l == 0 else nxt)["token"] if l == 0 or nxt else sp)
        s["x1"] = xc
        if l == 0:
            gma, gwo, gwb = gather_rest(first[1], xc, "0b")
            first.append(_hgather_start([pxa[0]], gma, "gather_start_0c"))
            first.append(_hgather_start([pf2[0]], first[2]["token"], "gather_start_0d"))
            nxt = gather_start(1, first[3]["token"]) if L > 1 else None
        s["hbm"], s["z"], s["g"] = _mix_in(xc, sp, gma, (nxt or first[3])["token"] if l == 0 else sp)
        s["ma"] = _mixA_fwd(s["z"], cwa, sp)
        s["mb"] = _mixB_fwd(s["z"], wp, sp)
        s["mc"] = _mixC_fwd(s["z"], ws, bsc, sp)
        s["yd"] = _mixD_conv_fwd(s["z"], cwd, sp)
        xc, s["md"], s["yk"], s["mg"], s["mo"] = _merge_fwd(s["ma"], s["mb"], s["mc"], s["yd"], s["g"], gwb, gwo, xc, sp)
        s["x2"] = xc
        if l == 0:
            (gxa,) = gather_rest(first[2], xc, "0c")
        s["mn"], s["k"], s["v"] = _xa_kv(memf, sp, gxa)
        xc, s["hbx"], s["q"], s["o"], s["po"] = _xa_fwd(xc, s["k"], s["v"], sp, gxa)
        s["x3"] = xc
        if l == 0:
            (gf2,) = gather_rest(first[3], xc, "0d")
            packs[0] = (gf1, gf2, gma, gwo, gxa, gwb)
        mid = _hgather_forward(nxt, xc, f"gather_forward_{l + 1}") if nxt and l > 0 else None
        xc, s["hb2"], s["a2"], s["b2"], s["y2"] = _ffn_fwd(xc, sp, "ffn2_pre_g", "ffn2_post_g", gf2,
                                                            mid["token"] if mid else sp)
        saved.append(s)
        if nxt:
            mid = mid or _hgather_forward(nxt, xc, f"gather_forward_{l + 1}")
            packs[l + 1] = _hgather_wait(mid, xc, f"gather_wait_{l + 1}")

    dx, lpart = _loss_head(xc, loss_target.reshape(S, D))
    loss = lax.psum(lpart[0, 0], ("x", "y", "c"))

    rep = {n: [None] * L for n in _REP_NAMES}
    summed = [dict() for _ in range(L)]
    pending = None
    last = []
    for l in reversed(range(L)):
        gf1, gf2, gma, gwo, gxa, gwb = packs[l]
        sp = sp_all[l:l + 1]
        cwa, cwd = cwf[l, :KA], cwf[l, KA:]
        wp, ws, bsc = W["pool_w"][l], W["sgu_ws"][l], bsc_all[l]
        s = saved[l]

        dx, dyb, da, db, gp = _ffn_bwd_act(dx, s["x3"], s["y2"], s["a2"], s["b2"], sp, "ffn2_pre_g", "ffn2_post_g", gf2,
                                           pending[1]["token"] if pending else sp)
        rep["ffn2_pre_g"][l], rep["ffn2_post_g"][l] = gp[0], gp[1]
        d_f2 = _ffn_bwd_w(s["hb2"], dyb, s["a2"], s["b2"], da, db, sp)
        if l == 0:
            last.append(_exchange_start([d_f2], (True,), dx, "scatter_start_0a"))

        dx, dpo, dq, dk, dv, gp = _xa_bwd_act(dx, s["x2"], s["po"], s["q"], s["k"], s["v"], sp, gxa,
                                              last[-1]["token"] if last else sp)
        rep["xa_pre_g"][l], rep["xa_post_g"][l] = gp[0], gp[1]
        d_xa = _xa_bwd_w(s["hbx"], dq, s["o"], dpo, s["mn"], dk, dv)
        rep["mem_g"][l] = _xa_kv_bwd(memf, dk, dv, sp, gxa)[0]
        if l == 0:
            last.append(_exchange_start([d_xa], (True,), dx, "scatter_start_0b"))

        dmo, dm, dgp, dyk, gp = _merge_bwd_act(dx, s["mo"], s["g"], s["yk"], gwb, gwo, sp, last[-1]["token"] if last else sp)
        rep["mix_post_g"][l] = gp[0]
        d_wb, d_wo = _merge_bwd_w(s["ma"], s["mb"], s["mc"], s["md"], dyk, s["mg"], dmo)
        dz, dcw, gp = _mixA_bwd(s["z"], dm, cwa, sp)
        rep["conv_a_w"][l], rep["conv_a_b"][l] = dcw, gp[0]
        dz, dwp, gp = _mixB_bwd(s["z"], dm, wp, sp, dz)
        rep["pool_w"][l], rep["pool_scale"][l] = dwp, gp[0]
        dz, dws, dbs, gp = _mixC_bwd(s["z"], dm, ws, bsc, sp, dz)
        rep["sgu_ws"][l], rep["sgu_b"][l], rep["sgu_ln_g"][l], rep["sgu_ln_b"][l] = dws, dbs[:, :, 0], gp[0], gp[1]
        dyd, gp = _mixD_ln_bwd(dm, s["yd"], sp)
        rep["conv_d_ln_g"][l], rep["conv_d_ln_b"][l] = gp[0], gp[1]
        dz, dcw, gp = _mixD_conv_bwd(s["z"], dyd, cwd, dz)
        rep["conv_d_w"][l], rep["conv_d_b"][l] = dcw, gp[0]
        dx, gp = _mix_in_bwd_act(dz, dgp, dx, s["x1"], sp, gma)
        rep["mix_pre_g"][l] = gp[0]
        d_ma, dbg = _mix_in_bwd_w(dz, dgp, s["hbm"])
        rep["b_gate"][l] = dbg[:, 0, :].reshape(-1)
        if l == 0:
            last.append(_exchange_start([d_ma, d_wo, d_wb], (True,) * 3, dx, "scatter_start_0c"))

        dx, dyb, da, db, gp = _ffn_bwd_act(dx, s["x0"], s["y1"], s["a1"], s["b1"], sp, "ffn1_pre_g", "ffn1_post_g", gf1,
                                           last[-1]["token"] if last else sp)
        rep["ffn1_pre_g"][l], rep["ffn1_post_g"][l] = gp[0], gp[1]
        flat = jnp.concatenate([rep[n][l].reshape(-1) for n in _REP_NAMES])
        flat = jnp.pad(flat, (0, -flat.size % 2048)).reshape(-1, 128).astype(CDT)
        if l == 0:
            last.append(_exchange_start([flat], (False,), dx, "scatter_start_0d"))
        d_f1 = _ffn_bwd_w(s["hb1"], dyb, s["a1"], s["b1"], da, db, last[-1]["token"] if last else sp)

        if pending:
            r = _exchange_wait(pending[1], dx, f"scatter_wait_{pending[0]}")
            summed[pending[0]] = dict(zip(("f1", "f2", "ma", "wo", "xa", "wb", "flat"), r))
        if l == 0:
            last.append(_exchange_start([d_f1], (True,), dx, "scatter_start_0e"))
        else:
            pending = (l, _exchange_start([d_f1, d_f2, d_ma, d_wo, d_xa, d_wb, flat], (True,) * 6 + (False,), dx,
                                          f"scatter_start_{l}"))

    pack_shape = {"f1": (3 * FS, D), "f2": (3 * FS, D), "ma": (2 * MW, D), "wo": (GW, D), "xa": (4 * GW, D),
                  "wb": (4 * MW, GW), "flat": tuple(flat.shape)}
    stk = {k: lax.empty((L,) + s, F32) for k, s in pack_shape.items()}

    def land(k, r, l):
        stk[k] = _slot_sum_into(stk[k], r.reshape((NS,) + pack_shape[k]), l)

    for l in range(1, L):
        for k, r in summed[l].items():
            land(k, r, l)
    (r,) = _exchange_wait(last[0], dx, "scatter_wait_0a")
    land("f2", r, 0)
    (r,) = _exchange_wait(last[1], dx, "scatter_wait_0b")
    land("xa", r, 0)

    G, deltas, new_m, new_v = {}, {}, {}, {}

    def update_block(n, k, blk, transposed):
        tr = _t if transposed else (lambda a: a)
        out = _adamw_block(tr(W[n]), stk[k], tr(M[n]), tr(V[n]), blk)
        G[n], deltas[n], new_m[n], new_v[n] = (tr(a) for a in out)
        return deltas[n]

    def update(n):
        deltas[n], new_m[n], new_v[n] = _adamw(W[n], G[n], M[n], V[n])
        return deltas[n]

    done = [update_block("ffn2_w1", "f2", 0, True), update_block("ffn2_w3", "f2", 1, True),
            update_block("ffn2_w2", "f2", 2, False)]
    done += [update_block(n, "xa", i, False) for i, n in enumerate(("xa_wq", "xa_wk", "xa_wv", "xa_wo"))]
    r = _exchange_wait(last[2], done + [stk[k] for k in ("f1", "ma", "wo", "wb", "flat")], "scatter_wait_0c")
    for k, v in zip(("ma", "wo", "wb"), r):
        land(k, v, 0)
    G["w_in"], G["w_gate"] = _t(stk["ma"][:, :MW]), _t(stk["ma"][:, MW:])
    G["w_branch"] = stk["wb"].reshape(W["w_branch"].shape)
    done = [update("w_in"), update("w_gate"), update("w_branch"), update_block("w_o", "wo", 0, False)]
    (r,) = _exchange_wait(last[3], done, "scatter_wait_0d")
    land("flat", r, 0)

    tot = [stk["flat"][l].reshape(-1) for l in range(L)]
    off = 0
    for n in _REP_NAMES:
        shape = (KA, NS * CS) if n == "conv_a_w" else (KD, NS * CS) if n == "conv_d_w" else W[n].shape[1:]
        size = 1
        for d in shape:
            size *= d
        G[n] = jnp.stack([tot[l][off:off + size].reshape(shape) for l in range(L)])
        off += size
    for n in ("conv_a_w", "conv_d_w"):
        G[n] = lax.dynamic_slice_in_dim(G[n], me * CS, CS, axis=2)
    done = [update(n) for n in _REP_NAMES]

    (r,) = _exchange_wait(last[4], done, "scatter_wait_0e")
    land("f1", r, 0)
    update_block("ffn1_w1", "f1", 0, True)
    update_block("ffn1_w3", "f1", 1, True)
    update_block("ffn1_w2", "f1", 2, False)
    grad_x = dx.reshape(x.shape)
    return (loss, grad_x, *[G[n] for n in _W_NAMES], *[deltas[n] for n in _W_NAMES],
            *[new_m[n] for n in _W_NAMES], *[new_v[n] for n in _W_NAMES])


def kernel(x, mem, ffn1_pre_g, ffn1_post_g, ffn1_w1, ffn1_w3, ffn1_w2, mix_pre_g, mix_post_g, w_in, conv_a_w, conv_a_b, pool_w, pool_scale, sgu_ln_g, sgu_ln_b, sgu_ws, sgu_b, conv_d_w, conv_d_b, conv_d_ln_g, conv_d_ln_b, w_branch, w_gate, b_gate, w_o, xa_pre_g, xa_post_g, mem_g, xa_wq, xa_wk, xa_wv, xa_wo, ffn2_pre_g, ffn2_post_g, ffn2_w1, ffn2_w3, ffn2_w2, loss_target, m_ffn1_pre_g, m_ffn1_post_g, m_ffn1_w1, m_ffn1_w3, m_ffn1_w2, m_mix_pre_g, m_mix_post_g, m_w_in, m_conv_a_w, m_conv_a_b, m_pool_w, m_pool_scale, m_sgu_ln_g, m_sgu_ln_b, m_sgu_ws, m_sgu_b, m_conv_d_w, m_conv_d_b, m_conv_d_ln_g, m_conv_d_ln_b, m_w_branch, m_w_gate, m_b_gate, m_w_o, m_xa_pre_g, m_xa_post_g, m_mem_g, m_xa_wq, m_xa_wk, m_xa_wv, m_xa_wo, m_ffn2_pre_g, m_ffn2_post_g, m_ffn2_w1, m_ffn2_w3, m_ffn2_w2, v_ffn1_pre_g, v_ffn1_post_g, v_ffn1_w1, v_ffn1_w3, v_ffn1_w2, v_mix_pre_g, v_mix_post_g, v_w_in, v_conv_a_w, v_conv_a_b, v_pool_w, v_pool_scale, v_sgu_ln_g, v_sgu_ln_b, v_sgu_ws, v_sgu_b, v_conv_d_w, v_conv_d_b, v_conv_d_ln_g, v_conv_d_ln_b, v_w_branch, v_w_gate, v_b_gate, v_w_o, v_xa_pre_g, v_xa_post_g, v_mem_g, v_xa_wq, v_xa_wk, v_xa_wv, v_xa_wo, v_ffn2_pre_g, v_ffn2_post_g, v_ffn2_w1, v_ffn2_w3, v_ffn2_w2):
    args = dict(locals())
    W = {n: args[n] for n in _W_NAMES}
    M = {n: args["m_" + n] for n in _W_NAMES}
    V = {n: args["v_" + n] for n in _W_NAMES}
    return _step(x, mem, loss_target, W, M, V)
```

```python
import jax
import jax.numpy as jnp
from jax import lax
from jax.experimental import pallas as pl
from jax.experimental.pallas import tpu as pltpu

F32 = jnp.float32
CDT = jnp.bfloat16
EPS = 1e-6
NS = 8
GW = 128
MW = 512
CHUNK = 64
XA_HEADS = 4
POOL_WINDOWS = (2, 4, 8, 16)
VMEM_LIMIT = 56 * 1024 * 1024
ADAM_LR, ADAM_B1, ADAM_B2, ADAM_EPS, ADAM_WD, ADAM_STEP = 0.001, 0.9, 0.999, 1e-08, 0.01, 10

SDS = jax.ShapeDtypeStruct

_SP_NAMES = (("ffn1_pre_g", 1024), ("ffn1_post_g", 1024), ("mix_pre_g", 1024), ("mix_post_g", 1024),
             ("xa_pre_g", 1024), ("xa_post_g", 1024), ("mem_g", 1024), ("ffn2_pre_g", 1024), ("ffn2_post_g", 1024),
             ("conv_a_b", 512), ("pool_scale", 512), ("sgu_ln_g", 512), ("sgu_ln_b", 512), ("conv_d_b", 512),
             ("conv_d_ln_g", 512), ("conv_d_ln_b", 512), ("b_gate", 4096))
_SP = {}
_off = 0
for _n, _w in _SP_NAMES:
    _SP[_n] = (_off, _w)
    _off += _w
_SP_TOTAL = _off


def _call(body, name, grid, in_specs, out_specs, out_shape, scratch=(), aliases=None):
    return pl.pallas_call(
        body, name=name, grid=grid, in_specs=in_specs, out_specs=out_specs, out_shape=out_shape,
        scratch_shapes=list(scratch), input_output_aliases=aliases or {},
        compiler_params=pltpu.CompilerParams(dimension_semantics=("arbitrary",) * len(grid),
                                             vmem_limit_bytes=VMEM_LIMIT))


def _nn(a, b):
    return lax.dot_general(a, b, (((1,), (0,)), ((), ())), preferred_element_type=F32)


def _nt(a, b):
    return lax.dot_general(a, b, (((1,), (1,)), ((), ())), preferred_element_type=F32)


def _tn(a, b):
    return lax.dot_general(a, b, (((0,), (0,)), ((), ())), preferred_element_type=F32)


def _rms(x):
    r = lax.rsqrt(jnp.mean(x * x, axis=-1, keepdims=True) + EPS)
    return x * r, r


def _rms_bwd(n, r, g, dout):
    dn = dout * g
    dx = r * (dn - n * jnp.mean(dn * n, axis=-1, keepdims=True))
    return dx, jnp.sum(dout * n, axis=0, keepdims=True)


def _ln(y):
    mu = jnp.mean(y, axis=-1, keepdims=True)
    yc = y - mu
    rs = lax.rsqrt(jnp.mean(yc * yc, axis=-1, keepdims=True) + EPS)
    return yc * rs, rs


def _ln_bwd(xh, rs, dxh):
    return rs * (dxh - jnp.mean(dxh, axis=-1, keepdims=True) - xh * jnp.mean(dxh * xh, axis=-1, keepdims=True))


def _silu_parts(a):
    s = jax.nn.sigmoid(a)
    sl = a * s
    return sl, s + sl * (1.0 - s)


_GELU_C = 0.7978845608028654
_GELU_A = 0.044715


def _gelu(x):
    return 0.5 * x * (1.0 + jnp.tanh(_GELU_C * (x + _GELU_A * x * x * x)))


def _gelu_parts(x):
    t = jnp.tanh(_GELU_C * (x + _GELU_A * x * x * x))
    g = 0.5 * x * (1.0 + t)
    dg = 0.5 * (1.0 + t) + 0.5 * x * (1.0 - t * t) * _GELU_C * (1.0 + 3.0 * _GELU_A * x * x)
    return g, dg


def _spspec(name, width, imap):
    off = _SP[name][0]
    assert off % width == 0
    return pl.BlockSpec((1, width), lambda *a: (0, off // width + imap(*a)))


def _zero(*a):
    return 0


def _row_once(tm, d):
    return pl.BlockSpec((tm, d), lambda i, j: (i, 0), pipeline_mode=pl.Buffered(1))


FFN_SG = 2


def _ffn_fwd(x, sp, pre, post, pf, dep):
    S, D = x.shape
    FS = pf.shape[1] // 3
    TM = min(512, S)
    SG, NG, W = FFN_SG, NS // FFN_SG, FFN_SG * FS

    def body(x_ref, pg_ref, qg_ref, w1_ref, w3_ref, w2_ref, dep_ref, xo_ref, hb_ref, a_ref, b_ref, y_ref, hb_s, acc):
        j = pl.program_id(1)

        @pl.when(j == 0)
        def _():
            n, _ = _rms(x_ref[...])
            hb = (n * pg_ref[...]).astype(CDT)
            hb_s[...] = hb
            hb_ref[...] = hb
            acc[...] = jnp.zeros_like(acc)

        hb = hb_s[...]
        a = _nt(hb, w1_ref[...].reshape(W, D))
        b = _nt(hb, w3_ref[...].reshape(W, D))
        a_ref[...] = a.astype(CDT)
        b_ref[...] = b.astype(CDT)
        u = (a * jax.nn.sigmoid(a) * b).astype(CDT)
        acc[...] += _nn(u, w2_ref[...].reshape(W, D))

        @pl.when(j == NG - 1)
        def _():
            y = acc[...]
            y_ref[...] = y.astype(CDT)
            n, _ = _rms(y)
            xo_ref[...] = x_ref[...] + 0.5 * (n * qg_ref[...])

    row1 = pl.BlockSpec((TM, D), lambda i, j: (i, 0))
    grp = lambda i, j: (j, i, 0)
    return _call(
        body, "ffn_fwd", (S // TM, NG),
        [row1, _spspec(pre, D, _zero), _spspec(post, D, _zero),
         pl.BlockSpec((SG, FS, D), lambda i, j: (j, 0, 0)), pl.BlockSpec((SG, FS, D), lambda i, j: (j, 1, 0)),
         pl.BlockSpec((SG, FS, D), lambda i, j: (j, 2, 0)), pl.BlockSpec(memory_space=pl.ANY)],
        [row1, row1, pl.BlockSpec((None, TM, W), grp), pl.BlockSpec((None, TM, W), grp), row1],
        [SDS((S, D), F32), SDS((S, D), CDT), SDS((NG, S, W), CDT), SDS((NG, S, W), CDT), SDS((S, D), CDT)],
        [pltpu.VMEM((TM, D), CDT), pltpu.VMEM((TM, D), F32)])(x, sp, sp, pf, pf, pf, dep)


def _ffn_bwd_act(dxo, x, y, a, b, sp, pre, post, pf, dep):
    S, D = x.shape
    FS = pf.shape[1] // 3
    TM = min(512, S)
    SG, NG, W = FFN_SG, NS // FFN_SG, FFN_SG * FS

    def body(dxo_ref, x_ref, y_ref, a_ref, b_ref, pg_ref, qg_ref, w1_ref, w3_ref, w2_ref, dep_ref,
             dx_ref, dyb_ref, da_ref, db_ref, gp_ref, dyb_s, acc):
        i = pl.program_id(0)
        j = pl.program_id(1)

        @pl.when((i == 0) & (j == 0))
        def _():
            gp_ref[...] = jnp.zeros_like(gp_ref)

        RC = min(64, TM)

        @pl.when(j == 0)
        def _():
            def chunk(c, dg):
                r = pl.ds(pl.multiple_of(c * RC, RC), RC)
                n, rr = _rms(y_ref[r, :].astype(F32))
                dy, dgc = _rms_bwd(n, rr, qg_ref[...], 0.5 * dxo_ref[r, :])
                dyb = dy.astype(CDT)
                dyb_s[r, :] = dyb
                dyb_ref[r, :] = dyb
                acc[r, :] = jnp.zeros((RC, D), F32)
                return dg + dgc

            gp_ref[1:2, :] += lax.fori_loop(0, TM // RC, chunk, jnp.zeros((1, D), F32))

        sl, dsl = _silu_parts(a_ref[...].astype(F32))
        du = _nt(dyb_s[...], w2_ref[...].reshape(W, D))
        db = (du * sl).astype(CDT)
        da = (du * b_ref[...].astype(F32) * dsl).astype(CDT)
        da_ref[...] = da
        db_ref[...] = db
        acc[...] += _nn(da, w1_ref[...].reshape(W, D)) + _nn(db, w3_ref[...].reshape(W, D))

        @pl.when(j == NG - 1)
        def _():
            def chunk(c, dg):
                r = pl.ds(pl.multiple_of(c * RC, RC), RC)
                n, rr = _rms(x_ref[r, :])
                dx, dgc = _rms_bwd(n, rr, pg_ref[...], acc[r, :])
                dx_ref[r, :] = dxo_ref[r, :] + dx
                return dg + dgc

            gp_ref[0:1, :] += lax.fori_loop(0, TM // RC, chunk, jnp.zeros((1, D), F32))

    row = lambda i, j: (i, 0)
    grp = lambda i, j: (j, i, 0)
    return _call(
        body, "ffn_bwd_act", (S // TM, NG),
        [pl.BlockSpec((TM, D), row), pl.BlockSpec((TM, D), row), pl.BlockSpec((TM, D), row),
         pl.BlockSpec((None, TM, W), grp), pl.BlockSpec((None, TM, W), grp),
         _spspec(pre, D, _zero), _spspec(post, D, _zero),
         pl.BlockSpec((SG, FS, D), lambda i, j: (j, 0, 0)), pl.BlockSpec((SG, FS, D), lambda i, j: (j, 1, 0)),
         pl.BlockSpec((SG, FS, D), lambda i, j: (j, 2, 0)), pl.BlockSpec(memory_space=pl.ANY)],
        [pl.BlockSpec((TM, D), row), pl.BlockSpec((TM, D), row), pl.BlockSpec((None, TM, W), grp),
         pl.BlockSpec((None, TM, W), grp), pl.BlockSpec((8, D), lambda i, j: (0, 0))],
        [SDS((S, D), F32), SDS((S, D), CDT), SDS((NG, S, W), CDT), SDS((NG, S, W), CDT), SDS((8, D), F32)],
        [pltpu.VMEM((TM, D), CDT), pltpu.VMEM((TM, D), F32)])(dxo, x, y, a, b, sp, sp, pf, pf, pf, dep)


def _ffn_bwd_w(hb, dyb, a, b, da, db, dep):
    S, D = hb.shape
    SG, NG = FFN_SG, NS // FFN_SG
    W = a.shape[2]
    FS = W // SG
    TK = min(1024, S)
    NK = S // TK

    def body(hb_ref, dyb_ref, a_ref, b_ref, da_ref, db_ref, dep_ref, g_ref, acc):
        k = pl.program_id(1)

        @pl.when(k == 0)
        def _():
            acc[...] = jnp.zeros_like(acc)

        af = a_ref[...].astype(F32)
        u = (af * jax.nn.sigmoid(af) * b_ref[...].astype(F32)).astype(CDT)
        hb = hb_ref[...]
        acc[0:W, :] += _tn(da_ref[...], hb)
        acc[W:2 * W, :] += _tn(db_ref[...], hb)
        acc[2 * W:3 * W, :] += _tn(u, dyb_ref[...])

        @pl.when(k == NK - 1)
        def _():
            for s in range(SG):
                for r in range(3):
                    g_ref[s, r * FS:(r + 1) * FS, :] = acc[r * W + s * FS:r * W + (s + 1) * FS, :].astype(CDT)

    row = lambda j, k: (k, 0)
    grp = lambda j, k: (j, k, 0)
    return _call(
        body, "ffn_bwd_w", (NG, NK),
        [pl.BlockSpec((TK, D), row), pl.BlockSpec((TK, D), row)] + [pl.BlockSpec((None, TK, W), grp)] * 4 + [_ANY],
        pl.BlockSpec((SG, 3 * FS, D), lambda j, k: (j, 0, 0)),
        SDS((NS, 3 * FS, D), CDT),
        [pltpu.VMEM((3 * W, D), F32)])(hb, dyb, a, b, da, db, dep)


def _mix_in(x, sp, pma, dep):
    S, D = x.shape
    TM = min(1024, S)

    def body(x_ref, pg_ref, bg_ref, wi_ref, wg_ref, dep_ref, hb_ref, z_ref, g_ref, hb_s):
        @pl.when(pl.program_id(1) == 0)
        def _():
            n, _ = _rms(x_ref[...])
            hb = (n * pg_ref[...]).astype(CDT)
            hb_s[...] = hb
            hb_ref[...] = hb

        hb = hb_s[...]
        z_ref[...] = _nt(hb, wi_ref[...]).astype(CDT)
        g_ref[...] = jax.nn.sigmoid(_nt(hb, wg_ref[...]) + bg_ref[...]).astype(CDT)

    return _call(
        body, "mix_in", (S // TM, NS),
        [_row_once(TM, D), _spspec("mix_pre_g", D, _zero), _spspec("b_gate", MW, lambda i, j: j),
         pl.BlockSpec((None, MW, D), lambda i, j: (j, 0, 0)), pl.BlockSpec((None, MW, D), lambda i, j: (j, 1, 0)), _ANY],
        [_row_once(TM, D), pl.BlockSpec((None, TM, MW), lambda i, j: (j, i, 0)),
         pl.BlockSpec((None, TM, MW), lambda i, j: (j // 2, i, j % 2))],
        [SDS((S, D), CDT), SDS((NS, S, MW), CDT), SDS((4, S, D), CDT)],
        [pltpu.VMEM((TM, D), CDT)])(x, sp, sp, pma, pma, dep)


def _mix_in_bwd_act(dz, dgp, dxr, x, sp, pma):
    S, D = x.shape
    TM = min(1024, S)

    def body(dz_ref, dg_ref, dxr_ref, x_ref, pg_ref, w_ref, dx_ref, gp_ref, acc):
        i = pl.program_id(0)
        j = pl.program_id(1)

        @pl.when((i == 0) & (j == 0))
        def _():
            gp_ref[...] = jnp.zeros_like(gp_ref)

        @pl.when(j == 0)
        def _():
            acc[...] = jnp.zeros_like(acc)

        acc[...] += _nn(jnp.concatenate([dz_ref[...], dg_ref[...]], axis=1), w_ref[...])

        @pl.when(j == NS - 1)
        def _():
            n, r = _rms(x_ref[...])
            dx, dg = _rms_bwd(n, r, pg_ref[...], acc[...])
            dx_ref[...] = dxr_ref[...] + dx
            gp_ref[0:1, :] += dg

    return _call(
        body, "mix_in_bwd_act", (S // TM, NS),
        [pl.BlockSpec((None, TM, MW), lambda i, j: (j, i, 0)), pl.BlockSpec((None, TM, MW), lambda i, j: (j // 2, i, j % 2)),
         _row_once(TM, D), _row_once(TM, D), _spspec("mix_pre_g", D, _zero),
         pl.BlockSpec((None, 2 * MW, D), lambda i, j: (j, 0, 0))],
        [_row_once(TM, D), pl.BlockSpec((8, D), lambda i, j: (0, 0))],
        [SDS((S, D), F32), SDS((8, D), F32)],
        [pltpu.VMEM((TM, D), F32)])(dz, dgp, dxr, x, sp, pma)


def _mix_in_bwd_w(dz, dgp, hb):
    S, D = hb.shape
    TK = min(2048, S)
    NK = S // TK

    def body(dz_ref, dg_ref, hb_ref, g_ref, bg_ref, acc):
        k = pl.program_id(1)

        @pl.when(k == 0)
        def _():
            acc[...] = jnp.zeros_like(acc)
            bg_ref[...] = jnp.zeros_like(bg_ref)

        hb = hb_ref[...]
        dg = dg_ref[...]
        acc[0:MW, :] += _tn(dz_ref[...], hb)
        acc[MW:2 * MW, :] += _tn(dg, hb)
        bg_ref[0:1, :] += jnp.sum(dg.astype(F32), axis=0, keepdims=True)

        @pl.when(k == NK - 1)
        def _():
            g_ref[...] = acc[...].astype(CDT)

    return _call(
        body, "mix_in_bwd_w", (NS, NK),
        [pl.BlockSpec((None, TK, MW), lambda j, k: (j, k, 0)), pl.BlockSpec((None, TK, MW), lambda j, k: (j // 2, k, j % 2)),
         pl.BlockSpec((TK, D), lambda j, k: (k, 0))],
        [pl.BlockSpec((None, 2 * MW, D), lambda j, k: (j, 0, 0)), pl.BlockSpec((None, 8, MW), lambda j, k: (j, 0, 0))],
        [SDS((NS, 2 * MW, D), CDT), SDS((NS, 8, MW), F32)],
        [pltpu.VMEM((2 * MW, D), F32)])(dz, dgp, hb)


def _causal_taps(pad_ref, i, ch, halo, k_taps, lanes=slice(None)):
    val = pad_ref[pl.ds(pl.multiple_of(i * ch, 8), ch + halo), lanes]
    base = {}
    out = []
    for k in range(k_taps):
        q, r = divmod(k_taps - 1 - k, 8)
        if r not in base:
            base[r] = pltpu.roll(val, r, 0) if r else val
        out.append((k, base[r][halo - 8 * q:halo - 8 * q + ch, :]))
    return out


def _anti_taps(pad_ref, i, ch, halo, k_taps, lanes=slice(None)):
    val = pad_ref[pl.ds(pl.multiple_of(i * ch, 8), ch + halo), lanes]
    n = ch + halo
    base = {}
    out = []
    for k in range(k_taps):
        q, r = divmod(k_taps - 1 - k, 8)
        if r not in base:
            base[r] = pltpu.roll(val, n - r, 0) if r else val
        out.append((k, base[r][8 * q:8 * q + ch, :]))
    return out


def _conv_geometry(S, k_taps):
    halo = 8 * ((k_taps - 1 + 7) // 8)
    ch = min(256, S)
    return halo, ch, S // ch


def _rows(i, ch):
    return pl.ds(pl.multiple_of(i * ch, ch), ch)


def _mixA_fwd(z, cw, sp):
    S = z.shape[1]
    K = cw.shape[0]
    H, CH, NCH = _conv_geometry(S, K)

    def body(z_ref, w_ref, b_ref, o_ref, pad):
        pad[0:H, :] = jnp.zeros((H, GW), F32)

        def fill(i, c):
            r = _rows(i, CH)
            pad[pl.ds(pl.multiple_of(i * CH + H, 8), CH), :] = z_ref[2, r, :].astype(F32) * z_ref[0, r, :].astype(F32)
            return c

        lax.fori_loop(0, NCH, fill, 0)

        def conv(i, c):
            r = _rows(i, CH)
            acc = jnp.zeros((CH, GW), F32)
            for k, sh in _causal_taps(pad, i, CH, H, K):
                acc = acc + w_ref[k:k + 1, :] * sh
            o_ref[r, :] = (z_ref[1, r, :].astype(F32) * (acc + b_ref[...])).astype(CDT)
            return c

        lax.fori_loop(0, NCH, conv, 0)

    return _call(
        body, "mixA_fwd", (MW // GW,),
        [pl.BlockSpec((3, S, GW), lambda c: (0, 0, c)), pl.BlockSpec((K, GW), lambda c: (0, c)),
         _spspec("conv_a_b", GW, lambda c: c)],
        pl.BlockSpec((S, GW), lambda c: (0, c)), SDS((S, MW), CDT),
        [pltpu.VMEM((H + S, GW), F32)])(z, cw, sp)


def _mixA_bwd(z, dm, cw, sp):
    S = z.shape[1]
    K = cw.shape[0]
    H, CH, NCH = _conv_geometry(S, K)

    def body(z_ref, dm_ref, w_ref, b_ref, dz_ref, dw_ref, db_ref, pad, dpad, dw_s):
        pad[0:H, :] = jnp.zeros((H, GW), F32)
        dpad[pl.ds(S, H), :] = jnp.zeros((H, GW), F32)
        dw_s[...] = jnp.zeros_like(dw_s)
        db_ref[...] = jnp.zeros_like(db_ref)

        def fill(i, c):
            r = _rows(i, CH)
            pad[pl.ds(pl.multiple_of(i * CH + H, 8), CH), :] = z_ref[2, r, :].astype(F32) * z_ref[0, r, :].astype(F32)
            return c

        lax.fori_loop(0, NCH, fill, 0)

        def p1(i, c):
            r = _rows(i, CH)
            taps = _causal_taps(pad, i, CH, H, K)
            acc = jnp.zeros((CH, GW), F32)
            for k, sh in taps:
                acc = acc + w_ref[k:k + 1, :] * sh
            dmf = dm_ref[r, :].astype(F32)
            dz_ref[1, r, :] = (dmf * (acc + b_ref[...])).astype(CDT)
            dc = dmf * z_ref[1, r, :].astype(F32)
            dpad[r, :] = dc
            for k, sh in taps:
                dw_s[k:k + 1, :] += jnp.sum(dc * sh, axis=0, keepdims=True)
            db_ref[0:1, :] += jnp.sum(dc, axis=0, keepdims=True)
            return c

        lax.fori_loop(0, NCH, p1, 0)

        def p2(i, c):
            r = _rows(i, CH)
            dq = jnp.zeros((CH, GW), F32)
            for k, sh in _anti_taps(dpad, i, CH, H, K):
                dq = dq + w_ref[k:k + 1, :] * sh
            dz_ref[0, r, :] = (dq * z_ref[2, r, :].astype(F32)).astype(CDT)
            dz_ref[2, r, :] = (dq * z_ref[0, r, :].astype(F32)).astype(CDT)
            return c

        lax.fori_loop(0, NCH, p2, 0)
        dw_ref[...] = dw_s[0:K, :]

    return _call(
        body, "mixA_bwd", (MW // GW,),
        [pl.BlockSpec((3, S, GW), lambda c: (0, 0, c)), pl.BlockSpec((None, S, GW), lambda c: (0, 0, c)),
         pl.BlockSpec((K, GW), lambda c: (0, c)), _spspec("conv_a_b", GW, lambda c: c)],
        [pl.BlockSpec((3, S, GW), lambda c: (0, 0, c)), pl.BlockSpec((K, GW), lambda c: (0, c)),
         pl.BlockSpec((8, GW), lambda c: (0, c))],
        [SDS((NS, S, MW), CDT), SDS((K, MW), F32), SDS((8, MW), F32)],
        [pltpu.VMEM((H + S, GW), F32), pltpu.VMEM((S + H, GW), F32), pltpu.VMEM((8 * ((K + 7) // 8), GW), F32)])(z, dm, cw, sp)


def _mixD_conv_fwd(z, cw, sp):
    S = z.shape[1]
    K = cw.shape[0]
    H, CH, NCH = _conv_geometry(S, K)

    def body(z_ref, w_ref, b_ref, o_ref, pad):
        pad[0:H, :] = jnp.zeros((H, GW), F32)

        def fill(i, c):
            r = _rows(i, CH)
            pad[pl.ds(pl.multiple_of(i * CH + H, 8), CH), :] = (
                z_ref[0, r, :].astype(F32) * jax.nn.sigmoid(z_ref[1, r, :].astype(F32)))
            return c

        lax.fori_loop(0, NCH, fill, 0)

        def conv(i, c):
            acc = jnp.zeros((CH, GW), F32)
            for k, sh in _causal_taps(pad, i, CH, H, K):
                acc = acc + w_ref[k:k + 1, :] * sh
            o_ref[_rows(i, CH), :] = (acc + b_ref[...]).astype(CDT)
            return c

        lax.fori_loop(0, NCH, conv, 0)

    return _call(
        body, "mixD_conv_fwd", (MW // GW,),
        [pl.BlockSpec((2, S, GW), lambda c: (3, 0, c)), pl.BlockSpec((K, GW), lambda c: (0, c)),
         _spspec("conv_d_b", GW, lambda c: c)],
        pl.BlockSpec((S, GW), lambda c: (0, c)), SDS((S, MW), CDT),
        [pltpu.VMEM((H + S, GW), F32)])(z, cw, sp)


def _mixD_conv_bwd(z, dy, cw, dz):
    S = z.shape[1]
    K = cw.shape[0]
    H, CH, NCH = _conv_geometry(S, K)

    def body(z_ref, dy_ref, w_ref, dzin_ref, dz_ref, dw_ref, db_ref, pad, dpad, dw_s):
        pad[0:H, :] = jnp.zeros((H, GW), F32)
        dpad[pl.ds(S, H), :] = jnp.zeros((H, GW), F32)
        dw_s[...] = jnp.zeros_like(dw_s)
        db_ref[...] = jnp.zeros_like(db_ref)

        def fill(i, c):
            r = _rows(i, CH)
            pad[pl.ds(pl.multiple_of(i * CH + H, 8), CH), :] = (
                z_ref[0, r, :].astype(F32) * jax.nn.sigmoid(z_ref[1, r, :].astype(F32)))
            dpad[r, :] = dy_ref[r, :].astype(F32)
            return c

        lax.fori_loop(0, NCH, fill, 0)

        def p1(i, c):
            dyf = dy_ref[_rows(i, CH), :].astype(F32)
            for k, sh in _causal_taps(pad, i, CH, H, K):
                dw_s[k:k + 1, :] += jnp.sum(dyf * sh, axis=0, keepdims=True)
            db_ref[0:1, :] += jnp.sum(dyf, axis=0, keepdims=True)
            return c

        lax.fori_loop(0, NCH, p1, 0)

        def p2(i, c):
            r = _rows(i, CH)
            dq = jnp.zeros((CH, GW), F32)
            for k, sh in _anti_taps(dpad, i, CH, H, K):
                dq = dq + w_ref[k:k + 1, :] * sh
            a = z_ref[0, r, :].astype(F32)
            sg = jax.nn.sigmoid(z_ref[1, r, :].astype(F32))
            dz_ref[0, r, :] = (dq * sg).astype(CDT)
            dz_ref[1, r, :] = (dq * a * sg * (1.0 - sg)).astype(CDT)
            return c

        lax.fori_loop(0, NCH, p2, 0)
        dw_ref[...] = dw_s[0:K, :]

    return _call(
        body, "mixD_conv_bwd", (MW // GW,),
        [pl.BlockSpec((2, S, GW), lambda c: (3, 0, c)), pl.BlockSpec((S, GW), lambda c: (0, c)),
         pl.BlockSpec((K, GW), lambda c: (0, c)), _ANY],
        [pl.BlockSpec((2, S, GW), lambda c: (3, 0, c)), pl.BlockSpec((K, GW), lambda c: (0, c)),
         pl.BlockSpec((8, GW), lambda c: (0, c))],
        [SDS((NS, S, MW), CDT), SDS((K, MW), F32), SDS((8, MW), F32)],
        [pltpu.VMEM((H + S, GW), F32), pltpu.VMEM((S + H, GW), F32), pltpu.VMEM((8 * ((K + 7) // 8), GW), F32)],
        aliases={3: 0})(z, dy, cw, dz)


def _mixD_ln_bwd(dm, yd, sp):
    S = yd.shape[0]
    TM = min(512, S)

    def body(dm_ref, y_ref, lg_ref, lb_ref, dy_ref, gp_ref):
        @pl.when(pl.program_id(0) == 0)
        def _():
            gp_ref[...] = jnp.zeros_like(gp_ref)

        xh, rs = _ln(y_ref[...].astype(F32))
        _, dsl = _silu_parts(xh * lg_ref[...] + lb_ref[...])
        dl = dm_ref[...].astype(F32) * dsl
        gp_ref[0:1, :] += jnp.sum(dl * xh, axis=0, keepdims=True)
        gp_ref[1:2, :] += jnp.sum(dl, axis=0, keepdims=True)
        dy_ref[...] = _ln_bwd(xh, rs, dl * lg_ref[...]).astype(CDT)

    row = lambda i: (i, 0)
    return _call(
        body, "mixD_ln_bwd", (S // TM,),
        [pl.BlockSpec((None, TM, MW), lambda i: (3, i, 0)), pl.BlockSpec((TM, MW), row), _spspec("conv_d_ln_g", MW, _zero),
         _spspec("conv_d_ln_b", MW, _zero)],
        [pl.BlockSpec((TM, MW), row), pl.BlockSpec((8, MW), lambda i: (0, 0))],
        [SDS((S, MW), CDT), SDS((8, MW), F32)])(dm, yd, sp, sp)


def _box_causal(val, g):
    s = val
    for d in range(g + 1):
        s = s + pltpu.roll(s, 1 << d, 0)
    return s


def _box_anti(val, g):
    n = val.shape[0]
    s = val
    for d in range(g + 1):
        s = s + pltpu.roll(s, n - (1 << d), 0)
    return s


def _pool_count(i, ch, win):
    t = lax.broadcasted_iota(jnp.int32, (ch, GW), 0) + (i * ch + 1)
    return jnp.minimum(t, win).astype(F32)


def _mixB_fwd(z, wp, sp):
    S = z.shape[1]
    H, CH = 16, min(256, S)
    NCH = S // CH
    assert POOL_WINDOWS == tuple(2 << g for g in range(4))

    def body(p_ref, wp_ref, sc_ref, o_ref, pad):
        pad[0:H, :] = jnp.zeros((H, MW), F32)

        def fill(i, c):
            pad[pl.ds(pl.multiple_of(i * CH + H, 8), CH), :] = p_ref[_rows(i, CH), :].astype(F32)
            return c

        lax.fori_loop(0, NCH, fill, 0)

        def step(i, c):
            r = _rows(i, CH)
            for g in range(4):
                gs = slice(g * GW, (g + 1) * GW)
                val = pad[pl.ds(pl.multiple_of(i * CH, 8), CH + H), gs]
                pooled = _box_causal(val, g)[H:, :] / _pool_count(i, CH, POOL_WINDOWS[g]) - val[H:, :]
                mixed = _nn(pooled.astype(CDT), wp_ref[g].astype(CDT))
                o_ref[r, gs] = (mixed * sc_ref[:, gs]).astype(CDT)
            return c

        lax.fori_loop(0, NCH, step, 0)

    return _call(
        body, "mixB_fwd", (1,),
        [pl.BlockSpec((None, S, MW), lambda i: (3, 0, 0)), pl.BlockSpec((4, GW, GW), lambda i: (0, 0, 0)),
         _spspec("pool_scale", MW, _zero)],
        pl.BlockSpec((S, MW), lambda i: (0, 0)), SDS((S, MW), CDT),
        [pltpu.VMEM((H + S, MW), F32)])(z, wp, sp)


def _mixB_bwd(z, dm, wp, sp, dz):
    S = z.shape[1]
    H, CH = 16, min(256, S)
    NCH = S // CH

    def body(p_ref, dm_ref, wp_ref, sc_ref, dzin_ref, dz_ref, dwp_ref, dsc_ref, pad, rpad):
        pad[0:H, :] = jnp.zeros((H, MW), F32)
        rpad[pl.ds(S, H), :] = jnp.zeros((H, MW), F32)
        dwp_ref[...] = jnp.zeros_like(dwp_ref)
        dsc_ref[...] = jnp.zeros_like(dsc_ref)

        def fill(i, c):
            pad[pl.ds(pl.multiple_of(i * CH + H, 8), CH), :] = p_ref[_rows(i, CH), :].astype(F32)
            return c

        lax.fori_loop(0, NCH, fill, 0)

        def p1(i, c):
            r = _rows(i, CH)
            for g in range(4):
                gs = slice(g * GW, (g + 1) * GW)
                cnt = _pool_count(i, CH, POOL_WINDOWS[g])
                val = pad[pl.ds(pl.multiple_of(i * CH, 8), CH + H), gs]
                pooled = (_box_causal(val, g)[H:, :] / cnt - val[H:, :]).astype(CDT)
                w = wp_ref[g].astype(CDT)
                mixed = _nn(pooled, w)
                dmf = dm_ref[r, gs].astype(F32)
                dsc_ref[0:1, gs] += jnp.sum(dmf * mixed, axis=0, keepdims=True)
                dmx = (dmf * sc_ref[:, gs]).astype(CDT)
                dwp_ref[g] += _tn(pooled, dmx)
                rpad[r, gs] = _nt(dmx, w) / cnt
            return c

        lax.fori_loop(0, NCH, p1, 0)

        def p2(i, c):
            r = _rows(i, CH)
            for g in range(4):
                gs = slice(g * GW, (g + 1) * GW)
                val = rpad[pl.ds(pl.multiple_of(i * CH, 8), CH + H), gs]
                dp = _box_anti(val, g)[:CH, :] - val[:CH, :] * _pool_count(i, CH, POOL_WINDOWS[g])
                dz_ref[r, gs] = dp.astype(CDT)
            return c

        lax.fori_loop(0, NCH, p2, 0)

    return _call(
        body, "mixB_bwd", (1,),
        [pl.BlockSpec((None, S, MW), lambda i: (3, 0, 0)), pl.BlockSpec((None, S, MW), lambda i: (1, 0, 0)),
         pl.BlockSpec((4, GW, GW), lambda i: (0, 0, 0)), _spspec("pool_scale", MW, _zero), _ANY],
        [pl.BlockSpec((None, S, MW), lambda i: (3, 0, 0)), pl.BlockSpec((4, GW, GW), lambda i: (0, 0, 0)),
         pl.BlockSpec((8, MW), lambda i: (0, 0))],
        [SDS((NS, S, MW), CDT), SDS((4, GW, GW), F32), SDS((8, MW), F32)],
        [pltpu.VMEM((H + S, MW), F32), pltpu.VMEM((S + H, MW), F32)], aliases={4: 0})(z, dm, wp, sp, dz)


def _sgu_mask():
    ci = lax.broadcasted_iota(jnp.int32, (GW, GW), 0) // CHUNK
    cj = lax.broadcasted_iota(jnp.int32, (GW, GW), 1) // CHUNK
    return cj <= ci


def _mixC_fwd(z, ws, bsc, sp):
    S = z.shape[1]
    RB = min(512, S)

    def body(z_ref, lg_ref, lb_ref, ws_ref, bs_ref, o_ref):
        mask = _sgu_mask()
        gu = _gelu(z_ref[0].astype(F32))
        xh, _ = _ln(_gelu(z_ref[1].astype(F32)))
        vn = (xh * lg_ref[...] + lb_ref[...]).astype(CDT)
        for g in range(4):
            gs = slice(g * GW, (g + 1) * GW)
            wm = jnp.where(mask, ws_ref[g], 0.0).astype(CDT)
            for nb in range(RB // GW):
                rs = slice(nb * GW, (nb + 1) * GW)
                mixed = _nn(wm, vn[rs, gs]) + bs_ref[g]
                o_ref[rs, gs] = (gu[rs, gs] * mixed).astype(CDT)

    return _call(
        body, "mixC_fwd", (S // RB,),
        [pl.BlockSpec((2, RB, MW), lambda i: (2, i, 0)), _spspec("sgu_ln_g", MW, _zero), _spspec("sgu_ln_b", MW, _zero),
         pl.BlockSpec((4, GW, GW), lambda i: (0, 0, 0)), pl.BlockSpec((4, GW, 1), lambda i: (0, 0, 0))],
        pl.BlockSpec((RB, MW), lambda i: (i, 0)), SDS((S, MW), CDT))(z, sp, sp, ws, bsc)


def _mixC_bwd(z, dm, ws, bsc, sp, dz):
    S = z.shape[1]
    RB = min(512, S)
    NR = S // RB

    def body(z_ref, dm_ref, lg_ref, lb_ref, ws_ref, bs_ref, dzin_ref, dz_ref, dws_ref, dbs_ref, gp_ref, dvn_s):
        i = pl.program_id(0)

        @pl.when(i == 0)
        def _():
            dws_ref[...] = jnp.zeros_like(dws_ref)
            dbs_ref[...] = jnp.zeros_like(dbs_ref)
            gp_ref[...] = jnp.zeros_like(gp_ref)

        mask = _sgu_mask()
        gu, dgu = _gelu_parts(z_ref[0].astype(F32))
        gv, dgv = _gelu_parts(z_ref[1].astype(F32))
        xh, rs_ = _ln(gv)
        vn = (xh * lg_ref[...] + lb_ref[...]).astype(CDT)
        dmf = dm_ref[...].astype(F32)
        for g in range(4):
            gs = slice(g * GW, (g + 1) * GW)
            wm = jnp.where(mask, ws_ref[g], 0.0).astype(CDT)
            for nb in range(RB // GW):
                rs = slice(nb * GW, (nb + 1) * GW)
                vb = vn[rs, gs]
                mixed = _nn(wm, vb) + bs_ref[g]
                dz_ref[0, rs, gs] = (dmf[rs, gs] * mixed * dgu[rs, gs]).astype(CDT)
                dmx = dmf[rs, gs] * gu[rs, gs]
                dbs_ref[g] += dmx
                dmxc = dmx.astype(CDT)
                dws_ref[g] += _nt(dmxc, vb)
                dvn_s[rs, gs] = _tn(wm, dmxc)
        dvn = dvn_s[...]
        gp_ref[0:1, :] += jnp.sum(dvn * xh, axis=0, keepdims=True)
        gp_ref[1:2, :] += jnp.sum(dvn, axis=0, keepdims=True)
        dz_ref[1] = (_ln_bwd(xh, rs_, dvn * lg_ref[...]) * dgv).astype(CDT)

        @pl.when(i == NR - 1)
        def _():
            for g in range(4):
                dws_ref[g] = jnp.where(mask, dws_ref[g], 0.0)
                dbs_ref[g] = jnp.broadcast_to(jnp.sum(dbs_ref[g], axis=1, keepdims=True), (GW, GW))

    full3 = lambda i: (0, 0, 0)
    return _call(
        body, "mixC_bwd", (NR,),
        [pl.BlockSpec((2, RB, MW), lambda i: (2, i, 0)), pl.BlockSpec((None, RB, MW), lambda i: (2, i, 0)),
         _spspec("sgu_ln_g", MW, _zero), _spspec("sgu_ln_b", MW, _zero),
         pl.BlockSpec((4, GW, GW), full3), pl.BlockSpec((4, GW, 1), full3), _ANY],
        [pl.BlockSpec((2, RB, MW), lambda i: (2, i, 0)), pl.BlockSpec((4, GW, GW), full3), pl.BlockSpec((4, GW, GW), full3),
         pl.BlockSpec((8, MW), lambda i: (0, 0))],
        [SDS((NS, S, MW), CDT), SDS((4, GW, GW), F32), SDS((4, GW, GW), F32), SDS((8, MW), F32)],
        [pltpu.VMEM((RB, MW), F32)], aliases={6: 0})(z, dm, sp, sp, ws, bsc, dz)


def _unpack_wb(wb_ref, wbf):
    for j in range(NS):
        for k in range(4):
            wbf[k, :, j * GW:(j + 1) * GW] = wb_ref[j, k]


def _merge_fwd(ma, mb, mc, yd, g, wb, pwo, x, sp):
    S, D = x.shape
    TM = min(512, S)

    def body(ma_ref, mb_ref, mc_ref, yd_ref, g_ref, wb_ref, wo_ref, x_ref, lg_ref, lb_ref, qg_ref,
             xo_ref, md_ref, yk_ref, mg_ref, mo_ref, wbf):
        @pl.when(pl.program_id(0) == 0)
        def _():
            _unpack_wb(wb_ref, wbf)

        xh, _ = _ln(yd_ref[...].astype(F32))
        sl, _ = _silu_parts(xh * lg_ref[...] + lb_ref[...])
        md = sl.astype(CDT)
        md_ref[...] = md
        merged = jnp.zeros((TM, D), F32)
        for k, m in enumerate((ma_ref[...], mb_ref[...], mc_ref[...], md)):
            yk = _nn(m, wbf[k])
            yk_ref[k] = yk.astype(CDT)
            merged = merged + g_ref[k].astype(F32) * yk
        mgc = merged.astype(CDT)
        mg_ref[...] = mgc
        mo = _nn(mgc, wo_ref[...].reshape(D, D))
        mo_ref[...] = mo.astype(CDT)
        n, _ = _rms(mo)
        xo_ref[...] = x_ref[...] + n * qg_ref[...]

    row = lambda i: (i, 0)
    rowm = pl.BlockSpec((TM, MW), row)
    rowd = pl.BlockSpec((TM, D), row)
    row4 = pl.BlockSpec((4, TM, D), lambda i: (0, i, 0))
    return _call(
        body, "merge_fwd", (S // TM,),
        [rowm, rowm, rowm, rowm, row4, pl.BlockSpec((NS, 4, MW, GW), lambda i: (0, 0, 0, 0), pipeline_mode=pl.Buffered(1)),
         pl.BlockSpec((NS, GW, D), lambda i: (0, 0, 0), pipeline_mode=pl.Buffered(1)), rowd,
         _spspec("conv_d_ln_g", MW, _zero), _spspec("conv_d_ln_b", MW, _zero), _spspec("mix_post_g", D, _zero)],
        [rowd, rowm, row4, rowd, rowd],
        [SDS((S, D), F32), SDS((S, MW), CDT), SDS((4, S, D), CDT), SDS((S, D), CDT), SDS((S, D), CDT)],
        [pltpu.VMEM((4, MW, D), CDT)])(ma, mb, mc, yd, g, wb, pwo, x, sp, sp, sp)


def _merge_bwd_act(dxo, mo, g, yk, wb, pwo, sp, dep):
    S, D = dxo.shape
    TM = min(256, S)

    def body(dxo_ref, mo_ref, g_ref, yk_ref, wb_ref, wo_ref, qg_ref, dep_ref, dmo_ref, dm_ref, dgp_ref, dyk_ref, gp_ref, wbf):
        @pl.when(pl.program_id(0) == 0)
        def _():
            gp_ref[...] = jnp.zeros_like(gp_ref)
            _unpack_wb(wb_ref, wbf)

        n, r = _rms(mo_ref[...].astype(F32))
        dmo, dg = _rms_bwd(n, r, qg_ref[...], dxo_ref[...])
        gp_ref[0:1, :] += dg
        dmoc = dmo.astype(CDT)
        dmo_ref[...] = dmoc
        dmg = _nt(dmoc, wo_ref[...].reshape(D, D))
        for k in range(4):
            gk = g_ref[k].astype(F32)
            dyk = (dmg * gk).astype(CDT)
            dyk_ref[k] = dyk
            dgp_ref[k] = (dmg * yk_ref[k].astype(F32) * gk * (1.0 - gk)).astype(CDT)
            dm_ref[k] = _nt(dyk, wbf[k]).astype(CDT)

    rowd = pl.BlockSpec((TM, D), lambda i: (i, 0))
    row4 = pl.BlockSpec((4, TM, D), lambda i: (0, i, 0))
    return _call(
        body, "merge_bwd_act", (S // TM,),
        [rowd, rowd, row4, row4, pl.BlockSpec((NS, 4, MW, GW), lambda i: (0, 0, 0, 0), pipeline_mode=pl.Buffered(1)),
         pl.BlockSpec((NS, GW, D), lambda i: (0, 0, 0), pipeline_mode=pl.Buffered(1)), _spspec("mix_post_g", D, _zero), _ANY],
        [rowd, pl.BlockSpec((4, TM, MW), lambda i: (0, i, 0)), row4, row4, pl.BlockSpec((8, D), lambda i: (0, 0))],
        [SDS((S, D), CDT), SDS((4, S, MW), CDT), SDS((4, S, D), CDT), SDS((4, S, D), CDT), SDS((8, D), F32)],
        [pltpu.VMEM((4, MW, D), CDT)])(dxo, mo, g, yk, wb, pwo, sp, dep)


def _merge_bwd_w(ma, mb, mc, md, dyk, mg, dmo):
    S, D = dmo.shape
    TK = min(512, S)
    NK = S // TK

    def body(ma_ref, mb_ref, mc_ref, md_ref, dyk_ref, mg_ref, dmo_ref, gwb_ref, gwo_ref, accb, acco):
        k = pl.program_id(0)

        @pl.when(k == 0)
        def _():
            accb[...] = jnp.zeros_like(accb)
            acco[...] = jnp.zeros_like(acco)

        for b, m in enumerate((ma_ref, mb_ref, mc_ref, md_ref)):
            accb[b] += _tn(m[...], dyk_ref[b])
        acco[...] += _tn(mg_ref[...], dmo_ref[...])

        @pl.when(k == NK - 1)
        def _():
            for j in range(NS):
                for b in range(4):
                    gwb_ref[j, b] = accb[b, :, j * GW:(j + 1) * GW].astype(CDT)
                gwo_ref[j] = acco[j * GW:(j + 1) * GW, :].astype(CDT)

    rowm = pl.BlockSpec((TK, MW), lambda k: (k, 0))
    rowd = pl.BlockSpec((TK, D), lambda k: (k, 0))
    return _call(
        body, "merge_bwd_w", (NK,),
        [rowm, rowm, rowm, rowm, pl.BlockSpec((4, TK, D), lambda k: (0, k, 0)), rowd, rowd],
        [pl.BlockSpec((NS, 4, MW, GW), lambda k: (0, 0, 0, 0)), pl.BlockSpec((NS, GW, D), lambda k: (0, 0, 0))],
        [SDS((NS, 4, MW, GW), CDT), SDS((NS, GW, D), CDT)],
        [pltpu.VMEM((4, MW, D), F32), pltpu.VMEM((D, D), F32)])(ma, mb, mc, md, dyk, mg, dmo)


def _xa_kv(mem, sp, pxa):
    M, D = mem.shape

    def body(m_ref, g_ref, wk_ref, wv_ref, mn_ref, k_ref, v_ref):
        n, _ = _rms(m_ref[...])
        mn = (n * g_ref[...]).astype(CDT)
        mn_ref[...] = mn
        k_ref[...] = _nn(mn, wk_ref[...].reshape(D, D)).astype(CDT)
        v_ref[...] = _nn(mn, wv_ref[...].reshape(D, D)).astype(CDT)

    full = pl.BlockSpec((M, D), lambda i: (0, 0))
    return _call(
        body, "xa_kv", (1,),
        [full, _spspec("mem_g", D, _zero), pl.BlockSpec((NS, GW, D), lambda i: (0, 1, 0)),
         pl.BlockSpec((NS, GW, D), lambda i: (0, 2, 0))],
        [full, full, full], [SDS((M, D), CDT)] * 3)(mem, sp, pxa, pxa)


def _softmax(s):
    e = jnp.exp(s - jnp.max(s, axis=-1, keepdims=True))
    return e / jnp.sum(e, axis=-1, keepdims=True)


def _xa_fwd(x, kk, vv, sp, pxa):
    S, D = x.shape
    M = kk.shape[0]
    TM = min(512, S)
    HD = D // XA_HEADS
    scale = HD ** -0.5

    def body(x_ref, k_ref, v_ref, pg_ref, qg_ref, wq_ref, wo_ref, xo_ref, hb_ref, q_ref, o_ref, po_ref):
        n, _ = _rms(x_ref[...])
        hb = (n * pg_ref[...]).astype(CDT)
        hb_ref[...] = hb
        q = _nn(hb, wq_ref[...].reshape(D, D)).astype(CDT)
        q_ref[...] = q
        for h in range(XA_HEADS):
            hs = slice(h * HD, (h + 1) * HD)
            p = _softmax(_nt(q[:, hs], k_ref[:, hs]) * scale)
            o_ref[:, hs] = _nn(p.astype(CDT), v_ref[:, hs]).astype(CDT)
        po = _nn(o_ref[...], wo_ref[...].reshape(D, D))
        po_ref[...] = po.astype(CDT)
        n, _ = _rms(po)
        xo_ref[...] = x_ref[...] + n * qg_ref[...]

    row = pl.BlockSpec((TM, D), lambda i: (i, 0))
    full = pl.BlockSpec((M, D), lambda i: (0, 0))
    return _call(
        body, "xa_fwd", (S // TM,),
        [row, full, full, _spspec("xa_pre_g", D, _zero), _spspec("xa_post_g", D, _zero),
         pl.BlockSpec((NS, GW, D), lambda i: (0, 0, 0)), pl.BlockSpec((NS, GW, D), lambda i: (0, 3, 0))],
        [row] * 5, [SDS((S, D), F32)] + [SDS((S, D), CDT)] * 4)(x, kk, vv, sp, sp, pxa, pxa)


def _xa_bwd_act(dxo, x, po, q, kk, vv, sp, pxa, dep):
    S, D = x.shape
    M = kk.shape[0]
    TM = min(512, S)
    HD = D // XA_HEADS
    scale = HD ** -0.5

    def body(dxo_ref, x_ref, po_ref, q_ref, k_ref, v_ref, pg_ref, qg_ref, wq_ref, wo_ref, dep_ref,
             dx_ref, dpo_ref, dq_ref, dk_ref, dv_ref, gp_ref):
        @pl.when(pl.program_id(0) == 0)
        def _():
            gp_ref[...] = jnp.zeros_like(gp_ref)
            dk_ref[...] = jnp.zeros_like(dk_ref)
            dv_ref[...] = jnp.zeros_like(dv_ref)

        n, r = _rms(po_ref[...].astype(F32))
        dpo, dg = _rms_bwd(n, r, qg_ref[...], dxo_ref[...])
        gp_ref[1:2, :] += dg
        dpoc = dpo.astype(CDT)
        dpo_ref[...] = dpoc
        do = _nt(dpoc, wo_ref[...].reshape(D, D)).astype(CDT)
        for h in range(XA_HEADS):
            hs = slice(h * HD, (h + 1) * HD)
            qh = q_ref[:, hs]
            p = _softmax(_nt(qh, k_ref[:, hs]) * scale)
            pc = p.astype(CDT)
            dv_ref[:, hs] += _tn(pc, do[:, hs])
            dp = _nt(do[:, hs], v_ref[:, hs])
            ds = (p * (dp - jnp.sum(p * dp, axis=-1, keepdims=True)) * scale).astype(CDT)
            dq_ref[:, hs] = _nn(ds, k_ref[:, hs]).astype(CDT)
            dk_ref[:, hs] += _tn(ds, qh)
        dhb = _nt(dq_ref[...], wq_ref[...].reshape(D, D))
        n, r = _rms(x_ref[...])
        dx, dg = _rms_bwd(n, r, pg_ref[...], dhb)
        dx_ref[...] = dxo_ref[...] + dx
        gp_ref[0:1, :] += dg

    row = pl.BlockSpec((TM, D), lambda i: (i, 0))
    full = pl.BlockSpec((M, D), lambda i: (0, 0))
    return _call(
        body, "xa_bwd_act", (S // TM,),
        [row, row, row, row, full, full, _spspec("xa_pre_g", D, _zero), _spspec("xa_post_g", D, _zero),
         pl.BlockSpec((NS, GW, D), lambda i: (0, 0, 0)), pl.BlockSpec((NS, GW, D), lambda i: (0, 3, 0)), _ANY],
        [row, row, row, full, full, pl.BlockSpec((8, D), lambda i: (0, 0))],
        [SDS((S, D), F32), SDS((S, D), CDT), SDS((S, D), CDT), SDS((M, D), F32), SDS((M, D), F32), SDS((8, D), F32)],
    )(dxo, x, po, q, kk, vv, sp, sp, pxa, pxa, dep)


def _xa_bwd_w(hb, dq, o, dpo, mn, dk, dv):
    S, D = hb.shape
    M = mn.shape[0]
    TK = min(1024, S)
    NK = S // TK

    def body(hb_ref, dq_ref, o_ref, dpo_ref, mn_ref, dk_ref, dv_ref, g_ref, accq, acco):
        k = pl.program_id(0)

        @pl.when(k == 0)
        def _():
            accq[...] = jnp.zeros_like(accq)
            acco[...] = jnp.zeros_like(acco)

        accq[...] += _tn(hb_ref[...], dq_ref[...])
        acco[...] += _tn(o_ref[...], dpo_ref[...])

        @pl.when(k == NK - 1)
        def _():
            gk = _tn(mn_ref[...], dk_ref[...].astype(CDT))
            gv = _tn(mn_ref[...], dv_ref[...].astype(CDT))
            for j in range(NS):
                rs = slice(j * GW, (j + 1) * GW)
                g_ref[j, 0:GW, :] = accq[rs, :].astype(CDT)
                g_ref[j, GW:2 * GW, :] = gk[rs, :].astype(CDT)
                g_ref[j, 2 * GW:3 * GW, :] = gv[rs, :].astype(CDT)
                g_ref[j, 3 * GW:4 * GW, :] = acco[rs, :].astype(CDT)

    rowb = pl.BlockSpec((TK, D), lambda k: (k, 0))
    full = pl.BlockSpec((M, D), lambda k: (0, 0))
    return _call(
        body, "xa_bwd_w", (NK,),
        [rowb, rowb, rowb, rowb, full, full, full],
        pl.BlockSpec((NS, 4 * GW, D), lambda k: (0, 0, 0)), SDS((NS, 4 * GW, D), CDT),
        [pltpu.VMEM((D, D), F32), pltpu.VMEM((D, D), F32)])(hb, dq, o, dpo, mn, dk, dv)


def _xa_kv_bwd(mem, dk, dv, sp, pxa):
    M, D = mem.shape

    def body(m_ref, dk_ref, dv_ref, wk_ref, wv_ref, gp_ref):
        dmn = _nt(dk_ref[...].astype(CDT), wk_ref[...].reshape(D, D)) + _nt(dv_ref[...].astype(CDT), wv_ref[...].reshape(D, D))
        n, _ = _rms(m_ref[...])
        gp_ref[...] = jnp.zeros_like(gp_ref)
        gp_ref[0:1, :] = jnp.sum(dmn * n, axis=0, keepdims=True)

    full = pl.BlockSpec((M, D), lambda i: (0, 0))
    return _call(
        body, "xa_kv_bwd", (1,),
        [full, full, full, pl.BlockSpec((NS, GW, D), lambda i: (0, 1, 0)), pl.BlockSpec((NS, GW, D), lambda i: (0, 2, 0))],
        pl.BlockSpec((8, D), lambda i: (0, 0)), SDS((8, D), F32))(mem, dk, dv, pxa, pxa)


def _loss_head(y, t):
    S, D = y.shape
    TM = min(512, S)

    def body(y_ref, t_ref, dy_ref, l_ref):
        @pl.when(pl.program_id(0) == 0)
        def _():
            l_ref[...] = jnp.zeros_like(l_ref)

        e = y_ref[...] - t_ref[...]
        dy_ref[...] = e * (1.0 / D)
        l_ref[...] += 0.5 * jnp.sum(jnp.mean(e * e, axis=-1, keepdims=True), axis=0, keepdims=True)

    row = pl.BlockSpec((TM, D), lambda i: (i, 0))
    return _call(body, "loss_head", (S // TM,), [row, row], [row, pl.BlockSpec((8, 128), lambda i: (0, 0))],
                 [SDS((S, D), F32), SDS((8, 128), F32)])(y, t)


def _row_tile(rows, cols, limit=1 << 18, step=8):
    if rows * cols <= limit or rows % step:
        return rows
    best = step
    for t in range(step, rows + 1, step):
        if rows % t == 0 and t * cols <= limit:
            best = t
    return best


def _adamw(w, g, m, v):
    shape = w.shape
    C = shape[-1]
    R = w.size // C
    TR = _row_tile(R, C)
    c1 = 1.0 - ADAM_B1 ** ADAM_STEP
    c2 = 1.0 - ADAM_B2 ** ADAM_STEP

    def body(w_ref, g_ref, m_ref, v_ref, d_ref, nm_ref, nv_ref):
        gg = g_ref[...]
        nm = ADAM_B1 * m_ref[...] + (1.0 - ADAM_B1) * gg
        nv = ADAM_B2 * v_ref[...] + (1.0 - ADAM_B2) * (gg * gg)
        nm_ref[...] = nm
        nv_ref[...] = nv
        d_ref[...] = -ADAM_LR * ((nm / c1) / (jnp.sqrt(nv / c2) + ADAM_EPS) + ADAM_WD * w_ref[...])

    blk = pl.BlockSpec((TR, C), lambda i: (i, 0))
    outs = _call(body, "adamw", (R // TR,), [blk] * 4, [blk] * 3, [SDS((R, C), F32)] * 3)(
        w.reshape(R, C), g.reshape(R, C), m.reshape(R, C), v.reshape(R, C))
    return tuple(o.reshape(shape) for o in outs)


def _adamw_block(w, gs, m, v, gblock):
    L, R, C = w.shape
    c1 = 1.0 - ADAM_B1 ** ADAM_STEP
    c2 = 1.0 - ADAM_B2 ** ADAM_STEP

    def body(w_ref, g_ref, m_ref, v_ref, go_ref, d_ref, nm_ref, nv_ref):
        gg = g_ref[...]
        go_ref[...] = gg
        nm = ADAM_B1 * m_ref[...] + (1.0 - ADAM_B1) * gg
        nv = ADAM_B2 * v_ref[...] + (1.0 - ADAM_B2) * (gg * gg)
        nm_ref[...] = nm
        nv_ref[...] = nv
        d_ref[...] = -ADAM_LR * ((nm / c1) / (jnp.sqrt(nv / c2) + ADAM_EPS) + ADAM_WD * w_ref[...])

    blk = pl.BlockSpec((None, R, C), lambda l: (l, 0, 0))
    return _call(body, "adamw_block", (L,), [blk, pl.BlockSpec((None, R, C), lambda l: (l, gblock, 0)), blk, blk],
                 [blk] * 4, [SDS((L, R, C), F32)] * 4)(w, gs, m, v)


def _slot_sum_into(stacked, r, l):
    _, R, C = r.shape
    TR = _row_tile(R, C * NS, limit=1 << 21, step=16)

    def body(r_ref, s_ref, o_ref):
        acc = r_ref[0].astype(F32)
        for j in range(1, NS):
            acc = acc + r_ref[j].astype(F32)
        o_ref[...] = acc

    return pl.pallas_call(
        body, name="slot_sum_into", grid=(R // TR,),
        in_specs=[pl.BlockSpec((NS, TR, C), lambda i: (0, i, 0)), _ANY],
        out_specs=pl.BlockSpec((None, TR, C), lambda i: (l, i, 0)), out_shape=SDS(stacked.shape, F32),
        input_output_aliases={1: 0},
        compiler_params=pltpu.CompilerParams(dimension_semantics=("arbitrary",), vmem_limit_bytes=VMEM_LIMIT))(r, stacked)


def _exchange(arrs, scatter, name):
    n = len(arrs)
    np_ = NS - 1

    def body(*refs):
        ins, outs = refs[:n], refs[n:2 * n]
        send_sems, recv_sems, loc_sems = refs[2 * n:]
        x, y, c = lax.axis_index("x"), lax.axis_index("y"), lax.axis_index("c")
        me = 4 * x + 2 * y + c
        peers = []
        for f in range(1, NS):
            px = 1 - x if f & 4 else x
            py = 1 - y if f & 2 else y
            pc = 1 - c if f & 1 else c
            peers.append(((px, py, pc), 4 * px + 2 * py + pc))

        def src(a, pid):
            return ins[a].at[pid] if scatter else ins[a]

        local = [pltpu.make_async_copy(src(a, me), outs[a].at[me], loc_sems.at[a]) for a in range(n)]
        for cp in local:
            cp.start()
        sends = []
        for a in range(n):
            for f, (dev, pid) in enumerate(peers):
                sends.append(pltpu.make_async_remote_copy(
                    src_ref=src(a, pid), dst_ref=outs[a].at[me], send_sem=send_sems.at[a * np_ + f],
                    recv_sem=recv_sems.at[a * np_ + f], device_id=dev, device_id_type=pl.DeviceIdType.MESH))
        for cp in sends:
            cp.start()
        for a in range(n):
            for f, (dev, pid) in enumerate(peers):
                pltpu.make_async_remote_copy(
                    src_ref=src(a, pid), dst_ref=outs[a].at[pid], send_sem=send_sems.at[a * np_ + f],
                    recv_sem=recv_sems.at[a * np_ + f], device_id=dev, device_id_type=pl.DeviceIdType.MESH).wait_recv()
        for cp in sends:
            cp.wait_send()
        for cp in local:
            cp.wait()

    out_shape = [SDS(a.shape if scatter else (NS,) + a.shape, a.dtype) for a in arrs]
    anyspec = pl.BlockSpec(memory_space=pl.ANY)
    outs = pl.pallas_call(
        body, name=name, in_specs=[anyspec] * n, out_specs=[anyspec] * n, out_shape=out_shape,
        scratch_shapes=[pltpu.SemaphoreType.DMA((n * np_,)), pltpu.SemaphoreType.DMA((n * np_,)),
                        pltpu.SemaphoreType.DMA((n,))],
        compiler_params=pltpu.CompilerParams(has_side_effects=True))(*arrs)
    return list(outs)


def _peers():
    x, y, c = lax.axis_index("x"), lax.axis_index("y"), lax.axis_index("c")
    out = []
    for f in range(1, NS):
        px = 1 - x if f & 4 else x
        py = 1 - y if f & 2 else y
        pc = 1 - c if f & 1 else c
        out.append(((px, py, pc), 4 * px + 2 * py + pc))
    return 4 * x + 2 * y + c, out


def _exchange_copies(ins, lands, scatter, send_sems, recv_sems, loc_sems):
    me, peers = _peers()
    np_ = NS - 1

    def src(a, pid):
        return ins[a].at[pid] if scatter[a] else ins[a]

    def rcopy(a, f, dev, land_slot):
        return pltpu.make_async_remote_copy(
            src_ref=src(a, peers[f][1]), dst_ref=lands[a].at[land_slot], send_sem=send_sems.at[a * np_ + f],
            recv_sem=recv_sems.at[a * np_ + f], device_id=dev, device_id_type=pl.DeviceIdType.MESH)

    local = [pltpu.make_async_copy(src(a, me), lands[a].at[me], loc_sems.at[a]) for a in range(len(ins))]
    sends = [rcopy(a, f, dev, me) for a in range(len(ins)) for f, (dev, _) in enumerate(peers)]
    arrivals = [rcopy(a, f, dev, pid) for a in range(len(ins)) for f, (dev, pid) in enumerate(peers)]
    return local, sends, arrivals


_HBM = pl.BlockSpec(memory_space=pltpu.HBM)
_SEM = pl.BlockSpec(memory_space=pltpu.SEMAPHORE)
_ANY = pl.BlockSpec(memory_space=pl.ANY)


def _exchange_start(arrs, scatter, after, name):
    n = len(arrs)
    np_ = NS - 1
    lands = [lax.empty(a.shape if sc else (NS,) + a.shape, a.dtype) for a, sc in zip(arrs, scatter)]

    def body(*refs):
        ins, lnd = refs[:n], refs[n:2 * n]
        send_sems, recv_sems, loc_sems = refs[2 * n + 1:2 * n + 4]
        token = refs[-1]
        local, sends, _ = _exchange_copies(ins, lnd, scatter, send_sems, recv_sems, loc_sems)
        for cp in local + sends:
            cp.start()
        token[...] = jnp.zeros_like(token)

    hbm = lambda a: pltpu.HBM(a.shape, a.dtype)
    outs = pl.pallas_call(
        body, name=name,
        out_shape=(pltpu.SemaphoreType.DMA((n * np_,)), pltpu.SemaphoreType.DMA((n * np_,)), pltpu.SemaphoreType.DMA((n,)),
                   *[hbm(a) for a in arrs], *[hbm(a) for a in lands], SDS((8, 128), F32)),
        in_specs=[_HBM] * (2 * n) + [_ANY],
        out_specs=(_SEM, _SEM, _SEM, *([_HBM] * (2 * n)), pl.BlockSpec(memory_space=pltpu.VMEM)),
        input_output_aliases={i: 3 + i for i in range(2 * n)},
        compiler_params=pltpu.CompilerParams(has_side_effects=pltpu.SideEffectType.DATAFLOW_SIDE_EFFECTING),
    )(*[pltpu.with_memory_space_constraint(a, pltpu.HBM) for a in list(arrs) + lands], after)
    return {"sems": outs[:3], "ins": outs[3:3 + n], "lands": outs[3 + n:3 + 2 * n], "token": outs[-1], "scatter": scatter}


def _exchange_wait(h, after, name):
    n = len(h["ins"])
    scatter = h["scatter"]
    after = list(after) if isinstance(after, (list, tuple)) else [after]

    def body(*refs):
        ins, lnd = refs[:n], refs[n:2 * n]
        send_sems, recv_sems, loc_sems = refs[2 * n:2 * n + 3]
        local, sends, arrivals = _exchange_copies(ins, lnd, scatter, send_sems, recv_sems, loc_sems)
        for cp in sends:
            cp.wait_send()
        for cp in arrivals:
            cp.wait_recv()
        for cp in local:
            cp.wait()

    hbm = lambda a: pltpu.HBM(a.shape, a.dtype)
    outs = pl.pallas_call(
        body, name=name,
        out_shape=tuple(hbm(a) for a in list(h["ins"]) + list(h["lands"])),
        in_specs=[_HBM] * (2 * n) + [_SEM] * 3 + [_ANY] * len(after),
        out_specs=tuple([_HBM] * (2 * n)),
        input_output_aliases={i: i for i in range(2 * n)},
        compiler_params=pltpu.CompilerParams(has_side_effects=pltpu.SideEffectType.DATAFLOW_SIDE_EFFECTING),
    )(*h["ins"], *h["lands"], *h["sems"], *after)
    return list(outs[n:])


def _hgather_copies(ins, lands, send_a, recv_a, send_b, recv_b, loc_sems):
    x, y, c = lax.axis_index("x"), lax.axis_index("y"), lax.axis_index("c")
    me = 4 * x + 2 * y + c
    sib = (x, y, 1 - c)
    chips = [(1 - x, y), (x, 1 - y), (1 - x, 1 - y)]
    slot = lambda px, py, pc: 4 * px + 2 * py + pc

    def rcopy(src, dst, ssem, rsem, dev):
        return pltpu.make_async_remote_copy(src_ref=src, dst_ref=dst, send_sem=ssem, recv_sem=rsem, device_id=dev,
                                            device_id_type=pl.DeviceIdType.MESH)

    local, s1, a1, s2, a2 = [], [], [], [], []
    for a in range(len(ins)):
        first = [(sib, slot(x, y, 1 - c))] + [((px, py, c), slot(px, py, c)) for px, py in chips]
        for k, (dev, origin) in enumerate(first if send_a is not None else ()):
            s1.append(rcopy(ins[a], lands[a].at[me], send_a.at[4 * a + k], recv_a.at[4 * a + k], dev))
            a1.append(rcopy(ins[a], lands[a].at[origin], send_a.at[4 * a + k], recv_a.at[4 * a + k], dev))
        if send_a is not None:
            local.append(pltpu.make_async_copy(ins[a], lands[a].at[me], loc_sems.at[a]))
        for k, (px, py) in enumerate(chips if send_b is not None else ()):
            mine, theirs = lands[a].at[slot(px, py, c)], lands[a].at[slot(px, py, 1 - c)]
            s2.append(rcopy(mine, mine, send_b.at[3 * a + k], recv_b.at[3 * a + k], sib))
            a2.append(rcopy(mine, theirs, send_b.at[3 * a + k], recv_b.at[3 * a + k], sib))
    return local, s1, a1, s2, a2


def _hgather_start(arrs, after, name):
    n = len(arrs)
    lands = [lax.empty((NS,) + a.shape, a.dtype) for a in arrs]

    def body(*refs):
        ins, lnd = refs[:n], refs[n:2 * n]
        send_a, recv_a, loc_sems = refs[2 * n + 1:2 * n + 4]
        token = refs[-1]
        local, s1, _, _, _ = _hgather_copies(ins, lnd, send_a, recv_a, None, None, loc_sems)
        for cp in local + s1:
            cp.start()
        token[...] = jnp.zeros_like(token)

    hbm = lambda a: pltpu.HBM(a.shape, a.dtype)
    outs = pl.pallas_call(
        body, name=name,
        out_shape=(pltpu.SemaphoreType.DMA((4 * n,)), pltpu.SemaphoreType.DMA((4 * n,)), pltpu.SemaphoreType.DMA((n,)),
                   *[hbm(a) for a in arrs], *[hbm(a) for a in lands], SDS((8, 128), F32)),
        in_specs=[_HBM] * (2 * n) + [_ANY],
        out_specs=(_SEM, _SEM, _SEM, *([_HBM] * (2 * n)), pl.BlockSpec(memory_space=pltpu.VMEM)),
        input_output_aliases={i: 3 + i for i in range(2 * n)},
        compiler_params=pltpu.CompilerParams(has_side_effects=pltpu.SideEffectType.DATAFLOW_SIDE_EFFECTING),
    )(*[pltpu.with_memory_space_constraint(a, pltpu.HBM) for a in list(arrs) + lands], after)
    return {"sems": outs[:3], "ins": outs[3:3 + n], "lands": outs[3 + n:3 + 2 * n], "token": outs[-1]}


def _hgather_forward(h, after, name):
    n = len(h["ins"])
    after = list(after) if isinstance(after, (list, tuple)) else [after]
    na = len(after)

    def body(*refs):
        ins, lnd = refs[:n], refs[n:2 * n]
        send_a, recv_a, loc_sems = refs[2 * n:2 * n + 3]
        send_b, recv_b = refs[2 * n + 3 + na:2 * n + 5 + na]
        token = refs[-1]
        local, s1, a1, s2, _ = _hgather_copies(ins, lnd, send_a, recv_a, send_b, recv_b, loc_sems)
        for cp in s1:
            cp.wait_send()
        for cp in a1:
            cp.wait_recv()
        for cp in local:
            cp.wait()
        for cp in s2:
            cp.start()
        token[...] = jnp.zeros_like(token)

    hbm = lambda a: pltpu.HBM(a.shape, a.dtype)
    outs = pl.pallas_call(
        body, name=name,
        out_shape=(pltpu.SemaphoreType.DMA((3 * n,)), pltpu.SemaphoreType.DMA((3 * n,)),
                   *[hbm(a) for a in list(h["ins"]) + list(h["lands"])], SDS((8, 128), F32)),
        in_specs=[_HBM] * (2 * n) + [_SEM] * 3 + [_ANY] * na,
        out_specs=(_SEM, _SEM, *([_HBM] * (2 * n)), pl.BlockSpec(memory_space=pltpu.VMEM)),
        input_output_aliases={i: 2 + i for i in range(2 * n)},
        compiler_params=pltpu.CompilerParams(has_side_effects=pltpu.SideEffectType.DATAFLOW_SIDE_EFFECTING),
    )(*h["ins"], *h["lands"], *h["sems"], *after)
    return {"sems": outs[:2], "ins": outs[2:2 + n], "lands": outs[2 + n:2 + 2 * n], "token": outs[-1]}


def _hgather_wait(h, after, name):
    n = len(h["ins"])

    def body(*refs):
        ins, lnd = refs[:n], refs[n:2 * n]
        send_b, recv_b = refs[2 * n:2 * n + 2]
        _, _, _, s2, a2 = _hgather_copies(ins, lnd, None, None, send_b, recv_b, None)
        for cp in s2:
            cp.wait_send()
        for cp in a2:
            cp.wait_recv()

    hbm = lambda a: pltpu.HBM(a.shape, a.dtype)
    outs = pl.pallas_call(
        body, name=name,
        out_shape=tuple(hbm(a) for a in list(h["ins"]) + list(h["lands"])),
        in_specs=[_HBM] * (2 * n) + [_SEM] * 2 + [_ANY],
        out_specs=tuple([_HBM] * (2 * n)),
        input_output_aliases={i: i for i in range(2 * n)},
        compiler_params=pltpu.CompilerParams(has_side_effects=pltpu.SideEffectType.DATAFLOW_SIDE_EFFECTING),
    )(*h["ins"], *h["lands"], *h["sems"], after)
    return list(outs[n:])


_W_NAMES = ("ffn1_pre_g", "ffn1_post_g", "ffn1_w1", "ffn1_w3", "ffn1_w2", "mix_pre_g", "mix_post_g", "w_in", "conv_a_w",
            "conv_a_b", "pool_w", "pool_scale", "sgu_ln_g", "sgu_ln_b", "sgu_ws", "sgu_b", "conv_d_w", "conv_d_b",
            "conv_d_ln_g", "conv_d_ln_b", "w_branch", "w_gate", "b_gate", "w_o", "xa_pre_g", "xa_post_g", "mem_g",
            "xa_wq", "xa_wk", "xa_wv", "xa_wo", "ffn2_pre_g", "ffn2_post_g", "ffn2_w1", "ffn2_w3", "ffn2_w2")
_REP_NAMES = tuple(n for n, _ in _SP_NAMES) + ("pool_w", "sgu_ws", "sgu_b", "conv_a_w", "conv_d_w")


def _t(w):
    return jnp.swapaxes(w, -1, -2)


def _step(x, mem, loss_target, W, M, V):
    L = W["w_in"].shape[0]
    S, D = x.shape[1], x.shape[2]
    x0 = x.reshape(S, D)
    memf = mem.reshape(mem.shape[1], D)
    me = 4 * lax.axis_index("x") + 2 * lax.axis_index("y") + lax.axis_index("c")
    FS = W["ffn1_w2"].shape[1]
    KA, KD = W["conv_a_w"].shape[1], W["conv_d_w"].shape[1]
    CS = W["conv_a_w"].shape[2]

    cat = lambda l, parts: jnp.concatenate([(_t(W[n][l]) if tr else W[n][l]) for n, tr in parts], axis=0).astype(CDT)
    pf1 = [cat(l, (("ffn1_w1", 1), ("ffn1_w3", 1), ("ffn1_w2", 0))) for l in range(L)]
    pf2 = [cat(l, (("ffn2_w1", 1), ("ffn2_w3", 1), ("ffn2_w2", 0))) for l in range(L)]
    pma = [cat(l, (("w_in", 1), ("w_gate", 1))) for l in range(L)]
    pwo = [W["w_o"][l].astype(CDT) for l in range(L)]
    pxa = [cat(l, (("xa_wq", 0), ("xa_wk", 0), ("xa_wv", 0), ("xa_wo", 0))) for l in range(L)]
    wbs = [W["w_branch"][l].astype(CDT) for l in range(L)]
    cws = jnp.concatenate([W["conv_a_w"], W["conv_d_w"]], axis=1).reshape(-1, 128)
    sp_all = jnp.concatenate([W[n] for n, _ in _SP_NAMES], axis=1)
    bsc_all = W["sgu_b"][..., None]

    (cwg,) = _exchange([cws], False, "gather_conv_w")
    cwf = cwg.reshape(NS, L, KA + KD, CS).transpose(1, 2, 0, 3).reshape(L, KA + KD, NS * CS)

    def gather_start(l, after):
        return _hgather_start([pf1[l], pf2[l], pma[l], pwo[l], pxa[l], wbs[l]], after, f"gather_start_{l}")

    def gather_rest(h, after, tag):
        mid = _hgather_forward(h, after, f"gather_forward_{tag}")
        return _hgather_wait(mid, mid["token"], f"gather_wait_{tag}")

    packs = [None] * L
    first = [_hgather_start([pf1[0]], cwg, "gather_start_0a")]
    saved = []
    xc = x0
    for l in range(L):
        if l == 0:
            rest = [p[k] for p in (pf1, pf2, pma, pwo, pxa, wbs) for k in range(L) if not (p is pf1 and k == 0)]
            (gf1,) = gather_rest(first[0], [sp_all, bsc_all] + rest, "0a")
            first.append(_hgather_start([pma[0], pwo[0], wbs[0]], gf1, "gather_start_0b"))
        else:
            gf1, gf2, gma, gwo, gxa, gwb = packs[l]
        sp = sp_all[l:l + 1]
        cwa, cwd = cwf[l, :KA], cwf[l, KA:]
        wp, ws, bsc = W["pool_w"][l], W["sgu_ws"][l], bsc_all[l]
        s = {"x0": xc}
        nxt = gather_start(l + 1, gf1) if 0 < l < L - 1 else None
        xc, s["hb1"], s["a1"], s["b1"], s["y1"] = _ffn_fwd(xc, sp, "ffn1_pre_g", "ffn1_post_g", gf1,
                                                            (first[1] if l == 0 else nxt)["token"] if l == 0 or nxt else sp)
        s["x1"] = xc
        if l == 0:
            gma, gwo, gwb = gather_rest(first[1], xc, "0b")
            first.append(_hgather_start([pxa[0]], gma, "gather_start_0c"))
            first.append(_hgather_start([pf2[0]], first[2]["token"], "gather_start_0d"))
            nxt = gather_start(1, first[3]["token"]) if L > 1 else None
        s["hbm"], s["z"], s["g"] = _mix_in(xc, sp, gma, (nxt or first[3])["token"] if l == 0 else sp)
        s["ma"] = _mixA_fwd(s["z"], cwa, sp)
        s["mb"] = _mixB_fwd(s["z"], wp, sp)
        s["mc"] = _mixC_fwd(s["z"], ws, bsc, sp)
        s["yd"] = _mixD_conv_fwd(s["z"], cwd, sp)
        xc, s["md"], s["yk"], s["mg"], s["mo"] = _merge_fwd(s["ma"], s["mb"], s["mc"], s["yd"], s["g"], gwb, gwo, xc, sp)
        s["x2"] = xc
        if l == 0:
            (gxa,) = gather_rest(first[2], xc, "0c")
        s["mn"], s["k"], s["v"] = _xa_kv(memf, sp, gxa)
        xc, s["hbx"], s["q"], s["o"], s["po"] = _xa_fwd(xc, s["k"], s["v"], sp, gxa)
        s["x3"] = xc
        if l == 0:
            (gf2,) = gather_rest(first[3], xc, "0d")
            packs[0] = (gf1, gf2, gma, gwo, gxa, gwb)
        mid = _hgather_forward(nxt, xc, f"gather_forward_{l + 1}") if nxt and l > 0 else None
        xc, s["hb2"], s["a2"], s["b2"], s["y2"] = _ffn_fwd(xc, sp, "ffn2_pre_g", "ffn2_post_g", gf2,
                                                            mid["token"] if mid else sp)
        saved.append(s)
        if nxt:
            mid = mid or _hgather_forward(nxt, xc, f"gather_forward_{l + 1}")
            packs[l + 1] = _hgather_wait(mid, xc, f"gather_wait_{l + 1}")

    dx, lpart = _loss_head(xc, loss_target.reshape(S, D))
    loss = lax.psum(lpart[0, 0], ("x", "y", "c"))

    rep = {n: [None] * L for n in _REP_NAMES}
    summed = [dict() for _ in range(L)]
    pending = None
    last = []
    for l in reversed(range(L)):
        gf1, gf2, gma, gwo, gxa, gwb = packs[l]
        sp = sp_all[l:l + 1]
        cwa, cwd = cwf[l, :KA], cwf[l, KA:]
        wp, ws, bsc = W["pool_w"][l], W["sgu_ws"][l], bsc_all[l]
        s = saved[l]

        dx, dyb, da, db, gp = _ffn_bwd_act(dx, s["x3"], s["y2"], s["a2"], s["b2"], sp, "ffn2_pre_g", "ffn2_post_g", gf2,
                                           pending[1]["token"] if pending else sp)
        rep["ffn2_pre_g"][l], rep["ffn2_post_g"][l] = gp[0], gp[1]
        d_f2 = _ffn_bwd_w(s["hb2"], dyb, s["a2"], s["b2"], da, db, sp)
        if l == 0:
            last.append(_exchange_start([d_f2], (True,), dx, "scatter_start_0a"))

        dx, dpo, dq, dk, dv, gp = _xa_bwd_act(dx, s["x2"], s["po"], s["q"], s["k"], s["v"], sp, gxa,
                                              last[-1]["token"] if last else sp)
        rep["xa_pre_g"][l], rep["xa_post_g"][l] = gp[0], gp[1]
        d_xa = _xa_bwd_w(s["hbx"], dq, s["o"], dpo, s["mn"], dk, dv)
        rep["mem_g"][l] = _xa_kv_bwd(memf, dk, dv, sp, gxa)[0]
        if l == 0:
            last.append(_exchange_start([d_xa], (True,), dx, "scatter_start_0b"))

        dmo, dm, dgp, dyk, gp = _merge_bwd_act(dx, s["mo"], s["g"], s["yk"], gwb, gwo, sp, last[-1]["token"] if last else sp)
        rep["mix_post_g"][l] = gp[0]
        d_wb, d_wo = _merge_bwd_w(s["ma"], s["mb"], s["mc"], s["md"], dyk, s["mg"], dmo)
        dz, dcw, gp = _mixA_bwd(s["z"], dm, cwa, sp)
        rep["conv_a_w"][l], rep["conv_a_b"][l] = dcw, gp[0]
        dz, dwp, gp = _mixB_bwd(s["z"], dm, wp, sp, dz)
        rep["pool_w"][l], rep["pool_scale"][l] = dwp, gp[0]
        dz, dws, dbs, gp = _mixC_bwd(s["z"], dm, ws, bsc, sp, dz)
        rep["sgu_ws"][l], rep["sgu_b"][l], rep["sgu_ln_g"][l], rep["sgu_ln_b"][l] = dws, dbs[:, :, 0], gp[0], gp[1]
        dyd, gp = _mixD_ln_bwd(dm, s["yd"], sp)
        rep["conv_d_ln_g"][l], rep["conv_d_ln_b"][l] = gp[0], gp[1]
        dz, dcw, gp = _mixD_conv_bwd(s["z"], dyd, cwd, dz)
        rep["conv_d_w"][l], rep["conv_d_b"][l] = dcw, gp[0]
        dx, gp = _mix_in_bwd_act(dz, dgp, dx, s["x1"], sp, gma)
        rep["mix_pre_g"][l] = gp[0]
        d_ma, dbg = _mix_in_bwd_w(dz, dgp, s["hbm"])
        rep["b_gate"][l] = dbg[:, 0, :].reshape(-1)
        if l == 0:
            last.append(_exchange_start([d_ma, d_wo, d_wb], (True,) * 3, dx, "scatter_start_0c"))

        dx, dyb, da, db, gp = _ffn_bwd_act(dx, s["x0"], s["y1"], s["a1"], s["b1"], sp, "ffn1_pre_g", "ffn1_post_g", gf1,
                                           last[-1]["token"] if last else sp)
        rep["ffn1_pre_g"][l], rep["ffn1_post_g"][l] = gp[0], gp[1]
        flat = jnp.concatenate([rep[n][l].reshape(-1) for n in _REP_NAMES])
        flat = jnp.pad(flat, (0, -flat.size % 2048)).reshape(-1, 128).astype(CDT)
        if l == 0:
            last.append(_exchange_start([flat], (False,), dx, "scatter_start_0d"))
        d_f1 = _ffn_bwd_w(s["hb1"], dyb, s["a1"], s["b1"], da, db, last[-1]["token"] if last else sp)

        if pending:
            r = _exchange_wait(pending[1], dx, f"scatter_wait_{pending[0]}")
            summed[pending[0]] = dict(zip(("f1", "f2", "ma", "wo", "xa", "wb", "flat"), r))
        if l == 0:
            last.append(_exchange_start([d_f1], (True,), dx, "scatter_start_0e"))
        else:
            pending = (l, _exchange_start([d_f1, d_f2, d_ma, d_wo, d_xa, d_wb, flat], (True,) * 6 + (False,), dx,
                                          f"scatter_start_{l}"))

    pack_shape = {"f1": (3 * FS, D), "f2": (3 * FS, D), "ma": (2 * MW, D), "wo": (GW, D), "xa": (4 * GW, D),
                  "wb": (4 * MW, GW), "flat": tuple(flat.shape)}
    stk = {k: lax.empty((L,) + s, F32) for k, s in pack_shape.items()}

    def land(k, r, l):
        stk[k] = _slot_sum_into(stk[k], r.reshape((NS,) + pack_shape[k]), l)

    for l in range(1, L):
        for k, r in summed[l].items():
            land(k, r, l)
    (r,) = _exchange_wait(last[0], dx, "scatter_wait_0a")
    land("f2", r, 0)
    (r,) = _exchange_wait(last[1], dx, "scatter_wait_0b")
    land("xa", r, 0)

    G, deltas, new_m, new_v = {}, {}, {}, {}

    def update_block(n, k, blk, transposed):
        tr = _t if transposed else (lambda a: a)
        out = _adamw_block(tr(W[n]), stk[k], tr(M[n]), tr(V[n]), blk)
        G[n], deltas[n], new_m[n], new_v[n] = (tr(a) for a in out)
        return deltas[n]

    def update(n):
        deltas[n], new_m[n], new_v[n] = _adamw(W[n], G[n], M[n], V[n])
        return deltas[n]

    done = [update_block("ffn2_w1", "f2", 0, True), update_block("ffn2_w3", "f2", 1, True),
            update_block("ffn2_w2", "f2", 2, False)]
    done += [update_block(n, "xa", i, False) for i, n in enumerate(("xa_wq", "xa_wk", "xa_wv", "xa_wo"))]
    r = _exchange_wait(last[2], done + [stk[k] for k in ("f1", "ma", "wo", "wb", "flat")], "scatter_wait_0c")
    for k, v in zip(("ma", "wo", "wb"), r):
        land(k, v, 0)
    G["w_in"], G["w_gate"] = _t(stk["ma"][:, :MW]), _t(stk["ma"][:, MW:])
    G["w_branch"] = stk["wb"].reshape(W["w_branch"].shape)
    done = [update("w_in"), update("w_gate"), update("w_branch"), update_block("w_o", "wo", 0, False)]
    (r,) = _exchange_wait(last[3], done, "scatter_wait_0d")
    land("flat", r, 0)

    tot = [stk["flat"][l].reshape(-1) for l in range(L)]
    off = 0
    for n in _REP_NAMES:
        shape = (KA, NS * CS) if n == "conv_a_w" else (KD, NS * CS) if n == "conv_d_w" else W[n].shape[1:]
        size = 1
        for d in shape:
            size *= d
        G[n] = jnp.stack([tot[l][off:off + size].reshape(shape) for l in range(L)])
        off += size
    for n in ("conv_a_w", "conv_d_w"):
        G[n] = lax.dynamic_slice_in_dim(G[n], me * CS, CS, axis=2)
    done = [update(n) for n in _REP_NAMES]

    (r,) = _exchange_wait(last[4], done, "scatter_wait_0e")
    land("f1", r, 0)
    update_block("ffn1_w1", "f1", 0, True)
    update_block("ffn1_w3", "f1", 1, True)
    update_block("ffn1_w2", "f1", 2, False)
    grad_x = dx.reshape(x.shape)
    return (loss, grad_x, *[G[n] for n in _W_NAMES], *[deltas[n] for n in _W_NAMES],
            *[new_m[n] for n in _W_NAMES], *[new_v[n] for n in _W_NAMES])


def kernel(x, mem, ffn1_pre_g, ffn1_post_g, ffn1_w1, ffn1_w3, ffn1_w2, mix_pre_g, mix_post_g, w_in, conv_a_w, conv_a_b, pool_w, pool_scale, sgu_ln_g, sgu_ln_b, sgu_ws, sgu_b, conv_d_w, conv_d_b, conv_d_ln_g, conv_d_ln_b, w_branch, w_gate, b_gate, w_o, xa_pre_g, xa_post_g, mem_g, xa_wq, xa_wk, xa_wv, xa_wo, ffn2_pre_g, ffn2_post_g, ffn2_w1, ffn2_w3, ffn2_w2, loss_target, m_ffn1_pre_g, m_ffn1_post_g, m_ffn1_w1, m_ffn1_w3, m_ffn1_w2, m_mix_pre_g, m_mix_post_g, m_w_in, m_conv_a_w, m_conv_a_b, m_pool_w, m_pool_scale, m_sgu_ln_g, m_sgu_ln_b, m_sgu_ws, m_sgu_b, m_conv_d_w, m_conv_d_b, m_conv_d_ln_g, m_conv_d_ln_b, m_w_branch, m_w_gate, m_b_gate, m_w_o, m_xa_pre_g, m_xa_post_g, m_mem_g, m_xa_wq, m_xa_wk, m_xa_wv, m_xa_wo, m_ffn2_pre_g, m_ffn2_post_g, m_ffn2_w1, m_ffn2_w3, m_ffn2_w2, v_ffn1_pre_g, v_ffn1_post_g, v_ffn1_w1, v_ffn1_w3, v_ffn1_w2, v_mix_pre_g, v_mix_post_g, v_w_in, v_conv_a_w, v_conv_a_b, v_pool_w, v_pool_scale, v_sgu_ln_g, v_sgu_ln_b, v_sgu_ws, v_sgu_b, v_conv_d_w, v_conv_d_b, v_conv_d_ln_g, v_conv_d_ln_b, v_w_branch, v_w_gate, v_b_gate, v_w_o, v_xa_pre_g, v_xa_post_g, v_mem_g, v_xa_wq, v_xa_wk, v_xa_wv, v_xa_wo, v_ffn2_pre_g, v_ffn2_post_g, v_ffn2_w1, v_ffn2_w3, v_ffn2_w2):
    args = dict(locals())
    W = {n: args[n] for n in _W_NAMES}
    M = {n: args["m_" + n] for n in _W_NAMES}
    V = {n: args["v_" + n] for n in _W_NAMES}
    return _step(x, mem, loss_target, W, M, V)
```
